```python
import math
import jax, jax.numpy as jnp
from jax import lax
import numpy as np

D_MODEL = 1024
BATCH = 2
SEQ = 8192
DEPTH = 1

NSA_HEADS = 8
NSA_KV_GROUPS = 2
NSA_HPG = NSA_HEADS // NSA_KV_GROUPS
HEAD_DIM = 64
NSA_WIDTH = NSA_HEADS * HEAD_DIM
KV_WIDTH = NSA_KV_GROUPS * HEAD_DIM
CMP_BLOCK = 32
CMP_STRIDE = 16
CMP_HIDDEN = 128
SEL_BLOCK = 64
N_SEL = 16
WINDOW = 512
Q_BLOCK = 128
NSA_N_BRANCH = 3
ROPE_THETA = 500000.0
ROT_DIM = HEAD_DIM // 4
S5_WIDTH = 512
S5_GROUP_DIM = 16
S5_GROUPS = S5_WIDTH // S5_GROUP_DIM
S5_STATE = 64
N_MEM = 256
MEM_HEADS = 4
MEM_HEAD_DIM = 128
MEM_WIDTH = MEM_HEADS * MEM_HEAD_DIM
N_BRANCH = 3
N_EXPERTS = 32
TOP_K = 4
D_FF = 1024
SWIGLU_LIMIT = 7.0
SWIGLU_ALPHA = 1.702
MOE_BLOCK = 256
LN_EPS = 1e-5
DEEPNORM_ALPHA = (2 * DEPTH) ** 0.25
DEEPNORM_BETA = (8 * DEPTH) ** -0.25

SPLIT_WIDTHS = (NSA_WIDTH, KV_WIDTH, KV_WIDTH, KV_WIDTH, KV_WIDTH, KV_WIDTH, KV_WIDTH,
                NSA_HEADS * NSA_N_BRANCH, S5_WIDTH, MEM_WIDTH, N_BRANCH * D_MODEL)
IN_WIDTH = sum(SPLIT_WIDTHS)

kernel_name = 'hybrid_nsa_s5_memory_moe_deepnorm'

F32 = jnp.float32


def _split_points():
    pts, acc = [], 0
    for w in SPLIT_WIDTHS[:-1]:
        acc += w
        pts.append(acc)
    return pts


def _layer_norm(x, g, b):
    xf = x.astype(F32)
    mu = jnp.mean(xf, axis=-1, keepdims=True)
    var = jnp.mean(jnp.square(xf - mu), axis=-1, keepdims=True)
    return ((xf - mu) * lax.rsqrt(var + LN_EPS) * g + b).astype(x.dtype)


def _masked_softmax(s, mask):
    s = jnp.where(mask, s.astype(F32), -jnp.inf)
    m = jnp.max(s, axis=-1, keepdims=True)
    m = jnp.where(jnp.isfinite(m), m, 0.0)
    e = jnp.exp(s - m)
    return e / jnp.maximum(jnp.sum(e, axis=-1, keepdims=True), jnp.finfo(F32).tiny)


def _rope_tables(positions):
    inv = ROPE_THETA ** (-jnp.arange(0, ROT_DIM, 2, dtype=F32) / ROT_DIM)
    ang = positions.astype(F32)[..., None] * inv
    return jnp.cos(ang)[:, :, None, :], jnp.sin(ang)[:, :, None, :]


def _apply_rope(t, cos, sin):
    half = ROT_DIM // 2
    t1 = t[..., :half].astype(F32)
    t2 = t[..., half:ROT_DIM].astype(F32)
    rot = jnp.concatenate([t1 * cos - t2 * sin, t2 * cos + t1 * sin], axis=-1).astype(t.dtype)
    return jnp.concatenate([rot, t[..., ROT_DIM:]], axis=-1)


def _compress(tok, pe, w1, w2):
    B_, L, G, HD = tok.shape
    ch = tok.reshape(B_, L // CMP_STRIDE, CMP_STRIDE, G, HD)
    blk = jnp.concatenate([ch[:, :-1], ch[:, 1:]], axis=2)
    blk = blk + pe[None, None, :, None, :]
    hid = jax.nn.gelu(jnp.einsum('bcsgd,sdf->bcgf', blk, w1))
    return jnp.einsum('bcgf,fd->bcgd', hid, w2)


def _nsa(q, kc, vc, ks, vs, kw, vw, gates, pe_k, pe_v, wk1, wk2, wv1, wv2):
    B_, L = q.shape[0], q.shape[1]
    G, HPG, HD = NSA_KV_GROUPS, NSA_HPG, HEAD_DIM
    dt = q.dtype
    ck = _compress(kc, pe_k, wk1, wk2)
    cv = _compress(vc, pe_v, wv1, wv2)
    n_cmp = ck.shape[1]
    n_sb = L // SEL_BLOCK
    n_sel = min(N_SEL, n_sb)
    ksb = ks.reshape(B_, n_sb, SEL_BLOCK, G, HD).transpose(0, 3, 1, 2, 4)
    vsb = vs.reshape(B_, n_sb, SEL_BLOCK, G, HD).transpose(0, 3, 1, 2, 4)
    kw_pad = jnp.pad(kw, ((0, 0), (WINDOW, 0), (0, 0), (0, 0)))
    vw_pad = jnp.pad(vw, ((0, 0), (WINDOW, 0), (0, 0), (0, 0)))
    b_ix = jnp.arange(B_)[:, None, None, None]
    g_ix = jnp.arange(G)[None, None, :, None]
    c_end = jnp.arange(n_cmp) * CMP_STRIDE + CMP_BLOCK - 1
    jb = jnp.arange(n_sb)
    scale = HEAD_DIM ** -0.5

    def block(qi):
        q0 = qi * Q_BLOCK
        t = q0 + jnp.arange(Q_BLOCK)
        qb = lax.dynamic_slice_in_dim(q, q0, Q_BLOCK, axis=1).reshape(B_, Q_BLOCK, G, HPG, HD)
        gb = lax.dynamic_slice_in_dim(gates, q0, Q_BLOCK, axis=1).reshape(B_, Q_BLOCK, G, HPG, NSA_N_BRANCH)
        s = jnp.einsum('bqghd,bcgd->bqghc', qb, ck) * scale
        p_cmp = _masked_softmax(s, (c_end[None, :] <= t[:, None])[None, :, None, None, :])
        o_cmp = jnp.einsum('bqghc,bcgd->bqghd', p_cmp.astype(dt), cv)
        imp = jnp.sum(p_cmp, axis=3)
        chunk = (jnp.pad(imp, ((0, 0), (0, 0), (0, 0), (0, 1)))
                 + jnp.pad(imp, ((0, 0), (0, 0), (0, 0), (1, 0))))
        score = jnp.sum(chunk.reshape(B_, Q_BLOCK, G, n_sb, SEL_BLOCK // CMP_STRIDE), axis=-1)
        tb = (t // SEL_BLOCK)[None, :, None, None]
        forced = (jb == 0) | (jb == tb) | (jb == tb - 1)
        score = jnp.where(forced, jnp.inf, jnp.where(jb > tb, -jnp.inf, score))
        _, idx = lax.top_k(score, n_sel)
        k_g = ksb[b_ix, g_ix, idx]
        v_g = vsb[b_ix, g_ix, idx]
        kpos = idx[..., None] * SEL_BLOCK + jnp.arange(SEL_BLOCK)
        m_sel = (kpos <= t[None, :, None, None, None]).reshape(B_, Q_BLOCK, G, 1, n_sel * SEL_BLOCK)
        s = jnp.einsum('bqghd,bqgnsd->bqghns', qb, k_g).reshape(B_, Q_BLOCK, G, HPG, n_sel * SEL_BLOCK) * scale
        p = _masked_softmax(s, m_sel)
        o_sel = jnp.einsum('bqghk,bqgkd->bqghd', p.astype(dt),
                           v_g.reshape(B_, Q_BLOCK, G, n_sel * SEL_BLOCK, HD))
        kwb = lax.dynamic_slice_in_dim(kw_pad, q0, WINDOW + Q_BLOCK, axis=1)
        vwb = lax.dynamic_slice_in_dim(vw_pad, q0, WINDOW + Q_BLOCK, axis=1)
        spos = q0 - WINDOW + jnp.arange(WINDOW + Q_BLOCK)
        diff = t[:, None] - spos[None, :]
        m_win = (spos[None, :] >= 0) & (diff >= 0) & (diff < WINDOW)
        s = jnp.einsum('bqghd,bsgd->bqghs', qb, kwb) * scale
        p = _masked_softmax(s, m_win[None, :, None, None, :])
        o_win = jnp.einsum('bqghs,bsgd->bqghd', p.astype(dt), vwb)
        o = o_cmp * gb[..., 0:1] + o_sel * gb[..., 1:2] + o_win * gb[..., 2:3]
        return o.reshape(B_, Q_BLOCK, NSA_HEADS, HD)

    out = lax.map(block, jnp.arange(L // Q_BLOCK))
    return out.transpose(1, 0, 2, 3, 4).reshape(B_, L, NSA_WIDTH)


def _s5(u, a_re, a_im, log_dt, b_re, b_im, c_re, c_im, d_skip):
    B_, L, _ = u.shape
    uf = u.astype(F32).reshape(B_, L, S5_GROUPS, S5_GROUP_DIM)
    lr, li = a_re.astype(F32), a_im.astype(F32)
    step = jnp.exp(log_dt.astype(F32))[:, None]
    mag = jnp.exp(lr * step)
    ab_re, ab_im = mag * jnp.cos(li * step), mag * jnp.sin(li * step)
    den = lr * lr + li * li
    nr = ab_re - 1.0
    coef_re = (nr * lr + ab_im * li) / den
    coef_im = (ab_im * lr - nr * li) / den
    br, bi = b_re.astype(F32), b_im.astype(F32)
    bb_re = coef_re[..., None] * br - coef_im[..., None] * bi
    bb_im = coef_re[..., None] * bi + coef_im[..., None] * br
    bu_re = jnp.einsum('blgp,gnp->blgn', uf, bb_re)
    bu_im = jnp.einsum('blgp,gnp->blgn', uf, bb_im)
    ar = jnp.broadcast_to(ab_re, bu_re.shape)
    ai = jnp.broadcast_to(ab_im, bu_re.shape)

    def combine(e1, e2):
        a1r, a1i, b1r, b1i = e1
        a2r, a2i, b2r, b2i = e2
        return (a2r * a1r - a2i * a1i, a2r * a1i + a2i * a1r,
                a2r * b1r - a2i * b1i + b2r, a2r * b1i + a2i * b1r + b2i)

    _, _, sr, si = lax.associative_scan(combine, (ar, ai, bu_re, bu_im), axis=1)
    y = (jnp.einsum('blgn,gpn->blgp', sr, c_re.astype(F32))
         - jnp.einsum('blgn,gpn->blgp', si, c_im.astype(F32)))
    y = y.reshape(B_, L, S5_WIDTH) + d_skip.astype(F32) * u.astype(F32)
    return y.astype(u.dtype)


def _memory_attention(qm, mem, w_mem_kv):
    B_, L, _ = qm.shape
    kv = jnp.einsum('bmd,de->bme', mem, w_mem_kv).reshape(B_, mem.shape[1], 2, MEM_HEADS, MEM_HEAD_DIM)
    k, v = kv[:, :, 0], kv[:, :, 1]
    q = qm.reshape(B_, L, MEM_HEADS, MEM_HEAD_DIM)
    s = jnp.einsum('blhd,bmhd->blhm', q, k).astype(F32) * (MEM_HEAD_DIM ** -0.5)
    p = jax.nn.softmax(s, axis=-1).astype(qm.dtype)
    return jnp.einsum('blhm,bmhd->blhd', p, v).reshape(B_, L, MEM_WIDTH)


def _hybrid_mixer(h, mem, cos, sin, w_in, pe_k_cmp, pe_v_cmp, w_kcmp1, w_kcmp2, w_vcmp1, w_vcmp2,
                  s5_a_re, s5_a_im, s5_log_dt, s5_b_re, s5_b_im, s5_c_re, s5_c_im, s5_d,
                  w_s5_glu, w_mem_kv, w_nsa_out, w_mem_out, w_o):
    B_, L, _ = h.shape
    z = jnp.einsum('bld,de->ble', h, w_in)
    q, kc, vc, ks, vs, kw, vw, g_nsa, u, qm, g_mrg = jnp.split(z, _split_points(), axis=-1)
    G = NSA_KV_GROUPS
    q = _apply_rope(q.reshape(B_, L, NSA_HEADS, HEAD_DIM), cos, sin)
    kc = _apply_rope(kc.reshape(B_, L, G, HEAD_DIM), cos, sin)
    ks = _apply_rope(ks.reshape(B_, L, G, HEAD_DIM), cos, sin)
    kw = _apply_rope(kw.reshape(B_, L, G, HEAD_DIM), cos, sin)
    vc = vc.reshape(B_, L, G, HEAD_DIM)
    vs = vs.reshape(B_, L, G, HEAD_DIM)
    vw = vw.reshape(B_, L, G, HEAD_DIM)
    g_nsa = jax.nn.sigmoid(g_nsa.astype(F32)).astype(h.dtype).reshape(B_, L, NSA_HEADS, NSA_N_BRANCH)
    y_nsa = _nsa(q, kc, vc, ks, vs, kw, vw, g_nsa, pe_k_cmp, pe_v_cmp,
                 w_kcmp1, w_kcmp2, w_vcmp1, w_vcmp2) @ w_nsa_out
    y = _s5(u, s5_a_re, s5_a_im, s5_log_dt, s5_b_re, s5_b_im, s5_c_re, s5_c_im, s5_d)
    ga, gb = jnp.split(jax.nn.gelu(y) @ w_s5_glu, 2, axis=-1)
    y_s5 = ga * jax.nn.sigmoid(gb)
    y_mem = _memory_attention(qm, mem, w_mem_kv) @ w_mem_out
    g = jax.nn.sigmoid(g_mrg.astype(F32)).astype(h.dtype).reshape(B_, L, N_BRANCH, D_MODEL)
    merged = g[:, :, 0] * y_nsa + g[:, :, 1] * y_s5 + g[:, :, 2] * y_mem
    return merged @ w_o


def _moe(h, w_router, b_router, w_gate_up, b_gate_up, w_down, b_down):
    B_, L, D = h.shape
    T = B_ * L
    xt = h.reshape(T, D)
    logits = (xt @ w_router + b_router).astype(F32)
    top_val, top_idx = lax.top_k(logits, TOP_K)
    gate = jax.nn.softmax(top_val, axis=-1)
    e_flat = top_idx.reshape(-1)
    tok_flat = jnp.arange(T * TOP_K, dtype=jnp.int32) // TOP_K
    w_flat = gate.reshape(-1)
    order = jnp.argsort(e_flat)
    e_s, tok_s, w_s = e_flat[order], tok_flat[order], w_flat[order]
    counts = jnp.bincount(e_flat, length=N_EXPERTS)
    padded = (counts + MOE_BLOCK - 1) // MOE_BLOCK * MOE_BLOCK
    start = jnp.cumsum(counts) - counts
    pend = jnp.cumsum(padded)
    pstart = pend - padded
    dest = pstart[e_s] + (jnp.arange(T * TOP_K, dtype=jnp.int32) - start[e_s])
    cap = (T * TOP_K + MOE_BLOCK - 1) // MOE_BLOCK * MOE_BLOCK + N_EXPERTS * MOE_BLOCK
    n_blk = cap // MOE_BLOCK
    slot_tok = jnp.full((cap,), T, jnp.int32).at[dest].set(tok_s)
    slot_w = jnp.zeros((cap,), F32).at[dest].set(w_s)
    blk_expert = jnp.minimum(jnp.searchsorted(pend, jnp.arange(n_blk) * MOE_BLOCK, side='right'),
                             N_EXPERTS - 1)
    x_pad = jnp.concatenate([xt, jnp.zeros((1, D), xt.dtype)], axis=0)
    xs = x_pad[slot_tok].reshape(n_blk, MOE_BLOCK, D)

    def expert_block(args):
        xb, e = args
        gu = xb @ w_gate_up[e] + b_gate_up[e]
        g, lin = jnp.split(gu, 2, axis=-1)
        g = jnp.minimum(g, SWIGLU_LIMIT)
        lin = jnp.clip(lin, -SWIGLU_LIMIT, SWIGLU_LIMIT)
        act = g * jax.nn.sigmoid(SWIGLU_ALPHA * g) * (lin + 1.0)
        return act @ w_down[e] + b_down[e]

    ys = lax.map(expert_block, (xs, blk_expert)).reshape(cap, D)
    ys = ys * slot_w[:, None].astype(ys.dtype)
    out = jnp.zeros((T + 1, D), ys.dtype).at[slot_tok].add(ys)[:T]
    return out.reshape(B_, L, D)


def setup_inputs(seed: int = 0) -> dict:
    key = jax.random.key(seed)
    ks = jax.random.split(key, 40)
    beta = DEEPNORM_BETA

    def nrm(k, shape, scale):
        return jax.random.normal(k, shape, F32) * scale

    x = nrm(ks[0], (BATCH, SEQ, D_MODEL), 1.0)
    mem = nrm(ks[1], (BATCH, N_MEM, D_MODEL), 1.0)
    positions = (jnp.arange(SEQ, dtype=jnp.int32)[None, :]
                 + jax.random.randint(ks[2], (BATCH, 1), 0, 1024, dtype=jnp.int32))
    ones = lambda n: jnp.ones((n,), F32)
    betas = lambda n: jnp.full((n,), beta, F32)
    col_scale = jnp.concatenate([
        ones(NSA_WIDTH),
        ones(KV_WIDTH), betas(KV_WIDTH),
        ones(KV_WIDTH), betas(KV_WIDTH),
        ones(KV_WIDTH), betas(KV_WIDTH),
        ones(NSA_HEADS * NSA_N_BRANCH),
        betas(S5_WIDTH),
        ones(MEM_WIDTH),
        ones(N_BRANCH * D_MODEL)])
    w_in = nrm(ks[3], (DEPTH, D_MODEL, IN_WIDTH), D_MODEL ** -0.5) * col_scale
    s5_a_re = -0.5 * jnp.exp(nrm(ks[14], (DEPTH, S5_GROUPS, S5_STATE), 0.05))
    s5_a_im = jnp.broadcast_to(jnp.pi * jnp.arange(S5_STATE, dtype=F32), (DEPTH, S5_GROUPS, S5_STATE))
    mem_kv_scale = jnp.concatenate([ones(MEM_WIDTH), betas(MEM_WIDTH)])
    return {
        'x': x,
        'mem': mem,
        'positions': positions,
        'ln_emb_g': 1.0 + nrm(ks[4], (D_MODEL,), 0.02),
        'ln_emb_b': nrm(ks[5], (D_MODEL,), 0.02),
        'w_in': w_in,
        'pe_k_cmp': nrm(ks[6], (DEPTH, CMP_BLOCK, HEAD_DIM), 0.1),
        'pe_v_cmp': nrm(ks[7], (DEPTH, CMP_BLOCK, HEAD_DIM), 0.1),
        'w_kcmp1': nrm(ks[8], (DEPTH, CMP_BLOCK, HEAD_DIM, CMP_HIDDEN), (CMP_BLOCK * HEAD_DIM) ** -0.5),
        'w_kcmp2': nrm(ks[9], (DEPTH, CMP_HIDDEN, HEAD_DIM), CMP_HIDDEN ** -0.5),
        'w_vcmp1': nrm(ks[10], (DEPTH, CMP_BLOCK, HEAD_DIM, CMP_HIDDEN), (CMP_BLOCK * HEAD_DIM) ** -0.5),
        'w_vcmp2': nrm(ks[11], (DEPTH, CMP_HIDDEN, HEAD_DIM), CMP_HIDDEN ** -0.5),
        's5_a_re': s5_a_re,
        's5_a_im': s5_a_im,
        's5_log_dt': jax.random.uniform(ks[12], (DEPTH, S5_GROUPS), F32, math.log(1e-3), math.log(1e-1)),
        's5_b_re': nrm(ks[13], (DEPTH, S5_GROUPS, S5_STATE, S5_GROUP_DIM), (2 * S5_GROUP_DIM) ** -0.5),
        's5_b_im': nrm(ks[15], (DEPTH, S5_GROUPS, S5_STATE, S5_GROUP_DIM), (2 * S5_GROUP_DIM) ** -0.5),
        's5_c_re': nrm(ks[16], (DEPTH, S5_GROUPS, S5_GROUP_DIM, S5_STATE), (2 * S5_STATE) ** -0.5),
        's5_c_im': nrm(ks[17], (DEPTH, S5_GROUPS, S5_GROUP_DIM, S5_STATE), (2 * S5_STATE) ** -0.5),
        's5_d': nrm(ks[18], (DEPTH, S5_WIDTH), 1.0),
        'w_s5_glu': nrm(ks[19], (DEPTH, S5_WIDTH, 2 * D_MODEL), S5_WIDTH ** -0.5 * beta),
        'w_mem_kv': nrm(ks[20], (DEPTH, D_MODEL, 2 * MEM_WIDTH), D_MODEL ** -0.5) * mem_kv_scale,
        'w_nsa_out': nrm(ks[21], (DEPTH, NSA_WIDTH, D_MODEL), NSA_WIDTH ** -0.5 * beta),
        'w_mem_out': nrm(ks[22], (DEPTH, MEM_WIDTH, D_MODEL), MEM_WIDTH ** -0.5 * beta),
        'w_o': nrm(ks[23], (DEPTH, D_MODEL, D_MODEL), D_MODEL ** -0.5 * beta),
        'ln1_g': 1.0 + nrm(ks[24], (DEPTH, D_MODEL), 0.02),
        'ln1_b': nrm(ks[25], (DEPTH, D_MODEL), 0.02),
        'w_router': nrm(ks[26], (DEPTH, D_MODEL, N_EXPERTS), D_MODEL ** -0.5),
        'b_router': nrm(ks[27], (DEPTH, N_EXPERTS), 0.01),
        'w_gate_up': nrm(ks[28], (DEPTH, N_EXPERTS, D_MODEL, 2 * D_FF), D_MODEL ** -0.5 * beta),
        'b_gate_up': nrm(ks[29], (DEPTH, N_EXPERTS, 2 * D_FF), 0.02),
        'w_down': nrm(ks[30], (DEPTH, N_EXPERTS, D_FF, D_MODEL), D_FF ** -0.5 * beta),
        'b_down': nrm(ks[31], (DEPTH, N_EXPERTS, D_MODEL), 0.02),
        'ln2_g': 1.0 + nrm(ks[32], (DEPTH, D_MODEL), 0.02),
        'ln2_b': nrm(ks[33], (DEPTH, D_MODEL), 0.02),
    }


def reference(x, mem, positions, ln_emb_g, ln_emb_b, w_in, pe_k_cmp, pe_v_cmp, w_kcmp1, w_kcmp2,
              w_vcmp1, w_vcmp2, s5_a_re, s5_a_im, s5_log_dt, s5_b_re, s5_b_im, s5_c_re, s5_c_im,
              s5_d, w_s5_glu, w_mem_kv, w_nsa_out, w_mem_out, w_o, ln1_g, ln1_b, w_router, b_router,
              w_gate_up, b_gate_up, w_down, b_down, ln2_g, ln2_b):
    h = _layer_norm(x, ln_emb_g, ln_emb_b)
    cos, sin = _rope_tables(positions)
    for l in range(DEPTH):
        mix = _hybrid_mixer(h, mem, cos, sin, w_in[l], pe_k_cmp[l], pe_v_cmp[l], w_kcmp1[l], w_kcmp2[l],
                            w_vcmp1[l], w_vcmp2[l], s5_a_re[l], s5_a_im[l], s5_log_dt[l], s5_b_re[l],
                            s5_b_im[l], s5_c_re[l], s5_c_im[l], s5_d[l], w_s5_glu[l], w_mem_kv[l],
                            w_nsa_out[l], w_mem_out[l], w_o[l])
        h = _layer_norm(DEEPNORM_ALPHA * h + mix, ln1_g[l], ln1_b[l])
        ffn = _moe(h, w_router[l], b_router[l], w_gate_up[l], b_gate_up[l], w_down[l], b_down[l])
        h = _layer_norm(DEEPNORM_ALPHA * h + ffn, ln2_g[l], ln2_b[l])
    return h
```

```python
import functools
import math

import jax
import jax.numpy as jnp
from jax import lax
from jax.experimental import pallas as pl
from jax.experimental.pallas import tpu as pltpu

F32 = jnp.float32
BF16 = jnp.bfloat16
I32 = jnp.int32

D_MODEL = 1024
NSA_HEADS = 8
NSA_GROUPS = 2
NSA_HPG = NSA_HEADS // NSA_GROUPS
HEAD_DIM = 64
NSA_WIDTH = NSA_HEADS * HEAD_DIM
KV_WIDTH = NSA_GROUPS * HEAD_DIM
CMP_BLOCK = 32
CMP_STRIDE = 16
CMP_HIDDEN = 128
SEL_BLOCK = 64
N_SEL = 16
WINDOW = 512
Q_BLOCK = 128
ROPE_THETA = 500000.0
ROT_DIM = HEAD_DIM // 4
S5_WIDTH = 512
S5_GROUP_DIM = 16
S5_GROUPS = S5_WIDTH // S5_GROUP_DIM
S5_STATE = 64
MEM_HEADS = 4
MEM_HEAD_DIM = 128
MEM_WIDTH = MEM_HEADS * MEM_HEAD_DIM
N_BRANCH = 3
N_EXPERTS = 32
TOP_K = 4
D_FF = 1024
SWIGLU_LIMIT = 7.0
SWIGLU_ALPHA = 1.702
LN_EPS = 1e-5
DEPTH = 1
DEEPNORM_ALPHA = (2 * DEPTH) ** 0.25

LANES = 128
SUBLANES = 8
VMEM_LIMIT_BYTES = 56 * 1024 * 1024

TOKEN_TILE = 256
SEL_KV_TILE = 512
S5_CHUNK = 512
S5_PITCH = S5_CHUNK + 8
MOE_ROWS = 256
NEG_BIG = -1e30


def _cparams(*sem):
    return pltpu.CompilerParams(dimension_semantics=sem, vmem_limit_bytes=VMEM_LIMIT_BYTES)


def _dot(a, b):
    return jnp.dot(a, b, preferred_element_type=F32)


def _dot_nt(a, b):
    return lax.dot_general(a, b, (((1,), (1,)), ((), ())), preferred_element_type=F32)


def _layer_norm(x, g, b):
    mu = jnp.mean(x, axis=-1, keepdims=True)
    xc = x - mu
    var = jnp.mean(xc * xc, axis=-1, keepdims=True)
    return xc * lax.rsqrt(var + LN_EPS) * g + b


def _gelu_tanh(x):
    cdf = 0.5 * (1.0 + jnp.tanh(math.sqrt(2.0 / math.pi) * (x + 0.044715 * (x * x * x))))
    return x * cdf


def _masked_softmax(s, mask):
    s = jnp.where(mask, s, -jnp.inf)
    m = jnp.max(s, axis=-1, keepdims=True)
    m = jnp.where(m > -jnp.inf, m, 0.0)
    e = jnp.exp(s - m)
    return e / jnp.maximum(jnp.sum(e, axis=-1, keepdims=True), jnp.finfo(F32).tiny)


def _split3(x):
    hi = x.astype(BF16)
    r1 = x - hi.astype(F32)
    mid = r1.astype(BF16)
    lo = (r1 - mid.astype(F32)).astype(BF16)
    return hi, mid, lo


def _full_spec(shape):
    nd = len(shape)
    return pl.BlockSpec(shape, lambda *_: (0,) * nd)


def _inproj_kernel(x_ref, g_ref, b_ref, cos_ref, sa_ref, sb_ref,
                   wq_ref, wk_ref, wv_ref, wg_ref, wu_ref, wqm_ref, wm_ref,
                   q_ref, kc_ref, vc_ref, ks_ref, vs_ref, kw_ref, vw_ref,
                   gate_ref, u_ref, qm_ref, gm_ref):
    h = _layer_norm(x_ref[0], g_ref[...], b_ref[...])
    hb = h.astype(BF16)
    cos_t, sin_a, sin_b = cos_ref[0], sa_ref[0], sb_ref[0]

    def rope(t):
        return (t * cos_t + pltpu.roll(t, ROT_DIM // 2, 1) * sin_a
                + pltpu.roll(t, LANES - ROT_DIM // 2, 1) * sin_b)

    q = _dot(hb, wq_ref[...])
    for c in range(NSA_WIDTH // LANES):
        qc = rope(q[:, c * LANES:(c + 1) * LANES])
        for hh in range(2):
            q_ref[0, 2 * c + hh] = qc[:, hh * HEAD_DIM:(hh + 1) * HEAD_DIM].astype(BF16)
    k3 = _dot(hb, wk_ref[...])
    kc = rope(k3[:, 0:LANES])
    ks = rope(k3[:, LANES:2 * LANES])
    kw = rope(k3[:, 2 * LANES:3 * LANES])
    v3 = _dot(hb, wv_ref[...])
    kc_ref[0] = kc.astype(BF16)
    vc_ref[0] = v3[:, 0:LANES].astype(BF16)
    for g in range(NSA_GROUPS):
        sl = slice(g * HEAD_DIM, (g + 1) * HEAD_DIM)
        ks_ref[0, g] = ks[:, sl].astype(BF16)
        kw_ref[0, g] = kw[:, sl].astype(BF16)
        vs_ref[0, g] = v3[:, LANES:2 * LANES][:, sl].astype(BF16)
        vw_ref[0, g] = v3[:, 2 * LANES:3 * LANES][:, sl].astype(BF16)
    gate_ref[0] = jax.nn.sigmoid(_dot(hb, wg_ref[...]))
    u_ref[0] = _dot(hb, wu_ref[...])
    qm_ref[0] = _dot(hb, wqm_ref[...]).astype(BF16)
    gm_ref[0] = jax.nn.sigmoid(_dot(hb, wm_ref[...]))


def _inproj(x, ln_g, ln_b, cos_t, sin_a, sin_b, wq, wk, wv, wg, wu, wqm, wm):
    B, L, D = x.shape
    tm = TOKEN_TILE
    grid = (B, L // tm)
    tok = lambda w: pl.BlockSpec((1, tm, w), lambda b, i: (b, i, 0))
    head = lambda n: pl.BlockSpec((1, n, tm, HEAD_DIM), lambda b, i: (b, 0, i, 0))
    in_specs = [tok(D), _full_spec((1, D)), _full_spec((1, D)), tok(LANES), tok(LANES), tok(LANES)]
    in_specs += [_full_spec(w.shape) for w in (wq, wk, wv, wg, wu, wqm, wm)]
    sd = jax.ShapeDtypeStruct
    out_shape = [
        sd((B, NSA_HEADS, L, HEAD_DIM), BF16),
        sd((B, L, KV_WIDTH), BF16), sd((B, L, KV_WIDTH), BF16),
        sd((B, NSA_GROUPS, L, HEAD_DIM), BF16), sd((B, NSA_GROUPS, L, HEAD_DIM), BF16),
        sd((B, NSA_GROUPS, L, HEAD_DIM), BF16), sd((B, NSA_GROUPS, L, HEAD_DIM), BF16),
        sd((B, L, NSA_GROUPS * LANES), F32),
        sd((B, L, S5_WIDTH), F32),
        sd((B, L, MEM_WIDTH), BF16),
        sd((B, L, N_BRANCH * D), F32),
    ]
    out_specs = [head(NSA_HEADS), tok(KV_WIDTH), tok(KV_WIDTH), head(NSA_GROUPS), head(NSA_GROUPS),
                 head(NSA_GROUPS), head(NSA_GROUPS), tok(NSA_GROUPS * LANES), tok(S5_WIDTH),
                 tok(MEM_WIDTH), tok(N_BRANCH * D)]
    return pl.pallas_call(
        _inproj_kernel, grid=grid, in_specs=in_specs, out_specs=out_specs, out_shape=out_shape,
        compiler_params=_cparams("parallel", "parallel"), name="inproj",
    )(x, ln_g, ln_b, cos_t, sin_a, sin_b, wq, wk, wv, wg, wu, wqm, wm)


def _compress_kernel(kc_ref, vc_ref, pek_ref, pev_ref, wk1f_ref, wv1f_ref,
                     wk1a_ref, wk1b_ref, wv1a_ref, wv1b_ref, wk2_ref, wv2_ref, ck_ref, cv_ref):
    n_chunk = kc_ref.shape[1]
    row = lax.broadcasted_iota(I32, (n_chunk, 1), 0)

    def one(x_ref, pe_ref, w1f_ref, w1a_ref, w1b_ref, w2_ref, o_ref):
        x = x_ref[0]
        first = _dot(x, w1a_ref[...])
        second = _dot(x, w1b_ref[...])
        second = pltpu.roll(second, n_chunk - 1, 0)
        pe_term = _dot(pe_ref[...], w1f_ref[...])[0:1]
        pe_term = jnp.concatenate([pe_term] * NSA_GROUPS, axis=1)
        hid = _gelu_tanh(first + second + pe_term).astype(BF16)
        for g in range(NSA_GROUPS):
            o = _dot(hid[:, g * CMP_HIDDEN:(g + 1) * CMP_HIDDEN], w2_ref[...])
            o_ref[0, g] = jnp.where(row < n_chunk - 1, o, 0.0).astype(BF16)

    one(kc_ref, pek_ref, wk1f_ref, wk1a_ref, wk1b_ref, wk2_ref, ck_ref)
    one(vc_ref, pev_ref, wv1f_ref, wv1a_ref, wv1b_ref, wv2_ref, cv_ref)


def _compress(kc_r, vc_r, pek, pev, wk1f, wv1f, wk1a, wk1b, wv1a, wv1b, wk2, wv2):
    B, n_chunk, width = kc_r.shape
    blk = pl.BlockSpec((1, n_chunk, width), lambda b: (b, 0, 0))
    out = pl.BlockSpec((1, NSA_GROUPS, n_chunk, HEAD_DIM), lambda b: (b, 0, 0, 0))
    ws = [pek, pev, wk1f, wv1f, wk1a, wk1b, wv1a, wv1b, wk2, wv2]
    sd = jax.ShapeDtypeStruct((B, NSA_GROUPS, n_chunk, HEAD_DIM), BF16)
    return pl.pallas_call(
        _compress_kernel, grid=(B,), in_specs=[blk, blk] + [_full_spec(w.shape) for w in ws],
        out_specs=[out, out], out_shape=[sd, sd], compiler_params=_cparams("parallel"), name="compress",
    )(kc_r, vc_r, *ws)


def _nsa_kernel(q_ref, ck_ref, cv_ref, ks_ref, vs_ref, kw_ref, vw_ref, gate_ref, wsc_ref, o_ref):
    seq_len = ks_ref.shape[2]
    n_cmp = ck_ref.shape[2]
    n_sb = seq_len // SEL_BLOCK
    n_sel = min(N_SEL, n_sb)
    rows = NSA_HPG * Q_BLOCK
    q0 = pl.program_id(2) * Q_BLOCK
    q = q_ref[0].reshape(rows, HEAD_DIM) * (HEAD_DIM ** -0.5)
    t1 = q0 + lax.broadcasted_iota(I32, (Q_BLOCK, 1), 0)
    t4 = jnp.concatenate([t1] * NSA_HPG, axis=0)

    s = _dot_nt(q, ck_ref[0, 0])
    c_end = lax.broadcasted_iota(I32, (1, n_cmp), 1) * CMP_STRIDE + (CMP_BLOCK - 1)
    p_cmp = _masked_softmax(s, c_end <= t4)
    o_cmp = _dot(p_cmp.astype(BF16), cv_ref[0, 0])

    imp = p_cmp[0:Q_BLOCK]
    for hh in range(1, NSA_HPG):
        imp = imp + p_cmp[hh * Q_BLOCK:(hh + 1) * Q_BLOCK]
    w_sc = wsc_ref[...]
    score = sum(_dot(part, w_sc) for part in _split3(imp))
    jb = lax.broadcasted_iota(I32, (Q_BLOCK, n_sb), 1)
    tb = t1 // SEL_BLOCK
    forced = (jb == 0) | (jb == tb) | (jb == tb - 1)
    score = jnp.where(forced, jnp.inf, jnp.where(jb > tb, -jnp.inf, score))
    jbf = jb.astype(F32)
    sel = jnp.zeros((Q_BLOCK, n_sb), F32)
    for _ in range(n_sel):
        m = jnp.max(score, axis=-1, keepdims=True)
        idx = jnp.min(jnp.where(score == m, jbf, float(n_sb)), axis=-1, keepdims=True)
        pick = jbf == idx
        sel = jnp.where(pick, 1.0, sel)
        score = jnp.where(pick, -jnp.inf, score)
    sel_b = sel.astype(BF16)

    tk = SEL_KV_TILE
    blk_row = lax.broadcasted_iota(I32, (n_sb, 1), 0)
    lane_k = lax.broadcasted_iota(I32, (1, tk), 1)

    def sel_step(j, carry):
        m_run, l_run, acc = carry
        k0 = pl.multiple_of(j * tk, tk)
        k = ks_ref[0, 0, pl.ds(k0, tk), :]
        v = vs_ref[0, 0, pl.ds(k0, tk), :]
        sc = _dot_nt(q, k)
        kpos = k0 + lane_k
        expand = jnp.where(blk_row == kpos // SEL_BLOCK, 1.0, 0.0).astype(BF16)
        chosen = _dot(sel_b, expand)
        bias = jnp.where((chosen > 0.5) & (kpos <= t1), 0.0, NEG_BIG)
        sc = sc + jnp.concatenate([bias] * NSA_HPG, axis=0)
        m_new = jnp.maximum(m_run, jnp.max(sc, axis=-1, keepdims=True))
        alpha = jnp.exp(m_run - m_new)
        p = jnp.exp(sc - m_new)
        l_new = alpha * l_run + jnp.sum(p, axis=-1, keepdims=True)
        acc_new = alpha * acc + _dot(p.astype(BF16), v)
        return m_new, l_new, acc_new

    init = (jnp.full((rows, 1), NEG_BIG, F32), jnp.zeros((rows, 1), F32), jnp.zeros((rows, HEAD_DIM), F32))
    n_tiles = (q0 + Q_BLOCK - 1) // tk + 1
    _, l_fin, acc = lax.fori_loop(0, n_tiles, sel_step, init)
    o_sel = acc / l_fin

    span = WINDOW + Q_BLOCK
    w0 = pl.multiple_of(jnp.maximum(q0 - WINDOW, 0), Q_BLOCK)
    kwin = kw_ref[0, 0, pl.ds(w0, span), :]
    vwin = vw_ref[0, 0, pl.ds(w0, span), :]
    s = _dot_nt(q, kwin)
    diff = t4 - (w0 + lax.broadcasted_iota(I32, (1, span), 1))
    p = _masked_softmax(s, (diff >= 0) & (diff < WINDOW))
    o_win = _dot(p.astype(BF16), vwin)

    gt = gate_ref[0]
    outs = []
    for hh in range(NSA_HPG):
        sl = slice(hh * Q_BLOCK, (hh + 1) * Q_BLOCK)
        c = hh * N_BRANCH
        outs.append(o_cmp[sl] * gt[:, c:c + 1] + o_sel[sl] * gt[:, c + 1:c + 2] + o_win[sl] * gt[:, c + 2:c + 3])
    o_ref[0] = jnp.concatenate(outs, axis=1).astype(BF16)


def _nsa(q_hm, ck, cv, ks, vs, kw, vw, gates, w_score):
    B, _, L, _ = q_hm.shape
    n_cmp = ck.shape[2]
    grid = (B, NSA_GROUPS, L // Q_BLOCK)
    qspec = pl.BlockSpec((1, NSA_HPG, Q_BLOCK, HEAD_DIM), lambda b, g, i: (b, g, i, 0))
    cspec = pl.BlockSpec((1, 1, n_cmp, HEAD_DIM), lambda b, g, i: (b, g, 0, 0))
    kvspec = pl.BlockSpec((1, 1, L, HEAD_DIM), lambda b, g, i: (b, g, 0, 0))
    gspec = pl.BlockSpec((1, Q_BLOCK, LANES), lambda b, g, i: (b, i, g))
    ospec = pl.BlockSpec((1, Q_BLOCK, NSA_HPG * HEAD_DIM), lambda b, g, i: (b, i, g))
    return pl.pallas_call(
        _nsa_kernel, grid=grid,
        in_specs=[qspec, cspec, cspec, kvspec, kvspec, kvspec, kvspec, gspec, _full_spec(w_score.shape)],
        out_specs=ospec, out_shape=jax.ShapeDtypeStruct((B, L, NSA_WIDTH), BF16),
        compiler_params=_cparams("parallel", "parallel", "arbitrary"), name="nsa",
    )(q_hm, ck, cv, ks, vs, kw, vw, gates, w_score)


def _s5_kernel(u_ref, wb_ref, wc_ref, are_ref, aim_ref, d_ref, y_ref, sre_ref, sim_ref, carry_ref):
    n_b, chunk, _ = u_ref.shape
    n_tile = wb_ref.shape[0]
    in_per = n_tile // (S5_WIDTH // LANES)
    pitch = S5_PITCH

    @pl.when(pl.program_id(0) == 0)
    def _():
        carry_ref[...] = jnp.zeros_like(carry_ref)

    for b in range(n_b):
        for c in range(n_tile):
            i = c // in_per
            ub = u_ref[b, :, i * LANES:(i + 1) * LANES].astype(BF16)
            r = _dot(ub, wb_ref[c])
            sre_ref[b, c * pitch:c * pitch + chunk, :] = r[:, :LANES]
            sim_ref[b, c * pitch:c * pitch + chunk, :] = r[:, LANES:]

    a_re, a_im = are_ref[...], aim_ref[...]

    def step(t, carry):
        out = []
        for b in range(n_b):
            s_re, s_im = carry[2 * b], carry[2 * b + 1]
            rows = pl.ds(t, n_tile, stride=pitch)
            n_re = a_re * s_re - a_im * s_im + sre_ref[b, rows, :]
            n_im = a_re * s_im + a_im * s_re + sim_ref[b, rows, :]
            sre_ref[b, rows, :] = n_re
            sim_ref[b, rows, :] = n_im
            out += [n_re, n_im]
        return tuple(out)

    init = tuple(carry_ref[i] for i in range(2 * n_b))
    fin = lax.fori_loop(0, chunk, step, init, unroll=8)
    for i in range(2 * n_b):
        carry_ref[i] = fin[i]

    for b in range(n_b):
        for o in range(S5_WIDTH // LANES):
            acc = jnp.zeros((chunk, LANES), F32)
            for c in range(o * in_per, (o + 1) * in_per):
                rows = slice(c * pitch, c * pitch + chunk)
                acc = acc + _dot(sre_ref[b, rows, :].astype(BF16), wc_ref[c, :LANES])
                acc = acc + _dot(sim_ref[b, rows, :].astype(BF16), wc_ref[c, LANES:])
            lanes = slice(o * LANES, (o + 1) * LANES)
            y = acc + d_ref[:, lanes] * u_ref[b, :, lanes]
            y_ref[b, :, lanes] = _gelu_tanh(y).astype(BF16)


def _s5(u, wb, wc, a_re, a_im, d_skip):
    B, L, W = u.shape
    chunk = S5_CHUNK
    n_tile = wb.shape[0]
    blk = pl.BlockSpec((B, chunk, W), lambda i: (0, i, 0))
    slab = pltpu.VMEM((B, n_tile * S5_PITCH, LANES), F32)
    return pl.pallas_call(
        _s5_kernel, grid=(L // chunk,),
        in_specs=[blk] + [_full_spec(w.shape) for w in (wb, wc, a_re, a_im, d_skip)],
        out_specs=blk, out_shape=jax.ShapeDtypeStruct((B, L, W), BF16),
        scratch_shapes=[slab, slab, pltpu.VMEM((2 * B, n_tile, LANES), F32)],
        compiler_params=_cparams("arbitrary"), name="s5",
    )(u, wb, wc, a_re, a_im, d_skip)


def _memkv_kernel(mem_ref, w_ref, k_ref, v_ref):
    kv = _dot(mem_ref[0].astype(BF16), w_ref[...])
    k_ref[0] = kv[:, :MEM_WIDTH].astype(BF16)
    v_ref[0] = kv[:, MEM_WIDTH:].astype(BF16)


def _memkv(mem, w_kv):
    B, M, D = mem.shape
    out = pl.BlockSpec((1, M, MEM_WIDTH), lambda b: (b, 0, 0))
    sd = jax.ShapeDtypeStruct((B, M, MEM_WIDTH), BF16)
    return pl.pallas_call(
        _memkv_kernel, grid=(B,),
        in_specs=[pl.BlockSpec((1, M, D), lambda b: (b, 0, 0)), _full_spec(w_kv.shape)],
        out_specs=[out, out], out_shape=[sd, sd], compiler_params=_cparams("parallel"), name="memkv",
    )(mem, w_kv)


def _memattn_kernel(q_ref, k_ref, v_ref, o_ref):
    outs = []
    for h in range(MEM_HEADS):
        sl = slice(h * MEM_HEAD_DIM, (h + 1) * MEM_HEAD_DIM)
        s = _dot_nt(q_ref[0, :, sl], k_ref[0, :, sl]) * (MEM_HEAD_DIM ** -0.5)
        m = jnp.max(s, axis=-1, keepdims=True)
        e = jnp.exp(s - m)
        p = e / jnp.sum(e, axis=-1, keepdims=True)
        outs.append(_dot(p.astype(BF16), v_ref[0, :, sl]))
    o_ref[0] = jnp.concatenate(outs, axis=1).astype(BF16)


def _memattn(qm, k, v):
    B, L, W = qm.shape
    M = k.shape[1]
    tm = TOKEN_TILE
    tok = pl.BlockSpec((1, tm, W), lambda b, i: (b, i, 0))
    kv = pl.BlockSpec((1, M, W), lambda b, i: (b, 0, 0))
    return pl.pallas_call(
        _memattn_kernel, grid=(B, L // tm), in_specs=[tok, kv, kv], out_specs=tok,
        out_shape=jax.ShapeDtypeStruct((B, L, W), BF16),
        compiler_params=_cparams("parallel", "parallel"), name="memattn",
    )(qm, k, v)


def _merge_kernel(x_ref, lng_ref, lnb_ref, on_ref, gy_ref, om_ref, gm_ref,
                  wn_ref, wglu_ref, wmo_ref, wo_ref, l1g_ref, l1b_ref,
                  wrh_ref, wrl_ref, br_ref, tri_ref,
                  h1_ref, e4_ref, w4_ref, r4_ref, cnt_ref, run_ref):
    D = x_ref.shape[1]
    tm = x_ref.shape[0]

    @pl.when(pl.program_id(0) == 0)
    def _():
        run_ref[...] = jnp.zeros_like(run_ref)

    h = _layer_norm(x_ref[...], lng_ref[...], lnb_ref[...])
    y_nsa = _dot(on_ref[...], wn_ref[...])
    glu = _dot(gy_ref[...], wglu_ref[...])
    y_s5 = glu[:, :D] * jax.nn.sigmoid(glu[:, D:])
    y_mem = _dot(om_ref[...], wmo_ref[...])
    merged = gm_ref[:, 0:D] * y_nsa + gm_ref[:, D:2 * D] * y_s5 + gm_ref[:, 2 * D:3 * D] * y_mem
    mix = _dot(merged.astype(BF16), wo_ref[...])
    h1 = _layer_norm(DEEPNORM_ALPHA * h + mix, l1g_ref[...], l1b_ref[...])
    h1_ref[...] = h1

    hh = h1.astype(BF16)
    hl = (h1 - hh.astype(F32)).astype(BF16)
    logits = _dot(hh, wrh_ref[...]) + _dot(hh, wrl_ref[...]) + _dot(hl, wrh_ref[...]) + br_ref[...]
    lane = lax.broadcasted_iota(I32, (tm, LANES), 1)
    lane_f = lane.astype(F32)
    work = logits
    multi = jnp.zeros((tm, LANES), F32)
    vals, picks = [], []
    for _ in range(TOP_K):
        m = jnp.max(work, axis=-1, keepdims=True)
        idx = jnp.min(jnp.where(work == m, lane_f, float(LANES)), axis=-1, keepdims=True)
        pick = lane_f == idx
        vals.append(m)
        picks.append((pick, idx))
        multi = jnp.where(pick, 1.0, multi)
        work = jnp.where(pick, -jnp.inf, work)
    es = [jnp.exp(v - vals[0]) for v in vals]
    den = es[0] + es[1] + es[2] + es[3]
    rank = run_ref[0:1, :] + _dot(tri_ref[...], multi.astype(BF16))
    run_ref[...] = run_ref[...] + jnp.sum(multi, axis=0, keepdims=True)
    cnt_ref[...] = run_ref[...]
    e4 = jnp.zeros((tm, LANES), F32)
    w4 = jnp.zeros((tm, LANES), F32)
    r4 = jnp.zeros((tm, LANES), F32)
    for k in range(TOP_K):
        pick, idx = picks[k]
        rk = jnp.sum(jnp.where(pick, rank, 0.0), axis=-1, keepdims=True)
        e4 = jnp.where(lane == k, idx, e4)
        w4 = jnp.where(lane == k, es[k] / den, w4)
        r4 = jnp.where(lane == k, rk, r4)
    e4_ref[...] = e4
    w4_ref[...] = w4
    r4_ref[...] = r4


def _merge(x2, lng, lnb, o_nsa, gy, om, gm, wn, wglu, wmo, wo, l1g, l1b, wrh, wrl, br, tri):
    T, D = x2.shape
    tm = TOKEN_TILE
    tok = lambda w: pl.BlockSpec((tm, w), lambda i: (i, 0))
    ws = [wn, wglu, wmo, wo, l1g, l1b, wrh, wrl, br, tri]
    sd = jax.ShapeDtypeStruct
    lane_out = sd((T, LANES), F32)
    return pl.pallas_call(
        _merge_kernel, grid=(T // tm,),
        in_specs=[tok(D), _full_spec((1, D)), _full_spec((1, D)), tok(NSA_WIDTH), tok(S5_WIDTH),
                  tok(MEM_WIDTH), tok(N_BRANCH * D)] + [_full_spec(w.shape) for w in ws],
        out_specs=[tok(D), tok(LANES), tok(LANES), tok(LANES), _full_spec((SUBLANES, LANES))],
        out_shape=[sd((T, D), F32), lane_out, lane_out, lane_out, sd((SUBLANES, LANES), F32)],
        scratch_shapes=[pltpu.VMEM((SUBLANES, LANES), F32)],
        compiler_params=_cparams("arbitrary"), name="merge",
    )(x2, lng, lnb, o_nsa, gy, om, gm, *ws)


def _slots_kernel(cnt_ref, e4_ref, r4_ref, triu_ref, dest_ref, blk_ref, used_ref):
    tm = e4_ref.shape[0]
    n_blk = blk_ref.shape[0]
    cnt = cnt_ref[...]
    nblk_e = jnp.floor((cnt + (MOE_ROWS - 1)) * (1.0 / MOE_ROWS))
    end_b = _dot(nblk_e.astype(BF16), triu_ref[...])
    start_rows = (end_b - nblk_e)[0:1] * MOE_ROWS
    lane = lax.broadcasted_iota(I32, (tm, LANES), 1)
    lane_f = lane.astype(F32)
    dest = jnp.zeros((tm, LANES), F32)
    for k in range(TOP_K):
        ek = e4_ref[:, k:k + 1]
        base = jnp.sum(jnp.where(lane_f == ek, start_rows, 0.0), axis=-1, keepdims=True)
        dest = jnp.where(lane == k, base + r4_ref[:, k:k + 1], dest)
    dest_ref[...] = dest.astype(I32)
    blk_i = lax.broadcasted_iota(I32, (n_blk, LANES), 0).astype(F32)
    lane_b = lax.broadcasted_iota(I32, (n_blk, LANES), 1)
    ended = jnp.where((end_b[0:1] <= blk_i) & (lane_b < N_EXPERTS), 1.0, 0.0)
    owner = jnp.minimum(jnp.sum(ended, axis=-1, keepdims=True), float(N_EXPERTS - 1))
    blk_ref[...] = jnp.broadcast_to(owner, (n_blk, LANES)).astype(I32)
    lane8 = lax.broadcasted_iota(I32, (SUBLANES, LANES), 1)
    total = jnp.sum(jnp.where(lane8 == N_EXPERTS - 1, end_b, 0.0), axis=-1, keepdims=True)
    used_ref[...] = jnp.broadcast_to(total, (SUBLANES, LANES)).astype(I32)


def _slots(cnt, e4, r4, triu, n_blk):
    T = e4.shape[0]
    tm = TOKEN_TILE
    tok = pl.BlockSpec((tm, LANES), lambda i: (i, 0))
    sd = jax.ShapeDtypeStruct
    return pl.pallas_call(
        _slots_kernel, grid=(T // tm,),
        in_specs=[_full_spec((SUBLANES, LANES)), tok, tok, _full_spec(triu.shape)],
        out_specs=[tok, _full_spec((n_blk, LANES)), _full_spec((SUBLANES, LANES))],
        out_shape=[sd((T, LANES), I32), sd((n_blk, LANES), I32), sd((SUBLANES, LANES), I32)],
        compiler_params=_cparams("arbitrary"), name="slots",
    )(cnt, e4, r4, triu)


def _dispatch_kernel(dest_ref, h_ref, xs_in_ref, xs_ref, sem):
    del xs_in_ref
    tm = h_ref.shape[0]

    def row_copy(r, slot):
        return pltpu.make_async_copy(h_ref.at[pl.ds(r, 1)], xs_ref.at[pl.ds(slot, 1)], sem)

    def issue(r, c):
        for k in range(TOP_K):
            row_copy(r, dest_ref[0, 0, r * TOP_K + k]).start()
        return c

    def drain(r, c):
        for k in range(TOP_K):
            row_copy(0, 0).wait()
        return c

    lax.fori_loop(0, tm, issue, 0)
    lax.fori_loop(0, tm, drain, 0)


def _dispatch(dest3, h1, xs_zero):
    T, D = h1.shape
    tm = TOKEN_TILE
    return pl.pallas_call(
        _dispatch_kernel, grid=(T // tm,),
        in_specs=[pl.BlockSpec((1, 1, tm * TOP_K), lambda i: (i, 0, 0), memory_space=pltpu.SMEM),
                  pl.BlockSpec((tm, D), lambda i: (i, 0)),
                  pl.BlockSpec(memory_space=pl.ANY)],
        out_specs=pl.BlockSpec(memory_space=pl.ANY),
        out_shape=jax.ShapeDtypeStruct(xs_zero.shape, xs_zero.dtype),
        scratch_shapes=[pltpu.SemaphoreType.DMA(())],
        input_output_aliases={2: 0},
        compiler_params=_cparams("arbitrary"), name="dispatch",
    )(dest3, h1, xs_zero)


def _expert_kernel(blk_ref, used_ref, xs_ref, wgu_ref, bgu_ref, wd_ref, bd_ref, ys_ref):
    @pl.when(pl.program_id(0) < used_ref[0])
    def _():
        gu = _dot(xs_ref[...].astype(BF16), wgu_ref[0].astype(BF16)) + bgu_ref[0]
        g = jnp.minimum(gu[:, :D_FF], SWIGLU_LIMIT)
        lin = jnp.clip(gu[:, D_FF:], -SWIGLU_LIMIT, SWIGLU_LIMIT)
        act = g * jax.nn.sigmoid(SWIGLU_ALPHA * g) * (lin + 1.0)
        ys_ref[...] = _dot(act.astype(BF16), wd_ref[0].astype(BF16)) + bd_ref[0]

    @pl.when(pl.program_id(0) >= used_ref[0])
    def _():
        ys_ref[...] = jnp.zeros_like(ys_ref)


def _experts(blk_expert, n_used, xs, w_gate_up, b_gate_up, w_down, b_down):
    cap, D = xs.shape
    n_blk = cap // MOE_ROWS
    E = w_gate_up.shape[0]
    live = lambda i, used: jnp.minimum(i, used[0] - 1)
    row = pl.BlockSpec((MOE_ROWS, D), lambda i, blk, used: (live(i, used), 0))
    by_e = lambda shape: pl.BlockSpec((1,) + shape, lambda i, blk, used: (blk[live(i, used)], 0, 0))
    grid_spec = pltpu.PrefetchScalarGridSpec(
        num_scalar_prefetch=2, grid=(n_blk,),
        in_specs=[row, by_e((D, 2 * D_FF)), by_e((1, 2 * D_FF)), by_e((D_FF, D)), by_e((1, D))],
        out_specs=pl.BlockSpec((MOE_ROWS, D), lambda i, blk, used: (i, 0)))
    return pl.pallas_call(
        _expert_kernel, grid_spec=grid_spec, out_shape=jax.ShapeDtypeStruct((cap, D), F32),
        compiler_params=_cparams("arbitrary"), name="experts",
    )(blk_expert, n_used, xs, w_gate_up, b_gate_up.reshape(E, 1, 2 * D_FF), w_down, b_down.reshape(E, 1, D))


def _combine_kernel(dest_ref, w4_ref, h1_ref, g_ref, b_ref, ys_ref, o_ref, buf_ref, sem):
    tm = h1_ref.shape[0]

    def row_copy(r, k, slot):
        return pltpu.make_async_copy(ys_ref.at[pl.ds(slot, 1)], buf_ref.at[k, pl.ds(r, 1)], sem)

    def issue(r, c):
        for k in range(TOP_K):
            row_copy(r, k, dest_ref[0, 0, r * TOP_K + k]).start()
        return c

    def drain(r, c):
        for k in range(TOP_K):
            row_copy(0, 0, 0).wait()
        return c

    lax.fori_loop(0, tm, issue, 0)
    lax.fori_loop(0, tm, drain, 0)
    acc = DEEPNORM_ALPHA * h1_ref[...]
    for k in range(TOP_K):
        acc = acc + buf_ref[k] * w4_ref[:, k:k + 1]
    o_ref[...] = _layer_norm(acc, g_ref[...], b_ref[...])


def _combine(dest3, w4, h1, ln_g, ln_b, ys):
    T, D = h1.shape
    tm = TOKEN_TILE
    return pl.pallas_call(
        _combine_kernel, grid=(T // tm,),
        in_specs=[pl.BlockSpec((1, 1, tm * TOP_K), lambda i: (i, 0, 0), memory_space=pltpu.SMEM),
                  pl.BlockSpec((tm, LANES), lambda i: (i, 0)),
                  pl.BlockSpec((tm, D), lambda i: (i, 0)),
                  _full_spec((1, D)), _full_spec((1, D)),
                  pl.BlockSpec(memory_space=pl.ANY)],
        out_specs=pl.BlockSpec((tm, D), lambda i: (i, 0)),
        out_shape=jax.ShapeDtypeStruct((T, D), F32),
        scratch_shapes=[pltpu.VMEM((TOP_K, tm, D), F32), pltpu.SemaphoreType.DMA(())],
        compiler_params=_cparams("arbitrary"), name="combine",
    )(dest3, w4, h1, ln_g, ln_b, ys)


def _rope_tables(positions):
    inv = ROPE_THETA ** (-jnp.arange(0, ROT_DIM, 2, dtype=F32) / ROT_DIM)
    ang = positions.astype(F32)[..., None] * inv
    cos, sin = jnp.cos(ang), jnp.sin(ang)
    pad = HEAD_DIM - ROT_DIM
    one = jnp.ones(cos.shape[:-1] + (pad,), F32)
    zero = jnp.zeros(cos.shape[:-1] + (pad,), F32)
    zh = jnp.zeros_like(sin)
    cos_t = jnp.concatenate([cos, cos, one], axis=-1)
    sin_a = jnp.concatenate([zh, sin, zero], axis=-1)
    sin_b = jnp.concatenate([-sin, zh, zero], axis=-1)
    two = lambda t: jnp.concatenate([t, t], axis=-1)
    return two(cos_t), two(sin_a), two(sin_b)


def _split_w_in(w_in):
    widths = (NSA_WIDTH,) + (KV_WIDTH,) * 6 + (NSA_HEADS * N_BRANCH, S5_WIDTH, MEM_WIDTH, N_BRANCH * D_MODEL)
    offs = [0]
    for w in widths:
        offs.append(offs[-1] + w)
    col = lambda i: w_in[:, offs[i]:offs[i + 1]]
    wq, kc, vc, ks, vs, kw, vw, wg, wu, wqm, wm = (col(i) for i in range(11))
    wk = jnp.concatenate([kc, ks, kw], axis=1)
    wv = jnp.concatenate([vc, vs, vw], axis=1)
    per_group = NSA_HPG * N_BRANCH
    wg_pad = jnp.zeros((w_in.shape[0], NSA_GROUPS * LANES), w_in.dtype)
    for g in range(NSA_GROUPS):
        wg_pad = wg_pad.at[:, g * LANES:g * LANES + per_group].set(wg[:, g * per_group:(g + 1) * per_group])
    return tuple(w.astype(BF16) for w in (wq, wk, wv, wg_pad, wu, wqm, wm))


def _compress_weights(w1):
    half = CMP_BLOCK // 2
    eye = jnp.eye(NSA_GROUPS, dtype=w1.dtype)

    def arrange(w_half):
        full = jnp.einsum('sdf,gh->sgdhf', w_half, eye)
        return full.reshape(half * NSA_GROUPS * HEAD_DIM, NSA_GROUPS * CMP_HIDDEN).astype(BF16)

    return (w1.reshape(CMP_BLOCK * HEAD_DIM, CMP_HIDDEN).astype(BF16), arrange(w1[:half]), arrange(w1[half:]))


def _s5_weights(a_re, a_im, log_dt, b_re, b_im, c_re, c_im):
    step = jnp.exp(log_dt)[:, None]
    mag = jnp.exp(a_re * step)
    ab_re, ab_im = mag * jnp.cos(a_im * step), mag * jnp.sin(a_im * step)
    den = a_re * a_re + a_im * a_im
    nr = ab_re - 1.0
    coef_re = (nr * a_re + ab_im * a_im) / den
    coef_im = (ab_im * a_re - nr * a_im) / den
    bb_re = coef_re[..., None] * b_re - coef_im[..., None] * b_im
    bb_im = coef_re[..., None] * b_im + coef_im[..., None] * b_re
    eye = jnp.eye(S5_GROUPS, dtype=F32)
    n_state = S5_GROUPS * S5_STATE
    n_tile = n_state // LANES
    in_per = n_tile // (S5_WIDTH // LANES)

    def in_map(bb):
        return jnp.einsum('gnp,gh->gphn', bb, eye).reshape(S5_WIDTH, n_state)

    def out_map(c):
        return jnp.einsum('gpn,gh->gnhp', c, eye).reshape(n_state, S5_WIDTH)

    bf_re, bf_im = in_map(bb_re), in_map(bb_im)
    cf_re, cf_im = out_map(c_re), out_map(-c_im)
    wb, wc = [], []
    for c in range(n_tile):
        i = c // in_per
        rs, cs = slice(i * LANES, (i + 1) * LANES), slice(c * LANES, (c + 1) * LANES)
        wb.append(jnp.concatenate([bf_re[rs, cs], bf_im[rs, cs]], axis=1))
        wc.append(jnp.concatenate([cf_re[cs, rs], cf_im[cs, rs]], axis=0))
    wb = jnp.stack(wb).astype(BF16)
    wc = jnp.stack(wc).astype(BF16)
    return wb, wc, ab_re.reshape(n_tile, LANES), ab_im.reshape(n_tile, LANES)


def _layer(x, mem, positions, ln_emb_g, ln_emb_b, w_in, pe_k, pe_v, w_kcmp1, w_kcmp2, w_vcmp1, w_vcmp2,
           s5_a_re, s5_a_im, s5_log_dt, s5_b_re, s5_b_im, s5_c_re, s5_c_im, s5_d,
           w_s5_glu, w_mem_kv, w_nsa_out, w_mem_out, w_o, ln1_g, ln1_b, w_router, b_router,
           w_gate_up, b_gate_up, w_down, b_down, ln2_g, ln2_b):
    B, L, D = x.shape
    T = B * L
    row = lambda v: v.reshape(1, -1)

    cos_t, sin_a, sin_b = _rope_tables(positions)
    (q_hm, kc, vc, ks, vs, kw, vw, gates, u, qm, gm) = _inproj(
        x, row(ln_emb_g), row(ln_emb_b), cos_t, sin_a, sin_b, *_split_w_in(w_in))

    n_chunk = L // CMP_STRIDE
    chunked = lambda t: t.reshape(B, n_chunk, CMP_STRIDE * KV_WIDTH)
    pe_rows = lambda pe: jnp.broadcast_to(pe.reshape(1, -1), (SUBLANES, CMP_BLOCK * HEAD_DIM)).astype(BF16)
    wk1f, wk1a, wk1b = _compress_weights(w_kcmp1)
    wv1f, wv1a, wv1b = _compress_weights(w_vcmp1)
    ck, cv = _compress(chunked(kc), chunked(vc), pe_rows(pe_k), pe_rows(pe_v), wk1f, wv1f,
                       wk1a, wk1b, wv1a, wv1b, w_kcmp2.astype(BF16), w_vcmp2.astype(BF16))

    per_sb = SEL_BLOCK // CMP_STRIDE
    c_ix = jnp.arange(n_chunk)[:, None]
    n_ix = jnp.arange(L // SEL_BLOCK)[None, :]
    w_score = ((c_ix // per_sb == n_ix).astype(F32) + ((c_ix + 1) // per_sb == n_ix).astype(F32)).astype(BF16)
    o_nsa = _nsa(q_hm, ck, cv, ks, vs, kw, vw, gates, w_score)

    wb, wc, a_re, a_im = _s5_weights(s5_a_re, s5_a_im, s5_log_dt, s5_b_re, s5_b_im, s5_c_re, s5_c_im)
    gy = _s5(u, wb, wc, a_re, a_im, row(s5_d))

    k_mem, v_mem = _memkv(mem, w_mem_kv.astype(BF16))
    o_mem = _memattn(qm, k_mem, v_mem)

    pad_e = LANES - N_EXPERTS
    wr = jnp.pad(w_router, ((0, 0), (0, pad_e)))
    wr_hi = wr.astype(BF16)
    wr_lo = (wr - wr_hi.astype(F32)).astype(BF16)
    br = jnp.concatenate([b_router, jnp.full((pad_e,), -jnp.inf, F32)]).reshape(1, LANES)
    tm = TOKEN_TILE
    tri = (jnp.arange(tm)[None, :] < jnp.arange(tm)[:, None]).astype(BF16)
    flat = lambda t: t.reshape(T, t.shape[-1])
    h1, e4, w4, r4, cnt = _merge(
        flat(x), row(ln_emb_g), row(ln_emb_b), flat(o_nsa), flat(gy), flat(o_mem), flat(gm),
        w_nsa_out.astype(BF16), w_s5_glu.astype(BF16), w_mem_out.astype(BF16), w_o.astype(BF16),
        row(ln1_g), row(ln1_b), wr_hi, wr_lo, br, tri)

    cap = (T * TOP_K + MOE_ROWS - 1) // MOE_ROWS * MOE_ROWS + N_EXPERTS * MOE_ROWS
    n_blk = cap // MOE_ROWS
    triu = (jnp.arange(LANES)[:, None] <= jnp.arange(LANES)[None, :]).astype(BF16)
    dest, blk_owner, used = _slots(cnt, e4, r4, triu, n_blk)
    dest3 = dest[:, :TOP_K].reshape(T // tm, 1, tm * TOP_K)
    blk_expert = blk_owner[:, 0]
    n_used = used[0, :1]

    xs = _dispatch(dest3, h1, jnp.zeros((cap, D), F32))
    ys = _experts(blk_expert, n_used, xs, w_gate_up, b_gate_up, w_down, b_down)
    out = _combine(dest3, w4, h1, row(ln2_g), row(ln2_b), ys)
    return out.reshape(B, L, D)


def kernel(x, mem, positions, ln_emb_g, ln_emb_b, w_in, pe_k_cmp, pe_v_cmp, w_kcmp1, w_kcmp2, w_vcmp1, w_vcmp2, s5_a_re, s5_a_im, s5_log_dt, s5_b_re, s5_b_im, s5_c_re, s5_c_im, s5_d, w_s5_glu, w_mem_kv, w_nsa_out, w_mem_out, w_o, ln1_g, ln1_b, w_router, b_router, w_gate_up, b_gate_up, w_down, b_down, ln2_g, ln2_b):
    assert w_in.shape[0] == DEPTH
    l = 0
    return _layer(x, mem, positions, ln_emb_g, ln_emb_b, w_in[l], pe_k_cmp[l], pe_v_cmp[l], w_kcmp1[l],
                  w_kcmp2[l], w_vcmp1[l], w_vcmp2[l], s5_a_re[l], s5_a_im[l], s5_log_dt[l], s5_b_re[l],
                  s5_b_im[l], s5_c_re[l], s5_c_im[l], s5_d[l], w_s5_glu[l], w_mem_kv[l], w_nsa_out[l],
                  w_mem_out[l], w_o[l], ln1_g[l], ln1_b[l], w_router[l], b_router[l], w_gate_up[l],
                  b_gate_up[l], w_down[l], b_down[l], ln2_g[l], ln2_b[l])
```

```python
import functools
import math

import jax
import jax.numpy as jnp
from jax import lax
from jax.experimental import pallas as pl
from jax.experimental.pallas import tpu as pltpu

F32 = jnp.float32
BF16 = jnp.bfloat16
I32 = jnp.int32

D_MODEL = 1024
NSA_HEADS = 8
NSA_GROUPS = 2
NSA_HPG = NSA_HEADS // NSA_GROUPS
HEAD_DIM = 64
NSA_WIDTH = NSA_HEADS * HEAD_DIM
KV_WIDTH = NSA_GROUPS * HEAD_DIM
CMP_BLOCK = 32
CMP_STRIDE = 16
CMP_HIDDEN = 128
SEL_BLOCK = 64
N_SEL = 16
WINDOW = 512
Q_BLOCK = 128
ROPE_THETA = 500000.0
ROT_DIM = HEAD_DIM // 4
S5_WIDTH = 512
S5_GROUP_DIM = 16
S5_GROUPS = S5_WIDTH // S5_GROUP_DIM
S5_STATE = 64
MEM_HEADS = 4
MEM_HEAD_DIM = 128
MEM_WIDTH = MEM_HEADS * MEM_HEAD_DIM
N_BRANCH = 3
N_EXPERTS = 32
TOP_K = 4
D_FF = 1024
SWIGLU_LIMIT = 7.0
SWIGLU_ALPHA = 1.702
LN_EPS = 1e-5
DEPTH = 1
DEEPNORM_ALPHA = (2 * DEPTH) ** 0.25

LANES = 128
SUBLANES = 8
VMEM_LIMIT_BYTES = 56 * 1024 * 1024

TOKEN_TILE = 256
SEL_KV_TILE = 512
S5_CHUNK = 512
S5_PITCH = S5_CHUNK + 8
MOE_ROWS = 256
NEG_BIG = -(2.0 ** 100)


def _cparams(*sem):
    return pltpu.CompilerParams(dimension_semantics=sem, vmem_limit_bytes=VMEM_LIMIT_BYTES)


def _dot(a, b):
    return jnp.dot(a, b, preferred_element_type=F32)


def _dot_nt(a, b):
    return lax.dot_general(a, b, (((1,), (1,)), ((), ())), preferred_element_type=F32)


def _layer_norm(x, g, b):
    mu = jnp.mean(x, axis=-1, keepdims=True)
    xc = x - mu
    var = jnp.mean(xc * xc, axis=-1, keepdims=True)
    return xc * lax.rsqrt(var + LN_EPS) * g + b


def _gelu_tanh(x):
    cdf = 0.5 * (1.0 + jnp.tanh(math.sqrt(2.0 / math.pi) * (x + 0.044715 * (x * x * x))))
    return x * cdf


def _masked_softmax(s, mask):
    s = jnp.where(mask, s, -jnp.inf)
    m = jnp.max(s, axis=-1, keepdims=True)
    m = jnp.where(m > -jnp.inf, m, 0.0)
    e = jnp.exp(s - m)
    return e / jnp.maximum(jnp.sum(e, axis=-1, keepdims=True), jnp.finfo(F32).tiny)


def _split3(x):
    hi = x.astype(BF16)
    r1 = x - hi.astype(F32)
    mid = r1.astype(BF16)
    lo = (r1 - mid.astype(F32)).astype(BF16)
    return hi, mid, lo


def _full_spec(shape):
    nd = len(shape)
    return pl.BlockSpec(shape, lambda *_: (0,) * nd)


def _inproj_kernel(x_ref, g_ref, b_ref, cos_ref, sa_ref, sb_ref,
                   wq_ref, wk_ref, wv_ref, wg_ref, wu_ref, wqm_ref, wm_ref,
                   q_ref, kc_ref, vc_ref, ks_ref, vs_ref, kw_ref, vw_ref,
                   gate_ref, u_ref, qm_ref, gm_ref):
    h = _layer_norm(x_ref[0], g_ref[...], b_ref[...])
    hb = h.astype(BF16)
    cos_t, sin_a, sin_b = cos_ref[0], sa_ref[0], sb_ref[0]

    def rope(t):
        return (t * cos_t + pltpu.roll(t, ROT_DIM // 2, 1) * sin_a
                + pltpu.roll(t, LANES - ROT_DIM // 2, 1) * sin_b)

    q = _dot(hb, wq_ref[...])
    for c in range(NSA_WIDTH // LANES):
        qc = rope(q[:, c * LANES:(c + 1) * LANES])
        for hh in range(2):
            q_ref[0, 2 * c + hh] = qc[:, hh * HEAD_DIM:(hh + 1) * HEAD_DIM].astype(BF16)
    k3 = _dot(hb, wk_ref[...])
    kc = rope(k3[:, 0:LANES])
    ks = rope(k3[:, LANES:2 * LANES])
    kw = rope(k3[:, 2 * LANES:3 * LANES])
    v3 = _dot(hb, wv_ref[...])
    kc_ref[0] = kc.astype(BF16)
    vc_ref[0] = v3[:, 0:LANES].astype(BF16)
    tm = x_ref.shape[1]
    pos = pl.program_id(1) * tm + lax.broadcasted_iota(I32, (tm, LANES), 0)
    blk_hot = jnp.where(lax.broadcasted_iota(I32, (tm, LANES), 1) == pos // SEL_BLOCK, 1.0, 0.0)
    lane_pad = jnp.zeros((tm, LANES - HEAD_DIM), F32)
    for g in range(NSA_GROUPS):
        sl = slice(g * HEAD_DIM, (g + 1) * HEAD_DIM)
        ks_ref[0, g] = jnp.concatenate([blk_hot, ks[:, sl], lane_pad], axis=1).astype(BF16)
        kw_ref[0, g] = kw[:, sl].astype(BF16)
        vs_ref[0, g] = v3[:, LANES:2 * LANES][:, sl].astype(BF16)
        vw_ref[0, g] = v3[:, 2 * LANES:3 * LANES][:, sl].astype(BF16)
    gate_ref[0] = jax.nn.sigmoid(_dot(hb, wg_ref[...]))
    u_ref[0] = _dot(hb, wu_ref[...])
    qm_ref[0] = _dot(hb, wqm_ref[...]).astype(BF16)
    gm_ref[0] = jax.nn.sigmoid(_dot(hb, wm_ref[...]))


def _inproj(x, ln_g, ln_b, cos_t, sin_a, sin_b, wq, wk, wv, wg, wu, wqm, wm):
    B, L, D = x.shape
    tm = TOKEN_TILE
    grid = (B, L // tm)
    tok = lambda w: pl.BlockSpec((1, tm, w), lambda b, i: (b, i, 0))
    head = lambda n: pl.BlockSpec((1, n, tm, HEAD_DIM), lambda b, i: (b, 0, i, 0))
    in_specs = [tok(D), _full_spec((1, D)), _full_spec((1, D)), tok(LANES), tok(LANES), tok(LANES)]
    in_specs += [_full_spec(w.shape) for w in (wq, wk, wv, wg, wu, wqm, wm)]
    sd = jax.ShapeDtypeStruct
    out_shape = [
        sd((B, NSA_HEADS, L, HEAD_DIM), BF16),
        sd((B, L, KV_WIDTH), BF16), sd((B, L, KV_WIDTH), BF16),
        sd((B, NSA_GROUPS, L, 2 * LANES), BF16), sd((B, NSA_GROUPS, L, HEAD_DIM), BF16),
        sd((B, NSA_GROUPS, L, HEAD_DIM), BF16), sd((B, NSA_GROUPS, L, HEAD_DIM), BF16),
        sd((B, L, NSA_GROUPS * LANES), F32),
        sd((B, L, S5_WIDTH), F32),
        sd((B, L, MEM_WIDTH), BF16),
        sd((B, L, N_BRANCH * D), F32),
    ]
    ksel = pl.BlockSpec((1, NSA_GROUPS, tm, 2 * LANES), lambda b, i: (b, 0, i, 0))
    out_specs = [head(NSA_HEADS), tok(KV_WIDTH), tok(KV_WIDTH), ksel, head(NSA_GROUPS),
                 head(NSA_GROUPS), head(NSA_GROUPS), tok(NSA_GROUPS * LANES), tok(S5_WIDTH),
                 tok(MEM_WIDTH), tok(N_BRANCH * D)]
    return pl.pallas_call(
        _inproj_kernel, grid=grid, in_specs=in_specs, out_specs=out_specs, out_shape=out_shape,
        compiler_params=_cparams("parallel", "parallel"), name="inproj",
    )(x, ln_g, ln_b, cos_t, sin_a, sin_b, wq, wk, wv, wg, wu, wqm, wm)


def _compress_kernel(kc_ref, vc_ref, pek_ref, pev_ref, wk1f_ref, wv1f_ref,
                     wk1a_ref, wk1b_ref, wv1a_ref, wv1b_ref, wk2_ref, wv2_ref, ck_ref, cv_ref):
    n_chunk = kc_ref.shape[1]
    row = lax.broadcasted_iota(I32, (n_chunk, 1), 0)

    def one(x_ref, pe_ref, w1f_ref, w1a_ref, w1b_ref, w2_ref, o_ref):
        x = x_ref[0]
        first = _dot(x, w1a_ref[...])
        second = _dot(x, w1b_ref[...])
        second = pltpu.roll(second, n_chunk - 1, 0)
        pe_term = _dot(pe_ref[...], w1f_ref[...])[0:1]
        pe_term = jnp.concatenate([pe_term] * NSA_GROUPS, axis=1)
        hid = _gelu_tanh(first + second + pe_term).astype(BF16)
        for g in range(NSA_GROUPS):
            o = _dot(hid[:, g * CMP_HIDDEN:(g + 1) * CMP_HIDDEN], w2_ref[...])
            o_ref[0, g] = jnp.where(row < n_chunk - 1, o, 0.0).astype(BF16)

    one(kc_ref, pek_ref, wk1f_ref, wk1a_ref, wk1b_ref, wk2_ref, ck_ref)
    one(vc_ref, pev_ref, wv1f_ref, wv1a_ref, wv1b_ref, wv2_ref, cv_ref)


def _compress(kc_r, vc_r, pek, pev, wk1f, wv1f, wk1a, wk1b, wv1a, wv1b, wk2, wv2):
    B, n_chunk, width = kc_r.shape
    blk = pl.BlockSpec((1, n_chunk, width), lambda b: (b, 0, 0))
    out = pl.BlockSpec((1, NSA_GROUPS, n_chunk, HEAD_DIM), lambda b: (b, 0, 0, 0))
    ws = [pek, pev, wk1f, wv1f, wk1a, wk1b, wv1a, wv1b, wk2, wv2]
    sd = jax.ShapeDtypeStruct((B, NSA_GROUPS, n_chunk, HEAD_DIM), BF16)
    return pl.pallas_call(
        _compress_kernel, grid=(B,), in_specs=[blk, blk] + [_full_spec(w.shape) for w in ws],
        out_specs=[out, out], out_shape=[sd, sd], compiler_params=_cparams("parallel"), name="compress",
    )(kc_r, vc_r, *ws)


def _nsa_kernel(q_ref, ck_ref, cv_ref, ks_ref, vs_ref, kw_ref, vw_ref, gate_ref, wsc_ref, o_ref):
    seq_len = ks_ref.shape[2]
    n_cmp = ck_ref.shape[2]
    n_sb = seq_len // SEL_BLOCK
    n_sel = min(N_SEL, n_sb)
    rows = NSA_HPG * Q_BLOCK
    q0 = pl.program_id(2) * Q_BLOCK
    q = q_ref[0].reshape(rows, HEAD_DIM) * (HEAD_DIM ** -0.5)
    t1 = q0 + lax.broadcasted_iota(I32, (Q_BLOCK, 1), 0)
    t4 = jnp.concatenate([t1] * NSA_HPG, axis=0)

    s = _dot_nt(q, ck_ref[0, 0])
    c_end = lax.broadcasted_iota(I32, (1, n_cmp), 1) * CMP_STRIDE + (CMP_BLOCK - 1)
    p_cmp = _masked_softmax(s, c_end <= t4)
    o_cmp = _dot(p_cmp.astype(BF16), cv_ref[0, 0])

    imp = p_cmp[0:Q_BLOCK]
    for hh in range(1, NSA_HPG):
        imp = imp + p_cmp[hh * Q_BLOCK:(hh + 1) * Q_BLOCK]
    w_sc = wsc_ref[...]
    score = sum(_dot(part, w_sc) for part in _split3(imp))
    jb = lax.broadcasted_iota(I32, (Q_BLOCK, n_sb), 1)
    tb = t1 // SEL_BLOCK
    forced = (jb == 0) | (jb == tb) | (jb == tb - 1)
    score = jnp.where(forced, jnp.inf, jnp.where(jb > tb, -jnp.inf, score))
    sel_bias = jnp.full((Q_BLOCK, n_sb), NEG_BIG, F32)
    for _ in range(n_sel):
        pick = jb == jnp.argmax(score, axis=-1, keepdims=True)
        sel_bias = jnp.where(pick, 0.0, sel_bias)
        score = jnp.where(pick, -jnp.inf, score)
    if n_sb < LANES:
        sel_bias = jnp.concatenate([sel_bias, jnp.zeros((Q_BLOCK, LANES - n_sb), F32)], axis=1)

    span = WINDOW + Q_BLOCK
    w0 = pl.multiple_of(jnp.maximum(q0 - WINDOW, 0), Q_BLOCK)
    kwin = kw_ref[0, 0, pl.ds(w0, span), :]
    vwin = vw_ref[0, 0, pl.ds(w0, span), :]
    s = _dot_nt(q, kwin)
    diff = t4 - (w0 + lax.broadcasted_iota(I32, (1, span), 1))
    p = _masked_softmax(s, (diff >= 0) & (diff < WINDOW))
    o_win = _dot(p.astype(BF16), vwin)

    tk = SEL_KV_TILE
    q_aug = jnp.concatenate([jnp.concatenate([sel_bias.astype(BF16)] * NSA_HPG, axis=0), q,
                             jnp.zeros((rows, LANES - HEAD_DIM), BF16)], axis=1)

    def sel_tile(j, carry, causal):
        m_run, l_run, acc = carry
        k0 = pl.multiple_of(j * tk, tk)
        sc = _dot_nt(q_aug, ks_ref[0, 0, pl.ds(k0, tk), :])
        if causal:
            kpos = k0 + lax.broadcasted_iota(I32, (1, tk), 1)
            sc = jnp.where(kpos <= t4, sc, NEG_BIG)
        m_new = jnp.maximum(m_run, jnp.max(sc, axis=-1, keepdims=True))
        alpha = jnp.exp(m_run - m_new)
        p = jnp.exp(sc - m_new)
        l_new = alpha * l_run + jnp.sum(p, axis=-1, keepdims=True)
        acc_new = alpha * acc + _dot(p.astype(BF16), vs_ref[0, 0, pl.ds(k0, tk), :])
        return m_new, l_new, acc_new

    def sel_pair(jj, carry, causal):
        return sel_tile(2 * jj + 1, sel_tile(2 * jj, carry, causal), causal)

    init = (jnp.full((rows, 1), NEG_BIG, F32), jnp.zeros((rows, 1), F32), jnp.zeros((rows, HEAD_DIM), F32))
    last_pair = (q0 // tk) // 2
    carry = lax.fori_loop(0, last_pair, functools.partial(sel_pair, causal=False), init)
    _, l_fin, acc = sel_pair(last_pair, carry, True)
    o_sel = acc / l_fin

    gt = gate_ref[0]
    outs = []
    for hh in range(NSA_HPG):
        sl = slice(hh * Q_BLOCK, (hh + 1) * Q_BLOCK)
        c = hh * N_BRANCH
        outs.append(o_cmp[sl] * gt[:, c:c + 1] + o_sel[sl] * gt[:, c + 1:c + 2] + o_win[sl] * gt[:, c + 2:c + 3])
    o_ref[0] = jnp.concatenate(outs, axis=1).astype(BF16)


def _nsa(q_hm, ck, cv, ks, vs, kw, vw, gates, w_score):
    B, _, L, _ = q_hm.shape
    assert L // SEL_BLOCK <= LANES and (L // SEL_KV_TILE) % 2 == 0 and L >= WINDOW + Q_BLOCK
    n_cmp = ck.shape[2]
    grid = (B, NSA_GROUPS, L // Q_BLOCK)
    qspec = pl.BlockSpec((1, NSA_HPG, Q_BLOCK, HEAD_DIM), lambda b, g, i: (b, g, i, 0))
    cspec = pl.BlockSpec((1, 1, n_cmp, HEAD_DIM), lambda b, g, i: (b, g, 0, 0))
    kvspec = pl.BlockSpec((1, 1, L, HEAD_DIM), lambda b, g, i: (b, g, 0, 0))
    ksspec = pl.BlockSpec((1, 1, L, 2 * LANES), lambda b, g, i: (b, g, 0, 0))
    gspec = pl.BlockSpec((1, Q_BLOCK, LANES), lambda b, g, i: (b, i, g))
    ospec = pl.BlockSpec((1, Q_BLOCK, NSA_HPG * HEAD_DIM), lambda b, g, i: (b, i, g))
    return pl.pallas_call(
        _nsa_kernel, grid=grid,
        in_specs=[qspec, cspec, cspec, ksspec, kvspec, kvspec, kvspec, gspec, _full_spec(w_score.shape)],
        out_specs=ospec, out_shape=jax.ShapeDtypeStruct((B, L, NSA_WIDTH), BF16),
        compiler_params=_cparams("parallel", "parallel", "arbitrary"), name="nsa",
    )(q_hm, ck, cv, ks, vs, kw, vw, gates, w_score)


def _s5_kernel(u_ref, wb_ref, wc_ref, are_ref, aim_ref, d_ref, y_ref, sre_ref, sim_ref, carry_ref):
    n_b, chunk, _ = u_ref.shape
    n_tile = wb_ref.shape[0]
    in_per = n_tile // (S5_WIDTH // LANES)
    pitch = S5_PITCH

    @pl.when(pl.program_id(0) == 0)
    def _():
        carry_ref[...] = jnp.zeros_like(carry_ref)

    for b in range(n_b):
        for c in range(n_tile):
            i = c // in_per
            ub = u_ref[b, :, i * LANES:(i + 1) * LANES].astype(BF16)
            r = _dot(ub, wb_ref[c])
            sre_ref[b, c * pitch:c * pitch + chunk, :] = r[:, :LANES]
            sim_ref[b, c * pitch:c * pitch + chunk, :] = r[:, LANES:]

    a_re, a_im = are_ref[...], aim_ref[...]

    def step(t, carry):
        out = []
        for b in range(n_b):
            s_re, s_im = carry[2 * b], carry[2 * b + 1]
            rows = pl.ds(t, n_tile, stride=pitch)
            n_re = a_re * s_re - a_im * s_im + sre_ref[b, rows, :]
            n_im = a_re * s_im + a_im * s_re + sim_ref[b, rows, :]
            sre_ref[b, rows, :] = n_re
            sim_ref[b, rows, :] = n_im
            out += [n_re, n_im]
        return tuple(out)

    init = tuple(carry_ref[i] for i in range(2 * n_b))
    fin = lax.fori_loop(0, chunk, step, init, unroll=8)
    for i in range(2 * n_b):
        carry_ref[i] = fin[i]

    for b in range(n_b):
        for o in range(S5_WIDTH // LANES):
            acc = jnp.zeros((chunk, LANES), F32)
            for c in range(o * in_per, (o + 1) * in_per):
                rows = slice(c * pitch, c * pitch + chunk)
                acc = acc + _dot(sre_ref[b, rows, :].astype(BF16), wc_ref[c, :LANES])
                acc = acc + _dot(sim_ref[b, rows, :].astype(BF16), wc_ref[c, LANES:])
            lanes = slice(o * LANES, (o + 1) * LANES)
            y = acc + d_ref[:, lanes] * u_ref[b, :, lanes]
            y_ref[b, :, lanes] = _gelu_tanh(y).astype(BF16)


def _s5(u, wb, wc, a_re, a_im, d_skip):
    B, L, W = u.shape
    chunk = S5_CHUNK
    n_tile = wb.shape[0]
    blk = pl.BlockSpec((B, chunk, W), lambda i: (0, i, 0))
    slab = pltpu.VMEM((B, n_tile * S5_PITCH, LANES), F32)
    return pl.pallas_call(
        _s5_kernel, grid=(L // chunk,),
        in_specs=[blk] + [_full_spec(w.shape) for w in (wb, wc, a_re, a_im, d_skip)],
        out_specs=blk, out_shape=jax.ShapeDtypeStruct((B, L, W), BF16),
        scratch_shapes=[slab, slab, pltpu.VMEM((2 * B, n_tile, LANES), F32)],
        compiler_params=_cparams("arbitrary"), name="s5",
    )(u, wb, wc, a_re, a_im, d_skip)


def _memkv_kernel(mem_ref, w_ref, k_ref, v_ref):
    kv = _dot(mem_ref[0].astype(BF16), w_ref[...])
    k_ref[0] = kv[:, :MEM_WIDTH].astype(BF16)
    v_ref[0] = kv[:, MEM_WIDTH:].astype(BF16)


def _memkv(mem, w_kv):
    B, M, D = mem.shape
    out = pl.BlockSpec((1, M, MEM_WIDTH), lambda b: (b, 0, 0))
    sd = jax.ShapeDtypeStruct((B, M, MEM_WIDTH), BF16)
    return pl.pallas_call(
        _memkv_kernel, grid=(B,),
        in_specs=[pl.BlockSpec((1, M, D), lambda b: (b, 0, 0)), _full_spec(w_kv.shape)],
        out_specs=[out, out], out_shape=[sd, sd], compiler_params=_cparams("parallel"), name="memkv",
    )(mem, w_kv)


def _memattn_kernel(q_ref, k_ref, v_ref, o_ref):
    outs = []
    for h in range(MEM_HEADS):
        sl = slice(h * MEM_HEAD_DIM, (h + 1) * MEM_HEAD_DIM)
        s = _dot_nt(q_ref[0, :, sl], k_ref[0, :, sl]) * (MEM_HEAD_DIM ** -0.5)
        m = jnp.max(s, axis=-1, keepdims=True)
        e = jnp.exp(s - m)
        p = e / jnp.sum(e, axis=-1, keepdims=True)
        outs.append(_dot(p.astype(BF16), v_ref[0, :, sl]))
    o_ref[0] = jnp.concatenate(outs, axis=1).astype(BF16)


def _memattn(qm, k, v):
    B, L, W = qm.shape
    M = k.shape[1]
    tm = TOKEN_TILE
    tok = pl.BlockSpec((1, tm, W), lambda b, i: (b, i, 0))
    kv = pl.BlockSpec((1, M, W), lambda b, i: (b, 0, 0))
    return pl.pallas_call(
        _memattn_kernel, grid=(B, L // tm), in_specs=[tok, kv, kv], out_specs=tok,
        out_shape=jax.ShapeDtypeStruct((B, L, W), BF16),
        compiler_params=_cparams("parallel", "parallel"), name="memattn",
    )(qm, k, v)


def _merge_kernel(x_ref, lng_ref, lnb_ref, on_ref, gy_ref, om_ref, gm_ref,
                  wn_ref, wglu_ref, wmo_ref, wo_ref, l1g_ref, l1b_ref,
                  wrh_ref, wrl_ref, br_ref, tri_ref,
                  h1_ref, e4_ref, w4_ref, r4_ref, cnt_ref, run_ref):
    D = x_ref.shape[1]
    tm = x_ref.shape[0]

    @pl.when(pl.program_id(0) == 0)
    def _():
        run_ref[...] = jnp.zeros_like(run_ref)

    h = _layer_norm(x_ref[...], lng_ref[...], lnb_ref[...])
    y_nsa = _dot(on_ref[...], wn_ref[...])
    glu = _dot(gy_ref[...], wglu_ref[...])
    y_s5 = glu[:, :D] * jax.nn.sigmoid(glu[:, D:])
    y_mem = _dot(om_ref[...], wmo_ref[...])
    merged = gm_ref[:, 0:D] * y_nsa + gm_ref[:, D:2 * D] * y_s5 + gm_ref[:, 2 * D:3 * D] * y_mem
    mix = _dot(merged.astype(BF16), wo_ref[...])
    h1 = _layer_norm(DEEPNORM_ALPHA * h + mix, l1g_ref[...], l1b_ref[...])
    h1_ref[...] = h1

    hh = h1.astype(BF16)
    hl = (h1 - hh.astype(F32)).astype(BF16)
    logits = _dot(hh, wrh_ref[...]) + _dot(hh, wrl_ref[...]) + _dot(hl, wrh_ref[...]) + br_ref[...]
    lane = lax.broadcasted_iota(I32, (tm, LANES), 1)
    lane_f = lane.astype(F32)
    work = logits
    multi = jnp.zeros((tm, LANES), F32)
    vals, picks = [], []
    for _ in range(TOP_K):
        m = jnp.max(work, axis=-1, keepdims=True)
        idx = jnp.min(jnp.where(work == m, lane_f, float(LANES)), axis=-1, keepdims=True)
        pick = lane_f == idx
        vals.append(m)
        picks.append((pick, idx))
        multi = jnp.where(pick, 1.0, multi)
        work = jnp.where(pick, -jnp.inf, work)
    es = [jnp.exp(v - vals[0]) for v in vals]
    den = es[0] + es[1] + es[2] + es[3]
    rank = run_ref[0:1, :] + _dot(tri_ref[...], multi.astype(BF16))
    run_ref[...] = run_ref[...] + jnp.sum(multi, axis=0, keepdims=True)
    cnt_ref[...] = run_ref[...]
    e4 = jnp.zeros((tm, LANES), F32)
    w4 = jnp.zeros((tm, LANES), F32)
    r4 = jnp.zeros((tm, LANES), F32)
    for k in range(TOP_K):
        pick, idx = picks[k]
        rk = jnp.sum(jnp.where(pick, rank, 0.0), axis=-1, keepdims=True)
        e4 = jnp.where(lane == k, idx, e4)
        w4 = jnp.where(lane == k, es[k] / den, w4)
        r4 = jnp.where(lane == k, rk, r4)
    e4_ref[...] = e4
    w4_ref[...] = w4
    r4_ref[...] = r4


def _merge(x2, lng, lnb, o_nsa, gy, om, gm, wn, wglu, wmo, wo, l1g, l1b, wrh, wrl, br, tri):
    T, D = x2.shape
    tm = TOKEN_TILE
    tok = lambda w: pl.BlockSpec((tm, w), lambda i: (i, 0))
    ws = [wn, wglu, wmo, wo, l1g, l1b, wrh, wrl, br, tri]
    sd = jax.ShapeDtypeStruct
    lane_out = sd((T, LANES), F32)
    return pl.pallas_call(
        _merge_kernel, grid=(T // tm,),
        in_specs=[tok(D), _full_spec((1, D)), _full_spec((1, D)), tok(NSA_WIDTH), tok(S5_WIDTH),
                  tok(MEM_WIDTH), tok(N_BRANCH * D)] + [_full_spec(w.shape) for w in ws],
        out_specs=[tok(D), tok(LANES), tok(LANES), tok(LANES), _full_spec((SUBLANES, LANES))],
        out_shape=[sd((T, D), F32), lane_out, lane_out, lane_out, sd((SUBLANES, LANES), F32)],
        scratch_shapes=[pltpu.VMEM((SUBLANES, LANES), F32)],
        compiler_params=_cparams("arbitrary"), name="merge",
    )(x2, lng, lnb, o_nsa, gy, om, gm, *ws)


def _slots_kernel(cnt_ref, e4_ref, r4_ref, triu_ref, dest_ref, blk_ref, used_ref):
    tm = e4_ref.shape[0]
    n_blk = blk_ref.shape[0]
    cnt = cnt_ref[...]
    nblk_e = jnp.floor((cnt + (MOE_ROWS - 1)) * (1.0 / MOE_ROWS))
    end_b = _dot(nblk_e.astype(BF16), triu_ref[...])
    start_rows = (end_b - nblk_e)[0:1] * MOE_ROWS
    lane = lax.broadcasted_iota(I32, (tm, LANES), 1)
    lane_f = lane.astype(F32)
    dest = jnp.zeros((tm, LANES), F32)
    for k in range(TOP_K):
        ek = e4_ref[:, k:k + 1]
        base = jnp.sum(jnp.where(lane_f == ek, start_rows, 0.0), axis=-1, keepdims=True)
        dest = jnp.where(lane == k, base + r4_ref[:, k:k + 1], dest)
    dest_ref[...] = dest.astype(I32)
    blk_i = lax.broadcasted_iota(I32, (n_blk, LANES), 0).astype(F32)
    lane_b = lax.broadcasted_iota(I32, (n_blk, LANES), 1)
    ended = jnp.where((end_b[0:1] <= blk_i) & (lane_b < N_EXPERTS), 1.0, 0.0)
    owner = jnp.minimum(jnp.sum(ended, axis=-1, keepdims=True), float(N_EXPERTS - 1))
    blk_ref[...] = jnp.broadcast_to(owner, (n_blk, LANES)).astype(I32)
    lane8 = lax.broadcasted_iota(I32, (SUBLANES, LANES), 1)
    total = jnp.sum(jnp.where(lane8 == N_EXPERTS - 1, end_b, 0.0), axis=-1, keepdims=True)
    used_ref[...] = jnp.broadcast_to(total, (SUBLANES, LANES)).astype(I32)


def _slots(cnt, e4, r4, triu, n_blk):
    T = e4.shape[0]
    tm = TOKEN_TILE
    tok = pl.BlockSpec((tm, LANES), lambda i: (i, 0))
    sd = jax.ShapeDtypeStruct
    return pl.pallas_call(
        _slots_kernel, grid=(T // tm,),
        in_specs=[_full_spec((SUBLANES, LANES)), tok, tok, _full_spec(triu.shape)],
        out_specs=[tok, _full_spec((n_blk, LANES)), _full_spec((SUBLANES, LANES))],
        out_shape=[sd((T, LANES), I32), sd((n_blk, LANES), I32), sd((SUBLANES, LANES), I32)],
        compiler_params=_cparams("arbitrary"), name="slots",
    )(cnt, e4, r4, triu)


def _dispatch_kernel(dest_ref, h_ref, xs_in_ref, xs_ref, sem):
    del xs_in_ref
    tm = h_ref.shape[0]

    def row_copy(r, slot):
        return pltpu.make_async_copy(h_ref.at[pl.ds(r, 1)], xs_ref.at[pl.ds(slot, 1)], sem)

    def issue(r, c):
        for k in range(TOP_K):
            row_copy(r, dest_ref[0, 0, r * TOP_K + k]).start()
        return c

    def drain(r, c):
        for k in range(TOP_K):
            row_copy(0, 0).wait()
        return c

    lax.fori_loop(0, tm, issue, 0)
    lax.fori_loop(0, tm, drain, 0)


def _dispatch(dest3, h1, xs_zero):
    T, D = h1.shape
    tm = TOKEN_TILE
    return pl.pallas_call(
        _dispatch_kernel, grid=(T // tm,),
        in_specs=[pl.BlockSpec((1, 1, tm * TOP_K), lambda i: (i, 0, 0), memory_space=pltpu.SMEM),
                  pl.BlockSpec((tm, D), lambda i: (i, 0)),
                  pl.BlockSpec(memory_space=pl.ANY)],
        out_specs=pl.BlockSpec(memory_space=pl.ANY),
        out_shape=jax.ShapeDtypeStruct(xs_zero.shape, xs_zero.dtype),
        scratch_shapes=[pltpu.SemaphoreType.DMA(())],
        input_output_aliases={2: 0},
        compiler_params=_cparams("arbitrary"), name="dispatch",
    )(dest3, h1, xs_zero)


def _expert_kernel(blk_ref, used_ref, xs_ref, wgu_ref, bgu_ref, wd_ref, bd_ref, ys_ref):
    @pl.when(pl.program_id(0) < used_ref[0])
    def _():
        gu = _dot(xs_ref[...].astype(BF16), wgu_ref[0].astype(BF16)) + bgu_ref[0]
        g = jnp.minimum(gu[:, :D_FF], SWIGLU_LIMIT)
        lin = jnp.clip(gu[:, D_FF:], -SWIGLU_LIMIT, SWIGLU_LIMIT)
        act = g * jax.nn.sigmoid(SWIGLU_ALPHA * g) * (lin + 1.0)
        ys_ref[...] = _dot(act.astype(BF16), wd_ref[0].astype(BF16)) + bd_ref[0]

    @pl.when(pl.program_id(0) >= used_ref[0])
    def _():
        ys_ref[...] = jnp.zeros_like(ys_ref)


def _experts(blk_expert, n_used, xs, w_gate_up, b_gate_up, w_down, b_down):
    cap, D = xs.shape
    n_blk = cap // MOE_ROWS
    E = w_gate_up.shape[0]
    live = lambda i, used: jnp.minimum(i, used[0] - 1)
    row = pl.BlockSpec((MOE_ROWS, D), lambda i, blk, used: (live(i, used), 0))
    by_e = lambda shape: pl.BlockSpec((1,) + shape, lambda i, blk, used: (blk[live(i, used)], 0, 0))
    grid_spec = pltpu.PrefetchScalarGridSpec(
        num_scalar_prefetch=2, grid=(n_blk,),
        in_specs=[row, by_e((D, 2 * D_FF)), by_e((1, 2 * D_FF)), by_e((D_FF, D)), by_e((1, D))],
        out_specs=pl.BlockSpec((MOE_ROWS, D), lambda i, blk, used: (i, 0)))
    return pl.pallas_call(
        _expert_kernel, grid_spec=grid_spec, out_shape=jax.ShapeDtypeStruct((cap, D), F32),
        compiler_params=_cparams("arbitrary"), name="experts",
    )(blk_expert, n_used, xs, w_gate_up, b_gate_up.reshape(E, 1, 2 * D_FF), w_down, b_down.reshape(E, 1, D))


def _combine_kernel(dest_ref, w4_ref, h1_ref, g_ref, b_ref, ys_ref, o_ref, buf_ref, sem):
    tm = h1_ref.shape[0]

    def row_copy(r, k, slot):
        return pltpu.make_async_copy(ys_ref.at[pl.ds(slot, 1)], buf_ref.at[k, pl.ds(r, 1)], sem)

    def issue(r, c):
        for k in range(TOP_K):
            row_copy(r, k, dest_ref[0, 0, r * TOP_K + k]).start()
        return c

    def drain(r, c):
        for k in range(TOP_K):
            row_copy(0, 0, 0).wait()
        return c

    lax.fori_loop(0, tm, issue, 0)
    lax.fori_loop(0, tm, drain, 0)
    acc = DEEPNORM_ALPHA * h1_ref[...]
    for k in range(TOP_K):
        acc = acc + buf_ref[k] * w4_ref[:, k:k + 1]
    o_ref[...] = _layer_norm(acc, g_ref[...], b_ref[...])


def _combine(dest3, w4, h1, ln_g, ln_b, ys):
    T, D = h1.shape
    tm = TOKEN_TILE
    return pl.pallas_call(
        _combine_kernel, grid=(T // tm,),
        in_specs=[pl.BlockSpec((1, 1, tm * TOP_K), lambda i: (i, 0, 0), memory_space=pltpu.SMEM),
                  pl.BlockSpec((tm, LANES), lambda i: (i, 0)),
                  pl.BlockSpec((tm, D), lambda i: (i, 0)),
                  _full_spec((1, D)), _full_spec((1, D)),
                  pl.BlockSpec(memory_space=pl.ANY)],
        out_specs=pl.BlockSpec((tm, D), lambda i: (i, 0)),
        out_shape=jax.ShapeDtypeStruct((T, D), F32),
        scratch_shapes=[pltpu.VMEM((TOP_K, tm, D), F32), pltpu.SemaphoreType.DMA(())],
        compiler_params=_cparams("arbitrary"), name="combine",
    )(dest3, w4, h1, ln_g, ln_b, ys)


def _rope_tables(positions):
    inv = ROPE_THETA ** (-jnp.arange(0, ROT_DIM, 2, dtype=F32) / ROT_DIM)
    ang = positions.astype(F32)[..., None] * inv
    cos, sin = jnp.cos(ang), jnp.sin(ang)
    pad = HEAD_DIM - ROT_DIM
    one = jnp.ones(cos.shape[:-1] + (pad,), F32)
    zero = jnp.zeros(cos.shape[:-1] + (pad,), F32)
    zh = jnp.zeros_like(sin)
    cos_t = jnp.concatenate([cos, cos, one], axis=-1)
    sin_a = jnp.concatenate([zh, sin, zero], axis=-1)
    sin_b = jnp.concatenate([-sin, zh, zero], axis=-1)
    two = lambda t: jnp.concatenate([t, t], axis=-1)
    return two(cos_t), two(sin_a), two(sin_b)


def _split_w_in(w_in):
    widths = (NSA_WIDTH,) + (KV_WIDTH,) * 6 + (NSA_HEADS * N_BRANCH, S5_WIDTH, MEM_WIDTH, N_BRANCH * D_MODEL)
    offs = [0]
    for w in widths:
        offs.append(offs[-1] + w)
    col = lambda i: w_in[:, offs[i]:offs[i + 1]]
    wq, kc, vc, ks, vs, kw, vw, wg, wu, wqm, wm = (col(i) for i in range(11))
    wk = jnp.concatenate([kc, ks, kw], axis=1)
    wv = jnp.concatenate([vc, vs, vw], axis=1)
    per_group = NSA_HPG * N_BRANCH
    wg_pad = jnp.zeros((w_in.shape[0], NSA_GROUPS * LANES), w_in.dtype)
    for g in range(NSA_GROUPS):
        wg_pad = wg_pad.at[:, g * LANES:g * LANES + per_group].set(wg[:, g * per_group:(g + 1) * per_group])
    return tuple(w.astype(BF16) for w in (wq, wk, wv, wg_pad, wu, wqm, wm))


def _compress_weights(w1):
    half = CMP_BLOCK // 2
    eye = jnp.eye(NSA_GROUPS, dtype=w1.dtype)

    def arrange(w_half):
        full = jnp.einsum('sdf,gh->sgdhf', w_half, eye)
        return full.reshape(half * NSA_GROUPS * HEAD_DIM, NSA_GROUPS * CMP_HIDDEN).astype(BF16)

    return (w1.reshape(CMP_BLOCK * HEAD_DIM, CMP_HIDDEN).astype(BF16), arrange(w1[:half]), arrange(w1[half:]))


def _s5_weights(a_re, a_im, log_dt, b_re, b_im, c_re, c_im):
    step = jnp.exp(log_dt)[:, None]
    mag = jnp.exp(a_re * step)
    ab_re, ab_im = mag * jnp.cos(a_im * step), mag * jnp.sin(a_im * step)
    den = a_re * a_re + a_im * a_im
    nr = ab_re - 1.0
    coef_re = (nr * a_re + ab_im * a_im) / den
    coef_im = (ab_im * a_re - nr * a_im) / den
    bb_re = coef_re[..., None] * b_re - coef_im[..., None] * b_im
    bb_im = coef_re[..., None] * b_im + coef_im[..., None] * b_re
    eye = jnp.eye(S5_GROUPS, dtype=F32)
    n_state = S5_GROUPS * S5_STATE
    n_tile = n_state // LANES
    in_per = n_tile // (S5_WIDTH // LANES)

    def in_map(bb):
        return jnp.einsum('gnp,gh->gphn', bb, eye).reshape(S5_WIDTH, n_state)

    def out_map(c):
        return jnp.einsum('gpn,gh->gnhp', c, eye).reshape(n_state, S5_WIDTH)

    bf_re, bf_im = in_map(bb_re), in_map(bb_im)
    cf_re, cf_im = out_map(c_re), out_map(-c_im)
    wb, wc = [], []
    for c in range(n_tile):
        i = c // in_per
        rs, cs = slice(i * LANES, (i + 1) * LANES), slice(c * LANES, (c + 1) * LANES)
        wb.append(jnp.concatenate([bf_re[rs, cs], bf_im[rs, cs]], axis=1))
        wc.append(jnp.concatenate([cf_re[cs, rs], cf_im[cs, rs]], axis=0))
    wb = jnp.stack(wb).astype(BF16)
    wc = jnp.stack(wc).astype(BF16)
    return wb, wc, ab_re.reshape(n_tile, LANES), ab_im.reshape(n_tile, LANES)


def _layer(x, mem, positions, ln_emb_g, ln_emb_b, w_in, pe_k, pe_v, w_kcmp1, w_kcmp2, w_vcmp1, w_vcmp2,
           s5_a_re, s5_a_im, s5_log_dt, s5_b_re, s5_b_im, s5_c_re, s5_c_im, s5_d,
           w_s5_glu, w_mem_kv, w_nsa_out, w_mem_out, w_o, ln1_g, ln1_b, w_router, b_router,
           w_gate_up, b_gate_up, w_down, b_down, ln2_g, ln2_b):
    B, L, D = x.shape
    T = B * L
    row = lambda v: v.reshape(1, -1)

    cos_t, sin_a, sin_b = _rope_tables(positions)
    (q_hm, kc, vc, ks, vs, kw, vw, gates, u, qm, gm) = _inproj(
        x, row(ln_emb_g), row(ln_emb_b), cos_t, sin_a, sin_b, *_split_w_in(w_in))

    n_chunk = L // CMP_STRIDE
    chunked = lambda t: t.reshape(B, n_chunk, CMP_STRIDE * KV_WIDTH)
    pe_rows = lambda pe: jnp.broadcast_to(pe.reshape(1, -1), (SUBLANES, CMP_BLOCK * HEAD_DIM)).astype(BF16)
    wk1f, wk1a, wk1b = _compress_weights(w_kcmp1)
    wv1f, wv1a, wv1b = _compress_weights(w_vcmp1)
    ck, cv = _compress(chunked(kc), chunked(vc), pe_rows(pe_k), pe_rows(pe_v), wk1f, wv1f,
                       wk1a, wk1b, wv1a, wv1b, w_kcmp2.astype(BF16), w_vcmp2.astype(BF16))

    per_sb = SEL_BLOCK // CMP_STRIDE
    c_ix = jnp.arange(n_chunk)[:, None]
    n_ix = jnp.arange(L // SEL_BLOCK)[None, :]
    w_score = ((c_ix // per_sb == n_ix).astype(F32) + ((c_ix + 1) // per_sb == n_ix).astype(F32)).astype(BF16)
    o_nsa = _nsa(q_hm, ck, cv, ks, vs, kw, vw, gates, w_score)

    wb, wc, a_re, a_im = _s5_weights(s5_a_re, s5_a_im, s5_log_dt, s5_b_re, s5_b_im, s5_c_re, s5_c_im)
    gy = _s5(u, wb, wc, a_re, a_im, row(s5_d))

    k_mem, v_mem = _memkv(mem, w_mem_kv.astype(BF16))
    o_mem = _memattn(qm, k_mem, v_mem)

    pad_e = LANES - N_EXPERTS
    wr = jnp.pad(w_router, ((0, 0), (0, pad_e)))
    wr_hi = wr.astype(BF16)
    wr_lo = (wr - wr_hi.astype(F32)).astype(BF16)
    br = jnp.concatenate([b_router, jnp.full((pad_e,), -jnp.inf, F32)]).reshape(1, LANES)
    tm = TOKEN_TILE
    tri = (jnp.arange(tm)[None, :] < jnp.arange(tm)[:, None]).astype(BF16)
    flat = lambda t: t.reshape(T, t.shape[-1])
    h1, e4, w4, r4, cnt = _merge(
        flat(x), row(ln_emb_g), row(ln_emb_b), flat(o_nsa), flat(gy), flat(o_mem), flat(gm),
        w_nsa_out.astype(BF16), w_s5_glu.astype(BF16), w_mem_out.astype(BF16), w_o.astype(BF16),
        row(ln1_g), row(ln1_b), wr_hi, wr_lo, br, tri)

    cap = (T * TOP_K + MOE_ROWS - 1) // MOE_ROWS * MOE_ROWS + N_EXPERTS * MOE_ROWS
    n_blk = cap // MOE_ROWS
    triu = (jnp.arange(LANES)[:, None] <= jnp.arange(LANES)[None, :]).astype(BF16)
    dest, blk_owner, used = _slots(cnt, e4, r4, triu, n_blk)
    dest3 = dest[:, :TOP_K].reshape(T // tm, 1, tm * TOP_K)
    blk_expert = blk_owner[:, 0]
    n_used = used[0, :1]

    xs = _dispatch(dest3, h1, jnp.zeros((cap, D), F32))
    ys = _experts(blk_expert, n_used, xs, w_gate_up, b_gate_up, w_down, b_down)
    out = _combine(dest3, w4, h1, row(ln2_g), row(ln2_b), ys)
    return out.reshape(B, L, D)


def kernel(x, mem, positions, ln_emb_g, ln_emb_b, w_in, pe_k_cmp, pe_v_cmp, w_kcmp1, w_kcmp2, w_vcmp1, w_vcmp2, s5_a_re, s5_a_im, s5_log_dt, s5_b_re, s5_b_im, s5_c_re, s5_c_im, s5_d, w_s5_glu, w_mem_kv, w_nsa_out, w_mem_out, w_o, ln1_g, ln1_b, w_router, b_router, w_gate_up, b_gate_up, w_down, b_down, ln2_g, ln2_b):
    assert w_in.shape[0] == DEPTH
    l = 0
    return _layer(x, mem, positions, ln_emb_g, ln_emb_b, w_in[l], pe_k_cmp[l], pe_v_cmp[l], w_kcmp1[l],
                  w_kcmp2[l], w_vcmp1[l], w_vcmp2[l], s5_a_re[l], s5_a_im[l], s5_log_dt[l], s5_b_re[l],
                  s5_b_im[l], s5_c_re[l], s5_c_im[l], s5_d[l], w_s5_glu[l], w_mem_kv[l], w_nsa_out[l],
                  w_mem_out[l], w_o[l], ln1_g[l], ln1_b[l], w_router[l], b_router[l], w_gate_up[l],
                  b_gate_up[l], w_down[l], b_down[l], ln2_g[l], ln2_b[l])
```

```python
import functools
import math

import jax
import jax.numpy as jnp
from jax import lax
from jax.experimental import pallas as pl
from jax.experimental.pallas import tpu as pltpu

F32 = jnp.float32
BF16 = jnp.bfloat16
I32 = jnp.int32

D_MODEL = 1024
NSA_HEADS = 8
NSA_GROUPS = 2
NSA_HPG = NSA_HEADS // NSA_GROUPS
HEAD_DIM = 64
NSA_WIDTH = NSA_HEADS * HEAD_DIM
KV_WIDTH = NSA_GROUPS * HEAD_DIM
CMP_BLOCK = 32
CMP_STRIDE = 16
CMP_HIDDEN = 128
SEL_BLOCK = 64
N_SEL = 16
WINDOW = 512
Q_BLOCK = 128
ROPE_THETA = 500000.0
ROT_DIM = HEAD_DIM // 4
S5_WIDTH = 512
S5_GROUP_DIM = 16
S5_GROUPS = S5_WIDTH // S5_GROUP_DIM
S5_STATE = 64
MEM_HEADS = 4
MEM_HEAD_DIM = 128
MEM_WIDTH = MEM_HEADS * MEM_HEAD_DIM
N_BRANCH = 3
N_EXPERTS = 32
TOP_K = 4
D_FF = 1024
SWIGLU_LIMIT = 7.0
SWIGLU_ALPHA = 1.702
LN_EPS = 1e-5
DEPTH = 1
DEEPNORM_ALPHA = (2 * DEPTH) ** 0.25

LANES = 128
SUBLANES = 8
VMEM_LIMIT_BYTES = 56 * 1024 * 1024

TOKEN_TILE = 256
SEL_KV_TILE = 512
S5_CHUNK = 512
S5_PITCH = S5_CHUNK + 8
MOE_ROWS = 256
NEG_BIG = -(2.0 ** 100)


def _cparams(*sem):
    return pltpu.CompilerParams(dimension_semantics=sem, vmem_limit_bytes=VMEM_LIMIT_BYTES)


def _dot(a, b):
    return jnp.dot(a, b, preferred_element_type=F32)


def _dot_nt(a, b):
    return lax.dot_general(a, b, (((1,), (1,)), ((), ())), preferred_element_type=F32)


def _layer_norm(x, g, b):
    mu = jnp.mean(x, axis=-1, keepdims=True)
    xc = x - mu
    var = jnp.mean(xc * xc, axis=-1, keepdims=True)
    return xc * lax.rsqrt(var + LN_EPS) * g + b


def _gelu_tanh(x):
    cdf = 0.5 * (1.0 + jnp.tanh(math.sqrt(2.0 / math.pi) * (x + 0.044715 * (x * x * x))))
    return x * cdf


def _masked_softmax(s, mask):
    s = jnp.where(mask, s, -jnp.inf)
    m = jnp.max(s, axis=-1, keepdims=True)
    m = jnp.where(m > -jnp.inf, m, 0.0)
    e = jnp.exp(s - m)
    return e / jnp.maximum(jnp.sum(e, axis=-1, keepdims=True), jnp.finfo(F32).tiny)


def _split3(x):
    hi = x.astype(BF16)
    r1 = x - hi.astype(F32)
    mid = r1.astype(BF16)
    lo = (r1 - mid.astype(F32)).astype(BF16)
    return hi, mid, lo


def _full_spec(shape):
    nd = len(shape)
    return pl.BlockSpec(shape, lambda *_: (0,) * nd)


def _inproj_kernel(x_ref, g_ref, b_ref, cos_ref, sa_ref, sb_ref,
                   wq_ref, wk_ref, wv_ref, wg_ref, wu_ref, wqm_ref, wm_ref,
                   q_ref, kc_ref, vc_ref, ks_ref, vs_ref, kw_ref, vw_ref,
                   gate_ref, u_ref, qm_ref, gm_ref):
    h = _layer_norm(x_ref[0], g_ref[...], b_ref[...])
    hb = h.astype(BF16)
    cos_t, sin_a, sin_b = cos_ref[0], sa_ref[0], sb_ref[0]

    def rope(t):
        return (t * cos_t + pltpu.roll(t, ROT_DIM // 2, 1) * sin_a
                + pltpu.roll(t, LANES - ROT_DIM // 2, 1) * sin_b)

    q = _dot(hb, wq_ref[...])
    for c in range(NSA_WIDTH // LANES):
        qc = rope(q[:, c * LANES:(c + 1) * LANES])
        for hh in range(2):
            q_ref[0, 2 * c + hh] = qc[:, hh * HEAD_DIM:(hh + 1) * HEAD_DIM].astype(BF16)
    k3 = _dot(hb, wk_ref[...])
    kc = rope(k3[:, 0:LANES])
    ks = rope(k3[:, LANES:2 * LANES])
    kw = rope(k3[:, 2 * LANES:3 * LANES])
    v3 = _dot(hb, wv_ref[...])
    kc_ref[0] = kc.astype(BF16)
    vc_ref[0] = v3[:, 0:LANES].astype(BF16)
    tm = x_ref.shape[1]
    pos = pl.program_id(1) * tm + lax.broadcasted_iota(I32, (tm, LANES), 0)
    blk_hot = jnp.where(lax.broadcasted_iota(I32, (tm, LANES), 1) == pos // SEL_BLOCK, 1.0, 0.0)
    lane_pad = jnp.zeros((tm, LANES - HEAD_DIM), F32)
    for g in range(NSA_GROUPS):
        sl = slice(g * HEAD_DIM, (g + 1) * HEAD_DIM)
        ks_ref[0, g] = jnp.concatenate([blk_hot, ks[:, sl], lane_pad], axis=1).astype(BF16)
        kw_ref[0, g] = kw[:, sl].astype(BF16)
        vs_ref[0, g] = v3[:, LANES:2 * LANES][:, sl].astype(BF16)
        vw_ref[0, g] = v3[:, 2 * LANES:3 * LANES][:, sl].astype(BF16)
    gate_ref[0] = jax.nn.sigmoid(_dot(hb, wg_ref[...]))
    u_ref[0] = _dot(hb, wu_ref[...])
    qm_ref[0] = _dot(hb, wqm_ref[...]).astype(BF16)
    gm_ref[0] = jax.nn.sigmoid(_dot(hb, wm_ref[...]))


def _inproj(x, ln_g, ln_b, cos_t, sin_a, sin_b, wq, wk, wv, wg, wu, wqm, wm):
    B, L, D = x.shape
    tm = TOKEN_TILE
    grid = (B, L // tm)
    tok = lambda w: pl.BlockSpec((1, tm, w), lambda b, i: (b, i, 0))
    head = lambda n: pl.BlockSpec((1, n, tm, HEAD_DIM), lambda b, i: (b, 0, i, 0))
    in_specs = [tok(D), _full_spec((1, D)), _full_spec((1, D)), tok(LANES), tok(LANES), tok(LANES)]
    in_specs += [_full_spec(w.shape) for w in (wq, wk, wv, wg, wu, wqm, wm)]
    sd = jax.ShapeDtypeStruct
    out_shape = [
        sd((B, NSA_HEADS, L, HEAD_DIM), BF16),
        sd((B, L, KV_WIDTH), BF16), sd((B, L, KV_WIDTH), BF16),
        sd((B, NSA_GROUPS, L, 2 * LANES), BF16), sd((B, NSA_GROUPS, L, HEAD_DIM), BF16),
        sd((B, NSA_GROUPS, L, HEAD_DIM), BF16), sd((B, NSA_GROUPS, L, HEAD_DIM), BF16),
        sd((B, L, NSA_GROUPS * LANES), F32),
        sd((B, L, S5_WIDTH), F32),
        sd((B, L, MEM_WIDTH), BF16),
        sd((B, L, N_BRANCH * D), F32),
    ]
    ksel = pl.BlockSpec((1, NSA_GROUPS, tm, 2 * LANES), lambda b, i: (b, 0, i, 0))
    out_specs = [head(NSA_HEADS), tok(KV_WIDTH), tok(KV_WIDTH), ksel, head(NSA_GROUPS),
                 head(NSA_GROUPS), head(NSA_GROUPS), tok(NSA_GROUPS * LANES), tok(S5_WIDTH),
                 tok(MEM_WIDTH), tok(N_BRANCH * D)]
    return pl.pallas_call(
        _inproj_kernel, grid=grid, in_specs=in_specs, out_specs=out_specs, out_shape=out_shape,
        compiler_params=_cparams("parallel", "parallel"), name="inproj",
    )(x, ln_g, ln_b, cos_t, sin_a, sin_b, wq, wk, wv, wg, wu, wqm, wm)


def _compress_kernel(kc_ref, vc_ref, pek_ref, pev_ref, wk1f_ref, wv1f_ref,
                     wk1a_ref, wk1b_ref, wv1a_ref, wv1b_ref, wk2_ref, wv2_ref, ck_ref, cv_ref):
    n_chunk = kc_ref.shape[1]
    row = lax.broadcasted_iota(I32, (n_chunk, 1), 0)

    def one(x_ref, pe_ref, w1f_ref, w1a_ref, w1b_ref, w2_ref, o_ref):
        x = x_ref[0]
        first = _dot(x, w1a_ref[...])
        second = _dot(x, w1b_ref[...])
        second = pltpu.roll(second, n_chunk - 1, 0)
        pe_term = _dot(pe_ref[...], w1f_ref[...])[0:1]
        pe_term = jnp.concatenate([pe_term] * NSA_GROUPS, axis=1)
        hid = _gelu_tanh(first + second + pe_term).astype(BF16)
        for g in range(NSA_GROUPS):
            o = _dot(hid[:, g * CMP_HIDDEN:(g + 1) * CMP_HIDDEN], w2_ref[...])
            o_ref[0, g] = jnp.where(row < n_chunk - 1, o, 0.0).astype(BF16)

    one(kc_ref, pek_ref, wk1f_ref, wk1a_ref, wk1b_ref, wk2_ref, ck_ref)
    one(vc_ref, pev_ref, wv1f_ref, wv1a_ref, wv1b_ref, wv2_ref, cv_ref)


def _compress(kc_r, vc_r, pek, pev, wk1f, wv1f, wk1a, wk1b, wv1a, wv1b, wk2, wv2):
    B, n_chunk, width = kc_r.shape
    blk = pl.BlockSpec((1, n_chunk, width), lambda b: (b, 0, 0))
    out = pl.BlockSpec((1, NSA_GROUPS, n_chunk, HEAD_DIM), lambda b: (b, 0, 0, 0))
    ws = [pek, pev, wk1f, wv1f, wk1a, wk1b, wv1a, wv1b, wk2, wv2]
    sd = jax.ShapeDtypeStruct((B, NSA_GROUPS, n_chunk, HEAD_DIM), BF16)
    return pl.pallas_call(
        _compress_kernel, grid=(B,), in_specs=[blk, blk] + [_full_spec(w.shape) for w in ws],
        out_specs=[out, out], out_shape=[sd, sd], compiler_params=_cparams("parallel"), name="compress",
    )(kc_r, vc_r, *ws)


def _nsa_kernel(q_ref, ck_ref, cv_ref, ks_ref, vs_ref, kw_ref, vw_ref, gate_ref, wsc_ref, o_ref):
    seq_len = ks_ref.shape[2]
    n_cmp = ck_ref.shape[2]
    n_sb = seq_len // SEL_BLOCK
    n_sel = min(N_SEL, n_sb)
    rows = NSA_HPG * Q_BLOCK
    q0 = pl.program_id(2) * Q_BLOCK
    q = q_ref[0].reshape(rows, HEAD_DIM) * (HEAD_DIM ** -0.5)
    t1 = q0 + lax.broadcasted_iota(I32, (Q_BLOCK, 1), 0)
    t4 = jnp.concatenate([t1] * NSA_HPG, axis=0)

    s = _dot_nt(q, ck_ref[0, 0])
    c_end = lax.broadcasted_iota(I32, (1, n_cmp), 1) * CMP_STRIDE + (CMP_BLOCK - 1)
    p_cmp = _masked_softmax(s, c_end <= t4)
    o_cmp = _dot(p_cmp.astype(BF16), cv_ref[0, 0])

    imp = p_cmp[0:Q_BLOCK]
    for hh in range(1, NSA_HPG):
        imp = imp + p_cmp[hh * Q_BLOCK:(hh + 1) * Q_BLOCK]
    w_sc = wsc_ref[...]
    score = sum(_dot(part, w_sc) for part in _split3(imp))
    score_t = score.T
    jb = lax.broadcasted_iota(I32, (n_sb, Q_BLOCK), 0)
    tb = (q0 + lax.broadcasted_iota(I32, (1, Q_BLOCK), 1)) // SEL_BLOCK
    forced = (jb == 0) | (jb == tb) | (jb == tb - 1)
    work = jnp.where(forced | (jb > tb), -jnp.inf, score_t)
    bias_t = jnp.where(forced, 0.0, NEG_BIG)
    jbf = jb.astype(F32)
    for _ in range(n_sel - 3):
        m = jnp.max(work, axis=0, keepdims=True)
        idx = jnp.min(jnp.where(work == m, jbf, float(n_sb)), axis=0, keepdims=True)
        pick = jbf == idx
        bias_t = jnp.where(pick, 0.0, bias_t)
        work = jnp.where(pick, -jnp.inf, work)
    sel_bias = bias_t.T
    if n_sb < LANES:
        sel_bias = jnp.concatenate([sel_bias, jnp.zeros((Q_BLOCK, LANES - n_sb), F32)], axis=1)

    span = WINDOW + Q_BLOCK
    w0 = pl.multiple_of(jnp.maximum(q0 - WINDOW, 0), Q_BLOCK)
    kwin = kw_ref[0, 0, pl.ds(w0, span), :]
    vwin = vw_ref[0, 0, pl.ds(w0, span), :]
    s = _dot_nt(q, kwin)
    diff = t4 - (w0 + lax.broadcasted_iota(I32, (1, span), 1))
    p = _masked_softmax(s, (diff >= 0) & (diff < WINDOW))
    o_win = _dot(p.astype(BF16), vwin)

    tk = SEL_KV_TILE
    q_aug = jnp.concatenate([jnp.concatenate([sel_bias.astype(BF16)] * NSA_HPG, axis=0), q,
                             jnp.zeros((rows, LANES - HEAD_DIM), BF16)], axis=1)

    def sel_tile(j, carry, causal):
        m_run, l_run, acc = carry
        k0 = pl.multiple_of(j * tk, tk)
        sc = _dot_nt(q_aug, ks_ref[0, 0, pl.ds(k0, tk), :])
        if causal:
            kpos = k0 + lax.broadcasted_iota(I32, (1, tk), 1)
            sc = jnp.where(kpos <= t4, sc, NEG_BIG)
        m_new = jnp.maximum(m_run, jnp.max(sc, axis=-1, keepdims=True))
        alpha = jnp.exp(m_run - m_new)
        p = jnp.exp(sc - m_new)
        l_new = alpha * l_run + jnp.sum(p, axis=-1, keepdims=True)
        acc_new = alpha * acc + _dot(p.astype(BF16), vs_ref[0, 0, pl.ds(k0, tk), :])
        return m_new, l_new, acc_new

    def sel_pair(jj, carry, causal):
        return sel_tile(2 * jj + 1, sel_tile(2 * jj, carry, causal), causal)

    init = (jnp.full((rows, 1), NEG_BIG, F32), jnp.zeros((rows, 1), F32), jnp.zeros((rows, HEAD_DIM), F32))
    last_pair = (q0 // tk) // 2
    carry = lax.fori_loop(0, last_pair, functools.partial(sel_pair, causal=False), init)
    _, l_fin, acc = sel_pair(last_pair, carry, True)
    o_sel = acc / l_fin

    gt = gate_ref[0]
    outs = []
    for hh in range(NSA_HPG):
        sl = slice(hh * Q_BLOCK, (hh + 1) * Q_BLOCK)
        c = hh * N_BRANCH
        outs.append(o_cmp[sl] * gt[:, c:c + 1] + o_sel[sl] * gt[:, c + 1:c + 2] + o_win[sl] * gt[:, c + 2:c + 3])
    o_ref[0] = jnp.concatenate(outs, axis=1).astype(BF16)


def _nsa(q_hm, ck, cv, ks, vs, kw, vw, gates, w_score):
    B, _, L, _ = q_hm.shape
    assert L // SEL_BLOCK <= LANES and (L // SEL_KV_TILE) % 2 == 0 and L >= WINDOW + Q_BLOCK
    n_cmp = ck.shape[2]
    grid = (B, NSA_GROUPS, L // Q_BLOCK)
    qspec = pl.BlockSpec((1, NSA_HPG, Q_BLOCK, HEAD_DIM), lambda b, g, i: (b, g, i, 0))
    cspec = pl.BlockSpec((1, 1, n_cmp, HEAD_DIM), lambda b, g, i: (b, g, 0, 0))
    kvspec = pl.BlockSpec((1, 1, L, HEAD_DIM), lambda b, g, i: (b, g, 0, 0))
    ksspec = pl.BlockSpec((1, 1, L, 2 * LANES), lambda b, g, i: (b, g, 0, 0))
    gspec = pl.BlockSpec((1, Q_BLOCK, LANES), lambda b, g, i: (b, i, g))
    ospec = pl.BlockSpec((1, Q_BLOCK, NSA_HPG * HEAD_DIM), lambda b, g, i: (b, i, g))
    return pl.pallas_call(
        _nsa_kernel, grid=grid,
        in_specs=[qspec, cspec, cspec, ksspec, kvspec, kvspec, kvspec, gspec, _full_spec(w_score.shape)],
        out_specs=ospec, out_shape=jax.ShapeDtypeStruct((B, L, NSA_WIDTH), BF16),
        compiler_params=_cparams("parallel", "parallel", "arbitrary"), name="nsa",
    )(q_hm, ck, cv, ks, vs, kw, vw, gates, w_score)


def _s5_kernel(u_ref, wb_ref, wc_ref, are_ref, aim_ref, d_ref, y_ref, sre_ref, sim_ref, carry_ref):
    n_b, chunk, _ = u_ref.shape
    n_tile = wb_ref.shape[0]
    in_per = n_tile // (S5_WIDTH // LANES)
    pitch = S5_PITCH

    @pl.when(pl.program_id(0) == 0)
    def _():
        carry_ref[...] = jnp.zeros_like(carry_ref)

    for b in range(n_b):
        for c in range(n_tile):
            i = c // in_per
            ub = u_ref[b, :, i * LANES:(i + 1) * LANES].astype(BF16)
            r = _dot(ub, wb_ref[c])
            sre_ref[b, c * pitch:c * pitch + chunk, :] = r[:, :LANES]
            sim_ref[b, c * pitch:c * pitch + chunk, :] = r[:, LANES:]

    a_re, a_im = are_ref[...], aim_ref[...]

    def step(t, carry):
        out = []
        for b in range(n_b):
            s_re, s_im = carry[2 * b], carry[2 * b + 1]
            rows = pl.ds(t, n_tile, stride=pitch)
            n_re = a_re * s_re - a_im * s_im + sre_ref[b, rows, :]
            n_im = a_re * s_im + a_im * s_re + sim_ref[b, rows, :]
            sre_ref[b, rows, :] = n_re
            sim_ref[b, rows, :] = n_im
            out += [n_re, n_im]
        return tuple(out)

    init = tuple(carry_ref[i] for i in range(2 * n_b))
    fin = lax.fori_loop(0, chunk, step, init, unroll=8)
    for i in range(2 * n_b):
        carry_ref[i] = fin[i]

    for b in range(n_b):
        for o in range(S5_WIDTH // LANES):
            acc = jnp.zeros((chunk, LANES), F32)
            for c in range(o * in_per, (o + 1) * in_per):
                rows = slice(c * pitch, c * pitch + chunk)
                acc = acc + _dot(sre_ref[b, rows, :].astype(BF16), wc_ref[c, :LANES])
                acc = acc + _dot(sim_ref[b, rows, :].astype(BF16), wc_ref[c, LANES:])
            lanes = slice(o * LANES, (o + 1) * LANES)
            y = acc + d_ref[:, lanes] * u_ref[b, :, lanes]
            y_ref[b, :, lanes] = _gelu_tanh(y).astype(BF16)


def _s5(u, wb, wc, a_re, a_im, d_skip):
    B, L, W = u.shape
    chunk = S5_CHUNK
    n_tile = wb.shape[0]
    blk = pl.BlockSpec((B, chunk, W), lambda i: (0, i, 0))
    slab = pltpu.VMEM((B, n_tile * S5_PITCH, LANES), F32)
    return pl.pallas_call(
        _s5_kernel, grid=(L // chunk,),
        in_specs=[blk] + [_full_spec(w.shape) for w in (wb, wc, a_re, a_im, d_skip)],
        out_specs=blk, out_shape=jax.ShapeDtypeStruct((B, L, W), BF16),
        scratch_shapes=[slab, slab, pltpu.VMEM((2 * B, n_tile, LANES), F32)],
        compiler_params=_cparams("arbitrary"), name="s5",
    )(u, wb, wc, a_re, a_im, d_skip)


def _memkv_kernel(mem_ref, w_ref, k_ref, v_ref):
    kv = _dot(mem_ref[0].astype(BF16), w_ref[...])
    k_ref[0] = kv[:, :MEM_WIDTH].astype(BF16)
    v_ref[0] = kv[:, MEM_WIDTH:].astype(BF16)


def _memkv(mem, w_kv):
    B, M, D = mem.shape
    out = pl.BlockSpec((1, M, MEM_WIDTH), lambda b: (b, 0, 0))
    sd = jax.ShapeDtypeStruct((B, M, MEM_WIDTH), BF16)
    return pl.pallas_call(
        _memkv_kernel, grid=(B,),
        in_specs=[pl.BlockSpec((1, M, D), lambda b: (b, 0, 0)), _full_spec(w_kv.shape)],
        out_specs=[out, out], out_shape=[sd, sd], compiler_params=_cparams("parallel"), name="memkv",
    )(mem, w_kv)


def _memattn_kernel(q_ref, k_ref, v_ref, o_ref):
    outs = []
    for h in range(MEM_HEADS):
        sl = slice(h * MEM_HEAD_DIM, (h + 1) * MEM_HEAD_DIM)
        s = _dot_nt(q_ref[0, :, sl], k_ref[0, :, sl]) * (MEM_HEAD_DIM ** -0.5)
        m = jnp.max(s, axis=-1, keepdims=True)
        e = jnp.exp(s - m)
        p = e / jnp.sum(e, axis=-1, keepdims=True)
        outs.append(_dot(p.astype(BF16), v_ref[0, :, sl]))
    o_ref[0] = jnp.concatenate(outs, axis=1).astype(BF16)


def _memattn(qm, k, v):
    B, L, W = qm.shape
    M = k.shape[1]
    tm = TOKEN_TILE
    tok = pl.BlockSpec((1, tm, W), lambda b, i: (b, i, 0))
    kv = pl.BlockSpec((1, M, W), lambda b, i: (b, 0, 0))
    return pl.pallas_call(
        _memattn_kernel, grid=(B, L // tm), in_specs=[tok, kv, kv], out_specs=tok,
        out_shape=jax.ShapeDtypeStruct((B, L, W), BF16),
        compiler_params=_cparams("parallel", "parallel"), name="memattn",
    )(qm, k, v)


def _merge_kernel(x_ref, lng_ref, lnb_ref, on_ref, gy_ref, om_ref, gm_ref,
                  wn_ref, wglu_ref, wmo_ref, wo_ref, l1g_ref, l1b_ref,
                  wrh_ref, wrl_ref, br_ref, tri_ref,
                  h1_ref, e4_ref, w4_ref, r4_ref, cnt_ref):
    D = x_ref.shape[1]
    tm = x_ref.shape[0]
    h =_layer_norm(x_ref[...], lng_ref[...], lnb_ref[...])
    y_nsa = _dot(on_ref[...], wn_ref[...])
    glu = _dot(gy_ref[...], wglu_ref[...])
    y_s5 = glu[:, :D] * jax.nn.sigmoid(glu[:, D:])
    y_mem = _dot(om_ref[...], wmo_ref[...])
    merged = gm_ref[:, 0:D] * y_nsa + gm_ref[:, D:2 * D] * y_s5 + gm_ref[:, 2 * D:3 * D] * y_mem
    mix = _dot(merged.astype(BF16), wo_ref[...])
    h1 = _layer_norm(DEEPNORM_ALPHA * h + mix, l1g_ref[...], l1b_ref[...])
    h1_ref[...] = h1

    hh = h1.astype(BF16)
    hl = (h1 - hh.astype(F32)).astype(BF16)
    logits = _dot(hh, wrh_ref[...]) + _dot(hh, wrl_ref[...]) + _dot(hl, wrh_ref[...]) + br_ref[...]
    lane = lax.broadcasted_iota(I32, (tm, LANES), 1)
    lane_f = lane.astype(F32)
    work = logits
    multi = jnp.zeros((tm, LANES), F32)
    vals, picks = [], []
    for _ in range(TOP_K):
        m = jnp.max(work, axis=-1, keepdims=True)
        idx = jnp.min(jnp.where(work == m, lane_f, float(LANES)), axis=-1, keepdims=True)
        pick = lane_f == idx
        vals.append(m)
        picks.append((pick, idx))
        multi = jnp.where(pick, 1.0, multi)
        work = jnp.where(pick, -jnp.inf, work)
    es = [jnp.exp(v - vals[0]) for v in vals]
    den = es[0] + es[1] + es[2] + es[3]
    rank = _dot(tri_ref[...], multi.astype(BF16))
    cnt_ref[0] = jnp.broadcast_to(jnp.sum(multi, axis=0, keepdims=True), (SUBLANES, LANES))
    e4 = jnp.zeros((tm, LANES), F32)
    w4 = jnp.zeros((tm, LANES), F32)
    r4 = jnp.zeros((tm, LANES), F32)
    for k in range(TOP_K):
        pick, idx = picks[k]
        rk = jnp.sum(jnp.where(pick, rank, 0.0), axis=-1, keepdims=True)
        e4 = jnp.where(lane == k, idx, e4)
        w4 = jnp.where(lane == k, es[k] / den, w4)
        r4 = jnp.where(lane == k, rk, r4)
    e4_ref[...] = e4
    w4_ref[...] = w4
    r4_ref[...] = r4


def _merge(x2, lng, lnb, o_nsa, gy, om, gm, wn, wglu, wmo, wo, l1g, l1b, wrh, wrl, br, tri):
    T, D = x2.shape
    tm = TOKEN_TILE
    tok = lambda w: pl.BlockSpec((tm, w), lambda i: (i, 0))
    ws = [wn, wglu, wmo, wo, l1g, l1b, wrh, wrl, br, tri]
    sd = jax.ShapeDtypeStruct
    lane_out = sd((T, LANES), F32)
    return pl.pallas_call(
        _merge_kernel, grid=(T // tm,),
        in_specs=[tok(D), _full_spec((1, D)), _full_spec((1, D)), tok(NSA_WIDTH), tok(S5_WIDTH),
                  tok(MEM_WIDTH), tok(N_BRANCH * D)] + [_full_spec(w.shape) for w in ws],
        out_specs=[tok(D), tok(LANES), tok(LANES), tok(LANES),
                   pl.BlockSpec((1, SUBLANES, LANES), lambda i: (i, 0, 0))],
        out_shape=[sd((T, D), F32), lane_out, lane_out, lane_out, sd((T // tm, SUBLANES, LANES), F32)],
        compiler_params=_cparams("parallel"), name="merge",
    )(x2, lng, lnb, o_nsa, gy, om, gm, *ws)


def _slots_kernel(cnt_ref, e4_ref, r4_ref, triu_ref, striu_ref, tril_ref,
                  lp_ref, seg_ref, blk_ref, misc_ref):
    i = pl.program_id(0)
    tm = e4_ref.shape[0]
    n_tile = cnt_ref.shape[0]
    n_blk = blk_ref.shape[0]
    cnt = cnt_ref[...]
    cnt_b = cnt.astype(BF16)
    total = jnp.sum(cnt, axis=0, keepdims=True)
    nblk_e = jnp.floor((total + (MOE_ROWS - 1)) * (1.0 / MOE_ROWS))
    nblk_8 = jnp.broadcast_to(nblk_e, (SUBLANES, LANES))
    end_b = _dot(nblk_8.astype(BF16), triu_ref[...])
    start_rows = (end_b - nblk_8)[0:1] * MOE_ROWS
    dst = start_rows + _dot(tril_ref[...], cnt_b)
    off = _dot(cnt_b, striu_ref[...])
    seg_ref[0] = cnt.astype(I32)
    seg_ref[1] = off.astype(I32)
    seg_ref[2] = dst.astype(I32)
    tile_row = lax.broadcasted_iota(I32, (n_tile, LANES), 0)
    off_i = jnp.sum(jnp.where(tile_row == i, off, 0.0), axis=0, keepdims=True)
    lane = lax.broadcasted_iota(I32, (tm, LANES), 1)
    lane_f = lane.astype(F32)
    lp = jnp.full((tm, LANES), -1.0, F32)
    for k in range(TOP_K):
        ek = e4_ref[:, k:k + 1]
        base = jnp.sum(jnp.where(lane_f == ek, off_i, 0.0), axis=-1, keepdims=True)
        lp = jnp.where(lane == k, base + r4_ref[:, k:k + 1], lp)
    lp_ref[...] = lp
    blk_i = lax.broadcasted_iota(I32, (n_blk, LANES), 0).astype(F32)
    lane_b = lax.broadcasted_iota(I32, (n_blk, LANES), 1)
    ended = jnp.where((end_b[0:1] <= blk_i) & (lane_b < N_EXPERTS), 1.0, 0.0)
    owner = jnp.minimum(jnp.sum(ended, axis=-1, keepdims=True), float(N_EXPERTS - 1))
    blk_ref[...] = jnp.broadcast_to(owner, (n_blk, LANES)).astype(I32)
    lane8 = lax.broadcasted_iota(I32, (SUBLANES, LANES), 1)
    row8 = lax.broadcasted_iota(I32, (SUBLANES, LANES), 0)
    used = jnp.sum(jnp.where(lane8 == N_EXPERTS - 1, end_b, 0.0), axis=-1, keepdims=True)
    misc = jnp.where(row8 == 0, used, jnp.where(row8 == 1, start_rows + total, nblk_e * MOE_ROWS - total))
    misc_ref[...] = misc.astype(I32)


def _slots(cnt, e4, r4, triu, striu, tril, n_blk):
    T = e4.shape[0]
    tm = TOKEN_TILE
    n_tile = T // tm
    tok = pl.BlockSpec((tm, LANES), lambda i: (i, 0))
    sd = jax.ShapeDtypeStruct
    return pl.pallas_call(
        _slots_kernel, grid=(n_tile,),
        in_specs=[_full_spec((n_tile, LANES)), tok, tok, _full_spec(triu.shape), _full_spec(striu.shape),
                  _full_spec(tril.shape)],
        out_specs=[tok, _full_spec((3, n_tile, LANES)), _full_spec((n_blk, LANES)),
                   _full_spec((SUBLANES, LANES))],
        out_shape=[sd((T, LANES), F32), sd((3, n_tile, LANES), I32), sd((n_blk, LANES), I32),
                   sd((SUBLANES, LANES), I32)],
        compiler_params=_cparams("arbitrary"), name="slots",
    )(cnt, e4, r4, triu, striu, tril)


ROW_TILES = D_MODEL // LANES


def _row_span(row, n_rows):
    start = row * ROW_TILES
    if not isinstance(start, int):
        start = pl.multiple_of(start, ROW_TILES)
    return pl.ds(start, n_rows * ROW_TILES)


def _store_rows(ref, val):
    for c in range(ROW_TILES):
        ref[pl.ds(c, val.shape[0], stride=ROW_TILES), :] = val[:, c * LANES:(c + 1) * LANES]


def _load_row_tile(ref, n_rows, c):
    return ref[pl.ds(c, n_rows, stride=ROW_TILES), :]


def _pieces(count, max_rows, fn):
    p = max_rows
    while p >= 1:
        def piece(p=p):
            fn(count & (-2 * p), p)
        pl.when((count & p) != 0)(piece)
        p //= 2


def _segment_copies(seg_ref, max_rows, make_copy, wait):
    def per_expert(e, c):
        cnt, off, dst = seg_ref[0, 0, 0, e], seg_ref[1, 0, 0, e], seg_ref[2, 0, 0, e]

        def one(first, rows):
            cp = make_copy(off + first, dst + first, rows)
            cp.wait() if wait else cp.start()

        _pieces(cnt, max_rows, one)
        return c

    lax.fori_loop(0, N_EXPERTS, per_expert, 0)


def _dispatch_kernel(seg_ref, segp_ref, misc_ref, lp_ref, h_ref, xs_ref, sorted_ref, zero_ref, sem, pad_sem):
    i = pl.program_id(0)
    n = pl.num_programs(0)
    tm, D = h_ref.shape
    rows = TOP_K * tm
    slot = lax.rem(i, 2)

    def row_copy(slot_):
        def make(src_row, dst_row, n_rows):
            return pltpu.make_async_copy(sorted_ref.at[slot_, _row_span(src_row, n_rows)],
                                         xs_ref.at[_row_span(dst_row, n_rows)], sem.at[slot_])
        return make

    @pl.when(i == 0)
    def _():
        zero_ref[...] = jnp.zeros_like(zero_ref)
        for wait in (False, True):
            def per_expert(e, c, wait=wait):
                def one(first, n_rows):
                    cp = pltpu.make_async_copy(zero_ref.at[_row_span(0, n_rows)],
                                               xs_ref.at[_row_span(misc_ref[1, e] + first, n_rows)], pad_sem)
                    cp.wait() if wait else cp.start()
                _pieces(misc_ref[2, e], MOE_ROWS // 2, one)
                return c
            lax.fori_loop(0, N_EXPERTS, per_expert, 0)

            def per_spare_half_block(hb, c, wait=wait):
                cp = pltpu.make_async_copy(zero_ref, xs_ref.at[_row_span(hb * (MOE_ROWS // 2), MOE_ROWS // 2)],
                                           pad_sem)
                cp.wait() if wait else cp.start()
                return c
            lax.fori_loop(2 * misc_ref[0, 0], 2 * (xs_ref.shape[0] // (MOE_ROWS * ROW_TILES)),
                          per_spare_half_block, 0)

    lp_t = lp_ref[...].T
    s_ix = lax.broadcasted_iota(I32, (rows, 1), 0).astype(F32)
    hit = s_ix == lp_t[0:1, :]
    for k in range(1, TOP_K):
        hit = hit | (s_ix == lp_t[k:k + 1, :])
    perm = jnp.where(hit, 1.0, 0.0).astype(BF16)
    _store_rows(sorted_ref.at[slot], _dot(perm, h_ref[...].astype(BF16)))

    _segment_copies(seg_ref, tm, row_copy(slot), wait=False)

    @pl.when(i > 0)
    def _():
        _segment_copies(segp_ref, tm, row_copy(1 - slot), wait=True)

    @pl.when(i == n - 1)
    def _():
        _segment_copies(seg_ref, tm, row_copy(slot), wait=True)


def _seg_spec(index_map):
    return pl.BlockSpec((3, 1, 1, LANES), index_map, memory_space=pltpu.SMEM)


def _dispatch(seg4, misc, lp, h1, cap):
    T, D = h1.shape
    assert D == ROW_TILES * LANES
    tm = TOKEN_TILE
    tok = lambda w: pl.BlockSpec((tm, w), lambda i: (i, 0))
    return pl.pallas_call(
        _dispatch_kernel, grid=(T // tm,),
        in_specs=[_seg_spec(lambda i: (0, i, 0, 0)), _seg_spec(lambda i: (0, jnp.maximum(i - 1, 0), 0, 0)),
                  pl.BlockSpec(memory_space=pltpu.SMEM), tok(LANES), tok(D)],
        out_specs=pl.BlockSpec(memory_space=pl.ANY),
        out_shape=jax.ShapeDtypeStruct((cap * ROW_TILES, LANES), F32),
        scratch_shapes=[pltpu.VMEM((2, TOP_K * tm * ROW_TILES, LANES), F32),
                        pltpu.VMEM((MOE_ROWS // 2 * ROW_TILES, LANES), F32),
                        pltpu.SemaphoreType.DMA((2,)), pltpu.SemaphoreType.DMA(())],
        compiler_params=_cparams("arbitrary"), name="dispatch",
    )(seg4, seg4, misc, lp, h1)


def _expert_kernel(blk_ref, used_ref, xs_ref, wgu_ref, bgu_ref, wd_ref, bd_ref, ys_ref):
    @pl.when(pl.program_id(0) < used_ref[0])
    def _():
        xb = jnp.concatenate([_load_row_tile(xs_ref, MOE_ROWS, c).astype(BF16) for c in range(ROW_TILES)],
                             axis=1)
        gu = _dot(xb, wgu_ref[0].astype(BF16)) + bgu_ref[0]
        g = jnp.minimum(gu[:, :D_FF], SWIGLU_LIMIT)
        lin = jnp.clip(gu[:, D_FF:], -SWIGLU_LIMIT, SWIGLU_LIMIT)
        act = g * jax.nn.sigmoid(SWIGLU_ALPHA * g) * (lin + 1.0)
        _store_rows(ys_ref, _dot(act.astype(BF16), wd_ref[0].astype(BF16)) + bd_ref[0])

    @pl.when(pl.program_id(0) >= used_ref[0])
    def _():
        ys_ref[...] = jnp.zeros_like(ys_ref)


def _experts(blk_expert, n_used, xs, w_gate_up, b_gate_up, w_down, b_down):
    D = w_down.shape[2]
    n_blk = xs.shape[0] // (MOE_ROWS * ROW_TILES)
    E = w_gate_up.shape[0]
    live = lambda i, used: jnp.minimum(i, used[0] - 1)
    row = pl.BlockSpec((MOE_ROWS * ROW_TILES, LANES), lambda i, blk, used: (live(i, used), 0))
    by_e = lambda shape: pl.BlockSpec((1,) + shape, lambda i, blk, used: (blk[live(i, used)], 0, 0))
    grid_spec = pltpu.PrefetchScalarGridSpec(
        num_scalar_prefetch=2, grid=(n_blk,),
        in_specs=[row, by_e((D, 2 * D_FF)), by_e((1, 2 * D_FF)), by_e((D_FF, D)), by_e((1, D))],
        out_specs=pl.BlockSpec((MOE_ROWS * ROW_TILES, LANES), lambda i, blk, used: (i, 0)))
    return pl.pallas_call(
        _expert_kernel, grid_spec=grid_spec, out_shape=jax.ShapeDtypeStruct(xs.shape, F32),
        compiler_params=_cparams("arbitrary"), name="experts",
    )(blk_expert, n_used, xs, w_gate_up, b_gate_up.reshape(E, 1, 2 * D_FF), w_down, b_down.reshape(E, 1, D))


def _combine_kernel(seg_ref, segn_ref, lp_ref, w4_ref, h1_ref, g_ref, b_ref, ys_ref, o_ref, buf_ref, sem):
    i = pl.program_id(0)
    n = pl.num_programs(0)
    tm = h1_ref.shape[0]
    rows = TOP_K * tm
    slot = lax.rem(i, 2)

    def row_copy(slot_):
        def make(buf_row, ys_row, n_rows):
            return pltpu.make_async_copy(ys_ref.at[_row_span(ys_row, n_rows)],
                                         buf_ref.at[slot_, _row_span(buf_row, n_rows)], sem.at[slot_])
        return make

    @pl.when(i == 0)
    def _():
        _segment_copies(seg_ref, tm, row_copy(slot), wait=False)

    @pl.when(i + 1 < n)
    def _():
        _segment_copies(segn_ref, tm, row_copy(1 - slot), wait=False)

    _segment_copies(seg_ref, tm, row_copy(slot), wait=True)

    lp = lp_ref[...]
    s_ix = lax.broadcasted_iota(I32, (1, rows), 1).astype(F32)
    wmat = jnp.zeros((tm, rows), F32)
    for k in range(TOP_K):
        wmat = jnp.where(s_ix == lp[:, k:k + 1], w4_ref[:, k:k + 1], wmat)
    w_hi = wmat.astype(BF16)
    w_lo = (wmat - w_hi.astype(F32)).astype(BF16)
    cols = []
    for c in range(ROW_TILES):
        y = _load_row_tile(buf_ref.at[slot], rows, c)
        y_hi = y.astype(BF16)
        y_lo = (y - y_hi.astype(F32)).astype(BF16)
        cols.append(_dot(w_hi, y_hi) + _dot(w_hi, y_lo) + _dot(w_lo, y_hi))
    acc = DEEPNORM_ALPHA * h1_ref[...] + jnp.concatenate(cols, axis=1)
    o_ref[...] = _layer_norm(acc, g_ref[...], b_ref[...])


def _combine(seg4, lp, w4, h1, ln_g, ln_b, ys):
    T, D = h1.shape
    tm = TOKEN_TILE
    n_tile = T // tm
    return pl.pallas_call(
        _combine_kernel, grid=(n_tile,),
        in_specs=[_seg_spec(lambda i: (0, i, 0, 0)),
                  _seg_spec(lambda i: (0, jnp.minimum(i + 1, n_tile - 1), 0, 0)),
                  pl.BlockSpec((tm, LANES), lambda i: (i, 0)),
                  pl.BlockSpec((tm, LANES), lambda i: (i, 0)),
                  pl.BlockSpec((tm, D), lambda i: (i, 0)),
                  _full_spec((1, D)), _full_spec((1, D)),
                  pl.BlockSpec(memory_space=pl.ANY)],
        out_specs=pl.BlockSpec((tm, D), lambda i: (i, 0)),
        out_shape=jax.ShapeDtypeStruct((T, D), F32),
        scratch_shapes=[pltpu.VMEM((2, TOP_K * tm * ROW_TILES, LANES), F32), pltpu.SemaphoreType.DMA((2,))],
        compiler_params=_cparams("arbitrary"), name="combine",
    )(seg4, seg4, lp, w4, h1, ln_g, ln_b, ys)


def _rope_tables(positions):
    inv = ROPE_THETA ** (-jnp.arange(0, ROT_DIM, 2, dtype=F32) / ROT_DIM)
    ang = positions.astype(F32)[..., None] * inv
    cos, sin = jnp.cos(ang), jnp.sin(ang)
    pad = HEAD_DIM - ROT_DIM
    one = jnp.ones(cos.shape[:-1] + (pad,), F32)
    zero = jnp.zeros(cos.shape[:-1] + (pad,), F32)
    zh = jnp.zeros_like(sin)
    cos_t = jnp.concatenate([cos, cos, one], axis=-1)
    sin_a = jnp.concatenate([zh, sin, zero], axis=-1)
    sin_b = jnp.concatenate([-sin, zh, zero], axis=-1)
    two = lambda t: jnp.concatenate([t, t], axis=-1)
    return two(cos_t), two(sin_a), two(sin_b)


def _split_w_in(w_in):
    widths = (NSA_WIDTH,) + (KV_WIDTH,) * 6 + (NSA_HEADS * N_BRANCH, S5_WIDTH, MEM_WIDTH, N_BRANCH * D_MODEL)
    offs = [0]
    for w in widths:
        offs.append(offs[-1] + w)
    col = lambda i: w_in[:, offs[i]:offs[i + 1]]
    wq, kc, vc, ks, vs, kw, vw, wg, wu, wqm, wm = (col(i) for i in range(11))
    wk = jnp.concatenate([kc, ks, kw], axis=1)
    wv = jnp.concatenate([vc, vs, vw], axis=1)
    per_group = NSA_HPG * N_BRANCH
    wg_pad = jnp.zeros((w_in.shape[0], NSA_GROUPS * LANES), w_in.dtype)
    for g in range(NSA_GROUPS):
        wg_pad = wg_pad.at[:, g * LANES:g * LANES + per_group].set(wg[:, g * per_group:(g + 1) * per_group])
    return tuple(w.astype(BF16) for w in (wq, wk, wv, wg_pad, wu, wqm, wm))


def _compress_weights(w1):
    half = CMP_BLOCK // 2
    eye = jnp.eye(NSA_GROUPS, dtype=w1.dtype)

    def arrange(w_half):
        full = jnp.einsum('sdf,gh->sgdhf', w_half, eye)
        return full.reshape(half * NSA_GROUPS * HEAD_DIM, NSA_GROUPS * CMP_HIDDEN).astype(BF16)

    return (w1.reshape(CMP_BLOCK * HEAD_DIM, CMP_HIDDEN).astype(BF16), arrange(w1[:half]), arrange(w1[half:]))


def _s5_weights(a_re, a_im, log_dt, b_re, b_im, c_re, c_im):
    step = jnp.exp(log_dt)[:, None]
    mag = jnp.exp(a_re * step)
    ab_re, ab_im = mag * jnp.cos(a_im * step), mag * jnp.sin(a_im * step)
    den = a_re * a_re + a_im * a_im
    nr = ab_re - 1.0
    coef_re = (nr * a_re + ab_im * a_im) / den
    coef_im = (ab_im * a_re - nr * a_im) / den
    bb_re = coef_re[..., None] * b_re - coef_im[..., None] * b_im
    bb_im = coef_re[..., None] * b_im + coef_im[..., None] * b_re
    eye = jnp.eye(S5_GROUPS, dtype=F32)
    n_state = S5_GROUPS * S5_STATE
    n_tile = n_state // LANES
    in_per = n_tile // (S5_WIDTH // LANES)

    def in_map(bb):
        return jnp.einsum('gnp,gh->gphn', bb, eye).reshape(S5_WIDTH, n_state)

    def out_map(c):
        return jnp.einsum('gpn,gh->gnhp', c, eye).reshape(n_state, S5_WIDTH)

    bf_re, bf_im = in_map(bb_re), in_map(bb_im)
    cf_re, cf_im = out_map(c_re), out_map(-c_im)
    wb, wc = [], []
    for c in range(n_tile):
        i = c // in_per
        rs, cs = slice(i * LANES, (i + 1) * LANES), slice(c * LANES, (c + 1) * LANES)
        wb.append(jnp.concatenate([bf_re[rs, cs], bf_im[rs, cs]], axis=1))
        wc.append(jnp.concatenate([cf_re[cs, rs], cf_im[cs, rs]], axis=0))
    wb = jnp.stack(wb).astype(BF16)
    wc = jnp.stack(wc).astype(BF16)
    return wb, wc, ab_re.reshape(n_tile, LANES), ab_im.reshape(n_tile, LANES)


def _layer(x, mem, positions, ln_emb_g, ln_emb_b, w_in, pe_k, pe_v, w_kcmp1, w_kcmp2, w_vcmp1, w_vcmp2,
           s5_a_re, s5_a_im, s5_log_dt, s5_b_re, s5_b_im, s5_c_re, s5_c_im, s5_d,
           w_s5_glu, w_mem_kv, w_nsa_out, w_mem_out, w_o, ln1_g, ln1_b, w_router, b_router,
           w_gate_up, b_gate_up, w_down, b_down, ln2_g, ln2_b):
    B, L, D = x.shape
    T = B * L
    row = lambda v: v.reshape(1, -1)

    cos_t, sin_a, sin_b = _rope_tables(positions)
    (q_hm, kc, vc, ks, vs, kw, vw, gates, u, qm, gm) = _inproj(
        x, row(ln_emb_g), row(ln_emb_b), cos_t, sin_a, sin_b, *_split_w_in(w_in))

    n_chunk = L // CMP_STRIDE
    chunked = lambda t: t.reshape(B, n_chunk, CMP_STRIDE * KV_WIDTH)
    pe_rows = lambda pe: jnp.broadcast_to(pe.reshape(1, -1), (SUBLANES, CMP_BLOCK * HEAD_DIM)).astype(BF16)
    wk1f, wk1a, wk1b = _compress_weights(w_kcmp1)
    wv1f, wv1a, wv1b = _compress_weights(w_vcmp1)
    ck, cv = _compress(chunked(kc), chunked(vc), pe_rows(pe_k), pe_rows(pe_v), wk1f, wv1f,
                       wk1a, wk1b, wv1a, wv1b, w_kcmp2.astype(BF16), w_vcmp2.astype(BF16))

    per_sb = SEL_BLOCK // CMP_STRIDE
    c_ix = jnp.arange(n_chunk)[:, None]
    n_ix = jnp.arange(L // SEL_BLOCK)[None, :]
    w_score = ((c_ix // per_sb == n_ix).astype(F32) + ((c_ix + 1) // per_sb == n_ix).astype(F32)).astype(BF16)
    o_nsa = _nsa(q_hm, ck, cv, ks, vs, kw, vw, gates, w_score)

    wb, wc, a_re, a_im = _s5_weights(s5_a_re, s5_a_im, s5_log_dt, s5_b_re, s5_b_im, s5_c_re, s5_c_im)
    gy = _s5(u, wb, wc, a_re, a_im, row(s5_d))

    k_mem, v_mem = _memkv(mem, w_mem_kv.astype(BF16))
    o_mem = _memattn(qm, k_mem, v_mem)

    pad_e = LANES - N_EXPERTS
    wr = jnp.pad(w_router, ((0, 0), (0, pad_e)))
    wr_hi = wr.astype(BF16)
    wr_lo = (wr - wr_hi.astype(F32)).astype(BF16)
    br = jnp.concatenate([b_router, jnp.full((pad_e,), -jnp.inf, F32)]).reshape(1, LANES)
    tm = TOKEN_TILE
    tri = (jnp.arange(tm)[None, :] < jnp.arange(tm)[:, None]).astype(BF16)
    flat = lambda t: t.reshape(T, t.shape[-1])
    h1, e4, w4, r4, cnt = _merge(
        flat(x), row(ln_emb_g), row(ln_emb_b), flat(o_nsa), flat(gy), flat(o_mem), flat(gm),
        w_nsa_out.astype(BF16), w_s5_glu.astype(BF16), w_mem_out.astype(BF16), w_o.astype(BF16),
        row(ln1_g), row(ln1_b), wr_hi, wr_lo, br, tri)

    cap = (T * TOP_K + MOE_ROWS - 1) // MOE_ROWS * MOE_ROWS + N_EXPERTS * MOE_ROWS
    n_blk = cap // MOE_ROWS
    n_tile = T // tm
    lane_r, lane_c = jnp.arange(LANES)[:, None], jnp.arange(LANES)[None, :]
    triu = (lane_r <= lane_c).astype(BF16)
    striu = (lane_r < lane_c).astype(BF16)
    tril = (jnp.arange(n_tile)[None, :] < jnp.arange(n_tile)[:, None]).astype(BF16)
    lp, seg, blk_owner, misc = _slots(cnt[:, 0, :], e4, r4, triu, striu, tril, n_blk)
    seg4 = seg.reshape(3, n_tile, 1, LANES)
    blk_expert = blk_owner[:, 0]
    n_used = misc[0, :1]

    xs = _dispatch(seg4, misc, lp, h1, cap)
    ys = _experts(blk_expert, n_used, xs, w_gate_up, b_gate_up, w_down, b_down)
    out = _combine(seg4, lp, w4, h1, row(ln2_g), row(ln2_b), ys)
    return out.reshape(B, L, D)


def kernel(x, mem, positions, ln_emb_g, ln_emb_b, w_in, pe_k_cmp, pe_v_cmp, w_kcmp1, w_kcmp2, w_vcmp1, w_vcmp2, s5_a_re, s5_a_im, s5_log_dt, s5_b_re, s5_b_im, s5_c_re, s5_c_im, s5_d, w_s5_glu, w_mem_kv, w_nsa_out, w_mem_out, w_o, ln1_g, ln1_b, w_router, b_router, w_gate_up, b_gate_up, w_down, b_down, ln2_g, ln2_b):
    assert w_in.shape[0] == DEPTH
    l = 0
    return _layer(x, mem, positions, ln_emb_g, ln_emb_b, w_in[l], pe_k_cmp[l], pe_v_cmp[l], w_kcmp1[l],
                  w_kcmp2[l], w_vcmp1[l], w_vcmp2[l], s5_a_re[l], s5_a_im[l], s5_log_dt[l], s5_b_re[l],
                  s5_b_im[l], s5_c_re[l], s5_c_im[l], s5_d[l], w_s5_glu[l], w_mem_kv[l], w_nsa_out[l],
                  w_mem_out[l], w_o[l], ln1_g[l], ln1_b[l], w_router[l], b_router[l], w_gate_up[l],
                  b_gate_up[l], w_down[l], b_down[l], ln2_g[l], ln2_b[l])
```

```python
import functools
import math

import jax
import jax.numpy as jnp
from jax import lax
from jax.experimental import pallas as pl
from jax.experimental.pallas import tpu as pltpu

F32 = jnp.float32
BF16 = jnp.bfloat16
I32 = jnp.int32

D_MODEL = 1024
NSA_HEADS = 8
NSA_GROUPS = 2
NSA_HPG = NSA_HEADS // NSA_GROUPS
HEAD_DIM = 64
NSA_WIDTH = NSA_HEADS * HEAD_DIM
KV_WIDTH = NSA_GROUPS * HEAD_DIM
CMP_BLOCK = 32
CMP_STRIDE = 16
CMP_HIDDEN = 128
SEL_BLOCK = 64
N_SEL = 16
WINDOW = 512
Q_BLOCK = 128
ROPE_THETA = 500000.0
ROT_DIM = HEAD_DIM // 4
S5_WIDTH = 512
S5_GROUP_DIM = 16
S5_GROUPS = S5_WIDTH // S5_GROUP_DIM
S5_STATE = 64
MEM_HEADS = 4
MEM_HEAD_DIM = 128
MEM_WIDTH = MEM_HEADS * MEM_HEAD_DIM
N_BRANCH = 3
N_EXPERTS = 32
TOP_K = 4
D_FF = 1024
SWIGLU_LIMIT = 7.0
SWIGLU_ALPHA = 1.702
LN_EPS = 1e-5
DEPTH = 1
DEEPNORM_ALPHA = (2 * DEPTH) ** 0.25

LANES = 128
SUBLANES = 8
VMEM_LIMIT_BYTES = 56 * 1024 * 1024

TOKEN_TILE = 256
SEL_KV_TILE = 512
S5_CHUNK = 512
S5_PITCH = S5_CHUNK + 8
MOE_ROWS = 256
NEG_BIG = -(2.0 ** 100)
Q_SCALE_LOG2 = HEAD_DIM ** -0.5 * math.log2(math.e)


def _cparams(*sem):
    return pltpu.CompilerParams(dimension_semantics=sem, vmem_limit_bytes=VMEM_LIMIT_BYTES)


def _dot(a, b):
    return jnp.dot(a, b, preferred_element_type=F32)


def _dot_nt(a, b):
    return lax.dot_general(a, b, (((1,), (1,)), ((), ())), preferred_element_type=F32)


def _layer_norm(x, g, b):
    mu = jnp.mean(x, axis=-1, keepdims=True)
    xc = x - mu
    var = jnp.mean(xc * xc, axis=-1, keepdims=True)
    return xc * lax.rsqrt(var + LN_EPS) * g + b


def _gelu_tanh(x):
    cdf = 0.5 * (1.0 + jnp.tanh(math.sqrt(2.0 / math.pi) * (x + 0.044715 * (x * x * x))))
    return x * cdf


def _masked_exp2(s, mask):
    s = jnp.where(mask, s, -jnp.inf)
    m = jnp.max(s, axis=-1, keepdims=True)
    m = jnp.where(m > -jnp.inf, m, 0.0)
    return jnp.exp2(s - m)


def _safe_recip(denom):
    return 1.0 / jnp.maximum(denom, jnp.finfo(F32).tiny)


def _split3(x):
    hi = x.astype(BF16)
    r1 = x - hi.astype(F32)
    mid = r1.astype(BF16)
    lo = (r1 - mid.astype(F32)).astype(BF16)
    return hi, mid, lo


def _full_spec(shape):
    nd = len(shape)
    return pl.BlockSpec(shape, lambda *_: (0,) * nd)


def _inproj_kernel(x_ref, g_ref, b_ref, cos_ref, sa_ref, sb_ref,
                   wq_ref, wk_ref, wv_ref, wg_ref, wu_ref, wqm_ref, wm_ref,
                   q_ref, kc_ref, vc_ref, ks_ref, vs_ref, kw_ref, vw_ref,
                   gate_ref, u_ref, qm_ref, gm_ref):
    h = _layer_norm(x_ref[0], g_ref[...], b_ref[...])
    hb = h.astype(BF16)
    cos_t, sin_a, sin_b = cos_ref[0], sa_ref[0], sb_ref[0]

    def rope(t):
        return (t * cos_t + pltpu.roll(t, ROT_DIM // 2, 1) * sin_a
                + pltpu.roll(t, LANES - ROT_DIM // 2, 1) * sin_b)

    q = _dot(hb, wq_ref[...])
    for c in range(NSA_WIDTH // LANES):
        qc = rope(q[:, c * LANES:(c + 1) * LANES]) * Q_SCALE_LOG2
        for hh in range(2):
            q_ref[0, 2 * c + hh] = qc[:, hh * HEAD_DIM:(hh + 1) * HEAD_DIM].astype(BF16)
    k3 = _dot(hb, wk_ref[...])
    kc = rope(k3[:, 0:LANES])
    ks = rope(k3[:, LANES:2 * LANES])
    kw = rope(k3[:, 2 * LANES:3 * LANES])
    v3 = _dot(hb, wv_ref[...])
    kc_ref[0] = kc.astype(BF16)
    vc_ref[0] = v3[:, 0:LANES].astype(BF16)
    tm = x_ref.shape[1]
    pos = pl.program_id(1) * tm + lax.broadcasted_iota(I32, (tm, LANES), 0)
    blk_hot = jnp.where(lax.broadcasted_iota(I32, (tm, LANES), 1) == pos // SEL_BLOCK, 1.0, 0.0)
    lane_pad = jnp.zeros((tm, LANES - HEAD_DIM), F32)
    ones_pad = jnp.where(lax.broadcasted_iota(I32, (tm, LANES - HEAD_DIM), 1) == 0, 1.0, 0.0)
    for g in range(NSA_GROUPS):
        sl = slice(g * HEAD_DIM, (g + 1) * HEAD_DIM)
        ks_ref[0, g] = jnp.concatenate([blk_hot, ks[:, sl], lane_pad], axis=1).astype(BF16)
        kw_ref[0, g] = kw[:, sl].astype(BF16)
        vs_ref[0, g] = jnp.concatenate([v3[:, LANES:2 * LANES][:, sl], ones_pad], axis=1).astype(BF16)
        vw_ref[0, g] = jnp.concatenate([v3[:, 2 * LANES:3 * LANES][:, sl], ones_pad], axis=1).astype(BF16)
    gate_ref[0] = jax.nn.sigmoid(_dot(hb, wg_ref[...]))
    u_ref[0] = _dot(hb, wu_ref[...])
    qm_ref[0] = _dot(hb, wqm_ref[...]).astype(BF16)
    gm_ref[0] = jax.nn.sigmoid(_dot(hb, wm_ref[...]))


def _inproj(x, ln_g, ln_b, cos_t, sin_a, sin_b, wq, wk, wv, wg, wu, wqm, wm):
    B, L, D = x.shape
    tm = TOKEN_TILE
    grid = (B, L // tm)
    tok = lambda w: pl.BlockSpec((1, tm, w), lambda b, i: (b, i, 0))
    head = lambda n, w=HEAD_DIM: pl.BlockSpec((1, n, tm, w), lambda b, i: (b, 0, i, 0))
    in_specs = [tok(D), _full_spec((1, D)), _full_spec((1, D)), tok(LANES), tok(LANES), tok(LANES)]
    in_specs += [_full_spec(w.shape) for w in (wq, wk, wv, wg, wu, wqm, wm)]
    sd = jax.ShapeDtypeStruct
    out_shape = [
        sd((B, NSA_HEADS, L, HEAD_DIM), BF16),
        sd((B, L, KV_WIDTH), BF16), sd((B, L, KV_WIDTH), BF16),
        sd((B, NSA_GROUPS, L, 2 * LANES), BF16), sd((B, NSA_GROUPS, L, LANES), BF16),
        sd((B, NSA_GROUPS, L, HEAD_DIM), BF16), sd((B, NSA_GROUPS, L, LANES), BF16),
        sd((B, L, NSA_GROUPS * LANES), F32),
        sd((B, L, S5_WIDTH), F32),
        sd((B, L, MEM_WIDTH), BF16),
        sd((B, L, N_BRANCH * D), F32),
    ]
    out_specs = [head(NSA_HEADS), tok(KV_WIDTH), tok(KV_WIDTH), head(NSA_GROUPS, 2 * LANES),
                 head(NSA_GROUPS, LANES), head(NSA_GROUPS), head(NSA_GROUPS, LANES),
                 tok(NSA_GROUPS * LANES), tok(S5_WIDTH),
                 tok(MEM_WIDTH), tok(N_BRANCH * D)]
    return pl.pallas_call(
        _inproj_kernel, grid=grid, in_specs=in_specs, out_specs=out_specs, out_shape=out_shape,
        compiler_params=_cparams("parallel", "parallel"), name="inproj",
    )(x, ln_g, ln_b, cos_t, sin_a, sin_b, wq, wk, wv, wg, wu, wqm, wm)


def _compress_kernel(kc_ref, vc_ref, pek_ref, pev_ref, wk1f_ref, wv1f_ref,
                     wk1a_ref, wk1b_ref, wv1a_ref, wv1b_ref, wk2_ref, wv2_ref, ck_ref, cv_ref):
    n_chunk = kc_ref.shape[1]
    row = lax.broadcasted_iota(I32, (n_chunk, 1), 0)

    def one(x_ref, pe_ref, w1f_ref, w1a_ref, w1b_ref, w2_ref, o_ref):
        x = x_ref[0]
        first = _dot(x, w1a_ref[...])
        second = _dot(x, w1b_ref[...])
        second = pltpu.roll(second, n_chunk - 1, 0)
        pe_term = _dot(pe_ref[...], w1f_ref[...])[0:1]
        pe_term = jnp.concatenate([pe_term] * NSA_GROUPS, axis=1)
        hid = _gelu_tanh(first + second + pe_term).astype(BF16)
        for g in range(NSA_GROUPS):
            o = _dot(hid[:, g * CMP_HIDDEN:(g + 1) * CMP_HIDDEN], w2_ref[...])
            o_ref[0, g] = jnp.where(row < n_chunk - 1, o, 0.0).astype(BF16)

    one(kc_ref, pek_ref, wk1f_ref, wk1a_ref, wk1b_ref, wk2_ref, ck_ref)
    one(vc_ref, pev_ref, wv1f_ref, wv1a_ref, wv1b_ref, wv2_ref, cv_ref)


def _compress(kc_r, vc_r, pek, pev, wk1f, wv1f, wk1a, wk1b, wv1a, wv1b, wk2, wv2):
    B, n_chunk, width = kc_r.shape
    blk = pl.BlockSpec((1, n_chunk, width), lambda b: (b, 0, 0))
    out = pl.BlockSpec((1, NSA_GROUPS, n_chunk, HEAD_DIM), lambda b: (b, 0, 0, 0))
    ws = [pek, pev, wk1f, wv1f, wk1a, wk1b, wv1a, wv1b, wk2, wv2]
    sd = jax.ShapeDtypeStruct((B, NSA_GROUPS, n_chunk, HEAD_DIM), BF16)
    return pl.pallas_call(
        _compress_kernel, grid=(B,), in_specs=[blk, blk] + [_full_spec(w.shape) for w in ws],
        out_specs=[out, out], out_shape=[sd, sd], compiler_params=_cparams("parallel"), name="compress",
    )(kc_r, vc_r, *ws)


def _nsa_kernel(q_ref, ck_ref, cv_ref, ks_ref, vs_ref, kw_ref, vw_ref, gate_ref, wsc_ref, o_ref):
    seq_len = ks_ref.shape[2]
    n_cmp = ck_ref.shape[2]
    n_sb = seq_len // SEL_BLOCK
    n_sel = min(N_SEL, n_sb)
    rows = NSA_HPG * Q_BLOCK
    groups = range(NSA_GROUPS)
    q0 = pl.program_id(1) * Q_BLOCK
    t1 = q0 + lax.broadcasted_iota(I32, (Q_BLOCK, 1), 0)
    t4 = jnp.concatenate([t1] * NSA_HPG, axis=0)
    tk = SEL_KV_TILE

    def front(g):
        q = q_ref[0, g * NSA_HPG:(g + 1) * NSA_HPG].reshape(rows, HEAD_DIM)

        s = _dot_nt(q, ck_ref[0, g])
        c_end = lax.broadcasted_iota(I32, (1, n_cmp), 1) * CMP_STRIDE + (CMP_BLOCK - 1)
        e = _masked_exp2(s, c_end <= t4)
        p_cmp = e * _safe_recip(jnp.sum(e, axis=-1, keepdims=True))
        o_cmp = _dot(p_cmp.astype(BF16), cv_ref[0, g])

        imp = p_cmp[0:Q_BLOCK]
        for hh in range(1, NSA_HPG):
            imp = imp + p_cmp[hh * Q_BLOCK:(hh + 1) * Q_BLOCK]
        w_sc = wsc_ref[...]
        score = sum(_dot(part, w_sc) for part in _split3(imp))
        score_t = score.T
        jb = lax.broadcasted_iota(I32, (n_sb, Q_BLOCK), 0)
        tb = (q0 + lax.broadcasted_iota(I32, (1, Q_BLOCK), 1)) // SEL_BLOCK
        forced = (jb == 0) | (jb == tb) | (jb == tb - 1)
        work = jnp.where(forced | (jb > tb), -jnp.inf, score_t)
        bias_t = jnp.where(forced, 0.0, NEG_BIG)
        jbf = jb.astype(F32)
        for _ in range(n_sel - 3):
            m = jnp.max(work, axis=0, keepdims=True)
            idx = jnp.min(jnp.where(work == m, jbf, float(n_sb)), axis=0, keepdims=True)
            pick = jbf == idx
            bias_t = jnp.where(pick, 0.0, bias_t)
            work = jnp.where(pick, -jnp.inf, work)
        sel_bias = bias_t.T
        if n_sb < LANES:
            sel_bias = jnp.concatenate([sel_bias, jnp.zeros((Q_BLOCK, LANES - n_sb), F32)], axis=1)

        span = WINDOW + Q_BLOCK
        w0 = pl.multiple_of(jnp.maximum(q0 - WINDOW, 0), Q_BLOCK)
        s = _dot_nt(q, kw_ref[0, g, pl.ds(w0, span), :])
        diff = t4 - (w0 + lax.broadcasted_iota(I32, (1, span), 1))
        e = _masked_exp2(s, (diff >= 0) & (diff < WINDOW))
        o_win = _dot(e.astype(BF16), vw_ref[0, g, pl.ds(w0, span), :])
        o_win = o_win[:, :HEAD_DIM] * _safe_recip(o_win[:, HEAD_DIM:HEAD_DIM + 1])

        q_aug = jnp.concatenate([jnp.concatenate([sel_bias.astype(BF16)] * NSA_HPG, axis=0), q,
                                 jnp.zeros((rows, LANES - HEAD_DIM), BF16)], axis=1)
        return q_aug, o_cmp, o_win

    fronts = [front(g) for g in groups]

    def sel_tile(g, j, carry, causal):
        m_run, acc = carry
        k0 = pl.multiple_of(j * tk, tk)
        sc = _dot_nt(fronts[g][0], ks_ref[0, g, pl.ds(k0, tk), :])
        if causal:
            kpos = k0 + lax.broadcasted_iota(I32, (1, tk), 1)
            sc = jnp.where(kpos <= t4, sc, NEG_BIG)
        m_new = jnp.maximum(m_run, jnp.max(sc, axis=-1, keepdims=True))
        p = jnp.exp2(sc - m_new)
        acc_new = jnp.exp2(m_run - m_new) * acc + _dot(p.astype(BF16), vs_ref[0, g, pl.ds(k0, tk), :])
        return m_new, acc_new

    def sel_pair(jj, carries, causal):
        return tuple(sel_tile(g, 2 * jj + 1, sel_tile(g, 2 * jj, carries[g], causal), causal) for g in groups)

    init = tuple((jnp.full((rows, 1), NEG_BIG, F32), jnp.zeros((rows, LANES), F32)) for _ in groups)
    last_pair = (q0 // tk) // 2
    carries = lax.fori_loop(0, last_pair, functools.partial(sel_pair, causal=False), init)
    carries = sel_pair(last_pair, carries, True)

    outs = []
    for g in groups:
        _, o_cmp, o_win = fronts[g]
        acc = carries[g][1]
        o_sel = acc[:, :HEAD_DIM] * (1.0 / acc[:, HEAD_DIM:HEAD_DIM + 1])
        gt = gate_ref[0, :, g * LANES:(g + 1) * LANES]
        for hh in range(NSA_HPG):
            sl = slice(hh * Q_BLOCK, (hh + 1) * Q_BLOCK)
            c = hh * N_BRANCH
            outs.append(o_cmp[sl] * gt[:, c:c + 1] + o_sel[sl] * gt[:, c + 1:c + 2]
                        + o_win[sl] * gt[:, c + 2:c + 3])
    o_ref[0] = jnp.concatenate(outs, axis=1).astype(BF16)


def _nsa(q_hm, ck, cv, ks, vs, kw, vw, gates, w_score):
    B, _, L, _ = q_hm.shape
    assert L // SEL_BLOCK <= LANES and (L // SEL_KV_TILE) % 2 == 0 and L >= WINDOW + Q_BLOCK
    n_cmp = ck.shape[2]
    grid = (B, L // Q_BLOCK)
    qspec = pl.BlockSpec((1, NSA_HEADS, Q_BLOCK, HEAD_DIM), lambda b, i: (b, 0, i, 0))
    cspec = pl.BlockSpec((1, NSA_GROUPS, n_cmp, HEAD_DIM), lambda b, i: (b, 0, 0, 0))
    kvspec = lambda w: pl.BlockSpec((1, NSA_GROUPS, L, w), lambda b, i: (b, 0, 0, 0),
                                    pipeline_mode=pl.Buffered(1))
    gspec = pl.BlockSpec((1, Q_BLOCK, NSA_GROUPS * LANES), lambda b, i: (b, i, 0))
    ospec = pl.BlockSpec((1, Q_BLOCK, NSA_WIDTH), lambda b, i: (b, i, 0))
    return pl.pallas_call(
        _nsa_kernel, grid=grid,
        in_specs=[qspec, cspec, cspec, kvspec(2 * LANES), kvspec(LANES), kvspec(HEAD_DIM), kvspec(LANES), gspec,
                  _full_spec(w_score.shape)],
        out_specs=ospec, out_shape=jax.ShapeDtypeStruct((B, L, NSA_WIDTH), BF16),
        compiler_params=_cparams("parallel", "arbitrary"), name="nsa",
    )(q_hm, ck, cv, ks, vs, kw, vw, gates, w_score)


def _s5_kernel(u_ref, wb_ref, wc_ref, are_ref, aim_ref, d_ref, y_ref, sre_ref, sim_ref, carry_ref):
    n_b, chunk, _ = u_ref.shape
    n_tile = wb_ref.shape[0]
    in_per = n_tile // (S5_WIDTH // LANES)
    pitch = S5_PITCH

    @pl.when(pl.program_id(0) == 0)
    def _():
        carry_ref[...] = jnp.zeros_like(carry_ref)

    for b in range(n_b):
        for c in range(n_tile):
            i = c // in_per
            ub = u_ref[b, :, i * LANES:(i + 1) * LANES].astype(BF16)
            r = _dot(ub, wb_ref[c])
            sre_ref[b, c * pitch:c * pitch + chunk, :] = r[:, :LANES]
            sim_ref[b, c * pitch:c * pitch + chunk, :] = r[:, LANES:]

    a_re, a_im = are_ref[...], aim_ref[...]

    def step(t, carry):
        out = []
        for b in range(n_b):
            s_re, s_im = carry[2 * b], carry[2 * b + 1]
            rows = pl.ds(t, n_tile, stride=pitch)
            n_re = a_re * s_re - a_im * s_im + sre_ref[b, rows, :]
            n_im = a_re * s_im + a_im * s_re + sim_ref[b, rows, :]
            sre_ref[b, rows, :] = n_re
            sim_ref[b, rows, :] = n_im
            out += [n_re, n_im]
        return tuple(out)

    init = tuple(carry_ref[i] for i in range(2 * n_b))
    fin = lax.fori_loop(0, chunk, step, init, unroll=8)
    for i in range(2 * n_b):
        carry_ref[i] = fin[i]

    for b in range(n_b):
        for o in range(S5_WIDTH // LANES):
            acc = jnp.zeros((chunk, LANES), F32)
            for c in range(o * in_per, (o + 1) * in_per):
                rows = slice(c * pitch, c * pitch + chunk)
                acc = acc + _dot(sre_ref[b, rows, :].astype(BF16), wc_ref[c, :LANES])
                acc = acc + _dot(sim_ref[b, rows, :].astype(BF16), wc_ref[c, LANES:])
            lanes = slice(o * LANES, (o + 1) * LANES)
            y = acc + d_ref[:, lanes] * u_ref[b, :, lanes]
            y_ref[b, :, lanes] = _gelu_tanh(y).astype(BF16)


def _s5(u, wb, wc, a_re, a_im, d_skip):
    B, L, W = u.shape
    chunk = S5_CHUNK
    n_tile = wb.shape[0]
    blk = pl.BlockSpec((B, chunk, W), lambda i: (0, i, 0))
    slab = pltpu.VMEM((B, n_tile * S5_PITCH, LANES), F32)
    return pl.pallas_call(
        _s5_kernel, grid=(L // chunk,),
        in_specs=[blk] + [_full_spec(w.shape) for w in (wb, wc, a_re, a_im, d_skip)],
        out_specs=blk, out_shape=jax.ShapeDtypeStruct((B, L, W), BF16),
        scratch_shapes=[slab, slab, pltpu.VMEM((2 * B, n_tile, LANES), F32)],
        compiler_params=_cparams("arbitrary"), name="s5",
    )(u, wb, wc, a_re, a_im, d_skip)


def _memkv_kernel(mem_ref, w_ref, k_ref, v_ref):
    kv = _dot(mem_ref[0].astype(BF16), w_ref[...])
    k_ref[0] = kv[:, :MEM_WIDTH].astype(BF16)
    v_ref[0] = kv[:, MEM_WIDTH:].astype(BF16)


def _memkv(mem, w_kv):
    B, M, D = mem.shape
    out = pl.BlockSpec((1, M, MEM_WIDTH), lambda b: (b, 0, 0))
    sd = jax.ShapeDtypeStruct((B, M, MEM_WIDTH), BF16)
    return pl.pallas_call(
        _memkv_kernel, grid=(B,),
        in_specs=[pl.BlockSpec((1, M, D), lambda b: (b, 0, 0)), _full_spec(w_kv.shape)],
        out_specs=[out, out], out_shape=[sd, sd], compiler_params=_cparams("parallel"), name="memkv",
    )(mem, w_kv)


def _memattn_kernel(q_ref, k_ref, v_ref, o_ref):
    outs = []
    for h in range(MEM_HEADS):
        sl = slice(h * MEM_HEAD_DIM, (h + 1) * MEM_HEAD_DIM)
        s = _dot_nt(q_ref[0, :, sl], k_ref[0, :, sl]) * (MEM_HEAD_DIM ** -0.5)
        m = jnp.max(s, axis=-1, keepdims=True)
        e = jnp.exp(s - m)
        p = e / jnp.sum(e, axis=-1, keepdims=True)
        outs.append(_dot(p.astype(BF16), v_ref[0, :, sl]))
    o_ref[0] = jnp.concatenate(outs, axis=1).astype(BF16)


def _memattn(qm, k, v):
    B, L, W = qm.shape
    M = k.shape[1]
    tm = TOKEN_TILE
    tok = pl.BlockSpec((1, tm, W), lambda b, i: (b, i, 0))
    kv = pl.BlockSpec((1, M, W), lambda b, i: (b, 0, 0))
    return pl.pallas_call(
        _memattn_kernel, grid=(B, L // tm), in_specs=[tok, kv, kv], out_specs=tok,
        out_shape=jax.ShapeDtypeStruct((B, L, W), BF16),
        compiler_params=_cparams("parallel", "parallel"), name="memattn",
    )(qm, k, v)


def _merge_kernel(x_ref, lng_ref, lnb_ref, on_ref, gy_ref, om_ref, gm_ref,
                  wn_ref, wglu_ref, wmo_ref, wo_ref, l1g_ref, l1b_ref,
                  wrh_ref, wrl_ref, br_ref, tri_ref,
                  h1_ref, e4_ref, w4_ref, r4_ref, cnt_ref):
    D = x_ref.shape[1]
    tm = x_ref.shape[0]
    h =_layer_norm(x_ref[...], lng_ref[...], lnb_ref[...])
    y_nsa = _dot(on_ref[...], wn_ref[...])
    glu = _dot(gy_ref[...], wglu_ref[...])
    y_s5 = glu[:, :D] * jax.nn.sigmoid(glu[:, D:])
    y_mem = _dot(om_ref[...], wmo_ref[...])
    merged = gm_ref[:, 0:D] * y_nsa + gm_ref[:, D:2 * D] * y_s5 + gm_ref[:, 2 * D:3 * D] * y_mem
    mix = _dot(merged.astype(BF16), wo_ref[...])
    h1 = _layer_norm(DEEPNORM_ALPHA * h + mix, l1g_ref[...], l1b_ref[...])
    h1_ref[...] = h1

    hh = h1.astype(BF16)
    hl = (h1 - hh.astype(F32)).astype(BF16)
    logits = _dot(hh, wrh_ref[...]) + _dot(hh, wrl_ref[...]) + _dot(hl, wrh_ref[...]) + br_ref[...]
    lane = lax.broadcasted_iota(I32, (tm, LANES), 1)
    lane_f = lane.astype(F32)
    work = logits
    multi = jnp.zeros((tm, LANES), F32)
    vals, picks = [], []
    for _ in range(TOP_K):
        m = jnp.max(work, axis=-1, keepdims=True)
        idx = jnp.min(jnp.where(work == m, lane_f, float(LANES)), axis=-1, keepdims=True)
        pick = lane_f == idx
        vals.append(m)
        picks.append((pick, idx))
        multi = jnp.where(pick, 1.0, multi)
        work = jnp.where(pick, -jnp.inf, work)
    es = [jnp.exp(v - vals[0]) for v in vals]
    den = es[0] + es[1] + es[2] + es[3]
    rank = _dot(tri_ref[...], multi.astype(BF16))
    cnt_ref[0] = jnp.broadcast_to(jnp.sum(multi, axis=0, keepdims=True), (SUBLANES, LANES))
    e4 = jnp.zeros((tm, LANES), F32)
    w4 = jnp.zeros((tm, LANES), F32)
    r4 = jnp.zeros((tm, LANES), F32)
    for k in range(TOP_K):
        pick, idx = picks[k]
        rk = jnp.sum(jnp.where(pick, rank, 0.0), axis=-1, keepdims=True)
        e4 = jnp.where(lane == k, idx, e4)
        w4 = jnp.where(lane == k, es[k] / den, w4)
        r4 = jnp.where(lane == k, rk, r4)
    e4_ref[...] = e4
    w4_ref[...] = w4
    r4_ref[...] = r4


def _merge(x2, lng, lnb, o_nsa, gy, om, gm, wn, wglu, wmo, wo, l1g, l1b, wrh, wrl, br, tri):
    T, D = x2.shape
    tm = TOKEN_TILE
    tok = lambda w: pl.BlockSpec((tm, w), lambda i: (i, 0))
    ws = [wn, wglu, wmo, wo, l1g, l1b, wrh, wrl, br, tri]
    sd = jax.ShapeDtypeStruct
    lane_out = sd((T, LANES), F32)
    return pl.pallas_call(
        _merge_kernel, grid=(T // tm,),
        in_specs=[tok(D), _full_spec((1, D)), _full_spec((1, D)), tok(NSA_WIDTH), tok(S5_WIDTH),
                  tok(MEM_WIDTH), tok(N_BRANCH * D)] + [_full_spec(w.shape) for w in ws],
        out_specs=[tok(D), tok(LANES), tok(LANES), tok(LANES),
                   pl.BlockSpec((1, SUBLANES, LANES), lambda i: (i, 0, 0))],
        out_shape=[sd((T, D), F32), lane_out, lane_out, lane_out, sd((T // tm, SUBLANES, LANES), F32)],
        compiler_params=_cparams("parallel"), name="merge",
    )(x2, lng, lnb, o_nsa, gy, om, gm, *ws)


def _slots_kernel(cnt_ref, e4_ref, r4_ref, triu_ref, striu_ref, tril_ref,
                  lp_ref, seg_ref, blk_ref, misc_ref):
    i = pl.program_id(0)
    tm = e4_ref.shape[0]
    n_tile = cnt_ref.shape[0]
    n_blk = blk_ref.shape[0]
    cnt = cnt_ref[...]
    cnt_b = cnt.astype(BF16)
    total = jnp.sum(cnt, axis=0, keepdims=True)
    nblk_e = jnp.floor((total + (MOE_ROWS - 1)) * (1.0 / MOE_ROWS))
    nblk_8 = jnp.broadcast_to(nblk_e, (SUBLANES, LANES))
    end_b = _dot(nblk_8.astype(BF16), triu_ref[...])
    start_rows = (end_b - nblk_8)[0:1] * MOE_ROWS
    dst = start_rows + _dot(tril_ref[...], cnt_b)
    off = _dot(cnt_b, striu_ref[...])
    seg_ref[0] = cnt.astype(I32)
    seg_ref[1] = off.astype(I32)
    seg_ref[2] = dst.astype(I32)
    tile_row = lax.broadcasted_iota(I32, (n_tile, LANES), 0)
    off_i = jnp.sum(jnp.where(tile_row == i, off, 0.0), axis=0, keepdims=True)
    lane = lax.broadcasted_iota(I32, (tm, LANES), 1)
    lane_f = lane.astype(F32)
    lp = jnp.full((tm, LANES), -1.0, F32)
    for k in range(TOP_K):
        ek = e4_ref[:, k:k + 1]
        base = jnp.sum(jnp.where(lane_f == ek, off_i, 0.0), axis=-1, keepdims=True)
        lp = jnp.where(lane == k, base + r4_ref[:, k:k + 1], lp)
    lp_ref[...] = lp
    blk_i = lax.broadcasted_iota(I32, (n_blk, LANES), 0).astype(F32)
    lane_b = lax.broadcasted_iota(I32, (n_blk, LANES), 1)
    ended = jnp.where((end_b[0:1] <= blk_i) & (lane_b < N_EXPERTS), 1.0, 0.0)
    owner = jnp.minimum(jnp.sum(ended, axis=-1, keepdims=True), float(N_EXPERTS - 1))
    blk_ref[...] = jnp.broadcast_to(owner, (n_blk, LANES)).astype(I32)
    lane8 = lax.broadcasted_iota(I32, (SUBLANES, LANES), 1)
    row8 = lax.broadcasted_iota(I32, (SUBLANES, LANES), 0)
    used = jnp.sum(jnp.where(lane8 == N_EXPERTS - 1, end_b, 0.0), axis=-1, keepdims=True)
    misc = jnp.where(row8 == 0, used, jnp.where(row8 == 1, start_rows + total, nblk_e * MOE_ROWS - total))
    misc_ref[...] = misc.astype(I32)


def _slots(cnt, e4, r4, triu, striu, tril, n_blk):
    T = e4.shape[0]
    tm = TOKEN_TILE
    n_tile = T // tm
    tok = pl.BlockSpec((tm, LANES), lambda i: (i, 0))
    sd = jax.ShapeDtypeStruct
    return pl.pallas_call(
        _slots_kernel, grid=(n_tile,),
        in_specs=[_full_spec((n_tile, LANES)), tok, tok, _full_spec(triu.shape), _full_spec(striu.shape),
                  _full_spec(tril.shape)],
        out_specs=[tok, _full_spec((3, n_tile, LANES)), _full_spec((n_blk, LANES)),
                   _full_spec((SUBLANES, LANES))],
        out_shape=[sd((T, LANES), F32), sd((3, n_tile, LANES), I32), sd((n_blk, LANES), I32),
                   sd((SUBLANES, LANES), I32)],
        compiler_params=_cparams("arbitrary"), name="slots",
    )(cnt, e4, r4, triu, striu, tril)


ROW_TILES = D_MODEL // LANES


def _row_span(row, n_rows):
    start = row * ROW_TILES
    if not isinstance(start, int):
        start = pl.multiple_of(start, ROW_TILES)
    return pl.ds(start, n_rows * ROW_TILES)


def _store_rows(ref, val):
    for c in range(ROW_TILES):
        ref[pl.ds(c, val.shape[0], stride=ROW_TILES), :] = val[:, c * LANES:(c + 1) * LANES]


def _load_row_tile(ref, n_rows, c):
    return ref[pl.ds(c, n_rows, stride=ROW_TILES), :]


def _pieces(count, max_rows, fn):
    p = max_rows
    while p >= 1:
        def piece(p=p):
            fn(count & (-2 * p), p)
        pl.when((count & p) != 0)(piece)
        p //= 2


def _segment_copies(seg_ref, max_rows, make_copy, wait):
    def per_expert(e, c):
        cnt, off, dst = seg_ref[0, 0, 0, e], seg_ref[1, 0, 0, e], seg_ref[2, 0, 0, e]

        def one(first, rows):
            cp = make_copy(off + first, dst + first, rows)
            cp.wait() if wait else cp.start()

        _pieces(cnt, max_rows, one)
        return c

    lax.fori_loop(0, N_EXPERTS, per_expert, 0)


def _dispatch_kernel(seg_ref, segp_ref, misc_ref, lp_ref, h_ref, xs_ref, sorted_ref, zero_ref, sem, pad_sem):
    i = pl.program_id(0)
    n = pl.num_programs(0)
    tm, D = h_ref.shape
    rows = TOP_K * tm
    slot = lax.rem(i, 2)

    def row_copy(slot_):
        def make(src_row, dst_row, n_rows):
            return pltpu.make_async_copy(sorted_ref.at[slot_, _row_span(src_row, n_rows)],
                                         xs_ref.at[_row_span(dst_row, n_rows)], sem.at[slot_])
        return make

    @pl.when(i == 0)
    def _():
        zero_ref[...] = jnp.zeros_like(zero_ref)
        for wait in (False, True):
            def per_expert(e, c, wait=wait):
                def one(first, n_rows):
                    cp = pltpu.make_async_copy(zero_ref.at[_row_span(0, n_rows)],
                                               xs_ref.at[_row_span(misc_ref[1, e] + first, n_rows)], pad_sem)
                    cp.wait() if wait else cp.start()
                _pieces(misc_ref[2, e], MOE_ROWS // 2, one)
                return c
            lax.fori_loop(0, N_EXPERTS, per_expert, 0)

            def per_spare_half_block(hb, c, wait=wait):
                cp = pltpu.make_async_copy(zero_ref, xs_ref.at[_row_span(hb * (MOE_ROWS // 2), MOE_ROWS // 2)],
                                           pad_sem)
                cp.wait() if wait else cp.start()
                return c
            lax.fori_loop(2 * misc_ref[0, 0], 2 * (xs_ref.shape[0] // (MOE_ROWS * ROW_TILES)),
                          per_spare_half_block, 0)

    lp_t = lp_ref[...].T
    s_ix = lax.broadcasted_iota(I32, (rows, 1), 0).astype(F32)
    hit = s_ix == lp_t[0:1, :]
    for k in range(1, TOP_K):
        hit = hit | (s_ix == lp_t[k:k + 1, :])
    perm = jnp.where(hit, 1.0, 0.0).astype(BF16)
    _store_rows(sorted_ref.at[slot], _dot(perm, h_ref[...].astype(BF16)))

    _segment_copies(seg_ref, tm, row_copy(slot), wait=False)

    @pl.when(i > 0)
    def _():
        _segment_copies(segp_ref, tm, row_copy(1 - slot), wait=True)

    @pl.when(i == n - 1)
    def _():
        _segment_copies(seg_ref, tm, row_copy(slot), wait=True)


def _seg_spec(index_map):
    return pl.BlockSpec((3, 1, 1, LANES), index_map, memory_space=pltpu.SMEM)


def _dispatch(seg4, misc, lp, h1, cap):
    T, D = h1.shape
    assert D == ROW_TILES * LANES
    tm = TOKEN_TILE
    tok = lambda w: pl.BlockSpec((tm, w), lambda i: (i, 0))
    return pl.pallas_call(
        _dispatch_kernel, grid=(T // tm,),
        in_specs=[_seg_spec(lambda i: (0, i, 0, 0)), _seg_spec(lambda i: (0, jnp.maximum(i - 1, 0), 0, 0)),
                  pl.BlockSpec(memory_space=pltpu.SMEM), tok(LANES), tok(D)],
        out_specs=pl.BlockSpec(memory_space=pl.ANY),
        out_shape=jax.ShapeDtypeStruct((cap * ROW_TILES, LANES), F32),
        scratch_shapes=[pltpu.VMEM((2, TOP_K * tm * ROW_TILES, LANES), F32),
                        pltpu.VMEM((MOE_ROWS // 2 * ROW_TILES, LANES), F32),
                        pltpu.SemaphoreType.DMA((2,)), pltpu.SemaphoreType.DMA(())],
        compiler_params=_cparams("arbitrary"), name="dispatch",
    )(seg4, seg4, misc, lp, h1)


def _expert_kernel(blk_ref, used_ref, xs_ref, wgu_ref, bgu_ref, wd_ref, bd_ref, ys_ref):
    @pl.when(pl.program_id(0) < used_ref[0])
    def _():
        xb = jnp.concatenate([_load_row_tile(xs_ref, MOE_ROWS, c).astype(BF16) for c in range(ROW_TILES)],
                             axis=1)
        gu = _dot(xb, wgu_ref[0].astype(BF16)) + bgu_ref[0]
        g = jnp.minimum(gu[:, :D_FF], SWIGLU_LIMIT)
        lin = jnp.clip(gu[:, D_FF:], -SWIGLU_LIMIT, SWIGLU_LIMIT)
        act = g * jax.nn.sigmoid(SWIGLU_ALPHA * g) * (lin + 1.0)
        _store_rows(ys_ref, _dot(act.astype(BF16), wd_ref[0].astype(BF16)) + bd_ref[0])

    @pl.when(pl.program_id(0) >= used_ref[0])
    def _():
        ys_ref[...] = jnp.zeros_like(ys_ref)


def _experts(blk_expert, n_used, xs, w_gate_up, b_gate_up, w_down, b_down):
    D = w_down.shape[2]
    n_blk = xs.shape[0] // (MOE_ROWS * ROW_TILES)
    E = w_gate_up.shape[0]
    live = lambda i, used: jnp.minimum(i, used[0] - 1)
    row = pl.BlockSpec((MOE_ROWS * ROW_TILES, LANES), lambda i, blk, used: (live(i, used), 0))
    by_e = lambda shape: pl.BlockSpec((1,) + shape, lambda i, blk, used: (blk[live(i, used)], 0, 0))
    grid_spec = pltpu.PrefetchScalarGridSpec(
        num_scalar_prefetch=2, grid=(n_blk,),
        in_specs=[row, by_e((D, 2 * D_FF)), by_e((1, 2 * D_FF)), by_e((D_FF, D)), by_e((1, D))],
        out_specs=pl.BlockSpec((MOE_ROWS * ROW_TILES, LANES), lambda i, blk, used: (i, 0)))
    return pl.pallas_call(
        _expert_kernel, grid_spec=grid_spec, out_shape=jax.ShapeDtypeStruct(xs.shape, F32),
        compiler_params=_cparams("arbitrary"), name="experts",
    )(blk_expert, n_used, xs, w_gate_up, b_gate_up.reshape(E, 1, 2 * D_FF), w_down, b_down.reshape(E, 1, D))


def _combine_kernel(seg_ref, segn_ref, lp_ref, w4_ref, h1_ref, g_ref, b_ref, ys_ref, o_ref, buf_ref, sem):
    i = pl.program_id(0)
    n = pl.num_programs(0)
    tm = h1_ref.shape[0]
    rows = TOP_K * tm
    slot = lax.rem(i, 2)

    def row_copy(slot_):
        def make(buf_row, ys_row, n_rows):
            return pltpu.make_async_copy(ys_ref.at[_row_span(ys_row, n_rows)],
                                         buf_ref.at[slot_, _row_span(buf_row, n_rows)], sem.at[slot_])
        return make

    @pl.when(i == 0)
    def _():
        _segment_copies(seg_ref, tm, row_copy(slot), wait=False)

    @pl.when(i + 1 < n)
    def _():
        _segment_copies(segn_ref, tm, row_copy(1 - slot), wait=False)

    _segment_copies(seg_ref, tm, row_copy(slot), wait=True)

    lp = lp_ref[...]
    s_ix = lax.broadcasted_iota(I32, (1, rows), 1).astype(F32)
    wmat = jnp.zeros((tm, rows), F32)
    for k in range(TOP_K):
        wmat = jnp.where(s_ix == lp[:, k:k + 1], w4_ref[:, k:k + 1], wmat)
    w_hi = wmat.astype(BF16)
    w_lo = (wmat - w_hi.astype(F32)).astype(BF16)
    cols = []
    for c in range(ROW_TILES):
        y = _load_row_tile(buf_ref.at[slot], rows, c)
        y_hi = y.astype(BF16)
        y_lo = (y - y_hi.astype(F32)).astype(BF16)
        cols.append(_dot(w_hi, y_hi) + _dot(w_hi, y_lo) + _dot(w_lo, y_hi))
    acc = DEEPNORM_ALPHA * h1_ref[...] + jnp.concatenate(cols, axis=1)
    o_ref[...] = _layer_norm(acc, g_ref[...], b_ref[...])


def _combine(seg4, lp, w4, h1, ln_g, ln_b, ys):
    T, D = h1.shape
    tm = TOKEN_TILE
    n_tile = T // tm
    return pl.pallas_call(
        _combine_kernel, grid=(n_tile,),
        in_specs=[_seg_spec(lambda i: (0, i, 0, 0)),
                  _seg_spec(lambda i: (0, jnp.minimum(i + 1, n_tile - 1), 0, 0)),
                  pl.BlockSpec((tm, LANES), lambda i: (i, 0)),
                  pl.BlockSpec((tm, LANES), lambda i: (i, 0)),
                  pl.BlockSpec((tm, D), lambda i: (i, 0)),
                  _full_spec((1, D)), _full_spec((1, D)),
                  pl.BlockSpec(memory_space=pl.ANY)],
        out_specs=pl.BlockSpec((tm, D), lambda i: (i, 0)),
        out_shape=jax.ShapeDtypeStruct((T, D), F32),
        scratch_shapes=[pltpu.VMEM((2, TOP_K * tm * ROW_TILES, LANES), F32), pltpu.SemaphoreType.DMA((2,))],
        compiler_params=_cparams("arbitrary"), name="combine",
    )(seg4, seg4, lp, w4, h1, ln_g, ln_b, ys)


def _rope_tables(positions):
    inv = ROPE_THETA ** (-jnp.arange(0, ROT_DIM, 2, dtype=F32) / ROT_DIM)
    ang = positions.astype(F32)[..., None] * inv
    cos, sin = jnp.cos(ang), jnp.sin(ang)
    pad = HEAD_DIM - ROT_DIM
    one = jnp.ones(cos.shape[:-1] + (pad,), F32)
    zero = jnp.zeros(cos.shape[:-1] + (pad,), F32)
    zh = jnp.zeros_like(sin)
    cos_t = jnp.concatenate([cos, cos, one], axis=-1)
    sin_a = jnp.concatenate([zh, sin, zero], axis=-1)
    sin_b = jnp.concatenate([-sin, zh, zero], axis=-1)
    two = lambda t: jnp.concatenate([t, t], axis=-1)
    return two(cos_t), two(sin_a), two(sin_b)


def _split_w_in(w_in):
    widths = (NSA_WIDTH,) + (KV_WIDTH,) * 6 + (NSA_HEADS * N_BRANCH, S5_WIDTH, MEM_WIDTH, N_BRANCH * D_MODEL)
    offs = [0]
    for w in widths:
        offs.append(offs[-1] + w)
    col = lambda i: w_in[:, offs[i]:offs[i + 1]]
    wq, kc, vc, ks, vs, kw, vw, wg, wu, wqm, wm = (col(i) for i in range(11))
    wk = jnp.concatenate([kc, ks, kw], axis=1)
    wv = jnp.concatenate([vc, vs, vw], axis=1)
    per_group = NSA_HPG * N_BRANCH
    wg_pad = jnp.zeros((w_in.shape[0], NSA_GROUPS * LANES), w_in.dtype)
    for g in range(NSA_GROUPS):
        wg_pad = wg_pad.at[:, g * LANES:g * LANES + per_group].set(wg[:, g * per_group:(g + 1) * per_group])
    return tuple(w.astype(BF16) for w in (wq, wk, wv, wg_pad, wu, wqm, wm))


def _compress_weights(w1):
    half = CMP_BLOCK // 2
    eye = jnp.eye(NSA_GROUPS, dtype=w1.dtype)

    def arrange(w_half):
        full = jnp.einsum('sdf,gh->sgdhf', w_half, eye)
        return full.reshape(half * NSA_GROUPS * HEAD_DIM, NSA_GROUPS * CMP_HIDDEN).astype(BF16)

    return (w1.reshape(CMP_BLOCK * HEAD_DIM, CMP_HIDDEN).astype(BF16), arrange(w1[:half]), arrange(w1[half:]))


def _s5_weights(a_re, a_im, log_dt, b_re, b_im, c_re, c_im):
    step = jnp.exp(log_dt)[:, None]
    mag = jnp.exp(a_re * step)
    ab_re, ab_im = mag * jnp.cos(a_im * step), mag * jnp.sin(a_im * step)
    den = a_re * a_re + a_im * a_im
    nr = ab_re - 1.0
    coef_re = (nr * a_re + ab_im * a_im) / den
    coef_im = (ab_im * a_re - nr * a_im) / den
    bb_re = coef_re[..., None] * b_re - coef_im[..., None] * b_im
    bb_im = coef_re[..., None] * b_im + coef_im[..., None] * b_re
    eye = jnp.eye(S5_GROUPS, dtype=F32)
    n_state = S5_GROUPS * S5_STATE
    n_tile = n_state // LANES
    in_per = n_tile // (S5_WIDTH // LANES)

    def in_map(bb):
        return jnp.einsum('gnp,gh->gphn', bb, eye).reshape(S5_WIDTH, n_state)

    def out_map(c):
        return jnp.einsum('gpn,gh->gnhp', c, eye).reshape(n_state, S5_WIDTH)

    bf_re, bf_im = in_map(bb_re), in_map(bb_im)
    cf_re, cf_im = out_map(c_re), out_map(-c_im)
    wb, wc = [], []
    for c in range(n_tile):
        i = c // in_per
        rs, cs = slice(i * LANES, (i + 1) * LANES), slice(c * LANES, (c + 1) * LANES)
        wb.append(jnp.concatenate([bf_re[rs, cs], bf_im[rs, cs]], axis=1))
        wc.append(jnp.concatenate([cf_re[cs, rs], cf_im[cs, rs]], axis=0))
    wb = jnp.stack(wb).astype(BF16)
    wc = jnp.stack(wc).astype(BF16)
    return wb, wc, ab_re.reshape(n_tile, LANES), ab_im.reshape(n_tile, LANES)


def _layer(x, mem, positions, ln_emb_g, ln_emb_b, w_in, pe_k, pe_v, w_kcmp1, w_kcmp2, w_vcmp1, w_vcmp2,
           s5_a_re, s5_a_im, s5_log_dt, s5_b_re, s5_b_im, s5_c_re, s5_c_im, s5_d,
           w_s5_glu, w_mem_kv, w_nsa_out, w_mem_out, w_o, ln1_g, ln1_b, w_router, b_router,
           w_gate_up, b_gate_up, w_down, b_down, ln2_g, ln2_b):
    B, L, D = x.shape
    T = B * L
    row = lambda v: v.reshape(1, -1)

    cos_t, sin_a, sin_b = _rope_tables(positions)
    (q_hm, kc, vc, ks, vs, kw, vw, gates, u, qm, gm) = _inproj(
        x, row(ln_emb_g), row(ln_emb_b), cos_t, sin_a, sin_b, *_split_w_in(w_in))

    n_chunk = L // CMP_STRIDE
    chunked = lambda t: t.reshape(B, n_chunk, CMP_STRIDE * KV_WIDTH)
    pe_rows = lambda pe: jnp.broadcast_to(pe.reshape(1, -1), (SUBLANES, CMP_BLOCK * HEAD_DIM)).astype(BF16)
    wk1f, wk1a, wk1b = _compress_weights(w_kcmp1)
    wv1f, wv1a, wv1b = _compress_weights(w_vcmp1)
    ck, cv = _compress(chunked(kc), chunked(vc), pe_rows(pe_k), pe_rows(pe_v), wk1f, wv1f,
                       wk1a, wk1b, wv1a, wv1b, w_kcmp2.astype(BF16), w_vcmp2.astype(BF16))

    per_sb = SEL_BLOCK // CMP_STRIDE
    c_ix = jnp.arange(n_chunk)[:, None]
    n_ix = jnp.arange(L // SEL_BLOCK)[None, :]
    w_score = ((c_ix // per_sb == n_ix).astype(F32) + ((c_ix + 1) // per_sb == n_ix).astype(F32)).astype(BF16)
    o_nsa = _nsa(q_hm, ck, cv, ks, vs, kw, vw, gates, w_score)

    wb, wc, a_re, a_im = _s5_weights(s5_a_re, s5_a_im, s5_log_dt, s5_b_re, s5_b_im, s5_c_re, s5_c_im)
    gy = _s5(u, wb, wc, a_re, a_im, row(s5_d))

    k_mem, v_mem = _memkv(mem, w_mem_kv.astype(BF16))
    o_mem = _memattn(qm, k_mem, v_mem)

    pad_e = LANES - N_EXPERTS
    wr = jnp.pad(w_router, ((0, 0), (0, pad_e)))
    wr_hi = wr.astype(BF16)
    wr_lo = (wr - wr_hi.astype(F32)).astype(BF16)
    br = jnp.concatenate([b_router, jnp.full((pad_e,), -jnp.inf, F32)]).reshape(1, LANES)
    tm = TOKEN_TILE
    tri = (jnp.arange(tm)[None, :] < jnp.arange(tm)[:, None]).astype(BF16)
    flat = lambda t: t.reshape(T, t.shape[-1])
    h1, e4, w4, r4, cnt = _merge(
        flat(x), row(ln_emb_g), row(ln_emb_b), flat(o_nsa), flat(gy), flat(o_mem), flat(gm),
        w_nsa_out.astype(BF16), w_s5_glu.astype(BF16), w_mem_out.astype(BF16), w_o.astype(BF16),
        row(ln1_g), row(ln1_b), wr_hi, wr_lo, br, tri)

    cap = (T * TOP_K + MOE_ROWS - 1) // MOE_ROWS * MOE_ROWS + N_EXPERTS * MOE_ROWS
    n_blk = cap // MOE_ROWS
    n_tile = T // tm
    lane_r, lane_c = jnp.arange(LANES)[:, None], jnp.arange(LANES)[None, :]
    triu = (lane_r <= lane_c).astype(BF16)
    striu = (lane_r < lane_c).astype(BF16)
    tril = (jnp.arange(n_tile)[None, :] < jnp.arange(n_tile)[:, None]).astype(BF16)
    lp, seg, blk_owner, misc = _slots(cnt[:, 0, :], e4, r4, triu, striu, tril, n_blk)
    seg4 = seg.reshape(3, n_tile, 1, LANES)
    blk_expert = blk_owner[:, 0]
    n_used = misc[0, :1]

    xs = _dispatch(seg4, misc, lp, h1, cap)
    ys = _experts(blk_expert, n_used, xs, w_gate_up, b_gate_up, w_down, b_down)
    out = _combine(seg4, lp, w4, h1, row(ln2_g), row(ln2_b), ys)
    return out.reshape(B, L, D)


def kernel(x, mem, positions, ln_emb_g, ln_emb_b, w_in, pe_k_cmp, pe_v_cmp, w_kcmp1, w_kcmp2, w_vcmp1, w_vcmp2, s5_a_re, s5_a_im, s5_log_dt, s5_b_re, s5_b_im, s5_c_re, s5_c_im, s5_d, w_s5_glu, w_mem_kv, w_nsa_out, w_mem_out, w_o, ln1_g, ln1_b, w_router, b_router, w_gate_up, b_gate_up, w_down, b_down, ln2_g, ln2_b):
    assert w_in.shape[0] == DEPTH
    l = 0
    return _layer(x, mem, positions, ln_emb_g, ln_emb_b, w_in[l], pe_k_cmp[l], pe_v_cmp[l], w_kcmp1[l],
                  w_kcmp2[l], w_vcmp1[l], w_vcmp2[l], s5_a_re[l], s5_a_im[l], s5_log_dt[l], s5_b_re[l],
                  s5_b_im[l], s5_c_re[l], s5_c_im[l], s5_d[l], w_s5_glu[l], w_mem_kv[l], w_nsa_out[l],
                  w_mem_out[l], w_o[l], ln1_g[l], ln1_b[l], w_router[l], b_router[l], w_gate_up[l],
                  b_gate_up[l], w_down[l], b_down[l], ln2_g[l], ln2_b[l])
```

```python
import functools
import math

import jax
import jax.numpy as jnp
import numpy as np
from jax import lax
from jax.experimental import pallas as pl
from jax.experimental.pallas import tpu as pltpu

F32 = jnp.float32
BF16 = jnp.bfloat16
I32 = jnp.int32

D_MODEL = 1024
NSA_HEADS = 8
NSA_GROUPS = 2
NSA_HPG = NSA_HEADS // NSA_GROUPS
HEAD_DIM = 64
NSA_WIDTH = NSA_HEADS * HEAD_DIM
KV_WIDTH = NSA_GROUPS * HEAD_DIM
CMP_BLOCK = 32
CMP_STRIDE = 16
CMP_HIDDEN = 128
SEL_BLOCK = 64
N_SEL = 16
WINDOW = 512
Q_BLOCK = 128
ROPE_THETA = 500000.0
ROT_DIM = HEAD_DIM // 4
S5_WIDTH = 512
S5_GROUP_DIM = 16
S5_GROUPS = S5_WIDTH // S5_GROUP_DIM
S5_STATE = 64
MEM_HEADS = 4
MEM_HEAD_DIM = 128
MEM_WIDTH = MEM_HEADS * MEM_HEAD_DIM
N_BRANCH = 3
N_EXPERTS = 32
TOP_K = 4
D_FF = 1024
SWIGLU_LIMIT = 7.0
SWIGLU_ALPHA = 1.702
LN_EPS = 1e-5
DEPTH = 1
DEEPNORM_ALPHA = (2 * DEPTH) ** 0.25

LANES = 128
SUBLANES = 8
VMEM_LIMIT_BYTES = 56 * 1024 * 1024

TOKEN_TILE = 256
SEL_KV_TILE = 512
S5_CHUNK = 512
S5_PITCH = S5_CHUNK + 8
MOE_ROWS = 256
NEG_BIG = -(2.0 ** 100)
Q_SCALE_LOG2 = HEAD_DIM ** -0.5 * math.log2(math.e)


def _cparams(*sem):
    return pltpu.CompilerParams(dimension_semantics=sem, vmem_limit_bytes=VMEM_LIMIT_BYTES)


def _dot(a, b):
    return jnp.dot(a, b, preferred_element_type=F32)


def _dot_nt(a, b):
    return lax.dot_general(a, b, (((1,), (1,)), ((), ())), preferred_element_type=F32)


def _layer_norm(x, g, b):
    mu = jnp.mean(x, axis=-1, keepdims=True)
    xc = x - mu
    var = jnp.mean(xc * xc, axis=-1, keepdims=True)
    return xc * lax.rsqrt(var + LN_EPS) * g + b


def _gelu_tanh(x):
    cdf = 0.5 * (1.0 + jnp.tanh(math.sqrt(2.0 / math.pi) * (x + 0.044715 * (x * x * x))))
    return x * cdf


def _masked_exp2(s, mask):
    s = jnp.where(mask, s, -jnp.inf)
    m = jnp.max(s, axis=-1, keepdims=True)
    m = jnp.where(m > -jnp.inf, m, 0.0)
    return jnp.exp2(s - m)


def _safe_recip(denom):
    return 1.0 / jnp.maximum(denom, jnp.finfo(F32).tiny)


def _split3(x):
    hi = x.astype(BF16)
    r1 = x - hi.astype(F32)
    mid = r1.astype(BF16)
    lo = (r1 - mid.astype(F32)).astype(BF16)
    return hi, mid, lo


def _full_spec(shape):
    nd = len(shape)
    return pl.BlockSpec(shape, lambda *_: (0,) * nd)


def _inproj_kernel(x_ref, g_ref, b_ref, cs_ref, spread_ref, unit_ref,
                   wq_ref, wk_ref, wv_ref, wg_ref, wu_ref, wqm_ref, wm_ref,
                   q_ref, kc_ref, vc_ref, ks_ref, vs_ref, kw_ref, vw_ref,
                   gate_ref, u_ref, qm_ref, gm_ref):
    h = _layer_norm(x_ref[0], g_ref[...], b_ref[...])
    hb = h.astype(BF16)
    tab = sum(_dot(part, spread_ref[...]) for part in _split3(cs_ref[0]))
    cos_t = tab[:, 0:LANES] + unit_ref[...]
    sin_a = tab[:, LANES:2 * LANES]
    sin_b = tab[:, 2 * LANES:3 * LANES]

    def rope(t):
        return (t * cos_t + pltpu.roll(t, ROT_DIM // 2, 1) * sin_a
                + pltpu.roll(t, LANES - ROT_DIM // 2, 1) * sin_b)

    q = _dot(hb, wq_ref[...])
    for c in range(NSA_WIDTH // LANES):
        qc = rope(q[:, c * LANES:(c + 1) * LANES]) * Q_SCALE_LOG2
        for hh in range(2):
            q_ref[0, 2 * c + hh] = qc[:, hh * HEAD_DIM:(hh + 1) * HEAD_DIM].astype(BF16)
    k3 = _dot(hb, wk_ref[...])
    kc = rope(k3[:, 0:LANES])
    ks = rope(k3[:, LANES:2 * LANES])
    kw = rope(k3[:, 2 * LANES:3 * LANES])
    v3 = _dot(hb, wv_ref[...])
    kc_ref[0] = kc.astype(BF16)
    vc_ref[0] = v3[:, 0:LANES].astype(BF16)
    tm = x_ref.shape[1]
    pos = pl.program_id(1) * tm + lax.broadcasted_iota(I32, (tm, LANES), 0)
    blk_hot = jnp.where(lax.broadcasted_iota(I32, (tm, LANES), 1) == pos // SEL_BLOCK, 1.0, 0.0)
    lane_pad = jnp.zeros((tm, LANES - HEAD_DIM), F32)
    ones_pad = jnp.where(lax.broadcasted_iota(I32, (tm, LANES - HEAD_DIM), 1) == 0, 1.0, 0.0)
    for g in range(NSA_GROUPS):
        sl = slice(g * HEAD_DIM, (g + 1) * HEAD_DIM)
        ks_ref[0, g] = jnp.concatenate([blk_hot, ks[:, sl], lane_pad], axis=1).astype(BF16)
        kw_ref[0, g] = kw[:, sl].astype(BF16)
        vs_ref[0, g] = jnp.concatenate([v3[:, LANES:2 * LANES][:, sl], ones_pad], axis=1).astype(BF16)
        vw_ref[0, g] = jnp.concatenate([v3[:, 2 * LANES:3 * LANES][:, sl], ones_pad], axis=1).astype(BF16)
    gate_ref[0] = jax.nn.sigmoid(_dot(hb, wg_ref[...]))
    u_ref[0] = _dot(hb, wu_ref[...])
    qm_ref[0] = _dot(hb, wqm_ref[...]).astype(BF16)
    gm_ref[0] = jax.nn.sigmoid(_dot(hb, wm_ref[...]))


def _inproj(x, ln_g, ln_b, cos_sin, spread, unit, wq, wk, wv, wg, wu, wqm, wm):
    B, L, D = x.shape
    tm = TOKEN_TILE
    grid = (B, L // tm)
    tok = lambda w: pl.BlockSpec((1, tm, w), lambda b, i: (b, i, 0))
    head = lambda n, w=HEAD_DIM: pl.BlockSpec((1, n, tm, w), lambda b, i: (b, 0, i, 0))
    in_specs = [tok(D), _full_spec((1, D)), _full_spec((1, D)), tok(ROT_DIM), _full_spec(spread.shape),
                _full_spec(unit.shape)]
    in_specs += [_full_spec(w.shape) for w in (wq, wk, wv, wg, wu, wqm, wm)]
    sd = jax.ShapeDtypeStruct
    out_shape = [
        sd((B, NSA_HEADS, L, HEAD_DIM), BF16),
        sd((B, L, KV_WIDTH), BF16), sd((B, L, KV_WIDTH), BF16),
        sd((B, NSA_GROUPS, L, 2 * LANES), BF16), sd((B, NSA_GROUPS, L, LANES), BF16),
        sd((B, NSA_GROUPS, L, HEAD_DIM), BF16), sd((B, NSA_GROUPS, L, LANES), BF16),
        sd((B, L, NSA_GROUPS * LANES), F32),
        sd((B, L, S5_WIDTH), F32),
        sd((B, L, MEM_WIDTH), BF16),
        sd((B, L, N_BRANCH * D), F32),
    ]
    out_specs = [head(NSA_HEADS), tok(KV_WIDTH), tok(KV_WIDTH), head(NSA_GROUPS, 2 * LANES),
                 head(NSA_GROUPS, LANES), head(NSA_GROUPS), head(NSA_GROUPS, LANES),
                 tok(NSA_GROUPS * LANES), tok(S5_WIDTH),
                 tok(MEM_WIDTH), tok(N_BRANCH * D)]
    return pl.pallas_call(
        _inproj_kernel, grid=grid, in_specs=in_specs, out_specs=out_specs, out_shape=out_shape,
        compiler_params=_cparams("parallel", "parallel"), name="inproj",
    )(x, ln_g, ln_b, cos_sin, spread, unit, wq, wk, wv, wg, wu, wqm, wm)


def _compress_kernel(kc_ref, vc_ref, pek_ref, pev_ref, wk1f_ref, wv1f_ref,
                     wk1a_ref, wk1b_ref, wv1a_ref, wv1b_ref, wk2_ref, wv2_ref, ck_ref, cv_ref):
    n_chunk = kc_ref.shape[1]
    row = lax.broadcasted_iota(I32, (n_chunk, 1), 0)

    def one(x_ref, pe_ref, w1f_ref, w1a_ref, w1b_ref, w2_ref, o_ref):
        x = x_ref[0]
        first = _dot(x, w1a_ref[...])
        second = _dot(x, w1b_ref[...])
        second = pltpu.roll(second, n_chunk - 1, 0)
        pe_term = _dot(pe_ref[...], w1f_ref[...])[0:1]
        pe_term = jnp.concatenate([pe_term] * NSA_GROUPS, axis=1)
        hid = _gelu_tanh(first + second + pe_term).astype(BF16)
        for g in range(NSA_GROUPS):
            o = _dot(hid[:, g * CMP_HIDDEN:(g + 1) * CMP_HIDDEN], w2_ref[...])
            o_ref[0, g] = jnp.where(row < n_chunk - 1, o, 0.0).astype(BF16)

    one(kc_ref, pek_ref, wk1f_ref, wk1a_ref, wk1b_ref, wk2_ref, ck_ref)
    one(vc_ref, pev_ref, wv1f_ref, wv1a_ref, wv1b_ref, wv2_ref, cv_ref)


def _compress(kc_r, vc_r, pek, pev, wk1f, wv1f, wk1a, wk1b, wv1a, wv1b, wk2, wv2):
    B, n_chunk, width = kc_r.shape
    blk = pl.BlockSpec((1, n_chunk, width), lambda b: (b, 0, 0))
    out = pl.BlockSpec((1, NSA_GROUPS, n_chunk, HEAD_DIM), lambda b: (b, 0, 0, 0))
    ws = [pek, pev, wk1f, wv1f, wk1a, wk1b, wv1a, wv1b, wk2, wv2]
    sd = jax.ShapeDtypeStruct((B, NSA_GROUPS, n_chunk, HEAD_DIM), BF16)
    return pl.pallas_call(
        _compress_kernel, grid=(B,), in_specs=[blk, blk] + [_full_spec(w.shape) for w in ws],
        out_specs=[out, out], out_shape=[sd, sd], compiler_params=_cparams("parallel"), name="compress",
    )(kc_r, vc_r, *ws)


def _nsa_kernel(q_ref, ck_ref, cv_ref, ks_ref, vs_ref, kw_ref, vw_ref, gate_ref, wsc_ref, o_ref):
    seq_len = ks_ref.shape[2]
    n_cmp = ck_ref.shape[2]
    n_sb = seq_len // SEL_BLOCK
    n_sel = min(N_SEL, n_sb)
    rows = NSA_HPG * Q_BLOCK
    groups = range(NSA_GROUPS)
    q0 = pl.program_id(1) * Q_BLOCK
    t1 = q0 + lax.broadcasted_iota(I32, (Q_BLOCK, 1), 0)
    t4 = jnp.concatenate([t1] * NSA_HPG, axis=0)
    tk = SEL_KV_TILE

    def front(g):
        q = q_ref[0, g * NSA_HPG:(g + 1) * NSA_HPG].reshape(rows, HEAD_DIM)

        s = _dot_nt(q, ck_ref[0, g])
        c_end = lax.broadcasted_iota(I32, (1, n_cmp), 1) * CMP_STRIDE + (CMP_BLOCK - 1)
        e = _masked_exp2(s, c_end <= t4)
        p_cmp = e * _safe_recip(jnp.sum(e, axis=-1, keepdims=True))
        o_cmp = _dot(p_cmp.astype(BF16), cv_ref[0, g])

        imp = p_cmp[0:Q_BLOCK]
        for hh in range(1, NSA_HPG):
            imp = imp + p_cmp[hh * Q_BLOCK:(hh + 1) * Q_BLOCK]
        w_sc = wsc_ref[...]
        score = sum(_dot(part, w_sc) for part in _split3(imp))
        score_t = score.T
        jb = lax.broadcasted_iota(I32, (n_sb, Q_BLOCK), 0)
        tb = (q0 + lax.broadcasted_iota(I32, (1, Q_BLOCK), 1)) // SEL_BLOCK
        forced = (jb == 0) | (jb == tb) | (jb == tb - 1)
        work = jnp.where(forced | (jb > tb), -jnp.inf, score_t)
        bias_t = jnp.where(forced, 0.0, NEG_BIG)
        jbf = jb.astype(F32)
        for _ in range(n_sel - 3):
            m = jnp.max(work, axis=0, keepdims=True)
            idx = jnp.min(jnp.where(work == m, jbf, float(n_sb)), axis=0, keepdims=True)
            pick = jbf == idx
            bias_t = jnp.where(pick, 0.0, bias_t)
            work = jnp.where(pick, -jnp.inf, work)
        sel_bias = bias_t.T
        if n_sb < LANES:
            sel_bias = jnp.concatenate([sel_bias, jnp.zeros((Q_BLOCK, LANES - n_sb), F32)], axis=1)

        span = WINDOW + Q_BLOCK
        w0 = pl.multiple_of(jnp.maximum(q0 - WINDOW, 0), Q_BLOCK)
        s = _dot_nt(q, kw_ref[0, g, pl.ds(w0, span), :])
        diff = t4 - (w0 + lax.broadcasted_iota(I32, (1, span), 1))
        e = _masked_exp2(s, (diff >= 0) & (diff < WINDOW))
        o_win = _dot(e.astype(BF16), vw_ref[0, g, pl.ds(w0, span), :])
        o_win = o_win[:, :HEAD_DIM] * _safe_recip(o_win[:, HEAD_DIM:HEAD_DIM + 1])

        q_aug = jnp.concatenate([jnp.concatenate([sel_bias.astype(BF16)] * NSA_HPG, axis=0), q,
                                 jnp.zeros((rows, LANES - HEAD_DIM), BF16)], axis=1)
        return q_aug, o_cmp, o_win

    fronts = [front(g) for g in groups]

    def sel_tile(g, j, carry, causal):
        m_run, acc = carry
        k0 = pl.multiple_of(j * tk, tk)
        sc = _dot_nt(fronts[g][0], ks_ref[0, g, pl.ds(k0, tk), :])
        if causal:
            kpos = k0 + lax.broadcasted_iota(I32, (1, tk), 1)
            sc = jnp.where(kpos <= t4, sc, NEG_BIG)
        m_new = jnp.maximum(m_run, jnp.max(sc, axis=-1, keepdims=True))
        p = jnp.exp2(sc - m_new)
        acc_new = jnp.exp2(m_run - m_new) * acc + _dot(p.astype(BF16), vs_ref[0, g, pl.ds(k0, tk), :])
        return m_new, acc_new

    def sel_pair(jj, carries, causal):
        return tuple(sel_tile(g, 2 * jj + 1, sel_tile(g, 2 * jj, carries[g], causal), causal) for g in groups)

    init = tuple((jnp.full((rows, 1), NEG_BIG, F32), jnp.zeros((rows, LANES), F32)) for _ in groups)
    last_pair = (q0 // tk) // 2
    carries = lax.fori_loop(0, last_pair, functools.partial(sel_pair, causal=False), init)
    carries = sel_pair(last_pair, carries, True)

    outs = []
    for g in groups:
        _, o_cmp, o_win = fronts[g]
        acc = carries[g][1]
        o_sel = acc[:, :HEAD_DIM] * (1.0 / acc[:, HEAD_DIM:HEAD_DIM + 1])
        gt = gate_ref[0, :, g * LANES:(g + 1) * LANES]
        for hh in range(NSA_HPG):
            sl = slice(hh * Q_BLOCK, (hh + 1) * Q_BLOCK)
            c = hh * N_BRANCH
            outs.append(o_cmp[sl] * gt[:, c:c + 1] + o_sel[sl] * gt[:, c + 1:c + 2]
                        + o_win[sl] * gt[:, c + 2:c + 3])
    o_ref[0] = jnp.concatenate(outs, axis=1).astype(BF16)


def _nsa(q_hm, ck, cv, ks, vs, kw, vw, gates, w_score):
    B, _, L, _ = q_hm.shape
    assert L // SEL_BLOCK <= LANES and (L // SEL_KV_TILE) % 2 == 0 and L >= WINDOW + Q_BLOCK
    n_cmp = ck.shape[2]
    grid = (B, L // Q_BLOCK)
    qspec = pl.BlockSpec((1, NSA_HEADS, Q_BLOCK, HEAD_DIM), lambda b, i: (b, 0, i, 0))
    cspec = pl.BlockSpec((1, NSA_GROUPS, n_cmp, HEAD_DIM), lambda b, i: (b, 0, 0, 0))
    kvspec = lambda w: pl.BlockSpec((1, NSA_GROUPS, L, w), lambda b, i: (b, 0, 0, 0),
                                    pipeline_mode=pl.Buffered(1))
    gspec = pl.BlockSpec((1, Q_BLOCK, NSA_GROUPS * LANES), lambda b, i: (b, i, 0))
    ospec = pl.BlockSpec((1, Q_BLOCK, NSA_WIDTH), lambda b, i: (b, i, 0))
    return pl.pallas_call(
        _nsa_kernel, grid=grid,
        in_specs=[qspec, cspec, cspec, kvspec(2 * LANES), kvspec(LANES), kvspec(HEAD_DIM), kvspec(LANES), gspec,
                  _full_spec(w_score.shape)],
        out_specs=ospec, out_shape=jax.ShapeDtypeStruct((B, L, NSA_WIDTH), BF16),
        compiler_params=_cparams("parallel", "arbitrary"), name="nsa",
    )(q_hm, ck, cv, ks, vs, kw, vw, gates, w_score)


def _s5_kernel(u_ref, wb_ref, wc_ref, are_ref, aim_ref, d_ref, y_ref, sre_ref, sim_ref, carry_ref):
    n_b, chunk, _ = u_ref.shape
    n_tile = wb_ref.shape[0]
    in_per = n_tile // (S5_WIDTH // LANES)
    pitch = S5_PITCH

    @pl.when(pl.program_id(0) == 0)
    def _():
        carry_ref[...] = jnp.zeros_like(carry_ref)

    for b in range(n_b):
        for c in range(n_tile):
            i = c // in_per
            ub = u_ref[b, :, i * LANES:(i + 1) * LANES].astype(BF16)
            r = _dot(ub, wb_ref[c])
            sre_ref[b, c * pitch:c * pitch + chunk, :] = r[:, :LANES]
            sim_ref[b, c * pitch:c * pitch + chunk, :] = r[:, LANES:]

    a_re, a_im = are_ref[...], aim_ref[...]

    def step(t, carry):
        out = []
        for b in range(n_b):
            s_re, s_im = carry[2 * b], carry[2 * b + 1]
            rows = pl.ds(t, n_tile, stride=pitch)
            n_re = a_re * s_re - a_im * s_im + sre_ref[b, rows, :]
            n_im = a_re * s_im + a_im * s_re + sim_ref[b, rows, :]
            sre_ref[b, rows, :] = n_re
            sim_ref[b, rows, :] = n_im
            out += [n_re, n_im]
        return tuple(out)

    init = tuple(carry_ref[i] for i in range(2 * n_b))
    fin = lax.fori_loop(0, chunk, step, init, unroll=8)
    for i in range(2 * n_b):
        carry_ref[i] = fin[i]

    for b in range(n_b):
        for o in range(S5_WIDTH // LANES):
            acc = jnp.zeros((chunk, LANES), F32)
            for c in range(o * in_per, (o + 1) * in_per):
                rows = slice(c * pitch, c * pitch + chunk)
                acc = acc + _dot(sre_ref[b, rows, :].astype(BF16), wc_ref[c, :LANES])
                acc = acc + _dot(sim_ref[b, rows, :].astype(BF16), wc_ref[c, LANES:])
            lanes = slice(o * LANES, (o + 1) * LANES)
            y = acc + d_ref[:, lanes] * u_ref[b, :, lanes]
            y_ref[b, :, lanes] = _gelu_tanh(y).astype(BF16)


def _s5(u, wb, wc, a_re, a_im, d_skip):
    B, L, W = u.shape
    chunk = S5_CHUNK
    n_tile = wb.shape[0]
    blk = pl.BlockSpec((B, chunk, W), lambda i: (0, i, 0))
    slab = pltpu.VMEM((B, n_tile * S5_PITCH, LANES), F32)
    return pl.pallas_call(
        _s5_kernel, grid=(L // chunk,),
        in_specs=[blk] + [_full_spec(w.shape) for w in (wb, wc, a_re, a_im, d_skip)],
        out_specs=blk, out_shape=jax.ShapeDtypeStruct((B, L, W), BF16),
        scratch_shapes=[slab, slab, pltpu.VMEM((2 * B, n_tile, LANES), F32)],
        compiler_params=_cparams("arbitrary"), name="s5",
    )(u, wb, wc, a_re, a_im, d_skip)


def _memkv_kernel(mem_ref, w_ref, k_ref, v_ref):
    kv = _dot(mem_ref[0].astype(BF16), w_ref[...])
    k_ref[0] = kv[:, :MEM_WIDTH].astype(BF16)
    v_ref[0] = kv[:, MEM_WIDTH:].astype(BF16)


def _memkv(mem, w_kv):
    B, M, D = mem.shape
    out = pl.BlockSpec((1, M, MEM_WIDTH), lambda b: (b, 0, 0))
    sd = jax.ShapeDtypeStruct((B, M, MEM_WIDTH), BF16)
    return pl.pallas_call(
        _memkv_kernel, grid=(B,),
        in_specs=[pl.BlockSpec((1, M, D), lambda b: (b, 0, 0)), _full_spec(w_kv.shape)],
        out_specs=[out, out], out_shape=[sd, sd], compiler_params=_cparams("parallel"), name="memkv",
    )(mem, w_kv)


def _memattn_kernel(q_ref, k_ref, v_ref, o_ref):
    outs = []
    for h in range(MEM_HEADS):
        sl = slice(h * MEM_HEAD_DIM, (h + 1) * MEM_HEAD_DIM)
        s = _dot_nt(q_ref[0, :, sl], k_ref[0, :, sl]) * (MEM_HEAD_DIM ** -0.5)
        m = jnp.max(s, axis=-1, keepdims=True)
        e = jnp.exp(s - m)
        p = e / jnp.sum(e, axis=-1, keepdims=True)
        outs.append(_dot(p.astype(BF16), v_ref[0, :, sl]))
    o_ref[0] = jnp.concatenate(outs, axis=1).astype(BF16)


def _memattn(qm, k, v):
    B, L, W = qm.shape
    M = k.shape[1]
    tm = TOKEN_TILE
    tok = pl.BlockSpec((1, tm, W), lambda b, i: (b, i, 0))
    kv = pl.BlockSpec((1, M, W), lambda b, i: (b, 0, 0))
    return pl.pallas_call(
        _memattn_kernel, grid=(B, L // tm), in_specs=[tok, kv, kv], out_specs=tok,
        out_shape=jax.ShapeDtypeStruct((B, L, W), BF16),
        compiler_params=_cparams("parallel", "parallel"), name="memattn",
    )(qm, k, v)


def _merge_kernel(x_ref, lng_ref, lnb_ref, on_ref, gy_ref, om_ref, gm_ref,
                  wn_ref, wglu_ref, wmo_ref, wo_ref, l1g_ref, l1b_ref,
                  wrh_ref, wrl_ref, br_ref, tri_ref, striu_ref,
                  h1_ref, lp_ref, w4_ref, cnt_ref):
    D = x_ref.shape[1]
    tm = x_ref.shape[0]
    h =_layer_norm(x_ref[...], lng_ref[...], lnb_ref[...])
    y_nsa = _dot(on_ref[...], wn_ref[...])
    glu = _dot(gy_ref[...], wglu_ref[...])
    y_s5 = glu[:, :D] * jax.nn.sigmoid(glu[:, D:])
    y_mem = _dot(om_ref[...], wmo_ref[...])
    merged = gm_ref[:, 0:D] * y_nsa + gm_ref[:, D:2 * D] * y_s5 + gm_ref[:, 2 * D:3 * D] * y_mem
    mix = _dot(merged.astype(BF16), wo_ref[...])
    h1 = _layer_norm(DEEPNORM_ALPHA * h + mix, l1g_ref[...], l1b_ref[...])
    h1_ref[...] = h1

    hh = h1.astype(BF16)
    hl = (h1 - hh.astype(F32)).astype(BF16)
    logits = _dot(hh, wrh_ref[...]) + _dot(hh, wrl_ref[...]) + _dot(hl, wrh_ref[...]) + br_ref[...]
    lane = lax.broadcasted_iota(I32, (tm, LANES), 1)
    lane_f = lane.astype(F32)
    work = logits
    multi = jnp.zeros((tm, LANES), F32)
    vals, picks = [], []
    for _ in range(TOP_K):
        m = jnp.max(work, axis=-1, keepdims=True)
        idx = jnp.min(jnp.where(work == m, lane_f, float(LANES)), axis=-1, keepdims=True)
        pick = lane_f == idx
        vals.append(m)
        picks.append((pick, idx))
        multi = jnp.where(pick, 1.0, multi)
        work = jnp.where(pick, -jnp.inf, work)
    es = [jnp.exp(v - vals[0]) for v in vals]
    den = es[0] + es[1] + es[2] + es[3]
    cnt = jnp.broadcast_to(jnp.sum(multi, axis=0, keepdims=True), (SUBLANES, LANES))
    cnt_ref[0] = cnt
    lower = _dot(cnt.astype(BF16), striu_ref[...])[0:1]
    pos = lower + _dot(tri_ref[...], multi.astype(BF16))
    lp = jnp.full((tm, LANES), -1.0, F32)
    w4 = jnp.zeros((tm, LANES), F32)
    for k in range(TOP_K):
        pick, _ = picks[k]
        lp = jnp.where(lane == k, jnp.sum(jnp.where(pick, pos, 0.0), axis=-1, keepdims=True), lp)
        w4 = jnp.where(lane == k, es[k] / den, w4)
    lp_ref[...] = lp
    w4_ref[...] = w4


def _merge(x2, lng, lnb, o_nsa, gy, om, gm, wn, wglu, wmo, wo, l1g, l1b, wrh, wrl, br, tri, striu):
    T, D = x2.shape
    tm = TOKEN_TILE
    tok = lambda w: pl.BlockSpec((tm, w), lambda i: (i, 0))
    ws = [wn, wglu, wmo, wo, l1g, l1b, wrh, wrl, br, tri, striu]
    sd = jax.ShapeDtypeStruct
    lane_out = sd((T, LANES), F32)
    return pl.pallas_call(
        _merge_kernel, grid=(T // tm,),
        in_specs=[tok(D), _full_spec((1, D)), _full_spec((1, D)), tok(NSA_WIDTH), tok(S5_WIDTH),
                  tok(MEM_WIDTH), tok(N_BRANCH * D)] + [_full_spec(w.shape) for w in ws],
        out_specs=[tok(D), tok(LANES), tok(LANES), pl.BlockSpec((1, SUBLANES, LANES), lambda i: (i, 0, 0))],
        out_shape=[sd((T, D), F32), lane_out, lane_out, sd((T // tm, SUBLANES, LANES), F32)],
        compiler_params=_cparams("parallel"), name="merge",
    )(x2, lng, lnb, o_nsa, gy, om, gm, *ws)


def _slots_kernel(cnt_ref, triu_ref, striu_ref, tril_ref, seg_ref, blk_ref, misc_ref):
    n_blk = blk_ref.shape[0]
    cnt = cnt_ref[...]
    cnt_b = cnt.astype(BF16)
    total = jnp.sum(cnt, axis=0, keepdims=True)
    nblk_e = jnp.floor((total + (MOE_ROWS - 1)) * (1.0 / MOE_ROWS))
    nblk_8 = jnp.broadcast_to(nblk_e, (SUBLANES, LANES))
    end_b = _dot(nblk_8.astype(BF16), triu_ref[...])
    start_rows = (end_b - nblk_8)[0:1] * MOE_ROWS
    dst = start_rows + _dot(tril_ref[...], cnt_b)
    off = _dot(cnt_b, striu_ref[...])
    seg_ref[0] = cnt.astype(I32)
    seg_ref[1] = off.astype(I32)
    seg_ref[2] = dst.astype(I32)
    blk_i = lax.broadcasted_iota(I32, (n_blk, LANES), 0).astype(F32)
    lane_b = lax.broadcasted_iota(I32, (n_blk, LANES), 1)
    ended = jnp.where((end_b[0:1] <= blk_i) & (lane_b < N_EXPERTS), 1.0, 0.0)
    owner = jnp.minimum(jnp.sum(ended, axis=-1, keepdims=True), float(N_EXPERTS - 1))
    blk_ref[...] = jnp.broadcast_to(owner, (n_blk, LANES)).astype(I32)
    lane8 = lax.broadcasted_iota(I32, (SUBLANES, LANES), 1)
    row8 = lax.broadcasted_iota(I32, (SUBLANES, LANES), 0)
    used = jnp.sum(jnp.where(lane8 == N_EXPERTS - 1, end_b, 0.0), axis=-1, keepdims=True)
    misc = jnp.where(row8 == 0, used, jnp.where(row8 == 1, start_rows + total, nblk_e * MOE_ROWS - total))
    misc_ref[...] = misc.astype(I32)


def _slots(cnt, triu, striu, tril, n_blk):
    n_tile = cnt.shape[0]
    sd = jax.ShapeDtypeStruct
    return pl.pallas_call(
        _slots_kernel, grid=(1,),
        in_specs=[_full_spec(cnt.shape), _full_spec(triu.shape), _full_spec(striu.shape), _full_spec(tril.shape)],
        out_specs=[_full_spec((3, n_tile, LANES)), _full_spec((n_blk, LANES)), _full_spec((SUBLANES, LANES))],
        out_shape=[sd((3, n_tile, LANES), I32), sd((n_blk, LANES), I32), sd((SUBLANES, LANES), I32)],
        compiler_params=_cparams("arbitrary"), name="slots",
    )(cnt, triu, striu, tril)


ROW_TILES = D_MODEL // LANES


def _row_span(row, n_rows):
    start = row * ROW_TILES
    if not isinstance(start, int):
        start = pl.multiple_of(start, ROW_TILES)
    return pl.ds(start, n_rows * ROW_TILES)


def _store_rows(ref, val):
    for c in range(ROW_TILES):
        ref[pl.ds(c, val.shape[0], stride=ROW_TILES), :] = val[:, c * LANES:(c + 1) * LANES]


def _load_row_tile(ref, n_rows, c):
    return ref[pl.ds(c, n_rows, stride=ROW_TILES), :]


def _pieces(count, max_rows, fn):
    p = max_rows
    while p >= 1:
        def piece(p=p):
            fn(count & (-2 * p), p)
        pl.when((count & p) != 0)(piece)
        p //= 2


def _segment_copies(seg_ref, max_rows, make_copy, wait):
    def per_expert(e, c):
        cnt, off, dst = seg_ref[0, 0, 0, e], seg_ref[1, 0, 0, e], seg_ref[2, 0, 0, e]

        def one(first, rows):
            cp = make_copy(off + first, dst + first, rows)
            cp.wait() if wait else cp.start()

        _pieces(cnt, max_rows, one)
        return c

    lax.fori_loop(0, N_EXPERTS, per_expert, 0)


def _dispatch_kernel(seg_ref, segp_ref, misc_ref, lp_ref, h_ref, xs_ref, sorted_ref, zero_ref, sem, pad_sem):
    i = pl.program_id(0)
    n = pl.num_programs(0)
    tm, D = h_ref.shape
    rows = TOP_K * tm
    slot = lax.rem(i, 2)

    def row_copy(slot_):
        def make(src_row, dst_row, n_rows):
            return pltpu.make_async_copy(sorted_ref.at[slot_, _row_span(src_row, n_rows)],
                                         xs_ref.at[_row_span(dst_row, n_rows)], sem.at[slot_])
        return make

    @pl.when(i == 0)
    def _():
        zero_ref[...] = jnp.zeros_like(zero_ref)
        for wait in (False, True):
            def per_expert(e, c, wait=wait):
                def one(first, n_rows):
                    cp = pltpu.make_async_copy(zero_ref.at[_row_span(0, n_rows)],
                                               xs_ref.at[_row_span(misc_ref[1, e] + first, n_rows)], pad_sem)
                    cp.wait() if wait else cp.start()
                _pieces(misc_ref[2, e], MOE_ROWS // 2, one)
                return c
            lax.fori_loop(0, N_EXPERTS, per_expert, 0)

            def per_spare_half_block(hb, c, wait=wait):
                cp = pltpu.make_async_copy(zero_ref, xs_ref.at[_row_span(hb * (MOE_ROWS // 2), MOE_ROWS // 2)],
                                           pad_sem)
                cp.wait() if wait else cp.start()
                return c
            lax.fori_loop(2 * misc_ref[0, 0], 2 * (xs_ref.shape[0] // (MOE_ROWS * ROW_TILES)),
                          per_spare_half_block, 0)

    lp_t = lp_ref[...].T
    s_ix = lax.broadcasted_iota(I32, (rows, 1), 0).astype(F32)
    hit = s_ix == lp_t[0:1, :]
    for k in range(1, TOP_K):
        hit = hit | (s_ix == lp_t[k:k + 1, :])
    perm = jnp.where(hit, 1.0, 0.0).astype(BF16)
    _store_rows(sorted_ref.at[slot], _dot(perm, h_ref[...].astype(BF16)))

    _segment_copies(seg_ref, tm, row_copy(slot), wait=False)

    @pl.when(i > 0)
    def _():
        _segment_copies(segp_ref, tm, row_copy(1 - slot), wait=True)

    @pl.when(i == n - 1)
    def _():
        _segment_copies(seg_ref, tm, row_copy(slot), wait=True)


def _seg_spec(index_map):
    return pl.BlockSpec((3, 1, 1, LANES), index_map, memory_space=pltpu.SMEM)


def _dispatch(seg4, misc, lp, h1, cap):
    T, D = h1.shape
    assert D == ROW_TILES * LANES
    tm = TOKEN_TILE
    tok = lambda w: pl.BlockSpec((tm, w), lambda i: (i, 0))
    return pl.pallas_call(
        _dispatch_kernel, grid=(T // tm,),
        in_specs=[_seg_spec(lambda i: (0, i, 0, 0)), _seg_spec(lambda i: (0, jnp.maximum(i - 1, 0), 0, 0)),
                  pl.BlockSpec(memory_space=pltpu.SMEM), tok(LANES), tok(D)],
        out_specs=pl.BlockSpec(memory_space=pl.ANY),
        out_shape=jax.ShapeDtypeStruct((cap * ROW_TILES, LANES), F32),
        scratch_shapes=[pltpu.VMEM((2, TOP_K * tm * ROW_TILES, LANES), F32),
                        pltpu.VMEM((MOE_ROWS // 2 * ROW_TILES, LANES), F32),
                        pltpu.SemaphoreType.DMA((2,)), pltpu.SemaphoreType.DMA(())],
        compiler_params=_cparams("arbitrary"), name="dispatch",
    )(seg4, seg4, misc, lp, h1)


def _expert_kernel(blk_ref, used_ref, xs_ref, wgu_ref, bgu_ref, wd_ref, bd_ref, ys_ref, wgu_bf, wd_bf):
    i = pl.program_id(0)
    live = i < used_ref[0]

    @pl.when(live & ((i == 0) | (blk_ref[i] != blk_ref[jnp.maximum(i - 1, 0)])))
    def _():
        wgu_bf[...] = wgu_ref[0].astype(BF16)
        wd_bf[...] = wd_ref[0].astype(BF16)

    @pl.when(live)
    def _():
        xb = jnp.concatenate([_load_row_tile(xs_ref, MOE_ROWS, c).astype(BF16) for c in range(ROW_TILES)],
                             axis=1)
        gu = _dot(xb, wgu_bf[...]) + bgu_ref[0]
        g = jnp.minimum(gu[:, :D_FF], SWIGLU_LIMIT)
        lin = jnp.clip(gu[:, D_FF:], -SWIGLU_LIMIT, SWIGLU_LIMIT)
        act = g * jax.nn.sigmoid(SWIGLU_ALPHA * g) * (lin + 1.0)
        _store_rows(ys_ref, _dot(act.astype(BF16), wd_bf[...]) + bd_ref[0])

    @pl.when(pl.program_id(0) >= used_ref[0])
    def _():
        ys_ref[...] = jnp.zeros_like(ys_ref)


def _experts(blk_expert, n_used, xs, w_gate_up, b_gate_up, w_down, b_down):
    D = w_down.shape[2]
    n_blk = xs.shape[0] // (MOE_ROWS * ROW_TILES)
    E = w_gate_up.shape[0]
    live = lambda i, used: jnp.minimum(i, used[0] - 1)
    row = pl.BlockSpec((MOE_ROWS * ROW_TILES, LANES), lambda i, blk, used: (live(i, used), 0))
    by_e = lambda shape: pl.BlockSpec((1,) + shape, lambda i, blk, used: (blk[live(i, used)], 0, 0))
    grid_spec = pltpu.PrefetchScalarGridSpec(
        num_scalar_prefetch=2, grid=(n_blk,),
        in_specs=[row, by_e((D, 2 * D_FF)), by_e((1, 2 * D_FF)), by_e((D_FF, D)), by_e((1, D))],
        out_specs=pl.BlockSpec((MOE_ROWS * ROW_TILES, LANES), lambda i, blk, used: (i, 0)),
        scratch_shapes=[pltpu.VMEM((D, 2 * D_FF), BF16), pltpu.VMEM((D_FF, D), BF16)])
    return pl.pallas_call(
        _expert_kernel, grid_spec=grid_spec, out_shape=jax.ShapeDtypeStruct(xs.shape, F32),
        compiler_params=_cparams("arbitrary"), name="experts",
    )(blk_expert, n_used, xs, w_gate_up, b_gate_up.reshape(E, 1, 2 * D_FF), w_down, b_down.reshape(E, 1, D))


def _combine_kernel(seg_ref, segn_ref, lp_ref, w4_ref, h1_ref, g_ref, b_ref, ys_ref, o_ref, buf_ref, sem):
    i = pl.program_id(0)
    n = pl.num_programs(0)
    tm = h1_ref.shape[0]
    rows = TOP_K * tm
    slot = lax.rem(i, 2)

    def row_copy(slot_):
        def make(buf_row, ys_row, n_rows):
            return pltpu.make_async_copy(ys_ref.at[_row_span(ys_row, n_rows)],
                                         buf_ref.at[slot_, _row_span(buf_row, n_rows)], sem.at[slot_])
        return make

    @pl.when(i == 0)
    def _():
        _segment_copies(seg_ref, tm, row_copy(slot), wait=False)

    @pl.when(i + 1 < n)
    def _():
        _segment_copies(segn_ref, tm, row_copy(1 - slot), wait=False)

    _segment_copies(seg_ref, tm, row_copy(slot), wait=True)

    lp = lp_ref[...]
    s_ix = lax.broadcasted_iota(I32, (1, rows), 1).astype(F32)
    wmat = jnp.zeros((tm, rows), F32)
    for k in range(TOP_K):
        wmat = jnp.where(s_ix == lp[:, k:k + 1], w4_ref[:, k:k + 1], wmat)
    w_hi = wmat.astype(BF16)
    w_lo = (wmat - w_hi.astype(F32)).astype(BF16)
    cols = []
    for c in range(ROW_TILES):
        y = _load_row_tile(buf_ref.at[slot], rows, c)
        y_hi = y.astype(BF16)
        y_lo = (y - y_hi.astype(F32)).astype(BF16)
        cols.append(_dot(w_hi, y_hi) + _dot(w_hi, y_lo) + _dot(w_lo, y_hi))
    acc = DEEPNORM_ALPHA * h1_ref[...] + jnp.concatenate(cols, axis=1)
    o_ref[...] = _layer_norm(acc, g_ref[...], b_ref[...])


def _combine(seg4, lp, w4, h1, ln_g, ln_b, ys):
    T, D = h1.shape
    tm = TOKEN_TILE
    n_tile = T // tm
    return pl.pallas_call(
        _combine_kernel, grid=(n_tile,),
        in_specs=[_seg_spec(lambda i: (0, i, 0, 0)),
                  _seg_spec(lambda i: (0, jnp.minimum(i + 1, n_tile - 1), 0, 0)),
                  pl.BlockSpec((tm, LANES), lambda i: (i, 0)),
                  pl.BlockSpec((tm, LANES), lambda i: (i, 0)),
                  pl.BlockSpec((tm, D), lambda i: (i, 0)),
                  _full_spec((1, D)), _full_spec((1, D)),
                  pl.BlockSpec(memory_space=pl.ANY)],
        out_specs=pl.BlockSpec((tm, D), lambda i: (i, 0)),
        out_shape=jax.ShapeDtypeStruct((T, D), F32),
        scratch_shapes=[pltpu.VMEM((2, TOP_K * tm * ROW_TILES, LANES), F32), pltpu.SemaphoreType.DMA((2,))],
        compiler_params=_cparams("arbitrary"), name="combine",
    )(seg4, seg4, lp, w4, h1, ln_g, ln_b, ys)


def _rope_tables(positions):
    inv = ROPE_THETA ** (-jnp.arange(0, ROT_DIM, 2, dtype=F32) / ROT_DIM)
    ang = positions.astype(F32)[..., None] * inv
    cos_sin = jnp.concatenate([jnp.cos(ang), jnp.sin(ang)], axis=-1)
    half = ROT_DIM // 2
    spread = np.zeros((ROT_DIM, 3 * LANES), np.float32)
    unit = np.ones((1, LANES), np.float32)
    for lane in range(LANES):
        d = lane % HEAD_DIM
        if d < half:
            spread[d, lane] = 1.0
            spread[half + d, 2 * LANES + lane] = -1.0
            unit[0, lane] = 0.0
        elif d < ROT_DIM:
            spread[d - half, lane] = 1.0
            spread[d, LANES + lane] = 1.0
            unit[0, lane] = 0.0
    return cos_sin, jnp.asarray(spread, BF16), jnp.asarray(unit)


def _split_w_in(w_in):
    widths = (NSA_WIDTH,) + (KV_WIDTH,) * 6 + (NSA_HEADS * N_BRANCH, S5_WIDTH, MEM_WIDTH, N_BRANCH * D_MODEL)
    offs = [0]
    for w in widths:
        offs.append(offs[-1] + w)
    col = lambda i: w_in[:, offs[i]:offs[i + 1]]
    wq, kc, vc, ks, vs, kw, vw, wg, wu, wqm, wm = (col(i) for i in range(11))
    wk = jnp.concatenate([kc, ks, kw], axis=1)
    wv = jnp.concatenate([vc, vs, vw], axis=1)
    per_group = NSA_HPG * N_BRANCH
    wg_pad = jnp.zeros((w_in.shape[0], NSA_GROUPS * LANES), w_in.dtype)
    for g in range(NSA_GROUPS):
        wg_pad = wg_pad.at[:, g * LANES:g * LANES + per_group].set(wg[:, g * per_group:(g + 1) * per_group])
    return tuple(w.astype(BF16) for w in (wq, wk, wv, wg_pad, wu, wqm, wm))


def _compress_weights(w1):
    half = CMP_BLOCK // 2
    eye = jnp.eye(NSA_GROUPS, dtype=w1.dtype)

    def arrange(w_half):
        full = jnp.einsum('sdf,gh->sgdhf', w_half, eye)
        return full.reshape(half * NSA_GROUPS * HEAD_DIM, NSA_GROUPS * CMP_HIDDEN).astype(BF16)

    return (w1.reshape(CMP_BLOCK * HEAD_DIM, CMP_HIDDEN).astype(BF16), arrange(w1[:half]), arrange(w1[half:]))


def _s5_weights(a_re, a_im, log_dt, b_re, b_im, c_re, c_im):
    step = jnp.exp(log_dt)[:, None]
    mag = jnp.exp(a_re * step)
    ab_re, ab_im = mag * jnp.cos(a_im * step), mag * jnp.sin(a_im * step)
    den = a_re * a_re + a_im * a_im
    nr = ab_re - 1.0
    coef_re = (nr * a_re + ab_im * a_im) / den
    coef_im = (ab_im * a_re - nr * a_im) / den
    bb_re = coef_re[..., None] * b_re - coef_im[..., None] * b_im
    bb_im = coef_re[..., None] * b_im + coef_im[..., None] * b_re
    eye = jnp.eye(S5_GROUPS, dtype=F32)
    n_state = S5_GROUPS * S5_STATE
    n_tile = n_state // LANES
    in_per = n_tile // (S5_WIDTH // LANES)

    def in_map(bb):
        return jnp.einsum('gnp,gh->gphn', bb, eye).reshape(S5_WIDTH, n_state)

    def out_map(c):
        return jnp.einsum('gpn,gh->gnhp', c, eye).reshape(n_state, S5_WIDTH)

    bf_re, bf_im = in_map(bb_re), in_map(bb_im)
    cf_re, cf_im = out_map(c_re), out_map(-c_im)
    wb, wc = [], []
    for c in range(n_tile):
        i = c // in_per
        rs, cs = slice(i * LANES, (i + 1) * LANES), slice(c * LANES, (c + 1) * LANES)
        wb.append(jnp.concatenate([bf_re[rs, cs], bf_im[rs, cs]], axis=1))
        wc.append(jnp.concatenate([cf_re[cs, rs], cf_im[cs, rs]], axis=0))
    wb = jnp.stack(wb).astype(BF16)
    wc = jnp.stack(wc).astype(BF16)
    return wb, wc, ab_re.reshape(n_tile, LANES), ab_im.reshape(n_tile, LANES)


def _layer(x, mem, positions, ln_emb_g, ln_emb_b, w_in, pe_k, pe_v, w_kcmp1, w_kcmp2, w_vcmp1, w_vcmp2,
           s5_a_re, s5_a_im, s5_log_dt, s5_b_re, s5_b_im, s5_c_re, s5_c_im, s5_d,
           w_s5_glu, w_mem_kv, w_nsa_out, w_mem_out, w_o, ln1_g, ln1_b, w_router, b_router,
           w_gate_up, b_gate_up, w_down, b_down, ln2_g, ln2_b):
    B, L, D = x.shape
    T = B * L
    row = lambda v: v.reshape(1, -1)

    cos_sin, spread, unit = _rope_tables(positions)
    (q_hm, kc, vc, ks, vs, kw, vw, gates, u, qm, gm) = _inproj(
        x, row(ln_emb_g), row(ln_emb_b), cos_sin, spread, unit, *_split_w_in(w_in))

    n_chunk = L // CMP_STRIDE
    chunked = lambda t: t.reshape(B, n_chunk, CMP_STRIDE * KV_WIDTH)
    pe_rows = lambda pe: jnp.broadcast_to(pe.reshape(1, -1), (SUBLANES, CMP_BLOCK * HEAD_DIM)).astype(BF16)
    wk1f, wk1a, wk1b = _compress_weights(w_kcmp1)
    wv1f, wv1a, wv1b = _compress_weights(w_vcmp1)
    ck, cv = _compress(chunked(kc), chunked(vc), pe_rows(pe_k), pe_rows(pe_v), wk1f, wv1f,
                       wk1a, wk1b, wv1a, wv1b, w_kcmp2.astype(BF16), w_vcmp2.astype(BF16))

    per_sb = SEL_BLOCK // CMP_STRIDE
    c_ix = np.arange(n_chunk)[:, None]
    n_ix = np.arange(L // SEL_BLOCK)[None, :]
    w_score = jnp.asarray((c_ix // per_sb == n_ix).astype(np.float32)
                          + ((c_ix + 1) // per_sb == n_ix).astype(np.float32), BF16)
    o_nsa = _nsa(q_hm, ck, cv, ks, vs, kw, vw, gates, w_score)

    wb, wc, a_re, a_im = _s5_weights(s5_a_re, s5_a_im, s5_log_dt, s5_b_re, s5_b_im, s5_c_re, s5_c_im)
    gy = _s5(u, wb, wc, a_re, a_im, row(s5_d))

    k_mem, v_mem = _memkv(mem, w_mem_kv.astype(BF16))
    o_mem = _memattn(qm, k_mem, v_mem)

    pad_e = LANES - N_EXPERTS
    wr = jnp.pad(w_router, ((0, 0), (0, pad_e)))
    wr_hi = wr.astype(BF16)
    wr_lo = (wr - wr_hi.astype(F32)).astype(BF16)
    br = jnp.concatenate([b_router, jnp.full((pad_e,), -jnp.inf, F32)]).reshape(1, LANES)
    tm = TOKEN_TILE
    n_tile = T // tm
    strict_lower = lambda n: jnp.asarray(np.tril(np.ones((n, n), np.float32), -1), BF16)
    triu = jnp.asarray(np.triu(np.ones((LANES, LANES), np.float32)), BF16)
    striu = jnp.asarray(np.triu(np.ones((LANES, LANES), np.float32), 1), BF16)
    flat = lambda t: t.reshape(T, t.shape[-1])
    h1, lp, w4, cnt = _merge(
        flat(x), row(ln_emb_g), row(ln_emb_b), flat(o_nsa), flat(gy), flat(o_mem), flat(gm),
        w_nsa_out.astype(BF16), w_s5_glu.astype(BF16), w_mem_out.astype(BF16), w_o.astype(BF16),
        row(ln1_g), row(ln1_b), wr_hi, wr_lo, br, strict_lower(tm), striu)

    cap = (T * TOP_K + MOE_ROWS - 1) // MOE_ROWS * MOE_ROWS + N_EXPERTS * MOE_ROWS
    n_blk = cap // MOE_ROWS
    seg, blk_owner, misc = _slots(cnt[:, 0, :], triu, striu, strict_lower(n_tile), n_blk)
    seg4 = seg.reshape(3, n_tile, 1, LANES)
    blk_expert = blk_owner[:, 0]
    n_used = misc[0, :1]

    xs = _dispatch(seg4, misc, lp, h1, cap)
    ys = _experts(blk_expert, n_used, xs, w_gate_up, b_gate_up, w_down, b_down)
    out = _combine(seg4, lp, w4, h1, row(ln2_g), row(ln2_b), ys)
    return out.reshape(B, L, D)


def kernel(x, mem, positions, ln_emb_g, ln_emb_b, w_in, pe_k_cmp, pe_v_cmp, w_kcmp1, w_kcmp2, w_vcmp1, w_vcmp2, s5_a_re, s5_a_im, s5_log_dt, s5_b_re, s5_b_im, s5_c_re, s5_c_im, s5_d, w_s5_glu, w_mem_kv, w_nsa_out, w_mem_out, w_o, ln1_g, ln1_b, w_router, b_router, w_gate_up, b_gate_up, w_down, b_down, ln2_g, ln2_b):
    assert w_in.shape[0] == DEPTH
    l = 0
    return _layer(x, mem, positions, ln_emb_g, ln_emb_b, w_in[l], pe_k_cmp[l], pe_v_cmp[l], w_kcmp1[l],
                  w_kcmp2[l], w_vcmp1[l], w_vcmp2[l], s5_a_re[l], s5_a_im[l], s5_log_dt[l], s5_b_re[l],
                  s5_b_im[l], s5_c_re[l], s5_c_im[l], s5_d[l], w_s5_glu[l], w_mem_kv[l], w_nsa_out[l],
                  w_mem_out[l], w_o[l], ln1_g[l], ln1_b[l], w_router[l], b_router[l], w_gate_up[l],
                  b_gate_up[l], w_down[l], b_down[l], ln2_g[l], ln2_b[l])
```

```python
import functools
import math

import jax
import jax.numpy as jnp
import numpy as np
from jax import lax
from jax.experimental import pallas as pl
from jax.experimental.pallas import tpu as pltpu

F32 = jnp.float32
BF16 = jnp.bfloat16
I32 = jnp.int32

D_MODEL = 1024
NSA_HEADS = 8
NSA_GROUPS = 2
NSA_HPG = NSA_HEADS // NSA_GROUPS
HEAD_DIM = 64
NSA_WIDTH = NSA_HEADS * HEAD_DIM
KV_WIDTH = NSA_GROUPS * HEAD_DIM
CMP_BLOCK = 32
CMP_STRIDE = 16
CMP_HIDDEN = 128
SEL_BLOCK = 64
N_SEL = 16
WINDOW = 512
Q_BLOCK = 128
ROPE_THETA = 500000.0
ROT_DIM = HEAD_DIM // 4
S5_WIDTH = 512
S5_GROUP_DIM = 16
S5_GROUPS = S5_WIDTH // S5_GROUP_DIM
S5_STATE = 64
MEM_HEADS = 4
MEM_HEAD_DIM = 128
MEM_WIDTH = MEM_HEADS * MEM_HEAD_DIM
N_BRANCH = 3
N_EXPERTS = 32
TOP_K = 4
D_FF = 1024
SWIGLU_LIMIT = 7.0
SWIGLU_ALPHA = 1.702
LN_EPS = 1e-5
DEPTH = 1
DEEPNORM_ALPHA = (2 * DEPTH) ** 0.25

LANES = 128
SUBLANES = 8
VMEM_LIMIT_BYTES = 56 * 1024 * 1024

TOKEN_TILE = 256
SEL_KV_TILE = 512
S5_CHUNK = 512
S5_PITCH = S5_CHUNK + 8
MOE_ROWS = 512
NEG_BIG = -(2.0 ** 100)
Q_SCALE_LOG2 = HEAD_DIM ** -0.5 * math.log2(math.e)


def _cparams(*sem):
    return pltpu.CompilerParams(dimension_semantics=sem, vmem_limit_bytes=VMEM_LIMIT_BYTES)


def _dot(a, b):
    return jnp.dot(a, b, preferred_element_type=F32)


def _dot_nt(a, b):
    return lax.dot_general(a, b, (((1,), (1,)), ((), ())), preferred_element_type=F32)


def _layer_norm(x, g, b):
    mu = jnp.mean(x, axis=-1, keepdims=True)
    xc = x - mu
    var = jnp.mean(xc * xc, axis=-1, keepdims=True)
    return xc * lax.rsqrt(var + LN_EPS) * g + b


def _gelu_tanh(x):
    cdf = 0.5 * (1.0 + jnp.tanh(math.sqrt(2.0 / math.pi) * (x + 0.044715 * (x * x * x))))
    return x * cdf


def _masked_exp2(s, mask):
    s = jnp.where(mask, s, -jnp.inf)
    m = jnp.max(s, axis=-1, keepdims=True)
    m = jnp.where(m > -jnp.inf, m, 0.0)
    return jnp.exp2(s - m)


def _safe_recip(denom):
    return 1.0 / jnp.maximum(denom, jnp.finfo(F32).tiny)


def _split3(x):
    hi = x.astype(BF16)
    r1 = x - hi.astype(F32)
    mid = r1.astype(BF16)
    lo = (r1 - mid.astype(F32)).astype(BF16)
    return hi, mid, lo


def _full_spec(shape):
    nd = len(shape)
    return pl.BlockSpec(shape, lambda *_: (0,) * nd)


def _inproj_kernel(x_ref, g_ref, b_ref, cs_ref, spread_ref, unit_ref,
                   wq_ref, wk_ref, wv_ref, wg_ref, wu_ref, wqm_ref, wm_ref,
                   q_ref, kc_ref, vc_ref, ks_ref, vs_ref, kw_ref, vw_ref,
                   gate_ref, u_ref, qm_ref, gm_ref):
    h = _layer_norm(x_ref[0], g_ref[...], b_ref[...])
    hb = h.astype(BF16)
    tab = sum(_dot(part, spread_ref[...]) for part in _split3(cs_ref[0]))
    cos_t = tab[:, 0:LANES] + unit_ref[...]
    sin_a = tab[:, LANES:2 * LANES]
    sin_b = tab[:, 2 * LANES:3 * LANES]

    def rope(t):
        return (t * cos_t + pltpu.roll(t, ROT_DIM // 2, 1) * sin_a
                + pltpu.roll(t, LANES - ROT_DIM // 2, 1) * sin_b)

    q = _dot(hb, wq_ref[...])
    for c in range(NSA_WIDTH // LANES):
        qc = rope(q[:, c * LANES:(c + 1) * LANES]) * Q_SCALE_LOG2
        for hh in range(2):
            q_ref[0, 2 * c + hh] = qc[:, hh * HEAD_DIM:(hh + 1) * HEAD_DIM].astype(BF16)
    k3 = _dot(hb, wk_ref[...])
    kc = rope(k3[:, 0:LANES])
    ks = rope(k3[:, LANES:2 * LANES])
    kw = rope(k3[:, 2 * LANES:3 * LANES])
    v3 = _dot(hb, wv_ref[...])
    kc_ref[0] = kc.astype(BF16)
    vc_ref[0] = v3[:, 0:LANES].astype(BF16)
    tm = x_ref.shape[1]
    pos = pl.program_id(1) * tm + lax.broadcasted_iota(I32, (tm, LANES), 0)
    blk_hot = jnp.where(lax.broadcasted_iota(I32, (tm, LANES), 1) == pos // SEL_BLOCK, 1.0, 0.0)
    lane_pad = jnp.zeros((tm, LANES - HEAD_DIM), F32)
    ones_pad = jnp.where(lax.broadcasted_iota(I32, (tm, LANES - HEAD_DIM), 1) == 0, 1.0, 0.0)
    for g in range(NSA_GROUPS):
        sl = slice(g * HEAD_DIM, (g + 1) * HEAD_DIM)
        ks_ref[0, g] = jnp.concatenate([blk_hot, ks[:, sl], lane_pad], axis=1).astype(BF16)
        kw_ref[0, g] = kw[:, sl].astype(BF16)
        vs_ref[0, g] = jnp.concatenate([v3[:, LANES:2 * LANES][:, sl], ones_pad], axis=1).astype(BF16)
        vw_ref[0, g] = jnp.concatenate([v3[:, 2 * LANES:3 * LANES][:, sl], ones_pad], axis=1).astype(BF16)
    gate_ref[0] = jax.nn.sigmoid(_dot(hb, wg_ref[...]))
    u_ref[0] = _dot(hb, wu_ref[...])
    qm_ref[0] = _dot(hb, wqm_ref[...]).astype(BF16)
    gm_ref[0] = jax.nn.sigmoid(_dot(hb, wm_ref[...]))


def _inproj(x, ln_g, ln_b, cos_sin, spread, unit, wq, wk, wv, wg, wu, wqm, wm):
    B, L, D = x.shape
    tm = TOKEN_TILE
    grid = (B, L // tm)
    tok = lambda w: pl.BlockSpec((1, tm, w), lambda b, i: (b, i, 0))
    head = lambda n, w=HEAD_DIM: pl.BlockSpec((1, n, tm, w), lambda b, i: (b, 0, i, 0))
    in_specs = [tok(D), _full_spec((1, D)), _full_spec((1, D)), tok(ROT_DIM), _full_spec(spread.shape),
                _full_spec(unit.shape)]
    in_specs += [_full_spec(w.shape) for w in (wq, wk, wv, wg, wu, wqm, wm)]
    sd = jax.ShapeDtypeStruct
    out_shape = [
        sd((B, NSA_HEADS, L, HEAD_DIM), BF16),
        sd((B, L, KV_WIDTH), BF16), sd((B, L, KV_WIDTH), BF16),
        sd((B, NSA_GROUPS, L, 2 * LANES), BF16), sd((B, NSA_GROUPS, L, LANES), BF16),
        sd((B, NSA_GROUPS, L, HEAD_DIM), BF16), sd((B, NSA_GROUPS, L, LANES), BF16),
        sd((B, L, NSA_GROUPS * LANES), F32),
        sd((B, L, S5_WIDTH), F32),
        sd((B, L, MEM_WIDTH), BF16),
        sd((B, L, N_BRANCH * D), F32),
    ]
    out_specs = [head(NSA_HEADS), tok(KV_WIDTH), tok(KV_WIDTH), head(NSA_GROUPS, 2 * LANES),
                 head(NSA_GROUPS, LANES), head(NSA_GROUPS), head(NSA_GROUPS, LANES),
                 tok(NSA_GROUPS * LANES), tok(S5_WIDTH),
                 tok(MEM_WIDTH), tok(N_BRANCH * D)]
    return pl.pallas_call(
        _inproj_kernel, grid=grid, in_specs=in_specs, out_specs=out_specs, out_shape=out_shape,
        compiler_params=_cparams("parallel", "parallel"), name="inproj",
    )(x, ln_g, ln_b, cos_sin, spread, unit, wq, wk, wv, wg, wu, wqm, wm)


def _compress_kernel(kc_ref, vc_ref, pek_ref, pev_ref, wk1f_ref, wv1f_ref,
                     wk1a_ref, wk1b_ref, wv1a_ref, wv1b_ref, wk2_ref, wv2_ref, ck_ref, cv_ref):
    n_chunk = kc_ref.shape[1]
    row = lax.broadcasted_iota(I32, (n_chunk, 1), 0)

    def one(x_ref, pe_ref, w1f_ref, w1a_ref, w1b_ref, w2_ref, o_ref):
        x = x_ref[0]
        first = _dot(x, w1a_ref[...])
        second = _dot(x, w1b_ref[...])
        second = pltpu.roll(second, n_chunk - 1, 0)
        pe_term = _dot(pe_ref[...], w1f_ref[...])[0:1]
        pe_term = jnp.concatenate([pe_term] * NSA_GROUPS, axis=1)
        hid = _gelu_tanh(first + second + pe_term).astype(BF16)
        for g in range(NSA_GROUPS):
            o = _dot(hid[:, g * CMP_HIDDEN:(g + 1) * CMP_HIDDEN], w2_ref[...])
            o_ref[0, g] = jnp.where(row < n_chunk - 1, o, 0.0).astype(BF16)

    one(kc_ref, pek_ref, wk1f_ref, wk1a_ref, wk1b_ref, wk2_ref, ck_ref)
    one(vc_ref, pev_ref, wv1f_ref, wv1a_ref, wv1b_ref, wv2_ref, cv_ref)


def _compress(kc_r, vc_r, pek, pev, wk1f, wv1f, wk1a, wk1b, wv1a, wv1b, wk2, wv2):
    B, n_chunk, width = kc_r.shape
    blk = pl.BlockSpec((1, n_chunk, width), lambda b: (b, 0, 0))
    out = pl.BlockSpec((1, NSA_GROUPS, n_chunk, HEAD_DIM), lambda b: (b, 0, 0, 0))
    ws = [pek, pev, wk1f, wv1f, wk1a, wk1b, wv1a, wv1b, wk2, wv2]
    sd = jax.ShapeDtypeStruct((B, NSA_GROUPS, n_chunk, HEAD_DIM), BF16)
    return pl.pallas_call(
        _compress_kernel, grid=(B,), in_specs=[blk, blk] + [_full_spec(w.shape) for w in ws],
        out_specs=[out, out], out_shape=[sd, sd], compiler_params=_cparams("parallel"), name="compress",
    )(kc_r, vc_r, *ws)


def _nsa_kernel(q_ref, ck_ref, cv_ref, ks_ref, vs_ref, kw_ref, vw_ref, gate_ref, wsc_ref, o_ref):
    seq_len = ks_ref.shape[2]
    n_cmp = ck_ref.shape[2]
    n_sb = seq_len // SEL_BLOCK
    n_sel = min(N_SEL, n_sb)
    rows = NSA_HPG * Q_BLOCK
    groups = range(NSA_GROUPS)
    q0 = pl.program_id(1) * Q_BLOCK
    t1 = q0 + lax.broadcasted_iota(I32, (Q_BLOCK, 1), 0)
    t4 = jnp.concatenate([t1] * NSA_HPG, axis=0)
    tk = SEL_KV_TILE

    def front(g):
        q = q_ref[0, g * NSA_HPG:(g + 1) * NSA_HPG].reshape(rows, HEAD_DIM)

        s = _dot_nt(q, ck_ref[0, g])
        c_end = lax.broadcasted_iota(I32, (1, n_cmp), 1) * CMP_STRIDE + (CMP_BLOCK - 1)
        e = _masked_exp2(s, c_end <= t4)
        p_cmp = e * _safe_recip(jnp.sum(e, axis=-1, keepdims=True))
        o_cmp = _dot(p_cmp.astype(BF16), cv_ref[0, g])

        imp = p_cmp[0:Q_BLOCK]
        for hh in range(1, NSA_HPG):
            imp = imp + p_cmp[hh * Q_BLOCK:(hh + 1) * Q_BLOCK]
        w_sc = wsc_ref[...]
        score = sum(_dot(part, w_sc) for part in _split3(imp))
        score_t = score.T
        jb = lax.broadcasted_iota(I32, (n_sb, Q_BLOCK), 0)
        tb = (q0 + lax.broadcasted_iota(I32, (1, Q_BLOCK), 1)) // SEL_BLOCK
        forced = (jb == 0) | (jb == tb) | (jb == tb - 1)
        work = jnp.where(forced | (jb > tb), -jnp.inf, score_t)
        bias_t = jnp.where(forced, 0.0, NEG_BIG)
        jbf = jb.astype(F32)
        for _ in range(n_sel - 3):
            m = jnp.max(work, axis=0, keepdims=True)
            idx = jnp.min(jnp.where(work == m, jbf, float(n_sb)), axis=0, keepdims=True)
            pick = jbf == idx
            bias_t = jnp.where(pick, 0.0, bias_t)
            work = jnp.where(pick, -jnp.inf, work)
        sel_bias = bias_t.T
        if n_sb < LANES:
            sel_bias = jnp.concatenate([sel_bias, jnp.zeros((Q_BLOCK, LANES - n_sb), F32)], axis=1)

        span = WINDOW + Q_BLOCK
        w0 = pl.multiple_of(jnp.maximum(q0 - WINDOW, 0), Q_BLOCK)
        s = _dot_nt(q, kw_ref[0, g, pl.ds(w0, span), :])
        diff = t4 - (w0 + lax.broadcasted_iota(I32, (1, span), 1))
        e = _masked_exp2(s, (diff >= 0) & (diff < WINDOW))
        o_win = _dot(e.astype(BF16), vw_ref[0, g, pl.ds(w0, span), :])
        o_win = o_win[:, :HEAD_DIM] * _safe_recip(o_win[:, HEAD_DIM:HEAD_DIM + 1])

        q_aug = jnp.concatenate([jnp.concatenate([sel_bias.astype(BF16)] * NSA_HPG, axis=0), q,
                                 jnp.zeros((rows, LANES - HEAD_DIM), BF16)], axis=1)
        return q_aug, o_cmp, o_win

    fronts = [front(g) for g in groups]

    def sel_tile(g, j, carry, causal):
        m_run, acc = carry
        k0 = pl.multiple_of(j * tk, tk)
        sc = _dot_nt(fronts[g][0], ks_ref[0, g, pl.ds(k0, tk), :])
        if causal:
            kpos = k0 + lax.broadcasted_iota(I32, (1, tk), 1)
            sc = jnp.where(kpos <= t4, sc, NEG_BIG)
        m_new = jnp.maximum(m_run, jnp.max(sc, axis=-1, keepdims=True))
        p = jnp.exp2(sc - m_new)
        acc_new = jnp.exp2(m_run - m_new) * acc + _dot(p.astype(BF16), vs_ref[0, g, pl.ds(k0, tk), :])
        return m_new, acc_new

    def sel_pair(jj, carries, causal):
        return tuple(sel_tile(g, 2 * jj + 1, sel_tile(g, 2 * jj, carries[g], causal), causal) for g in groups)

    init = tuple((jnp.full((rows, 1), NEG_BIG, F32), jnp.zeros((rows, LANES), F32)) for _ in groups)
    last_pair = (q0 // tk) // 2
    carries = lax.fori_loop(0, last_pair, functools.partial(sel_pair, causal=False), init)
    carries = sel_pair(last_pair, carries, True)

    outs = []
    for g in groups:
        _, o_cmp, o_win = fronts[g]
        acc = carries[g][1]
        o_sel = acc[:, :HEAD_DIM] * (1.0 / acc[:, HEAD_DIM:HEAD_DIM + 1])
        gt = gate_ref[0, :, g * LANES:(g + 1) * LANES]
        for hh in range(NSA_HPG):
            sl = slice(hh * Q_BLOCK, (hh + 1) * Q_BLOCK)
            c = hh * N_BRANCH
            outs.append(o_cmp[sl] * gt[:, c:c + 1] + o_sel[sl] * gt[:, c + 1:c + 2]
                        + o_win[sl] * gt[:, c + 2:c + 3])
    o_ref[0] = jnp.concatenate(outs, axis=1).astype(BF16)


def _nsa(q_hm, ck, cv, ks, vs, kw, vw, gates, w_score):
    B, _, L, _ = q_hm.shape
    assert L // SEL_BLOCK <= LANES and (L // SEL_KV_TILE) % 2 == 0 and L >= WINDOW + Q_BLOCK
    n_cmp = ck.shape[2]
    grid = (B, L // Q_BLOCK)
    qspec = pl.BlockSpec((1, NSA_HEADS, Q_BLOCK, HEAD_DIM), lambda b, i: (b, 0, i, 0))
    cspec = pl.BlockSpec((1, NSA_GROUPS, n_cmp, HEAD_DIM), lambda b, i: (b, 0, 0, 0))
    kvspec = lambda w: pl.BlockSpec((1, NSA_GROUPS, L, w), lambda b, i: (b, 0, 0, 0),
                                    pipeline_mode=pl.Buffered(1))
    gspec = pl.BlockSpec((1, Q_BLOCK, NSA_GROUPS * LANES), lambda b, i: (b, i, 0))
    ospec = pl.BlockSpec((1, Q_BLOCK, NSA_WIDTH), lambda b, i: (b, i, 0))
    return pl.pallas_call(
        _nsa_kernel, grid=grid,
        in_specs=[qspec, cspec, cspec, kvspec(2 * LANES), kvspec(LANES), kvspec(HEAD_DIM), kvspec(LANES), gspec,
                  _full_spec(w_score.shape)],
        out_specs=ospec, out_shape=jax.ShapeDtypeStruct((B, L, NSA_WIDTH), BF16),
        compiler_params=_cparams("parallel", "arbitrary"), name="nsa",
    )(q_hm, ck, cv, ks, vs, kw, vw, gates, w_score)


def _s5_kernel(u_ref, wb_ref, wc_ref, are_ref, aim_ref, d_ref, y_ref, sre_ref, sim_ref, carry_ref):
    n_b, chunk, _ = u_ref.shape
    n_tile = wb_ref.shape[0]
    in_per = n_tile // (S5_WIDTH // LANES)
    pitch = S5_PITCH

    @pl.when(pl.program_id(0) == 0)
    def _():
        carry_ref[...] = jnp.zeros_like(carry_ref)

    for b in range(n_b):
        for c in range(n_tile):
            i = c // in_per
            ub = u_ref[b, :, i * LANES:(i + 1) * LANES].astype(BF16)
            r = _dot(ub, wb_ref[c])
            sre_ref[b, c * pitch:c * pitch + chunk, :] = r[:, :LANES]
            sim_ref[b, c * pitch:c * pitch + chunk, :] = r[:, LANES:]

    a_re, a_im = are_ref[...], aim_ref[...]

    def step(t, carry):
        out = []
        for b in range(n_b):
            s_re, s_im = carry[2 * b], carry[2 * b + 1]
            rows = pl.ds(t, n_tile, stride=pitch)
            n_re = a_re * s_re - a_im * s_im + sre_ref[b, rows, :]
            n_im = a_re * s_im + a_im * s_re + sim_ref[b, rows, :]
            sre_ref[b, rows, :] = n_re
            sim_ref[b, rows, :] = n_im
            out += [n_re, n_im]
        return tuple(out)

    init = tuple(carry_ref[i] for i in range(2 * n_b))
    fin = lax.fori_loop(0, chunk, step, init, unroll=8)
    for i in range(2 * n_b):
        carry_ref[i] = fin[i]

    for b in range(n_b):
        for o in range(S5_WIDTH // LANES):
            acc = jnp.zeros((chunk, LANES), F32)
            for c in range(o * in_per, (o + 1) * in_per):
                rows = slice(c * pitch, c * pitch + chunk)
                acc = acc + _dot(sre_ref[b, rows, :].astype(BF16), wc_ref[c, :LANES])
                acc = acc + _dot(sim_ref[b, rows, :].astype(BF16), wc_ref[c, LANES:])
            lanes = slice(o * LANES, (o + 1) * LANES)
            y = acc + d_ref[:, lanes] * u_ref[b, :, lanes]
            y_ref[b, :, lanes] = _gelu_tanh(y).astype(BF16)


def _s5(u, wb, wc, a_re, a_im, d_skip):
    B, L, W = u.shape
    chunk = S5_CHUNK
    n_tile = wb.shape[0]
    blk = pl.BlockSpec((B, chunk, W), lambda i: (0, i, 0))
    slab = pltpu.VMEM((B, n_tile * S5_PITCH, LANES), F32)
    return pl.pallas_call(
        _s5_kernel, grid=(L // chunk,),
        in_specs=[blk] + [_full_spec(w.shape) for w in (wb, wc, a_re, a_im, d_skip)],
        out_specs=blk, out_shape=jax.ShapeDtypeStruct((B, L, W), BF16),
        scratch_shapes=[slab, slab, pltpu.VMEM((2 * B, n_tile, LANES), F32)],
        compiler_params=_cparams("arbitrary"), name="s5",
    )(u, wb, wc, a_re, a_im, d_skip)


def _memkv_kernel(mem_ref, w_ref, k_ref, v_ref):
    kv = _dot(mem_ref[0].astype(BF16), w_ref[...])
    k_ref[0] = kv[:, :MEM_WIDTH].astype(BF16)
    v_ref[0] = kv[:, MEM_WIDTH:].astype(BF16)


def _memkv(mem, w_kv):
    B, M, D = mem.shape
    out = pl.BlockSpec((1, M, MEM_WIDTH), lambda b: (b, 0, 0))
    sd = jax.ShapeDtypeStruct((B, M, MEM_WIDTH), BF16)
    return pl.pallas_call(
        _memkv_kernel, grid=(B,),
        in_specs=[pl.BlockSpec((1, M, D), lambda b: (b, 0, 0)), _full_spec(w_kv.shape)],
        out_specs=[out, out], out_shape=[sd, sd], compiler_params=_cparams("parallel"), name="memkv",
    )(mem, w_kv)


def _memattn_kernel(q_ref, k_ref, v_ref, o_ref):
    outs = []
    for h in range(MEM_HEADS):
        sl = slice(h * MEM_HEAD_DIM, (h + 1) * MEM_HEAD_DIM)
        s = _dot_nt(q_ref[0, :, sl], k_ref[0, :, sl]) * (MEM_HEAD_DIM ** -0.5)
        m = jnp.max(s, axis=-1, keepdims=True)
        e = jnp.exp(s - m)
        p = e / jnp.sum(e, axis=-1, keepdims=True)
        outs.append(_dot(p.astype(BF16), v_ref[0, :, sl]))
    o_ref[0] = jnp.concatenate(outs, axis=1).astype(BF16)


def _memattn(qm, k, v):
    B, L, W = qm.shape
    M = k.shape[1]
    tm = TOKEN_TILE
    tok = pl.BlockSpec((1, tm, W), lambda b, i: (b, i, 0))
    kv = pl.BlockSpec((1, M, W), lambda b, i: (b, 0, 0))
    return pl.pallas_call(
        _memattn_kernel, grid=(B, L // tm), in_specs=[tok, kv, kv], out_specs=tok,
        out_shape=jax.ShapeDtypeStruct((B, L, W), BF16),
        compiler_params=_cparams("parallel", "parallel"), name="memattn",
    )(qm, k, v)


def _merge_kernel(x_ref, lng_ref, lnb_ref, on_ref, gy_ref, om_ref, gm_ref,
                  wn_ref, wglu_ref, wmo_ref, wo_ref, l1g_ref, l1b_ref,
                  wrh_ref, wrl_ref, br_ref, tri_ref, striu_ref,
                  h1_ref, lp_ref, w4_ref, cnt_ref):
    D = x_ref.shape[1]
    tm = x_ref.shape[0]
    h =_layer_norm(x_ref[...], lng_ref[...], lnb_ref[...])
    y_nsa = _dot(on_ref[...], wn_ref[...])
    glu = _dot(gy_ref[...], wglu_ref[...])
    y_s5 = glu[:, :D] * jax.nn.sigmoid(glu[:, D:])
    y_mem = _dot(om_ref[...], wmo_ref[...])
    merged = gm_ref[:, 0:D] * y_nsa + gm_ref[:, D:2 * D] * y_s5 + gm_ref[:, 2 * D:3 * D] * y_mem
    mix = _dot(merged.astype(BF16), wo_ref[...])
    h1 = _layer_norm(DEEPNORM_ALPHA * h + mix, l1g_ref[...], l1b_ref[...])
    h1_ref[...] = h1

    hh = h1.astype(BF16)
    hl = (h1 - hh.astype(F32)).astype(BF16)
    logits = _dot(hh, wrh_ref[...]) + _dot(hh, wrl_ref[...]) + _dot(hl, wrh_ref[...]) + br_ref[...]
    lane = lax.broadcasted_iota(I32, (tm, LANES), 1)
    lane_f = lane.astype(F32)
    work = logits
    multi = jnp.zeros((tm, LANES), F32)
    vals, picks = [], []
    for _ in range(TOP_K):
        m = jnp.max(work, axis=-1, keepdims=True)
        idx = jnp.min(jnp.where(work == m, lane_f, float(LANES)), axis=-1, keepdims=True)
        pick = lane_f == idx
        vals.append(m)
        picks.append((pick, idx))
        multi = jnp.where(pick, 1.0, multi)
        work = jnp.where(pick, -jnp.inf, work)
    es = [jnp.exp(v - vals[0]) for v in vals]
    den = es[0] + es[1] + es[2] + es[3]
    cnt = jnp.broadcast_to(jnp.sum(multi, axis=0, keepdims=True), (SUBLANES, LANES))
    cnt_ref[0] = cnt
    lower = _dot(cnt.astype(BF16), striu_ref[...])[0:1]
    pos = lower + _dot(tri_ref[...], multi.astype(BF16))
    lp = jnp.full((tm, LANES), -1.0, F32)
    w4 = jnp.zeros((tm, LANES), F32)
    for k in range(TOP_K):
        pick, _ = picks[k]
        lp = jnp.where(lane == k, jnp.sum(jnp.where(pick, pos, 0.0), axis=-1, keepdims=True), lp)
        w4 = jnp.where(lane == k, es[k] / den, w4)
    lp_ref[...] = lp
    w4_ref[...] = w4


def _merge(x2, lng, lnb, o_nsa, gy, om, gm, wn, wglu, wmo, wo, l1g, l1b, wrh, wrl, br, tri, striu):
    T, D = x2.shape
    tm = TOKEN_TILE
    tok = lambda w: pl.BlockSpec((tm, w), lambda i: (i, 0))
    ws = [wn, wglu, wmo, wo, l1g, l1b, wrh, wrl, br, tri, striu]
    sd = jax.ShapeDtypeStruct
    lane_out = sd((T, LANES), F32)
    return pl.pallas_call(
        _merge_kernel, grid=(T // tm,),
        in_specs=[tok(D), _full_spec((1, D)), _full_spec((1, D)), tok(NSA_WIDTH), tok(S5_WIDTH),
                  tok(MEM_WIDTH), tok(N_BRANCH * D)] + [_full_spec(w.shape) for w in ws],
        out_specs=[tok(D), tok(LANES), tok(LANES), pl.BlockSpec((1, SUBLANES, LANES), lambda i: (i, 0, 0))],
        out_shape=[sd((T, D), F32), lane_out, lane_out, sd((T // tm, SUBLANES, LANES), F32)],
        compiler_params=_cparams("parallel"), name="merge",
    )(x2, lng, lnb, o_nsa, gy, om, gm, *ws)


def _slots_kernel(cnt_ref, triu_ref, striu_ref, tril_ref, seg_ref, blk_ref, misc_ref):
    n_blk = blk_ref.shape[0]
    cnt = cnt_ref[...]
    cnt_b = cnt.astype(BF16)
    total = jnp.sum(cnt, axis=0, keepdims=True)
    nblk_e = jnp.floor((total + (MOE_ROWS - 1)) * (1.0 / MOE_ROWS))
    nblk_8 = jnp.broadcast_to(nblk_e, (SUBLANES, LANES))
    end_b = _dot(nblk_8.astype(BF16), triu_ref[...])
    start_rows = (end_b - nblk_8)[0:1] * MOE_ROWS
    dst = start_rows + _dot(tril_ref[...], cnt_b)
    off = _dot(cnt_b, striu_ref[...])
    seg_ref[0] = cnt.astype(I32)
    seg_ref[1] = off.astype(I32)
    seg_ref[2] = dst.astype(I32)
    blk_i = lax.broadcasted_iota(I32, (n_blk, LANES), 0).astype(F32)
    lane_b = lax.broadcasted_iota(I32, (n_blk, LANES), 1)
    ended = jnp.where((end_b[0:1] <= blk_i) & (lane_b < N_EXPERTS), 1.0, 0.0)
    owner = jnp.minimum(jnp.sum(ended, axis=-1, keepdims=True), float(N_EXPERTS - 1))
    blk_ref[...] = jnp.broadcast_to(owner, (n_blk, LANES)).astype(I32)
    lane8 = lax.broadcasted_iota(I32, (SUBLANES, LANES), 1)
    row8 = lax.broadcasted_iota(I32, (SUBLANES, LANES), 0)
    used = jnp.sum(jnp.where(lane8 == N_EXPERTS - 1, end_b, 0.0), axis=-1, keepdims=True)
    misc = jnp.where(row8 == 0, used, jnp.where(row8 == 1, start_rows + total, nblk_e * MOE_ROWS - total))
    misc_ref[...] = misc.astype(I32)


def _slots(cnt, triu, striu, tril, n_blk):
    n_tile = cnt.shape[0]
    sd = jax.ShapeDtypeStruct
    return pl.pallas_call(
        _slots_kernel, grid=(1,),
        in_specs=[_full_spec(cnt.shape), _full_spec(triu.shape), _full_spec(striu.shape), _full_spec(tril.shape)],
        out_specs=[_full_spec((3, n_tile, LANES)), _full_spec((n_blk, LANES)), _full_spec((SUBLANES, LANES))],
        out_shape=[sd((3, n_tile, LANES), I32), sd((n_blk, LANES), I32), sd((SUBLANES, LANES), I32)],
        compiler_params=_cparams("arbitrary"), name="slots",
    )(cnt, triu, striu, tril)


ROW_TILES = D_MODEL // LANES


def _row_span(row, n_rows):
    start = row * ROW_TILES
    if not isinstance(start, int):
        start = pl.multiple_of(start, ROW_TILES)
    return pl.ds(start, n_rows * ROW_TILES)


def _store_rows(ref, val):
    for c in range(ROW_TILES):
        ref[pl.ds(c, val.shape[0], stride=ROW_TILES), :] = val[:, c * LANES:(c + 1) * LANES]


def _load_row_tile(ref, n_rows, c):
    return ref[pl.ds(c, n_rows, stride=ROW_TILES), :]


def _pieces(count, max_rows, fn):
    p = max_rows
    while p >= 1:
        def piece(p=p):
            fn(count & (-2 * p), p)
        pl.when((count & p) != 0)(piece)
        p //= 2


def _segment_copies(seg_ref, max_rows, make_copy, wait):
    def per_expert(e, c):
        cnt, off, dst = seg_ref[0, 0, 0, e], seg_ref[1, 0, 0, e], seg_ref[2, 0, 0, e]

        def one(first, rows):
            cp = make_copy(off + first, dst + first, rows)
            cp.wait() if wait else cp.start()

        _pieces(cnt, max_rows, one)
        return c

    lax.fori_loop(0, N_EXPERTS, per_expert, 0)


def _dispatch_kernel(seg_ref, segp_ref, misc_ref, lp_ref, h_ref, xs_ref, sorted_ref, zero_ref, sem, pad_sem):
    i = pl.program_id(0)
    n = pl.num_programs(0)
    tm, D = h_ref.shape
    rows = TOP_K * tm
    slot = lax.rem(i, 2)

    def row_copy(slot_):
        def make(src_row, dst_row, n_rows):
            return pltpu.make_async_copy(sorted_ref.at[slot_, _row_span(src_row, n_rows)],
                                         xs_ref.at[_row_span(dst_row, n_rows)], sem.at[slot_])
        return make

    @pl.when(i == 0)
    def _():
        zero_ref[...] = jnp.zeros_like(zero_ref)
        for wait in (False, True):
            def per_expert(e, c, wait=wait):
                def one(first, n_rows):
                    cp = pltpu.make_async_copy(zero_ref.at[_row_span(0, n_rows)],
                                               xs_ref.at[_row_span(misc_ref[1, e] + first, n_rows)], pad_sem)
                    cp.wait() if wait else cp.start()
                _pieces(misc_ref[2, e], MOE_ROWS // 2, one)
                return c
            lax.fori_loop(0, N_EXPERTS, per_expert, 0)

            def per_spare_half_block(hb, c, wait=wait):
                cp = pltpu.make_async_copy(zero_ref, xs_ref.at[_row_span(hb * (MOE_ROWS // 2), MOE_ROWS // 2)],
                                           pad_sem)
                cp.wait() if wait else cp.start()
                return c
            lax.fori_loop(2 * misc_ref[0, 0], 2 * (xs_ref.shape[0] // (MOE_ROWS * ROW_TILES)),
                          per_spare_half_block, 0)

    lp_t = lp_ref[...].T
    s_ix = lax.broadcasted_iota(I32, (rows, 1), 0).astype(F32)
    hit = s_ix == lp_t[0:1, :]
    for k in range(1, TOP_K):
        hit = hit | (s_ix == lp_t[k:k + 1, :])
    perm = jnp.where(hit, 1.0, 0.0).astype(BF16)
    _store_rows(sorted_ref.at[slot], _dot(perm, h_ref[...].astype(BF16)))

    _segment_copies(seg_ref, tm, row_copy(slot), wait=False)

    @pl.when(i > 0)
    def _():
        _segment_copies(segp_ref, tm, row_copy(1 - slot), wait=True)

    @pl.when(i == n - 1)
    def _():
        _segment_copies(seg_ref, tm, row_copy(slot), wait=True)


def _seg_spec(index_map):
    return pl.BlockSpec((3, 1, 1, LANES), index_map, memory_space=pltpu.SMEM)


def _dispatch(seg4, misc, lp, h1, cap):
    T, D = h1.shape
    assert D == ROW_TILES * LANES
    tm = TOKEN_TILE
    tok = lambda w: pl.BlockSpec((tm, w), lambda i: (i, 0))
    return pl.pallas_call(
        _dispatch_kernel, grid=(T // tm,),
        in_specs=[_seg_spec(lambda i: (0, i, 0, 0)), _seg_spec(lambda i: (0, jnp.maximum(i - 1, 0), 0, 0)),
                  pl.BlockSpec(memory_space=pltpu.SMEM), tok(LANES), tok(D)],
        out_specs=pl.BlockSpec(memory_space=pl.ANY),
        out_shape=jax.ShapeDtypeStruct((cap * ROW_TILES, LANES), F32),
        scratch_shapes=[pltpu.VMEM((2, TOP_K * tm * ROW_TILES, LANES), F32),
                        pltpu.VMEM((MOE_ROWS // 2 * ROW_TILES, LANES), F32),
                        pltpu.SemaphoreType.DMA((2,)), pltpu.SemaphoreType.DMA(())],
        compiler_params=_cparams("arbitrary"), name="dispatch",
    )(seg4, seg4, misc, lp, h1)


def _expert_kernel(blk_ref, used_ref, xs_ref, wgu_ref, bgu_ref, wd_ref, bd_ref, ys_ref, wgu_bf, wd_bf):
    i = pl.program_id(0)
    live = i < used_ref[0]

    @pl.when(live & ((i == 0) | (blk_ref[i] != blk_ref[jnp.maximum(i - 1, 0)])))
    def _():
        wgu_bf[...] = wgu_ref[0].astype(BF16)
        wd_bf[...] = wd_ref[0].astype(BF16)

    @pl.when(live)
    def _():
        xb = jnp.concatenate([_load_row_tile(xs_ref, MOE_ROWS, c).astype(BF16) for c in range(ROW_TILES)],
                             axis=1)
        gu = _dot(xb, wgu_bf[...]) + bgu_ref[0]
        g = jnp.minimum(gu[:, :D_FF], SWIGLU_LIMIT)
        lin = jnp.clip(gu[:, D_FF:], -SWIGLU_LIMIT, SWIGLU_LIMIT)
        act = g * jax.nn.sigmoid(SWIGLU_ALPHA * g) * (lin + 1.0)
        _store_rows(ys_ref, _dot(act.astype(BF16), wd_bf[...]) + bd_ref[0])

    @pl.when(pl.program_id(0) >= used_ref[0])
    def _():
        ys_ref[...] = jnp.zeros_like(ys_ref)


def _experts(blk_expert, n_used, xs, w_gate_up, b_gate_up, w_down, b_down):
    D = w_down.shape[2]
    n_blk = xs.shape[0] // (MOE_ROWS * ROW_TILES)
    E = w_gate_up.shape[0]
    live = lambda i, used: jnp.minimum(i, used[0] - 1)
    row = pl.BlockSpec((MOE_ROWS * ROW_TILES, LANES), lambda i, blk, used: (live(i, used), 0))
    by_e = lambda shape: pl.BlockSpec((1,) + shape, lambda i, blk, used: (blk[live(i, used)], 0, 0))
    grid_spec = pltpu.PrefetchScalarGridSpec(
        num_scalar_prefetch=2, grid=(n_blk,),
        in_specs=[row, by_e((D, 2 * D_FF)), by_e((1, 2 * D_FF)), by_e((D_FF, D)), by_e((1, D))],
        out_specs=pl.BlockSpec((MOE_ROWS * ROW_TILES, LANES), lambda i, blk, used: (i, 0)),
        scratch_shapes=[pltpu.VMEM((D, 2 * D_FF), BF16), pltpu.VMEM((D_FF, D), BF16)])
    return pl.pallas_call(
        _expert_kernel, grid_spec=grid_spec, out_shape=jax.ShapeDtypeStruct(xs.shape, F32),
        compiler_params=_cparams("arbitrary"), name="experts",
    )(blk_expert, n_used, xs, w_gate_up, b_gate_up.reshape(E, 1, 2 * D_FF), w_down, b_down.reshape(E, 1, D))


def _combine_kernel(seg_ref, segn_ref, lp_ref, w4_ref, h1_ref, g_ref, b_ref, ys_ref, o_ref, buf_ref, sem):
    i = pl.program_id(0)
    n = pl.num_programs(0)
    tm = h1_ref.shape[0]
    rows = TOP_K * tm
    slot = lax.rem(i, 2)

    def row_copy(slot_):
        def make(buf_row, ys_row, n_rows):
            return pltpu.make_async_copy(ys_ref.at[_row_span(ys_row, n_rows)],
                                         buf_ref.at[slot_, _row_span(buf_row, n_rows)], sem.at[slot_])
        return make

    @pl.when(i == 0)
    def _():
        _segment_copies(seg_ref, tm, row_copy(slot), wait=False)

    @pl.when(i + 1 < n)
    def _():
        _segment_copies(segn_ref, tm, row_copy(1 - slot), wait=False)

    _segment_copies(seg_ref, tm, row_copy(slot), wait=True)

    lp = lp_ref[...]
    s_ix = lax.broadcasted_iota(I32, (1, rows), 1).astype(F32)
    wmat = jnp.zeros((tm, rows), F32)
    for k in range(TOP_K):
        wmat = jnp.where(s_ix == lp[:, k:k + 1], w4_ref[:, k:k + 1], wmat)
    w_hi = wmat.astype(BF16)
    w_lo = (wmat - w_hi.astype(F32)).astype(BF16)
    cols = []
    for c in range(ROW_TILES):
        y = _load_row_tile(buf_ref.at[slot], rows, c)
        y_hi = y.astype(BF16)
        y_lo = (y - y_hi.astype(F32)).astype(BF16)
        cols.append(_dot(w_hi, y_hi) + _dot(w_hi, y_lo) + _dot(w_lo, y_hi))
    acc = DEEPNORM_ALPHA * h1_ref[...] + jnp.concatenate(cols, axis=1)
    o_ref[...] = _layer_norm(acc, g_ref[...], b_ref[...])


def _combine(seg4, lp, w4, h1, ln_g, ln_b, ys):
    T, D = h1.shape
    tm = TOKEN_TILE
    n_tile = T // tm
    return pl.pallas_call(
        _combine_kernel, grid=(n_tile,),
        in_specs=[_seg_spec(lambda i: (0, i, 0, 0)),
                  _seg_spec(lambda i: (0, jnp.minimum(i + 1, n_tile - 1), 0, 0)),
                  pl.BlockSpec((tm, LANES), lambda i: (i, 0)),
                  pl.BlockSpec((tm, LANES), lambda i: (i, 0)),
                  pl.BlockSpec((tm, D), lambda i: (i, 0)),
                  _full_spec((1, D)), _full_spec((1, D)),
                  pl.BlockSpec(memory_space=pl.ANY)],
        out_specs=pl.BlockSpec((tm, D), lambda i: (i, 0)),
        out_shape=jax.ShapeDtypeStruct((T, D), F32),
        scratch_shapes=[pltpu.VMEM((2, TOP_K * tm * ROW_TILES, LANES), F32), pltpu.SemaphoreType.DMA((2,))],
        compiler_params=_cparams("arbitrary"), name="combine",
    )(seg4, seg4, lp, w4, h1, ln_g, ln_b, ys)


def _rope_tables(positions):
    inv = ROPE_THETA ** (-jnp.arange(0, ROT_DIM, 2, dtype=F32) / ROT_DIM)
    ang = positions.astype(F32)[..., None] * inv
    cos_sin = jnp.concatenate([jnp.cos(ang), jnp.sin(ang)], axis=-1)
    half = ROT_DIM // 2
    spread = np.zeros((ROT_DIM, 3 * LANES), np.float32)
    unit = np.ones((1, LANES), np.float32)
    for lane in range(LANES):
        d = lane % HEAD_DIM
        if d < half:
            spread[d, lane] = 1.0
            spread[half + d, 2 * LANES + lane] = -1.0
            unit[0, lane] = 0.0
        elif d < ROT_DIM:
            spread[d - half, lane] = 1.0
            spread[d, LANES + lane] = 1.0
            unit[0, lane] = 0.0
    return cos_sin, jnp.asarray(spread, BF16), jnp.asarray(unit)


def _split_w_in(w_in):
    widths = (NSA_WIDTH,) + (KV_WIDTH,) * 6 + (NSA_HEADS * N_BRANCH, S5_WIDTH, MEM_WIDTH, N_BRANCH * D_MODEL)
    offs = [0]
    for w in widths:
        offs.append(offs[-1] + w)
    col = lambda i: w_in[:, offs[i]:offs[i + 1]]
    wq, kc, vc, ks, vs, kw, vw, wg, wu, wqm, wm = (col(i) for i in range(11))
    wk = jnp.concatenate([kc, ks, kw], axis=1)
    wv = jnp.concatenate([vc, vs, vw], axis=1)
    per_group = NSA_HPG * N_BRANCH
    wg_pad = jnp.zeros((w_in.shape[0], NSA_GROUPS * LANES), w_in.dtype)
    for g in range(NSA_GROUPS):
        wg_pad = wg_pad.at[:, g * LANES:g * LANES + per_group].set(wg[:, g * per_group:(g + 1) * per_group])
    return tuple(w.astype(BF16) for w in (wq, wk, wv, wg_pad, wu, wqm, wm))


def _compress_weights(w1):
    half = CMP_BLOCK // 2
    eye = jnp.eye(NSA_GROUPS, dtype=w1.dtype)

    def arrange(w_half):
        full = jnp.einsum('sdf,gh->sgdhf', w_half, eye)
        return full.reshape(half * NSA_GROUPS * HEAD_DIM, NSA_GROUPS * CMP_HIDDEN).astype(BF16)

    return (w1.reshape(CMP_BLOCK * HEAD_DIM, CMP_HIDDEN).astype(BF16), arrange(w1[:half]), arrange(w1[half:]))


def _s5_weights(a_re, a_im, log_dt, b_re, b_im, c_re, c_im):
    step = jnp.exp(log_dt)[:, None]
    mag = jnp.exp(a_re * step)
    ab_re, ab_im = mag * jnp.cos(a_im * step), mag * jnp.sin(a_im * step)
    den = a_re * a_re + a_im * a_im
    nr = ab_re - 1.0
    coef_re = (nr * a_re + ab_im * a_im) / den
    coef_im = (ab_im * a_re - nr * a_im) / den
    bb_re = coef_re[..., None] * b_re - coef_im[..., None] * b_im
    bb_im = coef_re[..., None] * b_im + coef_im[..., None] * b_re
    eye = jnp.eye(S5_GROUPS, dtype=F32)
    n_state = S5_GROUPS * S5_STATE
    n_tile = n_state // LANES
    in_per = n_tile // (S5_WIDTH // LANES)

    def in_map(bb):
        return jnp.einsum('gnp,gh->gphn', bb, eye).reshape(S5_WIDTH, n_state)

    def out_map(c):
        return jnp.einsum('gpn,gh->gnhp', c, eye).reshape(n_state, S5_WIDTH)

    bf_re, bf_im = in_map(bb_re), in_map(bb_im)
    cf_re, cf_im = out_map(c_re), out_map(-c_im)
    wb, wc = [], []
    for c in range(n_tile):
        i = c // in_per
        rs, cs = slice(i * LANES, (i + 1) * LANES), slice(c * LANES, (c + 1) * LANES)
        wb.append(jnp.concatenate([bf_re[rs, cs], bf_im[rs, cs]], axis=1))
        wc.append(jnp.concatenate([cf_re[cs, rs], cf_im[cs, rs]], axis=0))
    wb = jnp.stack(wb).astype(BF16)
    wc = jnp.stack(wc).astype(BF16)
    return wb, wc, ab_re.reshape(n_tile, LANES), ab_im.reshape(n_tile, LANES)


def _layer(x, mem, positions, ln_emb_g, ln_emb_b, w_in, pe_k, pe_v, w_kcmp1, w_kcmp2, w_vcmp1, w_vcmp2,
           s5_a_re, s5_a_im, s5_log_dt, s5_b_re, s5_b_im, s5_c_re, s5_c_im, s5_d,
           w_s5_glu, w_mem_kv, w_nsa_out, w_mem_out, w_o, ln1_g, ln1_b, w_router, b_router,
           w_gate_up, b_gate_up, w_down, b_down, ln2_g, ln2_b):
    B, L, D = x.shape
    T = B * L
    row = lambda v: v.reshape(1, -1)

    cos_sin, spread, unit = _rope_tables(positions)
    (q_hm, kc, vc, ks, vs, kw, vw, gates, u, qm, gm) = _inproj(
        x, row(ln_emb_g), row(ln_emb_b), cos_sin, spread, unit, *_split_w_in(w_in))

    n_chunk = L // CMP_STRIDE
    chunked = lambda t: t.reshape(B, n_chunk, CMP_STRIDE * KV_WIDTH)
    pe_rows = lambda pe: jnp.broadcast_to(pe.reshape(1, -1), (SUBLANES, CMP_BLOCK * HEAD_DIM)).astype(BF16)
    wk1f, wk1a, wk1b = _compress_weights(w_kcmp1)
    wv1f, wv1a, wv1b = _compress_weights(w_vcmp1)
    ck, cv = _compress(chunked(kc), chunked(vc), pe_rows(pe_k), pe_rows(pe_v), wk1f, wv1f,
                       wk1a, wk1b, wv1a, wv1b, w_kcmp2.astype(BF16), w_vcmp2.astype(BF16))

    per_sb = SEL_BLOCK // CMP_STRIDE
    c_ix = np.arange(n_chunk)[:, None]
    n_ix = np.arange(L // SEL_BLOCK)[None, :]
    w_score = jnp.asarray((c_ix // per_sb == n_ix).astype(np.float32)
                          + ((c_ix + 1) // per_sb == n_ix).astype(np.float32), BF16)
    o_nsa = _nsa(q_hm, ck, cv, ks, vs, kw, vw, gates, w_score)

    wb, wc, a_re, a_im = _s5_weights(s5_a_re, s5_a_im, s5_log_dt, s5_b_re, s5_b_im, s5_c_re, s5_c_im)
    gy = _s5(u, wb, wc, a_re, a_im, row(s5_d))

    k_mem, v_mem = _memkv(mem, w_mem_kv.astype(BF16))
    o_mem = _memattn(qm, k_mem, v_mem)

    pad_e = LANES - N_EXPERTS
    wr = jnp.pad(w_router, ((0, 0), (0, pad_e)))
    wr_hi = wr.astype(BF16)
    wr_lo = (wr - wr_hi.astype(F32)).astype(BF16)
    br = jnp.concatenate([b_router, jnp.full((pad_e,), -jnp.inf, F32)]).reshape(1, LANES)
    tm = TOKEN_TILE
    n_tile = T // tm
    strict_lower = lambda n: jnp.asarray(np.tril(np.ones((n, n), np.float32), -1), BF16)
    triu = jnp.asarray(np.triu(np.ones((LANES, LANES), np.float32)), BF16)
    striu = jnp.asarray(np.triu(np.ones((LANES, LANES), np.float32), 1), BF16)
    flat = lambda t: t.reshape(T, t.shape[-1])
    h1, lp, w4, cnt = _merge(
        flat(x), row(ln_emb_g), row(ln_emb_b), flat(o_nsa), flat(gy), flat(o_mem), flat(gm),
        w_nsa_out.astype(BF16), w_s5_glu.astype(BF16), w_mem_out.astype(BF16), w_o.astype(BF16),
        row(ln1_g), row(ln1_b), wr_hi, wr_lo, br, strict_lower(tm), striu)

    cap = (T * TOP_K + MOE_ROWS - 1) // MOE_ROWS * MOE_ROWS + N_EXPERTS * MOE_ROWS
    n_blk = cap // MOE_ROWS
    seg, blk_owner, misc = _slots(cnt[:, 0, :], triu, striu, strict_lower(n_tile), n_blk)
    seg4 = seg.reshape(3, n_tile, 1, LANES)
    blk_expert = blk_owner[:, 0]
    n_used = misc[0, :1]

    xs = _dispatch(seg4, misc, lp, h1, cap)
    ys = _experts(blk_expert, n_used, xs, w_gate_up, b_gate_up, w_down, b_down)
    out = _combine(seg4, lp, w4, h1, row(ln2_g), row(ln2_b), ys)
    return out.reshape(B, L, D)


def kernel(x, mem, positions, ln_emb_g, ln_emb_b, w_in, pe_k_cmp, pe_v_cmp, w_kcmp1, w_kcmp2, w_vcmp1, w_vcmp2, s5_a_re, s5_a_im, s5_log_dt, s5_b_re, s5_b_im, s5_c_re, s5_c_im, s5_d, w_s5_glu, w_mem_kv, w_nsa_out, w_mem_out, w_o, ln1_g, ln1_b, w_router, b_router, w_gate_up, b_gate_up, w_down, b_down, ln2_g, ln2_b):
    assert w_in.shape[0] == DEPTH
    l = 0
    return _layer(x, mem, positions, ln_emb_g, ln_emb_b, w_in[l], pe_k_cmp[l], pe_v_cmp[l], w_kcmp1[l],
                  w_kcmp2[l], w_vcmp1[l], w_vcmp2[l], s5_a_re[l], s5_a_im[l], s5_log_dt[l], s5_b_re[l],
                  s5_b_im[l], s5_c_re[l], s5_c_im[l], s5_d[l], w_s5_glu[l], w_mem_kv[l], w_nsa_out[l],
                  w_mem_out[l], w_o[l], ln1_g[l], ln1_b[l], w_router[l], b_router[l], w_gate_up[l],
                  b_gate_up[l], w_down[l], b_down[l], ln2_g[l], ln2_b[l])
```

```python
import functools
import math

import jax
import jax.numpy as jnp
import numpy as np
from jax import lax
from jax.experimental import pallas as pl
from jax.experimental.pallas import tpu as pltpu

F32 = jnp.float32
BF16 = jnp.bfloat16
I32 = jnp.int32

D_MODEL = 1024
NSA_HEADS = 8
NSA_GROUPS = 2
NSA_HPG = NSA_HEADS // NSA_GROUPS
HEAD_DIM = 64
NSA_WIDTH = NSA_HEADS * HEAD_DIM
KV_WIDTH = NSA_GROUPS * HEAD_DIM
CMP_BLOCK = 32
CMP_STRIDE = 16
CMP_HIDDEN = 128
SEL_BLOCK = 64
N_SEL = 16
WINDOW = 512
Q_BLOCK = 256
ROPE_THETA = 500000.0
ROT_DIM = HEAD_DIM // 4
S5_WIDTH = 512
S5_GROUP_DIM = 16
S5_GROUPS = S5_WIDTH // S5_GROUP_DIM
S5_STATE = 64
MEM_HEADS = 4
MEM_HEAD_DIM = 128
MEM_WIDTH = MEM_HEADS * MEM_HEAD_DIM
N_BRANCH = 3
N_EXPERTS = 32
TOP_K = 4
D_FF = 1024
SWIGLU_LIMIT = 7.0
SWIGLU_ALPHA = 1.702
LN_EPS = 1e-5
DEPTH = 1
DEEPNORM_ALPHA = (2 * DEPTH) ** 0.25

LANES = 128
SUBLANES = 8
VMEM_LIMIT_BYTES = 56 * 1024 * 1024

TOKEN_TILE = 256
SEL_KV_TILE = 512
S5_CHUNK = 512
S5_PITCH = S5_CHUNK + 8
MOE_ROWS = 512
NEG_BIG = -(2.0 ** 100)
Q_SCALE_LOG2 = HEAD_DIM ** -0.5 * math.log2(math.e)


def _cparams(*sem):
    return pltpu.CompilerParams(dimension_semantics=sem, vmem_limit_bytes=VMEM_LIMIT_BYTES)


def _dot(a, b):
    return jnp.dot(a, b, preferred_element_type=F32)


def _dot_nt(a, b):
    return lax.dot_general(a, b, (((1,), (1,)), ((), ())), preferred_element_type=F32)


def _layer_norm(x, g, b):
    mu = jnp.mean(x, axis=-1, keepdims=True)
    xc = x - mu
    var = jnp.mean(xc * xc, axis=-1, keepdims=True)
    return xc * lax.rsqrt(var + LN_EPS) * g + b


def _gelu_tanh(x):
    cdf = 0.5 * (1.0 + jnp.tanh(math.sqrt(2.0 / math.pi) * (x + 0.044715 * (x * x * x))))
    return x * cdf


def _masked_exp2(s, mask):
    s = jnp.where(mask, s, -jnp.inf)
    m = jnp.max(s, axis=-1, keepdims=True)
    m = jnp.where(m > -jnp.inf, m, 0.0)
    return jnp.exp2(s - m)


def _safe_recip(denom):
    return 1.0 / jnp.maximum(denom, jnp.finfo(F32).tiny)


def _split3(x):
    hi = x.astype(BF16)
    r1 = x - hi.astype(F32)
    mid = r1.astype(BF16)
    lo = (r1 - mid.astype(F32)).astype(BF16)
    return hi, mid, lo


def _full_spec(shape):
    nd = len(shape)
    return pl.BlockSpec(shape, lambda *_: (0,) * nd)


def _inproj_kernel(x_ref, g_ref, b_ref, cs_ref, spread_ref, unit_ref,
                   wq_ref, wk_ref, wv_ref, wg_ref, wu_ref, wqm_ref, wm_ref,
                   q_ref, kc_ref, vc_ref, ks_ref, vs_ref, kw_ref, vw_ref,
                   gate_ref, u_ref, qm_ref, gm_ref):
    h = _layer_norm(x_ref[0], g_ref[...], b_ref[...])
    hb = h.astype(BF16)
    tab = sum(_dot(part, spread_ref[...]) for part in _split3(cs_ref[0]))
    cos_t = tab[:, 0:LANES] + unit_ref[...]
    sin_a = tab[:, LANES:2 * LANES]
    sin_b = tab[:, 2 * LANES:3 * LANES]

    def rope(t):
        return (t * cos_t + pltpu.roll(t, ROT_DIM // 2, 1) * sin_a
                + pltpu.roll(t, LANES - ROT_DIM // 2, 1) * sin_b)

    q = _dot(hb, wq_ref[...])
    for c in range(NSA_WIDTH // LANES):
        qc = rope(q[:, c * LANES:(c + 1) * LANES]) * Q_SCALE_LOG2
        for hh in range(2):
            q_ref[0, 2 * c + hh] = qc[:, hh * HEAD_DIM:(hh + 1) * HEAD_DIM].astype(BF16)
    k3 = _dot(hb, wk_ref[...])
    kc = rope(k3[:, 0:LANES])
    ks = rope(k3[:, LANES:2 * LANES])
    kw = rope(k3[:, 2 * LANES:3 * LANES])
    v3 = _dot(hb, wv_ref[...])
    kc_ref[0] = kc.astype(BF16)
    vc_ref[0] = v3[:, 0:LANES].astype(BF16)
    tm = x_ref.shape[1]
    pos = pl.program_id(1) * tm + lax.broadcasted_iota(I32, (tm, LANES), 0)
    blk_hot = jnp.where(lax.broadcasted_iota(I32, (tm, LANES), 1) == pos // SEL_BLOCK, 1.0, 0.0)
    lane_pad = jnp.zeros((tm, LANES - HEAD_DIM), F32)
    ones_pad = jnp.where(lax.broadcasted_iota(I32, (tm, LANES - HEAD_DIM), 1) == 0, 1.0, 0.0)
    for g in range(NSA_GROUPS):
        sl = slice(g * HEAD_DIM, (g + 1) * HEAD_DIM)
        ks_ref[0, g] = jnp.concatenate([blk_hot, ks[:, sl], lane_pad], axis=1).astype(BF16)
        kw_ref[0, g] = kw[:, sl].astype(BF16)
        vs_ref[0, g] = jnp.concatenate([v3[:, LANES:2 * LANES][:, sl], ones_pad], axis=1).astype(BF16)
        vw_ref[0, g] = jnp.concatenate([v3[:, 2 * LANES:3 * LANES][:, sl], ones_pad], axis=1).astype(BF16)
    gate_ref[0] = jax.nn.sigmoid(_dot(hb, wg_ref[...]))
    u_ref[0] = _dot(hb, wu_ref[...])
    qm_ref[0] = _dot(hb, wqm_ref[...]).astype(BF16)
    gm_ref[0] = jax.nn.sigmoid(_dot(hb, wm_ref[...]))


def _inproj(x, ln_g, ln_b, cos_sin, spread, unit, wq, wk, wv, wg, wu, wqm, wm):
    B, L, D = x.shape
    tm = TOKEN_TILE
    grid = (B, L // tm)
    tok = lambda w: pl.BlockSpec((1, tm, w), lambda b, i: (b, i, 0))
    head = lambda n, w=HEAD_DIM: pl.BlockSpec((1, n, tm, w), lambda b, i: (b, 0, i, 0))
    in_specs = [tok(D), _full_spec((1, D)), _full_spec((1, D)), tok(ROT_DIM), _full_spec(spread.shape),
                _full_spec(unit.shape)]
    in_specs += [_full_spec(w.shape) for w in (wq, wk, wv, wg, wu, wqm, wm)]
    sd = jax.ShapeDtypeStruct
    out_shape = [
        sd((B, NSA_HEADS, L, HEAD_DIM), BF16),
        sd((B, L, KV_WIDTH), BF16), sd((B, L, KV_WIDTH), BF16),
        sd((B, NSA_GROUPS, L, 2 * LANES), BF16), sd((B, NSA_GROUPS, L, LANES), BF16),
        sd((B, NSA_GROUPS, L, HEAD_DIM), BF16), sd((B, NSA_GROUPS, L, LANES), BF16),
        sd((B, L, NSA_GROUPS * LANES), F32),
        sd((B, L, S5_WIDTH), F32),
        sd((B, L, MEM_WIDTH), BF16),
        sd((B, L, N_BRANCH * D), F32),
    ]
    out_specs = [head(NSA_HEADS), tok(KV_WIDTH), tok(KV_WIDTH), head(NSA_GROUPS, 2 * LANES),
                 head(NSA_GROUPS, LANES), head(NSA_GROUPS), head(NSA_GROUPS, LANES),
                 tok(NSA_GROUPS * LANES), tok(S5_WIDTH),
                 tok(MEM_WIDTH), tok(N_BRANCH * D)]
    return pl.pallas_call(
        _inproj_kernel, grid=grid, in_specs=in_specs, out_specs=out_specs, out_shape=out_shape,
        compiler_params=_cparams("parallel", "parallel"), name="inproj",
    )(x, ln_g, ln_b, cos_sin, spread, unit, wq, wk, wv, wg, wu, wqm, wm)


def _compress_kernel(kc_ref, vc_ref, pek_ref, pev_ref, wk1f_ref, wv1f_ref,
                     wk1a_ref, wk1b_ref, wv1a_ref, wv1b_ref, wk2_ref, wv2_ref, ck_ref, cv_ref):
    n_chunk = kc_ref.shape[1]
    row = lax.broadcasted_iota(I32, (n_chunk, 1), 0)

    def one(x_ref, pe_ref, w1f_ref, w1a_ref, w1b_ref, w2_ref, o_ref):
        x = x_ref[0]
        first = _dot(x, w1a_ref[...])
        second = _dot(x, w1b_ref[...])
        second = pltpu.roll(second, n_chunk - 1, 0)
        pe_term = _dot(pe_ref[...], w1f_ref[...])[0:1]
        pe_term = jnp.concatenate([pe_term] * NSA_GROUPS, axis=1)
        hid = _gelu_tanh(first + second + pe_term).astype(BF16)
        for g in range(NSA_GROUPS):
            o = _dot(hid[:, g * CMP_HIDDEN:(g + 1) * CMP_HIDDEN], w2_ref[...])
            o_ref[0, g] = jnp.where(row < n_chunk - 1, o, 0.0).astype(BF16)

    one(kc_ref, pek_ref, wk1f_ref, wk1a_ref, wk1b_ref, wk2_ref, ck_ref)
    one(vc_ref, pev_ref, wv1f_ref, wv1a_ref, wv1b_ref, wv2_ref, cv_ref)


def _compress(kc_r, vc_r, pek, pev, wk1f, wv1f, wk1a, wk1b, wv1a, wv1b, wk2, wv2):
    B, n_chunk, width = kc_r.shape
    blk = pl.BlockSpec((1, n_chunk, width), lambda b: (b, 0, 0))
    out = pl.BlockSpec((1, NSA_GROUPS, n_chunk, HEAD_DIM), lambda b: (b, 0, 0, 0))
    ws = [pek, pev, wk1f, wv1f, wk1a, wk1b, wv1a, wv1b, wk2, wv2]
    sd = jax.ShapeDtypeStruct((B, NSA_GROUPS, n_chunk, HEAD_DIM), BF16)
    return pl.pallas_call(
        _compress_kernel, grid=(B,), in_specs=[blk, blk] + [_full_spec(w.shape) for w in ws],
        out_specs=[out, out], out_shape=[sd, sd], compiler_params=_cparams("parallel"), name="compress",
    )(kc_r, vc_r, *ws)


def _nsa_kernel(q_ref, ck_ref, cv_ref, ks_ref, vs_ref, kw_ref, vw_ref, gate_ref, wsc_ref, o_ref):
    seq_len = ks_ref.shape[2]
    n_cmp = ck_ref.shape[2]
    n_sb = seq_len // SEL_BLOCK
    n_sel = min(N_SEL, n_sb)
    rows = NSA_HPG * Q_BLOCK
    groups = range(NSA_GROUPS)
    q0 = pl.program_id(1) * Q_BLOCK
    t1 = q0 + lax.broadcasted_iota(I32, (Q_BLOCK, 1), 0)
    t4 = jnp.concatenate([t1] * NSA_HPG, axis=0)
    tk = SEL_KV_TILE

    def front(g):
        q = q_ref[0, g * NSA_HPG:(g + 1) * NSA_HPG].reshape(rows, HEAD_DIM)

        s = _dot_nt(q, ck_ref[0, g])
        c_end = lax.broadcasted_iota(I32, (1, n_cmp), 1) * CMP_STRIDE + (CMP_BLOCK - 1)
        e = _masked_exp2(s, c_end <= t4)
        p_cmp = e * _safe_recip(jnp.sum(e, axis=-1, keepdims=True))
        o_cmp = _dot(p_cmp.astype(BF16), cv_ref[0, g])

        imp = p_cmp[0:Q_BLOCK]
        for hh in range(1, NSA_HPG):
            imp = imp + p_cmp[hh * Q_BLOCK:(hh + 1) * Q_BLOCK]
        w_sc = wsc_ref[...]
        score = sum(_dot(part, w_sc) for part in _split3(imp))
        score_t = score.T
        jb = lax.broadcasted_iota(I32, (n_sb, Q_BLOCK), 0)
        tb = (q0 + lax.broadcasted_iota(I32, (1, Q_BLOCK), 1)) // SEL_BLOCK
        forced = (jb == 0) | (jb == tb) | (jb == tb - 1)
        work = jnp.where(forced | (jb > tb), -jnp.inf, score_t)
        bias_t = jnp.where(forced, 0.0, NEG_BIG)
        jbf = jb.astype(F32)
        for _ in range(n_sel - 3):
            m = jnp.max(work, axis=0, keepdims=True)
            idx = jnp.min(jnp.where(work == m, jbf, float(n_sb)), axis=0, keepdims=True)
            pick = jbf == idx
            bias_t = jnp.where(pick, 0.0, bias_t)
            work = jnp.where(pick, -jnp.inf, work)
        sel_bias = bias_t.T
        if n_sb < LANES:
            sel_bias = jnp.concatenate([sel_bias, jnp.zeros((Q_BLOCK, LANES - n_sb), F32)], axis=1)

        span = WINDOW + Q_BLOCK
        w0 = pl.multiple_of(jnp.maximum(q0 - WINDOW, 0), Q_BLOCK)
        s = _dot_nt(q, kw_ref[0, g, pl.ds(w0, span), :])
        diff = t4 - (w0 + lax.broadcasted_iota(I32, (1, span), 1))
        e = _masked_exp2(s, (diff >= 0) & (diff < WINDOW))
        o_win = _dot(e.astype(BF16), vw_ref[0, g, pl.ds(w0, span), :])
        o_win = o_win[:, :HEAD_DIM] * _safe_recip(o_win[:, HEAD_DIM:HEAD_DIM + 1])

        q_aug = jnp.concatenate([jnp.concatenate([sel_bias.astype(BF16)] * NSA_HPG, axis=0), q,
                                 jnp.zeros((rows, LANES - HEAD_DIM), BF16)], axis=1)
        return q_aug, o_cmp, o_win

    fronts = [front(g) for g in groups]

    def sel_tile(g, j, carry, causal):
        m_run, acc = carry
        k0 = pl.multiple_of(j * tk, tk)
        sc = _dot_nt(fronts[g][0], ks_ref[0, g, pl.ds(k0, tk), :])
        if causal:
            kpos = k0 + lax.broadcasted_iota(I32, (1, tk), 1)
            sc = jnp.where(kpos <= t4, sc, NEG_BIG)
        m_new = jnp.maximum(m_run, jnp.max(sc, axis=-1, keepdims=True))
        p = jnp.exp2(sc - m_new)
        acc_new = jnp.exp2(m_run - m_new) * acc + _dot(p.astype(BF16), vs_ref[0, g, pl.ds(k0, tk), :])
        return m_new, acc_new

    def sel_pair(jj, carries, causal):
        return tuple(sel_tile(g, 2 * jj + 1, sel_tile(g, 2 * jj, carries[g], causal), causal) for g in groups)

    init = tuple((jnp.full((rows, 1), NEG_BIG, F32), jnp.zeros((rows, LANES), F32)) for _ in groups)
    last_pair = (q0 // tk) // 2
    carries = lax.fori_loop(0, last_pair, functools.partial(sel_pair, causal=False), init)
    carries = sel_pair(last_pair, carries, True)

    outs = []
    for g in groups:
        _, o_cmp, o_win = fronts[g]
        acc = carries[g][1]
        o_sel = acc[:, :HEAD_DIM] * (1.0 / acc[:, HEAD_DIM:HEAD_DIM + 1])
        gt = gate_ref[0, :, g * LANES:(g + 1) * LANES]
        for hh in range(NSA_HPG):
            sl = slice(hh * Q_BLOCK, (hh + 1) * Q_BLOCK)
            c = hh * N_BRANCH
            outs.append(o_cmp[sl] * gt[:, c:c + 1] + o_sel[sl] * gt[:, c + 1:c + 2]
                        + o_win[sl] * gt[:, c + 2:c + 3])
    o_ref[0] = jnp.concatenate(outs, axis=1).astype(BF16)


def _nsa(q_hm, ck, cv, ks, vs, kw, vw, gates, w_score):
    B, _, L, _ = q_hm.shape
    assert L // SEL_BLOCK <= LANES and (L // SEL_KV_TILE) % 2 == 0 and L >= WINDOW + Q_BLOCK
    n_cmp = ck.shape[2]
    grid = (B, L // Q_BLOCK)
    qspec = pl.BlockSpec((1, NSA_HEADS, Q_BLOCK, HEAD_DIM), lambda b, i: (b, 0, i, 0))
    cspec = pl.BlockSpec((1, NSA_GROUPS, n_cmp, HEAD_DIM), lambda b, i: (b, 0, 0, 0))
    kvspec = lambda w: pl.BlockSpec((1, NSA_GROUPS, L, w), lambda b, i: (b, 0, 0, 0),
                                    pipeline_mode=pl.Buffered(1))
    gspec = pl.BlockSpec((1, Q_BLOCK, NSA_GROUPS * LANES), lambda b, i: (b, i, 0))
    ospec = pl.BlockSpec((1, Q_BLOCK, NSA_WIDTH), lambda b, i: (b, i, 0))
    return pl.pallas_call(
        _nsa_kernel, grid=grid,
        in_specs=[qspec, cspec, cspec, kvspec(2 * LANES), kvspec(LANES), kvspec(HEAD_DIM), kvspec(LANES), gspec,
                  _full_spec(w_score.shape)],
        out_specs=ospec, out_shape=jax.ShapeDtypeStruct((B, L, NSA_WIDTH), BF16),
        compiler_params=_cparams("parallel", "arbitrary"), name="nsa",
    )(q_hm, ck, cv, ks, vs, kw, vw, gates, w_score)


def _s5_kernel(u_ref, wb_ref, wc_ref, are_ref, aim_ref, d_ref, y_ref, sre_ref, sim_ref, carry_ref):
    n_b, chunk, _ = u_ref.shape
    n_tile = wb_ref.shape[0]
    in_per = n_tile // (S5_WIDTH // LANES)
    pitch = S5_PITCH

    @pl.when(pl.program_id(0) == 0)
    def _():
        carry_ref[...] = jnp.zeros_like(carry_ref)

    for b in range(n_b):
        for c in range(n_tile):
            i = c // in_per
            ub = u_ref[b, :, i * LANES:(i + 1) * LANES].astype(BF16)
            r = _dot(ub, wb_ref[c])
            sre_ref[b, c * pitch:c * pitch + chunk, :] = r[:, :LANES]
            sim_ref[b, c * pitch:c * pitch + chunk, :] = r[:, LANES:]

    a_re, a_im = are_ref[...], aim_ref[...]

    def step(t, carry):
        out = []
        for b in range(n_b):
            s_re, s_im = carry[2 * b], carry[2 * b + 1]
            rows = pl.ds(t, n_tile, stride=pitch)
            n_re = a_re * s_re - a_im * s_im + sre_ref[b, rows, :]
            n_im = a_re * s_im + a_im * s_re + sim_ref[b, rows, :]
            sre_ref[b, rows, :] = n_re
            sim_ref[b, rows, :] = n_im
            out += [n_re, n_im]
        return tuple(out)

    init = tuple(carry_ref[i] for i in range(2 * n_b))
    fin = lax.fori_loop(0, chunk, step, init, unroll=8)
    for i in range(2 * n_b):
        carry_ref[i] = fin[i]

    for b in range(n_b):
        for o in range(S5_WIDTH // LANES):
            acc = jnp.zeros((chunk, LANES), F32)
            for c in range(o * in_per, (o + 1) * in_per):
                rows = slice(c * pitch, c * pitch + chunk)
                acc = acc + _dot(sre_ref[b, rows, :].astype(BF16), wc_ref[c, :LANES])
                acc = acc + _dot(sim_ref[b, rows, :].astype(BF16), wc_ref[c, LANES:])
            lanes = slice(o * LANES, (o + 1) * LANES)
            y = acc + d_ref[:, lanes] * u_ref[b, :, lanes]
            y_ref[b, :, lanes] = _gelu_tanh(y).astype(BF16)


def _s5(u, wb, wc, a_re, a_im, d_skip):
    B, L, W = u.shape
    chunk = S5_CHUNK
    n_tile = wb.shape[0]
    blk = pl.BlockSpec((B, chunk, W), lambda i: (0, i, 0))
    slab = pltpu.VMEM((B, n_tile * S5_PITCH, LANES), F32)
    return pl.pallas_call(
        _s5_kernel, grid=(L // chunk,),
        in_specs=[blk] + [_full_spec(w.shape) for w in (wb, wc, a_re, a_im, d_skip)],
        out_specs=blk, out_shape=jax.ShapeDtypeStruct((B, L, W), BF16),
        scratch_shapes=[slab, slab, pltpu.VMEM((2 * B, n_tile, LANES), F32)],
        compiler_params=_cparams("arbitrary"), name="s5",
    )(u, wb, wc, a_re, a_im, d_skip)


def _memkv_kernel(mem_ref, w_ref, k_ref, v_ref):
    kv = _dot(mem_ref[0].astype(BF16), w_ref[...])
    k_ref[0] = kv[:, :MEM_WIDTH].astype(BF16)
    v_ref[0] = kv[:, MEM_WIDTH:].astype(BF16)


def _memkv(mem, w_kv):
    B, M, D = mem.shape
    out = pl.BlockSpec((1, M, MEM_WIDTH), lambda b: (b, 0, 0))
    sd = jax.ShapeDtypeStruct((B, M, MEM_WIDTH), BF16)
    return pl.pallas_call(
        _memkv_kernel, grid=(B,),
        in_specs=[pl.BlockSpec((1, M, D), lambda b: (b, 0, 0)), _full_spec(w_kv.shape)],
        out_specs=[out, out], out_shape=[sd, sd], compiler_params=_cparams("parallel"), name="memkv",
    )(mem, w_kv)


def _memattn_kernel(q_ref, k_ref, v_ref, o_ref):
    outs = []
    for h in range(MEM_HEADS):
        sl = slice(h * MEM_HEAD_DIM, (h + 1) * MEM_HEAD_DIM)
        s = _dot_nt(q_ref[0, :, sl], k_ref[0, :, sl]) * (MEM_HEAD_DIM ** -0.5)
        m = jnp.max(s, axis=-1, keepdims=True)
        e = jnp.exp(s - m)
        p = e / jnp.sum(e, axis=-1, keepdims=True)
        outs.append(_dot(p.astype(BF16), v_ref[0, :, sl]))
    o_ref[0] = jnp.concatenate(outs, axis=1).astype(BF16)


def _memattn(qm, k, v):
    B, L, W = qm.shape
    M = k.shape[1]
    tm = TOKEN_TILE
    tok = pl.BlockSpec((1, tm, W), lambda b, i: (b, i, 0))
    kv = pl.BlockSpec((1, M, W), lambda b, i: (b, 0, 0))
    return pl.pallas_call(
        _memattn_kernel, grid=(B, L // tm), in_specs=[tok, kv, kv], out_specs=tok,
        out_shape=jax.ShapeDtypeStruct((B, L, W), BF16),
        compiler_params=_cparams("parallel", "parallel"), name="memattn",
    )(qm, k, v)


def _merge_kernel(x_ref, lng_ref, lnb_ref, on_ref, gy_ref, om_ref, gm_ref,
                  wn_ref, wglu_ref, wmo_ref, wo_ref, l1g_ref, l1b_ref,
                  wrh_ref, wrl_ref, br_ref, tri_ref, striu_ref,
                  h1_ref, lp_ref, w4_ref, cnt_ref):
    D = x_ref.shape[1]
    tm = x_ref.shape[0]
    h =_layer_norm(x_ref[...], lng_ref[...], lnb_ref[...])
    y_nsa = _dot(on_ref[...], wn_ref[...])
    glu = _dot(gy_ref[...], wglu_ref[...])
    y_s5 = glu[:, :D] * jax.nn.sigmoid(glu[:, D:])
    y_mem = _dot(om_ref[...], wmo_ref[...])
    merged = gm_ref[:, 0:D] * y_nsa + gm_ref[:, D:2 * D] * y_s5 + gm_ref[:, 2 * D:3 * D] * y_mem
    mix = _dot(merged.astype(BF16), wo_ref[...])
    h1 = _layer_norm(DEEPNORM_ALPHA * h + mix, l1g_ref[...], l1b_ref[...])
    h1_ref[...] = h1

    hh = h1.astype(BF16)
    hl = (h1 - hh.astype(F32)).astype(BF16)
    logits = _dot(hh, wrh_ref[...]) + _dot(hh, wrl_ref[...]) + _dot(hl, wrh_ref[...]) + br_ref[...]
    lane = lax.broadcasted_iota(I32, (tm, LANES), 1)
    lane_f = lane.astype(F32)
    work = logits
    multi = jnp.zeros((tm, LANES), F32)
    vals, picks = [], []
    for _ in range(TOP_K):
        m = jnp.max(work, axis=-1, keepdims=True)
        idx = jnp.min(jnp.where(work == m, lane_f, float(LANES)), axis=-1, keepdims=True)
        pick = lane_f == idx
        vals.append(m)
        picks.append((pick, idx))
        multi = jnp.where(pick, 1.0, multi)
        work = jnp.where(pick, -jnp.inf, work)
    es = [jnp.exp(v - vals[0]) for v in vals]
    den = es[0] + es[1] + es[2] + es[3]
    cnt = jnp.broadcast_to(jnp.sum(multi, axis=0, keepdims=True), (SUBLANES, LANES))
    cnt_ref[0] = cnt
    lower = _dot(cnt.astype(BF16), striu_ref[...])[0:1]
    pos = lower + _dot(tri_ref[...], multi.astype(BF16))
    lp = jnp.full((tm, LANES), -1.0, F32)
    w4 = jnp.zeros((tm, LANES), F32)
    for k in range(TOP_K):
        pick, _ = picks[k]
        lp = jnp.where(lane == k, jnp.sum(jnp.where(pick, pos, 0.0), axis=-1, keepdims=True), lp)
        w4 = jnp.where(lane == k, es[k] / den, w4)
    lp_ref[...] = lp
    w4_ref[...] = w4


def _merge(x2, lng, lnb, o_nsa, gy, om, gm, wn, wglu, wmo, wo, l1g, l1b, wrh, wrl, br, tri, striu):
    T, D = x2.shape
    tm = TOKEN_TILE
    tok = lambda w: pl.BlockSpec((tm, w), lambda i: (i, 0))
    ws = [wn, wglu, wmo, wo, l1g, l1b, wrh, wrl, br, tri, striu]
    sd = jax.ShapeDtypeStruct
    lane_out = sd((T, LANES), F32)
    return pl.pallas_call(
        _merge_kernel, grid=(T // tm,),
        in_specs=[tok(D), _full_spec((1, D)), _full_spec((1, D)), tok(NSA_WIDTH), tok(S5_WIDTH),
                  tok(MEM_WIDTH), tok(N_BRANCH * D)] + [_full_spec(w.shape) for w in ws],
        out_specs=[tok(D), tok(LANES), tok(LANES), pl.BlockSpec((1, SUBLANES, LANES), lambda i: (i, 0, 0))],
        out_shape=[sd((T, D), F32), lane_out, lane_out, sd((T // tm, SUBLANES, LANES), F32)],
        compiler_params=_cparams("parallel"), name="merge",
    )(x2, lng, lnb, o_nsa, gy, om, gm, *ws)


def _slots_kernel(cnt_ref, triu_ref, striu_ref, tril_ref, seg_ref, blk_ref, misc_ref):
    n_blk = blk_ref.shape[0]
    cnt = cnt_ref[...]
    cnt_b = cnt.astype(BF16)
    total = jnp.sum(cnt, axis=0, keepdims=True)
    nblk_e = jnp.floor((total + (MOE_ROWS - 1)) * (1.0 / MOE_ROWS))
    nblk_8 = jnp.broadcast_to(nblk_e, (SUBLANES, LANES))
    end_b = _dot(nblk_8.astype(BF16), triu_ref[...])
    start_rows = (end_b - nblk_8)[0:1] * MOE_ROWS
    dst = start_rows + _dot(tril_ref[...], cnt_b)
    off = _dot(cnt_b, striu_ref[...])
    seg_ref[0] = cnt.astype(I32)
    seg_ref[1] = off.astype(I32)
    seg_ref[2] = dst.astype(I32)
    blk_i = lax.broadcasted_iota(I32, (n_blk, LANES), 0).astype(F32)
    lane_b = lax.broadcasted_iota(I32, (n_blk, LANES), 1)
    ended = jnp.where((end_b[0:1] <= blk_i) & (lane_b < N_EXPERTS), 1.0, 0.0)
    owner = jnp.minimum(jnp.sum(ended, axis=-1, keepdims=True), float(N_EXPERTS - 1))
    blk_ref[...] = jnp.broadcast_to(owner, (n_blk, LANES)).astype(I32)
    lane8 = lax.broadcasted_iota(I32, (SUBLANES, LANES), 1)
    row8 = lax.broadcasted_iota(I32, (SUBLANES, LANES), 0)
    used = jnp.sum(jnp.where(lane8 == N_EXPERTS - 1, end_b, 0.0), axis=-1, keepdims=True)
    misc = jnp.where(row8 == 0, used, jnp.where(row8 == 1, start_rows + total, nblk_e * MOE_ROWS - total))
    misc_ref[...] = misc.astype(I32)


def _slots(cnt, triu, striu, tril, n_blk):
    n_tile = cnt.shape[0]
    sd = jax.ShapeDtypeStruct
    return pl.pallas_call(
        _slots_kernel, grid=(1,),
        in_specs=[_full_spec(cnt.shape), _full_spec(triu.shape), _full_spec(striu.shape), _full_spec(tril.shape)],
        out_specs=[_full_spec((3, n_tile, LANES)), _full_spec((n_blk, LANES)), _full_spec((SUBLANES, LANES))],
        out_shape=[sd((3, n_tile, LANES), I32), sd((n_blk, LANES), I32), sd((SUBLANES, LANES), I32)],
        compiler_params=_cparams("arbitrary"), name="slots",
    )(cnt, triu, striu, tril)


ROW_TILES = D_MODEL // LANES


def _row_span(row, n_rows):
    start = row * ROW_TILES
    if not isinstance(start, int):
        start = pl.multiple_of(start, ROW_TILES)
    return pl.ds(start, n_rows * ROW_TILES)


def _store_rows(ref, val):
    for c in range(ROW_TILES):
        ref[pl.ds(c, val.shape[0], stride=ROW_TILES), :] = val[:, c * LANES:(c + 1) * LANES]


def _load_row_tile(ref, n_rows, c):
    return ref[pl.ds(c, n_rows, stride=ROW_TILES), :]


def _pieces(count, max_rows, fn):
    p = max_rows
    while p >= 1:
        def piece(p=p):
            fn(count & (-2 * p), p)
        pl.when((count & p) != 0)(piece)
        p //= 2


def _segment_copies(seg_ref, max_rows, make_copy, wait):
    def per_expert(e, c):
        cnt, off, dst = seg_ref[0, 0, 0, e], seg_ref[1, 0, 0, e], seg_ref[2, 0, 0, e]

        def one(first, rows):
            cp = make_copy(off + first, dst + first, rows)
            cp.wait() if wait else cp.start()

        _pieces(cnt, max_rows, one)
        return c

    lax.fori_loop(0, N_EXPERTS, per_expert, 0)


def _dispatch_kernel(seg_ref, segp_ref, misc_ref, lp_ref, h_ref, xs_ref, sorted_ref, zero_ref, sem, pad_sem):
    i = pl.program_id(0)
    n = pl.num_programs(0)
    tm, D = h_ref.shape
    rows = TOP_K * tm
    slot = lax.rem(i, 2)

    def row_copy(slot_):
        def make(src_row, dst_row, n_rows):
            return pltpu.make_async_copy(sorted_ref.at[slot_, _row_span(src_row, n_rows)],
                                         xs_ref.at[_row_span(dst_row, n_rows)], sem.at[slot_])
        return make

    @pl.when(i == 0)
    def _():
        zero_ref[...] = jnp.zeros_like(zero_ref)
        for wait in (False, True):
            def per_expert(e, c, wait=wait):
                def one(first, n_rows):
                    cp = pltpu.make_async_copy(zero_ref.at[_row_span(0, n_rows)],
                                               xs_ref.at[_row_span(misc_ref[1, e] + first, n_rows)], pad_sem)
                    cp.wait() if wait else cp.start()
                _pieces(misc_ref[2, e], MOE_ROWS // 2, one)
                return c
            lax.fori_loop(0, N_EXPERTS, per_expert, 0)

            def per_spare_half_block(hb, c, wait=wait):
                cp = pltpu.make_async_copy(zero_ref, xs_ref.at[_row_span(hb * (MOE_ROWS // 2), MOE_ROWS // 2)],
                                           pad_sem)
                cp.wait() if wait else cp.start()
                return c
            lax.fori_loop(2 * misc_ref[0, 0], 2 * (xs_ref.shape[0] // (MOE_ROWS * ROW_TILES)),
                          per_spare_half_block, 0)

    lp_t = lp_ref[...].T
    s_ix = lax.broadcasted_iota(I32, (rows, 1), 0).astype(F32)
    hit = s_ix == lp_t[0:1, :]
    for k in range(1, TOP_K):
        hit = hit | (s_ix == lp_t[k:k + 1, :])
    perm = jnp.where(hit, 1.0, 0.0).astype(BF16)
    _store_rows(sorted_ref.at[slot], _dot(perm, h_ref[...].astype(BF16)))

    _segment_copies(seg_ref, tm, row_copy(slot), wait=False)

    @pl.when(i > 0)
    def _():
        _segment_copies(segp_ref, tm, row_copy(1 - slot), wait=True)

    @pl.when(i == n - 1)
    def _():
        _segment_copies(seg_ref, tm, row_copy(slot), wait=True)


def _seg_spec(index_map):
    return pl.BlockSpec((3, 1, 1, LANES), index_map, memory_space=pltpu.SMEM)


def _dispatch(seg4, misc, lp, h1, cap):
    T, D = h1.shape
    assert D == ROW_TILES * LANES
    tm = TOKEN_TILE
    tok = lambda w: pl.BlockSpec((tm, w), lambda i: (i, 0))
    return pl.pallas_call(
        _dispatch_kernel, grid=(T // tm,),
        in_specs=[_seg_spec(lambda i: (0, i, 0, 0)), _seg_spec(lambda i: (0, jnp.maximum(i - 1, 0), 0, 0)),
                  pl.BlockSpec(memory_space=pltpu.SMEM), tok(LANES), tok(D)],
        out_specs=pl.BlockSpec(memory_space=pl.ANY),
        out_shape=jax.ShapeDtypeStruct((cap * ROW_TILES, LANES), F32),
        scratch_shapes=[pltpu.VMEM((2, TOP_K * tm * ROW_TILES, LANES), F32),
                        pltpu.VMEM((MOE_ROWS // 2 * ROW_TILES, LANES), F32),
                        pltpu.SemaphoreType.DMA((2,)), pltpu.SemaphoreType.DMA(())],
        compiler_params=_cparams("arbitrary"), name="dispatch",
    )(seg4, seg4, misc, lp, h1)


def _expert_kernel(blk_ref, used_ref, xs_ref, wgu_ref, bgu_ref, wd_ref, bd_ref, ys_ref, wgu_bf, wd_bf):
    i = pl.program_id(0)
    live = i < used_ref[0]

    @pl.when(live & ((i == 0) | (blk_ref[i] != blk_ref[jnp.maximum(i - 1, 0)])))
    def _():
        wgu_bf[...] = wgu_ref[0].astype(BF16)
        wd_bf[...] = wd_ref[0].astype(BF16)

    @pl.when(live)
    def _():
        xb = jnp.concatenate([_load_row_tile(xs_ref, MOE_ROWS, c).astype(BF16) for c in range(ROW_TILES)],
                             axis=1)
        gu = _dot(xb, wgu_bf[...]) + bgu_ref[0]
        g = jnp.minimum(gu[:, :D_FF], SWIGLU_LIMIT)
        lin = jnp.clip(gu[:, D_FF:], -SWIGLU_LIMIT, SWIGLU_LIMIT)
        act = g * jax.nn.sigmoid(SWIGLU_ALPHA * g) * (lin + 1.0)
        _store_rows(ys_ref, _dot(act.astype(BF16), wd_bf[...]) + bd_ref[0])

    @pl.when(pl.program_id(0) >= used_ref[0])
    def _():
        ys_ref[...] = jnp.zeros_like(ys_ref)


def _experts(blk_expert, n_used, xs, w_gate_up, b_gate_up, w_down, b_down):
    D = w_down.shape[2]
    n_blk = xs.shape[0] // (MOE_ROWS * ROW_TILES)
    E = w_gate_up.shape[0]
    live = lambda i, used: jnp.minimum(i, used[0] - 1)
    row = pl.BlockSpec((MOE_ROWS * ROW_TILES, LANES), lambda i, blk, used: (live(i, used), 0))
    by_e = lambda shape: pl.BlockSpec((1,) + shape, lambda i, blk, used: (blk[live(i, used)], 0, 0))
    grid_spec = pltpu.PrefetchScalarGridSpec(
        num_scalar_prefetch=2, grid=(n_blk,),
        in_specs=[row, by_e((D, 2 * D_FF)), by_e((1, 2 * D_FF)), by_e((D_FF, D)), by_e((1, D))],
        out_specs=pl.BlockSpec((MOE_ROWS * ROW_TILES, LANES), lambda i, blk, used: (i, 0)),
        scratch_shapes=[pltpu.VMEM((D, 2 * D_FF), BF16), pltpu.VMEM((D_FF, D), BF16)])
    return pl.pallas_call(
        _expert_kernel, grid_spec=grid_spec, out_shape=jax.ShapeDtypeStruct(xs.shape, F32),
        compiler_params=_cparams("arbitrary"), name="experts",
    )(blk_expert, n_used, xs, w_gate_up, b_gate_up.reshape(E, 1, 2 * D_FF), w_down, b_down.reshape(E, 1, D))


def _combine_kernel(seg_ref, segn_ref, lp_ref, w4_ref, h1_ref, g_ref, b_ref, ys_ref, o_ref, buf_ref, sem):
    i = pl.program_id(0)
    n = pl.num_programs(0)
    tm = h1_ref.shape[0]
    rows = TOP_K * tm
    slot = lax.rem(i, 2)

    def row_copy(slot_):
        def make(buf_row, ys_row, n_rows):
            return pltpu.make_async_copy(ys_ref.at[_row_span(ys_row, n_rows)],
                                         buf_ref.at[slot_, _row_span(buf_row, n_rows)], sem.at[slot_])
        return make

    @pl.when(i == 0)
    def _():
        _segment_copies(seg_ref, tm, row_copy(slot), wait=False)

    @pl.when(i + 1 < n)
    def _():
        _segment_copies(segn_ref, tm, row_copy(1 - slot), wait=False)

    _segment_copies(seg_ref, tm, row_copy(slot), wait=True)

    lp = lp_ref[...]
    s_ix = lax.broadcasted_iota(I32, (1, rows), 1).astype(F32)
    wmat = jnp.zeros((tm, rows), F32)
    for k in range(TOP_K):
        wmat = jnp.where(s_ix == lp[:, k:k + 1], w4_ref[:, k:k + 1], wmat)
    w_hi = wmat.astype(BF16)
    w_lo = (wmat - w_hi.astype(F32)).astype(BF16)
    cols = []
    for c in range(ROW_TILES):
        y = _load_row_tile(buf_ref.at[slot], rows, c)
        y_hi = y.astype(BF16)
        y_lo = (y - y_hi.astype(F32)).astype(BF16)
        cols.append(_dot(w_hi, y_hi) + _dot(w_hi, y_lo) + _dot(w_lo, y_hi))
    acc = DEEPNORM_ALPHA * h1_ref[...] + jnp.concatenate(cols, axis=1)
    o_ref[...] = _layer_norm(acc, g_ref[...], b_ref[...])


def _combine(seg4, lp, w4, h1, ln_g, ln_b, ys):
    T, D = h1.shape
    tm = TOKEN_TILE
    n_tile = T // tm
    return pl.pallas_call(
        _combine_kernel, grid=(n_tile,),
        in_specs=[_seg_spec(lambda i: (0, i, 0, 0)),
                  _seg_spec(lambda i: (0, jnp.minimum(i + 1, n_tile - 1), 0, 0)),
                  pl.BlockSpec((tm, LANES), lambda i: (i, 0)),
                  pl.BlockSpec((tm, LANES), lambda i: (i, 0)),
                  pl.BlockSpec((tm, D), lambda i: (i, 0)),
                  _full_spec((1, D)), _full_spec((1, D)),
                  pl.BlockSpec(memory_space=pl.ANY)],
        out_specs=pl.BlockSpec((tm, D), lambda i: (i, 0)),
        out_shape=jax.ShapeDtypeStruct((T, D), F32),
        scratch_shapes=[pltpu.VMEM((2, TOP_K * tm * ROW_TILES, LANES), F32), pltpu.SemaphoreType.DMA((2,))],
        compiler_params=_cparams("arbitrary"), name="combine",
    )(seg4, seg4, lp, w4, h1, ln_g, ln_b, ys)


def _rope_tables(positions):
    inv = ROPE_THETA ** (-jnp.arange(0, ROT_DIM, 2, dtype=F32) / ROT_DIM)
    ang = positions.astype(F32)[..., None] * inv
    cos_sin = jnp.concatenate([jnp.cos(ang), jnp.sin(ang)], axis=-1)
    half = ROT_DIM // 2
    spread = np.zeros((ROT_DIM, 3 * LANES), np.float32)
    unit = np.ones((1, LANES), np.float32)
    for lane in range(LANES):
        d = lane % HEAD_DIM
        if d < half:
            spread[d, lane] = 1.0
            spread[half + d, 2 * LANES + lane] = -1.0
            unit[0, lane] = 0.0
        elif d < ROT_DIM:
            spread[d - half, lane] = 1.0
            spread[d, LANES + lane] = 1.0
            unit[0, lane] = 0.0
    return cos_sin, jnp.asarray(spread, BF16), jnp.asarray(unit)


def _split_w_in(w_in):
    widths = (NSA_WIDTH,) + (KV_WIDTH,) * 6 + (NSA_HEADS * N_BRANCH, S5_WIDTH, MEM_WIDTH, N_BRANCH * D_MODEL)
    offs = [0]
    for w in widths:
        offs.append(offs[-1] + w)
    col = lambda i: w_in[:, offs[i]:offs[i + 1]]
    wq, kc, vc, ks, vs, kw, vw, wg, wu, wqm, wm = (col(i) for i in range(11))
    wk = jnp.concatenate([kc, ks, kw], axis=1)
    wv = jnp.concatenate([vc, vs, vw], axis=1)
    per_group = NSA_HPG * N_BRANCH
    wg_pad = jnp.zeros((w_in.shape[0], NSA_GROUPS * LANES), w_in.dtype)
    for g in range(NSA_GROUPS):
        wg_pad = wg_pad.at[:, g * LANES:g * LANES + per_group].set(wg[:, g * per_group:(g + 1) * per_group])
    return tuple(w.astype(BF16) for w in (wq, wk, wv, wg_pad, wu, wqm, wm))


def _compress_weights(w1):
    half = CMP_BLOCK // 2
    eye = jnp.eye(NSA_GROUPS, dtype=w1.dtype)

    def arrange(w_half):
        full = jnp.einsum('sdf,gh->sgdhf', w_half, eye)
        return full.reshape(half * NSA_GROUPS * HEAD_DIM, NSA_GROUPS * CMP_HIDDEN).astype(BF16)

    return (w1.reshape(CMP_BLOCK * HEAD_DIM, CMP_HIDDEN).astype(BF16), arrange(w1[:half]), arrange(w1[half:]))


def _s5_weights(a_re, a_im, log_dt, b_re, b_im, c_re, c_im):
    step = jnp.exp(log_dt)[:, None]
    mag = jnp.exp(a_re * step)
    ab_re, ab_im = mag * jnp.cos(a_im * step), mag * jnp.sin(a_im * step)
    den = a_re * a_re + a_im * a_im
    nr = ab_re - 1.0
    coef_re = (nr * a_re + ab_im * a_im) / den
    coef_im = (ab_im * a_re - nr * a_im) / den
    bb_re = coef_re[..., None] * b_re - coef_im[..., None] * b_im
    bb_im = coef_re[..., None] * b_im + coef_im[..., None] * b_re
    eye = jnp.eye(S5_GROUPS, dtype=F32)
    n_state = S5_GROUPS * S5_STATE
    n_tile = n_state // LANES
    in_per = n_tile // (S5_WIDTH // LANES)

    def in_map(bb):
        return jnp.einsum('gnp,gh->gphn', bb, eye).reshape(S5_WIDTH, n_state)

    def out_map(c):
        return jnp.einsum('gpn,gh->gnhp', c, eye).reshape(n_state, S5_WIDTH)

    bf_re, bf_im = in_map(bb_re), in_map(bb_im)
    cf_re, cf_im = out_map(c_re), out_map(-c_im)
    wb, wc = [], []
    for c in range(n_tile):
        i = c // in_per
        rs, cs = slice(i * LANES, (i + 1) * LANES), slice(c * LANES, (c + 1) * LANES)
        wb.append(jnp.concatenate([bf_re[rs, cs], bf_im[rs, cs]], axis=1))
        wc.append(jnp.concatenate([cf_re[cs, rs], cf_im[cs, rs]], axis=0))
    wb = jnp.stack(wb).astype(BF16)
    wc = jnp.stack(wc).astype(BF16)
    return wb, wc, ab_re.reshape(n_tile, LANES), ab_im.reshape(n_tile, LANES)


def _layer(x, mem, positions, ln_emb_g, ln_emb_b, w_in, pe_k, pe_v, w_kcmp1, w_kcmp2, w_vcmp1, w_vcmp2,
           s5_a_re, s5_a_im, s5_log_dt, s5_b_re, s5_b_im, s5_c_re, s5_c_im, s5_d,
           w_s5_glu, w_mem_kv, w_nsa_out, w_mem_out, w_o, ln1_g, ln1_b, w_router, b_router,
           w_gate_up, b_gate_up, w_down, b_down, ln2_g, ln2_b):
    B, L, D = x.shape
    T = B * L
    row = lambda v: v.reshape(1, -1)

    cos_sin, spread, unit = _rope_tables(positions)
    (q_hm, kc, vc, ks, vs, kw, vw, gates, u, qm, gm) = _inproj(
        x, row(ln_emb_g), row(ln_emb_b), cos_sin, spread, unit, *_split_w_in(w_in))

    n_chunk = L // CMP_STRIDE
    chunked = lambda t: t.reshape(B, n_chunk, CMP_STRIDE * KV_WIDTH)
    pe_rows = lambda pe: jnp.broadcast_to(pe.reshape(1, -1), (SUBLANES, CMP_BLOCK * HEAD_DIM)).astype(BF16)
    wk1f, wk1a, wk1b = _compress_weights(w_kcmp1)
    wv1f, wv1a, wv1b = _compress_weights(w_vcmp1)
    ck, cv = _compress(chunked(kc), chunked(vc), pe_rows(pe_k), pe_rows(pe_v), wk1f, wv1f,
                       wk1a, wk1b, wv1a, wv1b, w_kcmp2.astype(BF16), w_vcmp2.astype(BF16))

    per_sb = SEL_BLOCK // CMP_STRIDE
    c_ix = np.arange(n_chunk)[:, None]
    n_ix = np.arange(L // SEL_BLOCK)[None, :]
    w_score = jnp.asarray((c_ix // per_sb == n_ix).astype(np.float32)
                          + ((c_ix + 1) // per_sb == n_ix).astype(np.float32), BF16)
    o_nsa = _nsa(q_hm, ck, cv, ks, vs, kw, vw, gates, w_score)

    wb, wc, a_re, a_im = _s5_weights(s5_a_re, s5_a_im, s5_log_dt, s5_b_re, s5_b_im, s5_c_re, s5_c_im)
    gy = _s5(u, wb, wc, a_re, a_im, row(s5_d))

    k_mem, v_mem = _memkv(mem, w_mem_kv.astype(BF16))
    o_mem = _memattn(qm, k_mem, v_mem)

    pad_e = LANES - N_EXPERTS
    wr = jnp.pad(w_router, ((0, 0), (0, pad_e)))
    wr_hi = wr.astype(BF16)
    wr_lo = (wr - wr_hi.astype(F32)).astype(BF16)
    br = jnp.concatenate([b_router, jnp.full((pad_e,), -jnp.inf, F32)]).reshape(1, LANES)
    tm = TOKEN_TILE
    n_tile = T // tm
    strict_lower = lambda n: jnp.asarray(np.tril(np.ones((n, n), np.float32), -1), BF16)
    triu = jnp.asarray(np.triu(np.ones((LANES, LANES), np.float32)), BF16)
    striu = jnp.asarray(np.triu(np.ones((LANES, LANES), np.float32), 1), BF16)
    flat = lambda t: t.reshape(T, t.shape[-1])
    h1, lp, w4, cnt = _merge(
        flat(x), row(ln_emb_g), row(ln_emb_b), flat(o_nsa), flat(gy), flat(o_mem), flat(gm),
        w_nsa_out.astype(BF16), w_s5_glu.astype(BF16), w_mem_out.astype(BF16), w_o.astype(BF16),
        row(ln1_g), row(ln1_b), wr_hi, wr_lo, br, strict_lower(tm), striu)

    cap = (T * TOP_K + MOE_ROWS - 1) // MOE_ROWS * MOE_ROWS + N_EXPERTS * MOE_ROWS
    n_blk = cap // MOE_ROWS
    seg, blk_owner, misc = _slots(cnt[:, 0, :], triu, striu, strict_lower(n_tile), n_blk)
    seg4 = seg.reshape(3, n_tile, 1, LANES)
    blk_expert = blk_owner[:, 0]
    n_used = misc[0, :1]

    xs = _dispatch(seg4, misc, lp, h1, cap)
    ys = _experts(blk_expert, n_used, xs, w_gate_up, b_gate_up, w_down, b_down)
    out = _combine(seg4, lp, w4, h1, row(ln2_g), row(ln2_b), ys)
    return out.reshape(B, L, D)


def kernel(x, mem, positions, ln_emb_g, ln_emb_b, w_in, pe_k_cmp, pe_v_cmp, w_kcmp1, w_kcmp2, w_vcmp1, w_vcmp2, s5_a_re, s5_a_im, s5_log_dt, s5_b_re, s5_b_im, s5_c_re, s5_c_im, s5_d, w_s5_glu, w_mem_kv, w_nsa_out, w_mem_out, w_o, ln1_g, ln1_b, w_router, b_router, w_gate_up, b_gate_up, w_down, b_down, ln2_g, ln2_b):
    assert w_in.shape[0] == DEPTH
    l = 0
    return _layer(x, mem, positions, ln_emb_g, ln_emb_b, w_in[l], pe_k_cmp[l], pe_v_cmp[l], w_kcmp1[l],
                  w_kcmp2[l], w_vcmp1[l], w_vcmp2[l], s5_a_re[l], s5_a_im[l], s5_log_dt[l], s5_b_re[l],
                  s5_b_im[l], s5_c_re[l], s5_c_im[l], s5_d[l], w_s5_glu[l], w_mem_kv[l], w_nsa_out[l],
                  w_mem_out[l], w_o[l], ln1_g[l], ln1_b[l], w_router[l], b_router[l], w_gate_up[l],
                  b_gate_up[l], w_down[l], b_down[l], ln2_g[l], ln2_b[l])
```

```python
import functools
import math

import jax
import jax.numpy as jnp
import numpy as np
from jax import lax
from jax.experimental import pallas as pl
from jax.experimental.pallas import tpu as pltpu

F32 = jnp.float32
BF16 = jnp.bfloat16
I32 = jnp.int32

D_MODEL = 1024
NSA_HEADS = 8
NSA_GROUPS = 2
NSA_HPG = NSA_HEADS // NSA_GROUPS
HEAD_DIM = 64
NSA_WIDTH = NSA_HEADS * HEAD_DIM
KV_WIDTH = NSA_GROUPS * HEAD_DIM
CMP_BLOCK = 32
CMP_STRIDE = 16
CMP_HIDDEN = 128
SEL_BLOCK = 64
N_SEL = 16
WINDOW = 512
Q_BLOCK = 256
ROPE_THETA = 500000.0
ROT_DIM = HEAD_DIM // 4
S5_WIDTH = 512
S5_GROUP_DIM = 16
S5_GROUPS = S5_WIDTH // S5_GROUP_DIM
S5_STATE = 64
MEM_HEADS = 4
MEM_HEAD_DIM = 128
MEM_WIDTH = MEM_HEADS * MEM_HEAD_DIM
N_BRANCH = 3
N_EXPERTS = 32
TOP_K = 4
D_FF = 1024
SWIGLU_LIMIT = 7.0
SWIGLU_ALPHA = 1.702
LN_EPS = 1e-5
DEPTH = 1
DEEPNORM_ALPHA = (2 * DEPTH) ** 0.25

LANES = 128
SUBLANES = 8
VMEM_LIMIT_BYTES = 56 * 1024 * 1024

TOKEN_TILE = 256
MERGE_SORT_TILES = 2
MEM_TOKEN_TILE = 512
SEL_KV_TILE = 512
S5_CHUNK = 512
S5_PITCH = S5_CHUNK + 8
MOE_ROWS = 512
NEG_BIG = -(2.0 ** 100)
Q_SCALE_LOG2 = HEAD_DIM ** -0.5 * math.log2(math.e)


def _cparams(*sem):
    return pltpu.CompilerParams(dimension_semantics=sem, vmem_limit_bytes=VMEM_LIMIT_BYTES)


def _dot(a, b):
    return jnp.dot(a, b, preferred_element_type=F32)


def _dot_nt(a, b):
    return lax.dot_general(a, b, (((1,), (1,)), ((), ())), preferred_element_type=F32)


def _layer_norm(x, g, b):
    mu = jnp.mean(x, axis=-1, keepdims=True)
    xc = x - mu
    var = jnp.mean(xc * xc, axis=-1, keepdims=True)
    return xc * lax.rsqrt(var + LN_EPS) * g + b


def _gelu_tanh(x):
    cdf = 0.5 * (1.0 + jnp.tanh(math.sqrt(2.0 / math.pi) * (x + 0.044715 * (x * x * x))))
    return x * cdf


def _masked_exp2(s, mask):
    s = jnp.where(mask, s, -jnp.inf)
    m = jnp.max(s, axis=-1, keepdims=True)
    m = jnp.where(m > -jnp.inf, m, 0.0)
    return jnp.exp2(s - m)


def _safe_recip(denom):
    return 1.0 / jnp.maximum(denom, jnp.finfo(F32).tiny)


def _split3(x):
    hi = x.astype(BF16)
    r1 = x - hi.astype(F32)
    mid = r1.astype(BF16)
    lo = (r1 - mid.astype(F32)).astype(BF16)
    return hi, mid, lo


def _full_spec(shape):
    nd = len(shape)
    return pl.BlockSpec(shape, lambda *_: (0,) * nd)


def _inproj_kernel(x_ref, g_ref, b_ref, cs_ref, spread_ref, unit_ref,
                   wq_ref, wk_ref, wv_ref, wg_ref, wu_ref, wqm_ref, wm_ref,
                   q_ref, kc_ref, vc_ref, ks_ref, vs_ref, kw_ref, vw_ref,
                   gate_ref, u_ref, qm_ref, gm_ref):
    h = _layer_norm(x_ref[0], g_ref[...], b_ref[...])
    hb = h.astype(BF16)
    tab = sum(_dot(part, spread_ref[...]) for part in _split3(cs_ref[0]))
    cos_t = tab[:, 0:LANES] + unit_ref[...]
    sin_a = tab[:, LANES:2 * LANES]
    sin_b = tab[:, 2 * LANES:3 * LANES]

    def rope(t):
        return (t * cos_t + pltpu.roll(t, ROT_DIM // 2, 1) * sin_a
                + pltpu.roll(t, LANES - ROT_DIM // 2, 1) * sin_b)

    q = _dot(hb, wq_ref[...])
    for c in range(NSA_WIDTH // LANES):
        qc = rope(q[:, c * LANES:(c + 1) * LANES]) * Q_SCALE_LOG2
        for hh in range(2):
            q_ref[0, 2 * c + hh] = qc[:, hh * HEAD_DIM:(hh + 1) * HEAD_DIM].astype(BF16)
    k3 = _dot(hb, wk_ref[...])
    kc = rope(k3[:, 0:LANES])
    ks = rope(k3[:, LANES:2 * LANES])
    kw = rope(k3[:, 2 * LANES:3 * LANES])
    v3 = _dot(hb, wv_ref[...])
    kc_ref[0] = kc.astype(BF16)
    vc_ref[0] = v3[:, 0:LANES].astype(BF16)
    tm = x_ref.shape[1]
    pos = pl.program_id(1) * tm + lax.broadcasted_iota(I32, (tm, LANES), 0)
    blk_hot = jnp.where(lax.broadcasted_iota(I32, (tm, LANES), 1) == pos // SEL_BLOCK, 1.0, 0.0)
    lane_pad = jnp.zeros((tm, LANES - HEAD_DIM), F32)
    ones_pad = jnp.where(lax.broadcasted_iota(I32, (tm, LANES - HEAD_DIM), 1) == 0, 1.0, 0.0)
    for g in range(NSA_GROUPS):
        sl = slice(g * HEAD_DIM, (g + 1) * HEAD_DIM)
        ks_ref[0, g] = jnp.concatenate([blk_hot, ks[:, sl], lane_pad], axis=1).astype(BF16)
        kw_ref[0, g] = kw[:, sl].astype(BF16)
        vs_ref[0, g] = jnp.concatenate([v3[:, LANES:2 * LANES][:, sl], ones_pad], axis=1).astype(BF16)
        vw_ref[0, g] = jnp.concatenate([v3[:, 2 * LANES:3 * LANES][:, sl], ones_pad], axis=1).astype(BF16)
    gate_ref[0] = jax.nn.sigmoid(_dot(hb, wg_ref[...]))
    u_ref[0] = _dot(hb, wu_ref[...])
    qm_ref[0] = _dot(hb, wqm_ref[...]).astype(BF16)
    gm_ref[0] = jax.nn.sigmoid(_dot(hb, wm_ref[...]))


def _inproj(x, ln_g, ln_b, cos_sin, spread, unit, wq, wk, wv, wg, wu, wqm, wm):
    B, L, D = x.shape
    tm = TOKEN_TILE
    grid = (B, L // tm)
    tok = lambda w: pl.BlockSpec((1, tm, w), lambda b, i: (b, i, 0))
    head = lambda n, w=HEAD_DIM: pl.BlockSpec((1, n, tm, w), lambda b, i: (b, 0, i, 0))
    in_specs = [tok(D), _full_spec((1, D)), _full_spec((1, D)), tok(ROT_DIM), _full_spec(spread.shape),
                _full_spec(unit.shape)]
    in_specs += [_full_spec(w.shape) for w in (wq, wk, wv, wg, wu, wqm, wm)]
    sd = jax.ShapeDtypeStruct
    out_shape = [
        sd((B, NSA_HEADS, L, HEAD_DIM), BF16),
        sd((B, L, KV_WIDTH), BF16), sd((B, L, KV_WIDTH), BF16),
        sd((B, NSA_GROUPS, L, 2 * LANES), BF16), sd((B, NSA_GROUPS, L, LANES), BF16),
        sd((B, NSA_GROUPS, L, HEAD_DIM), BF16), sd((B, NSA_GROUPS, L, LANES), BF16),
        sd((B, L, NSA_GROUPS * LANES), F32),
        sd((B, L, S5_WIDTH), F32),
        sd((B, L, MEM_WIDTH), BF16),
        sd((B, L, N_BRANCH * D), F32),
    ]
    out_specs = [head(NSA_HEADS), tok(KV_WIDTH), tok(KV_WIDTH), head(NSA_GROUPS, 2 * LANES),
                 head(NSA_GROUPS, LANES), head(NSA_GROUPS), head(NSA_GROUPS, LANES),
                 tok(NSA_GROUPS * LANES), tok(S5_WIDTH),
                 tok(MEM_WIDTH), tok(N_BRANCH * D)]
    return pl.pallas_call(
        _inproj_kernel, grid=grid, in_specs=in_specs, out_specs=out_specs, out_shape=out_shape,
        compiler_params=_cparams("parallel", "parallel"), name="inproj",
    )(x, ln_g, ln_b, cos_sin, spread, unit, wq, wk, wv, wg, wu, wqm, wm)


def _compress_kernel(kc_ref, vc_ref, pek_ref, pev_ref, wk1f_ref, wv1f_ref,
                     wk1a_ref, wk1b_ref, wv1a_ref, wv1b_ref, wk2_ref, wv2_ref, ck_ref, cv_ref):
    n_chunk = kc_ref.shape[1]
    row = lax.broadcasted_iota(I32, (n_chunk, 1), 0)

    def one(x_ref, pe_ref, w1f_ref, w1a_ref, w1b_ref, w2_ref, o_ref):
        x = x_ref[0]
        first = _dot(x, w1a_ref[...])
        second = _dot(x, w1b_ref[...])
        second = pltpu.roll(second, n_chunk - 1, 0)
        pe_term = _dot(pe_ref[...], w1f_ref[...])[0:1]
        pe_term = jnp.concatenate([pe_term] * NSA_GROUPS, axis=1)
        hid = _gelu_tanh(first + second + pe_term).astype(BF16)
        for g in range(NSA_GROUPS):
            o = _dot(hid[:, g * CMP_HIDDEN:(g + 1) * CMP_HIDDEN], w2_ref[...])
            o_ref[0, g] = jnp.where(row < n_chunk - 1, o, 0.0).astype(BF16)

    one(kc_ref, pek_ref, wk1f_ref, wk1a_ref, wk1b_ref, wk2_ref, ck_ref)
    one(vc_ref, pev_ref, wv1f_ref, wv1a_ref, wv1b_ref, wv2_ref, cv_ref)


def _compress(kc_r, vc_r, pek, pev, wk1f, wv1f, wk1a, wk1b, wv1a, wv1b, wk2, wv2):
    B, n_chunk, width = kc_r.shape
    blk = pl.BlockSpec((1, n_chunk, width), lambda b: (b, 0, 0))
    out = pl.BlockSpec((1, NSA_GROUPS, n_chunk, HEAD_DIM), lambda b: (b, 0, 0, 0))
    ws = [pek, pev, wk1f, wv1f, wk1a, wk1b, wv1a, wv1b, wk2, wv2]
    sd = jax.ShapeDtypeStruct((B, NSA_GROUPS, n_chunk, HEAD_DIM), BF16)
    return pl.pallas_call(
        _compress_kernel, grid=(B,), in_specs=[blk, blk] + [_full_spec(w.shape) for w in ws],
        out_specs=[out, out], out_shape=[sd, sd], compiler_params=_cparams("parallel"), name="compress",
    )(kc_r, vc_r, *ws)


def _nsa_kernel(q_ref, ck_ref, cv_ref, ks_ref, vs_ref, kw_ref, vw_ref, gate_ref, wsc_ref, o_ref):
    seq_len = ks_ref.shape[2]
    n_cmp = ck_ref.shape[2]
    n_sb = seq_len // SEL_BLOCK
    n_sel = min(N_SEL, n_sb)
    rows = NSA_HPG * Q_BLOCK
    groups = range(NSA_GROUPS)
    q0 = pl.program_id(1) * Q_BLOCK
    t1 = q0 + lax.broadcasted_iota(I32, (Q_BLOCK, 1), 0)
    t4 = jnp.concatenate([t1] * NSA_HPG, axis=0)
    tk = SEL_KV_TILE

    def front(g):
        q = q_ref[0, g * NSA_HPG:(g + 1) * NSA_HPG].reshape(rows, HEAD_DIM)

        s = _dot_nt(q, ck_ref[0, g])
        c_end = lax.broadcasted_iota(I32, (1, n_cmp), 1) * CMP_STRIDE + (CMP_BLOCK - 1)
        e = _masked_exp2(s, c_end <= t4)
        p_cmp = e * _safe_recip(jnp.sum(e, axis=-1, keepdims=True))
        o_cmp = _dot(p_cmp.astype(BF16), cv_ref[0, g])

        imp = p_cmp[0:Q_BLOCK]
        for hh in range(1, NSA_HPG):
            imp = imp + p_cmp[hh * Q_BLOCK:(hh + 1) * Q_BLOCK]
        w_sc = wsc_ref[...]
        score = sum(_dot(part, w_sc) for part in _split3(imp))
        score_t = score.T
        jb = lax.broadcasted_iota(I32, (n_sb, Q_BLOCK), 0)
        tb = (q0 + lax.broadcasted_iota(I32, (1, Q_BLOCK), 1)) // SEL_BLOCK
        forced = (jb == 0) | (jb == tb) | (jb == tb - 1)
        work = jnp.where(forced | (jb > tb), -jnp.inf, score_t)
        bias_t = jnp.where(forced, 0.0, NEG_BIG)
        jbf = jb.astype(F32)
        for _ in range(n_sel - 3):
            m = jnp.max(work, axis=0, keepdims=True)
            idx = jnp.min(jnp.where(work == m, jbf, float(n_sb)), axis=0, keepdims=True)
            pick = jbf == idx
            bias_t = jnp.where(pick, 0.0, bias_t)
            work = jnp.where(pick, -jnp.inf, work)
        sel_bias = bias_t.T
        if n_sb < LANES:
            sel_bias = jnp.concatenate([sel_bias, jnp.zeros((Q_BLOCK, LANES - n_sb), F32)], axis=1)

        span = WINDOW + Q_BLOCK
        w0 = pl.multiple_of(jnp.maximum(q0 - WINDOW, 0), Q_BLOCK)
        s = _dot_nt(q, kw_ref[0, g, pl.ds(w0, span), :])
        diff = t4 - (w0 + lax.broadcasted_iota(I32, (1, span), 1))
        e = _masked_exp2(s, (diff >= 0) & (diff < WINDOW))
        o_win = _dot(e.astype(BF16), vw_ref[0, g, pl.ds(w0, span), :])
        o_win = o_win[:, :HEAD_DIM] * _safe_recip(o_win[:, HEAD_DIM:HEAD_DIM + 1])

        q_aug = jnp.concatenate([jnp.concatenate([sel_bias.astype(BF16)] * NSA_HPG, axis=0), q,
                                 jnp.zeros((rows, LANES - HEAD_DIM), BF16)], axis=1)
        return q_aug, o_cmp, o_win

    fronts = [front(g) for g in groups]

    def sel_tile(g, j, carry, causal):
        m_run, acc = carry
        k0 = pl.multiple_of(j * tk, tk)
        sc = _dot_nt(fronts[g][0], ks_ref[0, g, pl.ds(k0, tk), :])
        if causal:
            kpos = k0 + lax.broadcasted_iota(I32, (1, tk), 1)
            sc = jnp.where(kpos <= t4, sc, NEG_BIG)
        m_new = jnp.maximum(m_run, jnp.max(sc, axis=-1, keepdims=True))
        p = jnp.exp2(sc - m_new)
        acc_new = jnp.exp2(m_run - m_new) * acc + _dot(p.astype(BF16), vs_ref[0, g, pl.ds(k0, tk), :])
        return m_new, acc_new

    def sel_pair(jj, carries, causal):
        return tuple(sel_tile(g, 2 * jj + 1, sel_tile(g, 2 * jj, carries[g], causal), causal) for g in groups)

    init = tuple((jnp.full((rows, 1), NEG_BIG, F32), jnp.zeros((rows, LANES), F32)) for _ in groups)
    last_pair = (q0 // tk) // 2
    carries = lax.fori_loop(0, last_pair, functools.partial(sel_pair, causal=False), init)
    carries = sel_pair(last_pair, carries, True)

    outs = []
    for g in groups:
        _, o_cmp, o_win = fronts[g]
        acc = carries[g][1]
        o_sel = acc[:, :HEAD_DIM] * (1.0 / acc[:, HEAD_DIM:HEAD_DIM + 1])
        gt = gate_ref[0, :, g * LANES:(g + 1) * LANES]
        for hh in range(NSA_HPG):
            sl = slice(hh * Q_BLOCK, (hh + 1) * Q_BLOCK)
            c = hh * N_BRANCH
            outs.append(o_cmp[sl] * gt[:, c:c + 1] + o_sel[sl] * gt[:, c + 1:c + 2]
                        + o_win[sl] * gt[:, c + 2:c + 3])
    o_ref[0] = jnp.concatenate(outs, axis=1).astype(BF16)


def _nsa(q_hm, ck, cv, ks, vs, kw, vw, gates, w_score):
    B, _, L, _ = q_hm.shape
    assert L // SEL_BLOCK <= LANES and (L // SEL_KV_TILE) % 2 == 0 and L >= WINDOW + Q_BLOCK
    n_cmp = ck.shape[2]
    grid = (B, L // Q_BLOCK)
    qspec = pl.BlockSpec((1, NSA_HEADS, Q_BLOCK, HEAD_DIM), lambda b, i: (b, 0, i, 0))
    cspec = pl.BlockSpec((1, NSA_GROUPS, n_cmp, HEAD_DIM), lambda b, i: (b, 0, 0, 0))
    kvspec = lambda w: pl.BlockSpec((1, NSA_GROUPS, L, w), lambda b, i: (b, 0, 0, 0),
                                    pipeline_mode=pl.Buffered(1))
    gspec = pl.BlockSpec((1, Q_BLOCK, NSA_GROUPS * LANES), lambda b, i: (b, i, 0))
    ospec = pl.BlockSpec((1, Q_BLOCK, NSA_WIDTH), lambda b, i: (b, i, 0))
    return pl.pallas_call(
        _nsa_kernel, grid=grid,
        in_specs=[qspec, cspec, cspec, kvspec(2 * LANES), kvspec(LANES), kvspec(HEAD_DIM), kvspec(LANES), gspec,
                  _full_spec(w_score.shape)],
        out_specs=ospec, out_shape=jax.ShapeDtypeStruct((B, L, NSA_WIDTH), BF16),
        compiler_params=_cparams("parallel", "arbitrary"), name="nsa",
    )(q_hm, ck, cv, ks, vs, kw, vw, gates, w_score)


def _s5_kernel(u_ref, wb_ref, wc_ref, are_ref, aim_ref, d_ref, y_ref, sre_ref, sim_ref, carry_ref):
    n_b, chunk, _ = u_ref.shape
    n_tile = wb_ref.shape[0]
    in_per = n_tile // (S5_WIDTH // LANES)
    pitch = S5_PITCH

    @pl.when(pl.program_id(0) == 0)
    def _():
        carry_ref[...] = jnp.zeros_like(carry_ref)

    for b in range(n_b):
        for c in range(n_tile):
            i = c // in_per
            ub = u_ref[b, :, i * LANES:(i + 1) * LANES].astype(BF16)
            r = _dot(ub, wb_ref[c])
            sre_ref[b, c * pitch:c * pitch + chunk, :] = r[:, :LANES]
            sim_ref[b, c * pitch:c * pitch + chunk, :] = r[:, LANES:]

    a_re, a_im = are_ref[...], aim_ref[...]

    def step(t, carry):
        out = []
        for b in range(n_b):
            s_re, s_im = carry[2 * b], carry[2 * b + 1]
            rows = pl.ds(t, n_tile, stride=pitch)
            n_re = a_re * s_re - a_im * s_im + sre_ref[b, rows, :]
            n_im = a_re * s_im + a_im * s_re + sim_ref[b, rows, :]
            sre_ref[b, rows, :] = n_re
            sim_ref[b, rows, :] = n_im
            out += [n_re, n_im]
        return tuple(out)

    init = tuple(carry_ref[i] for i in range(2 * n_b))
    fin = lax.fori_loop(0, chunk, step, init, unroll=8)
    for i in range(2 * n_b):
        carry_ref[i] = fin[i]

    for b in range(n_b):
        for o in range(S5_WIDTH // LANES):
            acc = jnp.zeros((chunk, LANES), F32)
            for c in range(o * in_per, (o + 1) * in_per):
                rows = slice(c * pitch, c * pitch + chunk)
                state = jnp.concatenate([sre_ref[b, rows, :], sim_ref[b, rows, :]], axis=1).astype(BF16)
                acc = acc + _dot(state, wc_ref[c])
            lanes = slice(o * LANES, (o + 1) * LANES)
            y = acc + d_ref[:, lanes] * u_ref[b, :, lanes]
            y_ref[b, :, lanes] = _gelu_tanh(y).astype(BF16)


def _s5(u, wb, wc, a_re, a_im, d_skip):
    B, L, W = u.shape
    chunk = S5_CHUNK
    n_tile = wb.shape[0]
    blk = pl.BlockSpec((B, chunk, W), lambda i: (0, i, 0))
    slab = pltpu.VMEM((B, n_tile * S5_PITCH, LANES), F32)
    return pl.pallas_call(
        _s5_kernel, grid=(L // chunk,),
        in_specs=[blk] + [_full_spec(w.shape) for w in (wb, wc, a_re, a_im, d_skip)],
        out_specs=blk, out_shape=jax.ShapeDtypeStruct((B, L, W), BF16),
        scratch_shapes=[slab, slab, pltpu.VMEM((2 * B, n_tile, LANES), F32)],
        compiler_params=_cparams("arbitrary"), name="s5",
    )(u, wb, wc, a_re, a_im, d_skip)


def _memkv_kernel(mem_ref, w_ref, k_ref, v_ref):
    kv = _dot(mem_ref[0].astype(BF16), w_ref[...])
    k_ref[0] = kv[:, :MEM_WIDTH].astype(BF16)
    v_ref[0] = kv[:, MEM_WIDTH:].astype(BF16)


def _memkv(mem, w_kv):
    B, M, D = mem.shape
    out = pl.BlockSpec((1, M, MEM_WIDTH), lambda b: (b, 0, 0))
    sd = jax.ShapeDtypeStruct((B, M, MEM_WIDTH), BF16)
    return pl.pallas_call(
        _memkv_kernel, grid=(B,),
        in_specs=[pl.BlockSpec((1, M, D), lambda b: (b, 0, 0)), _full_spec(w_kv.shape)],
        out_specs=[out, out], out_shape=[sd, sd], compiler_params=_cparams("parallel"), name="memkv",
    )(mem, w_kv)


def _memattn_kernel(q_ref, k_ref, v_ref, o_ref):
    outs = []
    for h in range(MEM_HEADS):
        sl = slice(h * MEM_HEAD_DIM, (h + 1) * MEM_HEAD_DIM)
        s = _dot_nt(q_ref[0, :, sl], k_ref[0, :, sl]) * (MEM_HEAD_DIM ** -0.5)
        m = jnp.max(s, axis=-1, keepdims=True)
        e = jnp.exp(s - m)
        p = e / jnp.sum(e, axis=-1, keepdims=True)
        outs.append(_dot(p.astype(BF16), v_ref[0, :, sl]))
    o_ref[0] = jnp.concatenate(outs, axis=1).astype(BF16)


def _memattn(qm, k, v):
    B, L, W = qm.shape
    M = k.shape[1]
    tm = MEM_TOKEN_TILE
    tok = pl.BlockSpec((1, tm, W), lambda b, i: (b, i, 0))
    kv = pl.BlockSpec((1, M, W), lambda b, i: (b, 0, 0))
    return pl.pallas_call(
        _memattn_kernel, grid=(B, L // tm), in_specs=[tok, kv, kv], out_specs=tok,
        out_shape=jax.ShapeDtypeStruct((B, L, W), BF16),
        compiler_params=_cparams("parallel", "parallel"), name="memattn",
    )(qm, k, v)


def _merge_kernel(x_ref, lng_ref, lnb_ref, on_ref, gy_ref, om_ref, gm_ref,
                  wn_ref, wglu_ref, wmo_ref, wo_ref, l1g_ref, l1b_ref,
                  wrh_ref, wrl_ref, br_ref, tri_ref, striu_ref,
                  h1_ref, lp_ref, w4_ref, cnt_ref):
    D = x_ref.shape[1]
    tm = x_ref.shape[0]
    h =_layer_norm(x_ref[...], lng_ref[...], lnb_ref[...])
    y_nsa = _dot(on_ref[...], wn_ref[...])
    glu = _dot(gy_ref[...], wglu_ref[...])
    y_s5 = glu[:, :D] * jax.nn.sigmoid(glu[:, D:])
    y_mem = _dot(om_ref[...], wmo_ref[...])
    merged = gm_ref[:, 0:D] * y_nsa + gm_ref[:, D:2 * D] * y_s5 + gm_ref[:, 2 * D:3 * D] * y_mem
    mix = _dot(merged.astype(BF16), wo_ref[...])
    h1 = _layer_norm(DEEPNORM_ALPHA * h + mix, l1g_ref[...], l1b_ref[...])
    h1_ref[...] = h1

    hh = h1.astype(BF16)
    hl = (h1 - hh.astype(F32)).astype(BF16)
    logits = _dot(hh, wrh_ref[...]) + _dot(hh, wrl_ref[...]) + _dot(hl, wrh_ref[...]) + br_ref[...]
    lane = lax.broadcasted_iota(I32, (tm, LANES), 1)
    lane_f = lane.astype(F32)
    work = logits
    multi = jnp.zeros((tm, LANES), F32)
    vals, picks = [], []
    for _ in range(TOP_K):
        m = jnp.max(work, axis=-1, keepdims=True)
        idx = jnp.min(jnp.where(work == m, lane_f, float(LANES)), axis=-1, keepdims=True)
        pick = lane_f == idx
        vals.append(m)
        picks.append((pick, idx))
        multi = jnp.where(pick, 1.0, multi)
        work = jnp.where(pick, -jnp.inf, work)
    es = [jnp.exp(v - vals[0]) for v in vals]
    den = es[0] + es[1] + es[2] + es[3]
    st = TOKEN_TILE
    pos = []
    for t in range(tm // st):
        multi_t = multi[t * st:(t + 1) * st]
        cnt = jnp.broadcast_to(jnp.sum(multi_t, axis=0, keepdims=True), (SUBLANES, LANES))
        cnt_ref[t] = cnt
        lower = _dot(cnt.astype(BF16), striu_ref[...])[0:1]
        pos.append(lower + _dot(tri_ref[...], multi_t.astype(BF16)))
    pos = jnp.concatenate(pos, axis=0)
    lp = jnp.full((tm, LANES), -1.0, F32)
    w4 = jnp.zeros((tm, LANES), F32)
    for k in range(TOP_K):
        pick, _ = picks[k]
        lp = jnp.where(lane == k, jnp.sum(jnp.where(pick, pos, 0.0), axis=-1, keepdims=True), lp)
        w4 = jnp.where(lane == k, es[k] / den, w4)
    lp_ref[...] = lp
    w4_ref[...] = w4


def _merge(x2, lng, lnb, o_nsa, gy, om, gm, wn, wglu, wmo, wo, l1g, l1b, wrh, wrl, br, tri, striu):
    T, D = x2.shape
    tm = MERGE_SORT_TILES * TOKEN_TILE
    tok = lambda w: pl.BlockSpec((tm, w), lambda i: (i, 0))
    ws = [wn, wglu, wmo, wo, l1g, l1b, wrh, wrl, br, tri, striu]
    sd = jax.ShapeDtypeStruct
    lane_out = sd((T, LANES), F32)
    return pl.pallas_call(
        _merge_kernel, grid=(T // tm,),
        in_specs=[tok(D), _full_spec((1, D)), _full_spec((1, D)), tok(NSA_WIDTH), tok(S5_WIDTH),
                  tok(MEM_WIDTH), tok(N_BRANCH * D)] + [_full_spec(w.shape) for w in ws],
        out_specs=[tok(D), tok(LANES), tok(LANES),
                   pl.BlockSpec((MERGE_SORT_TILES, SUBLANES, LANES), lambda i: (i, 0, 0))],
        out_shape=[sd((T, D), F32), lane_out, lane_out, sd((T // TOKEN_TILE, SUBLANES, LANES), F32)],
        compiler_params=_cparams("parallel"), name="merge",
    )(x2, lng, lnb, o_nsa, gy, om, gm, *ws)


def _slots_kernel(cnt_ref, triu_ref, striu_ref, tril_ref, seg_ref, blk_ref, misc_ref):
    n_blk = blk_ref.shape[0]
    cnt = cnt_ref[...]
    cnt_b = cnt.astype(BF16)
    total = jnp.sum(cnt, axis=0, keepdims=True)
    nblk_e = jnp.floor((total + (MOE_ROWS - 1)) * (1.0 / MOE_ROWS))
    nblk_8 = jnp.broadcast_to(nblk_e, (SUBLANES, LANES))
    end_b = _dot(nblk_8.astype(BF16), triu_ref[...])
    start_rows = (end_b - nblk_8)[0:1] * MOE_ROWS
    dst = start_rows + _dot(tril_ref[...], cnt_b)
    off = _dot(cnt_b, striu_ref[...])
    seg_ref[0] = cnt.astype(I32)
    seg_ref[1] = off.astype(I32)
    seg_ref[2] = dst.astype(I32)
    blk_i = lax.broadcasted_iota(I32, (n_blk, LANES), 0).astype(F32)
    lane_b = lax.broadcasted_iota(I32, (n_blk, LANES), 1)
    ended = jnp.where((end_b[0:1] <= blk_i) & (lane_b < N_EXPERTS), 1.0, 0.0)
    owner = jnp.minimum(jnp.sum(ended, axis=-1, keepdims=True), float(N_EXPERTS - 1))
    blk_ref[...] = jnp.broadcast_to(owner, (n_blk, LANES)).astype(I32)
    lane8 = lax.broadcasted_iota(I32, (SUBLANES, LANES), 1)
    row8 = lax.broadcasted_iota(I32, (SUBLANES, LANES), 0)
    used = jnp.sum(jnp.where(lane8 == N_EXPERTS - 1, end_b, 0.0), axis=-1, keepdims=True)
    misc = jnp.where(row8 == 0, used, jnp.where(row8 == 1, start_rows + total, nblk_e * MOE_ROWS - total))
    misc_ref[...] = misc.astype(I32)


def _slots(cnt, triu, striu, tril, n_blk):
    n_tile = cnt.shape[0]
    sd = jax.ShapeDtypeStruct
    return pl.pallas_call(
        _slots_kernel, grid=(1,),
        in_specs=[_full_spec(cnt.shape), _full_spec(triu.shape), _full_spec(striu.shape), _full_spec(tril.shape)],
        out_specs=[_full_spec((3, n_tile, LANES)), _full_spec((n_blk, LANES)), _full_spec((SUBLANES, LANES))],
        out_shape=[sd((3, n_tile, LANES), I32), sd((n_blk, LANES), I32), sd((SUBLANES, LANES), I32)],
        compiler_params=_cparams("arbitrary"), name="slots",
    )(cnt, triu, striu, tril)


ROW_TILES = D_MODEL // LANES


def _row_span(row, n_rows):
    start = row * ROW_TILES
    if not isinstance(start, int):
        start = pl.multiple_of(start, ROW_TILES)
    return pl.ds(start, n_rows * ROW_TILES)


def _store_rows(ref, val):
    for c in range(ROW_TILES):
        ref[pl.ds(c, val.shape[0], stride=ROW_TILES), :] = val[:, c * LANES:(c + 1) * LANES]


def _load_row_tile(ref, n_rows, c):
    return ref[pl.ds(c, n_rows, stride=ROW_TILES), :]


def _pieces(count, max_rows, fn):
    p = max_rows
    while p >= 1:
        def piece(p=p):
            fn(count & (-2 * p), p)
        pl.when((count & p) != 0)(piece)
        p //= 2


def _start_segment_copies(seg_ref, max_rows, make_copy):
    def per_expert(e, c):
        cnt, off, dst = seg_ref[0, 0, 0, e], seg_ref[1, 0, 0, e], seg_ref[2, 0, 0, e]
        _pieces(cnt, max_rows, lambda first, rows: make_copy(off + first, dst + first, rows).start())
        return c

    lax.fori_loop(0, N_EXPERTS, per_expert, 0)


def _dispatch_kernel(seg_ref, misc_ref, lp_ref, h_ref, xs_ref, sorted_ref, zero_ref, sem, pad_sem):
    i = pl.program_id(0)
    n = pl.num_programs(0)
    tm, D = h_ref.shape
    rows = TOP_K * tm
    slot = lax.rem(i, 2)

    def row_copy(slot_):
        def make(src_row, dst_row, n_rows):
            return pltpu.make_async_copy(sorted_ref.at[slot_, _row_span(src_row, n_rows)],
                                         xs_ref.at[_row_span(dst_row, n_rows)], sem.at[slot_])
        return make

    @pl.when(i == 0)
    def _():
        zero_ref[...] = jnp.zeros_like(zero_ref)
        for wait in (False, True):
            def per_expert(e, c, wait=wait):
                def one(first, n_rows):
                    cp = pltpu.make_async_copy(zero_ref.at[_row_span(0, n_rows)],
                                               xs_ref.at[_row_span(misc_ref[1, e] + first, n_rows)], pad_sem)
                    cp.wait() if wait else cp.start()
                _pieces(misc_ref[2, e], MOE_ROWS // 2, one)
                return c
            lax.fori_loop(0, N_EXPERTS, per_expert, 0)

            def per_spare_half_block(hb, c, wait=wait):
                cp = pltpu.make_async_copy(zero_ref, xs_ref.at[_row_span(hb * (MOE_ROWS // 2), MOE_ROWS // 2)],
                                           pad_sem)
                cp.wait() if wait else cp.start()
                return c
            lax.fori_loop(2 * misc_ref[0, 0], 2 * (xs_ref.shape[0] // (MOE_ROWS * ROW_TILES)),
                          per_spare_half_block, 0)

    lp_t = lp_ref[...].T
    s_ix = lax.broadcasted_iota(I32, (rows, 1), 0).astype(F32)
    hit = s_ix == lp_t[0:1, :]
    for k in range(1, TOP_K):
        hit = hit | (s_ix == lp_t[k:k + 1, :])
    perm = jnp.where(hit, 1.0, 0.0).astype(BF16)
    _store_rows(sorted_ref.at[slot], _dot(perm, h_ref[...].astype(BF16)))

    _start_segment_copies(seg_ref, tm, row_copy(slot))

    @pl.when(i > 0)
    def _():
        row_copy(1 - slot)(0, 0, rows).wait()

    @pl.when(i == n - 1)
    def _():
        row_copy(slot)(0, 0, rows).wait()


def _seg_spec(index_map):
    return pl.BlockSpec((3, 1, 1, LANES), index_map, memory_space=pltpu.SMEM)


def _dispatch(seg4, misc, lp, h1, cap):
    T, D = h1.shape
    assert D == ROW_TILES * LANES
    tm = TOKEN_TILE
    tok = lambda w: pl.BlockSpec((tm, w), lambda i: (i, 0))
    return pl.pallas_call(
        _dispatch_kernel, grid=(T // tm,),
        in_specs=[_seg_spec(lambda i: (0, i, 0, 0)), pl.BlockSpec(memory_space=pltpu.SMEM), tok(LANES), tok(D)],
        out_specs=pl.BlockSpec(memory_space=pl.ANY),
        out_shape=jax.ShapeDtypeStruct((cap * ROW_TILES, LANES), F32),
        scratch_shapes=[pltpu.VMEM((2, TOP_K * tm * ROW_TILES, LANES), F32),
                        pltpu.VMEM((MOE_ROWS // 2 * ROW_TILES, LANES), F32),
                        pltpu.SemaphoreType.DMA((2,)), pltpu.SemaphoreType.DMA(())],
        compiler_params=_cparams("arbitrary"), name="dispatch",
    )(seg4, misc, lp, h1)


def _expert_kernel(blk_ref, used_ref, xs_ref, wgu_ref, bgu_ref, wd_ref, bd_ref, ys_ref, wgu_bf, wd_bf):
    i = pl.program_id(0)
    live = i < used_ref[0]

    @pl.when(live & ((i == 0) | (blk_ref[i] != blk_ref[jnp.maximum(i - 1, 0)])))
    def _():
        wgu_bf[...] = wgu_ref[0].astype(BF16)
        wd_bf[...] = wd_ref[0].astype(BF16)

    @pl.when(live)
    def _():
        xb = jnp.concatenate([_load_row_tile(xs_ref, MOE_ROWS, c).astype(BF16) for c in range(ROW_TILES)],
                             axis=1)
        gu = _dot(xb, wgu_bf[...]) + bgu_ref[0]
        g = jnp.minimum(gu[:, :D_FF], SWIGLU_LIMIT)
        lin = jnp.clip(gu[:, D_FF:], -SWIGLU_LIMIT, SWIGLU_LIMIT)
        act = g * jax.nn.sigmoid(SWIGLU_ALPHA * g) * (lin + 1.0)
        _store_rows(ys_ref, _dot(act.astype(BF16), wd_bf[...]) + bd_ref[0])

    @pl.when(pl.program_id(0) >= used_ref[0])
    def _():
        ys_ref[...] = jnp.zeros_like(ys_ref)


def _experts(blk_expert, n_used, xs, w_gate_up, b_gate_up, w_down, b_down):
    D = w_down.shape[2]
    n_blk = xs.shape[0] // (MOE_ROWS * ROW_TILES)
    E = w_gate_up.shape[0]
    live = lambda i, used: jnp.minimum(i, used[0] - 1)
    row = pl.BlockSpec((MOE_ROWS * ROW_TILES, LANES), lambda i, blk, used: (live(i, used), 0))
    by_e = lambda shape: pl.BlockSpec((1,) + shape, lambda i, blk, used: (blk[live(i, used)], 0, 0))
    grid_spec = pltpu.PrefetchScalarGridSpec(
        num_scalar_prefetch=2, grid=(n_blk,),
        in_specs=[row, by_e((D, 2 * D_FF)), by_e((1, 2 * D_FF)), by_e((D_FF, D)), by_e((1, D))],
        out_specs=pl.BlockSpec((MOE_ROWS * ROW_TILES, LANES), lambda i, blk, used: (i, 0)),
        scratch_shapes=[pltpu.VMEM((D, 2 * D_FF), BF16), pltpu.VMEM((D_FF, D), BF16)])
    return pl.pallas_call(
        _expert_kernel, grid_spec=grid_spec, out_shape=jax.ShapeDtypeStruct(xs.shape, F32),
        compiler_params=_cparams("arbitrary"), name="experts",
    )(blk_expert, n_used, xs, w_gate_up, b_gate_up.reshape(E, 1, 2 * D_FF), w_down, b_down.reshape(E, 1, D))


def _combine_kernel(seg_ref, segn_ref, lp_ref, w4_ref, h1_ref, g_ref, b_ref, ys_ref, o_ref, buf_ref, sem):
    i = pl.program_id(0)
    n = pl.num_programs(0)
    tm = h1_ref.shape[0]
    rows = TOP_K * tm
    slot = lax.rem(i, 2)

    def row_copy(slot_):
        def make(buf_row, ys_row, n_rows):
            return pltpu.make_async_copy(ys_ref.at[_row_span(ys_row, n_rows)],
                                         buf_ref.at[slot_, _row_span(buf_row, n_rows)], sem.at[slot_])
        return make

    @pl.when(i == 0)
    def _():
        _start_segment_copies(seg_ref, tm, row_copy(slot))

    @pl.when(i + 1 < n)
    def _():
        _start_segment_copies(segn_ref, tm, row_copy(1 - slot))

    row_copy(slot)(0, 0, rows).wait()

    lp = lp_ref[...]
    s_ix = lax.broadcasted_iota(I32, (1, rows), 1).astype(F32)
    wmat = jnp.zeros((tm, rows), F32)
    for k in range(TOP_K):
        wmat = jnp.where(s_ix == lp[:, k:k + 1], w4_ref[:, k:k + 1], wmat)
    w_hi = wmat.astype(BF16)
    w_lo = (wmat - w_hi.astype(F32)).astype(BF16)
    cols = []
    for c in range(ROW_TILES):
        y = _load_row_tile(buf_ref.at[slot], rows, c)
        y_hi = y.astype(BF16)
        y_lo = (y - y_hi.astype(F32)).astype(BF16)
        cols.append(_dot(w_hi, y_hi) + _dot(w_hi, y_lo) + _dot(w_lo, y_hi))
    acc = DEEPNORM_ALPHA * h1_ref[...] + jnp.concatenate(cols, axis=1)
    o_ref[...] = _layer_norm(acc, g_ref[...], b_ref[...])


def _combine(seg4, lp, w4, h1, ln_g, ln_b, ys):
    T, D = h1.shape
    tm = TOKEN_TILE
    n_tile = T // tm
    return pl.pallas_call(
        _combine_kernel, grid=(n_tile,),
        in_specs=[_seg_spec(lambda i: (0, i, 0, 0)),
                  _seg_spec(lambda i: (0, jnp.minimum(i + 1, n_tile - 1), 0, 0)),
                  pl.BlockSpec((tm, LANES), lambda i: (i, 0)),
                  pl.BlockSpec((tm, LANES), lambda i: (i, 0)),
                  pl.BlockSpec((tm, D), lambda i: (i, 0)),
                  _full_spec((1, D)), _full_spec((1, D)),
                  pl.BlockSpec(memory_space=pl.ANY)],
        out_specs=pl.BlockSpec((tm, D), lambda i: (i, 0)),
        out_shape=jax.ShapeDtypeStruct((T, D), F32),
        scratch_shapes=[pltpu.VMEM((2, TOP_K * tm * ROW_TILES, LANES), F32), pltpu.SemaphoreType.DMA((2,))],
        compiler_params=_cparams("arbitrary"), name="combine",
    )(seg4, seg4, lp, w4, h1, ln_g, ln_b, ys)


def _rope_tables(positions):
    inv = ROPE_THETA ** (-jnp.arange(0, ROT_DIM, 2, dtype=F32) / ROT_DIM)
    ang = positions.astype(F32)[..., None] * inv
    cos_sin = jnp.concatenate([jnp.cos(ang), jnp.sin(ang)], axis=-1)
    half = ROT_DIM // 2
    spread = np.zeros((ROT_DIM, 3 * LANES), np.float32)
    unit = np.ones((1, LANES), np.float32)
    for lane in range(LANES):
        d = lane % HEAD_DIM
        if d < half:
            spread[d, lane] = 1.0
            spread[half + d, 2 * LANES + lane] = -1.0
            unit[0, lane] = 0.0
        elif d < ROT_DIM:
            spread[d - half, lane] = 1.0
            spread[d, LANES + lane] = 1.0
            unit[0, lane] = 0.0
    return cos_sin, jnp.asarray(spread, BF16), jnp.asarray(unit)


def _split_w_in(w_in):
    widths = (NSA_WIDTH,) + (KV_WIDTH,) * 6 + (NSA_HEADS * N_BRANCH, S5_WIDTH, MEM_WIDTH, N_BRANCH * D_MODEL)
    offs = [0]
    for w in widths:
        offs.append(offs[-1] + w)
    col = lambda i: w_in[:, offs[i]:offs[i + 1]]
    wq, kc, vc, ks, vs, kw, vw, wg, wu, wqm, wm = (col(i) for i in range(11))
    wk = jnp.concatenate([kc, ks, kw], axis=1)
    wv = jnp.concatenate([vc, vs, vw], axis=1)
    per_group = NSA_HPG * N_BRANCH
    wg_pad = jnp.zeros((w_in.shape[0], NSA_GROUPS * LANES), w_in.dtype)
    for g in range(NSA_GROUPS):
        wg_pad = wg_pad.at[:, g * LANES:g * LANES + per_group].set(wg[:, g * per_group:(g + 1) * per_group])
    return tuple(w.astype(BF16) for w in (wq, wk, wv, wg_pad, wu, wqm, wm))


def _compress_weights(w1):
    half = CMP_BLOCK // 2
    eye = jnp.eye(NSA_GROUPS, dtype=w1.dtype)

    def arrange(w_half):
        full = jnp.einsum('sdf,gh->sgdhf', w_half, eye)
        return full.reshape(half * NSA_GROUPS * HEAD_DIM, NSA_GROUPS * CMP_HIDDEN).astype(BF16)

    return (w1.reshape(CMP_BLOCK * HEAD_DIM, CMP_HIDDEN).astype(BF16), arrange(w1[:half]), arrange(w1[half:]))


def _s5_weights(a_re, a_im, log_dt, b_re, b_im, c_re, c_im):
    step = jnp.exp(log_dt)[:, None]
    mag = jnp.exp(a_re * step)
    ab_re, ab_im = mag * jnp.cos(a_im * step), mag * jnp.sin(a_im * step)
    den = a_re * a_re + a_im * a_im
    nr = ab_re - 1.0
    coef_re = (nr * a_re + ab_im * a_im) / den
    coef_im = (ab_im * a_re - nr * a_im) / den
    bb_re = coef_re[..., None] * b_re - coef_im[..., None] * b_im
    bb_im = coef_re[..., None] * b_im + coef_im[..., None] * b_re
    eye = jnp.eye(S5_GROUPS, dtype=F32)
    n_state = S5_GROUPS * S5_STATE
    n_tile = n_state // LANES
    in_per = n_tile // (S5_WIDTH // LANES)

    def in_map(bb):
        return jnp.einsum('gnp,gh->gphn', bb, eye).reshape(S5_WIDTH, n_state)

    def out_map(c):
        return jnp.einsum('gpn,gh->gnhp', c, eye).reshape(n_state, S5_WIDTH)

    bf_re, bf_im = in_map(bb_re), in_map(bb_im)
    cf_re, cf_im = out_map(c_re), out_map(-c_im)
    wb, wc = [], []
    for c in range(n_tile):
        i = c // in_per
        rs, cs = slice(i * LANES, (i + 1) * LANES), slice(c * LANES, (c + 1) * LANES)
        wb.append(jnp.concatenate([bf_re[rs, cs], bf_im[rs, cs]], axis=1))
        wc.append(jnp.concatenate([cf_re[cs, rs], cf_im[cs, rs]], axis=0))
    wb = jnp.stack(wb).astype(BF16)
    wc = jnp.stack(wc).astype(BF16)
    return wb, wc, ab_re.reshape(n_tile, LANES), ab_im.reshape(n_tile, LANES)


def _layer(x, mem, positions, ln_emb_g, ln_emb_b, w_in, pe_k, pe_v, w_kcmp1, w_kcmp2, w_vcmp1, w_vcmp2,
           s5_a_re, s5_a_im, s5_log_dt, s5_b_re, s5_b_im, s5_c_re, s5_c_im, s5_d,
           w_s5_glu, w_mem_kv, w_nsa_out, w_mem_out, w_o, ln1_g, ln1_b, w_router, b_router,
           w_gate_up, b_gate_up, w_down, b_down, ln2_g, ln2_b):
    B, L, D = x.shape
    T = B * L
    row = lambda v: v.reshape(1, -1)

    cos_sin, spread, unit = _rope_tables(positions)
    (q_hm, kc, vc, ks, vs, kw, vw, gates, u, qm, gm) = _inproj(
        x, row(ln_emb_g), row(ln_emb_b), cos_sin, spread, unit, *_split_w_in(w_in))

    n_chunk = L // CMP_STRIDE
    chunked = lambda t: t.reshape(B, n_chunk, CMP_STRIDE * KV_WIDTH)
    pe_rows = lambda pe: jnp.broadcast_to(pe.reshape(1, -1), (SUBLANES, CMP_BLOCK * HEAD_DIM)).astype(BF16)
    wk1f, wk1a, wk1b = _compress_weights(w_kcmp1)
    wv1f, wv1a, wv1b = _compress_weights(w_vcmp1)
    ck, cv = _compress(chunked(kc), chunked(vc), pe_rows(pe_k), pe_rows(pe_v), wk1f, wv1f,
                       wk1a, wk1b, wv1a, wv1b, w_kcmp2.astype(BF16), w_vcmp2.astype(BF16))

    per_sb = SEL_BLOCK // CMP_STRIDE
    c_ix = np.arange(n_chunk)[:, None]
    n_ix = np.arange(L // SEL_BLOCK)[None, :]
    w_score = jnp.asarray((c_ix // per_sb == n_ix).astype(np.float32)
                          + ((c_ix + 1) // per_sb == n_ix).astype(np.float32), BF16)
    o_nsa = _nsa(q_hm, ck, cv, ks, vs, kw, vw, gates, w_score)

    wb, wc, a_re, a_im = _s5_weights(s5_a_re, s5_a_im, s5_log_dt, s5_b_re, s5_b_im, s5_c_re, s5_c_im)
    gy = _s5(u, wb, wc, a_re, a_im, row(s5_d))

    k_mem, v_mem = _memkv(mem, w_mem_kv.astype(BF16))
    o_mem = _memattn(qm, k_mem, v_mem)

    pad_e = LANES - N_EXPERTS
    wr = jnp.pad(w_router, ((0, 0), (0, pad_e)))
    wr_hi = wr.astype(BF16)
    wr_lo = (wr - wr_hi.astype(F32)).astype(BF16)
    br = jnp.concatenate([b_router, jnp.full((pad_e,), -jnp.inf, F32)]).reshape(1, LANES)
    tm = TOKEN_TILE
    n_tile = T // tm
    strict_lower = lambda n: jnp.asarray(np.tril(np.ones((n, n), np.float32), -1), BF16)
    triu = jnp.asarray(np.triu(np.ones((LANES, LANES), np.float32)), BF16)
    striu = jnp.asarray(np.triu(np.ones((LANES, LANES), np.float32), 1), BF16)
    flat = lambda t: t.reshape(T, t.shape[-1])
    h1, lp, w4, cnt = _merge(
        flat(x), row(ln_emb_g), row(ln_emb_b), flat(o_nsa), flat(gy), flat(o_mem), flat(gm),
        w_nsa_out.astype(BF16), w_s5_glu.astype(BF16), w_mem_out.astype(BF16), w_o.astype(BF16),
        row(ln1_g), row(ln1_b), wr_hi, wr_lo, br, strict_lower(tm), striu)

    cap = (T * TOP_K + MOE_ROWS - 1) // MOE_ROWS * MOE_ROWS + N_EXPERTS * MOE_ROWS
    n_blk = cap // MOE_ROWS
    seg, blk_owner, misc = _slots(cnt[:, 0, :], triu, striu, strict_lower(n_tile), n_blk)
    seg4 = seg.reshape(3, n_tile, 1, LANES)
    blk_expert = blk_owner[:, 0]
    n_used = misc[0, :1]

    xs = _dispatch(seg4, misc, lp, h1, cap)
    ys = _experts(blk_expert, n_used, xs, w_gate_up, b_gate_up, w_down, b_down)
    out = _combine(seg4, lp, w4, h1, row(ln2_g), row(ln2_b), ys)
    return out.reshape(B, L, D)


def kernel(x, mem, positions, ln_emb_g, ln_emb_b, w_in, pe_k_cmp, pe_v_cmp, w_kcmp1, w_kcmp2, w_vcmp1, w_vcmp2, s5_a_re, s5_a_im, s5_log_dt, s5_b_re, s5_b_im, s5_c_re, s5_c_im, s5_d, w_s5_glu, w_mem_kv, w_nsa_out, w_mem_out, w_o, ln1_g, ln1_b, w_router, b_router, w_gate_up, b_gate_up, w_down, b_down, ln2_g, ln2_b):
    assert w_in.shape[0] == DEPTH
    l = 0
    return _layer(x, mem, positions, ln_emb_g, ln_emb_b, w_in[l], pe_k_cmp[l], pe_v_cmp[l], w_kcmp1[l],
                  w_kcmp2[l], w_vcmp1[l], w_vcmp2[l], s5_a_re[l], s5_a_im[l], s5_log_dt[l], s5_b_re[l],
                  s5_b_im[l], s5_c_re[l], s5_c_im[l], s5_d[l], w_s5_glu[l], w_mem_kv[l], w_nsa_out[l],
                  w_mem_out[l], w_o[l], ln1_g[l], ln1_b[l], w_router[l], b_router[l], w_gate_up[l],
                  b_gate_up[l], w_down[l], b_down[l], ln2_g[l], ln2_b[l])
```

```python
import functools
import math

import jax
import jax.numpy as jnp
import numpy as np
from jax import lax
from jax.experimental import pallas as pl
from jax.experimental.pallas import tpu as pltpu

F32 = jnp.float32
BF16 = jnp.bfloat16
I32 = jnp.int32

D_MODEL = 1024
NSA_HEADS = 8
NSA_GROUPS = 2
NSA_HPG = NSA_HEADS // NSA_GROUPS
HEAD_DIM = 64
NSA_WIDTH = NSA_HEADS * HEAD_DIM
KV_WIDTH = NSA_GROUPS * HEAD_DIM
CMP_BLOCK = 32
CMP_STRIDE = 16
CMP_HIDDEN = 128
SEL_BLOCK = 64
N_SEL = 16
WINDOW = 512
Q_BLOCK = 256
ROPE_THETA = 500000.0
ROT_DIM = HEAD_DIM // 4
S5_WIDTH = 512
S5_GROUP_DIM = 16
S5_GROUPS = S5_WIDTH // S5_GROUP_DIM
S5_STATE = 64
MEM_HEADS = 4
MEM_HEAD_DIM = 128
MEM_WIDTH = MEM_HEADS * MEM_HEAD_DIM
N_BRANCH = 3
N_EXPERTS = 32
TOP_K = 4
D_FF = 1024
SWIGLU_LIMIT = 7.0
SWIGLU_ALPHA = 1.702
LN_EPS = 1e-5
DEPTH = 1
DEEPNORM_ALPHA = (2 * DEPTH) ** 0.25

LANES = 128
SUBLANES = 8
VMEM_LIMIT_BYTES = 56 * 1024 * 1024

TOKEN_TILE = 256
MERGE_SORT_TILES = 2
MEM_TOKEN_TILE = 512
SEL_KV_TILE = 512
S5_CHUNK = 512
S5_PITCH = S5_CHUNK + 8
MOE_ROWS = 512
NEG_BIG = -(2.0 ** 100)
Q_SCALE_LOG2 = HEAD_DIM ** -0.5 * math.log2(math.e)


def _cparams(*sem):
    return pltpu.CompilerParams(dimension_semantics=sem, vmem_limit_bytes=VMEM_LIMIT_BYTES)


def _dot(a, b):
    return jnp.dot(a, b, preferred_element_type=F32)


def _dot_nt(a, b):
    return lax.dot_general(a, b, (((1,), (1,)), ((), ())), preferred_element_type=F32)


def _layer_norm(x, g, b):
    mu = jnp.mean(x, axis=-1, keepdims=True)
    xc = x - mu
    var = jnp.mean(xc * xc, axis=-1, keepdims=True)
    return xc * lax.rsqrt(var + LN_EPS) * g + b


def _gelu_tanh(x):
    cdf = 0.5 * (1.0 + jnp.tanh(math.sqrt(2.0 / math.pi) * (x + 0.044715 * (x * x * x))))
    return x * cdf


def _masked_exp2(s, mask):
    s = jnp.where(mask, s, -jnp.inf)
    m = jnp.max(s, axis=-1, keepdims=True)
    m = jnp.where(m > -jnp.inf, m, 0.0)
    return jnp.exp2(s - m)


def _safe_recip(denom):
    return 1.0 / jnp.maximum(denom, jnp.finfo(F32).tiny)


def _split3(x):
    hi = x.astype(BF16)
    r1 = x - hi.astype(F32)
    mid = r1.astype(BF16)
    lo = (r1 - mid.astype(F32)).astype(BF16)
    return hi, mid, lo


def _full_spec(shape):
    nd = len(shape)
    return pl.BlockSpec(shape, lambda *_: (0,) * nd)


def _inproj_kernel(x_ref, g_ref, b_ref, cs_ref, spread_ref, unit_ref,
                   wq_ref, wk_ref, wv_ref, wg_ref, wu_ref, wqm_ref, wm_ref,
                   q_ref, kc_ref, vc_ref, ks_ref, vs_ref, kw_ref, vw_ref,
                   gate_ref, u_ref, qm_ref, gm_ref):
    h = _layer_norm(x_ref[0], g_ref[...], b_ref[...])
    hb = h.astype(BF16)
    tab = sum(_dot(part, spread_ref[...]) for part in _split3(cs_ref[0]))
    cos_t = tab[:, 0:LANES] + unit_ref[...]
    sin_a = tab[:, LANES:2 * LANES]
    sin_b = tab[:, 2 * LANES:3 * LANES]

    def rope(t):
        return (t * cos_t + pltpu.roll(t, ROT_DIM // 2, 1) * sin_a
                + pltpu.roll(t, LANES - ROT_DIM // 2, 1) * sin_b)

    q = _dot(hb, wq_ref[...])
    for c in range(NSA_WIDTH // LANES):
        qc = rope(q[:, c * LANES:(c + 1) * LANES]) * Q_SCALE_LOG2
        for hh in range(2):
            q_ref[0, 2 * c + hh] = qc[:, hh * HEAD_DIM:(hh + 1) * HEAD_DIM].astype(BF16)
    k3 = _dot(hb, wk_ref[...])
    kc = rope(k3[:, 0:LANES])
    ks = rope(k3[:, LANES:2 * LANES])
    kw = rope(k3[:, 2 * LANES:3 * LANES])
    v3 = _dot(hb, wv_ref[...])
    kc_ref[0] = kc.astype(BF16)
    vc_ref[0] = v3[:, 0:LANES].astype(BF16)
    tm = x_ref.shape[1]
    pos = pl.program_id(1) * tm + lax.broadcasted_iota(I32, (tm, LANES), 0)
    blk_hot = jnp.where(lax.broadcasted_iota(I32, (tm, LANES), 1) == pos // SEL_BLOCK, 1.0, 0.0)
    lane_pad = jnp.zeros((tm, LANES - HEAD_DIM), F32)
    ones_pad = jnp.where(lax.broadcasted_iota(I32, (tm, LANES - HEAD_DIM), 1) == 0, 1.0, 0.0)
    for g in range(NSA_GROUPS):
        sl = slice(g * HEAD_DIM, (g + 1) * HEAD_DIM)
        ks_ref[0, g] = jnp.concatenate([blk_hot, ks[:, sl], lane_pad], axis=1).astype(BF16)
        kw_ref[0, g] = kw[:, sl].astype(BF16)
        vs_ref[0, g] = jnp.concatenate([v3[:, LANES:2 * LANES][:, sl], ones_pad], axis=1).astype(BF16)
        vw_ref[0, g] = jnp.concatenate([v3[:, 2 * LANES:3 * LANES][:, sl], ones_pad], axis=1).astype(BF16)
    gate_ref[0] = jax.nn.sigmoid(_dot(hb, wg_ref[...]))
    u_ref[0] = _dot(hb, wu_ref[...])
    qm_ref[0] = _dot(hb, wqm_ref[...]).astype(BF16)
    gm_ref[0] = jax.nn.sigmoid(_dot(hb, wm_ref[...]))


def _inproj(x, ln_g, ln_b, cos_sin, spread, unit, wq, wk, wv, wg, wu, wqm, wm):
    B, L, D = x.shape
    tm = TOKEN_TILE
    grid = (B, L // tm)
    tok = lambda w: pl.BlockSpec((1, tm, w), lambda b, i: (b, i, 0))
    head = lambda n, w=HEAD_DIM: pl.BlockSpec((1, n, tm, w), lambda b, i: (b, 0, i, 0))
    in_specs = [tok(D), _full_spec((1, D)), _full_spec((1, D)), tok(ROT_DIM), _full_spec(spread.shape),
                _full_spec(unit.shape)]
    in_specs += [_full_spec(w.shape) for w in (wq, wk, wv, wg, wu, wqm, wm)]
    sd = jax.ShapeDtypeStruct
    out_shape = [
        sd((B, NSA_HEADS, L, HEAD_DIM), BF16),
        sd((B, L, KV_WIDTH), BF16), sd((B, L, KV_WIDTH), BF16),
        sd((B, NSA_GROUPS, L, 2 * LANES), BF16), sd((B, NSA_GROUPS, L, LANES), BF16),
        sd((B, NSA_GROUPS, L, HEAD_DIM), BF16), sd((B, NSA_GROUPS, L, LANES), BF16),
        sd((B, L, NSA_GROUPS * LANES), F32),
        sd((B, L, S5_WIDTH), F32),
        sd((B, L, MEM_WIDTH), BF16),
        sd((B, L, N_BRANCH * D), F32),
    ]
    out_specs = [head(NSA_HEADS), tok(KV_WIDTH), tok(KV_WIDTH), head(NSA_GROUPS, 2 * LANES),
                 head(NSA_GROUPS, LANES), head(NSA_GROUPS), head(NSA_GROUPS, LANES),
                 tok(NSA_GROUPS * LANES), tok(S5_WIDTH),
                 tok(MEM_WIDTH), tok(N_BRANCH * D)]
    return pl.pallas_call(
        _inproj_kernel, grid=grid, in_specs=in_specs, out_specs=out_specs, out_shape=out_shape,
        compiler_params=_cparams("parallel", "parallel"), name="inproj",
    )(x, ln_g, ln_b, cos_sin, spread, unit, wq, wk, wv, wg, wu, wqm, wm)


def _compress_kernel(kc_ref, vc_ref, pek_ref, pev_ref, wk1f_ref, wv1f_ref,
                     wk1a_ref, wk1b_ref, wv1a_ref, wv1b_ref, wk2_ref, wv2_ref, ck_ref, cv_ref):
    n_chunk = kc_ref.shape[1]
    row = lax.broadcasted_iota(I32, (n_chunk, 1), 0)

    def one(x_ref, pe_ref, w1f_ref, w1a_ref, w1b_ref, w2_ref, o_ref):
        x = x_ref[0]
        first = _dot(x, w1a_ref[...])
        second = _dot(x, w1b_ref[...])
        second = pltpu.roll(second, n_chunk - 1, 0)
        pe_term = _dot(pe_ref[...], w1f_ref[...])[0:1]
        pe_term = jnp.concatenate([pe_term] * NSA_GROUPS, axis=1)
        hid = _gelu_tanh(first + second + pe_term).astype(BF16)
        for g in range(NSA_GROUPS):
            o = _dot(hid[:, g * CMP_HIDDEN:(g + 1) * CMP_HIDDEN], w2_ref[...])
            o_ref[0, g] = jnp.where(row < n_chunk - 1, o, 0.0).astype(BF16)

    one(kc_ref, pek_ref, wk1f_ref, wk1a_ref, wk1b_ref, wk2_ref, ck_ref)
    one(vc_ref, pev_ref, wv1f_ref, wv1a_ref, wv1b_ref, wv2_ref, cv_ref)


def _compress(kc_r, vc_r, pek, pev, wk1f, wv1f, wk1a, wk1b, wv1a, wv1b, wk2, wv2):
    B, n_chunk, width = kc_r.shape
    blk = pl.BlockSpec((1, n_chunk, width), lambda b: (b, 0, 0))
    out = pl.BlockSpec((1, NSA_GROUPS, n_chunk, HEAD_DIM), lambda b: (b, 0, 0, 0))
    ws = [pek, pev, wk1f, wv1f, wk1a, wk1b, wv1a, wv1b, wk2, wv2]
    sd = jax.ShapeDtypeStruct((B, NSA_GROUPS, n_chunk, HEAD_DIM), BF16)
    return pl.pallas_call(
        _compress_kernel, grid=(B,), in_specs=[blk, blk] + [_full_spec(w.shape) for w in ws],
        out_specs=[out, out], out_shape=[sd, sd], compiler_params=_cparams("parallel"), name="compress",
    )(kc_r, vc_r, *ws)


def _nsa_kernel(q_ref, ck_ref, cv_ref, ks_ref, vs_ref, kw_ref, vw_ref, gate_ref, wsc_ref, o_ref):
    seq_len = ks_ref.shape[2]
    n_cmp = ck_ref.shape[2]
    n_sb = seq_len // SEL_BLOCK
    n_sel = min(N_SEL, n_sb)
    rows = NSA_HPG * Q_BLOCK
    groups = range(NSA_GROUPS)
    q0 = pl.program_id(1) * Q_BLOCK
    t1 = q0 + lax.broadcasted_iota(I32, (Q_BLOCK, 1), 0)
    t4 = jnp.concatenate([t1] * NSA_HPG, axis=0)
    tk = SEL_KV_TILE

    def front(g):
        q = q_ref[0, g * NSA_HPG:(g + 1) * NSA_HPG].reshape(rows, HEAD_DIM)

        s = _dot_nt(q, ck_ref[0, g])
        c_end = lax.broadcasted_iota(I32, (1, n_cmp), 1) * CMP_STRIDE + (CMP_BLOCK - 1)
        e = _masked_exp2(s, c_end <= t4)
        p_cmp = e * _safe_recip(jnp.sum(e, axis=-1, keepdims=True))
        o_cmp = _dot(p_cmp.astype(BF16), cv_ref[0, g])

        imp = p_cmp[0:Q_BLOCK]
        for hh in range(1, NSA_HPG):
            imp = imp + p_cmp[hh * Q_BLOCK:(hh + 1) * Q_BLOCK]
        w_sc = wsc_ref[...]
        score = sum(_dot(part, w_sc) for part in _split3(imp))
        score_t = score.T
        jb = lax.broadcasted_iota(I32, (n_sb, Q_BLOCK), 0)
        tb = (q0 + lax.broadcasted_iota(I32, (1, Q_BLOCK), 1)) // SEL_BLOCK
        forced = (jb == 0) | (jb == tb) | (jb == tb - 1)
        work = jnp.where(forced | (jb > tb), -jnp.inf, score_t)
        bias_t = jnp.where(forced, 0.0, NEG_BIG)
        jbf = jb.astype(F32)
        for _ in range(n_sel - 3):
            m = jnp.max(work, axis=0, keepdims=True)
            idx = jnp.min(jnp.where(work == m, jbf, float(n_sb)), axis=0, keepdims=True)
            pick = jbf == idx
            bias_t = jnp.where(pick, 0.0, bias_t)
            work = jnp.where(pick, -jnp.inf, work)
        sel_bias = bias_t.T
        if n_sb < LANES:
            sel_bias = jnp.concatenate([sel_bias, jnp.zeros((Q_BLOCK, LANES - n_sb), F32)], axis=1)

        span = WINDOW + Q_BLOCK
        w0 = pl.multiple_of(jnp.maximum(q0 - WINDOW, 0), Q_BLOCK)
        s = _dot_nt(q, kw_ref[0, g, pl.ds(w0, span), :])
        diff = t4 - (w0 + lax.broadcasted_iota(I32, (1, span), 1))
        e = _masked_exp2(s, (diff >= 0) & (diff < WINDOW))
        o_win = _dot(e.astype(BF16), vw_ref[0, g, pl.ds(w0, span), :])
        o_win = o_win[:, :HEAD_DIM] * _safe_recip(o_win[:, HEAD_DIM:HEAD_DIM + 1])

        q_aug = jnp.concatenate([jnp.concatenate([sel_bias.astype(BF16)] * NSA_HPG, axis=0), q,
                                 jnp.zeros((rows, LANES - HEAD_DIM), BF16)], axis=1)
        return q_aug, o_cmp, o_win

    fronts = [front(g) for g in groups]

    def sel_tile(g, j, carry, causal):
        m_run, acc = carry
        k0 = pl.multiple_of(j * tk, tk)
        sc = _dot_nt(fronts[g][0], ks_ref[0, g, pl.ds(k0, tk), :])
        if causal:
            kpos = k0 + lax.broadcasted_iota(I32, (1, tk), 1)
            sc = jnp.where(kpos <= t4, sc, NEG_BIG)
        m_new = jnp.maximum(m_run, jnp.max(sc, axis=-1, keepdims=True))
        p = jnp.exp2(sc - m_new)
        acc_new = jnp.exp2(m_run - m_new) * acc + _dot(p.astype(BF16), vs_ref[0, g, pl.ds(k0, tk), :])
        return m_new, acc_new

    def sel_pair(jj, carries, causal):
        return tuple(sel_tile(g, 2 * jj + 1, sel_tile(g, 2 * jj, carries[g], causal), causal) for g in groups)

    init = tuple((jnp.full((rows, 1), NEG_BIG, F32), jnp.zeros((rows, LANES), F32)) for _ in groups)
    last_pair = (q0 // tk) // 2
    carries = lax.fori_loop(0, last_pair, functools.partial(sel_pair, causal=False), init)
    carries = sel_pair(last_pair, carries, True)

    outs = []
    for g in groups:
        _, o_cmp, o_win = fronts[g]
        acc = carries[g][1]
        o_sel = acc[:, :HEAD_DIM] * (1.0 / acc[:, HEAD_DIM:HEAD_DIM + 1])
        gt = gate_ref[0, :, g * LANES:(g + 1) * LANES]
        for hh in range(NSA_HPG):
            sl = slice(hh * Q_BLOCK, (hh + 1) * Q_BLOCK)
            c = hh * N_BRANCH
            outs.append(o_cmp[sl] * gt[:, c:c + 1] + o_sel[sl] * gt[:, c + 1:c + 2]
                        + o_win[sl] * gt[:, c + 2:c + 3])
    o_ref[0] = jnp.concatenate(outs, axis=1).astype(BF16)


def _nsa(q_hm, ck, cv, ks, vs, kw, vw, gates, w_score):
    B, _, L, _ = q_hm.shape
    assert L // SEL_BLOCK <= LANES and (L // SEL_KV_TILE) % 2 == 0 and L >= WINDOW + Q_BLOCK
    n_cmp = ck.shape[2]
    grid = (B, L // Q_BLOCK)
    qspec = pl.BlockSpec((1, NSA_HEADS, Q_BLOCK, HEAD_DIM), lambda b, i: (b, 0, i, 0))
    cspec = pl.BlockSpec((1, NSA_GROUPS, n_cmp, HEAD_DIM), lambda b, i: (b, 0, 0, 0))
    kvspec = lambda w: pl.BlockSpec((1, NSA_GROUPS, L, w), lambda b, i: (b, 0, 0, 0),
                                    pipeline_mode=pl.Buffered(1))
    gspec = pl.BlockSpec((1, Q_BLOCK, NSA_GROUPS * LANES), lambda b, i: (b, i, 0))
    ospec = pl.BlockSpec((1, Q_BLOCK, NSA_WIDTH), lambda b, i: (b, i, 0))
    return pl.pallas_call(
        _nsa_kernel, grid=grid,
        in_specs=[qspec, cspec, cspec, kvspec(2 * LANES), kvspec(LANES), kvspec(HEAD_DIM), kvspec(LANES), gspec,
                  _full_spec(w_score.shape)],
        out_specs=ospec, out_shape=jax.ShapeDtypeStruct((B, L, NSA_WIDTH), BF16),
        compiler_params=_cparams("parallel", "arbitrary"), name="nsa",
    )(q_hm, ck, cv, ks, vs, kw, vw, gates, w_score)


def _s5_kernel(u_ref, wb_ref, wc_ref, are_ref, aim_ref, d_ref, y_ref, sre_ref, sim_ref, carry_ref):
    n_b, chunk, _ = u_ref.shape
    n_tile = wb_ref.shape[0]
    in_per = n_tile // (S5_WIDTH // LANES)
    pitch = S5_PITCH

    @pl.when(pl.program_id(0) == 0)
    def _():
        carry_ref[...] = jnp.zeros_like(carry_ref)

    for b in range(n_b):
        for c in range(n_tile):
            i = c // in_per
            ub = u_ref[b, :, i * LANES:(i + 1) * LANES].astype(BF16)
            r = _dot(ub, wb_ref[c])
            sre_ref[b, c * pitch:c * pitch + chunk, :] = r[:, :LANES]
            sim_ref[b, c * pitch:c * pitch + chunk, :] = r[:, LANES:]

    a_re, a_im = are_ref[...], aim_ref[...]

    def step(t, carry):
        out = []
        for b in range(n_b):
            s_re, s_im = carry[2 * b], carry[2 * b + 1]
            rows = pl.ds(t, n_tile, stride=pitch)
            n_re = a_re * s_re - a_im * s_im + sre_ref[b, rows, :]
            n_im = a_re * s_im + a_im * s_re + sim_ref[b, rows, :]
            sre_ref[b, rows, :] = n_re
            sim_ref[b, rows, :] = n_im
            out += [n_re, n_im]
        return tuple(out)

    init = tuple(carry_ref[i] for i in range(2 * n_b))
    fin = lax.fori_loop(0, chunk, step, init, unroll=8)
    for i in range(2 * n_b):
        carry_ref[i] = fin[i]

    for b in range(n_b):
        for o in range(S5_WIDTH // LANES):
            acc = jnp.zeros((chunk, LANES), F32)
            for c in range(o * in_per, (o + 1) * in_per):
                rows = slice(c * pitch, c * pitch + chunk)
                state = jnp.concatenate([sre_ref[b, rows, :], sim_ref[b, rows, :]], axis=1).astype(BF16)
                acc = acc + _dot(state, wc_ref[c])
            lanes = slice(o * LANES, (o + 1) * LANES)
            y = acc + d_ref[:, lanes] * u_ref[b, :, lanes]
            y_ref[b, :, lanes] = _gelu_tanh(y).astype(BF16)


def _s5(u, wb, wc, a_re, a_im, d_skip):
    B, L, W = u.shape
    chunk = S5_CHUNK
    n_tile = wb.shape[0]
    blk = pl.BlockSpec((B, chunk, W), lambda i: (0, i, 0))
    slab = pltpu.VMEM((B, n_tile * S5_PITCH, LANES), F32)
    return pl.pallas_call(
        _s5_kernel, grid=(L // chunk,),
        in_specs=[blk] + [_full_spec(w.shape) for w in (wb, wc, a_re, a_im, d_skip)],
        out_specs=blk, out_shape=jax.ShapeDtypeStruct((B, L, W), BF16),
        scratch_shapes=[slab, slab, pltpu.VMEM((2 * B, n_tile, LANES), F32)],
        compiler_params=_cparams("arbitrary"), name="s5",
    )(u, wb, wc, a_re, a_im, d_skip)


def _memkv_kernel(mem_ref, w_ref, k_ref, v_ref):
    kv = _dot(mem_ref[0].astype(BF16), w_ref[...])
    k_ref[0] = kv[:, :MEM_WIDTH].astype(BF16)
    v_ref[0] = kv[:, MEM_WIDTH:].astype(BF16)


def _memkv(mem, w_kv):
    B, M, D = mem.shape
    out = pl.BlockSpec((1, M, MEM_WIDTH), lambda b: (b, 0, 0))
    sd = jax.ShapeDtypeStruct((B, M, MEM_WIDTH), BF16)
    return pl.pallas_call(
        _memkv_kernel, grid=(B,),
        in_specs=[pl.BlockSpec((1, M, D), lambda b: (b, 0, 0)), _full_spec(w_kv.shape)],
        out_specs=[out, out], out_shape=[sd, sd], compiler_params=_cparams("parallel"), name="memkv",
    )(mem, w_kv)


def _memattn_kernel(q_ref, k_ref, v_ref, o_ref):
    outs = []
    for h in range(MEM_HEADS):
        sl = slice(h * MEM_HEAD_DIM, (h + 1) * MEM_HEAD_DIM)
        s = _dot_nt(q_ref[0, :, sl], k_ref[0, :, sl]) * (MEM_HEAD_DIM ** -0.5)
        m = jnp.max(s, axis=-1, keepdims=True)
        e = jnp.exp(s - m)
        p = e / jnp.sum(e, axis=-1, keepdims=True)
        outs.append(_dot(p.astype(BF16), v_ref[0, :, sl]))
    o_ref[0] = jnp.concatenate(outs, axis=1).astype(BF16)


def _memattn(qm, k, v):
    B, L, W = qm.shape
    M = k.shape[1]
    tm = MEM_TOKEN_TILE
    tok = pl.BlockSpec((1, tm, W), lambda b, i: (b, i, 0))
    kv = pl.BlockSpec((1, M, W), lambda b, i: (b, 0, 0))
    return pl.pallas_call(
        _memattn_kernel, grid=(B, L // tm), in_specs=[tok, kv, kv], out_specs=tok,
        out_shape=jax.ShapeDtypeStruct((B, L, W), BF16),
        compiler_params=_cparams("parallel", "parallel"), name="memattn",
    )(qm, k, v)


def _merge_kernel(x_ref, lng_ref, lnb_ref, on_ref, gy_ref, om_ref, gm_ref,
                  wn_ref, wglu_ref, wmo_ref, wo_ref, l1g_ref, l1b_ref,
                  wrh_ref, wrl_ref, br_ref, tri_ref, striu_ref,
                  h1_ref, lp_ref, w4_ref, cnt_ref):
    D = x_ref.shape[1]
    tm = x_ref.shape[0]
    h =_layer_norm(x_ref[...], lng_ref[...], lnb_ref[...])
    y_nsa = _dot(on_ref[...], wn_ref[...])
    glu = _dot(gy_ref[...], wglu_ref[...])
    y_s5 = glu[:, :D] * jax.nn.sigmoid(glu[:, D:])
    y_mem = _dot(om_ref[...], wmo_ref[...])
    merged = gm_ref[:, 0:D] * y_nsa + gm_ref[:, D:2 * D] * y_s5 + gm_ref[:, 2 * D:3 * D] * y_mem
    mix = _dot(merged.astype(BF16), wo_ref[...])
    h1 = _layer_norm(DEEPNORM_ALPHA * h + mix, l1g_ref[...], l1b_ref[...])
    h1_ref[...] = h1

    hh = h1.astype(BF16)
    hl = (h1 - hh.astype(F32)).astype(BF16)
    logits = _dot(hh, wrh_ref[...]) + _dot(hh, wrl_ref[...]) + _dot(hl, wrh_ref[...]) + br_ref[...]
    lane = lax.broadcasted_iota(I32, (tm, LANES), 1)
    lane_f = lane.astype(F32)
    work = logits
    multi = jnp.zeros((tm, LANES), F32)
    vals, picks = [], []
    for _ in range(TOP_K):
        m = jnp.max(work, axis=-1, keepdims=True)
        idx = jnp.min(jnp.where(work == m, lane_f, float(LANES)), axis=-1, keepdims=True)
        pick = lane_f == idx
        vals.append(m)
        picks.append((pick, idx))
        multi = jnp.where(pick, 1.0, multi)
        work = jnp.where(pick, -jnp.inf, work)
    es = [jnp.exp(v - vals[0]) for v in vals]
    den = es[0] + es[1] + es[2] + es[3]
    st = TOKEN_TILE
    pos = []
    for t in range(tm // st):
        multi_t = multi[t * st:(t + 1) * st]
        cnt = jnp.broadcast_to(jnp.sum(multi_t, axis=0, keepdims=True), (SUBLANES, LANES))
        cnt_ref[t] = cnt
        lower = _dot(cnt.astype(BF16), striu_ref[...])[0:1]
        pos.append(lower + _dot(tri_ref[...], multi_t.astype(BF16)))
    pos = jnp.concatenate(pos, axis=0)
    lp = jnp.full((tm, LANES), -1.0, F32)
    w4 = jnp.zeros((tm, LANES), F32)
    for k in range(TOP_K):
        pick, _ = picks[k]
        lp = jnp.where(lane == k, jnp.sum(jnp.where(pick, pos, 0.0), axis=-1, keepdims=True), lp)
        w4 = jnp.where(lane == k, es[k] / den, w4)
    lp_ref[...] = lp
    w4_ref[...] = w4


def _merge(x2, lng, lnb, o_nsa, gy, om, gm, wn, wglu, wmo, wo, l1g, l1b, wrh, wrl, br, tri, striu):
    T, D = x2.shape
    tm = MERGE_SORT_TILES * TOKEN_TILE
    tok = lambda w: pl.BlockSpec((tm, w), lambda i: (i, 0))
    ws = [wn, wglu, wmo, wo, l1g, l1b, wrh, wrl, br, tri, striu]
    sd = jax.ShapeDtypeStruct
    lane_out = sd((T, LANES), F32)
    return pl.pallas_call(
        _merge_kernel, grid=(T // tm,),
        in_specs=[tok(D), _full_spec((1, D)), _full_spec((1, D)), tok(NSA_WIDTH), tok(S5_WIDTH),
                  tok(MEM_WIDTH), tok(N_BRANCH * D)] + [_full_spec(w.shape) for w in ws],
        out_specs=[tok(D), tok(LANES), tok(LANES),
                   pl.BlockSpec((MERGE_SORT_TILES, SUBLANES, LANES), lambda i: (i, 0, 0))],
        out_shape=[sd((T, D), F32), lane_out, lane_out, sd((T // TOKEN_TILE, SUBLANES, LANES), F32)],
        compiler_params=_cparams("parallel"), name="merge",
    )(x2, lng, lnb, o_nsa, gy, om, gm, *ws)


def _slots_kernel(cnt_ref, triu_ref, striu_ref, tril_ref, seg_ref, blk_ref, misc_ref):
    n_blk = blk_ref.shape[0]
    cnt = cnt_ref[...]
    cnt_b = cnt.astype(BF16)
    total = jnp.sum(cnt, axis=0, keepdims=True)
    nblk_e = jnp.floor((total + (MOE_ROWS - 1)) * (1.0 / MOE_ROWS))
    nblk_8 = jnp.broadcast_to(nblk_e, (SUBLANES, LANES))
    end_b = _dot(nblk_8.astype(BF16), triu_ref[...])
    start_rows = (end_b - nblk_8)[0:1] * MOE_ROWS
    dst = start_rows + _dot(tril_ref[...], cnt_b)
    off = _dot(cnt_b, striu_ref[...])
    seg_ref[0] = cnt.astype(I32)
    seg_ref[1] = off.astype(I32)
    seg_ref[2] = dst.astype(I32)
    blk_i = lax.broadcasted_iota(I32, (n_blk, LANES), 0).astype(F32)
    lane_b = lax.broadcasted_iota(I32, (n_blk, LANES), 1)
    ended = jnp.where((end_b[0:1] <= blk_i) & (lane_b < N_EXPERTS), 1.0, 0.0)
    owner = jnp.minimum(jnp.sum(ended, axis=-1, keepdims=True), float(N_EXPERTS - 1))
    blk_ref[...] = jnp.broadcast_to(owner, (n_blk, LANES)).astype(I32)
    cand = lax.broadcasted_iota(I32, (LANES, LANES), 0)
    has_blocks = jnp.broadcast_to(nblk_e, (LANES, LANES)).T > 0.0
    later = (cand > lax.broadcasted_iota(I32, (LANES, LANES), 1)) & has_blocks
    nxt = jnp.min(jnp.where(later, cand.astype(F32), float(LANES)), axis=0, keepdims=True)
    nxt = jnp.where(nxt < float(LANES), nxt, -1.0)
    lane8 = lax.broadcasted_iota(I32, (SUBLANES, LANES), 1)
    row8 = lax.broadcasted_iota(I32, (SUBLANES, LANES), 0)
    used = jnp.sum(jnp.where(lane8 == N_EXPERTS - 1, end_b, 0.0), axis=-1, keepdims=True)
    misc = jnp.where(row8 == 0, used,
                     jnp.where(row8 == 1, start_rows + total,
                               jnp.where(row8 == 2, nblk_e * MOE_ROWS - total, nxt)))
    misc_ref[...] = misc.astype(I32)


def _slots(cnt, triu, striu, tril, n_blk):
    n_tile = cnt.shape[0]
    sd = jax.ShapeDtypeStruct
    return pl.pallas_call(
        _slots_kernel, grid=(1,),
        in_specs=[_full_spec(cnt.shape), _full_spec(triu.shape), _full_spec(striu.shape), _full_spec(tril.shape)],
        out_specs=[_full_spec((3, n_tile, LANES)), _full_spec((n_blk, LANES)), _full_spec((SUBLANES, LANES))],
        out_shape=[sd((3, n_tile, LANES), I32), sd((n_blk, LANES), I32), sd((SUBLANES, LANES), I32)],
        compiler_params=_cparams("arbitrary"), name="slots",
    )(cnt, triu, striu, tril)


ROW_TILES = D_MODEL // LANES


def _row_span(row, n_rows):
    start = row * ROW_TILES
    if not isinstance(start, int):
        start = pl.multiple_of(start, ROW_TILES)
    return pl.ds(start, n_rows * ROW_TILES)


def _store_rows(ref, val):
    for c in range(ROW_TILES):
        ref[pl.ds(c, val.shape[0], stride=ROW_TILES), :] = val[:, c * LANES:(c + 1) * LANES]


def _load_row_tile(ref, n_rows, c):
    return ref[pl.ds(c, n_rows, stride=ROW_TILES), :]


BIG_PIECE_ROWS = 64


def _pieces(count, max_rows, fn):
    def run(sizes):
        for p in sizes:
            def piece(p=p):
                fn(count & (-2 * p), p)
            pl.when((count & p) != 0)(piece)

    sizes = [max_rows >> s for s in range(max_rows.bit_length())]
    big = [p for p in sizes if p >= BIG_PIECE_ROWS]
    if big:
        pl.when(count >= BIG_PIECE_ROWS)(lambda: run(big))
    run([p for p in sizes if p < BIG_PIECE_ROWS])


def _start_segment_copies(seg_ref, max_rows, make_copy):
    def per_expert(e, c):
        cnt, off, dst = seg_ref[0, 0, 0, e], seg_ref[1, 0, 0, e], seg_ref[2, 0, 0, e]
        _pieces(cnt, max_rows, lambda first, rows: make_copy(off + first, dst + first, rows).start())
        return c

    lax.fori_loop(0, N_EXPERTS, per_expert, 0)


def _dispatch_kernel(seg_ref, misc_ref, lp_ref, h_ref, xs_ref, sorted_ref, zero_ref, sem, pad_sem):
    i = pl.program_id(0)
    n = pl.num_programs(0)
    tm, D = h_ref.shape
    rows = TOP_K * tm
    slot = lax.rem(i, 2)

    def row_copy(slot_):
        def make(src_row, dst_row, n_rows):
            return pltpu.make_async_copy(sorted_ref.at[slot_, _row_span(src_row, n_rows)],
                                         xs_ref.at[_row_span(dst_row, n_rows)], sem.at[slot_])
        return make

    @pl.when(i == 0)
    def _():
        zero_ref[...] = jnp.zeros_like(zero_ref)
        for wait in (False, True):
            def per_expert(e, c, wait=wait):
                def one(first, n_rows):
                    cp = pltpu.make_async_copy(zero_ref.at[_row_span(0, n_rows)],
                                               xs_ref.at[_row_span(misc_ref[1, e] + first, n_rows)], pad_sem)
                    cp.wait() if wait else cp.start()
                _pieces(misc_ref[2, e], MOE_ROWS // 2, one)
                return c
            lax.fori_loop(0, N_EXPERTS, per_expert, 0)

            def per_spare_half_block(hb, c, wait=wait):
                cp = pltpu.make_async_copy(zero_ref, xs_ref.at[_row_span(hb * (MOE_ROWS // 2), MOE_ROWS // 2)],
                                           pad_sem)
                cp.wait() if wait else cp.start()
                return c
            lax.fori_loop(2 * misc_ref[0, 0], 2 * (xs_ref.shape[0] // (MOE_ROWS * ROW_TILES)),
                          per_spare_half_block, 0)

    lp_t = lp_ref[...].T
    s_ix = lax.broadcasted_iota(I32, (rows, 1), 0).astype(F32)
    hit = s_ix == lp_t[0:1, :]
    for k in range(1, TOP_K):
        hit = hit | (s_ix == lp_t[k:k + 1, :])
    perm = jnp.where(hit, 1.0, 0.0).astype(BF16)
    _store_rows(sorted_ref.at[slot], _dot(perm, h_ref[...].astype(BF16)))

    _start_segment_copies(seg_ref, tm, row_copy(slot))

    @pl.when(i > 0)
    def _():
        row_copy(1 - slot)(0, 0, rows).wait()

    @pl.when(i == n - 1)
    def _():
        row_copy(slot)(0, 0, rows).wait()


def _seg_spec(index_map):
    return pl.BlockSpec((3, 1, 1, LANES), index_map, memory_space=pltpu.SMEM)


def _dispatch(seg4, misc, lp, h1, cap):
    T, D = h1.shape
    assert D == ROW_TILES * LANES
    tm = TOKEN_TILE
    tok = lambda w: pl.BlockSpec((tm, w), lambda i: (i, 0))
    return pl.pallas_call(
        _dispatch_kernel, grid=(T // tm,),
        in_specs=[_seg_spec(lambda i: (0, i, 0, 0)), pl.BlockSpec(memory_space=pltpu.SMEM), tok(LANES), tok(D)],
        out_specs=pl.BlockSpec(memory_space=pl.ANY),
        out_shape=jax.ShapeDtypeStruct((cap * ROW_TILES, LANES), F32),
        scratch_shapes=[pltpu.VMEM((2, TOP_K * tm * ROW_TILES, LANES), F32),
                        pltpu.VMEM((MOE_ROWS // 2 * ROW_TILES, LANES), F32),
                        pltpu.SemaphoreType.DMA((2,)), pltpu.SemaphoreType.DMA(())],
        compiler_params=_cparams("arbitrary"), name="dispatch",
    )(seg4, misc, lp, h1)


def _expert_kernel(blk_ref, used_ref, next_ref, xs_ref, wgu_hbm, bgu_ref, wd_hbm, bd_ref, ys_ref,
                   wgu_f32, wd_f32, wgu_bf, wd_bf, sem, run_ref):
    i = pl.program_id(0)
    live = i < used_ref[0]
    expert = blk_ref[i]

    def weight_copies(e, slot):
        return (pltpu.make_async_copy(wgu_hbm.at[e], wgu_f32.at[slot], sem.at[0, slot]),
                pltpu.make_async_copy(wd_hbm.at[e], wd_f32.at[slot], sem.at[1, slot]))

    @pl.when(i == 0)
    def _():
        run_ref[0] = 0
        for cp in weight_copies(expert, 0):
            cp.start()

    @pl.when(live & ((i == 0) | (expert != blk_ref[jnp.maximum(i - 1, 0)])))
    def _():
        slot = lax.rem(run_ref[0], 2)
        for cp in weight_copies(expert, slot):
            cp.wait()
        wgu_bf[...] = wgu_f32[slot].astype(BF16)
        wd_bf[...] = wd_f32[slot].astype(BF16)
        nxt = next_ref[expert]

        @pl.when(nxt >= 0)
        def _():
            for cp in weight_copies(nxt, 1 - slot):
                cp.start()

        run_ref[0] = run_ref[0] + 1

    @pl.when(live)
    def _():
        xb = jnp.concatenate([_load_row_tile(xs_ref, MOE_ROWS, c).astype(BF16) for c in range(ROW_TILES)],
                             axis=1)
        gu = _dot(xb, wgu_bf[...]) + bgu_ref[0]
        g = jnp.minimum(gu[:, :D_FF], SWIGLU_LIMIT)
        lin = jnp.clip(gu[:, D_FF:], -SWIGLU_LIMIT, SWIGLU_LIMIT)
        act = g * jax.nn.sigmoid(SWIGLU_ALPHA * g) * (lin + 1.0)
        _store_rows(ys_ref, _dot(act.astype(BF16), wd_bf[...]) + bd_ref[0])

    @pl.when(pl.program_id(0) >= used_ref[0])
    def _():
        ys_ref[...] = jnp.zeros_like(ys_ref)


def _experts(blk_expert, n_used, next_expert, xs, w_gate_up, b_gate_up, w_down, b_down):
    D = w_down.shape[2]
    n_blk = xs.shape[0] // (MOE_ROWS * ROW_TILES)
    E = w_gate_up.shape[0]
    live = lambda i, used: jnp.minimum(i, used[0] - 1)
    row = pl.BlockSpec((MOE_ROWS * ROW_TILES, LANES), lambda i, blk, used, nxt: (live(i, used), 0))
    by_e = lambda shape: pl.BlockSpec((1,) + shape, lambda i, blk, used, nxt: (blk[live(i, used)], 0, 0))
    in_hbm = pl.BlockSpec(memory_space=pl.ANY)
    grid_spec = pltpu.PrefetchScalarGridSpec(
        num_scalar_prefetch=3, grid=(n_blk,),
        in_specs=[row, in_hbm, by_e((1, 2 * D_FF)), in_hbm, by_e((1, D))],
        out_specs=pl.BlockSpec((MOE_ROWS * ROW_TILES, LANES), lambda i, blk, used, nxt: (i, 0)),
        scratch_shapes=[pltpu.VMEM((2, D, 2 * D_FF), F32), pltpu.VMEM((2, D_FF, D), F32),
                        pltpu.VMEM((D, 2 * D_FF), BF16), pltpu.VMEM((D_FF, D), BF16),
                        pltpu.SemaphoreType.DMA((2, 2)), pltpu.SMEM((1,), I32)])
    return pl.pallas_call(
        _expert_kernel, grid_spec=grid_spec, out_shape=jax.ShapeDtypeStruct(xs.shape, F32),
        compiler_params=_cparams("arbitrary"), name="experts",
    )(blk_expert, n_used, next_expert, xs, w_gate_up, b_gate_up.reshape(E, 1, 2 * D_FF), w_down,
      b_down.reshape(E, 1, D))


def _combine_kernel(seg_ref, segn_ref, lp_ref, w4_ref, h1_ref, g_ref, b_ref, ys_ref, o_ref, buf_ref, sem):
    i = pl.program_id(0)
    n = pl.num_programs(0)
    tm = h1_ref.shape[0]
    rows = TOP_K * tm
    slot = lax.rem(i, 2)

    def row_copy(slot_):
        def make(buf_row, ys_row, n_rows):
            return pltpu.make_async_copy(ys_ref.at[_row_span(ys_row, n_rows)],
                                         buf_ref.at[slot_, _row_span(buf_row, n_rows)], sem.at[slot_])
        return make

    @pl.when(i == 0)
    def _():
        _start_segment_copies(seg_ref, tm, row_copy(slot))

    @pl.when(i + 1 < n)
    def _():
        _start_segment_copies(segn_ref, tm, row_copy(1 - slot))

    row_copy(slot)(0, 0, rows).wait()

    lp = lp_ref[...]
    s_ix = lax.broadcasted_iota(I32, (1, rows), 1).astype(F32)
    wmat = jnp.zeros((tm, rows), F32)
    for k in range(TOP_K):
        wmat = jnp.where(s_ix == lp[:, k:k + 1], w4_ref[:, k:k + 1], wmat)
    w_hi = wmat.astype(BF16)
    w_lo = (wmat - w_hi.astype(F32)).astype(BF16)
    cols = []
    for c in range(ROW_TILES):
        y = _load_row_tile(buf_ref.at[slot], rows, c)
        y_hi = y.astype(BF16)
        y_lo = (y - y_hi.astype(F32)).astype(BF16)
        cols.append(_dot(w_hi, y_hi) + _dot(w_hi, y_lo) + _dot(w_lo, y_hi))
    acc = DEEPNORM_ALPHA * h1_ref[...] + jnp.concatenate(cols, axis=1)
    o_ref[...] = _layer_norm(acc, g_ref[...], b_ref[...])


def _combine(seg4, lp, w4, h1, ln_g, ln_b, ys):
    T, D = h1.shape
    tm = TOKEN_TILE
    n_tile = T // tm
    return pl.pallas_call(
        _combine_kernel, grid=(n_tile,),
        in_specs=[_seg_spec(lambda i: (0, i, 0, 0)),
                  _seg_spec(lambda i: (0, jnp.minimum(i + 1, n_tile - 1), 0, 0)),
                  pl.BlockSpec((tm, LANES), lambda i: (i, 0)),
                  pl.BlockSpec((tm, LANES), lambda i: (i, 0)),
                  pl.BlockSpec((tm, D), lambda i: (i, 0)),
                  _full_spec((1, D)), _full_spec((1, D)),
                  pl.BlockSpec(memory_space=pl.ANY)],
        out_specs=pl.BlockSpec((tm, D), lambda i: (i, 0)),
        out_shape=jax.ShapeDtypeStruct((T, D), F32),
        scratch_shapes=[pltpu.VMEM((2, TOP_K * tm * ROW_TILES, LANES), F32), pltpu.SemaphoreType.DMA((2,))],
        compiler_params=_cparams("arbitrary"), name="combine",
    )(seg4, seg4, lp, w4, h1, ln_g, ln_b, ys)


def _rope_tables(positions):
    inv = ROPE_THETA ** (-jnp.arange(0, ROT_DIM, 2, dtype=F32) / ROT_DIM)
    ang = positions.astype(F32)[..., None] * inv
    cos_sin = jnp.concatenate([jnp.cos(ang), jnp.sin(ang)], axis=-1)
    half = ROT_DIM // 2
    spread = np.zeros((ROT_DIM, 3 * LANES), np.float32)
    unit = np.ones((1, LANES), np.float32)
    for lane in range(LANES):
        d = lane % HEAD_DIM
        if d < half:
            spread[d, lane] = 1.0
            spread[half + d, 2 * LANES + lane] = -1.0
            unit[0, lane] = 0.0
        elif d < ROT_DIM:
            spread[d - half, lane] = 1.0
            spread[d, LANES + lane] = 1.0
            unit[0, lane] = 0.0
    return cos_sin, jnp.asarray(spread, BF16), jnp.asarray(unit)


def _split_w_in(w_in):
    widths = (NSA_WIDTH,) + (KV_WIDTH,) * 6 + (NSA_HEADS * N_BRANCH, S5_WIDTH, MEM_WIDTH, N_BRANCH * D_MODEL)
    offs = [0]
    for w in widths:
        offs.append(offs[-1] + w)
    col = lambda i: w_in[:, offs[i]:offs[i + 1]]
    wq, kc, vc, ks, vs, kw, vw, wg, wu, wqm, wm = (col(i) for i in range(11))
    wk = jnp.concatenate([kc, ks, kw], axis=1)
    wv = jnp.concatenate([vc, vs, vw], axis=1)
    per_group = NSA_HPG * N_BRANCH
    wg_pad = jnp.zeros((w_in.shape[0], NSA_GROUPS * LANES), w_in.dtype)
    for g in range(NSA_GROUPS):
        wg_pad = wg_pad.at[:, g * LANES:g * LANES + per_group].set(wg[:, g * per_group:(g + 1) * per_group])
    return tuple(w.astype(BF16) for w in (wq, wk, wv, wg_pad, wu, wqm, wm))


def _compress_weights(w1):
    half = CMP_BLOCK // 2
    eye = np.eye(NSA_GROUPS, dtype=np.float32)

    def arrange(w_half):
        full = jnp.einsum('sdf,gh->sgdhf', w_half, eye)
        return full.reshape(half * NSA_GROUPS * HEAD_DIM, NSA_GROUPS * CMP_HIDDEN).astype(BF16)

    return (w1.reshape(CMP_BLOCK * HEAD_DIM, CMP_HIDDEN).astype(BF16), arrange(w1[:half]), arrange(w1[half:]))


def _s5_weights(a_re, a_im, log_dt, b_re, b_im, c_re, c_im):
    step = jnp.exp(log_dt)[:, None]
    mag = jnp.exp(a_re * step)
    ab_re, ab_im = mag * jnp.cos(a_im * step), mag * jnp.sin(a_im * step)
    den = a_re * a_re + a_im * a_im
    nr = ab_re - 1.0
    coef_re = (nr * a_re + ab_im * a_im) / den
    coef_im = (ab_im * a_re - nr * a_im) / den
    bb_re = coef_re[..., None] * b_re - coef_im[..., None] * b_im
    bb_im = coef_re[..., None] * b_im + coef_im[..., None] * b_re
    eye = np.eye(S5_GROUPS, dtype=np.float32)
    n_state = S5_GROUPS * S5_STATE
    n_tile = n_state // LANES
    in_per = n_tile // (S5_WIDTH // LANES)

    def in_map(bb):
        return jnp.einsum('gnp,gh->gphn', bb, eye).reshape(S5_WIDTH, n_state)

    def out_map(c):
        return jnp.einsum('gpn,gh->gnhp', c, eye).reshape(n_state, S5_WIDTH)

    n_in = S5_WIDTH // LANES

    def in_blocks(bb):
        x = in_map(bb).reshape(n_in, LANES, n_in, in_per, LANES)
        return jnp.einsum('iaijb->ijab', x).reshape(n_tile, LANES, LANES)

    def out_blocks(c):
        y = out_map(c).reshape(n_in, in_per, LANES, n_in, LANES)
        return jnp.einsum('ijaib->ijab', y).reshape(n_tile, LANES, LANES)

    wb = jnp.concatenate([in_blocks(bb_re), in_blocks(bb_im)], axis=2).astype(BF16)
    wc = jnp.concatenate([out_blocks(c_re), out_blocks(-c_im)], axis=1).astype(BF16)
    return wb, wc, ab_re.reshape(n_tile, LANES), ab_im.reshape(n_tile, LANES)


def _layer(x, mem, positions, ln_emb_g, ln_emb_b, w_in, pe_k, pe_v, w_kcmp1, w_kcmp2, w_vcmp1, w_vcmp2,
           s5_a_re, s5_a_im, s5_log_dt, s5_b_re, s5_b_im, s5_c_re, s5_c_im, s5_d,
           w_s5_glu, w_mem_kv, w_nsa_out, w_mem_out, w_o, ln1_g, ln1_b, w_router, b_router,
           w_gate_up, b_gate_up, w_down, b_down, ln2_g, ln2_b):
    B, L, D = x.shape
    T = B * L
    row = lambda v: v.reshape(1, -1)

    cos_sin, spread, unit = _rope_tables(positions)
    (q_hm, kc, vc, ks, vs, kw, vw, gates, u, qm, gm) = _inproj(
        x, row(ln_emb_g), row(ln_emb_b), cos_sin, spread, unit, *_split_w_in(w_in))

    n_chunk = L // CMP_STRIDE
    chunked = lambda t: t.reshape(B, n_chunk, CMP_STRIDE * KV_WIDTH)
    pe_rows = lambda pe: jnp.broadcast_to(pe.reshape(1, -1), (SUBLANES, CMP_BLOCK * HEAD_DIM)).astype(BF16)
    wk1f, wk1a, wk1b = _compress_weights(w_kcmp1)
    wv1f, wv1a, wv1b = _compress_weights(w_vcmp1)
    ck, cv = _compress(chunked(kc), chunked(vc), pe_rows(pe_k), pe_rows(pe_v), wk1f, wv1f,
                       wk1a, wk1b, wv1a, wv1b, w_kcmp2.astype(BF16), w_vcmp2.astype(BF16))

    per_sb = SEL_BLOCK // CMP_STRIDE
    c_ix = np.arange(n_chunk)[:, None]
    n_ix = np.arange(L // SEL_BLOCK)[None, :]
    w_score = jnp.asarray((c_ix // per_sb == n_ix).astype(np.float32)
                          + ((c_ix + 1) // per_sb == n_ix).astype(np.float32), BF16)
    o_nsa = _nsa(q_hm, ck, cv, ks, vs, kw, vw, gates, w_score)

    wb, wc, a_re, a_im = _s5_weights(s5_a_re, s5_a_im, s5_log_dt, s5_b_re, s5_b_im, s5_c_re, s5_c_im)
    gy = _s5(u, wb, wc, a_re, a_im, row(s5_d))

    k_mem, v_mem = _memkv(mem, w_mem_kv.astype(BF16))
    o_mem = _memattn(qm, k_mem, v_mem)

    pad_e = LANES - N_EXPERTS
    wr = jnp.pad(w_router, ((0, 0), (0, pad_e)))
    wr_hi = wr.astype(BF16)
    wr_lo = (wr - wr_hi.astype(F32)).astype(BF16)
    br = jnp.concatenate([b_router, jnp.full((pad_e,), -jnp.inf, F32)]).reshape(1, LANES)
    tm = TOKEN_TILE
    n_tile = T // tm
    strict_lower = lambda n: jnp.asarray(np.tril(np.ones((n, n), np.float32), -1), BF16)
    triu = jnp.asarray(np.triu(np.ones((LANES, LANES), np.float32)), BF16)
    striu = jnp.asarray(np.triu(np.ones((LANES, LANES), np.float32), 1), BF16)
    flat = lambda t: t.reshape(T, t.shape[-1])
    h1, lp, w4, cnt = _merge(
        flat(x), row(ln_emb_g), row(ln_emb_b), flat(o_nsa), flat(gy), flat(o_mem), flat(gm),
        w_nsa_out.astype(BF16), w_s5_glu.astype(BF16), w_mem_out.astype(BF16), w_o.astype(BF16),
        row(ln1_g), row(ln1_b), wr_hi, wr_lo, br, strict_lower(tm), striu)

    cap = (T * TOP_K + MOE_ROWS - 1) // MOE_ROWS * MOE_ROWS + N_EXPERTS * MOE_ROWS
    n_blk = cap // MOE_ROWS
    seg, blk_owner, misc = _slots(cnt[:, 0, :], triu, striu, strict_lower(n_tile), n_blk)
    seg4 = seg.reshape(3, n_tile, 1, LANES)
    blk_expert = blk_owner[:, 0]
    n_used = misc[0, :1]

    xs = _dispatch(seg4, misc, lp, h1, cap)
    ys = _experts(blk_expert, n_used, misc[3, :N_EXPERTS], xs, w_gate_up, b_gate_up, w_down, b_down)
    out = _combine(seg4, lp, w4, h1, row(ln2_g), row(ln2_b), ys)
    return out.reshape(B, L, D)


def kernel(x, mem, positions, ln_emb_g, ln_emb_b, w_in, pe_k_cmp, pe_v_cmp, w_kcmp1, w_kcmp2, w_vcmp1, w_vcmp2, s5_a_re, s5_a_im, s5_log_dt, s5_b_re, s5_b_im, s5_c_re, s5_c_im, s5_d, w_s5_glu, w_mem_kv, w_nsa_out, w_mem_out, w_o, ln1_g, ln1_b, w_router, b_router, w_gate_up, b_gate_up, w_down, b_down, ln2_g, ln2_b):
    assert w_in.shape[0] == DEPTH
    l = 0
    return _layer(x, mem, positions, ln_emb_g, ln_emb_b, w_in[l], pe_k_cmp[l], pe_v_cmp[l], w_kcmp1[l],
                  w_kcmp2[l], w_vcmp1[l], w_vcmp2[l], s5_a_re[l], s5_a_im[l], s5_log_dt[l], s5_b_re[l],
                  s5_b_im[l], s5_c_re[l], s5_c_im[l], s5_d[l], w_s5_glu[l], w_mem_kv[l], w_nsa_out[l],
                  w_mem_out[l], w_o[l], ln1_g[l], ln1_b[l], w_router[l], b_router[l], w_gate_up[l],
                  b_gate_up[l], w_down[l], b_down[l], ln2_g[l], ln2_b[l])
```

```python
import functools
import math

import jax
import jax.numpy as jnp
import numpy as np
from jax import lax
from jax.experimental import pallas as pl
from jax.experimental.pallas import tpu as pltpu

F32 = jnp.float32
BF16 = jnp.bfloat16
I32 = jnp.int32

D_MODEL = 1024
NSA_HEADS = 8
NSA_GROUPS = 2
NSA_HPG = NSA_HEADS // NSA_GROUPS
HEAD_DIM = 64
NSA_WIDTH = NSA_HEADS * HEAD_DIM
KV_WIDTH = NSA_GROUPS * HEAD_DIM
CMP_BLOCK = 32
CMP_STRIDE = 16
CMP_HIDDEN = 128
SEL_BLOCK = 64
N_SEL = 16
WINDOW = 512
Q_BLOCK = 256
WINDOW_Q = 128
ROPE_THETA = 500000.0
ROT_DIM = HEAD_DIM // 4
S5_WIDTH = 512
S5_GROUP_DIM = 16
S5_GROUPS = S5_WIDTH // S5_GROUP_DIM
S5_STATE = 64
MEM_HEADS = 4
MEM_HEAD_DIM = 128
MEM_WIDTH = MEM_HEADS * MEM_HEAD_DIM
N_BRANCH = 3
N_EXPERTS = 32
TOP_K = 4
D_FF = 1024
SWIGLU_LIMIT = 7.0
SWIGLU_ALPHA = 1.702
LN_EPS = 1e-5
DEPTH = 1
DEEPNORM_ALPHA = (2 * DEPTH) ** 0.25

LANES = 128
SUBLANES = 8
VMEM_LIMIT_BYTES = 56 * 1024 * 1024

TOKEN_TILE = 256
MERGE_SORT_TILES = 2
MEM_TOKEN_TILE = 512
SEL_KV_TILE = 512
S5_CHUNK = 512
S5_PITCH = S5_CHUNK + 8
MOE_ROWS = 512
NEG_BIG = -(2.0 ** 100)
Q_SCALE_LOG2 = HEAD_DIM ** -0.5 * math.log2(math.e)


def _cparams(*sem):
    return pltpu.CompilerParams(dimension_semantics=sem, vmem_limit_bytes=VMEM_LIMIT_BYTES)


def _dot(a, b):
    return jnp.dot(a, b, preferred_element_type=F32)


def _dot_nt(a, b):
    return lax.dot_general(a, b, (((1,), (1,)), ((), ())), preferred_element_type=F32)


def _layer_norm(x, g, b):
    mu = jnp.mean(x, axis=-1, keepdims=True)
    xc = x - mu
    var = jnp.mean(xc * xc, axis=-1, keepdims=True)
    return xc * lax.rsqrt(var + LN_EPS) * g + b


def _gelu_tanh(x):
    cdf = 0.5 * (1.0 + jnp.tanh(math.sqrt(2.0 / math.pi) * (x + 0.044715 * (x * x * x))))
    return x * cdf


def _masked_exp2(s, mask):
    s = jnp.where(mask, s, -jnp.inf)
    m = jnp.max(s, axis=-1, keepdims=True)
    m = jnp.where(m > -jnp.inf, m, 0.0)
    return jnp.exp2(s - m)


def _safe_recip(denom):
    return 1.0 / jnp.maximum(denom, jnp.finfo(F32).tiny)


def _split3(x):
    hi = x.astype(BF16)
    r1 = x - hi.astype(F32)
    mid = r1.astype(BF16)
    lo = (r1 - mid.astype(F32)).astype(BF16)
    return hi, mid, lo


def _full_spec(shape):
    nd = len(shape)
    return pl.BlockSpec(shape, lambda *_: (0,) * nd)


def _inproj_kernel(x_ref, g_ref, b_ref, cs_ref, spread_ref, unit_ref,
                   wq_ref, wk_ref, wv_ref, wg_ref, wu_ref, wqm_ref, wm_ref,
                   q_ref, kc_ref, vc_ref, ks_ref, vs_ref, kw_ref, vw_ref,
                   gate_ref, u_ref, qm_ref, gm_ref):
    h = _layer_norm(x_ref[0], g_ref[...], b_ref[...])
    hb = h.astype(BF16)
    tab = sum(_dot(part, spread_ref[...]) for part in _split3(cs_ref[0]))
    cos_t = tab[:, 0:LANES] + unit_ref[...]
    sin_a = tab[:, LANES:2 * LANES]
    sin_b = tab[:, 2 * LANES:3 * LANES]

    def rope(t):
        return (t * cos_t + pltpu.roll(t, ROT_DIM // 2, 1) * sin_a
                + pltpu.roll(t, LANES - ROT_DIM // 2, 1) * sin_b)

    q = _dot(hb, wq_ref[...])
    for c in range(NSA_WIDTH // LANES):
        qc = rope(q[:, c * LANES:(c + 1) * LANES]) * Q_SCALE_LOG2
        for hh in range(2):
            q_ref[0, 2 * c + hh] = qc[:, hh * HEAD_DIM:(hh + 1) * HEAD_DIM].astype(BF16)
    k3 = _dot(hb, wk_ref[...])
    kc = rope(k3[:, 0:LANES])
    ks = rope(k3[:, LANES:2 * LANES])
    kw = rope(k3[:, 2 * LANES:3 * LANES])
    v3 = _dot(hb, wv_ref[...])
    kc_ref[0] = kc.astype(BF16)
    vc_ref[0] = v3[:, 0:LANES].astype(BF16)
    tm = x_ref.shape[1]
    pos = pl.program_id(1) * tm + lax.broadcasted_iota(I32, (tm, LANES), 0)
    blk_hot = jnp.where(lax.broadcasted_iota(I32, (tm, LANES), 1) == pos // SEL_BLOCK, 1.0, 0.0)
    lane_pad = jnp.zeros((tm, LANES - HEAD_DIM), F32)
    ones_pad = jnp.where(lax.broadcasted_iota(I32, (tm, LANES - HEAD_DIM), 1) == 0, 1.0, 0.0)
    for g in range(NSA_GROUPS):
        sl = slice(g * HEAD_DIM, (g + 1) * HEAD_DIM)
        ks_ref[0, g] = jnp.concatenate([blk_hot, ks[:, sl], lane_pad], axis=1).astype(BF16)
        kw_ref[0, g] = kw[:, sl].astype(BF16)
        vs_ref[0, g] = jnp.concatenate([v3[:, LANES:2 * LANES][:, sl], ones_pad], axis=1).astype(BF16)
        vw_ref[0, g] = jnp.concatenate([v3[:, 2 * LANES:3 * LANES][:, sl], ones_pad], axis=1).astype(BF16)
    gate_ref[0] = jax.nn.sigmoid(_dot(hb, wg_ref[...]))
    u_ref[0] = _dot(hb, wu_ref[...])
    qm_ref[0] = _dot(hb, wqm_ref[...]).astype(BF16)
    gm_ref[0] = jax.nn.sigmoid(_dot(hb, wm_ref[...]))


def _inproj(x, ln_g, ln_b, cos_sin, spread, unit, wq, wk, wv, wg, wu, wqm, wm):
    B, L, D = x.shape
    tm = TOKEN_TILE
    grid = (B, L // tm)
    tok = lambda w: pl.BlockSpec((1, tm, w), lambda b, i: (b, i, 0))
    head = lambda n, w=HEAD_DIM: pl.BlockSpec((1, n, tm, w), lambda b, i: (b, 0, i, 0))
    in_specs = [tok(D), _full_spec((1, D)), _full_spec((1, D)), tok(ROT_DIM), _full_spec(spread.shape),
                _full_spec(unit.shape)]
    in_specs += [_full_spec(w.shape) for w in (wq, wk, wv, wg, wu, wqm, wm)]
    sd = jax.ShapeDtypeStruct
    out_shape = [
        sd((B, NSA_HEADS, L, HEAD_DIM), BF16),
        sd((B, L, KV_WIDTH), BF16), sd((B, L, KV_WIDTH), BF16),
        sd((B, NSA_GROUPS, L, 2 * LANES), BF16), sd((B, NSA_GROUPS, L, LANES), BF16),
        sd((B, NSA_GROUPS, L, HEAD_DIM), BF16), sd((B, NSA_GROUPS, L, LANES), BF16),
        sd((B, L, NSA_GROUPS * LANES), F32),
        sd((B, L, S5_WIDTH), F32),
        sd((B, L, MEM_WIDTH), BF16),
        sd((B, L, N_BRANCH * D), F32),
    ]
    out_specs = [head(NSA_HEADS), tok(KV_WIDTH), tok(KV_WIDTH), head(NSA_GROUPS, 2 * LANES),
                 head(NSA_GROUPS, LANES), head(NSA_GROUPS), head(NSA_GROUPS, LANES),
                 tok(NSA_GROUPS * LANES), tok(S5_WIDTH),
                 tok(MEM_WIDTH), tok(N_BRANCH * D)]
    return pl.pallas_call(
        _inproj_kernel, grid=grid, in_specs=in_specs, out_specs=out_specs, out_shape=out_shape,
        compiler_params=_cparams("parallel", "parallel"), name="inproj",
    )(x, ln_g, ln_b, cos_sin, spread, unit, wq, wk, wv, wg, wu, wqm, wm)


def _compress_kernel(kc_ref, vc_ref, pek_ref, pev_ref, wk1f_ref, wv1f_ref,
                     wk1a_ref, wk1b_ref, wv1a_ref, wv1b_ref, wk2_ref, wv2_ref, ck_ref, cv_ref):
    n_chunk = kc_ref.shape[1]
    row = lax.broadcasted_iota(I32, (n_chunk, 1), 0)

    def one(x_ref, pe_ref, w1f_ref, w1a_ref, w1b_ref, w2_ref, o_ref):
        x = x_ref[0]
        first = _dot(x, w1a_ref[...])
        second = _dot(x, w1b_ref[...])
        second = pltpu.roll(second, n_chunk - 1, 0)
        pe_term = _dot(pe_ref[...], w1f_ref[...])[0:1]
        pe_term = jnp.concatenate([pe_term] * NSA_GROUPS, axis=1)
        hid = _gelu_tanh(first + second + pe_term).astype(BF16)
        for g in range(NSA_GROUPS):
            o = _dot(hid[:, g * CMP_HIDDEN:(g + 1) * CMP_HIDDEN], w2_ref[...])
            o_ref[0, g] = jnp.where(row < n_chunk - 1, o, 0.0).astype(BF16)

    one(kc_ref, pek_ref, wk1f_ref, wk1a_ref, wk1b_ref, wk2_ref, ck_ref)
    one(vc_ref, pev_ref, wv1f_ref, wv1a_ref, wv1b_ref, wv2_ref, cv_ref)


def _compress(kc_r, vc_r, pek, pev, wk1f, wv1f, wk1a, wk1b, wv1a, wv1b, wk2, wv2):
    B, n_chunk, width = kc_r.shape
    blk = pl.BlockSpec((1, n_chunk, width), lambda b: (b, 0, 0))
    out = pl.BlockSpec((1, NSA_GROUPS, n_chunk, HEAD_DIM), lambda b: (b, 0, 0, 0))
    ws = [pek, pev, wk1f, wv1f, wk1a, wk1b, wv1a, wv1b, wk2, wv2]
    sd = jax.ShapeDtypeStruct((B, NSA_GROUPS, n_chunk, HEAD_DIM), BF16)
    return pl.pallas_call(
        _compress_kernel, grid=(B,), in_specs=[blk, blk] + [_full_spec(w.shape) for w in ws],
        out_specs=[out, out], out_shape=[sd, sd], compiler_params=_cparams("parallel"), name="compress",
    )(kc_r, vc_r, *ws)


def _nsa_kernel(q_ref, ck_ref, cv_ref, ks_ref, vs_ref, kw_ref, vw_ref, gate_ref, wsc_ref, o_ref):
    seq_len = ks_ref.shape[2]
    n_cmp = ck_ref.shape[2]
    n_sb = seq_len // SEL_BLOCK
    n_sel = min(N_SEL, n_sb)
    rows = NSA_HPG * Q_BLOCK
    groups = range(NSA_GROUPS)
    q0 = pl.program_id(1) * Q_BLOCK
    t1 = q0 + lax.broadcasted_iota(I32, (Q_BLOCK, 1), 0)
    t4 = jnp.concatenate([t1] * NSA_HPG, axis=0)
    tk = SEL_KV_TILE

    def front(g):
        q = q_ref[0, g * NSA_HPG:(g + 1) * NSA_HPG].reshape(rows, HEAD_DIM)

        s = _dot_nt(q, ck_ref[0, g])
        c_end = lax.broadcasted_iota(I32, (1, n_cmp), 1) * CMP_STRIDE + (CMP_BLOCK - 1)
        e = _masked_exp2(s, c_end <= t4)
        p_cmp = e * _safe_recip(jnp.sum(e, axis=-1, keepdims=True))
        o_cmp = _dot(p_cmp.astype(BF16), cv_ref[0, g])

        imp = p_cmp[0:Q_BLOCK]
        for hh in range(1, NSA_HPG):
            imp = imp + p_cmp[hh * Q_BLOCK:(hh + 1) * Q_BLOCK]
        w_sc = wsc_ref[...]
        score = sum(_dot(part, w_sc) for part in _split3(imp))
        score_t = score.T
        jb = lax.broadcasted_iota(I32, (n_sb, Q_BLOCK), 0)
        tb = (q0 + lax.broadcasted_iota(I32, (1, Q_BLOCK), 1)) // SEL_BLOCK
        forced = (jb == 0) | (jb == tb) | (jb == tb - 1)
        work = jnp.where(forced | (jb > tb), -jnp.inf, score_t)
        bias_t = jnp.where(forced, 0.0, NEG_BIG)
        jbf = jb.astype(F32)
        for _ in range(n_sel - 3):
            m = jnp.max(work, axis=0, keepdims=True)
            idx = jnp.min(jnp.where(work == m, jbf, float(n_sb)), axis=0, keepdims=True)
            pick = jbf == idx
            bias_t = jnp.where(pick, 0.0, bias_t)
            work = jnp.where(pick, -jnp.inf, work)
        sel_bias = bias_t.T
        if n_sb < LANES:
            sel_bias = jnp.concatenate([sel_bias, jnp.zeros((Q_BLOCK, LANES - n_sb), F32)], axis=1)

        span = WINDOW + WINDOW_Q
        parts = []
        for sub in range(Q_BLOCK // WINDOW_Q):
            pick = lambda a: jnp.concatenate(
                [a[hh * Q_BLOCK + sub * WINDOW_Q:hh * Q_BLOCK + (sub + 1) * WINDOW_Q] for hh in range(NSA_HPG)],
                axis=0)
            w0 = pl.multiple_of(jnp.maximum(q0 + sub * WINDOW_Q - WINDOW, 0), WINDOW_Q)
            s = _dot_nt(pick(q), kw_ref[0, g, pl.ds(w0, span), :])
            diff = pick(t4) - (w0 + lax.broadcasted_iota(I32, (1, span), 1))
            e = _masked_exp2(s, (diff >= 0) & (diff < WINDOW))
            o = _dot(e.astype(BF16), vw_ref[0, g, pl.ds(w0, span), :])
            parts.append(o[:, :HEAD_DIM] * _safe_recip(o[:, HEAD_DIM:HEAD_DIM + 1]))
        o_win = jnp.concatenate([parts[sub][hh * WINDOW_Q:(hh + 1) * WINDOW_Q]
                                 for hh in range(NSA_HPG) for sub in range(len(parts))], axis=0)

        q_aug = jnp.concatenate([jnp.concatenate([sel_bias.astype(BF16)] * NSA_HPG, axis=0), q,
                                 jnp.zeros((rows, LANES - HEAD_DIM), BF16)], axis=1)
        return q_aug, o_cmp, o_win

    fronts = [front(g) for g in groups]

    def sel_tile(g, j, carry, causal):
        m_run, acc = carry
        k0 = pl.multiple_of(j * tk, tk)
        sc = _dot_nt(fronts[g][0], ks_ref[0, g, pl.ds(k0, tk), :])
        if causal:
            kpos = k0 + lax.broadcasted_iota(I32, (1, tk), 1)
            sc = jnp.where(kpos <= t4, sc, NEG_BIG)
        m_new = jnp.maximum(m_run, jnp.max(sc, axis=-1, keepdims=True))
        p = jnp.exp2(sc - m_new)
        acc_new = jnp.exp2(m_run - m_new) * acc + _dot(p.astype(BF16), vs_ref[0, g, pl.ds(k0, tk), :])
        return m_new, acc_new

    def sel_pair(jj, carries, causal):
        return tuple(sel_tile(g, 2 * jj + 1, sel_tile(g, 2 * jj, carries[g], causal), causal) for g in groups)

    init = tuple((jnp.full((rows, 1), NEG_BIG, F32), jnp.zeros((rows, LANES), F32)) for _ in groups)
    last_pair = (q0 // tk) // 2
    carries = lax.fori_loop(0, last_pair, functools.partial(sel_pair, causal=False), init)
    carries = sel_pair(last_pair, carries, True)

    outs = []
    for g in groups:
        _, o_cmp, o_win = fronts[g]
        acc = carries[g][1]
        o_sel = acc[:, :HEAD_DIM] * (1.0 / acc[:, HEAD_DIM:HEAD_DIM + 1])
        gt = gate_ref[0, :, g * LANES:(g + 1) * LANES]
        for hh in range(NSA_HPG):
            sl = slice(hh * Q_BLOCK, (hh + 1) * Q_BLOCK)
            c = hh * N_BRANCH
            outs.append(o_cmp[sl] * gt[:, c:c + 1] + o_sel[sl] * gt[:, c + 1:c + 2]
                        + o_win[sl] * gt[:, c + 2:c + 3])
    o_ref[0] = jnp.concatenate(outs, axis=1).astype(BF16)


def _nsa(q_hm, ck, cv, ks, vs, kw, vw, gates, w_score):
    B, _, L, _ = q_hm.shape
    assert L // SEL_BLOCK <= LANES and (L // SEL_KV_TILE) % 2 == 0 and L >= WINDOW + Q_BLOCK
    n_cmp = ck.shape[2]
    grid = (B, L // Q_BLOCK)
    qspec = pl.BlockSpec((1, NSA_HEADS, Q_BLOCK, HEAD_DIM), lambda b, i: (b, 0, i, 0))
    cspec = pl.BlockSpec((1, NSA_GROUPS, n_cmp, HEAD_DIM), lambda b, i: (b, 0, 0, 0))
    kvspec = lambda w: pl.BlockSpec((1, NSA_GROUPS, L, w), lambda b, i: (b, 0, 0, 0),
                                    pipeline_mode=pl.Buffered(1))
    gspec = pl.BlockSpec((1, Q_BLOCK, NSA_GROUPS * LANES), lambda b, i: (b, i, 0))
    ospec = pl.BlockSpec((1, Q_BLOCK, NSA_WIDTH), lambda b, i: (b, i, 0))
    return pl.pallas_call(
        _nsa_kernel, grid=grid,
        in_specs=[qspec, cspec, cspec, kvspec(2 * LANES), kvspec(LANES), kvspec(HEAD_DIM), kvspec(LANES), gspec,
                  _full_spec(w_score.shape)],
        out_specs=ospec, out_shape=jax.ShapeDtypeStruct((B, L, NSA_WIDTH), BF16),
        compiler_params=_cparams("parallel", "arbitrary"), name="nsa",
    )(q_hm, ck, cv, ks, vs, kw, vw, gates, w_score)


def _s5_kernel(u_ref, wb_ref, wc_ref, are_ref, aim_ref, d_ref, y_ref, sre_ref, sim_ref, carry_ref):
    n_b, chunk, _ = u_ref.shape
    n_tile = wb_ref.shape[0]
    in_per = n_tile // (S5_WIDTH // LANES)
    pitch = S5_PITCH

    @pl.when(pl.program_id(0) == 0)
    def _():
        carry_ref[...] = jnp.zeros_like(carry_ref)

    for b in range(n_b):
        for c in range(n_tile):
            i = c // in_per
            ub = u_ref[b, :, i * LANES:(i + 1) * LANES].astype(BF16)
            r = _dot(ub, wb_ref[c])
            sre_ref[b, c * pitch:c * pitch + chunk, :] = r[:, :LANES]
            sim_ref[b, c * pitch:c * pitch + chunk, :] = r[:, LANES:]

    a_re, a_im = are_ref[...], aim_ref[...]

    def step(t, carry):
        out = []
        for b in range(n_b):
            s_re, s_im = carry[2 * b], carry[2 * b + 1]
            rows = pl.ds(t, n_tile, stride=pitch)
            n_re = a_re * s_re - a_im * s_im + sre_ref[b, rows, :]
            n_im = a_re * s_im + a_im * s_re + sim_ref[b, rows, :]
            sre_ref[b, rows, :] = n_re
            sim_ref[b, rows, :] = n_im
            out += [n_re, n_im]
        return tuple(out)

    init = tuple(carry_ref[i] for i in range(2 * n_b))
    fin = lax.fori_loop(0, chunk, step, init, unroll=8)
    for i in range(2 * n_b):
        carry_ref[i] = fin[i]

    for b in range(n_b):
        for o in range(S5_WIDTH // LANES):
            acc = jnp.zeros((chunk, LANES), F32)
            for c in range(o * in_per, (o + 1) * in_per):
                rows = slice(c * pitch, c * pitch + chunk)
                state = jnp.concatenate([sre_ref[b, rows, :], sim_ref[b, rows, :]], axis=1).astype(BF16)
                acc = acc + _dot(state, wc_ref[c])
            lanes = slice(o * LANES, (o + 1) * LANES)
            y = acc + d_ref[:, lanes] * u_ref[b, :, lanes]
            y_ref[b, :, lanes] = _gelu_tanh(y).astype(BF16)


def _s5(u, wb, wc, a_re, a_im, d_skip):
    B, L, W = u.shape
    chunk = S5_CHUNK
    n_tile = wb.shape[0]
    blk = pl.BlockSpec((B, chunk, W), lambda i: (0, i, 0))
    slab = pltpu.VMEM((B, n_tile * S5_PITCH, LANES), F32)
    return pl.pallas_call(
        _s5_kernel, grid=(L // chunk,),
        in_specs=[blk] + [_full_spec(w.shape) for w in (wb, wc, a_re, a_im, d_skip)],
        out_specs=blk, out_shape=jax.ShapeDtypeStruct((B, L, W), BF16),
        scratch_shapes=[slab, slab, pltpu.VMEM((2 * B, n_tile, LANES), F32)],
        compiler_params=_cparams("arbitrary"), name="s5",
    )(u, wb, wc, a_re, a_im, d_skip)


def _memkv_kernel(mem_ref, w_ref, k_ref, v_ref):
    kv = _dot(mem_ref[0].astype(BF16), w_ref[...])
    k_ref[0] = kv[:, :MEM_WIDTH].astype(BF16)
    v_ref[0] = kv[:, MEM_WIDTH:].astype(BF16)


def _memkv(mem, w_kv):
    B, M, D = mem.shape
    out = pl.BlockSpec((1, M, MEM_WIDTH), lambda b: (b, 0, 0))
    sd = jax.ShapeDtypeStruct((B, M, MEM_WIDTH), BF16)
    return pl.pallas_call(
        _memkv_kernel, grid=(B,),
        in_specs=[pl.BlockSpec((1, M, D), lambda b: (b, 0, 0)), _full_spec(w_kv.shape)],
        out_specs=[out, out], out_shape=[sd, sd], compiler_params=_cparams("parallel"), name="memkv",
    )(mem, w_kv)


def _memattn_kernel(q_ref, k_ref, v_ref, o_ref):
    outs = []
    for h in range(MEM_HEADS):
        sl = slice(h * MEM_HEAD_DIM, (h + 1) * MEM_HEAD_DIM)
        s = _dot_nt(q_ref[0, :, sl], k_ref[0, :, sl]) * (MEM_HEAD_DIM ** -0.5)
        m = jnp.max(s, axis=-1, keepdims=True)
        e = jnp.exp(s - m)
        p = e / jnp.sum(e, axis=-1, keepdims=True)
        outs.append(_dot(p.astype(BF16), v_ref[0, :, sl]))
    o_ref[0] = jnp.concatenate(outs, axis=1).astype(BF16)


def _memattn(qm, k, v):
    B, L, W = qm.shape
    M = k.shape[1]
    tm = MEM_TOKEN_TILE
    tok = pl.BlockSpec((1, tm, W), lambda b, i: (b, i, 0))
    kv = pl.BlockSpec((1, M, W), lambda b, i: (b, 0, 0))
    return pl.pallas_call(
        _memattn_kernel, grid=(B, L // tm), in_specs=[tok, kv, kv], out_specs=tok,
        out_shape=jax.ShapeDtypeStruct((B, L, W), BF16),
        compiler_params=_cparams("parallel", "parallel"), name="memattn",
    )(qm, k, v)


def _merge_kernel(x_ref, lng_ref, lnb_ref, on_ref, gy_ref, om_ref, gm_ref,
                  wn_ref, wglu_ref, wmo_ref, wo_ref, l1g_ref, l1b_ref,
                  wrh_ref, wrl_ref, br_ref, tri_ref, striu_ref,
                  h1_ref, lp_ref, w4_ref, cnt_ref):
    D = x_ref.shape[1]
    tm = x_ref.shape[0]
    h =_layer_norm(x_ref[...], lng_ref[...], lnb_ref[...])
    y_nsa = _dot(on_ref[...], wn_ref[...])
    glu = _dot(gy_ref[...], wglu_ref[...])
    y_s5 = glu[:, :D] * jax.nn.sigmoid(glu[:, D:])
    y_mem = _dot(om_ref[...], wmo_ref[...])
    merged = gm_ref[:, 0:D] * y_nsa + gm_ref[:, D:2 * D] * y_s5 + gm_ref[:, 2 * D:3 * D] * y_mem
    mix = _dot(merged.astype(BF16), wo_ref[...])
    h1 = _layer_norm(DEEPNORM_ALPHA * h + mix, l1g_ref[...], l1b_ref[...])
    h1_ref[...] = h1

    hh = h1.astype(BF16)
    hl = (h1 - hh.astype(F32)).astype(BF16)
    logits = _dot(hh, wrh_ref[...]) + _dot(hh, wrl_ref[...]) + _dot(hl, wrh_ref[...]) + br_ref[...]
    lane = lax.broadcasted_iota(I32, (tm, LANES), 1)
    lane_f = lane.astype(F32)
    work = logits
    multi = jnp.zeros((tm, LANES), F32)
    vals, picks = [], []
    for _ in range(TOP_K):
        m = jnp.max(work, axis=-1, keepdims=True)
        idx = jnp.min(jnp.where(work == m, lane_f, float(LANES)), axis=-1, keepdims=True)
        pick = lane_f == idx
        vals.append(m)
        picks.append((pick, idx))
        multi = jnp.where(pick, 1.0, multi)
        work = jnp.where(pick, -jnp.inf, work)
    es = [jnp.exp(v - vals[0]) for v in vals]
    den = es[0] + es[1] + es[2] + es[3]
    st = TOKEN_TILE
    pos = []
    for t in range(tm // st):
        multi_t = multi[t * st:(t + 1) * st]
        cnt = jnp.broadcast_to(jnp.sum(multi_t, axis=0, keepdims=True), (SUBLANES, LANES))
        cnt_ref[t] = cnt
        lower = _dot(cnt.astype(BF16), striu_ref[...])[0:1]
        pos.append(lower + _dot(tri_ref[...], multi_t.astype(BF16)))
    pos = jnp.concatenate(pos, axis=0)
    lp = jnp.full((tm, LANES), -1.0, F32)
    w4 = jnp.zeros((tm, LANES), F32)
    for k in range(TOP_K):
        pick, _ = picks[k]
        lp = jnp.where(lane == k, jnp.sum(jnp.where(pick, pos, 0.0), axis=-1, keepdims=True), lp)
        w4 = jnp.where(lane == k, es[k] / den, w4)
    lp_ref[...] = lp
    w4_ref[...] = w4


def _merge(x2, lng, lnb, o_nsa, gy, om, gm, wn, wglu, wmo, wo, l1g, l1b, wrh, wrl, br, tri, striu):
    T, D = x2.shape
    tm = MERGE_SORT_TILES * TOKEN_TILE
    tok = lambda w: pl.BlockSpec((tm, w), lambda i: (i, 0))
    ws = [wn, wglu, wmo, wo, l1g, l1b, wrh, wrl, br, tri, striu]
    sd = jax.ShapeDtypeStruct
    lane_out = sd((T, LANES), F32)
    return pl.pallas_call(
        _merge_kernel, grid=(T // tm,),
        in_specs=[tok(D), _full_spec((1, D)), _full_spec((1, D)), tok(NSA_WIDTH), tok(S5_WIDTH),
                  tok(MEM_WIDTH), tok(N_BRANCH * D)] + [_full_spec(w.shape) for w in ws],
        out_specs=[tok(D), tok(LANES), tok(LANES),
                   pl.BlockSpec((MERGE_SORT_TILES, SUBLANES, LANES), lambda i: (i, 0, 0))],
        out_shape=[sd((T, D), F32), lane_out, lane_out, sd((T // TOKEN_TILE, SUBLANES, LANES), F32)],
        compiler_params=_cparams("parallel"), name="merge",
    )(x2, lng, lnb, o_nsa, gy, om, gm, *ws)


def _slots_kernel(cnt_ref, triu_ref, striu_ref, tril_ref, seg_ref, blk_ref, misc_ref):
    n_blk = blk_ref.shape[0]
    cnt = cnt_ref[...]
    cnt_b = cnt.astype(BF16)
    total = jnp.sum(cnt, axis=0, keepdims=True)
    nblk_e = jnp.floor((total + (MOE_ROWS - 1)) * (1.0 / MOE_ROWS))
    nblk_8 = jnp.broadcast_to(nblk_e, (SUBLANES, LANES))
    end_b = _dot(nblk_8.astype(BF16), triu_ref[...])
    start_rows = (end_b - nblk_8)[0:1] * MOE_ROWS
    dst = start_rows + _dot(tril_ref[...], cnt_b)
    off = _dot(cnt_b, striu_ref[...])
    seg_ref[0] = cnt.astype(I32)
    seg_ref[1] = off.astype(I32)
    seg_ref[2] = dst.astype(I32)
    blk_i = lax.broadcasted_iota(I32, (n_blk, LANES), 0).astype(F32)
    lane_b = lax.broadcasted_iota(I32, (n_blk, LANES), 1)
    ended = jnp.where((end_b[0:1] <= blk_i) & (lane_b < N_EXPERTS), 1.0, 0.0)
    owner = jnp.minimum(jnp.sum(ended, axis=-1, keepdims=True), float(N_EXPERTS - 1))
    blk_ref[...] = jnp.broadcast_to(owner, (n_blk, LANES)).astype(I32)
    cand = lax.broadcasted_iota(I32, (LANES, LANES), 0)
    has_blocks = jnp.broadcast_to(nblk_e, (LANES, LANES)).T > 0.0
    later = (cand > lax.broadcasted_iota(I32, (LANES, LANES), 1)) & has_blocks
    nxt = jnp.min(jnp.where(later, cand.astype(F32), float(LANES)), axis=0, keepdims=True)
    nxt = jnp.where(nxt < float(LANES), nxt, -1.0)
    lane8 = lax.broadcasted_iota(I32, (SUBLANES, LANES), 1)
    row8 = lax.broadcasted_iota(I32, (SUBLANES, LANES), 0)
    used = jnp.sum(jnp.where(lane8 == N_EXPERTS - 1, end_b, 0.0), axis=-1, keepdims=True)
    misc = jnp.where(row8 == 0, used,
                     jnp.where(row8 == 1, start_rows + total,
                               jnp.where(row8 == 2, nblk_e * MOE_ROWS - total, nxt)))
    misc_ref[...] = misc.astype(I32)


def _slots(cnt, triu, striu, tril, n_blk):
    n_tile = cnt.shape[0]
    sd = jax.ShapeDtypeStruct
    return pl.pallas_call(
        _slots_kernel, grid=(1,),
        in_specs=[_full_spec(cnt.shape), _full_spec(triu.shape), _full_spec(striu.shape), _full_spec(tril.shape)],
        out_specs=[_full_spec((3, n_tile, LANES)), _full_spec((n_blk, LANES)), _full_spec((SUBLANES, LANES))],
        out_shape=[sd((3, n_tile, LANES), I32), sd((n_blk, LANES), I32), sd((SUBLANES, LANES), I32)],
        compiler_params=_cparams("arbitrary"), name="slots",
    )(cnt, triu, striu, tril)


ROW_TILES = D_MODEL // LANES


def _row_span(row, n_rows):
    start = row * ROW_TILES
    if not isinstance(start, int):
        start = pl.multiple_of(start, ROW_TILES)
    return pl.ds(start, n_rows * ROW_TILES)


def _store_rows(ref, val):
    for c in range(ROW_TILES):
        ref[pl.ds(c, val.shape[0], stride=ROW_TILES), :] = val[:, c * LANES:(c + 1) * LANES]


def _load_row_tile(ref, n_rows, c):
    return ref[pl.ds(c, n_rows, stride=ROW_TILES), :]


BIG_PIECE_ROWS = 64


def _pieces(count, max_rows, fn):
    def run(sizes):
        for p in sizes:
            def piece(p=p):
                fn(count & (-2 * p), p)
            pl.when((count & p) != 0)(piece)

    sizes = [max_rows >> s for s in range(max_rows.bit_length())]
    big = [p for p in sizes if p >= BIG_PIECE_ROWS]
    if big:
        pl.when(count >= BIG_PIECE_ROWS)(lambda: run(big))
    run([p for p in sizes if p < BIG_PIECE_ROWS])


def _start_segment_copies(seg_ref, max_rows, make_copy):
    def per_expert(e, c):
        cnt, off, dst = seg_ref[0, 0, 0, e], seg_ref[1, 0, 0, e], seg_ref[2, 0, 0, e]
        _pieces(cnt, max_rows, lambda first, rows: make_copy(off + first, dst + first, rows).start())
        return c

    lax.fori_loop(0, N_EXPERTS, per_expert, 0)


def _dispatch_kernel(seg_ref, misc_ref, lp_ref, h_ref, xs_ref, sorted_ref, zero_ref, sem, pad_sem):
    i = pl.program_id(0)
    n = pl.num_programs(0)
    tm, D = h_ref.shape
    rows = TOP_K * tm
    slot = lax.rem(i, 2)

    def row_copy(slot_):
        def make(src_row, dst_row, n_rows):
            return pltpu.make_async_copy(sorted_ref.at[slot_, _row_span(src_row, n_rows)],
                                         xs_ref.at[_row_span(dst_row, n_rows)], sem.at[slot_])
        return make

    @pl.when(i == 0)
    def _():
        zero_ref[...] = jnp.zeros_like(zero_ref)
        for wait in (False, True):
            def per_expert(e, c, wait=wait):
                def one(first, n_rows):
                    cp = pltpu.make_async_copy(zero_ref.at[_row_span(0, n_rows)],
                                               xs_ref.at[_row_span(misc_ref[1, e] + first, n_rows)], pad_sem)
                    cp.wait() if wait else cp.start()
                _pieces(misc_ref[2, e], MOE_ROWS // 2, one)
                return c
            lax.fori_loop(0, N_EXPERTS, per_expert, 0)

            def per_spare_half_block(hb, c, wait=wait):
                cp = pltpu.make_async_copy(zero_ref, xs_ref.at[_row_span(hb * (MOE_ROWS // 2), MOE_ROWS // 2)],
                                           pad_sem)
                cp.wait() if wait else cp.start()
                return c
            lax.fori_loop(2 * misc_ref[0, 0], 2 * (xs_ref.shape[0] // (MOE_ROWS * ROW_TILES)),
                          per_spare_half_block, 0)

    lp_t = lp_ref[...].T
    s_ix = lax.broadcasted_iota(I32, (rows, 1), 0).astype(F32)
    hit = s_ix == lp_t[0:1, :]
    for k in range(1, TOP_K):
        hit = hit | (s_ix == lp_t[k:k + 1, :])
    perm = jnp.where(hit, 1.0, 0.0).astype(BF16)
    _store_rows(sorted_ref.at[slot], _dot(perm, h_ref[...].astype(BF16)))

    _start_segment_copies(seg_ref, tm, row_copy(slot))

    @pl.when(i > 0)
    def _():
        row_copy(1 - slot)(0, 0, rows).wait()

    @pl.when(i == n - 1)
    def _():
        row_copy(slot)(0, 0, rows).wait()


def _seg_spec(index_map):
    return pl.BlockSpec((3, 1, 1, LANES), index_map, memory_space=pltpu.SMEM)


def _dispatch(seg4, misc, lp, h1, cap):
    T, D = h1.shape
    assert D == ROW_TILES * LANES
    tm = TOKEN_TILE
    tok = lambda w: pl.BlockSpec((tm, w), lambda i: (i, 0))
    return pl.pallas_call(
        _dispatch_kernel, grid=(T // tm,),
        in_specs=[_seg_spec(lambda i: (0, i, 0, 0)), pl.BlockSpec(memory_space=pltpu.SMEM), tok(LANES), tok(D)],
        out_specs=pl.BlockSpec(memory_space=pl.ANY),
        out_shape=jax.ShapeDtypeStruct((cap * ROW_TILES, LANES), F32),
        scratch_shapes=[pltpu.VMEM((2, TOP_K * tm * ROW_TILES, LANES), F32),
                        pltpu.VMEM((MOE_ROWS // 2 * ROW_TILES, LANES), F32),
                        pltpu.SemaphoreType.DMA((2,)), pltpu.SemaphoreType.DMA(())],
        compiler_params=_cparams("arbitrary"), name="dispatch",
    )(seg4, misc, lp, h1)


def _expert_kernel(blk_ref, used_ref, next_ref, xs_ref, wgu_hbm, bgu_ref, wd_hbm, bd_ref, ys_ref,
                   wgu_f32, wd_f32, wgu_bf, wd_bf, sem, run_ref):
    i = pl.program_id(0)
    live = i < used_ref[0]
    expert = blk_ref[i]

    def weight_copies(e, slot):
        return (pltpu.make_async_copy(wgu_hbm.at[e], wgu_f32.at[slot], sem.at[0, slot]),
                pltpu.make_async_copy(wd_hbm.at[e], wd_f32.at[slot], sem.at[1, slot]))

    @pl.when(i == 0)
    def _():
        run_ref[0] = 0
        for cp in weight_copies(expert, 0):
            cp.start()

    @pl.when(live & ((i == 0) | (expert != blk_ref[jnp.maximum(i - 1, 0)])))
    def _():
        slot = lax.rem(run_ref[0], 2)
        for cp in weight_copies(expert, slot):
            cp.wait()
        wgu_bf[...] = wgu_f32[slot].astype(BF16)
        wd_bf[...] = wd_f32[slot].astype(BF16)
        nxt = next_ref[expert]

        @pl.when(nxt >= 0)
        def _():
            for cp in weight_copies(nxt, 1 - slot):
                cp.start()

        run_ref[0] = run_ref[0] + 1

    @pl.when(live)
    def _():
        xb = jnp.concatenate([_load_row_tile(xs_ref, MOE_ROWS, c).astype(BF16) for c in range(ROW_TILES)],
                             axis=1)
        gu = _dot(xb, wgu_bf[...]) + bgu_ref[0]
        g = jnp.minimum(gu[:, :D_FF], SWIGLU_LIMIT)
        lin = jnp.clip(gu[:, D_FF:], -SWIGLU_LIMIT, SWIGLU_LIMIT)
        act = g * jax.nn.sigmoid(SWIGLU_ALPHA * g) * (lin + 1.0)
        _store_rows(ys_ref, _dot(act.astype(BF16), wd_bf[...]) + bd_ref[0])

    @pl.when(pl.program_id(0) >= used_ref[0])
    def _():
        ys_ref[...] = jnp.zeros_like(ys_ref)


def _experts(blk_expert, n_used, next_expert, xs, w_gate_up, b_gate_up, w_down, b_down):
    D = w_down.shape[2]
    n_blk = xs.shape[0] // (MOE_ROWS * ROW_TILES)
    E = w_gate_up.shape[0]
    live = lambda i, used: jnp.minimum(i, used[0] - 1)
    row = pl.BlockSpec((MOE_ROWS * ROW_TILES, LANES), lambda i, blk, used, nxt: (live(i, used), 0))
    by_e = lambda shape: pl.BlockSpec((1,) + shape, lambda i, blk, used, nxt: (blk[live(i, used)], 0, 0))
    in_hbm = pl.BlockSpec(memory_space=pl.ANY)
    grid_spec = pltpu.PrefetchScalarGridSpec(
        num_scalar_prefetch=3, grid=(n_blk,),
        in_specs=[row, in_hbm, by_e((1, 2 * D_FF)), in_hbm, by_e((1, D))],
        out_specs=pl.BlockSpec((MOE_ROWS * ROW_TILES, LANES), lambda i, blk, used, nxt: (i, 0)),
        scratch_shapes=[pltpu.VMEM((2, D, 2 * D_FF), F32), pltpu.VMEM((2, D_FF, D), F32),
                        pltpu.VMEM((D, 2 * D_FF), BF16), pltpu.VMEM((D_FF, D), BF16),
                        pltpu.SemaphoreType.DMA((2, 2)), pltpu.SMEM((1,), I32)])
    return pl.pallas_call(
        _expert_kernel, grid_spec=grid_spec, out_shape=jax.ShapeDtypeStruct(xs.shape, F32),
        compiler_params=_cparams("arbitrary"), name="experts",
    )(blk_expert, n_used, next_expert, xs, w_gate_up, b_gate_up.reshape(E, 1, 2 * D_FF), w_down,
      b_down.reshape(E, 1, D))


def _combine_kernel(seg_ref, segn_ref, lp_ref, w4_ref, h1_ref, g_ref, b_ref, ys_ref, o_ref, buf_ref, sem):
    i = pl.program_id(0)
    n = pl.num_programs(0)
    tm = h1_ref.shape[0]
    rows = TOP_K * tm
    slot = lax.rem(i, 2)

    def row_copy(slot_):
        def make(buf_row, ys_row, n_rows):
            return pltpu.make_async_copy(ys_ref.at[_row_span(ys_row, n_rows)],
                                         buf_ref.at[slot_, _row_span(buf_row, n_rows)], sem.at[slot_])
        return make

    @pl.when(i == 0)
    def _():
        _start_segment_copies(seg_ref, tm, row_copy(slot))

    @pl.when(i + 1 < n)
    def _():
        _start_segment_copies(segn_ref, tm, row_copy(1 - slot))

    row_copy(slot)(0, 0, rows).wait()

    lp = lp_ref[...]
    s_ix = lax.broadcasted_iota(I32, (1, rows), 1).astype(F32)
    wmat = jnp.zeros((tm, rows), F32)
    for k in range(TOP_K):
        wmat = jnp.where(s_ix == lp[:, k:k + 1], w4_ref[:, k:k + 1], wmat)
    w_hi = wmat.astype(BF16)
    w_lo = (wmat - w_hi.astype(F32)).astype(BF16)
    cols = []
    for c in range(ROW_TILES):
        y = _load_row_tile(buf_ref.at[slot], rows, c)
        y_hi = y.astype(BF16)
        y_lo = (y - y_hi.astype(F32)).astype(BF16)
        cols.append(_dot(w_hi, y_hi) + _dot(w_hi, y_lo) + _dot(w_lo, y_hi))
    acc = DEEPNORM_ALPHA * h1_ref[...] + jnp.concatenate(cols, axis=1)
    o_ref[...] = _layer_norm(acc, g_ref[...], b_ref[...])


def _combine(seg4, lp, w4, h1, ln_g, ln_b, ys):
    T, D = h1.shape
    tm = TOKEN_TILE
    n_tile = T // tm
    return pl.pallas_call(
        _combine_kernel, grid=(n_tile,),
        in_specs=[_seg_spec(lambda i: (0, i, 0, 0)),
                  _seg_spec(lambda i: (0, jnp.minimum(i + 1, n_tile - 1), 0, 0)),
                  pl.BlockSpec((tm, LANES), lambda i: (i, 0)),
                  pl.BlockSpec((tm, LANES), lambda i: (i, 0)),
                  pl.BlockSpec((tm, D), lambda i: (i, 0)),
                  _full_spec((1, D)), _full_spec((1, D)),
                  pl.BlockSpec(memory_space=pl.ANY)],
        out_specs=pl.BlockSpec((tm, D), lambda i: (i, 0)),
        out_shape=jax.ShapeDtypeStruct((T, D), F32),
        scratch_shapes=[pltpu.VMEM((2, TOP_K * tm * ROW_TILES, LANES), F32), pltpu.SemaphoreType.DMA((2,))],
        compiler_params=_cparams("arbitrary"), name="combine",
    )(seg4, seg4, lp, w4, h1, ln_g, ln_b, ys)


def _rope_tables(positions):
    inv = ROPE_THETA ** (-jnp.arange(0, ROT_DIM, 2, dtype=F32) / ROT_DIM)
    ang = positions.astype(F32)[..., None] * inv
    cos_sin = jnp.concatenate([jnp.cos(ang), jnp.sin(ang)], axis=-1)
    half = ROT_DIM // 2
    spread = np.zeros((ROT_DIM, 3 * LANES), np.float32)
    unit = np.ones((1, LANES), np.float32)
    for lane in range(LANES):
        d = lane % HEAD_DIM
        if d < half:
            spread[d, lane] = 1.0
            spread[half + d, 2 * LANES + lane] = -1.0
            unit[0, lane] = 0.0
        elif d < ROT_DIM:
            spread[d - half, lane] = 1.0
            spread[d, LANES + lane] = 1.0
            unit[0, lane] = 0.0
    return cos_sin, jnp.asarray(spread, BF16), jnp.asarray(unit)


def _split_w_in(w_in):
    widths = (NSA_WIDTH,) + (KV_WIDTH,) * 6 + (NSA_HEADS * N_BRANCH, S5_WIDTH, MEM_WIDTH, N_BRANCH * D_MODEL)
    offs = [0]
    for w in widths:
        offs.append(offs[-1] + w)
    col = lambda i: w_in[:, offs[i]:offs[i + 1]]
    wq, kc, vc, ks, vs, kw, vw, wg, wu, wqm, wm = (col(i) for i in range(11))
    wk = jnp.concatenate([kc, ks, kw], axis=1)
    wv = jnp.concatenate([vc, vs, vw], axis=1)
    per_group = NSA_HPG * N_BRANCH
    wg_pad = jnp.zeros((w_in.shape[0], NSA_GROUPS * LANES), w_in.dtype)
    for g in range(NSA_GROUPS):
        wg_pad = wg_pad.at[:, g * LANES:g * LANES + per_group].set(wg[:, g * per_group:(g + 1) * per_group])
    return tuple(w.astype(BF16) for w in (wq, wk, wv, wg_pad, wu, wqm, wm))


def _compress_weights(w1):
    half = CMP_BLOCK // 2
    eye = np.eye(NSA_GROUPS, dtype=np.float32)

    def arrange(w_half):
        full = jnp.einsum('sdf,gh->sgdhf', w_half, eye)
        return full.reshape(half * NSA_GROUPS * HEAD_DIM, NSA_GROUPS * CMP_HIDDEN).astype(BF16)

    return (w1.reshape(CMP_BLOCK * HEAD_DIM, CMP_HIDDEN).astype(BF16), arrange(w1[:half]), arrange(w1[half:]))


def _s5_weights(a_re, a_im, log_dt, b_re, b_im, c_re, c_im):
    step = jnp.exp(log_dt)[:, None]
    mag = jnp.exp(a_re * step)
    ab_re, ab_im = mag * jnp.cos(a_im * step), mag * jnp.sin(a_im * step)
    den = a_re * a_re + a_im * a_im
    nr = ab_re - 1.0
    coef_re = (nr * a_re + ab_im * a_im) / den
    coef_im = (ab_im * a_re - nr * a_im) / den
    bb_re = coef_re[..., None] * b_re - coef_im[..., None] * b_im
    bb_im = coef_re[..., None] * b_im + coef_im[..., None] * b_re
    n_tile = S5_GROUPS * S5_STATE // LANES
    tile_groups = LANES // S5_STATE
    lane_groups = LANES // S5_GROUP_DIM
    tiles_per_lane_tile = lane_groups // tile_groups
    place = np.zeros((n_tile, lane_groups, tile_groups), np.float32)
    for c in range(n_tile):
        for j in range(tile_groups):
            place[c, (c % tiles_per_lane_tile) * tile_groups + j, j] = 1.0

    def in_blocks(bb):
        pairs = bb.reshape(n_tile, tile_groups, S5_STATE, S5_GROUP_DIM)
        return jnp.einsum('cjnp,caj->capjn', pairs, place).reshape(n_tile, LANES, LANES)

    def out_blocks(c):
        pairs = c.reshape(n_tile, tile_groups, S5_GROUP_DIM, S5_STATE)
        return jnp.einsum('cjpn,caj->cjnap', pairs, place).reshape(n_tile, LANES, LANES)

    wb = jnp.concatenate([in_blocks(bb_re), in_blocks(bb_im)], axis=2).astype(BF16)
    wc = jnp.concatenate([out_blocks(c_re), out_blocks(-c_im)], axis=1).astype(BF16)
    return wb, wc, ab_re.reshape(n_tile, LANES), ab_im.reshape(n_tile, LANES)


def _layer(x, mem, positions, ln_emb_g, ln_emb_b, w_in, pe_k, pe_v, w_kcmp1, w_kcmp2, w_vcmp1, w_vcmp2,
           s5_a_re, s5_a_im, s5_log_dt, s5_b_re, s5_b_im, s5_c_re, s5_c_im, s5_d,
           w_s5_glu, w_mem_kv, w_nsa_out, w_mem_out, w_o, ln1_g, ln1_b, w_router, b_router,
           w_gate_up, b_gate_up, w_down, b_down, ln2_g, ln2_b):
    B, L, D = x.shape
    T = B * L
    row = lambda v: v.reshape(1, -1)

    cos_sin, spread, unit = _rope_tables(positions)
    (q_hm, kc, vc, ks, vs, kw, vw, gates, u, qm, gm) = _inproj(
        x, row(ln_emb_g), row(ln_emb_b), cos_sin, spread, unit, *_split_w_in(w_in))

    n_chunk = L // CMP_STRIDE
    chunked = lambda t: t.reshape(B, n_chunk, CMP_STRIDE * KV_WIDTH)
    pe_rows = lambda pe: jnp.broadcast_to(pe.reshape(1, -1), (SUBLANES, CMP_BLOCK * HEAD_DIM)).astype(BF16)
    wk1f, wk1a, wk1b = _compress_weights(w_kcmp1)
    wv1f, wv1a, wv1b = _compress_weights(w_vcmp1)
    ck, cv = _compress(chunked(kc), chunked(vc), pe_rows(pe_k), pe_rows(pe_v), wk1f, wv1f,
                       wk1a, wk1b, wv1a, wv1b, w_kcmp2.astype(BF16), w_vcmp2.astype(BF16))

    per_sb = SEL_BLOCK // CMP_STRIDE
    c_ix = np.arange(n_chunk)[:, None]
    n_ix = np.arange(L // SEL_BLOCK)[None, :]
    w_score = jnp.asarray((c_ix // per_sb == n_ix).astype(np.float32)
                          + ((c_ix + 1) // per_sb == n_ix).astype(np.float32), BF16)
    o_nsa = _nsa(q_hm, ck, cv, ks, vs, kw, vw, gates, w_score)

    wb, wc, a_re, a_im = _s5_weights(s5_a_re, s5_a_im, s5_log_dt, s5_b_re, s5_b_im, s5_c_re, s5_c_im)
    gy = _s5(u, wb, wc, a_re, a_im, row(s5_d))

    k_mem, v_mem = _memkv(mem, w_mem_kv.astype(BF16))
    o_mem = _memattn(qm, k_mem, v_mem)

    pad_e = LANES - N_EXPERTS
    wr = jnp.pad(w_router, ((0, 0), (0, pad_e)))
    wr_hi = wr.astype(BF16)
    wr_lo = (wr - wr_hi.astype(F32)).astype(BF16)
    br = jnp.concatenate([b_router, jnp.full((pad_e,), -jnp.inf, F32)]).reshape(1, LANES)
    tm = TOKEN_TILE
    n_tile = T // tm
    strict_lower = lambda n: jnp.asarray(np.tril(np.ones((n, n), np.float32), -1), BF16)
    triu = jnp.asarray(np.triu(np.ones((LANES, LANES), np.float32)), BF16)
    striu = jnp.asarray(np.triu(np.ones((LANES, LANES), np.float32), 1), BF16)
    flat = lambda t: t.reshape(T, t.shape[-1])
    h1, lp, w4, cnt = _merge(
        flat(x), row(ln_emb_g), row(ln_emb_b), flat(o_nsa), flat(gy), flat(o_mem), flat(gm),
        w_nsa_out.astype(BF16), w_s5_glu.astype(BF16), w_mem_out.astype(BF16), w_o.astype(BF16),
        row(ln1_g), row(ln1_b), wr_hi, wr_lo, br, strict_lower(tm), striu)

    cap = (T * TOP_K + MOE_ROWS - 1) // MOE_ROWS * MOE_ROWS + N_EXPERTS * MOE_ROWS
    n_blk = cap // MOE_ROWS
    seg, blk_owner, misc = _slots(cnt[:, 0, :], triu, striu, strict_lower(n_tile), n_blk)
    seg4 = seg.reshape(3, n_tile, 1, LANES)
    blk_expert = blk_owner[:, 0]
    n_used = misc[0, :1]

    xs = _dispatch(seg4, misc, lp, h1, cap)
    ys = _experts(blk_expert, n_used, misc[3, :N_EXPERTS], xs, w_gate_up, b_gate_up, w_down, b_down)
    out = _combine(seg4, lp, w4, h1, row(ln2_g), row(ln2_b), ys)
    return out.reshape(B, L, D)


def kernel(x, mem, positions, ln_emb_g, ln_emb_b, w_in, pe_k_cmp, pe_v_cmp, w_kcmp1, w_kcmp2, w_vcmp1, w_vcmp2, s5_a_re, s5_a_im, s5_log_dt, s5_b_re, s5_b_im, s5_c_re, s5_c_im, s5_d, w_s5_glu, w_mem_kv, w_nsa_out, w_mem_out, w_o, ln1_g, ln1_b, w_router, b_router, w_gate_up, b_gate_up, w_down, b_down, ln2_g, ln2_b):
    assert w_in.shape[0] == DEPTH
    l = 0
    return _layer(x, mem, positions, ln_emb_g, ln_emb_b, w_in[l], pe_k_cmp[l], pe_v_cmp[l], w_kcmp1[l],
                  w_kcmp2[l], w_vcmp1[l], w_vcmp2[l], s5_a_re[l], s5_a_im[l], s5_log_dt[l], s5_b_re[l],
                  s5_b_im[l], s5_c_re[l], s5_c_im[l], s5_d[l], w_s5_glu[l], w_mem_kv[l], w_nsa_out[l],
                  w_mem_out[l], w_o[l], ln1_g[l], ln1_b[l], w_router[l], b_router[l], w_gate_up[l],
                  b_gate_up[l], w_down[l], b_down[l], ln2_g[l], ln2_b[l])
```

```python
import functools
import math

import jax
import jax.numpy as jnp
import numpy as np
from jax import lax
from jax.experimental import pallas as pl
from jax.experimental.pallas import tpu as pltpu

F32 = jnp.float32
BF16 = jnp.bfloat16
I32 = jnp.int32

D_MODEL = 1024
NSA_HEADS = 8
NSA_GROUPS = 2
NSA_HPG = NSA_HEADS // NSA_GROUPS
HEAD_DIM = 64
NSA_WIDTH = NSA_HEADS * HEAD_DIM
KV_WIDTH = NSA_GROUPS * HEAD_DIM
CMP_BLOCK = 32
CMP_STRIDE = 16
CMP_HIDDEN = 128
SEL_BLOCK = 64
N_SEL = 16
WINDOW = 512
Q_BLOCK = 256
WINDOW_Q = 128
ROPE_THETA = 500000.0
ROT_DIM = HEAD_DIM // 4
S5_WIDTH = 512
S5_GROUP_DIM = 16
S5_GROUPS = S5_WIDTH // S5_GROUP_DIM
S5_STATE = 64
MEM_HEADS = 4
MEM_HEAD_DIM = 128
MEM_WIDTH = MEM_HEADS * MEM_HEAD_DIM
N_BRANCH = 3
N_EXPERTS = 32
TOP_K = 4
D_FF = 1024
SWIGLU_LIMIT = 7.0
SWIGLU_ALPHA = 1.702
LN_EPS = 1e-5
DEPTH = 1
DEEPNORM_ALPHA = (2 * DEPTH) ** 0.25

LANES = 128
SUBLANES = 8
VMEM_LIMIT_BYTES = 56 * 1024 * 1024

TOKEN_TILE = 256
INPROJ_TOKEN_TILE = 512
MERGE_SORT_TILES = 2
SEL_KV_TILE = 512
S5_CHUNK = 512
S5_PITCH = S5_CHUNK + 8
MOE_ROWS = 512
NEG_BIG = -(2.0 ** 100)
Q_SCALE_LOG2 = HEAD_DIM ** -0.5 * math.log2(math.e)


def _cparams(*sem):
    return pltpu.CompilerParams(dimension_semantics=sem, vmem_limit_bytes=VMEM_LIMIT_BYTES)


def _dot(a, b):
    return jnp.dot(a, b, preferred_element_type=F32)


def _dot_nt(a, b):
    return lax.dot_general(a, b, (((1,), (1,)), ((), ())), preferred_element_type=F32)


def _layer_norm(x, g, b):
    mu = jnp.mean(x, axis=-1, keepdims=True)
    xc = x - mu
    var = jnp.mean(xc * xc, axis=-1, keepdims=True)
    return xc * lax.rsqrt(var + LN_EPS) * g + b


def _gelu_tanh(x):
    cdf = 0.5 * (1.0 + jnp.tanh(math.sqrt(2.0 / math.pi) * (x + 0.044715 * (x * x * x))))
    return x * cdf


def _masked_exp2(s, mask):
    s = jnp.where(mask, s, -jnp.inf)
    m = jnp.max(s, axis=-1, keepdims=True)
    m = jnp.where(m > -jnp.inf, m, 0.0)
    return jnp.exp2(s - m)


def _safe_recip(denom):
    return 1.0 / jnp.maximum(denom, jnp.finfo(F32).tiny)


def _split3(x):
    hi = x.astype(BF16)
    r1 = x - hi.astype(F32)
    mid = r1.astype(BF16)
    lo = (r1 - mid.astype(F32)).astype(BF16)
    return hi, mid, lo


def _full_spec(shape):
    nd = len(shape)
    return pl.BlockSpec(shape, lambda *_: (0,) * nd)


def _inproj_kernel(x_ref, g_ref, b_ref, cs_ref, spread_ref, unit_ref,
                   wq_ref, wk_ref, wv_ref, wg_ref, wu_ref, wqm_ref, wm_ref,
                   q_ref, kc_ref, vc_ref, ks_ref, vs_ref, kw_ref, vw_ref,
                   gate_ref, u_ref, qm_ref, gm_ref):
    h = _layer_norm(x_ref[0], g_ref[...], b_ref[...])
    hb = h.astype(BF16)
    tab = sum(_dot(part, spread_ref[...]) for part in _split3(cs_ref[0]))
    cos_t = tab[:, 0:LANES] + unit_ref[...]
    sin_a = tab[:, LANES:2 * LANES]
    sin_b = tab[:, 2 * LANES:3 * LANES]

    def rope(t):
        return (t * cos_t + pltpu.roll(t, ROT_DIM // 2, 1) * sin_a
                + pltpu.roll(t, LANES - ROT_DIM // 2, 1) * sin_b)

    q = _dot(hb, wq_ref[...])
    for c in range(NSA_WIDTH // LANES):
        qc = rope(q[:, c * LANES:(c + 1) * LANES]) * Q_SCALE_LOG2
        for hh in range(2):
            q_ref[0, 2 * c + hh] = qc[:, hh * HEAD_DIM:(hh + 1) * HEAD_DIM].astype(BF16)
    k3 = _dot(hb, wk_ref[...])
    kc = rope(k3[:, 0:LANES])
    ks = rope(k3[:, LANES:2 * LANES])
    kw = rope(k3[:, 2 * LANES:3 * LANES])
    v3 = _dot(hb, wv_ref[...])
    kc_ref[0] = kc.astype(BF16)
    vc_ref[0] = v3[:, 0:LANES].astype(BF16)
    tm = x_ref.shape[1]
    pos = pl.program_id(1) * tm + lax.broadcasted_iota(I32, (tm, LANES), 0)
    blk_hot = jnp.where(lax.broadcasted_iota(I32, (tm, LANES), 1) == pos // SEL_BLOCK, 1.0, 0.0)
    lane_pad = jnp.zeros((tm, LANES - HEAD_DIM), F32)
    ones_pad = jnp.where(lax.broadcasted_iota(I32, (tm, LANES - HEAD_DIM), 1) == 0, 1.0, 0.0)
    for g in range(NSA_GROUPS):
        sl = slice(g * HEAD_DIM, (g + 1) * HEAD_DIM)
        ks_ref[0, g] = jnp.concatenate([blk_hot, ks[:, sl], lane_pad], axis=1).astype(BF16)
        kw_ref[0, g] = kw[:, sl].astype(BF16)
        vs_ref[0, g] = jnp.concatenate([v3[:, LANES:2 * LANES][:, sl], ones_pad], axis=1).astype(BF16)
        vw_ref[0, g] = jnp.concatenate([v3[:, 2 * LANES:3 * LANES][:, sl], ones_pad], axis=1).astype(BF16)
    gate_ref[0] = jax.nn.sigmoid(_dot(hb, wg_ref[...]))
    u_ref[0] = _dot(hb, wu_ref[...])
    qm_ref[0] = _dot(hb, wqm_ref[...]).astype(BF16)
    gm_ref[0] = jax.nn.sigmoid(_dot(hb, wm_ref[...]))


def _inproj(x, ln_g, ln_b, cos_sin, spread, unit, wq, wk, wv, wg, wu, wqm, wm):
    B, L, D = x.shape
    tm = INPROJ_TOKEN_TILE
    grid = (B, L // tm)
    tok = lambda w: pl.BlockSpec((1, tm, w), lambda b, i: (b, i, 0))
    head = lambda n, w=HEAD_DIM: pl.BlockSpec((1, n, tm, w), lambda b, i: (b, 0, i, 0))
    in_specs = [tok(D), _full_spec((1, D)), _full_spec((1, D)), tok(ROT_DIM), _full_spec(spread.shape),
                _full_spec(unit.shape)]
    in_specs += [pl.BlockSpec(w.shape, lambda b, i: (0, 0), pipeline_mode=pl.Buffered(1))
                 for w in (wq, wk, wv, wg, wu, wqm, wm)]
    sd = jax.ShapeDtypeStruct
    out_shape = [
        sd((B, NSA_HEADS, L, HEAD_DIM), BF16),
        sd((B, L, KV_WIDTH), BF16), sd((B, L, KV_WIDTH), BF16),
        sd((B, NSA_GROUPS, L, 2 * LANES), BF16), sd((B, NSA_GROUPS, L, LANES), BF16),
        sd((B, NSA_GROUPS, L, HEAD_DIM), BF16), sd((B, NSA_GROUPS, L, LANES), BF16),
        sd((B, L, NSA_GROUPS * LANES), F32),
        sd((B, L, S5_WIDTH), F32),
        sd((B, L, MEM_WIDTH), BF16),
        sd((B, L, N_BRANCH * D), F32),
    ]
    out_specs = [head(NSA_HEADS), tok(KV_WIDTH), tok(KV_WIDTH), head(NSA_GROUPS, 2 * LANES),
                 head(NSA_GROUPS, LANES), head(NSA_GROUPS), head(NSA_GROUPS, LANES),
                 tok(NSA_GROUPS * LANES), tok(S5_WIDTH),
                 tok(MEM_WIDTH), tok(N_BRANCH * D)]
    return pl.pallas_call(
        _inproj_kernel, grid=grid, in_specs=in_specs, out_specs=out_specs, out_shape=out_shape,
        compiler_params=_cparams("parallel", "parallel"), name="inproj",
    )(x, ln_g, ln_b, cos_sin, spread, unit, wq, wk, wv, wg, wu, wqm, wm)


def _compress_kernel(kc_ref, vc_ref, pek_ref, pev_ref, wk1f_ref, wv1f_ref,
                     wk1a_ref, wk1b_ref, wv1a_ref, wv1b_ref, wk2_ref, wv2_ref, ck_ref, cv_ref):
    n_chunk = kc_ref.shape[1]
    row = lax.broadcasted_iota(I32, (n_chunk, 1), 0)

    def one(x_ref, pe_ref, w1f_ref, w1a_ref, w1b_ref, w2_ref, o_ref):
        x = x_ref[0]
        first = _dot(x, w1a_ref[...])
        second = _dot(x, w1b_ref[...])
        second = pltpu.roll(second, n_chunk - 1, 0)
        pe_term = _dot(pe_ref[...], w1f_ref[...])[0:1]
        pe_term = jnp.concatenate([pe_term] * NSA_GROUPS, axis=1)
        hid = _gelu_tanh(first + second + pe_term).astype(BF16)
        for g in range(NSA_GROUPS):
            o = _dot(hid[:, g * CMP_HIDDEN:(g + 1) * CMP_HIDDEN], w2_ref[...])
            o_ref[0, g] = jnp.where(row < n_chunk - 1, o, 0.0).astype(BF16)

    one(kc_ref, pek_ref, wk1f_ref, wk1a_ref, wk1b_ref, wk2_ref, ck_ref)
    one(vc_ref, pev_ref, wv1f_ref, wv1a_ref, wv1b_ref, wv2_ref, cv_ref)


def _compress(kc_r, vc_r, pek, pev, wk1f, wv1f, wk1a, wk1b, wv1a, wv1b, wk2, wv2):
    B, n_chunk, width = kc_r.shape
    blk = pl.BlockSpec((1, n_chunk, width), lambda b: (b, 0, 0))
    out = pl.BlockSpec((1, NSA_GROUPS, n_chunk, HEAD_DIM), lambda b: (b, 0, 0, 0))
    ws = [pek, pev, wk1f, wv1f, wk1a, wk1b, wv1a, wv1b, wk2, wv2]
    sd = jax.ShapeDtypeStruct((B, NSA_GROUPS, n_chunk, HEAD_DIM), BF16)
    return pl.pallas_call(
        _compress_kernel, grid=(B,), in_specs=[blk, blk] + [_full_spec(w.shape) for w in ws],
        out_specs=[out, out], out_shape=[sd, sd], compiler_params=_cparams("parallel"), name="compress",
    )(kc_r, vc_r, *ws)


def _nsa_kernel(q_ref, ck_ref, cv_ref, ks_ref, vs_ref, kw_ref, vw_ref, gate_ref, wsc_ref, o_ref):
    seq_len = ks_ref.shape[2]
    n_cmp = ck_ref.shape[2]
    n_sb = seq_len // SEL_BLOCK
    n_sel = min(N_SEL, n_sb)
    rows = NSA_HPG * Q_BLOCK
    groups = range(NSA_GROUPS)
    q0 = pl.program_id(1) * Q_BLOCK
    t1 = q0 + lax.broadcasted_iota(I32, (Q_BLOCK, 1), 0)
    t4 = jnp.concatenate([t1] * NSA_HPG, axis=0)
    tk = SEL_KV_TILE

    def front(g):
        q = q_ref[0, g * NSA_HPG:(g + 1) * NSA_HPG].reshape(rows, HEAD_DIM)

        s = _dot_nt(q, ck_ref[0, g])
        c_end = lax.broadcasted_iota(I32, (1, n_cmp), 1) * CMP_STRIDE + (CMP_BLOCK - 1)
        e = _masked_exp2(s, c_end <= t4)
        p_cmp = e * _safe_recip(jnp.sum(e, axis=-1, keepdims=True))
        o_cmp = _dot(p_cmp.astype(BF16), cv_ref[0, g])

        imp = p_cmp[0:Q_BLOCK]
        for hh in range(1, NSA_HPG):
            imp = imp + p_cmp[hh * Q_BLOCK:(hh + 1) * Q_BLOCK]
        w_sc = wsc_ref[...]
        score = sum(_dot(part, w_sc) for part in _split3(imp))
        score_t = score.T
        jb = lax.broadcasted_iota(I32, (n_sb, Q_BLOCK), 0)
        tb = (q0 + lax.broadcasted_iota(I32, (1, Q_BLOCK), 1)) // SEL_BLOCK
        forced = (jb == 0) | (jb == tb) | (jb == tb - 1)
        work = jnp.where(forced | (jb > tb), -jnp.inf, score_t)
        bias_t = jnp.where(forced, 0.0, NEG_BIG)
        jbf = jb.astype(F32)
        for _ in range(n_sel - 3):
            m = jnp.max(work, axis=0, keepdims=True)
            idx = jnp.min(jnp.where(work == m, jbf, float(n_sb)), axis=0, keepdims=True)
            pick = jbf == idx
            bias_t = jnp.where(pick, 0.0, bias_t)
            work = jnp.where(pick, -jnp.inf, work)
        sel_bias = bias_t.T
        if n_sb < LANES:
            sel_bias = jnp.concatenate([sel_bias, jnp.zeros((Q_BLOCK, LANES - n_sb), F32)], axis=1)

        span = WINDOW + WINDOW_Q
        parts = []
        for sub in range(Q_BLOCK // WINDOW_Q):
            pick = lambda a: jnp.concatenate(
                [a[hh * Q_BLOCK + sub * WINDOW_Q:hh * Q_BLOCK + (sub + 1) * WINDOW_Q] for hh in range(NSA_HPG)],
                axis=0)
            w0 = pl.multiple_of(jnp.maximum(q0 + sub * WINDOW_Q - WINDOW, 0), WINDOW_Q)
            s = _dot_nt(pick(q), kw_ref[0, g, pl.ds(w0, span), :])
            diff = pick(t4) - (w0 + lax.broadcasted_iota(I32, (1, span), 1))
            e = _masked_exp2(s, (diff >= 0) & (diff < WINDOW))
            o = _dot(e.astype(BF16), vw_ref[0, g, pl.ds(w0, span), :])
            parts.append(o[:, :HEAD_DIM] * _safe_recip(o[:, HEAD_DIM:HEAD_DIM + 1]))
        o_win = jnp.concatenate([parts[sub][hh * WINDOW_Q:(hh + 1) * WINDOW_Q]
                                 for hh in range(NSA_HPG) for sub in range(len(parts))], axis=0)

        q_aug = jnp.concatenate([jnp.concatenate([sel_bias.astype(BF16)] * NSA_HPG, axis=0), q,
                                 jnp.zeros((rows, LANES - HEAD_DIM), BF16)], axis=1)
        return q_aug, o_cmp, o_win

    fronts = [front(g) for g in groups]

    def sel_tile(g, j, carry, causal):
        m_run, acc = carry
        k0 = pl.multiple_of(j * tk, tk)
        sc = _dot_nt(fronts[g][0], ks_ref[0, g, pl.ds(k0, tk), :])
        if causal:
            kpos = k0 + lax.broadcasted_iota(I32, (1, tk), 1)
            sc = jnp.where(kpos <= t4, sc, NEG_BIG)
        m_new = jnp.maximum(m_run, jnp.max(sc, axis=-1, keepdims=True))
        p = jnp.exp2(sc - m_new)
        acc_new = jnp.exp2(m_run - m_new) * acc + _dot(p.astype(BF16), vs_ref[0, g, pl.ds(k0, tk), :])
        return m_new, acc_new

    def sel_pair(jj, carries, causal):
        return tuple(sel_tile(g, 2 * jj + 1, sel_tile(g, 2 * jj, carries[g], causal), causal) for g in groups)

    init = tuple((jnp.full((rows, 1), NEG_BIG, F32), jnp.zeros((rows, LANES), F32)) for _ in groups)
    last_pair = (q0 // tk) // 2
    carries = lax.fori_loop(0, last_pair, functools.partial(sel_pair, causal=False), init)
    carries = sel_pair(last_pair, carries, True)

    outs = []
    for g in groups:
        _, o_cmp, o_win = fronts[g]
        acc = carries[g][1]
        o_sel = acc[:, :HEAD_DIM] * (1.0 / acc[:, HEAD_DIM:HEAD_DIM + 1])
        gt = gate_ref[0, :, g * LANES:(g + 1) * LANES]
        for hh in range(NSA_HPG):
            sl = slice(hh * Q_BLOCK, (hh + 1) * Q_BLOCK)
            c = hh * N_BRANCH
            outs.append(o_cmp[sl] * gt[:, c:c + 1] + o_sel[sl] * gt[:, c + 1:c + 2]
                        + o_win[sl] * gt[:, c + 2:c + 3])
    o_ref[0] = jnp.concatenate(outs, axis=1).astype(BF16)


def _nsa(q_hm, ck, cv, ks, vs, kw, vw, gates, w_score):
    B, _, L, _ = q_hm.shape
    assert L // SEL_BLOCK <= LANES and (L // SEL_KV_TILE) % 2 == 0 and L >= WINDOW + Q_BLOCK
    n_cmp = ck.shape[2]
    grid = (B, L // Q_BLOCK)
    qspec = pl.BlockSpec((1, NSA_HEADS, Q_BLOCK, HEAD_DIM), lambda b, i: (b, 0, i, 0))
    cspec = pl.BlockSpec((1, NSA_GROUPS, n_cmp, HEAD_DIM), lambda b, i: (b, 0, 0, 0))
    kvspec = lambda w: pl.BlockSpec((1, NSA_GROUPS, L, w), lambda b, i: (b, 0, 0, 0),
                                    pipeline_mode=pl.Buffered(1))
    gspec = pl.BlockSpec((1, Q_BLOCK, NSA_GROUPS * LANES), lambda b, i: (b, i, 0))
    ospec = pl.BlockSpec((1, Q_BLOCK, NSA_WIDTH), lambda b, i: (b, i, 0))
    return pl.pallas_call(
        _nsa_kernel, grid=grid,
        in_specs=[qspec, cspec, cspec, kvspec(2 * LANES), kvspec(LANES), kvspec(HEAD_DIM), kvspec(LANES), gspec,
                  _full_spec(w_score.shape)],
        out_specs=ospec, out_shape=jax.ShapeDtypeStruct((B, L, NSA_WIDTH), BF16),
        compiler_params=_cparams("parallel", "arbitrary"), name="nsa",
    )(q_hm, ck, cv, ks, vs, kw, vw, gates, w_score)


def _s5_kernel(u_ref, wb_ref, wc_ref, are_ref, aim_ref, d_ref, y_ref, sre_ref, sim_ref, carry_ref):
    n_b, chunk, _ = u_ref.shape
    n_tile = wb_ref.shape[0]
    in_per = n_tile // (S5_WIDTH // LANES)
    pitch = S5_PITCH

    @pl.when(pl.program_id(0) == 0)
    def _():
        carry_ref[...] = jnp.zeros_like(carry_ref)

    for b in range(n_b):
        for c in range(n_tile):
            i = c // in_per
            ub = u_ref[b, :, i * LANES:(i + 1) * LANES].astype(BF16)
            r = _dot(ub, wb_ref[c])
            sre_ref[b, c * pitch:c * pitch + chunk, :] = r[:, :LANES]
            sim_ref[b, c * pitch:c * pitch + chunk, :] = r[:, LANES:]

    a_re, a_im = are_ref[...], aim_ref[...]

    def step(t, carry):
        out = []
        for b in range(n_b):
            s_re, s_im = carry[2 * b], carry[2 * b + 1]
            rows = pl.ds(t, n_tile, stride=pitch)
            n_re = a_re * s_re - a_im * s_im + sre_ref[b, rows, :]
            n_im = a_re * s_im + a_im * s_re + sim_ref[b, rows, :]
            sre_ref[b, rows, :] = n_re
            sim_ref[b, rows, :] = n_im
            out += [n_re, n_im]
        return tuple(out)

    init = tuple(carry_ref[i] for i in range(2 * n_b))
    fin = lax.fori_loop(0, chunk, step, init, unroll=8)
    for i in range(2 * n_b):
        carry_ref[i] = fin[i]

    for b in range(n_b):
        for o in range(S5_WIDTH // LANES):
            acc = jnp.zeros((chunk, LANES), F32)
            for c in range(o * in_per, (o + 1) * in_per):
                rows = slice(c * pitch, c * pitch + chunk)
                state = jnp.concatenate([sre_ref[b, rows, :], sim_ref[b, rows, :]], axis=1).astype(BF16)
                acc = acc + _dot(state, wc_ref[c])
            lanes = slice(o * LANES, (o + 1) * LANES)
            y = acc + d_ref[:, lanes] * u_ref[b, :, lanes]
            y_ref[b, :, lanes] = _gelu_tanh(y).astype(BF16)


def _s5(u, wb, wc, a_re, a_im, d_skip):
    B, L, W = u.shape
    chunk = S5_CHUNK
    n_tile = wb.shape[0]
    blk = pl.BlockSpec((B, chunk, W), lambda i: (0, i, 0))
    slab = pltpu.VMEM((B, n_tile * S5_PITCH, LANES), F32)
    return pl.pallas_call(
        _s5_kernel, grid=(L // chunk,),
        in_specs=[blk] + [_full_spec(w.shape) for w in (wb, wc, a_re, a_im, d_skip)],
        out_specs=blk, out_shape=jax.ShapeDtypeStruct((B, L, W), BF16),
        scratch_shapes=[slab, slab, pltpu.VMEM((2 * B, n_tile, LANES), F32)],
        compiler_params=_cparams("arbitrary"), name="s5",
    )(u, wb, wc, a_re, a_im, d_skip)


def _memkv_kernel(mem_ref, w_ref, k_ref, v_ref):
    kv = _dot(mem_ref[0].astype(BF16), w_ref[...])
    k_ref[0] = kv[:, :MEM_WIDTH].astype(BF16)
    v_ref[0] = kv[:, MEM_WIDTH:].astype(BF16)


def _memkv(mem, w_kv):
    B, M, D = mem.shape
    out = pl.BlockSpec((1, M, MEM_WIDTH), lambda b: (b, 0, 0))
    sd = jax.ShapeDtypeStruct((B, M, MEM_WIDTH), BF16)
    return pl.pallas_call(
        _memkv_kernel, grid=(B,),
        in_specs=[pl.BlockSpec((1, M, D), lambda b: (b, 0, 0)), _full_spec(w_kv.shape)],
        out_specs=[out, out], out_shape=[sd, sd], compiler_params=_cparams("parallel"), name="memkv",
    )(mem, w_kv)


def _memory_attention(q_ref, k_ref, v_ref):
    outs = []
    for h in range(MEM_HEADS):
        sl = slice(h * MEM_HEAD_DIM, (h + 1) * MEM_HEAD_DIM)
        s = _dot_nt(q_ref[:, sl], k_ref[0, :, sl]) * (MEM_HEAD_DIM ** -0.5)
        m = jnp.max(s, axis=-1, keepdims=True)
        e = jnp.exp(s - m)
        p = e / jnp.sum(e, axis=-1, keepdims=True)
        outs.append(_dot(p.astype(BF16), v_ref[0, :, sl]))
    return jnp.concatenate(outs, axis=1).astype(BF16)


def _merge_kernel(x_ref, lng_ref, lnb_ref, on_ref, gy_ref, qm_ref, km_ref, vm_ref, gm_ref,
                  wn_ref, wglu_ref, wmo_ref, wo_ref, l1g_ref, l1b_ref,
                  wrh_ref, wrl_ref, br_ref, tri_ref, striu_ref,
                  h1_ref, lp_ref, w4_ref, cnt_ref):
    D = x_ref.shape[1]
    tm = x_ref.shape[0]
    h =_layer_norm(x_ref[...], lng_ref[...], lnb_ref[...])
    y_nsa = _dot(on_ref[...], wn_ref[...])
    glu = _dot(gy_ref[...], wglu_ref[...])
    y_s5 = glu[:, :D] * jax.nn.sigmoid(glu[:, D:])
    y_mem = _dot(_memory_attention(qm_ref, km_ref, vm_ref), wmo_ref[...])
    merged = gm_ref[:, 0:D] * y_nsa + gm_ref[:, D:2 * D] * y_s5 + gm_ref[:, 2 * D:3 * D] * y_mem
    mix = _dot(merged.astype(BF16), wo_ref[...])
    h1 = _layer_norm(DEEPNORM_ALPHA * h + mix, l1g_ref[...], l1b_ref[...])
    h1_ref[...] = h1

    hh = h1.astype(BF16)
    hl = (h1 - hh.astype(F32)).astype(BF16)
    logits = _dot(hh, wrh_ref[...]) + _dot(hh, wrl_ref[...]) + _dot(hl, wrh_ref[...]) + br_ref[...]
    lane = lax.broadcasted_iota(I32, (tm, LANES), 1)
    lane_f = lane.astype(F32)
    work = logits
    multi = jnp.zeros((tm, LANES), F32)
    vals, picks = [], []
    for _ in range(TOP_K):
        m = jnp.max(work, axis=-1, keepdims=True)
        idx = jnp.min(jnp.where(work == m, lane_f, float(LANES)), axis=-1, keepdims=True)
        pick = lane_f == idx
        vals.append(m)
        picks.append((pick, idx))
        multi = jnp.where(pick, 1.0, multi)
        work = jnp.where(pick, -jnp.inf, work)
    es = [jnp.exp(v - vals[0]) for v in vals]
    den = es[0] + es[1] + es[2] + es[3]
    st = TOKEN_TILE
    pos = []
    for t in range(tm // st):
        multi_t = multi[t * st:(t + 1) * st]
        cnt = jnp.broadcast_to(jnp.sum(multi_t, axis=0, keepdims=True), (SUBLANES, LANES))
        cnt_ref[t] = cnt
        lower = _dot(cnt.astype(BF16), striu_ref[...])[0:1]
        pos.append(lower + _dot(tri_ref[...], multi_t.astype(BF16)))
    pos = jnp.concatenate(pos, axis=0)
    lp = jnp.full((tm, LANES), -1.0, F32)
    w4 = jnp.zeros((tm, LANES), F32)
    for k in range(TOP_K):
        pick, _ = picks[k]
        lp = jnp.where(lane == k, jnp.sum(jnp.where(pick, pos, 0.0), axis=-1, keepdims=True), lp)
        w4 = jnp.where(lane == k, es[k] / den, w4)
    lp_ref[...] = lp
    w4_ref[...] = w4


def _merge(x2, lng, lnb, o_nsa, gy, qm, k_mem, v_mem, gm, wn, wglu, wmo, wo, l1g, l1b, wrh, wrl, br, tri, striu):
    T, D = x2.shape
    tm = MERGE_SORT_TILES * TOKEN_TILE
    tok = lambda w: pl.BlockSpec((tm, w), lambda i: (i, 0))
    steps_per_batch = T // k_mem.shape[0] // tm
    mem_kv = pl.BlockSpec((1,) + k_mem.shape[1:], lambda i: (i // steps_per_batch, 0, 0))
    ws = [wn, wglu, wmo, wo, l1g, l1b, wrh, wrl, br, tri, striu]
    sd = jax.ShapeDtypeStruct
    lane_out = sd((T, LANES), F32)
    return pl.pallas_call(
        _merge_kernel, grid=(T // tm,),
        in_specs=[tok(D), _full_spec((1, D)), _full_spec((1, D)), tok(NSA_WIDTH), tok(S5_WIDTH),
                  tok(MEM_WIDTH), mem_kv, mem_kv, tok(N_BRANCH * D)] + [_full_spec(w.shape) for w in ws],
        out_specs=[tok(D), tok(LANES), tok(LANES),
                   pl.BlockSpec((MERGE_SORT_TILES, SUBLANES, LANES), lambda i: (i, 0, 0))],
        out_shape=[sd((T, D), F32), lane_out, lane_out, sd((T // TOKEN_TILE, SUBLANES, LANES), F32)],
        compiler_params=_cparams("parallel"), name="merge",
    )(x2, lng, lnb, o_nsa, gy, qm, k_mem, v_mem, gm, *ws)


def _slots_kernel(cnt_ref, triu_ref, striu_ref, tril_ref, seg_ref, blk_ref, misc_ref):
    n_blk = blk_ref.shape[0]
    cnt = cnt_ref[...]
    cnt_b = cnt.astype(BF16)
    total = jnp.sum(cnt, axis=0, keepdims=True)
    nblk_e = jnp.floor((total + (MOE_ROWS - 1)) * (1.0 / MOE_ROWS))
    nblk_8 = jnp.broadcast_to(nblk_e, (SUBLANES, LANES))
    end_b = _dot(nblk_8.astype(BF16), triu_ref[...])
    start_rows = (end_b - nblk_8)[0:1] * MOE_ROWS
    dst = start_rows + _dot(tril_ref[...], cnt_b)
    off = _dot(cnt_b, striu_ref[...])
    seg_ref[0] = cnt.astype(I32)
    seg_ref[1] = off.astype(I32)
    seg_ref[2] = dst.astype(I32)
    blk_i = lax.broadcasted_iota(I32, (n_blk, LANES), 0).astype(F32)
    lane_b = lax.broadcasted_iota(I32, (n_blk, LANES), 1)
    ended = jnp.where((end_b[0:1] <= blk_i) & (lane_b < N_EXPERTS), 1.0, 0.0)
    owner = jnp.minimum(jnp.sum(ended, axis=-1, keepdims=True), float(N_EXPERTS - 1))
    blk_ref[...] = jnp.broadcast_to(owner, (n_blk, LANES)).astype(I32)
    cand = lax.broadcasted_iota(I32, (LANES, LANES), 0)
    has_blocks = jnp.broadcast_to(nblk_e, (LANES, LANES)).T > 0.0
    later = (cand > lax.broadcasted_iota(I32, (LANES, LANES), 1)) & has_blocks
    nxt = jnp.min(jnp.where(later, cand.astype(F32), float(LANES)), axis=0, keepdims=True)
    nxt = jnp.where(nxt < float(LANES), nxt, -1.0)
    lane8 = lax.broadcasted_iota(I32, (SUBLANES, LANES), 1)
    row8 = lax.broadcasted_iota(I32, (SUBLANES, LANES), 0)
    used = jnp.sum(jnp.where(lane8 == N_EXPERTS - 1, end_b, 0.0), axis=-1, keepdims=True)
    misc = jnp.where(row8 == 0, used,
                     jnp.where(row8 == 1, start_rows + total,
                               jnp.where(row8 == 2, nblk_e * MOE_ROWS - total, nxt)))
    misc_ref[...] = misc.astype(I32)


def _slots(cnt, triu, striu, tril, n_blk):
    n_tile = cnt.shape[0]
    sd = jax.ShapeDtypeStruct
    return pl.pallas_call(
        _slots_kernel, grid=(1,),
        in_specs=[_full_spec(cnt.shape), _full_spec(triu.shape), _full_spec(striu.shape), _full_spec(tril.shape)],
        out_specs=[_full_spec((3, n_tile, LANES)), _full_spec((n_blk, LANES)), _full_spec((SUBLANES, LANES))],
        out_shape=[sd((3, n_tile, LANES), I32), sd((n_blk, LANES), I32), sd((SUBLANES, LANES), I32)],
        compiler_params=_cparams("arbitrary"), name="slots",
    )(cnt, triu, striu, tril)


ROW_TILES = D_MODEL // LANES


def _row_span(row, n_rows):
    start = row * ROW_TILES
    if not isinstance(start, int):
        start = pl.multiple_of(start, ROW_TILES)
    return pl.ds(start, n_rows * ROW_TILES)


def _store_rows(ref, val):
    for c in range(ROW_TILES):
        ref[pl.ds(c, val.shape[0], stride=ROW_TILES), :] = val[:, c * LANES:(c + 1) * LANES]


def _load_row_tile(ref, n_rows, c):
    return ref[pl.ds(c, n_rows, stride=ROW_TILES), :]


BIG_PIECE_ROWS = 64


def _pieces(count, max_rows, fn):
    def run(sizes):
        for p in sizes:
            def piece(p=p):
                fn(count & (-2 * p), p)
            pl.when((count & p) != 0)(piece)

    sizes = [max_rows >> s for s in range(max_rows.bit_length())]
    big = [p for p in sizes if p >= BIG_PIECE_ROWS]
    if big:
        pl.when(count >= BIG_PIECE_ROWS)(lambda: run(big))
    run([p for p in sizes if p < BIG_PIECE_ROWS])


def _start_segment_copies(seg_ref, max_rows, make_copy):
    def per_expert(e, c):
        cnt, off, dst = seg_ref[0, 0, 0, e], seg_ref[1, 0, 0, e], seg_ref[2, 0, 0, e]
        _pieces(cnt, max_rows, lambda first, rows: make_copy(off + first, dst + first, rows).start())
        return c

    lax.fori_loop(0, N_EXPERTS, per_expert, 0)


def _dispatch_kernel(seg_ref, misc_ref, lp_ref, h_ref, xs_ref, sorted_ref, zero_ref, sem, pad_sem):
    i = pl.program_id(0)
    n = pl.num_programs(0)
    tm, D = h_ref.shape
    rows = TOP_K * tm
    slot = lax.rem(i, 2)

    def row_copy(slot_):
        def make(src_row, dst_row, n_rows):
            return pltpu.make_async_copy(sorted_ref.at[slot_, _row_span(src_row, n_rows)],
                                         xs_ref.at[_row_span(dst_row, n_rows)], sem.at[slot_])
        return make

    @pl.when(i == 0)
    def _():
        zero_ref[...] = jnp.zeros_like(zero_ref)
        for wait in (False, True):
            def per_expert(e, c, wait=wait):
                def one(first, n_rows):
                    cp = pltpu.make_async_copy(zero_ref.at[_row_span(0, n_rows)],
                                               xs_ref.at[_row_span(misc_ref[1, e] + first, n_rows)], pad_sem)
                    cp.wait() if wait else cp.start()
                _pieces(misc_ref[2, e], MOE_ROWS // 2, one)
                return c
            lax.fori_loop(0, N_EXPERTS, per_expert, 0)

            def per_spare_half_block(hb, c, wait=wait):
                cp = pltpu.make_async_copy(zero_ref, xs_ref.at[_row_span(hb * (MOE_ROWS // 2), MOE_ROWS // 2)],
                                           pad_sem)
                cp.wait() if wait else cp.start()
                return c
            lax.fori_loop(2 * misc_ref[0, 0], 2 * (xs_ref.shape[0] // (MOE_ROWS * ROW_TILES)),
                          per_spare_half_block, 0)

    lp_t = lp_ref[...].T
    s_ix = lax.broadcasted_iota(I32, (rows, 1), 0).astype(F32)
    hit = s_ix == lp_t[0:1, :]
    for k in range(1, TOP_K):
        hit = hit | (s_ix == lp_t[k:k + 1, :])
    perm = jnp.where(hit, 1.0, 0.0).astype(BF16)
    _store_rows(sorted_ref.at[slot], _dot(perm, h_ref[...].astype(BF16)))

    _start_segment_copies(seg_ref, tm, row_copy(slot))

    @pl.when(i > 0)
    def _():
        row_copy(1 - slot)(0, 0, rows).wait()

    @pl.when(i == n - 1)
    def _():
        row_copy(slot)(0, 0, rows).wait()


def _seg_spec(index_map):
    return pl.BlockSpec((3, 1, 1, LANES), index_map, memory_space=pltpu.SMEM)


def _dispatch(seg4, misc, lp, h1, cap):
    T, D = h1.shape
    assert D == ROW_TILES * LANES
    tm = TOKEN_TILE
    tok = lambda w: pl.BlockSpec((tm, w), lambda i: (i, 0))
    return pl.pallas_call(
        _dispatch_kernel, grid=(T // tm,),
        in_specs=[_seg_spec(lambda i: (0, i, 0, 0)), pl.BlockSpec(memory_space=pltpu.SMEM), tok(LANES), tok(D)],
        out_specs=pl.BlockSpec(memory_space=pl.ANY),
        out_shape=jax.ShapeDtypeStruct((cap * ROW_TILES, LANES), F32),
        scratch_shapes=[pltpu.VMEM((2, TOP_K * tm * ROW_TILES, LANES), F32),
                        pltpu.VMEM((MOE_ROWS // 2 * ROW_TILES, LANES), F32),
                        pltpu.SemaphoreType.DMA((2,)), pltpu.SemaphoreType.DMA(())],
        compiler_params=_cparams("arbitrary"), name="dispatch",
    )(seg4, misc, lp, h1)


def _expert_kernel(blk_ref, used_ref, next_ref, xs_ref, wgu_hbm, bgu_ref, wd_hbm, bd_ref, ys_ref,
                   wgu_f32, wd_f32, wgu_bf, wd_bf, sem, run_ref):
    i = pl.program_id(0)
    live = i < used_ref[0]
    expert = blk_ref[i]

    def weight_copies(e, slot):
        return (pltpu.make_async_copy(wgu_hbm.at[e], wgu_f32.at[slot], sem.at[0, slot]),
                pltpu.make_async_copy(wd_hbm.at[e], wd_f32.at[slot], sem.at[1, slot]))

    @pl.when(i == 0)
    def _():
        run_ref[0] = 0
        for cp in weight_copies(expert, 0):
            cp.start()

    @pl.when(live & ((i == 0) | (expert != blk_ref[jnp.maximum(i - 1, 0)])))
    def _():
        slot = lax.rem(run_ref[0], 2)
        for cp in weight_copies(expert, slot):
            cp.wait()
        wgu_bf[...] = wgu_f32[slot].astype(BF16)
        wd_bf[...] = wd_f32[slot].astype(BF16)
        nxt = next_ref[expert]

        @pl.when(nxt >= 0)
        def _():
            for cp in weight_copies(nxt, 1 - slot):
                cp.start()

        run_ref[0] = run_ref[0] + 1

    @pl.when(live)
    def _():
        xb = jnp.concatenate([_load_row_tile(xs_ref, MOE_ROWS, c).astype(BF16) for c in range(ROW_TILES)],
                             axis=1)
        gu = _dot(xb, wgu_bf[...]) + bgu_ref[0]
        g = jnp.minimum(gu[:, :D_FF], SWIGLU_LIMIT)
        lin = jnp.clip(gu[:, D_FF:], -SWIGLU_LIMIT, SWIGLU_LIMIT)
        act = g * jax.nn.sigmoid(SWIGLU_ALPHA * g) * (lin + 1.0)
        _store_rows(ys_ref, _dot(act.astype(BF16), wd_bf[...]) + bd_ref[0])

    @pl.when(pl.program_id(0) >= used_ref[0])
    def _():
        ys_ref[...] = jnp.zeros_like(ys_ref)


def _experts(blk_expert, n_used, next_expert, xs, w_gate_up, b_gate_up, w_down, b_down):
    D = w_down.shape[2]
    n_blk = xs.shape[0] // (MOE_ROWS * ROW_TILES)
    E = w_gate_up.shape[0]
    live = lambda i, used: jnp.minimum(i, used[0] - 1)
    row = pl.BlockSpec((MOE_ROWS * ROW_TILES, LANES), lambda i, blk, used, nxt: (live(i, used), 0))
    by_e = lambda shape: pl.BlockSpec((1,) + shape, lambda i, blk, used, nxt: (blk[live(i, used)], 0, 0))
    in_hbm = pl.BlockSpec(memory_space=pl.ANY)
    grid_spec = pltpu.PrefetchScalarGridSpec(
        num_scalar_prefetch=3, grid=(n_blk,),
        in_specs=[row, in_hbm, by_e((1, 2 * D_FF)), in_hbm, by_e((1, D))],
        out_specs=pl.BlockSpec((MOE_ROWS * ROW_TILES, LANES), lambda i, blk, used, nxt: (i, 0)),
        scratch_shapes=[pltpu.VMEM((2, D, 2 * D_FF), F32), pltpu.VMEM((2, D_FF, D), F32),
                        pltpu.VMEM((D, 2 * D_FF), BF16), pltpu.VMEM((D_FF, D), BF16),
                        pltpu.SemaphoreType.DMA((2, 2)), pltpu.SMEM((1,), I32)])
    return pl.pallas_call(
        _expert_kernel, grid_spec=grid_spec, out_shape=jax.ShapeDtypeStruct(xs.shape, F32),
        compiler_params=_cparams("arbitrary"), name="experts",
    )(blk_expert, n_used, next_expert, xs, w_gate_up, b_gate_up.reshape(E, 1, 2 * D_FF), w_down,
      b_down.reshape(E, 1, D))


def _combine_kernel(seg_ref, segn_ref, lp_ref, w4_ref, h1_ref, g_ref, b_ref, ys_ref, o_ref, buf_ref, sem):
    i = pl.program_id(0)
    n = pl.num_programs(0)
    tm = h1_ref.shape[0]
    rows = TOP_K * tm
    slot = lax.rem(i, 2)

    def row_copy(slot_):
        def make(buf_row, ys_row, n_rows):
            return pltpu.make_async_copy(ys_ref.at[_row_span(ys_row, n_rows)],
                                         buf_ref.at[slot_, _row_span(buf_row, n_rows)], sem.at[slot_])
        return make

    @pl.when(i == 0)
    def _():
        _start_segment_copies(seg_ref, tm, row_copy(slot))

    @pl.when(i + 1 < n)
    def _():
        _start_segment_copies(segn_ref, tm, row_copy(1 - slot))

    row_copy(slot)(0, 0, rows).wait()

    lp = lp_ref[...]
    s_ix = lax.broadcasted_iota(I32, (1, rows), 1).astype(F32)
    wmat = jnp.zeros((tm, rows), F32)
    for k in range(TOP_K):
        wmat = jnp.where(s_ix == lp[:, k:k + 1], w4_ref[:, k:k + 1], wmat)
    w_hi = wmat.astype(BF16)
    w_lo = (wmat - w_hi.astype(F32)).astype(BF16)
    cols = []
    for c in range(ROW_TILES):
        y = _load_row_tile(buf_ref.at[slot], rows, c)
        y_hi = y.astype(BF16)
        y_lo = (y - y_hi.astype(F32)).astype(BF16)
        cols.append(_dot(w_hi, y_hi) + _dot(w_hi, y_lo) + _dot(w_lo, y_hi))
    acc = DEEPNORM_ALPHA * h1_ref[...] + jnp.concatenate(cols, axis=1)
    o_ref[...] = _layer_norm(acc, g_ref[...], b_ref[...])


def _combine(seg4, lp, w4, h1, ln_g, ln_b, ys):
    T, D = h1.shape
    tm = TOKEN_TILE
    n_tile = T // tm
    return pl.pallas_call(
        _combine_kernel, grid=(n_tile,),
        in_specs=[_seg_spec(lambda i: (0, i, 0, 0)),
                  _seg_spec(lambda i: (0, jnp.minimum(i + 1, n_tile - 1), 0, 0)),
                  pl.BlockSpec((tm, LANES), lambda i: (i, 0)),
                  pl.BlockSpec((tm, LANES), lambda i: (i, 0)),
                  pl.BlockSpec((tm, D), lambda i: (i, 0)),
                  _full_spec((1, D)), _full_spec((1, D)),
                  pl.BlockSpec(memory_space=pl.ANY)],
        out_specs=pl.BlockSpec((tm, D), lambda i: (i, 0)),
        out_shape=jax.ShapeDtypeStruct((T, D), F32),
        scratch_shapes=[pltpu.VMEM((2, TOP_K * tm * ROW_TILES, LANES), F32), pltpu.SemaphoreType.DMA((2,))],
        compiler_params=_cparams("arbitrary"), name="combine",
    )(seg4, seg4, lp, w4, h1, ln_g, ln_b, ys)


def _rope_tables(positions):
    inv = ROPE_THETA ** (-jnp.arange(0, ROT_DIM, 2, dtype=F32) / ROT_DIM)
    ang = positions.astype(F32)[..., None] * inv
    cos_sin = jnp.concatenate([jnp.cos(ang), jnp.sin(ang)], axis=-1)
    half = ROT_DIM // 2
    spread = np.zeros((ROT_DIM, 3 * LANES), np.float32)
    unit = np.ones((1, LANES), np.float32)
    for lane in range(LANES):
        d = lane % HEAD_DIM
        if d < half:
            spread[d, lane] = 1.0
            spread[half + d, 2 * LANES + lane] = -1.0
            unit[0, lane] = 0.0
        elif d < ROT_DIM:
            spread[d - half, lane] = 1.0
            spread[d, LANES + lane] = 1.0
            unit[0, lane] = 0.0
    return cos_sin, jnp.asarray(spread, BF16), jnp.asarray(unit)


def _split_w_in(w_in):
    widths = (NSA_WIDTH,) + (KV_WIDTH,) * 6 + (NSA_HEADS * N_BRANCH, S5_WIDTH, MEM_WIDTH, N_BRANCH * D_MODEL)
    offs = [0]
    for w in widths:
        offs.append(offs[-1] + w)
    col = lambda i: w_in[:, offs[i]:offs[i + 1]]
    wq, kc, vc, ks, vs, kw, vw, wg, wu, wqm, wm = (col(i) for i in range(11))
    wk = jnp.concatenate([kc, ks, kw], axis=1)
    wv = jnp.concatenate([vc, vs, vw], axis=1)
    per_group = NSA_HPG * N_BRANCH
    wg_pad = jnp.zeros((w_in.shape[0], NSA_GROUPS * LANES), w_in.dtype)
    for g in range(NSA_GROUPS):
        wg_pad = wg_pad.at[:, g * LANES:g * LANES + per_group].set(wg[:, g * per_group:(g + 1) * per_group])
    return tuple(w.astype(BF16) for w in (wq, wk, wv, wg_pad, wu, wqm, wm))


def _compress_weights(w1):
    half = CMP_BLOCK // 2
    eye = np.eye(NSA_GROUPS, dtype=np.float32)

    def arrange(w_half):
        full = jnp.einsum('sdf,gh->sgdhf', w_half, eye)
        return full.reshape(half * NSA_GROUPS * HEAD_DIM, NSA_GROUPS * CMP_HIDDEN).astype(BF16)

    return (w1.reshape(CMP_BLOCK * HEAD_DIM, CMP_HIDDEN).astype(BF16), arrange(w1[:half]), arrange(w1[half:]))


def _s5_weights(a_re, a_im, log_dt, b_re, b_im, c_re, c_im):
    step = jnp.exp(log_dt)[:, None]
    mag = jnp.exp(a_re * step)
    ab_re, ab_im = mag * jnp.cos(a_im * step), mag * jnp.sin(a_im * step)
    den = a_re * a_re + a_im * a_im
    nr = ab_re - 1.0
    coef_re = (nr * a_re + ab_im * a_im) / den
    coef_im = (ab_im * a_re - nr * a_im) / den
    bb_re = coef_re[..., None] * b_re - coef_im[..., None] * b_im
    bb_im = coef_re[..., None] * b_im + coef_im[..., None] * b_re
    n_tile = S5_GROUPS * S5_STATE // LANES
    tile_groups = LANES // S5_STATE
    lane_groups = LANES // S5_GROUP_DIM
    tiles_per_lane_tile = lane_groups // tile_groups
    place = np.zeros((n_tile, lane_groups, tile_groups), np.float32)
    for c in range(n_tile):
        for j in range(tile_groups):
            place[c, (c % tiles_per_lane_tile) * tile_groups + j, j] = 1.0

    def in_blocks(bb):
        pairs = bb.reshape(n_tile, tile_groups, S5_STATE, S5_GROUP_DIM)
        return jnp.einsum('cjnp,caj->capjn', pairs, place).reshape(n_tile, LANES, LANES)

    def out_blocks(c):
        pairs = c.reshape(n_tile, tile_groups, S5_GROUP_DIM, S5_STATE)
        return jnp.einsum('cjpn,caj->cjnap', pairs, place).reshape(n_tile, LANES, LANES)

    wb = jnp.concatenate([in_blocks(bb_re), in_blocks(bb_im)], axis=2).astype(BF16)
    wc = jnp.concatenate([out_blocks(c_re), out_blocks(-c_im)], axis=1).astype(BF16)
    return wb, wc, ab_re.reshape(n_tile, LANES), ab_im.reshape(n_tile, LANES)


def _layer(x, mem, positions, ln_emb_g, ln_emb_b, w_in, pe_k, pe_v, w_kcmp1, w_kcmp2, w_vcmp1, w_vcmp2,
           s5_a_re, s5_a_im, s5_log_dt, s5_b_re, s5_b_im, s5_c_re, s5_c_im, s5_d,
           w_s5_glu, w_mem_kv, w_nsa_out, w_mem_out, w_o, ln1_g, ln1_b, w_router, b_router,
           w_gate_up, b_gate_up, w_down, b_down, ln2_g, ln2_b):
    B, L, D = x.shape
    T = B * L
    row = lambda v: v.reshape(1, -1)

    cos_sin, spread, unit = _rope_tables(positions)
    (q_hm, kc, vc, ks, vs, kw, vw, gates, u, qm, gm) = _inproj(
        x, row(ln_emb_g), row(ln_emb_b), cos_sin, spread, unit, *_split_w_in(w_in))

    n_chunk = L // CMP_STRIDE
    chunked = lambda t: t.reshape(B, n_chunk, CMP_STRIDE * KV_WIDTH)
    pe_rows = lambda pe: jnp.broadcast_to(pe.reshape(1, -1), (SUBLANES, CMP_BLOCK * HEAD_DIM)).astype(BF16)
    wk1f, wk1a, wk1b = _compress_weights(w_kcmp1)
    wv1f, wv1a, wv1b = _compress_weights(w_vcmp1)
    ck, cv = _compress(chunked(kc), chunked(vc), pe_rows(pe_k), pe_rows(pe_v), wk1f, wv1f,
                       wk1a, wk1b, wv1a, wv1b, w_kcmp2.astype(BF16), w_vcmp2.astype(BF16))

    per_sb = SEL_BLOCK // CMP_STRIDE
    c_ix = np.arange(n_chunk)[:, None]
    n_ix = np.arange(L // SEL_BLOCK)[None, :]
    w_score = jnp.asarray((c_ix // per_sb == n_ix).astype(np.float32)
                          + ((c_ix + 1) // per_sb == n_ix).astype(np.float32), BF16)
    o_nsa = _nsa(q_hm, ck, cv, ks, vs, kw, vw, gates, w_score)

    wb, wc, a_re, a_im = _s5_weights(s5_a_re, s5_a_im, s5_log_dt, s5_b_re, s5_b_im, s5_c_re, s5_c_im)
    gy = _s5(u, wb, wc, a_re, a_im, row(s5_d))

    k_mem, v_mem = _memkv(mem, w_mem_kv.astype(BF16))

    pad_e = LANES - N_EXPERTS
    wr = jnp.pad(w_router, ((0, 0), (0, pad_e)))
    wr_hi = wr.astype(BF16)
    wr_lo = (wr - wr_hi.astype(F32)).astype(BF16)
    br = jnp.concatenate([b_router, jnp.full((pad_e,), -jnp.inf, F32)]).reshape(1, LANES)
    tm = TOKEN_TILE
    n_tile = T // tm
    strict_lower = lambda n: jnp.asarray(np.tril(np.ones((n, n), np.float32), -1), BF16)
    triu = jnp.asarray(np.triu(np.ones((LANES, LANES), np.float32)), BF16)
    striu = jnp.asarray(np.triu(np.ones((LANES, LANES), np.float32), 1), BF16)
    flat = lambda t: t.reshape(T, t.shape[-1])
    h1, lp, w4, cnt = _merge(
        flat(x), row(ln_emb_g), row(ln_emb_b), flat(o_nsa), flat(gy), flat(qm), k_mem, v_mem, flat(gm),
        w_nsa_out.astype(BF16), w_s5_glu.astype(BF16), w_mem_out.astype(BF16), w_o.astype(BF16),
        row(ln1_g), row(ln1_b), wr_hi, wr_lo, br, strict_lower(tm), striu)

    cap = (T * TOP_K + MOE_ROWS - 1) // MOE_ROWS * MOE_ROWS + N_EXPERTS * MOE_ROWS
    n_blk = cap // MOE_ROWS
    seg, blk_owner, misc = _slots(cnt[:, 0, :], triu, striu, strict_lower(n_tile), n_blk)
    seg4 = seg.reshape(3, n_tile, 1, LANES)
    blk_expert = blk_owner[:, 0]
    n_used = misc[0, :1]

    xs = _dispatch(seg4, misc, lp, h1, cap)
    ys = _experts(blk_expert, n_used, misc[3, :N_EXPERTS], xs, w_gate_up, b_gate_up, w_down, b_down)
    out = _combine(seg4, lp, w4, h1, row(ln2_g), row(ln2_b), ys)
    return out.reshape(B, L, D)


def kernel(x, mem, positions, ln_emb_g, ln_emb_b, w_in, pe_k_cmp, pe_v_cmp, w_kcmp1, w_kcmp2, w_vcmp1, w_vcmp2, s5_a_re, s5_a_im, s5_log_dt, s5_b_re, s5_b_im, s5_c_re, s5_c_im, s5_d, w_s5_glu, w_mem_kv, w_nsa_out, w_mem_out, w_o, ln1_g, ln1_b, w_router, b_router, w_gate_up, b_gate_up, w_down, b_down, ln2_g, ln2_b):
    assert w_in.shape[0] == DEPTH
    l = 0
    return _layer(x, mem, positions, ln_emb_g, ln_emb_b, w_in[l], pe_k_cmp[l], pe_v_cmp[l], w_kcmp1[l],
                  w_kcmp2[l], w_vcmp1[l], w_vcmp2[l], s5_a_re[l], s5_a_im[l], s5_log_dt[l], s5_b_re[l],
                  s5_b_im[l], s5_c_re[l], s5_c_im[l], s5_d[l], w_s5_glu[l], w_mem_kv[l], w_nsa_out[l],
                  w_mem_out[l], w_o[l], ln1_g[l], ln1_b[l], w_router[l], b_router[l], w_gate_up[l],
                  b_gate_up[l], w_down[l], b_down[l], ln2_g[l], ln2_b[l])
```

```python
import functools
import math

import jax
import jax.numpy as jnp
import numpy as np
from jax import lax
from jax.experimental import pallas as pl
from jax.experimental.pallas import tpu as pltpu

F32 = jnp.float32
BF16 = jnp.bfloat16
I32 = jnp.int32

D_MODEL = 1024
NSA_HEADS = 8
NSA_GROUPS = 2
NSA_HPG = NSA_HEADS // NSA_GROUPS
HEAD_DIM = 64
NSA_WIDTH = NSA_HEADS * HEAD_DIM
KV_WIDTH = NSA_GROUPS * HEAD_DIM
CMP_BLOCK = 32
CMP_STRIDE = 16
CMP_HIDDEN = 128
SEL_BLOCK = 64
N_SEL = 16
WINDOW = 512
Q_BLOCK = 256
WINDOW_Q = 128
ROPE_THETA = 500000.0
ROT_DIM = HEAD_DIM // 4
S5_WIDTH = 512
S5_GROUP_DIM = 16
S5_GROUPS = S5_WIDTH // S5_GROUP_DIM
S5_STATE = 64
MEM_HEADS = 4
MEM_HEAD_DIM = 128
MEM_WIDTH = MEM_HEADS * MEM_HEAD_DIM
N_BRANCH = 3
N_EXPERTS = 32
TOP_K = 4
D_FF = 1024
SWIGLU_LIMIT = 7.0
SWIGLU_ALPHA = 1.702
LN_EPS = 1e-5
DEPTH = 1
DEEPNORM_ALPHA = (2 * DEPTH) ** 0.25

LANES = 128
SUBLANES = 8
VMEM_LIMIT_BYTES = 56 * 1024 * 1024

TOKEN_TILE = 256
INPROJ_TOKEN_TILE = 512
MERGE_SORT_TILES = 2
SEL_KV_TILE = 512
S5_CHUNK = 512
S5_PITCH = S5_CHUNK + 8
MOE_ROWS = 512
NEG_BIG = -(2.0 ** 100)
Q_SCALE_LOG2 = HEAD_DIM ** -0.5 * math.log2(math.e)


def _cparams(*sem):
    return pltpu.CompilerParams(dimension_semantics=sem, vmem_limit_bytes=VMEM_LIMIT_BYTES)


def _dot(a, b):
    return jnp.dot(a, b, preferred_element_type=F32)


def _dot_nt(a, b):
    return lax.dot_general(a, b, (((1,), (1,)), ((), ())), preferred_element_type=F32)


def _layer_norm(x, g, b):
    mu = jnp.mean(x, axis=-1, keepdims=True)
    xc = x - mu
    var = jnp.mean(xc * xc, axis=-1, keepdims=True)
    return xc * lax.rsqrt(var + LN_EPS) * g + b


def _gelu_tanh(x):
    cdf = 0.5 * (1.0 + jnp.tanh(math.sqrt(2.0 / math.pi) * (x + 0.044715 * (x * x * x))))
    return x * cdf


def _masked_exp2(s, mask):
    s = jnp.where(mask, s, -jnp.inf)
    m = jnp.max(s, axis=-1, keepdims=True)
    m = jnp.where(m > -jnp.inf, m, 0.0)
    return jnp.exp2(s - m)


def _safe_recip(denom):
    return 1.0 / jnp.maximum(denom, jnp.finfo(F32).tiny)


def _split3(x):
    hi = x.astype(BF16)
    r1 = x - hi.astype(F32)
    mid = r1.astype(BF16)
    lo = (r1 - mid.astype(F32)).astype(BF16)
    return hi, mid, lo


def _full_spec(shape):
    nd = len(shape)
    return pl.BlockSpec(shape, lambda *_: (0,) * nd)


def _inproj_kernel(x_ref, g_ref, b_ref, cs_ref, spread_ref, unit_ref,
                   wq_ref, wk_ref, wv_ref, wg_ref, wu_ref, wqm_ref, wm_ref,
                   q_ref, kc_ref, vc_ref, ks_ref, vs_ref, kw_ref, vw_ref,
                   gate_ref, u_ref, qm_ref, gm_ref):
    h = _layer_norm(x_ref[0], g_ref[...], b_ref[...])
    hb = h.astype(BF16)
    tab = sum(_dot(part, spread_ref[...]) for part in _split3(cs_ref[0]))
    cos_t = tab[:, 0:LANES] + unit_ref[...]
    sin_a = tab[:, LANES:2 * LANES]
    sin_b = tab[:, 2 * LANES:3 * LANES]

    def rope(t):
        return (t * cos_t + pltpu.roll(t, ROT_DIM // 2, 1) * sin_a
                + pltpu.roll(t, LANES - ROT_DIM // 2, 1) * sin_b)

    q = _dot(hb, wq_ref[...])
    for c in range(NSA_WIDTH // LANES):
        qc = rope(q[:, c * LANES:(c + 1) * LANES]) * Q_SCALE_LOG2
        for hh in range(2):
            q_ref[0, 2 * c + hh] = qc[:, hh * HEAD_DIM:(hh + 1) * HEAD_DIM].astype(BF16)
    k3 = _dot(hb, wk_ref[...])
    kc = rope(k3[:, 0:LANES])
    ks = rope(k3[:, LANES:2 * LANES])
    kw = rope(k3[:, 2 * LANES:3 * LANES])
    v3 = _dot(hb, wv_ref[...])
    kc_ref[0] = kc.astype(BF16)
    vc_ref[0] = v3[:, 0:LANES].astype(BF16)
    tm = x_ref.shape[1]
    pos = pl.program_id(1) * tm + lax.broadcasted_iota(I32, (tm, LANES), 0)
    blk_hot = jnp.where(lax.broadcasted_iota(I32, (tm, LANES), 1) == pos // SEL_BLOCK, 1.0, 0.0)
    lane_pad = jnp.zeros((tm, LANES - HEAD_DIM), F32)
    ones_pad = jnp.where(lax.broadcasted_iota(I32, (tm, LANES - HEAD_DIM), 1) == 0, 1.0, 0.0)
    for g in range(NSA_GROUPS):
        sl = slice(g * HEAD_DIM, (g + 1) * HEAD_DIM)
        ks_ref[0, g] = jnp.concatenate([blk_hot, ks[:, sl], lane_pad], axis=1).astype(BF16)
        kw_ref[0, g] = kw[:, sl].astype(BF16)
        vs_ref[0, g] = jnp.concatenate([v3[:, LANES:2 * LANES][:, sl], ones_pad], axis=1).astype(BF16)
        vw_ref[0, g] = jnp.concatenate([v3[:, 2 * LANES:3 * LANES][:, sl], ones_pad], axis=1).astype(BF16)
    gate_ref[0] = jax.nn.sigmoid(_dot(hb, wg_ref[...]))
    u_ref[0] = _dot(hb, wu_ref[...])
    qm_ref[0] = _dot(hb, wqm_ref[...]).astype(BF16)
    gm_ref[0] = jax.nn.sigmoid(_dot(hb, wm_ref[...]))


def _inproj(x, ln_g, ln_b, cos_sin, spread, unit, wq, wk, wv, wg, wu, wqm, wm):
    B, L, D = x.shape
    tm = INPROJ_TOKEN_TILE
    grid = (B, L // tm)
    tok = lambda w: pl.BlockSpec((1, tm, w), lambda b, i: (b, i, 0))
    head = lambda n, w=HEAD_DIM: pl.BlockSpec((1, n, tm, w), lambda b, i: (b, 0, i, 0))
    in_specs = [tok(D), _full_spec((1, D)), _full_spec((1, D)), tok(ROT_DIM), _full_spec(spread.shape),
                _full_spec(unit.shape)]
    in_specs += [pl.BlockSpec(w.shape, lambda b, i: (0, 0), pipeline_mode=pl.Buffered(1))
                 for w in (wq, wk, wv, wg, wu, wqm, wm)]
    sd = jax.ShapeDtypeStruct
    out_shape = [
        sd((B, NSA_HEADS, L, HEAD_DIM), BF16),
        sd((B, L, KV_WIDTH), BF16), sd((B, L, KV_WIDTH), BF16),
        sd((B, NSA_GROUPS, L, 2 * LANES), BF16), sd((B, NSA_GROUPS, L, LANES), BF16),
        sd((B, NSA_GROUPS, L, HEAD_DIM), BF16), sd((B, NSA_GROUPS, L, LANES), BF16),
        sd((B, L, NSA_GROUPS * LANES), F32),
        sd((B, L, S5_WIDTH), F32),
        sd((B, L, MEM_WIDTH), BF16),
        sd((B, L, N_BRANCH * D), F32),
    ]
    out_specs = [head(NSA_HEADS), tok(KV_WIDTH), tok(KV_WIDTH), head(NSA_GROUPS, 2 * LANES),
                 head(NSA_GROUPS, LANES), head(NSA_GROUPS), head(NSA_GROUPS, LANES),
                 tok(NSA_GROUPS * LANES), tok(S5_WIDTH),
                 tok(MEM_WIDTH), tok(N_BRANCH * D)]
    return pl.pallas_call(
        _inproj_kernel, grid=grid, in_specs=in_specs, out_specs=out_specs, out_shape=out_shape,
        compiler_params=_cparams("parallel", "parallel"), name="inproj",
    )(x, ln_g, ln_b, cos_sin, spread, unit, wq, wk, wv, wg, wu, wqm, wm)


def _compress_kernel(kc_ref, vc_ref, pek_ref, pev_ref, wk1f_ref, wv1f_ref,
                     wk1a_ref, wk1b_ref, wv1a_ref, wv1b_ref, wk2_ref, wv2_ref, ck_ref, cv_ref):
    n_chunk = kc_ref.shape[1]
    row = lax.broadcasted_iota(I32, (n_chunk, 1), 0)

    def one(x_ref, pe_ref, w1f_ref, w1a_ref, w1b_ref, w2_ref, o_ref):
        x = x_ref[0]
        first = _dot(x, w1a_ref[...])
        second = _dot(x, w1b_ref[...])
        second = pltpu.roll(second, n_chunk - 1, 0)
        pe_term = _dot(pe_ref[...], w1f_ref[...])[0:1]
        pe_term = jnp.concatenate([pe_term] * NSA_GROUPS, axis=1)
        hid = _gelu_tanh(first + second + pe_term).astype(BF16)
        for g in range(NSA_GROUPS):
            o = _dot(hid[:, g * CMP_HIDDEN:(g + 1) * CMP_HIDDEN], w2_ref[...])
            o_ref[0, g] = jnp.where(row < n_chunk - 1, o, 0.0).astype(BF16)

    one(kc_ref, pek_ref, wk1f_ref, wk1a_ref, wk1b_ref, wk2_ref, ck_ref)
    one(vc_ref, pev_ref, wv1f_ref, wv1a_ref, wv1b_ref, wv2_ref, cv_ref)


def _compress(kc_r, vc_r, pek, pev, wk1f, wv1f, wk1a, wk1b, wv1a, wv1b, wk2, wv2):
    B, n_chunk, width = kc_r.shape
    blk = pl.BlockSpec((1, n_chunk, width), lambda b: (b, 0, 0))
    out = pl.BlockSpec((1, NSA_GROUPS, n_chunk, HEAD_DIM), lambda b: (b, 0, 0, 0))
    ws = [pek, pev, wk1f, wv1f, wk1a, wk1b, wv1a, wv1b, wk2, wv2]
    sd = jax.ShapeDtypeStruct((B, NSA_GROUPS, n_chunk, HEAD_DIM), BF16)
    return pl.pallas_call(
        _compress_kernel, grid=(B,), in_specs=[blk, blk] + [_full_spec(w.shape) for w in ws],
        out_specs=[out, out], out_shape=[sd, sd], compiler_params=_cparams("parallel"), name="compress",
    )(kc_r, vc_r, *ws)


def _nsa_kernel(q_ref, ck_ref, cv_ref, ks_ref, vs_ref, kw_ref, vw_ref, gate_ref, wsc_ref, o_ref):
    seq_len = ks_ref.shape[2]
    n_cmp = ck_ref.shape[2]
    n_sb = seq_len // SEL_BLOCK
    n_sel = min(N_SEL, n_sb)
    rows = NSA_HPG * Q_BLOCK
    groups = range(NSA_GROUPS)
    q0 = pl.program_id(1) * Q_BLOCK
    t1 = q0 + lax.broadcasted_iota(I32, (Q_BLOCK, 1), 0)
    t4 = jnp.concatenate([t1] * NSA_HPG, axis=0)
    tk = SEL_KV_TILE

    def front(g):
        q = q_ref[0, g * NSA_HPG:(g + 1) * NSA_HPG].reshape(rows, HEAD_DIM)

        s = _dot_nt(q, ck_ref[0, g])
        c_end = lax.broadcasted_iota(I32, (1, n_cmp), 1) * CMP_STRIDE + (CMP_BLOCK - 1)
        e = _masked_exp2(s, c_end <= t4)
        p_cmp = e * _safe_recip(jnp.sum(e, axis=-1, keepdims=True))
        o_cmp = _dot(p_cmp.astype(BF16), cv_ref[0, g])

        imp = p_cmp[0:Q_BLOCK]
        for hh in range(1, NSA_HPG):
            imp = imp + p_cmp[hh * Q_BLOCK:(hh + 1) * Q_BLOCK]
        w_sc = wsc_ref[...]
        score = sum(_dot(part, w_sc) for part in _split3(imp))
        score_t = score.T
        jb = lax.broadcasted_iota(I32, (n_sb, Q_BLOCK), 0)
        tb = (q0 + lax.broadcasted_iota(I32, (1, Q_BLOCK), 1)) // SEL_BLOCK
        forced = (jb == 0) | (jb == tb) | (jb == tb - 1)
        work = jnp.where(forced | (jb > tb), -jnp.inf, score_t)
        bias_t = jnp.where(forced, 0.0, NEG_BIG)
        jbf = jb.astype(F32)
        for _ in range(n_sel - 3):
            m = jnp.max(work, axis=0, keepdims=True)
            idx = jnp.min(jnp.where(work == m, jbf, float(n_sb)), axis=0, keepdims=True)
            pick = jbf == idx
            bias_t = jnp.where(pick, 0.0, bias_t)
            work = jnp.where(pick, -jnp.inf, work)
        sel_bias = bias_t.T
        if n_sb < LANES:
            sel_bias = jnp.concatenate([sel_bias, jnp.zeros((Q_BLOCK, LANES - n_sb), F32)], axis=1)

        span = WINDOW + WINDOW_Q
        parts = []
        for sub in range(Q_BLOCK // WINDOW_Q):
            pick = lambda a: jnp.concatenate(
                [a[hh * Q_BLOCK + sub * WINDOW_Q:hh * Q_BLOCK + (sub + 1) * WINDOW_Q] for hh in range(NSA_HPG)],
                axis=0)
            w0 = pl.multiple_of(jnp.maximum(q0 + sub * WINDOW_Q - WINDOW, 0), WINDOW_Q)
            s = _dot_nt(pick(q), kw_ref[0, g, pl.ds(w0, span), :])
            diff = pick(t4) - (w0 + lax.broadcasted_iota(I32, (1, span), 1))
            e = _masked_exp2(s, (diff >= 0) & (diff < WINDOW))
            o = _dot(e.astype(BF16), vw_ref[0, g, pl.ds(w0, span), :])
            parts.append(o[:, :HEAD_DIM] * _safe_recip(o[:, HEAD_DIM:HEAD_DIM + 1]))
        o_win = jnp.concatenate([parts[sub][hh * WINDOW_Q:(hh + 1) * WINDOW_Q]
                                 for hh in range(NSA_HPG) for sub in range(len(parts))], axis=0)

        q_aug = jnp.concatenate([jnp.concatenate([sel_bias.astype(BF16)] * NSA_HPG, axis=0), q,
                                 jnp.zeros((rows, LANES - HEAD_DIM), BF16)], axis=1)
        return q_aug, o_cmp, o_win

    fronts = [front(g) for g in groups]

    def sel_tile(g, j, carry, causal):
        m_run, acc = carry
        k0 = pl.multiple_of(j * tk, tk)
        sc = _dot_nt(fronts[g][0], ks_ref[0, g, pl.ds(k0, tk), :])
        if causal:
            kpos = k0 + lax.broadcasted_iota(I32, (1, tk), 1)
            sc = jnp.where(kpos <= t4, sc, NEG_BIG)
        m_new = jnp.maximum(m_run, jnp.max(sc, axis=-1, keepdims=True))
        p = jnp.exp2(sc - m_new)
        acc_new = jnp.exp2(m_run - m_new) * acc + _dot(p.astype(BF16), vs_ref[0, g, pl.ds(k0, tk), :])
        return m_new, acc_new

    def sel_pair(jj, carries, causal):
        return tuple(sel_tile(g, 2 * jj + 1, sel_tile(g, 2 * jj, carries[g], causal), causal) for g in groups)

    init = tuple((jnp.full((rows, 1), NEG_BIG, F32), jnp.zeros((rows, LANES), F32)) for _ in groups)
    last_pair = (q0 // tk) // 2
    carries = lax.fori_loop(0, last_pair, functools.partial(sel_pair, causal=False), init)
    carries = sel_pair(last_pair, carries, True)

    outs = []
    for g in groups:
        _, o_cmp, o_win = fronts[g]
        acc = carries[g][1]
        o_sel = acc[:, :HEAD_DIM] * (1.0 / acc[:, HEAD_DIM:HEAD_DIM + 1])
        gt = gate_ref[0, :, g * LANES:(g + 1) * LANES]
        for hh in range(NSA_HPG):
            sl = slice(hh * Q_BLOCK, (hh + 1) * Q_BLOCK)
            c = hh * N_BRANCH
            outs.append(o_cmp[sl] * gt[:, c:c + 1] + o_sel[sl] * gt[:, c + 1:c + 2]
                        + o_win[sl] * gt[:, c + 2:c + 3])
    o_ref[0] = jnp.concatenate(outs, axis=1).astype(BF16)


def _nsa(q_hm, ck, cv, ks, vs, kw, vw, gates, w_score):
    B, _, L, _ = q_hm.shape
    assert L // SEL_BLOCK <= LANES and (L // SEL_KV_TILE) % 2 == 0 and L >= WINDOW + Q_BLOCK
    n_cmp = ck.shape[2]
    grid = (B, L // Q_BLOCK)
    qspec = pl.BlockSpec((1, NSA_HEADS, Q_BLOCK, HEAD_DIM), lambda b, i: (b, 0, i, 0))
    cspec = pl.BlockSpec((1, NSA_GROUPS, n_cmp, HEAD_DIM), lambda b, i: (b, 0, 0, 0))
    kvspec = lambda w: pl.BlockSpec((1, NSA_GROUPS, L, w), lambda b, i: (b, 0, 0, 0),
                                    pipeline_mode=pl.Buffered(1))
    gspec = pl.BlockSpec((1, Q_BLOCK, NSA_GROUPS * LANES), lambda b, i: (b, i, 0))
    ospec = pl.BlockSpec((1, Q_BLOCK, NSA_WIDTH), lambda b, i: (b, i, 0))
    return pl.pallas_call(
        _nsa_kernel, grid=grid,
        in_specs=[qspec, cspec, cspec, kvspec(2 * LANES), kvspec(LANES), kvspec(HEAD_DIM), kvspec(LANES), gspec,
                  _full_spec(w_score.shape)],
        out_specs=ospec, out_shape=jax.ShapeDtypeStruct((B, L, NSA_WIDTH), BF16),
        compiler_params=_cparams("parallel", "arbitrary"), name="nsa",
    )(q_hm, ck, cv, ks, vs, kw, vw, gates, w_score)


def _s5_kernel(u_ref, wb_ref, wc_ref, are_ref, aim_ref, d_ref, y_ref, sre_ref, sim_ref, carry_ref):
    n_b, chunk, _ = u_ref.shape
    n_tile = wb_ref.shape[0]
    in_per = n_tile // (S5_WIDTH // LANES)
    pitch = S5_PITCH

    @pl.when(pl.program_id(0) == 0)
    def _():
        carry_ref[...] = jnp.zeros_like(carry_ref)

    for b in range(n_b):
        for c in range(n_tile):
            i = c // in_per
            ub = u_ref[b, :, i * LANES:(i + 1) * LANES].astype(BF16)
            r = _dot(ub, wb_ref[c])
            sre_ref[b, c * pitch:c * pitch + chunk, :] = r[:, :LANES]
            sim_ref[b, c * pitch:c * pitch + chunk, :] = r[:, LANES:]

    a_re, a_im = are_ref[...], aim_ref[...]

    def step(t, carry):
        out = []
        for b in range(n_b):
            s_re, s_im = carry[2 * b], carry[2 * b + 1]
            rows = pl.ds(t, n_tile, stride=pitch)
            n_re = a_re * s_re - a_im * s_im + sre_ref[b, rows, :]
            n_im = a_re * s_im + a_im * s_re + sim_ref[b, rows, :]
            sre_ref[b, rows, :] = n_re
            sim_ref[b, rows, :] = n_im
            out += [n_re, n_im]
        return tuple(out)

    init = tuple(carry_ref[i] for i in range(2 * n_b))
    fin = lax.fori_loop(0, chunk, step, init, unroll=8)
    for i in range(2 * n_b):
        carry_ref[i] = fin[i]

    for b in range(n_b):
        for o in range(S5_WIDTH // LANES):
            acc = jnp.zeros((chunk, LANES), F32)
            for c in range(o * in_per, (o + 1) * in_per):
                rows = slice(c * pitch, c * pitch + chunk)
                state = jnp.concatenate([sre_ref[b, rows, :], sim_ref[b, rows, :]], axis=1).astype(BF16)
                acc = acc + _dot(state, wc_ref[c])
            lanes = slice(o * LANES, (o + 1) * LANES)
            y = acc + d_ref[:, lanes] * u_ref[b, :, lanes]
            y_ref[b, :, lanes] = _gelu_tanh(y).astype(BF16)


def _s5(u, wb, wc, a_re, a_im, d_skip):
    B, L, W = u.shape
    chunk = S5_CHUNK
    n_tile = wb.shape[0]
    blk = pl.BlockSpec((B, chunk, W), lambda i: (0, i, 0))
    slab = pltpu.VMEM((B, n_tile * S5_PITCH, LANES), F32)
    return pl.pallas_call(
        _s5_kernel, grid=(L // chunk,),
        in_specs=[blk] + [_full_spec(w.shape) for w in (wb, wc, a_re, a_im, d_skip)],
        out_specs=blk, out_shape=jax.ShapeDtypeStruct((B, L, W), BF16),
        scratch_shapes=[slab, slab, pltpu.VMEM((2 * B, n_tile, LANES), F32)],
        compiler_params=_cparams("arbitrary"), name="s5",
    )(u, wb, wc, a_re, a_im, d_skip)


def _memkv_kernel(mem_ref, w_ref, k_ref, v_ref):
    kv = _dot(mem_ref[0].astype(BF16), w_ref[...])
    k_ref[0] = kv[:, :MEM_WIDTH].astype(BF16)
    v_ref[0] = kv[:, MEM_WIDTH:].astype(BF16)


def _memkv(mem, w_kv):
    B, M, D = mem.shape
    out = pl.BlockSpec((1, M, MEM_WIDTH), lambda b: (b, 0, 0))
    sd = jax.ShapeDtypeStruct((B, M, MEM_WIDTH), BF16)
    return pl.pallas_call(
        _memkv_kernel, grid=(B,),
        in_specs=[pl.BlockSpec((1, M, D), lambda b: (b, 0, 0)), _full_spec(w_kv.shape)],
        out_specs=[out, out], out_shape=[sd, sd], compiler_params=_cparams("parallel"), name="memkv",
    )(mem, w_kv)


def _memory_attention(q_ref, k_ref, v_ref):
    outs = []
    for h in range(MEM_HEADS):
        sl = slice(h * MEM_HEAD_DIM, (h + 1) * MEM_HEAD_DIM)
        s = _dot_nt(q_ref[:, sl], k_ref[0, :, sl]) * (MEM_HEAD_DIM ** -0.5)
        m = jnp.max(s, axis=-1, keepdims=True)
        e = jnp.exp(s - m)
        p = e / jnp.sum(e, axis=-1, keepdims=True)
        outs.append(_dot(p.astype(BF16), v_ref[0, :, sl]))
    return jnp.concatenate(outs, axis=1).astype(BF16)


def _merge_kernel(x_ref, lng_ref, lnb_ref, on_ref, gy_ref, qm_ref, km_ref, vm_ref, gm_ref,
                  wn_ref, wglu_ref, wmo_ref, wo_ref, l1g_ref, l1b_ref,
                  wrh_ref, wrp_ref, br_ref, tri_ref, striu_ref,
                  h1_ref, lp_ref, w4_ref, cnt_ref):
    D = x_ref.shape[1]
    tm = x_ref.shape[0]
    h =_layer_norm(x_ref[...], lng_ref[...], lnb_ref[...])
    y_nsa = _dot(on_ref[...], wn_ref[...])
    glu = _dot(gy_ref[...], wglu_ref[...])
    y_s5 = glu[:, :D] * jax.nn.sigmoid(glu[:, D:])
    y_mem = _dot(_memory_attention(qm_ref, km_ref, vm_ref), wmo_ref[...])
    merged = gm_ref[:, 0:D] * y_nsa + gm_ref[:, D:2 * D] * y_s5 + gm_ref[:, 2 * D:3 * D] * y_mem
    mix = _dot(merged.astype(BF16), wo_ref[...])
    h1 = _layer_norm(DEEPNORM_ALPHA * h + mix, l1g_ref[...], l1b_ref[...])
    h1_ref[...] = h1

    hh = h1.astype(BF16)
    hl = (h1 - hh.astype(F32)).astype(BF16)
    both = _dot(hh, wrp_ref[...])
    logits = both[:, :LANES] + both[:, LANES:] + _dot(hl, wrh_ref[...]) + br_ref[...]
    lane = lax.broadcasted_iota(I32, (tm, LANES), 1)
    lane_f = lane.astype(F32)
    work = logits
    multi = jnp.zeros((tm, LANES), F32)
    vals, picks = [], []
    for _ in range(TOP_K):
        m = jnp.max(work, axis=-1, keepdims=True)
        idx = jnp.min(jnp.where(work == m, lane_f, float(LANES)), axis=-1, keepdims=True)
        pick = lane_f == idx
        vals.append(m)
        picks.append((pick, idx))
        multi = jnp.where(pick, 1.0, multi)
        work = jnp.where(pick, -jnp.inf, work)
    es = [jnp.exp(v - vals[0]) for v in vals]
    den = es[0] + es[1] + es[2] + es[3]
    st = TOKEN_TILE
    pos = []
    for t in range(tm // st):
        multi_t = multi[t * st:(t + 1) * st]
        cnt = jnp.broadcast_to(jnp.sum(multi_t, axis=0, keepdims=True), (SUBLANES, LANES))
        cnt_ref[t] = cnt
        lower = _dot(cnt.astype(BF16), striu_ref[...])[0:1]
        pos.append(lower + _dot(tri_ref[...], multi_t.astype(BF16)))
    pos = jnp.concatenate(pos, axis=0)
    lp = jnp.full((tm, LANES), -1.0, F32)
    w4 = jnp.zeros((tm, LANES), F32)
    for k in range(TOP_K):
        pick, _ = picks[k]
        lp = jnp.where(lane == k, jnp.sum(jnp.where(pick, pos, 0.0), axis=-1, keepdims=True), lp)
        w4 = jnp.where(lane == k, es[k] / den, w4)
    lp_ref[...] = lp
    w4_ref[...] = w4


def _merge(x2, lng, lnb, o_nsa, gy, qm, k_mem, v_mem, gm, wn, wglu, wmo, wo, l1g, l1b, wrh, wrp, br, tri, striu):
    T, D = x2.shape
    tm = MERGE_SORT_TILES * TOKEN_TILE
    tok = lambda w: pl.BlockSpec((tm, w), lambda i: (i, 0))
    steps_per_batch = T // k_mem.shape[0] // tm
    mem_kv = pl.BlockSpec((1,) + k_mem.shape[1:], lambda i: (i // steps_per_batch, 0, 0))
    ws = [wn, wglu, wmo, wo, l1g, l1b, wrh, wrp, br, tri, striu]
    sd = jax.ShapeDtypeStruct
    lane_out = sd((T, LANES), F32)
    return pl.pallas_call(
        _merge_kernel, grid=(T // tm,),
        in_specs=[tok(D), _full_spec((1, D)), _full_spec((1, D)), tok(NSA_WIDTH), tok(S5_WIDTH),
                  tok(MEM_WIDTH), mem_kv, mem_kv, tok(N_BRANCH * D)] + [_full_spec(w.shape) for w in ws],
        out_specs=[tok(D), tok(LANES), tok(LANES),
                   pl.BlockSpec((MERGE_SORT_TILES, SUBLANES, LANES), lambda i: (i, 0, 0))],
        out_shape=[sd((T, D), F32), lane_out, lane_out, sd((T // TOKEN_TILE, SUBLANES, LANES), F32)],
        compiler_params=_cparams("parallel"), name="merge",
    )(x2, lng, lnb, o_nsa, gy, qm, k_mem, v_mem, gm, *ws)


def _slots_kernel(cnt_ref, triu_ref, striu_ref, tril_ref, seg_ref, blk_ref, misc_ref):
    n_blk = blk_ref.shape[0]
    cnt = cnt_ref[...]
    cnt_b = cnt.astype(BF16)
    total = jnp.sum(cnt, axis=0, keepdims=True)
    nblk_e = jnp.floor((total + (MOE_ROWS - 1)) * (1.0 / MOE_ROWS))
    nblk_8 = jnp.broadcast_to(nblk_e, (SUBLANES, LANES))
    end_b = _dot(nblk_8.astype(BF16), triu_ref[...])
    start_rows = (end_b - nblk_8)[0:1] * MOE_ROWS
    dst = start_rows + _dot(tril_ref[...], cnt_b)
    off = _dot(cnt_b, striu_ref[...])
    seg_ref[0] = cnt.astype(I32)
    seg_ref[1] = off.astype(I32)
    seg_ref[2] = dst.astype(I32)
    blk_i = lax.broadcasted_iota(I32, (n_blk, LANES), 0).astype(F32)
    lane_b = lax.broadcasted_iota(I32, (n_blk, LANES), 1)
    ended = jnp.where((end_b[0:1] <= blk_i) & (lane_b < N_EXPERTS), 1.0, 0.0)
    owner = jnp.minimum(jnp.sum(ended, axis=-1, keepdims=True), float(N_EXPERTS - 1))
    blk_ref[...] = jnp.broadcast_to(owner, (n_blk, LANES)).astype(I32)
    cand = lax.broadcasted_iota(I32, (LANES, LANES), 0)
    has_blocks = jnp.broadcast_to(nblk_e, (LANES, LANES)).T > 0.0
    later = (cand > lax.broadcasted_iota(I32, (LANES, LANES), 1)) & has_blocks
    nxt = jnp.min(jnp.where(later, cand.astype(F32), float(LANES)), axis=0, keepdims=True)
    nxt = jnp.where(nxt < float(LANES), nxt, -1.0)
    lane8 = lax.broadcasted_iota(I32, (SUBLANES, LANES), 1)
    row8 = lax.broadcasted_iota(I32, (SUBLANES, LANES), 0)
    used = jnp.sum(jnp.where(lane8 == N_EXPERTS - 1, end_b, 0.0), axis=-1, keepdims=True)
    misc = jnp.where(row8 == 0, used,
                     jnp.where(row8 == 1, start_rows + total,
                               jnp.where(row8 == 2, nblk_e * MOE_ROWS - total, nxt)))
    misc_ref[...] = misc.astype(I32)


def _slots(cnt, triu, striu, tril, n_blk):
    n_tile = cnt.shape[0]
    sd = jax.ShapeDtypeStruct
    return pl.pallas_call(
        _slots_kernel, grid=(1,),
        in_specs=[_full_spec(cnt.shape), _full_spec(triu.shape), _full_spec(striu.shape), _full_spec(tril.shape)],
        out_specs=[_full_spec((3, n_tile, LANES)), _full_spec((n_blk, LANES)), _full_spec((SUBLANES, LANES))],
        out_shape=[sd((3, n_tile, LANES), I32), sd((n_blk, LANES), I32), sd((SUBLANES, LANES), I32)],
        compiler_params=_cparams("arbitrary"), name="slots",
    )(cnt, triu, striu, tril)


ROW_TILES = D_MODEL // LANES


def _row_span(row, n_rows):
    start = row * ROW_TILES
    if not isinstance(start, int):
        start = pl.multiple_of(start, ROW_TILES)
    return pl.ds(start, n_rows * ROW_TILES)


def _store_rows(ref, val):
    for c in range(ROW_TILES):
        ref[pl.ds(c, val.shape[0], stride=ROW_TILES), :] = val[:, c * LANES:(c + 1) * LANES]


def _load_row_tile(ref, n_rows, c):
    return ref[pl.ds(c, n_rows, stride=ROW_TILES), :]


BIG_PIECE_ROWS = 64


def _pieces(count, max_rows, fn):
    def run(sizes):
        for p in sizes:
            def piece(p=p):
                fn(count & (-2 * p), p)
            pl.when((count & p) != 0)(piece)

    sizes = [max_rows >> s for s in range(max_rows.bit_length())]
    big = [p for p in sizes if p >= BIG_PIECE_ROWS]
    if big:
        pl.when(count >= BIG_PIECE_ROWS)(lambda: run(big))
    run([p for p in sizes if p < BIG_PIECE_ROWS])


def _start_segment_copies(seg_ref, max_rows, make_copy):
    def per_expert(e, c):
        cnt, off, dst = seg_ref[0, 0, 0, e], seg_ref[1, 0, 0, e], seg_ref[2, 0, 0, e]
        _pieces(cnt, max_rows, lambda first, rows: make_copy(off + first, dst + first, rows).start())
        return c

    lax.fori_loop(0, N_EXPERTS, per_expert, 0)


def _dispatch_kernel(seg_ref, misc_ref, lp_ref, h_ref, xs_ref, sorted_ref, zero_ref, sem, pad_sem):
    i = pl.program_id(0)
    n = pl.num_programs(0)
    tm, D = h_ref.shape
    rows = TOP_K * tm
    slot = lax.rem(i, 2)

    def row_copy(slot_):
        def make(src_row, dst_row, n_rows):
            return pltpu.make_async_copy(sorted_ref.at[slot_, _row_span(src_row, n_rows)],
                                         xs_ref.at[_row_span(dst_row, n_rows)], sem.at[slot_])
        return make

    @pl.when(i == 0)
    def _():
        zero_ref[...] = jnp.zeros_like(zero_ref)
        for wait in (False, True):
            def per_expert(e, c, wait=wait):
                def one(first, n_rows):
                    cp = pltpu.make_async_copy(zero_ref.at[_row_span(0, n_rows)],
                                               xs_ref.at[_row_span(misc_ref[1, e] + first, n_rows)], pad_sem)
                    cp.wait() if wait else cp.start()
                _pieces(misc_ref[2, e], MOE_ROWS // 2, one)
                return c
            lax.fori_loop(0, N_EXPERTS, per_expert, 0)

            def per_spare_half_block(hb, c, wait=wait):
                cp = pltpu.make_async_copy(zero_ref, xs_ref.at[_row_span(hb * (MOE_ROWS // 2), MOE_ROWS // 2)],
                                           pad_sem)
                cp.wait() if wait else cp.start()
                return c
            lax.fori_loop(2 * misc_ref[0, 0], 2 * (xs_ref.shape[0] // (MOE_ROWS * ROW_TILES)),
                          per_spare_half_block, 0)

    lp_t = lp_ref[...].T
    s_ix = lax.broadcasted_iota(I32, (rows, 1), 0).astype(F32)
    hit = s_ix == lp_t[0:1, :]
    for k in range(1, TOP_K):
        hit = hit | (s_ix == lp_t[k:k + 1, :])
    perm = jnp.where(hit, 1.0, 0.0).astype(BF16)
    _store_rows(sorted_ref.at[slot], _dot(perm, h_ref[...].astype(BF16)))

    _start_segment_copies(seg_ref, tm, row_copy(slot))

    @pl.when(i > 0)
    def _():
        row_copy(1 - slot)(0, 0, rows).wait()

    @pl.when(i == n - 1)
    def _():
        row_copy(slot)(0, 0, rows).wait()


def _seg_spec(index_map):
    return pl.BlockSpec((3, 1, 1, LANES), index_map, memory_space=pltpu.SMEM)


def _dispatch(seg4, misc, lp, h1, cap):
    T, D = h1.shape
    assert D == ROW_TILES * LANES
    tm = TOKEN_TILE
    tok = lambda w: pl.BlockSpec((tm, w), lambda i: (i, 0))
    return pl.pallas_call(
        _dispatch_kernel, grid=(T // tm,),
        in_specs=[_seg_spec(lambda i: (0, i, 0, 0)), pl.BlockSpec(memory_space=pltpu.SMEM), tok(LANES), tok(D)],
        out_specs=pl.BlockSpec(memory_space=pl.ANY),
        out_shape=jax.ShapeDtypeStruct((cap * ROW_TILES, LANES), F32),
        scratch_shapes=[pltpu.VMEM((2, TOP_K * tm * ROW_TILES, LANES), F32),
                        pltpu.VMEM((MOE_ROWS // 2 * ROW_TILES, LANES), F32),
                        pltpu.SemaphoreType.DMA((2,)), pltpu.SemaphoreType.DMA(())],
        compiler_params=_cparams("arbitrary"), name="dispatch",
    )(seg4, misc, lp, h1)


def _expert_kernel(blk_ref, used_ref, next_ref, xs_ref, wgu_hbm, bgu_ref, wd_hbm, bd_ref, ys_ref,
                   wgu_f32, wd_f32, wgu_bf, wd_bf, sem, run_ref):
    i = pl.program_id(0)
    live = i < used_ref[0]
    expert = blk_ref[i]

    def weight_copies(e, slot):
        return (pltpu.make_async_copy(wgu_hbm.at[e], wgu_f32.at[slot], sem.at[0, slot]),
                pltpu.make_async_copy(wd_hbm.at[e], wd_f32.at[slot], sem.at[1, slot]))

    @pl.when(i == 0)
    def _():
        run_ref[0] = 0
        for cp in weight_copies(expert, 0):
            cp.start()

    @pl.when(live & ((i == 0) | (expert != blk_ref[jnp.maximum(i - 1, 0)])))
    def _():
        slot = lax.rem(run_ref[0], 2)
        for cp in weight_copies(expert, slot):
            cp.wait()
        wgu_bf[...] = wgu_f32[slot].astype(BF16)
        wd_bf[...] = wd_f32[slot].astype(BF16)
        nxt = next_ref[expert]

        @pl.when(nxt >= 0)
        def _():
            for cp in weight_copies(nxt, 1 - slot):
                cp.start()

        run_ref[0] = run_ref[0] + 1

    @pl.when(live)
    def _():
        xb = jnp.concatenate([_load_row_tile(xs_ref, MOE_ROWS, c).astype(BF16) for c in range(ROW_TILES)],
                             axis=1)
        gu = _dot(xb, wgu_bf[...]) + bgu_ref[0]
        g = jnp.minimum(gu[:, :D_FF], SWIGLU_LIMIT)
        lin = jnp.clip(gu[:, D_FF:], -SWIGLU_LIMIT, SWIGLU_LIMIT)
        act = g * jax.nn.sigmoid(SWIGLU_ALPHA * g) * (lin + 1.0)
        _store_rows(ys_ref, _dot(act.astype(BF16), wd_bf[...]) + bd_ref[0])

    @pl.when(pl.program_id(0) >= used_ref[0])
    def _():
        ys_ref[...] = jnp.zeros_like(ys_ref)


def _experts(blk_expert, n_used, next_expert, xs, w_gate_up, b_gate_up, w_down, b_down):
    D = w_down.shape[2]
    n_blk = xs.shape[0] // (MOE_ROWS * ROW_TILES)
    E = w_gate_up.shape[0]
    live = lambda i, used: jnp.minimum(i, used[0] - 1)
    row = pl.BlockSpec((MOE_ROWS * ROW_TILES, LANES), lambda i, blk, used, nxt: (live(i, used), 0))
    by_e = lambda shape: pl.BlockSpec((1,) + shape, lambda i, blk, used, nxt: (blk[live(i, used)], 0, 0))
    in_hbm = pl.BlockSpec(memory_space=pl.ANY)
    grid_spec = pltpu.PrefetchScalarGridSpec(
        num_scalar_prefetch=3, grid=(n_blk,),
        in_specs=[row, in_hbm, by_e((1, 2 * D_FF)), in_hbm, by_e((1, D))],
        out_specs=pl.BlockSpec((MOE_ROWS * ROW_TILES, LANES), lambda i, blk, used, nxt: (i, 0)),
        scratch_shapes=[pltpu.VMEM((2, D, 2 * D_FF), F32), pltpu.VMEM((2, D_FF, D), F32),
                        pltpu.VMEM((D, 2 * D_FF), BF16), pltpu.VMEM((D_FF, D), BF16),
                        pltpu.SemaphoreType.DMA((2, 2)), pltpu.SMEM((1,), I32)])
    return pl.pallas_call(
        _expert_kernel, grid_spec=grid_spec, out_shape=jax.ShapeDtypeStruct(xs.shape, F32),
        compiler_params=_cparams("arbitrary"), name="experts",
    )(blk_expert, n_used, next_expert, xs, w_gate_up, b_gate_up.reshape(E, 1, 2 * D_FF), w_down,
      b_down.reshape(E, 1, D))


def _combine_kernel(seg_ref, segn_ref, lp_ref, w4_ref, h1_ref, g_ref, b_ref, ys_ref, o_ref, buf_ref, sem):
    i = pl.program_id(0)
    n = pl.num_programs(0)
    tm = h1_ref.shape[0]
    rows = TOP_K * tm
    slot = lax.rem(i, 2)

    def row_copy(slot_):
        def make(buf_row, ys_row, n_rows):
            return pltpu.make_async_copy(ys_ref.at[_row_span(ys_row, n_rows)],
                                         buf_ref.at[slot_, _row_span(buf_row, n_rows)], sem.at[slot_])
        return make

    @pl.when(i == 0)
    def _():
        _start_segment_copies(seg_ref, tm, row_copy(slot))

    @pl.when(i + 1 < n)
    def _():
        _start_segment_copies(segn_ref, tm, row_copy(1 - slot))

    row_copy(slot)(0, 0, rows).wait()

    lp = lp_ref[...]
    s_ix = lax.broadcasted_iota(I32, (1, rows), 1).astype(F32)
    wmat = jnp.zeros((tm, rows), F32)
    for k in range(TOP_K):
        wmat = jnp.where(s_ix == lp[:, k:k + 1], w4_ref[:, k:k + 1], wmat)
    w_hi = wmat.astype(BF16)
    w_lo = (wmat - w_hi.astype(F32)).astype(BF16)
    y = jnp.concatenate([_load_row_tile(buf_ref.at[slot], rows, c) for c in range(ROW_TILES)], axis=1)
    y_hi = y.astype(BF16)
    y_lo = (y - y_hi.astype(F32)).astype(BF16)
    acc = DEEPNORM_ALPHA * h1_ref[...] + (_dot(w_hi, y_hi) + _dot(w_hi, y_lo) + _dot(w_lo, y_hi))
    o_ref[...] = _layer_norm(acc, g_ref[...], b_ref[...])


def _combine(seg4, lp, w4, h1, ln_g, ln_b, ys):
    T, D = h1.shape
    tm = TOKEN_TILE
    n_tile = T // tm
    return pl.pallas_call(
        _combine_kernel, grid=(n_tile,),
        in_specs=[_seg_spec(lambda i: (0, i, 0, 0)),
                  _seg_spec(lambda i: (0, jnp.minimum(i + 1, n_tile - 1), 0, 0)),
                  pl.BlockSpec((tm, LANES), lambda i: (i, 0)),
                  pl.BlockSpec((tm, LANES), lambda i: (i, 0)),
                  pl.BlockSpec((tm, D), lambda i: (i, 0)),
                  _full_spec((1, D)), _full_spec((1, D)),
                  pl.BlockSpec(memory_space=pl.ANY)],
        out_specs=pl.BlockSpec((tm, D), lambda i: (i, 0)),
        out_shape=jax.ShapeDtypeStruct((T, D), F32),
        scratch_shapes=[pltpu.VMEM((2, TOP_K * tm * ROW_TILES, LANES), F32), pltpu.SemaphoreType.DMA((2,))],
        compiler_params=_cparams("arbitrary"), name="combine",
    )(seg4, seg4, lp, w4, h1, ln_g, ln_b, ys)


def _rope_tables(positions):
    inv = ROPE_THETA ** (-jnp.arange(0, ROT_DIM, 2, dtype=F32) / ROT_DIM)
    ang = positions.astype(F32)[..., None] * inv
    cos_sin = jnp.concatenate([jnp.cos(ang), jnp.sin(ang)], axis=-1)
    half = ROT_DIM // 2
    spread = np.zeros((ROT_DIM, 3 * LANES), np.float32)
    unit = np.ones((1, LANES), np.float32)
    for lane in range(LANES):
        d = lane % HEAD_DIM
        if d < half:
            spread[d, lane] = 1.0
            spread[half + d, 2 * LANES + lane] = -1.0
            unit[0, lane] = 0.0
        elif d < ROT_DIM:
            spread[d - half, lane] = 1.0
            spread[d, LANES + lane] = 1.0
            unit[0, lane] = 0.0
    return cos_sin, jnp.asarray(spread, BF16), jnp.asarray(unit)


def _split_w_in(w_in):
    widths = (NSA_WIDTH,) + (KV_WIDTH,) * 6 + (NSA_HEADS * N_BRANCH, S5_WIDTH, MEM_WIDTH, N_BRANCH * D_MODEL)
    offs = [0]
    for w in widths:
        offs.append(offs[-1] + w)
    col = lambda i: w_in[:, offs[i]:offs[i + 1]]
    wq, kc, vc, ks, vs, kw, vw, wg, wu, wqm, wm = (col(i) for i in range(11))
    wk = jnp.concatenate([kc, ks, kw], axis=1)
    wv = jnp.concatenate([vc, vs, vw], axis=1)
    per_group = NSA_HPG * N_BRANCH
    wg_pad = jnp.zeros((w_in.shape[0], NSA_GROUPS * LANES), w_in.dtype)
    for g in range(NSA_GROUPS):
        wg_pad = wg_pad.at[:, g * LANES:g * LANES + per_group].set(wg[:, g * per_group:(g + 1) * per_group])
    return tuple(w.astype(BF16) for w in (wq, wk, wv, wg_pad, wu, wqm, wm))


def _compress_weights(w1):
    half = CMP_BLOCK // 2
    eye = np.eye(NSA_GROUPS, dtype=np.float32)

    def arrange(w_half):
        full = jnp.einsum('sdf,gh->sgdhf', w_half, eye)
        return full.reshape(half * NSA_GROUPS * HEAD_DIM, NSA_GROUPS * CMP_HIDDEN).astype(BF16)

    return (w1.reshape(CMP_BLOCK * HEAD_DIM, CMP_HIDDEN).astype(BF16), arrange(w1[:half]), arrange(w1[half:]))


def _s5_weights(a_re, a_im, log_dt, b_re, b_im, c_re, c_im):
    step = jnp.exp(log_dt)[:, None]
    mag = jnp.exp(a_re * step)
    ab_re, ab_im = mag * jnp.cos(a_im * step), mag * jnp.sin(a_im * step)
    den = a_re * a_re + a_im * a_im
    nr = ab_re - 1.0
    coef_re = (nr * a_re + ab_im * a_im) / den
    coef_im = (ab_im * a_re - nr * a_im) / den
    bb_re = coef_re[..., None] * b_re - coef_im[..., None] * b_im
    bb_im = coef_re[..., None] * b_im + coef_im[..., None] * b_re
    n_tile = S5_GROUPS * S5_STATE // LANES
    tile_groups = LANES // S5_STATE
    lane_groups = LANES // S5_GROUP_DIM
    tiles_per_lane_tile = lane_groups // tile_groups
    place = np.zeros((n_tile, lane_groups, tile_groups), np.float32)
    for c in range(n_tile):
        for j in range(tile_groups):
            place[c, (c % tiles_per_lane_tile) * tile_groups + j, j] = 1.0

    def in_blocks(bb):
        pairs = bb.reshape(n_tile, tile_groups, S5_STATE, S5_GROUP_DIM)
        return jnp.einsum('cjnp,caj->capjn', pairs, place).reshape(n_tile, LANES, LANES)

    def out_blocks(c):
        pairs = c.reshape(n_tile, tile_groups, S5_GROUP_DIM, S5_STATE)
        return jnp.einsum('cjpn,caj->cjnap', pairs, place).reshape(n_tile, LANES, LANES)

    wb = jnp.concatenate([in_blocks(bb_re), in_blocks(bb_im)], axis=2).astype(BF16)
    wc = jnp.concatenate([out_blocks(c_re), out_blocks(-c_im)], axis=1).astype(BF16)
    return wb, wc, ab_re.reshape(n_tile, LANES), ab_im.reshape(n_tile, LANES)


def _layer(x, mem, positions, ln_emb_g, ln_emb_b, w_in, pe_k, pe_v, w_kcmp1, w_kcmp2, w_vcmp1, w_vcmp2,
           s5_a_re, s5_a_im, s5_log_dt, s5_b_re, s5_b_im, s5_c_re, s5_c_im, s5_d,
           w_s5_glu, w_mem_kv, w_nsa_out, w_mem_out, w_o, ln1_g, ln1_b, w_router, b_router,
           w_gate_up, b_gate_up, w_down, b_down, ln2_g, ln2_b):
    B, L, D = x.shape
    T = B * L
    row = lambda v: v.reshape(1, -1)

    cos_sin, spread, unit = _rope_tables(positions)
    (q_hm, kc, vc, ks, vs, kw, vw, gates, u, qm, gm) = _inproj(
        x, row(ln_emb_g), row(ln_emb_b), cos_sin, spread, unit, *_split_w_in(w_in))

    n_chunk = L // CMP_STRIDE
    chunked = lambda t: t.reshape(B, n_chunk, CMP_STRIDE * KV_WIDTH)
    pe_rows = lambda pe: jnp.broadcast_to(pe.reshape(1, -1), (SUBLANES, CMP_BLOCK * HEAD_DIM)).astype(BF16)
    wk1f, wk1a, wk1b = _compress_weights(w_kcmp1)
    wv1f, wv1a, wv1b = _compress_weights(w_vcmp1)
    ck, cv = _compress(chunked(kc), chunked(vc), pe_rows(pe_k), pe_rows(pe_v), wk1f, wv1f,
                       wk1a, wk1b, wv1a, wv1b, w_kcmp2.astype(BF16), w_vcmp2.astype(BF16))

    per_sb = SEL_BLOCK // CMP_STRIDE
    c_ix = np.arange(n_chunk)[:, None]
    n_ix = np.arange(L // SEL_BLOCK)[None, :]
    w_score = jnp.asarray((c_ix // per_sb == n_ix).astype(np.float32)
                          + ((c_ix + 1) // per_sb == n_ix).astype(np.float32), BF16)
    o_nsa = _nsa(q_hm, ck, cv, ks, vs, kw, vw, gates, w_score)

    wb, wc, a_re, a_im = _s5_weights(s5_a_re, s5_a_im, s5_log_dt, s5_b_re, s5_b_im, s5_c_re, s5_c_im)
    gy = _s5(u, wb, wc, a_re, a_im, row(s5_d))

    k_mem, v_mem = _memkv(mem, w_mem_kv.astype(BF16))

    pad_e = LANES - N_EXPERTS
    wr = jnp.pad(w_router, ((0, 0), (0, pad_e)))
    wr_hi = wr.astype(BF16)
    wr_pair = jnp.concatenate([wr_hi, (wr - wr_hi.astype(F32)).astype(BF16)], axis=1)
    br = jnp.concatenate([b_router, jnp.full((pad_e,), -jnp.inf, F32)]).reshape(1, LANES)
    tm = TOKEN_TILE
    n_tile = T // tm
    strict_lower = lambda n: jnp.asarray(np.tril(np.ones((n, n), np.float32), -1), BF16)
    triu = jnp.asarray(np.triu(np.ones((LANES, LANES), np.float32)), BF16)
    striu = jnp.asarray(np.triu(np.ones((LANES, LANES), np.float32), 1), BF16)
    flat = lambda t: t.reshape(T, t.shape[-1])
    h1, lp, w4, cnt = _merge(
        flat(x), row(ln_emb_g), row(ln_emb_b), flat(o_nsa), flat(gy), flat(qm), k_mem, v_mem, flat(gm),
        w_nsa_out.astype(BF16), w_s5_glu.astype(BF16), w_mem_out.astype(BF16), w_o.astype(BF16),
        row(ln1_g), row(ln1_b), wr_hi, wr_pair, br, strict_lower(tm), striu)

    cap = (T * TOP_K + MOE_ROWS - 1) // MOE_ROWS * MOE_ROWS + N_EXPERTS * MOE_ROWS
    n_blk = cap // MOE_ROWS
    seg, blk_owner, misc = _slots(cnt[:, 0, :], triu, striu, strict_lower(n_tile), n_blk)
    seg4 = seg.reshape(3, n_tile, 1, LANES)
    blk_expert = blk_owner[:, 0]
    n_used = misc[0, :1]

    xs = _dispatch(seg4, misc, lp, h1, cap)
    ys = _experts(blk_expert, n_used, misc[3, :N_EXPERTS], xs, w_gate_up, b_gate_up, w_down, b_down)
    out = _combine(seg4, lp, w4, h1, row(ln2_g), row(ln2_b), ys)
    return out.reshape(B, L, D)


def kernel(x, mem, positions, ln_emb_g, ln_emb_b, w_in, pe_k_cmp, pe_v_cmp, w_kcmp1, w_kcmp2, w_vcmp1, w_vcmp2, s5_a_re, s5_a_im, s5_log_dt, s5_b_re, s5_b_im, s5_c_re, s5_c_im, s5_d, w_s5_glu, w_mem_kv, w_nsa_out, w_mem_out, w_o, ln1_g, ln1_b, w_router, b_router, w_gate_up, b_gate_up, w_down, b_down, ln2_g, ln2_b):
    assert w_in.shape[0] == DEPTH
    l = 0
    return _layer(x, mem, positions, ln_emb_g, ln_emb_b, w_in[l], pe_k_cmp[l], pe_v_cmp[l], w_kcmp1[l],
                  w_kcmp2[l], w_vcmp1[l], w_vcmp2[l], s5_a_re[l], s5_a_im[l], s5_log_dt[l], s5_b_re[l],
                  s5_b_im[l], s5_c_re[l], s5_c_im[l], s5_d[l], w_s5_glu[l], w_mem_kv[l], w_nsa_out[l],
                  w_mem_out[l], w_o[l], ln1_g[l], ln1_b[l], w_router[l], b_router[l], w_gate_up[l],
                  b_gate_up[l], w_down[l], b_down[l], ln2_g[l], ln2_b[l])
```

```python
import functools
import math

import jax
import jax.numpy as jnp
import numpy as np
from jax import lax
from jax.experimental import pallas as pl
from jax.experimental.pallas import tpu as pltpu

F32 = jnp.float32
BF16 = jnp.bfloat16
I32 = jnp.int32

D_MODEL = 1024
NSA_HEADS = 8
NSA_GROUPS = 2
NSA_HPG = NSA_HEADS // NSA_GROUPS
HEAD_DIM = 64
NSA_WIDTH = NSA_HEADS * HEAD_DIM
KV_WIDTH = NSA_GROUPS * HEAD_DIM
CMP_BLOCK = 32
CMP_STRIDE = 16
CMP_HIDDEN = 128
SEL_BLOCK = 64
N_SEL = 16
WINDOW = 512
Q_BLOCK = 256
WINDOW_Q = 128
ROPE_THETA = 500000.0
ROT_DIM = HEAD_DIM // 4
S5_WIDTH = 512
S5_GROUP_DIM = 16
S5_GROUPS = S5_WIDTH // S5_GROUP_DIM
S5_STATE = 64
MEM_HEADS = 4
MEM_HEAD_DIM = 128
MEM_WIDTH = MEM_HEADS * MEM_HEAD_DIM
N_BRANCH = 3
N_EXPERTS = 32
TOP_K = 4
D_FF = 1024
SWIGLU_LIMIT = 7.0
SWIGLU_ALPHA = 1.702
LN_EPS = 1e-5
DEPTH = 1
DEEPNORM_ALPHA = (2 * DEPTH) ** 0.25

LANES = 128
SUBLANES = 8
VMEM_LIMIT_BYTES = 56 * 1024 * 1024

TOKEN_TILE = 256
INPROJ_TOKEN_TILE = 512
MERGE_SORT_TILES = 2
SEL_KV_TILE = 512
S5_CHUNK = 512
S5_PITCH = S5_CHUNK + 8
MOE_ROWS = 512
NEG_BIG = -(2.0 ** 100)
Q_SCALE_LOG2 = HEAD_DIM ** -0.5 * math.log2(math.e)


def _cparams(*sem):
    return pltpu.CompilerParams(dimension_semantics=sem, vmem_limit_bytes=VMEM_LIMIT_BYTES)


def _dot(a, b):
    return jnp.dot(a, b, preferred_element_type=F32)


def _dot_nt(a, b):
    return lax.dot_general(a, b, (((1,), (1,)), ((), ())), preferred_element_type=F32)


def _layer_norm(x, g, b):
    mu = jnp.mean(x, axis=-1, keepdims=True)
    xc = x - mu
    var = jnp.mean(xc * xc, axis=-1, keepdims=True)
    return xc * lax.rsqrt(var + LN_EPS) * g + b


def _gelu_tanh(x):
    cdf = 0.5 * (1.0 + jnp.tanh(math.sqrt(2.0 / math.pi) * (x + 0.044715 * (x * x * x))))
    return x * cdf


def _masked_exp2(s, mask):
    s = jnp.where(mask, s, -jnp.inf)
    m = jnp.max(s, axis=-1, keepdims=True)
    m = jnp.where(m > -jnp.inf, m, 0.0)
    return jnp.exp2(s - m)


def _safe_recip(denom):
    return 1.0 / jnp.maximum(denom, jnp.finfo(F32).tiny)


def _split3(x):
    hi = x.astype(BF16)
    r1 = x - hi.astype(F32)
    mid = r1.astype(BF16)
    lo = (r1 - mid.astype(F32)).astype(BF16)
    return hi, mid, lo


def _full_spec(shape):
    nd = len(shape)
    return pl.BlockSpec(shape, lambda *_: (0,) * nd)


def _inproj_kernel(x_ref, g_ref, b_ref, cs_ref, spread_ref, unit_ref,
                   wq_ref, wk_ref, wv_ref, wg_ref, wu_ref, wqm_ref, wm_ref,
                   q_ref, kc_ref, vc_ref, ks_ref, vs_ref, kw_ref, vw_ref,
                   gate_ref, u_ref, qm_ref, gm_ref):
    h = _layer_norm(x_ref[0], g_ref[...], b_ref[...])
    hb = h.astype(BF16)
    tab = sum(_dot(part, spread_ref[...]) for part in _split3(cs_ref[0]))
    cos_t = tab[:, 0:LANES] + unit_ref[...]
    sin_a = tab[:, LANES:2 * LANES]
    sin_b = tab[:, 2 * LANES:3 * LANES]

    def rope(t):
        return (t * cos_t + pltpu.roll(t, ROT_DIM // 2, 1) * sin_a
                + pltpu.roll(t, LANES - ROT_DIM // 2, 1) * sin_b)

    q = _dot(hb, wq_ref[...])
    for c in range(NSA_WIDTH // LANES):
        qc = rope(q[:, c * LANES:(c + 1) * LANES]) * Q_SCALE_LOG2
        for hh in range(2):
            q_ref[0, 2 * c + hh] = qc[:, hh * HEAD_DIM:(hh + 1) * HEAD_DIM].astype(BF16)
    k3 = _dot(hb, wk_ref[...])
    kc = rope(k3[:, 0:LANES])
    ks = rope(k3[:, LANES:2 * LANES])
    kw = rope(k3[:, 2 * LANES:3 * LANES])
    v3 = _dot(hb, wv_ref[...])
    kc_ref[0] = kc.astype(BF16)
    vc_ref[0] = v3[:, 0:LANES].astype(BF16)
    tm = x_ref.shape[1]
    pos = pl.program_id(1) * tm + lax.broadcasted_iota(I32, (tm, LANES), 0)
    blk_hot = jnp.where(lax.broadcasted_iota(I32, (tm, LANES), 1) == pos // SEL_BLOCK, 1.0, 0.0)
    lane_pad = jnp.zeros((tm, LANES - HEAD_DIM), F32)
    ones_pad = jnp.where(lax.broadcasted_iota(I32, (tm, LANES - HEAD_DIM), 1) == 0, 1.0, 0.0)
    for g in range(NSA_GROUPS):
        sl = slice(g * HEAD_DIM, (g + 1) * HEAD_DIM)
        ks_ref[0, g] = jnp.concatenate([blk_hot, ks[:, sl], lane_pad], axis=1).astype(BF16)
        kw_ref[0, g] = kw[:, sl].astype(BF16)
        vs_ref[0, g] = jnp.concatenate([v3[:, LANES:2 * LANES][:, sl], ones_pad], axis=1).astype(BF16)
        vw_ref[0, g] = jnp.concatenate([v3[:, 2 * LANES:3 * LANES][:, sl], ones_pad], axis=1).astype(BF16)
    gate_ref[0] = jax.nn.sigmoid(_dot(hb, wg_ref[...]))
    u_ref[0] = _dot(hb, wu_ref[...])
    qm_ref[0] = _dot(hb, wqm_ref[...]).astype(BF16)
    gm_ref[0] = jax.nn.sigmoid(_dot(hb, wm_ref[...]))


def _inproj(x, ln_g, ln_b, cos_sin, spread, unit, wq, wk, wv, wg, wu, wqm, wm):
    B, L, D = x.shape
    tm = INPROJ_TOKEN_TILE
    grid = (B, L // tm)
    tok = lambda w: pl.BlockSpec((1, tm, w), lambda b, i: (b, i, 0))
    head = lambda n, w=HEAD_DIM: pl.BlockSpec((1, n, tm, w), lambda b, i: (b, 0, i, 0))
    in_specs = [tok(D), _full_spec((1, D)), _full_spec((1, D)), tok(ROT_DIM), _full_spec(spread.shape),
                _full_spec(unit.shape)]
    in_specs += [pl.BlockSpec(w.shape, lambda b, i: (0, 0), pipeline_mode=pl.Buffered(1))
                 for w in (wq, wk, wv, wg, wu, wqm, wm)]
    sd = jax.ShapeDtypeStruct
    out_shape = [
        sd((B, NSA_HEADS, L, HEAD_DIM), BF16),
        sd((B, L, KV_WIDTH), BF16), sd((B, L, KV_WIDTH), BF16),
        sd((B, NSA_GROUPS, L, 2 * LANES), BF16), sd((B, NSA_GROUPS, L, LANES), BF16),
        sd((B, NSA_GROUPS, L, HEAD_DIM), BF16), sd((B, NSA_GROUPS, L, LANES), BF16),
        sd((B, L, NSA_GROUPS * LANES), F32),
        sd((B, L, S5_WIDTH), F32),
        sd((B, L, MEM_WIDTH), BF16),
        sd((B, L, N_BRANCH * D), F32),
    ]
    out_specs = [head(NSA_HEADS), tok(KV_WIDTH), tok(KV_WIDTH), head(NSA_GROUPS, 2 * LANES),
                 head(NSA_GROUPS, LANES), head(NSA_GROUPS), head(NSA_GROUPS, LANES),
                 tok(NSA_GROUPS * LANES), tok(S5_WIDTH),
                 tok(MEM_WIDTH), tok(N_BRANCH * D)]
    return pl.pallas_call(
        _inproj_kernel, grid=grid, in_specs=in_specs, out_specs=out_specs, out_shape=out_shape,
        compiler_params=_cparams("parallel", "parallel"), name="inproj",
    )(x, ln_g, ln_b, cos_sin, spread, unit, wq, wk, wv, wg, wu, wqm, wm)


def _compress_kernel(kc_ref, vc_ref, pek_ref, pev_ref, wk1f_ref, wv1f_ref,
                     wk1a_ref, wk1b_ref, wv1a_ref, wv1b_ref, wk2_ref, wv2_ref, ck_ref, cv_ref):
    n_chunk = kc_ref.shape[1]
    row = lax.broadcasted_iota(I32, (n_chunk, 1), 0)

    def one(x_ref, pe_ref, w1f_ref, w1a_ref, w1b_ref, w2_ref, o_ref):
        x = x_ref[0]
        first = _dot(x, w1a_ref[...])
        second = _dot(x, w1b_ref[...])
        second = pltpu.roll(second, n_chunk - 1, 0)
        pe_term = _dot(pe_ref[...], w1f_ref[...])[0:1]
        pe_term = jnp.concatenate([pe_term] * NSA_GROUPS, axis=1)
        hid = _gelu_tanh(first + second + pe_term).astype(BF16)
        for g in range(NSA_GROUPS):
            o = _dot(hid[:, g * CMP_HIDDEN:(g + 1) * CMP_HIDDEN], w2_ref[...])
            o_ref[0, g] = jnp.where(row < n_chunk - 1, o, 0.0).astype(BF16)

    one(kc_ref, pek_ref, wk1f_ref, wk1a_ref, wk1b_ref, wk2_ref, ck_ref)
    one(vc_ref, pev_ref, wv1f_ref, wv1a_ref, wv1b_ref, wv2_ref, cv_ref)


def _compress(kc_r, vc_r, pek, pev, wk1f, wv1f, wk1a, wk1b, wv1a, wv1b, wk2, wv2):
    B, n_chunk, width = kc_r.shape
    blk = pl.BlockSpec((1, n_chunk, width), lambda b: (b, 0, 0))
    out = pl.BlockSpec((1, NSA_GROUPS, n_chunk, HEAD_DIM), lambda b: (b, 0, 0, 0))
    ws = [pek, pev, wk1f, wv1f, wk1a, wk1b, wv1a, wv1b, wk2, wv2]
    sd = jax.ShapeDtypeStruct((B, NSA_GROUPS, n_chunk, HEAD_DIM), BF16)
    return pl.pallas_call(
        _compress_kernel, grid=(B,), in_specs=[blk, blk] + [_full_spec(w.shape) for w in ws],
        out_specs=[out, out], out_shape=[sd, sd], compiler_params=_cparams("parallel"), name="compress",
    )(kc_r, vc_r, *ws)


def _nsa_kernel(q_ref, ck_ref, cv_ref, ks_ref, vs_ref, kw_ref, vw_ref, gate_ref, wsc_ref, o_ref):
    seq_len = ks_ref.shape[2]
    n_cmp = ck_ref.shape[2]
    n_sb = seq_len // SEL_BLOCK
    n_sel = min(N_SEL, n_sb)
    rows = NSA_HPG * Q_BLOCK
    groups = range(NSA_GROUPS)
    q0 = pl.program_id(1) * Q_BLOCK
    t1 = q0 + lax.broadcasted_iota(I32, (Q_BLOCK, 1), 0)
    t4 = jnp.concatenate([t1] * NSA_HPG, axis=0)
    tk = SEL_KV_TILE

    def front(g):
        q = q_ref[0, g * NSA_HPG:(g + 1) * NSA_HPG].reshape(rows, HEAD_DIM)

        s = _dot_nt(q, ck_ref[0, g])
        c_end = lax.broadcasted_iota(I32, (1, n_cmp), 1) * CMP_STRIDE + (CMP_BLOCK - 1)
        e = _masked_exp2(s, c_end <= t4)
        p_cmp = e * _safe_recip(jnp.sum(e, axis=-1, keepdims=True))
        o_cmp = _dot(p_cmp.astype(BF16), cv_ref[0, g])

        imp = p_cmp[0:Q_BLOCK]
        for hh in range(1, NSA_HPG):
            imp = imp + p_cmp[hh * Q_BLOCK:(hh + 1) * Q_BLOCK]
        w_sc = wsc_ref[...]
        score = sum(_dot(part, w_sc) for part in _split3(imp))
        score_t = score.T
        jb = lax.broadcasted_iota(I32, (n_sb, Q_BLOCK), 0)
        tb = (q0 + lax.broadcasted_iota(I32, (1, Q_BLOCK), 1)) // SEL_BLOCK
        forced = (jb == 0) | (jb == tb) | (jb == tb - 1)
        work = jnp.where(forced | (jb > tb), -jnp.inf, score_t)
        bias_t = jnp.where(forced, 0.0, NEG_BIG)
        jbf = jb.astype(F32)
        for _ in range(n_sel - 3):
            m = jnp.max(work, axis=0, keepdims=True)
            idx = jnp.min(jnp.where(work == m, jbf, float(n_sb)), axis=0, keepdims=True)
            pick = jbf == idx
            bias_t = jnp.where(pick, 0.0, bias_t)
            work = jnp.where(pick, -jnp.inf, work)
        sel_bias = bias_t.T
        if n_sb < LANES:
            sel_bias = jnp.concatenate([sel_bias, jnp.zeros((Q_BLOCK, LANES - n_sb), F32)], axis=1)

        span = WINDOW + WINDOW_Q
        parts = []
        for sub in range(Q_BLOCK // WINDOW_Q):
            pick = lambda a: jnp.concatenate(
                [a[hh * Q_BLOCK + sub * WINDOW_Q:hh * Q_BLOCK + (sub + 1) * WINDOW_Q] for hh in range(NSA_HPG)],
                axis=0)
            w0 = pl.multiple_of(jnp.maximum(q0 + sub * WINDOW_Q - WINDOW, 0), WINDOW_Q)
            s = _dot_nt(pick(q), kw_ref[0, g, pl.ds(w0, span), :])
            diff = pick(t4) - (w0 + lax.broadcasted_iota(I32, (1, span), 1))
            e = _masked_exp2(s, (diff >= 0) & (diff < WINDOW))
            o = _dot(e.astype(BF16), vw_ref[0, g, pl.ds(w0, span), :])
            parts.append(o[:, :HEAD_DIM] * _safe_recip(o[:, HEAD_DIM:HEAD_DIM + 1]))
        o_win = jnp.concatenate([parts[sub][hh * WINDOW_Q:(hh + 1) * WINDOW_Q]
                                 for hh in range(NSA_HPG) for sub in range(len(parts))], axis=0)

        q_aug = jnp.concatenate([jnp.concatenate([sel_bias.astype(BF16)] * NSA_HPG, axis=0), q,
                                 jnp.zeros((rows, LANES - HEAD_DIM), BF16)], axis=1)
        return q_aug, o_cmp, o_win

    fronts = [front(g) for g in groups]

    def sel_tile(g, j, carry, causal):
        m_run, acc = carry
        k0 = pl.multiple_of(j * tk, tk)
        sc = _dot_nt(fronts[g][0], ks_ref[0, g, pl.ds(k0, tk), :])
        if causal:
            kpos = k0 + lax.broadcasted_iota(I32, (1, tk), 1)
            sc = jnp.where(kpos <= t4, sc, NEG_BIG)
        m_new = jnp.maximum(m_run, jnp.max(sc, axis=-1, keepdims=True))
        p = jnp.exp2(sc - m_new)
        acc_new = jnp.exp2(m_run - m_new) * acc + _dot(p.astype(BF16), vs_ref[0, g, pl.ds(k0, tk), :])
        return m_new, acc_new

    def sel_pair(jj, carries, causal):
        return tuple(sel_tile(g, 2 * jj + 1, sel_tile(g, 2 * jj, carries[g], causal), causal) for g in groups)

    init = tuple((jnp.full((rows, 1), NEG_BIG, F32), jnp.zeros((rows, LANES), F32)) for _ in groups)
    last_pair = (q0 // tk) // 2
    carries = lax.fori_loop(0, last_pair, functools.partial(sel_pair, causal=False), init)
    carries = sel_pair(last_pair, carries, True)

    outs = []
    for g in groups:
        _, o_cmp, o_win = fronts[g]
        acc = carries[g][1]
        o_sel = acc[:, :HEAD_DIM] * (1.0 / acc[:, HEAD_DIM:HEAD_DIM + 1])
        gt = gate_ref[0, :, g * LANES:(g + 1) * LANES]
        for hh in range(NSA_HPG):
            sl = slice(hh * Q_BLOCK, (hh + 1) * Q_BLOCK)
            c = hh * N_BRANCH
            outs.append(o_cmp[sl] * gt[:, c:c + 1] + o_sel[sl] * gt[:, c + 1:c + 2]
                        + o_win[sl] * gt[:, c + 2:c + 3])
    o_ref[0] = jnp.concatenate(outs, axis=1).astype(BF16)


def _nsa(q_hm, ck, cv, ks, vs, kw, vw, gates, w_score):
    B, _, L, _ = q_hm.shape
    assert L // SEL_BLOCK <= LANES and (L // SEL_KV_TILE) % 2 == 0 and L >= WINDOW + Q_BLOCK
    n_cmp = ck.shape[2]
    grid = (B, L // Q_BLOCK)
    qspec = pl.BlockSpec((1, NSA_HEADS, Q_BLOCK, HEAD_DIM), lambda b, i: (b, 0, i, 0))
    cspec = pl.BlockSpec((1, NSA_GROUPS, n_cmp, HEAD_DIM), lambda b, i: (b, 0, 0, 0))
    kvspec = lambda w: pl.BlockSpec((1, NSA_GROUPS, L, w), lambda b, i: (b, 0, 0, 0),
                                    pipeline_mode=pl.Buffered(1))
    gspec = pl.BlockSpec((1, Q_BLOCK, NSA_GROUPS * LANES), lambda b, i: (b, i, 0))
    ospec = pl.BlockSpec((1, Q_BLOCK, NSA_WIDTH), lambda b, i: (b, i, 0))
    return pl.pallas_call(
        _nsa_kernel, grid=grid,
        in_specs=[qspec, cspec, cspec, kvspec(2 * LANES), kvspec(LANES), kvspec(HEAD_DIM), kvspec(LANES), gspec,
                  _full_spec(w_score.shape)],
        out_specs=ospec, out_shape=jax.ShapeDtypeStruct((B, L, NSA_WIDTH), BF16),
        compiler_params=_cparams("parallel", "arbitrary"), name="nsa",
    )(q_hm, ck, cv, ks, vs, kw, vw, gates, w_score)


def _s5_kernel(u_ref, wb_ref, wc_ref, are_ref, aim_ref, d_ref, y_ref, sre_ref, sim_ref, carry_ref):
    n_b, chunk, _ = u_ref.shape
    n_tile = wb_ref.shape[0]
    in_per = n_tile // (S5_WIDTH // LANES)
    pitch = S5_PITCH

    @pl.when(pl.program_id(0) == 0)
    def _():
        carry_ref[...] = jnp.zeros_like(carry_ref)

    for b in range(n_b):
        for c in range(n_tile):
            i = c // in_per
            ub = u_ref[b, :, i * LANES:(i + 1) * LANES].astype(BF16)
            r = _dot(ub, wb_ref[c])
            sre_ref[b, c * pitch:c * pitch + chunk, :] = r[:, :LANES]
            sim_ref[b, c * pitch:c * pitch + chunk, :] = r[:, LANES:]

    a_re, a_im = are_ref[...], aim_ref[...]

    def step(t, carry):
        out = []
        for b in range(n_b):
            s_re, s_im = carry[2 * b], carry[2 * b + 1]
            rows = pl.ds(t, n_tile, stride=pitch)
            n_re = a_re * s_re - a_im * s_im + sre_ref[b, rows, :]
            n_im = a_re * s_im + a_im * s_re + sim_ref[b, rows, :]
            sre_ref[b, rows, :] = n_re
            sim_ref[b, rows, :] = n_im
            out += [n_re, n_im]
        return tuple(out)

    init = tuple(carry_ref[i] for i in range(2 * n_b))
    fin = lax.fori_loop(0, chunk, step, init, unroll=8)
    for i in range(2 * n_b):
        carry_ref[i] = fin[i]

    for b in range(n_b):
        for o in range(S5_WIDTH // LANES):
            acc = jnp.zeros((chunk, LANES), F32)
            for c in range(o * in_per, (o + 1) * in_per):
                rows = slice(c * pitch, c * pitch + chunk)
                state = jnp.concatenate([sre_ref[b, rows, :], sim_ref[b, rows, :]], axis=1).astype(BF16)
                acc = acc + _dot(state, wc_ref[c])
            lanes = slice(o * LANES, (o + 1) * LANES)
            y = acc + d_ref[:, lanes] * u_ref[b, :, lanes]
            y_ref[b, :, lanes] = _gelu_tanh(y).astype(BF16)


def _s5(u, wb, wc, a_re, a_im, d_skip):
    B, L, W = u.shape
    chunk = S5_CHUNK
    n_tile = wb.shape[0]
    blk = pl.BlockSpec((B, chunk, W), lambda i: (0, i, 0))
    slab = pltpu.VMEM((B, n_tile * S5_PITCH, LANES), F32)
    return pl.pallas_call(
        _s5_kernel, grid=(L // chunk,),
        in_specs=[blk] + [_full_spec(w.shape) for w in (wb, wc, a_re, a_im, d_skip)],
        out_specs=blk, out_shape=jax.ShapeDtypeStruct((B, L, W), BF16),
        scratch_shapes=[slab, slab, pltpu.VMEM((2 * B, n_tile, LANES), F32)],
        compiler_params=_cparams("arbitrary"), name="s5",
    )(u, wb, wc, a_re, a_im, d_skip)


def _memkv_kernel(mem_ref, w_ref, k_ref, v_ref):
    kv = _dot(mem_ref[0].astype(BF16), w_ref[...])
    k_ref[0] = kv[:, :MEM_WIDTH].astype(BF16)
    v_ref[0] = kv[:, MEM_WIDTH:].astype(BF16)


def _memkv(mem, w_kv):
    B, M, D = mem.shape
    out = pl.BlockSpec((1, M, MEM_WIDTH), lambda b: (b, 0, 0))
    sd = jax.ShapeDtypeStruct((B, M, MEM_WIDTH), BF16)
    return pl.pallas_call(
        _memkv_kernel, grid=(B,),
        in_specs=[pl.BlockSpec((1, M, D), lambda b: (b, 0, 0)), _full_spec(w_kv.shape)],
        out_specs=[out, out], out_shape=[sd, sd], compiler_params=_cparams("parallel"), name="memkv",
    )(mem, w_kv)


def _memory_attention(q_ref, k_ref, v_ref):
    outs = []
    for h in range(MEM_HEADS):
        sl = slice(h * MEM_HEAD_DIM, (h + 1) * MEM_HEAD_DIM)
        s = _dot_nt(q_ref[:, sl], k_ref[0, :, sl]) * (MEM_HEAD_DIM ** -0.5)
        m = jnp.max(s, axis=-1, keepdims=True)
        e = jnp.exp(s - m)
        p = e / jnp.sum(e, axis=-1, keepdims=True)
        outs.append(_dot(p.astype(BF16), v_ref[0, :, sl]))
    return jnp.concatenate(outs, axis=1).astype(BF16)


def _merge_kernel(x_ref, lng_ref, lnb_ref, on_ref, gy_ref, qm_ref, km_ref, vm_ref, gm_ref,
                  wn_ref, wglu_ref, wmo_ref, wo_ref, l1g_ref, l1b_ref,
                  wrh_ref, wrp_ref, br_ref, tri_ref, striu_ref,
                  h1_ref, lp_ref, w4_ref, cnt_ref):
    D = x_ref.shape[1]
    tm = x_ref.shape[0]
    h =_layer_norm(x_ref[...], lng_ref[...], lnb_ref[...])
    y_nsa = _dot(on_ref[...], wn_ref[...])
    glu = _dot(gy_ref[...], wglu_ref[...])
    y_s5 = glu[:, :D] * jax.nn.sigmoid(glu[:, D:])
    y_mem = _dot(_memory_attention(qm_ref, km_ref, vm_ref), wmo_ref[...])
    merged = gm_ref[:, 0:D] * y_nsa + gm_ref[:, D:2 * D] * y_s5 + gm_ref[:, 2 * D:3 * D] * y_mem
    mix = _dot(merged.astype(BF16), wo_ref[...])
    h1 = _layer_norm(DEEPNORM_ALPHA * h + mix, l1g_ref[...], l1b_ref[...])
    h1_ref[...] = h1

    hh = h1.astype(BF16)
    hl = (h1 - hh.astype(F32)).astype(BF16)
    both = _dot(hh, wrp_ref[...])
    logits = both[:, :LANES] + both[:, LANES:] + _dot(hl, wrh_ref[...]) + br_ref[...]
    lane = lax.broadcasted_iota(I32, (tm, LANES), 1)
    lane_f = lane.astype(F32)
    work = logits
    multi = jnp.zeros((tm, LANES), F32)
    vals, picks = [], []
    for _ in range(TOP_K):
        m = jnp.max(work, axis=-1, keepdims=True)
        idx = jnp.min(jnp.where(work == m, lane_f, float(LANES)), axis=-1, keepdims=True)
        pick = lane_f == idx
        vals.append(m)
        picks.append((pick, idx))
        multi = jnp.where(pick, 1.0, multi)
        work = jnp.where(pick, -jnp.inf, work)
    es = [jnp.exp(v - vals[0]) for v in vals]
    den = es[0] + es[1] + es[2] + es[3]
    st = TOKEN_TILE
    pos = []
    for t in range(tm // st):
        multi_t = multi[t * st:(t + 1) * st]
        cnt = jnp.broadcast_to(jnp.sum(multi_t, axis=0, keepdims=True), (SUBLANES, LANES))
        cnt_ref[t] = cnt
        lower = _dot(cnt.astype(BF16), striu_ref[...])[0:1]
        pos.append(lower + _dot(tri_ref[...], multi_t.astype(BF16)))
    pos = jnp.concatenate(pos, axis=0)
    lp = jnp.full((tm, LANES), -1.0, F32)
    w4 = jnp.zeros((tm, LANES), F32)
    for k in range(TOP_K):
        pick, _ = picks[k]
        lp = jnp.where(lane == k, jnp.sum(jnp.where(pick, pos, 0.0), axis=-1, keepdims=True), lp)
        w4 = jnp.where(lane == k, es[k] / den, w4)
    lp_ref[...] = lp
    w4_ref[...] = w4


def _merge(x2, lng, lnb, o_nsa, gy, qm, k_mem, v_mem, gm, wn, wglu, wmo, wo, l1g, l1b, wrh, wrp, br, tri, striu):
    T, D = x2.shape
    tm = MERGE_SORT_TILES * TOKEN_TILE
    tok = lambda w: pl.BlockSpec((tm, w), lambda i: (i, 0))
    steps_per_batch = T // k_mem.shape[0] // tm
    mem_kv = pl.BlockSpec((1,) + k_mem.shape[1:], lambda i: (i // steps_per_batch, 0, 0))
    ws = [wn, wglu, wmo, wo, l1g, l1b, wrh, wrp, br, tri, striu]
    sd = jax.ShapeDtypeStruct
    lane_out = sd((T, LANES), F32)
    return pl.pallas_call(
        _merge_kernel, grid=(T // tm,),
        in_specs=[tok(D), _full_spec((1, D)), _full_spec((1, D)), tok(NSA_WIDTH), tok(S5_WIDTH),
                  tok(MEM_WIDTH), mem_kv, mem_kv, tok(N_BRANCH * D)] + [_full_spec(w.shape) for w in ws],
        out_specs=[tok(D), tok(LANES), tok(LANES),
                   pl.BlockSpec((MERGE_SORT_TILES, SUBLANES, LANES), lambda i: (i, 0, 0))],
        out_shape=[sd((T, D), F32), lane_out, lane_out, sd((T // TOKEN_TILE, SUBLANES, LANES), F32)],
        compiler_params=_cparams("parallel"), name="merge",
    )(x2, lng, lnb, o_nsa, gy, qm, k_mem, v_mem, gm, *ws)


def _slots_kernel(cnt_ref, triu_ref, striu_ref, tril_ref, seg_ref, blk_ref, misc_ref):
    n_blk = blk_ref.shape[0]
    cnt = cnt_ref[...]
    cnt_b = cnt.astype(BF16)
    total = jnp.sum(cnt, axis=0, keepdims=True)
    nblk_e = jnp.floor((total + (MOE_ROWS - 1)) * (1.0 / MOE_ROWS))
    nblk_8 = jnp.broadcast_to(nblk_e, (SUBLANES, LANES))
    end_b = _dot(nblk_8.astype(BF16), triu_ref[...])
    start_rows = (end_b - nblk_8)[0:1] * MOE_ROWS
    dst = start_rows + _dot(tril_ref[...], cnt_b)
    off = _dot(cnt_b, striu_ref[...])
    seg_ref[0] = cnt.astype(I32)
    seg_ref[1] = off.astype(I32)
    seg_ref[2] = dst.astype(I32)
    blk_i = lax.broadcasted_iota(I32, (n_blk, LANES), 0).astype(F32)
    lane_b = lax.broadcasted_iota(I32, (n_blk, LANES), 1)
    ended = jnp.where((end_b[0:1] <= blk_i) & (lane_b < N_EXPERTS), 1.0, 0.0)
    owner = jnp.minimum(jnp.sum(ended, axis=-1, keepdims=True), float(N_EXPERTS - 1))
    mine = lane_b.astype(F32) == owner
    pick = lambda per_expert: jnp.sum(jnp.where(mine, per_expert, 0.0), axis=-1, keepdims=True)
    earlier = blk_i[:, 0:1] - (pick(end_b[0:1]) - pick(nblk_e))
    held = jnp.clip(pick(total) - earlier * MOE_ROWS, 0.0, float(MOE_ROWS))
    blk_ref[...] = jnp.where(lane_b == 1, held, owner).astype(I32)
    cand = lax.broadcasted_iota(I32, (LANES, LANES), 0)
    has_blocks = jnp.broadcast_to(nblk_e, (LANES, LANES)).T > 0.0
    later = (cand > lax.broadcasted_iota(I32, (LANES, LANES), 1)) & has_blocks
    nxt = jnp.min(jnp.where(later, cand.astype(F32), float(LANES)), axis=0, keepdims=True)
    nxt = jnp.where(nxt < float(LANES), nxt, -1.0)
    lane8 = lax.broadcasted_iota(I32, (SUBLANES, LANES), 1)
    row8 = lax.broadcasted_iota(I32, (SUBLANES, LANES), 0)
    used = jnp.sum(jnp.where(lane8 == N_EXPERTS - 1, end_b, 0.0), axis=-1, keepdims=True)
    misc = jnp.where(row8 == 0, used,
                     jnp.where(row8 == 1, start_rows + total,
                               jnp.where(row8 == 2, nblk_e * MOE_ROWS - total, nxt)))
    misc_ref[...] = misc.astype(I32)


def _slots(cnt, triu, striu, tril, n_blk):
    n_tile = cnt.shape[0]
    sd = jax.ShapeDtypeStruct
    return pl.pallas_call(
        _slots_kernel, grid=(1,),
        in_specs=[_full_spec(cnt.shape), _full_spec(triu.shape), _full_spec(striu.shape), _full_spec(tril.shape)],
        out_specs=[_full_spec((3, n_tile, LANES)), _full_spec((n_blk, LANES)), _full_spec((SUBLANES, LANES))],
        out_shape=[sd((3, n_tile, LANES), I32), sd((n_blk, LANES), I32), sd((SUBLANES, LANES), I32)],
        compiler_params=_cparams("arbitrary"), name="slots",
    )(cnt, triu, striu, tril)


ROW_TILES = D_MODEL // LANES


def _row_span(row, n_rows):
    start = row * ROW_TILES
    if not isinstance(start, int):
        start = pl.multiple_of(start, ROW_TILES)
    return pl.ds(start, n_rows * ROW_TILES)


def _store_rows(ref, val):
    for c in range(ROW_TILES):
        ref[pl.ds(c, val.shape[0], stride=ROW_TILES), :] = val[:, c * LANES:(c + 1) * LANES]


def _load_row_tile(ref, n_rows, c):
    return ref[pl.ds(c, n_rows, stride=ROW_TILES), :]


BIG_PIECE_ROWS = 64


def _pieces(count, max_rows, fn):
    def run(sizes):
        for p in sizes:
            def piece(p=p):
                fn(count & (-2 * p), p)
            pl.when((count & p) != 0)(piece)

    sizes = [max_rows >> s for s in range(max_rows.bit_length())]
    big = [p for p in sizes if p >= BIG_PIECE_ROWS]
    if big:
        pl.when(count >= BIG_PIECE_ROWS)(lambda: run(big))
    run([p for p in sizes if p < BIG_PIECE_ROWS])


def _start_segment_copies(seg_ref, max_rows, make_copy):
    def per_expert(e, c):
        cnt, off, dst = seg_ref[0, 0, 0, e], seg_ref[1, 0, 0, e], seg_ref[2, 0, 0, e]
        _pieces(cnt, max_rows, lambda first, rows: make_copy(off + first, dst + first, rows).start())
        return c

    lax.fori_loop(0, N_EXPERTS, per_expert, 0)


def _dispatch_kernel(seg_ref, misc_ref, lp_ref, h_ref, xs_ref, sorted_ref, zero_ref, sem, pad_sem):
    i = pl.program_id(0)
    n = pl.num_programs(0)
    tm, D = h_ref.shape
    rows = TOP_K * tm
    slot = lax.rem(i, 2)

    def row_copy(slot_):
        def make(src_row, dst_row, n_rows):
            return pltpu.make_async_copy(sorted_ref.at[slot_, _row_span(src_row, n_rows)],
                                         xs_ref.at[_row_span(dst_row, n_rows)], sem.at[slot_])
        return make

    @pl.when(i == 0)
    def _():
        zero_ref[...] = jnp.zeros_like(zero_ref)
        for wait in (False, True):
            def per_expert(e, c, wait=wait):
                def one(first, n_rows):
                    cp = pltpu.make_async_copy(zero_ref.at[_row_span(0, n_rows)],
                                               xs_ref.at[_row_span(misc_ref[1, e] + first, n_rows)], pad_sem)
                    cp.wait() if wait else cp.start()
                _pieces(misc_ref[2, e], MOE_ROWS // 2, one)
                return c
            lax.fori_loop(0, N_EXPERTS, per_expert, 0)

            def per_spare_half_block(hb, c, wait=wait):
                cp = pltpu.make_async_copy(zero_ref, xs_ref.at[_row_span(hb * (MOE_ROWS // 2), MOE_ROWS // 2)],
                                           pad_sem)
                cp.wait() if wait else cp.start()
                return c
            lax.fori_loop(2 * misc_ref[0, 0], 2 * (xs_ref.shape[0] // (MOE_ROWS * ROW_TILES)),
                          per_spare_half_block, 0)

    lp_t = lp_ref[...].T
    s_ix = lax.broadcasted_iota(I32, (rows, 1), 0).astype(F32)
    hit = s_ix == lp_t[0:1, :]
    for k in range(1, TOP_K):
        hit = hit | (s_ix == lp_t[k:k + 1, :])
    perm = jnp.where(hit, 1.0, 0.0).astype(BF16)
    _store_rows(sorted_ref.at[slot], _dot(perm, h_ref[...].astype(BF16)))

    _start_segment_copies(seg_ref, tm, row_copy(slot))

    @pl.when(i > 0)
    def _():
        row_copy(1 - slot)(0, 0, rows).wait()

    @pl.when(i == n - 1)
    def _():
        row_copy(slot)(0, 0, rows).wait()


def _seg_spec(index_map):
    return pl.BlockSpec((3, 1, 1, LANES), index_map, memory_space=pltpu.SMEM)


def _dispatch(seg4, misc, lp, h1, cap):
    T, D = h1.shape
    assert D == ROW_TILES * LANES
    tm = TOKEN_TILE
    tok = lambda w: pl.BlockSpec((tm, w), lambda i: (i, 0))
    return pl.pallas_call(
        _dispatch_kernel, grid=(T // tm,),
        in_specs=[_seg_spec(lambda i: (0, i, 0, 0)), pl.BlockSpec(memory_space=pltpu.SMEM), tok(LANES), tok(D)],
        out_specs=pl.BlockSpec(memory_space=pl.ANY),
        out_shape=jax.ShapeDtypeStruct((cap * ROW_TILES, LANES), F32),
        scratch_shapes=[pltpu.VMEM((2, TOP_K * tm * ROW_TILES, LANES), F32),
                        pltpu.VMEM((MOE_ROWS // 2 * ROW_TILES, LANES), F32),
                        pltpu.SemaphoreType.DMA((2,)), pltpu.SemaphoreType.DMA(())],
        compiler_params=_cparams("arbitrary"), name="dispatch",
    )(seg4, misc, lp, h1)


def _expert_kernel(blk_ref, used_ref, next_ref, rows_ref, xs_ref, wgu_hbm, bgu_ref, wd_hbm, bd_ref, ys_ref,
                   wgu_f32, wd_f32, wgu_bf, wd_bf, sem, run_ref):
    i = pl.program_id(0)
    live = i < used_ref[0]
    expert = blk_ref[i]

    def weight_copies(e, slot):
        return (pltpu.make_async_copy(wgu_hbm.at[e], wgu_f32.at[slot], sem.at[0, slot]),
                pltpu.make_async_copy(wd_hbm.at[e], wd_f32.at[slot], sem.at[1, slot]))

    @pl.when(i == 0)
    def _():
        run_ref[0] = 0
        for cp in weight_copies(expert, 0):
            cp.start()

    @pl.when(live & ((i == 0) | (expert != blk_ref[jnp.maximum(i - 1, 0)])))
    def _():
        slot = lax.rem(run_ref[0], 2)
        for cp in weight_copies(expert, slot):
            cp.wait()
        wgu_bf[...] = wgu_f32[slot].astype(BF16)
        wd_bf[...] = wd_f32[slot].astype(BF16)
        nxt = next_ref[expert]

        @pl.when(nxt >= 0)
        def _():
            for cp in weight_copies(nxt, 1 - slot):
                cp.start()

        run_ref[0] = run_ref[0] + 1

    def expert_rows(n_rows):
        xb = jnp.concatenate([_load_row_tile(xs_ref, n_rows, c).astype(BF16) for c in range(ROW_TILES)], axis=1)
        gu = _dot(xb, wgu_bf[...]) + bgu_ref[0]
        g = jnp.minimum(gu[:, :D_FF], SWIGLU_LIMIT)
        lin = jnp.clip(gu[:, D_FF:], -SWIGLU_LIMIT, SWIGLU_LIMIT)
        act = g * jax.nn.sigmoid(SWIGLU_ALPHA * g) * (lin + 1.0)
        _store_rows(ys_ref, _dot(act.astype(BF16), wd_bf[...]) + bd_ref[0])

    half = MOE_ROWS // 2
    half_full = rows_ref[i] <= half

    @pl.when(live & jnp.logical_not(half_full))
    def _():
        expert_rows(MOE_ROWS)

    @pl.when(live & half_full)
    def _():
        expert_rows(half)
        ys_ref[half * ROW_TILES:, :] = jnp.zeros((half * ROW_TILES, LANES), F32)

    @pl.when(pl.program_id(0) >= used_ref[0])
    def _():
        ys_ref[...] = jnp.zeros_like(ys_ref)


def _experts(blk_expert, n_used, next_expert, blk_rows, xs, w_gate_up, b_gate_up, w_down, b_down):
    D = w_down.shape[2]
    n_blk = xs.shape[0] // (MOE_ROWS * ROW_TILES)
    E = w_gate_up.shape[0]
    live = lambda i, used: jnp.minimum(i, used[0] - 1)
    row = pl.BlockSpec((MOE_ROWS * ROW_TILES, LANES), lambda i, blk, used, nxt, held: (live(i, used), 0))
    by_e = lambda shape: pl.BlockSpec((1,) + shape, lambda i, blk, used, nxt, held: (blk[live(i, used)], 0, 0))
    in_hbm = pl.BlockSpec(memory_space=pl.ANY)
    grid_spec = pltpu.PrefetchScalarGridSpec(
        num_scalar_prefetch=4, grid=(n_blk,),
        in_specs=[row, in_hbm, by_e((1, 2 * D_FF)), in_hbm, by_e((1, D))],
        out_specs=pl.BlockSpec((MOE_ROWS * ROW_TILES, LANES), lambda i, blk, used, nxt, held: (i, 0)),
        scratch_shapes=[pltpu.VMEM((2, D, 2 * D_FF), F32), pltpu.VMEM((2, D_FF, D), F32),
                        pltpu.VMEM((D, 2 * D_FF), BF16), pltpu.VMEM((D_FF, D), BF16),
                        pltpu.SemaphoreType.DMA((2, 2)), pltpu.SMEM((1,), I32)])
    return pl.pallas_call(
        _expert_kernel, grid_spec=grid_spec, out_shape=jax.ShapeDtypeStruct(xs.shape, F32),
        compiler_params=_cparams("arbitrary"), name="experts",
    )(blk_expert, n_used, next_expert, blk_rows, xs, w_gate_up, b_gate_up.reshape(E, 1, 2 * D_FF), w_down,
      b_down.reshape(E, 1, D))


def _combine_kernel(seg_ref, segn_ref, lp_ref, w4_ref, h1_ref, g_ref, b_ref, ys_ref, o_ref, buf_ref, sem):
    i = pl.program_id(0)
    n = pl.num_programs(0)
    tm = h1_ref.shape[0]
    rows = TOP_K * tm
    slot = lax.rem(i, 2)

    def row_copy(slot_):
        def make(buf_row, ys_row, n_rows):
            return pltpu.make_async_copy(ys_ref.at[_row_span(ys_row, n_rows)],
                                         buf_ref.at[slot_, _row_span(buf_row, n_rows)], sem.at[slot_])
        return make

    @pl.when(i == 0)
    def _():
        _start_segment_copies(seg_ref, tm, row_copy(slot))

    @pl.when(i + 1 < n)
    def _():
        _start_segment_copies(segn_ref, tm, row_copy(1 - slot))

    row_copy(slot)(0, 0, rows).wait()

    lp = lp_ref[...]
    s_ix = lax.broadcasted_iota(I32, (1, rows), 1).astype(F32)
    wmat = jnp.zeros((tm, rows), F32)
    for k in range(TOP_K):
        wmat = jnp.where(s_ix == lp[:, k:k + 1], w4_ref[:, k:k + 1], wmat)
    w_hi = wmat.astype(BF16)
    w_lo = (wmat - w_hi.astype(F32)).astype(BF16)
    y = jnp.concatenate([_load_row_tile(buf_ref.at[slot], rows, c) for c in range(ROW_TILES)], axis=1)
    y_hi = y.astype(BF16)
    y_lo = (y - y_hi.astype(F32)).astype(BF16)
    acc = DEEPNORM_ALPHA * h1_ref[...] + (_dot(w_hi, y_hi) + _dot(w_hi, y_lo) + _dot(w_lo, y_hi))
    o_ref[...] = _layer_norm(acc, g_ref[...], b_ref[...])


def _combine(seg4, lp, w4, h1, ln_g, ln_b, ys):
    T, D = h1.shape
    tm = TOKEN_TILE
    n_tile = T // tm
    return pl.pallas_call(
        _combine_kernel, grid=(n_tile,),
        in_specs=[_seg_spec(lambda i: (0, i, 0, 0)),
                  _seg_spec(lambda i: (0, jnp.minimum(i + 1, n_tile - 1), 0, 0)),
                  pl.BlockSpec((tm, LANES), lambda i: (i, 0)),
                  pl.BlockSpec((tm, LANES), lambda i: (i, 0)),
                  pl.BlockSpec((tm, D), lambda i: (i, 0)),
                  _full_spec((1, D)), _full_spec((1, D)),
                  pl.BlockSpec(memory_space=pl.ANY)],
        out_specs=pl.BlockSpec((tm, D), lambda i: (i, 0)),
        out_shape=jax.ShapeDtypeStruct((T, D), F32),
        scratch_shapes=[pltpu.VMEM((2, TOP_K * tm * ROW_TILES, LANES), F32), pltpu.SemaphoreType.DMA((2,))],
        compiler_params=_cparams("arbitrary"), name="combine",
    )(seg4, seg4, lp, w4, h1, ln_g, ln_b, ys)


def _rope_tables(positions):
    inv = ROPE_THETA ** (-jnp.arange(0, ROT_DIM, 2, dtype=F32) / ROT_DIM)
    ang = positions.astype(F32)[..., None] * inv
    cos_sin = jnp.concatenate([jnp.cos(ang), jnp.sin(ang)], axis=-1)
    half = ROT_DIM // 2
    spread = np.zeros((ROT_DIM, 3 * LANES), np.float32)
    unit = np.ones((1, LANES), np.float32)
    for lane in range(LANES):
        d = lane % HEAD_DIM
        if d < half:
            spread[d, lane] = 1.0
            spread[half + d, 2 * LANES + lane] = -1.0
            unit[0, lane] = 0.0
        elif d < ROT_DIM:
            spread[d - half, lane] = 1.0
            spread[d, LANES + lane] = 1.0
            unit[0, lane] = 0.0
    return cos_sin, jnp.asarray(spread, BF16), jnp.asarray(unit)


def _split_w_in(w_in):
    widths = (NSA_WIDTH,) + (KV_WIDTH,) * 6 + (NSA_HEADS * N_BRANCH, S5_WIDTH, MEM_WIDTH, N_BRANCH * D_MODEL)
    offs = [0]
    for w in widths:
        offs.append(offs[-1] + w)
    col = lambda i: w_in[:, offs[i]:offs[i + 1]]
    wq, kc, vc, ks, vs, kw, vw, wg, wu, wqm, wm = (col(i) for i in range(11))
    wk = jnp.concatenate([kc, ks, kw], axis=1)
    wv = jnp.concatenate([vc, vs, vw], axis=1)
    per_group = NSA_HPG * N_BRANCH
    wg_pad = jnp.zeros((w_in.shape[0], NSA_GROUPS * LANES), w_in.dtype)
    for g in range(NSA_GROUPS):
        wg_pad = wg_pad.at[:, g * LANES:g * LANES + per_group].set(wg[:, g * per_group:(g + 1) * per_group])
    return tuple(w.astype(BF16) for w in (wq, wk, wv, wg_pad, wu, wqm, wm))


def _compress_weights(w1):
    half = CMP_BLOCK // 2
    eye = np.eye(NSA_GROUPS, dtype=np.float32)

    def arrange(w_half):
        full = jnp.einsum('sdf,gh->sgdhf', w_half, eye)
        return full.reshape(half * NSA_GROUPS * HEAD_DIM, NSA_GROUPS * CMP_HIDDEN).astype(BF16)

    return (w1.reshape(CMP_BLOCK * HEAD_DIM, CMP_HIDDEN).astype(BF16), arrange(w1[:half]), arrange(w1[half:]))


def _s5_weights(a_re, a_im, log_dt, b_re, b_im, c_re, c_im):
    step = jnp.exp(log_dt)[:, None]
    mag = jnp.exp(a_re * step)
    ab_re, ab_im = mag * jnp.cos(a_im * step), mag * jnp.sin(a_im * step)
    den = a_re * a_re + a_im * a_im
    nr = ab_re - 1.0
    coef_re = (nr * a_re + ab_im * a_im) / den
    coef_im = (ab_im * a_re - nr * a_im) / den
    bb_re = coef_re[..., None] * b_re - coef_im[..., None] * b_im
    bb_im = coef_re[..., None] * b_im + coef_im[..., None] * b_re
    n_tile = S5_GROUPS * S5_STATE // LANES
    tile_groups = LANES // S5_STATE
    lane_groups = LANES // S5_GROUP_DIM
    tiles_per_lane_tile = lane_groups // tile_groups
    place = np.zeros((n_tile, lane_groups, tile_groups), np.float32)
    for c in range(n_tile):
        for j in range(tile_groups):
            place[c, (c % tiles_per_lane_tile) * tile_groups + j, j] = 1.0

    def in_blocks(bb):
        pairs = bb.reshape(n_tile, tile_groups, S5_STATE, S5_GROUP_DIM)
        return jnp.einsum('cjnp,caj->capjn', pairs, place).reshape(n_tile, LANES, LANES)

    def out_blocks(c):
        pairs = c.reshape(n_tile, tile_groups, S5_GROUP_DIM, S5_STATE)
        return jnp.einsum('cjpn,caj->cjnap', pairs, place).reshape(n_tile, LANES, LANES)

    wb = jnp.concatenate([in_blocks(bb_re), in_blocks(bb_im)], axis=2).astype(BF16)
    wc = jnp.concatenate([out_blocks(c_re), out_blocks(-c_im)], axis=1).astype(BF16)
    return wb, wc, ab_re.reshape(n_tile, LANES), ab_im.reshape(n_tile, LANES)


def _layer(x, mem, positions, ln_emb_g, ln_emb_b, w_in, pe_k, pe_v, w_kcmp1, w_kcmp2, w_vcmp1, w_vcmp2,
           s5_a_re, s5_a_im, s5_log_dt, s5_b_re, s5_b_im, s5_c_re, s5_c_im, s5_d,
           w_s5_glu, w_mem_kv, w_nsa_out, w_mem_out, w_o, ln1_g, ln1_b, w_router, b_router,
           w_gate_up, b_gate_up, w_down, b_down, ln2_g, ln2_b):
    B, L, D = x.shape
    T = B * L
    row = lambda v: v.reshape(1, -1)

    cos_sin, spread, unit = _rope_tables(positions)
    (q_hm, kc, vc, ks, vs, kw, vw, gates, u, qm, gm) = _inproj(
        x, row(ln_emb_g), row(ln_emb_b), cos_sin, spread, unit, *_split_w_in(w_in))

    n_chunk = L // CMP_STRIDE
    chunked = lambda t: t.reshape(B, n_chunk, CMP_STRIDE * KV_WIDTH)
    pe_rows = lambda pe: jnp.broadcast_to(pe.reshape(1, -1), (SUBLANES, CMP_BLOCK * HEAD_DIM)).astype(BF16)
    wk1f, wk1a, wk1b = _compress_weights(w_kcmp1)
    wv1f, wv1a, wv1b = _compress_weights(w_vcmp1)
    ck, cv = _compress(chunked(kc), chunked(vc), pe_rows(pe_k), pe_rows(pe_v), wk1f, wv1f,
                       wk1a, wk1b, wv1a, wv1b, w_kcmp2.astype(BF16), w_vcmp2.astype(BF16))

    per_sb = SEL_BLOCK // CMP_STRIDE
    c_ix = np.arange(n_chunk)[:, None]
    n_ix = np.arange(L // SEL_BLOCK)[None, :]
    w_score = jnp.asarray((c_ix // per_sb == n_ix).astype(np.float32)
                          + ((c_ix + 1) // per_sb == n_ix).astype(np.float32), BF16)
    o_nsa = _nsa(q_hm, ck, cv, ks, vs, kw, vw, gates, w_score)

    wb, wc, a_re, a_im = _s5_weights(s5_a_re, s5_a_im, s5_log_dt, s5_b_re, s5_b_im, s5_c_re, s5_c_im)
    gy = _s5(u, wb, wc, a_re, a_im, row(s5_d))

    k_mem, v_mem = _memkv(mem, w_mem_kv.astype(BF16))

    pad_e = LANES - N_EXPERTS
    wr = jnp.pad(w_router, ((0, 0), (0, pad_e)))
    wr_hi = wr.astype(BF16)
    wr_pair = jnp.concatenate([wr_hi, (wr - wr_hi.astype(F32)).astype(BF16)], axis=1)
    br = jnp.concatenate([b_router, jnp.full((pad_e,), -jnp.inf, F32)]).reshape(1, LANES)
    tm = TOKEN_TILE
    n_tile = T // tm
    strict_lower = lambda n: jnp.asarray(np.tril(np.ones((n, n), np.float32), -1), BF16)
    triu = jnp.asarray(np.triu(np.ones((LANES, LANES), np.float32)), BF16)
    striu = jnp.asarray(np.triu(np.ones((LANES, LANES), np.float32), 1), BF16)
    flat = lambda t: t.reshape(T, t.shape[-1])
    h1, lp, w4, cnt = _merge(
        flat(x), row(ln_emb_g), row(ln_emb_b), flat(o_nsa), flat(gy), flat(qm), k_mem, v_mem, flat(gm),
        w_nsa_out.astype(BF16), w_s5_glu.astype(BF16), w_mem_out.astype(BF16), w_o.astype(BF16),
        row(ln1_g), row(ln1_b), wr_hi, wr_pair, br, strict_lower(tm), striu)

    cap = (T * TOP_K + MOE_ROWS - 1) // MOE_ROWS * MOE_ROWS + N_EXPERTS * MOE_ROWS
    n_blk = cap // MOE_ROWS
    seg, blk_owner, misc = _slots(cnt[:, 0, :], triu, striu, strict_lower(n_tile), n_blk)
    seg4 = seg.reshape(3, n_tile, 1, LANES)
    blk_expert = blk_owner[:, 0]
    n_used = misc[0, :1]

    xs = _dispatch(seg4, misc, lp, h1, cap)
    ys = _experts(blk_expert, n_used, misc[3, :N_EXPERTS], blk_owner[:, 1], xs, w_gate_up, b_gate_up, w_down,
                  b_down)
    out = _combine(seg4, lp, w4, h1, row(ln2_g), row(ln2_b), ys)
    return out.reshape(B, L, D)


def kernel(x, mem, positions, ln_emb_g, ln_emb_b, w_in, pe_k_cmp, pe_v_cmp, w_kcmp1, w_kcmp2, w_vcmp1, w_vcmp2, s5_a_re, s5_a_im, s5_log_dt, s5_b_re, s5_b_im, s5_c_re, s5_c_im, s5_d, w_s5_glu, w_mem_kv, w_nsa_out, w_mem_out, w_o, ln1_g, ln1_b, w_router, b_router, w_gate_up, b_gate_up, w_down, b_down, ln2_g, ln2_b):
    assert w_in.shape[0] == DEPTH
    l = 0
    return _layer(x, mem, positions, ln_emb_g, ln_emb_b, w_in[l], pe_k_cmp[l], pe_v_cmp[l], w_kcmp1[l],
                  w_kcmp2[l], w_vcmp1[l], w_vcmp2[l], s5_a_re[l], s5_a_im[l], s5_log_dt[l], s5_b_re[l],
                  s5_b_im[l], s5_c_re[l], s5_c_im[l], s5_d[l], w_s5_glu[l], w_mem_kv[l], w_nsa_out[l],
                  w_mem_out[l], w_o[l], ln1_g[l], ln1_b[l], w_router[l], b_router[l], w_gate_up[l],
                  b_gate_up[l], w_down[l], b_down[l], ln2_g[l], ln2_b[l])
```

```python
import functools
import math

import jax
import jax.numpy as jnp
import numpy as np
from jax import lax
from jax.experimental import pallas as pl
from jax.experimental.pallas import tpu as pltpu

F32 = jnp.float32
BF16 = jnp.bfloat16
I32 = jnp.int32

D_MODEL = 1024
NSA_HEADS = 8
NSA_GROUPS = 2
NSA_HPG = NSA_HEADS // NSA_GROUPS
HEAD_DIM = 64
NSA_WIDTH = NSA_HEADS * HEAD_DIM
KV_WIDTH = NSA_GROUPS * HEAD_DIM
CMP_BLOCK = 32
CMP_STRIDE = 16
CMP_HIDDEN = 128
SEL_BLOCK = 64
N_SEL = 16
WINDOW = 512
Q_BLOCK = 256
WINDOW_Q = 128
ROPE_THETA = 500000.0
ROT_DIM = HEAD_DIM // 4
S5_WIDTH = 512
S5_GROUP_DIM = 16
S5_GROUPS = S5_WIDTH // S5_GROUP_DIM
S5_STATE = 64
MEM_HEADS = 4
MEM_HEAD_DIM = 128
MEM_WIDTH = MEM_HEADS * MEM_HEAD_DIM
N_BRANCH = 3
N_EXPERTS = 32
TOP_K = 4
D_FF = 1024
SWIGLU_LIMIT = 7.0
SWIGLU_ALPHA = 1.702
LN_EPS = 1e-5
DEPTH = 1
DEEPNORM_ALPHA = (2 * DEPTH) ** 0.25

LANES = 128
SUBLANES = 8
VMEM_LIMIT_BYTES = 56 * 1024 * 1024

TOKEN_TILE = 256
INPROJ_TOKEN_TILE = 512
MERGE_SORT_TILES = 2
SEL_KV_TILE = 512
S5_CHUNK = 512
S5_PITCH = S5_CHUNK + 8
MOE_ROWS = 512
NEG_BIG = -(2.0 ** 100)
Q_SCALE_LOG2 = HEAD_DIM ** -0.5 * math.log2(math.e)


def _cparams(*sem):
    return pltpu.CompilerParams(dimension_semantics=sem, vmem_limit_bytes=VMEM_LIMIT_BYTES)


def _dot(a, b):
    return jnp.dot(a, b, preferred_element_type=F32)


def _dot_nt(a, b):
    return lax.dot_general(a, b, (((1,), (1,)), ((), ())), preferred_element_type=F32)


def _layer_norm(x, g, b):
    mu = jnp.mean(x, axis=-1, keepdims=True)
    xc = x - mu
    var = jnp.mean(xc * xc, axis=-1, keepdims=True)
    return xc * lax.rsqrt(var + LN_EPS) * g + b


def _gelu_tanh(x):
    cdf = 0.5 * (1.0 + jnp.tanh(math.sqrt(2.0 / math.pi) * (x + 0.044715 * (x * x * x))))
    return x * cdf


def _masked_exp2(s, mask):
    s = jnp.where(mask, s, -jnp.inf)
    m = jnp.max(s, axis=-1, keepdims=True)
    m = jnp.where(m > -jnp.inf, m, 0.0)
    return jnp.exp2(s - m)


def _safe_recip(denom):
    return 1.0 / jnp.maximum(denom, jnp.finfo(F32).tiny)


def _split3(x):
    hi = x.astype(BF16)
    r1 = x - hi.astype(F32)
    mid = r1.astype(BF16)
    lo = (r1 - mid.astype(F32)).astype(BF16)
    return hi, mid, lo


def _full_spec(shape):
    nd = len(shape)
    return pl.BlockSpec(shape, lambda *_: (0,) * nd)


def _inproj_kernel(x_ref, g_ref, b_ref, cs_ref, spread_ref, unit_ref,
                   wq_ref, wk_ref, wv_ref, wg_ref, wu_ref, wqm_ref, wm_ref,
                   q_ref, kc_ref, vc_ref, ks_ref, vs_ref, kw_ref, vw_ref,
                   gate_ref, u_ref, qm_ref, gm_ref):
    h = _layer_norm(x_ref[0], g_ref[...], b_ref[...])
    hb = h.astype(BF16)
    tab = sum(_dot(part, spread_ref[...]) for part in _split3(cs_ref[0]))
    cos_t = tab[:, 0:LANES] + unit_ref[...]
    sin_a = tab[:, LANES:2 * LANES]
    sin_b = tab[:, 2 * LANES:3 * LANES]

    def rope(t):
        return (t * cos_t + pltpu.roll(t, ROT_DIM // 2, 1) * sin_a
                + pltpu.roll(t, LANES - ROT_DIM // 2, 1) * sin_b)

    q = _dot(hb, wq_ref[...])
    for c in range(NSA_WIDTH // LANES):
        qc = rope(q[:, c * LANES:(c + 1) * LANES]) * Q_SCALE_LOG2
        for hh in range(2):
            q_ref[0, 2 * c + hh] = qc[:, hh * HEAD_DIM:(hh + 1) * HEAD_DIM].astype(BF16)
    k3 = _dot(hb, wk_ref[...])
    kc = rope(k3[:, 0:LANES])
    ks = rope(k3[:, LANES:2 * LANES])
    kw = rope(k3[:, 2 * LANES:3 * LANES])
    v3 = _dot(hb, wv_ref[...])
    kc_ref[0] = kc.astype(BF16)
    vc_ref[0] = v3[:, 0:LANES].astype(BF16)
    tm = x_ref.shape[1]
    pos = pl.program_id(1) * tm + lax.broadcasted_iota(I32, (tm, LANES), 0)
    blk_hot = jnp.where(lax.broadcasted_iota(I32, (tm, LANES), 1) == pos // SEL_BLOCK, 1.0, 0.0)
    lane_pad = jnp.zeros((tm, LANES - HEAD_DIM), F32)
    ones_pad = jnp.where(lax.broadcasted_iota(I32, (tm, LANES - HEAD_DIM), 1) == 0, 1.0, 0.0)
    for g in range(NSA_GROUPS):
        sl = slice(g * HEAD_DIM, (g + 1) * HEAD_DIM)
        ks_ref[0, g] = jnp.concatenate([blk_hot, ks[:, sl], lane_pad], axis=1).astype(BF16)
        kw_ref[0, g] = kw[:, sl].astype(BF16)
        vs_ref[0, g] = jnp.concatenate([v3[:, LANES:2 * LANES][:, sl], ones_pad], axis=1).astype(BF16)
        vw_ref[0, g] = jnp.concatenate([v3[:, 2 * LANES:3 * LANES][:, sl], ones_pad], axis=1).astype(BF16)
    gate_ref[0] = jax.nn.sigmoid(_dot(hb, wg_ref[...]))
    u_ref[0] = _dot(hb, wu_ref[...])
    qm_ref[0] = _dot(hb, wqm_ref[...]).astype(BF16)
    gm_ref[0] = jax.nn.sigmoid(_dot(hb, wm_ref[...]))


def _inproj(x, ln_g, ln_b, cos_sin, spread, unit, wq, wk, wv, wg, wu, wqm, wm):
    B, L, D = x.shape
    tm = INPROJ_TOKEN_TILE
    grid = (B, L // tm)
    tok = lambda w: pl.BlockSpec((1, tm, w), lambda b, i: (b, i, 0))
    head = lambda n, w=HEAD_DIM: pl.BlockSpec((1, n, tm, w), lambda b, i: (b, 0, i, 0))
    in_specs = [tok(D), _full_spec((1, D)), _full_spec((1, D)), tok(ROT_DIM), _full_spec(spread.shape),
                _full_spec(unit.shape)]
    in_specs += [pl.BlockSpec(w.shape, lambda b, i: (0, 0), pipeline_mode=pl.Buffered(1))
                 for w in (wq, wk, wv, wg, wu, wqm, wm)]
    sd = jax.ShapeDtypeStruct
    out_shape = [
        sd((B, NSA_HEADS, L, HEAD_DIM), BF16),
        sd((B, L, KV_WIDTH), BF16), sd((B, L, KV_WIDTH), BF16),
        sd((B, NSA_GROUPS, L, 2 * LANES), BF16), sd((B, NSA_GROUPS, L, LANES), BF16),
        sd((B, NSA_GROUPS, L, HEAD_DIM), BF16), sd((B, NSA_GROUPS, L, LANES), BF16),
        sd((B, L, NSA_GROUPS * LANES), F32),
        sd((B, L, S5_WIDTH), F32),
        sd((B, L, MEM_WIDTH), BF16),
        sd((B, L, N_BRANCH * D), F32),
    ]
    out_specs = [head(NSA_HEADS), tok(KV_WIDTH), tok(KV_WIDTH), head(NSA_GROUPS, 2 * LANES),
                 head(NSA_GROUPS, LANES), head(NSA_GROUPS), head(NSA_GROUPS, LANES),
                 tok(NSA_GROUPS * LANES), tok(S5_WIDTH),
                 tok(MEM_WIDTH), tok(N_BRANCH * D)]
    return pl.pallas_call(
        _inproj_kernel, grid=grid, in_specs=in_specs, out_specs=out_specs, out_shape=out_shape,
        compiler_params=_cparams("parallel", "parallel"), name="inproj",
    )(x, ln_g, ln_b, cos_sin, spread, unit, wq, wk, wv, wg, wu, wqm, wm)


def _compress_kernel(kc_ref, vc_ref, pek_ref, pev_ref, wk1f_ref, wv1f_ref,
                     wk1a_ref, wk1b_ref, wv1a_ref, wv1b_ref, wk2_ref, wv2_ref, ck_ref, cv_ref):
    n_chunk = kc_ref.shape[1]
    row = lax.broadcasted_iota(I32, (n_chunk, 1), 0)

    def one(x_ref, pe_ref, w1f_ref, w1a_ref, w1b_ref, w2_ref, o_ref):
        x = x_ref[0]
        first = _dot(x, w1a_ref[...])
        second = _dot(x, w1b_ref[...])
        second = pltpu.roll(second, n_chunk - 1, 0)
        pe_term = _dot(pe_ref[...], w1f_ref[...])[0:1]
        pe_term = jnp.concatenate([pe_term] * NSA_GROUPS, axis=1)
        hid = _gelu_tanh(first + second + pe_term).astype(BF16)
        for g in range(NSA_GROUPS):
            o = _dot(hid[:, g * CMP_HIDDEN:(g + 1) * CMP_HIDDEN], w2_ref[...])
            o_ref[0, g] = jnp.where(row < n_chunk - 1, o, 0.0).astype(BF16)

    one(kc_ref, pek_ref, wk1f_ref, wk1a_ref, wk1b_ref, wk2_ref, ck_ref)
    one(vc_ref, pev_ref, wv1f_ref, wv1a_ref, wv1b_ref, wv2_ref, cv_ref)


def _compress(kc_r, vc_r, pek, pev, wk1f, wv1f, wk1a, wk1b, wv1a, wv1b, wk2, wv2):
    B, n_chunk, width = kc_r.shape
    blk = pl.BlockSpec((1, n_chunk, width), lambda b: (b, 0, 0))
    out = pl.BlockSpec((1, NSA_GROUPS, n_chunk, HEAD_DIM), lambda b: (b, 0, 0, 0))
    ws = [pek, pev, wk1f, wv1f, wk1a, wk1b, wv1a, wv1b, wk2, wv2]
    sd = jax.ShapeDtypeStruct((B, NSA_GROUPS, n_chunk, HEAD_DIM), BF16)
    return pl.pallas_call(
        _compress_kernel, grid=(B,), in_specs=[blk, blk] + [_full_spec(w.shape) for w in ws],
        out_specs=[out, out], out_shape=[sd, sd], compiler_params=_cparams("parallel"), name="compress",
    )(kc_r, vc_r, *ws)


def _nsa_kernel(q_ref, ck_ref, cv_ref, ks_ref, vs_ref, kw_ref, vw_ref, gate_ref, wsc_ref, o_ref):
    seq_len = ks_ref.shape[2]
    n_cmp = ck_ref.shape[2]
    n_sb = seq_len // SEL_BLOCK
    n_sel = min(N_SEL, n_sb)
    rows = NSA_HPG * Q_BLOCK
    groups = range(NSA_GROUPS)
    q0 = pl.program_id(1) * Q_BLOCK
    t1 = q0 + lax.broadcasted_iota(I32, (Q_BLOCK, 1), 0)
    t4 = jnp.concatenate([t1] * NSA_HPG, axis=0)
    tk = SEL_KV_TILE

    def front(g):
        q = q_ref[0, g * NSA_HPG:(g + 1) * NSA_HPG].reshape(rows, HEAD_DIM)

        s = _dot_nt(q, ck_ref[0, g])
        c_end = lax.broadcasted_iota(I32, (1, n_cmp), 1) * CMP_STRIDE + (CMP_BLOCK - 1)
        e = _masked_exp2(s, c_end <= t4)
        p_cmp = e * _safe_recip(jnp.sum(e, axis=-1, keepdims=True))
        o_cmp = _dot(p_cmp.astype(BF16), cv_ref[0, g])

        imp = p_cmp[0:Q_BLOCK]
        for hh in range(1, NSA_HPG):
            imp = imp + p_cmp[hh * Q_BLOCK:(hh + 1) * Q_BLOCK]
        w_sc = wsc_ref[...]
        score = sum(_dot(part, w_sc) for part in _split3(imp))
        score_t = score.T
        jb = lax.broadcasted_iota(I32, (n_sb, Q_BLOCK), 0)
        tb = (q0 + lax.broadcasted_iota(I32, (1, Q_BLOCK), 1)) // SEL_BLOCK
        forced = (jb == 0) | (jb == tb) | (jb == tb - 1)
        work = jnp.where(forced | (jb > tb), -jnp.inf, score_t)
        bias_t = jnp.where(forced, 0.0, NEG_BIG)
        jbf = jb.astype(F32)
        for _ in range(n_sel - 3):
            m = jnp.max(work, axis=0, keepdims=True)
            idx = jnp.min(jnp.where(work == m, jbf, float(n_sb)), axis=0, keepdims=True)
            pick = jbf == idx
            bias_t = jnp.where(pick, 0.0, bias_t)
            work = jnp.where(pick, -jnp.inf, work)
        sel_bias = bias_t.T
        if n_sb < LANES:
            sel_bias = jnp.concatenate([sel_bias, jnp.zeros((Q_BLOCK, LANES - n_sb), F32)], axis=1)

        span = WINDOW + WINDOW_Q
        parts = []
        for sub in range(Q_BLOCK // WINDOW_Q):
            pick = lambda a: jnp.concatenate(
                [a[hh * Q_BLOCK + sub * WINDOW_Q:hh * Q_BLOCK + (sub + 1) * WINDOW_Q] for hh in range(NSA_HPG)],
                axis=0)
            w0 = pl.multiple_of(jnp.maximum(q0 + sub * WINDOW_Q - WINDOW, 0), WINDOW_Q)
            s = _dot_nt(pick(q), kw_ref[0, g, pl.ds(w0, span), :])
            diff = pick(t4) - (w0 + lax.broadcasted_iota(I32, (1, span), 1))
            e = _masked_exp2(s, (diff >= 0) & (diff < WINDOW))
            o = _dot(e.astype(BF16), vw_ref[0, g, pl.ds(w0, span), :])
            parts.append(o[:, :HEAD_DIM] * _safe_recip(o[:, HEAD_DIM:HEAD_DIM + 1]))
        o_win = jnp.concatenate([parts[sub][hh * WINDOW_Q:(hh + 1) * WINDOW_Q]
                                 for hh in range(NSA_HPG) for sub in range(len(parts))], axis=0)

        q_aug = jnp.concatenate([jnp.concatenate([sel_bias.astype(BF16)] * NSA_HPG, axis=0), q,
                                 jnp.zeros((rows, LANES - HEAD_DIM), BF16)], axis=1)
        return q_aug, o_cmp, o_win

    fronts = [front(g) for g in groups]

    def sel_tile(g, j, carry, causal):
        m_run, acc = carry
        k0 = pl.multiple_of(j * tk, tk)
        sc = _dot_nt(fronts[g][0], ks_ref[0, g, pl.ds(k0, tk), :])
        if causal:
            kpos = k0 + lax.broadcasted_iota(I32, (1, tk), 1)
            sc = jnp.where(kpos <= t4, sc, NEG_BIG)
        m_new = jnp.maximum(m_run, jnp.max(sc, axis=-1, keepdims=True))
        p = jnp.exp2(sc - m_new)
        acc_new = jnp.exp2(m_run - m_new) * acc + _dot(p.astype(BF16), vs_ref[0, g, pl.ds(k0, tk), :])
        return m_new, acc_new

    def sel_pair(jj, carries, causal):
        return tuple(sel_tile(g, 2 * jj + 1, sel_tile(g, 2 * jj, carries[g], causal), causal) for g in groups)

    init = tuple((jnp.full((rows, 1), NEG_BIG, F32), jnp.zeros((rows, LANES), F32)) for _ in groups)
    last_pair = (q0 // tk) // 2
    carries = sel_pair(last_pair, init, True)
    carries = lax.fori_loop(0, last_pair, functools.partial(sel_pair, causal=False), carries)

    outs = []
    for g in groups:
        _, o_cmp, o_win = fronts[g]
        acc = carries[g][1]
        o_sel = acc[:, :HEAD_DIM] * (1.0 / acc[:, HEAD_DIM:HEAD_DIM + 1])
        gt = gate_ref[0, :, g * LANES:(g + 1) * LANES]
        for hh in range(NSA_HPG):
            sl = slice(hh * Q_BLOCK, (hh + 1) * Q_BLOCK)
            c = hh * N_BRANCH
            outs.append(o_cmp[sl] * gt[:, c:c + 1] + o_sel[sl] * gt[:, c + 1:c + 2]
                        + o_win[sl] * gt[:, c + 2:c + 3])
    o_ref[0] = jnp.concatenate(outs, axis=1).astype(BF16)


def _nsa(q_hm, ck, cv, ks, vs, kw, vw, gates, w_score):
    B, _, L, _ = q_hm.shape
    assert L // SEL_BLOCK <= LANES and (L // SEL_KV_TILE) % 2 == 0 and L >= WINDOW + Q_BLOCK
    n_cmp = ck.shape[2]
    grid = (B, L // Q_BLOCK)
    qspec = pl.BlockSpec((1, NSA_HEADS, Q_BLOCK, HEAD_DIM), lambda b, i: (b, 0, i, 0))
    cspec = pl.BlockSpec((1, NSA_GROUPS, n_cmp, HEAD_DIM), lambda b, i: (b, 0, 0, 0))
    kvspec = lambda w: pl.BlockSpec((1, NSA_GROUPS, L, w), lambda b, i: (b, 0, 0, 0),
                                    pipeline_mode=pl.Buffered(1))
    gspec = pl.BlockSpec((1, Q_BLOCK, NSA_GROUPS * LANES), lambda b, i: (b, i, 0))
    ospec = pl.BlockSpec((1, Q_BLOCK, NSA_WIDTH), lambda b, i: (b, i, 0))
    return pl.pallas_call(
        _nsa_kernel, grid=grid,
        in_specs=[qspec, cspec, cspec, kvspec(2 * LANES), kvspec(LANES), kvspec(HEAD_DIM), kvspec(LANES), gspec,
                  _full_spec(w_score.shape)],
        out_specs=ospec, out_shape=jax.ShapeDtypeStruct((B, L, NSA_WIDTH), BF16),
        compiler_params=_cparams("parallel", "arbitrary"), name="nsa",
    )(q_hm, ck, cv, ks, vs, kw, vw, gates, w_score)


def _s5_kernel(u_ref, wb_ref, wc_ref, are_ref, aim_ref, d_ref, y_ref, sre_ref, sim_ref, carry_ref):
    n_b, chunk, _ = u_ref.shape
    n_tile = wb_ref.shape[0]
    in_per = n_tile // (S5_WIDTH // LANES)
    pitch = S5_PITCH

    @pl.when(pl.program_id(0) == 0)
    def _():
        carry_ref[...] = jnp.zeros_like(carry_ref)

    for b in range(n_b):
        for c in range(n_tile):
            i = c // in_per
            ub = u_ref[b, :, i * LANES:(i + 1) * LANES].astype(BF16)
            r = _dot(ub, wb_ref[c])
            sre_ref[b, c * pitch:c * pitch + chunk, :] = r[:, :LANES]
            sim_ref[b, c * pitch:c * pitch + chunk, :] = r[:, LANES:]

    a_re, a_im = are_ref[...], aim_ref[...]

    def step(t, carry):
        out = []
        for b in range(n_b):
            s_re, s_im = carry[2 * b], carry[2 * b + 1]
            rows = pl.ds(t, n_tile, stride=pitch)
            n_re = a_re * s_re - a_im * s_im + sre_ref[b, rows, :]
            n_im = a_re * s_im + a_im * s_re + sim_ref[b, rows, :]
            sre_ref[b, rows, :] = n_re
            sim_ref[b, rows, :] = n_im
            out += [n_re, n_im]
        return tuple(out)

    init = tuple(carry_ref[i] for i in range(2 * n_b))
    fin = lax.fori_loop(0, chunk, step, init, unroll=8)
    for i in range(2 * n_b):
        carry_ref[i] = fin[i]

    for b in range(n_b):
        for o in range(S5_WIDTH // LANES):
            acc = jnp.zeros((chunk, LANES), F32)
            for c in range(o * in_per, (o + 1) * in_per):
                rows = slice(c * pitch, c * pitch + chunk)
                state = jnp.concatenate([sre_ref[b, rows, :], sim_ref[b, rows, :]], axis=1).astype(BF16)
                acc = acc + _dot(state, wc_ref[c])
            lanes = slice(o * LANES, (o + 1) * LANES)
            y = acc + d_ref[:, lanes] * u_ref[b, :, lanes]
            y_ref[b, :, lanes] = _gelu_tanh(y).astype(BF16)


def _s5(u, wb, wc, a_re, a_im, d_skip):
    B, L, W = u.shape
    chunk = S5_CHUNK
    n_tile = wb.shape[0]
    blk = pl.BlockSpec((B, chunk, W), lambda i: (0, i, 0))
    slab = pltpu.VMEM((B, n_tile * S5_PITCH, LANES), F32)
    return pl.pallas_call(
        _s5_kernel, grid=(L // chunk,),
        in_specs=[blk] + [_full_spec(w.shape) for w in (wb, wc, a_re, a_im, d_skip)],
        out_specs=blk, out_shape=jax.ShapeDtypeStruct((B, L, W), BF16),
        scratch_shapes=[slab, slab, pltpu.VMEM((2 * B, n_tile, LANES), F32)],
        compiler_params=_cparams("arbitrary"), name="s5",
    )(u, wb, wc, a_re, a_im, d_skip)


def _memkv_kernel(mem_ref, w_ref, k_ref, v_ref):
    kv = _dot(mem_ref[0].astype(BF16), w_ref[...])
    k_ref[0] = kv[:, :MEM_WIDTH].astype(BF16)
    v_ref[0] = kv[:, MEM_WIDTH:].astype(BF16)


def _memkv(mem, w_kv):
    B, M, D = mem.shape
    out = pl.BlockSpec((1, M, MEM_WIDTH), lambda b: (b, 0, 0))
    sd = jax.ShapeDtypeStruct((B, M, MEM_WIDTH), BF16)
    return pl.pallas_call(
        _memkv_kernel, grid=(B,),
        in_specs=[pl.BlockSpec((1, M, D), lambda b: (b, 0, 0)), _full_spec(w_kv.shape)],
        out_specs=[out, out], out_shape=[sd, sd], compiler_params=_cparams("parallel"), name="memkv",
    )(mem, w_kv)


def _memory_attention(q_ref, k_ref, v_ref):
    outs = []
    for h in range(MEM_HEADS):
        sl = slice(h * MEM_HEAD_DIM, (h + 1) * MEM_HEAD_DIM)
        s = _dot_nt(q_ref[:, sl], k_ref[0, :, sl]) * (MEM_HEAD_DIM ** -0.5)
        m = jnp.max(s, axis=-1, keepdims=True)
        e = jnp.exp(s - m)
        p = e / jnp.sum(e, axis=-1, keepdims=True)
        outs.append(_dot(p.astype(BF16), v_ref[0, :, sl]))
    return jnp.concatenate(outs, axis=1).astype(BF16)


def _merge_kernel(x_ref, lng_ref, lnb_ref, on_ref, gy_ref, qm_ref, km_ref, vm_ref, gm_ref,
                  wn_ref, wglu_ref, wmo_ref, wo_ref, l1g_ref, l1b_ref,
                  wrh_ref, wrp_ref, br_ref, tri_ref, striu_ref,
                  h1_ref, lp_ref, w4_ref, cnt_ref):
    D = x_ref.shape[1]
    tm = x_ref.shape[0]
    h =_layer_norm(x_ref[...], lng_ref[...], lnb_ref[...])
    y_nsa = _dot(on_ref[...], wn_ref[...])
    glu = _dot(gy_ref[...], wglu_ref[...])
    y_s5 = glu[:, :D] * jax.nn.sigmoid(glu[:, D:])
    y_mem = _dot(_memory_attention(qm_ref, km_ref, vm_ref), wmo_ref[...])
    merged = gm_ref[:, 0:D] * y_nsa + gm_ref[:, D:2 * D] * y_s5 + gm_ref[:, 2 * D:3 * D] * y_mem
    mix = _dot(merged.astype(BF16), wo_ref[...])
    h1 = _layer_norm(DEEPNORM_ALPHA * h + mix, l1g_ref[...], l1b_ref[...])
    h1_ref[...] = h1

    hh = h1.astype(BF16)
    hl = (h1 - hh.astype(F32)).astype(BF16)
    both = _dot(hh, wrp_ref[...])
    logits = both[:, :LANES] + both[:, LANES:] + _dot(hl, wrh_ref[...]) + br_ref[...]
    lane = lax.broadcasted_iota(I32, (tm, LANES), 1)
    lane_f = lane.astype(F32)
    work = logits
    multi = jnp.zeros((tm, LANES), F32)
    vals, picks = [], []
    for _ in range(TOP_K):
        m = jnp.max(work, axis=-1, keepdims=True)
        idx = jnp.min(jnp.where(work == m, lane_f, float(LANES)), axis=-1, keepdims=True)
        pick = lane_f == idx
        vals.append(m)
        picks.append((pick, idx))
        multi = jnp.where(pick, 1.0, multi)
        work = jnp.where(pick, -jnp.inf, work)
    es = [jnp.exp(v - vals[0]) for v in vals]
    den = es[0] + es[1] + es[2] + es[3]
    st = TOKEN_TILE
    pos = []
    for t in range(tm // st):
        multi_t = multi[t * st:(t + 1) * st]
        cnt = jnp.broadcast_to(jnp.sum(multi_t, axis=0, keepdims=True), (SUBLANES, LANES))
        cnt_ref[t] = cnt
        lower = _dot(cnt.astype(BF16), striu_ref[...])[0:1]
        pos.append(lower + _dot(tri_ref[...], multi_t.astype(BF16)))
    pos = jnp.concatenate(pos, axis=0)
    lp = jnp.full((tm, LANES), -1.0, F32)
    w4 = jnp.zeros((tm, LANES), F32)
    for k in range(TOP_K):
        pick, _ = picks[k]
        lp = jnp.where(lane == k, jnp.sum(jnp.where(pick, pos, 0.0), axis=-1, keepdims=True), lp)
        w4 = jnp.where(lane == k, es[k] / den, w4)
    lp_ref[...] = lp
    w4_ref[...] = w4


def _merge(x2, lng, lnb, o_nsa, gy, qm, k_mem, v_mem, gm, wn, wglu, wmo, wo, l1g, l1b, wrh, wrp, br, tri, striu):
    T, D = x2.shape
    tm = MERGE_SORT_TILES * TOKEN_TILE
    tok = lambda w: pl.BlockSpec((tm, w), lambda i: (i, 0))
    steps_per_batch = T // k_mem.shape[0] // tm
    mem_kv = pl.BlockSpec((1,) + k_mem.shape[1:], lambda i: (i // steps_per_batch, 0, 0))
    ws = [wn, wglu, wmo, wo, l1g, l1b, wrh, wrp, br, tri, striu]
    sd = jax.ShapeDtypeStruct
    lane_out = sd((T, LANES), F32)
    return pl.pallas_call(
        _merge_kernel, grid=(T // tm,),
        in_specs=[tok(D), _full_spec((1, D)), _full_spec((1, D)), tok(NSA_WIDTH), tok(S5_WIDTH),
                  tok(MEM_WIDTH), mem_kv, mem_kv, tok(N_BRANCH * D)] + [_full_spec(w.shape) for w in ws],
        out_specs=[tok(D), tok(LANES), tok(LANES),
                   pl.BlockSpec((MERGE_SORT_TILES, SUBLANES, LANES), lambda i: (i, 0, 0))],
        out_shape=[sd((T, D), F32), lane_out, lane_out, sd((T // TOKEN_TILE, SUBLANES, LANES), F32)],
        compiler_params=_cparams("parallel"), name="merge",
    )(x2, lng, lnb, o_nsa, gy, qm, k_mem, v_mem, gm, *ws)


def _slots_kernel(cnt_ref, triu_ref, striu_ref, tril_ref, seg_ref, blk_ref, misc_ref):
    n_blk = blk_ref.shape[0]
    cnt = cnt_ref[...]
    cnt_b = cnt.astype(BF16)
    total = jnp.sum(cnt, axis=0, keepdims=True)
    nblk_e = jnp.floor((total + (MOE_ROWS - 1)) * (1.0 / MOE_ROWS))
    nblk_8 = jnp.broadcast_to(nblk_e, (SUBLANES, LANES))
    end_b = _dot(nblk_8.astype(BF16), triu_ref[...])
    start_rows = (end_b - nblk_8)[0:1] * MOE_ROWS
    dst = start_rows + _dot(tril_ref[...], cnt_b)
    off = _dot(cnt_b, striu_ref[...])
    seg_ref[0] = cnt.astype(I32)
    seg_ref[1] = off.astype(I32)
    seg_ref[2] = dst.astype(I32)
    blk_i = lax.broadcasted_iota(I32, (n_blk, LANES), 0).astype(F32)
    lane_b = lax.broadcasted_iota(I32, (n_blk, LANES), 1)
    ended = jnp.where((end_b[0:1] <= blk_i) & (lane_b < N_EXPERTS), 1.0, 0.0)
    owner = jnp.minimum(jnp.sum(ended, axis=-1, keepdims=True), float(N_EXPERTS - 1))
    mine = lane_b.astype(F32) == owner
    pick = lambda per_expert: jnp.sum(jnp.where(mine, per_expert, 0.0), axis=-1, keepdims=True)
    earlier = blk_i[:, 0:1] - (pick(end_b[0:1]) - pick(nblk_e))
    held = jnp.clip(pick(total) - earlier * MOE_ROWS, 0.0, float(MOE_ROWS))
    blk_ref[...] = jnp.where(lane_b == 1, held, owner).astype(I32)
    cand = lax.broadcasted_iota(I32, (LANES, LANES), 0)
    has_blocks = jnp.broadcast_to(nblk_e, (LANES, LANES)).T > 0.0
    later = (cand > lax.broadcasted_iota(I32, (LANES, LANES), 1)) & has_blocks
    nxt = jnp.min(jnp.where(later, cand.astype(F32), float(LANES)), axis=0, keepdims=True)
    nxt = jnp.where(nxt < float(LANES), nxt, -1.0)
    lane8 = lax.broadcasted_iota(I32, (SUBLANES, LANES), 1)
    row8 = lax.broadcasted_iota(I32, (SUBLANES, LANES), 0)
    used = jnp.sum(jnp.where(lane8 == N_EXPERTS - 1, end_b, 0.0), axis=-1, keepdims=True)
    misc = jnp.where(row8 == 0, used,
                     jnp.where(row8 == 1, start_rows + total,
                               jnp.where(row8 == 2, nblk_e * MOE_ROWS - total, nxt)))
    misc_ref[...] = misc.astype(I32)


def _slots(cnt, triu, striu, tril, n_blk):
    n_tile = cnt.shape[0]
    sd = jax.ShapeDtypeStruct
    return pl.pallas_call(
        _slots_kernel, grid=(1,),
        in_specs=[_full_spec(cnt.shape), _full_spec(triu.shape), _full_spec(striu.shape), _full_spec(tril.shape)],
        out_specs=[_full_spec((3, n_tile, LANES)), _full_spec((n_blk, LANES)), _full_spec((SUBLANES, LANES))],
        out_shape=[sd((3, n_tile, LANES), I32), sd((n_blk, LANES), I32), sd((SUBLANES, LANES), I32)],
        compiler_params=_cparams("arbitrary"), name="slots",
    )(cnt, triu, striu, tril)


ROW_TILES = D_MODEL // LANES


def _row_span(row, n_rows):
    start = row * ROW_TILES
    if not isinstance(start, int):
        start = pl.multiple_of(start, ROW_TILES)
    return pl.ds(start, n_rows * ROW_TILES)


def _store_rows(ref, val):
    for c in range(ROW_TILES):
        ref[pl.ds(c, val.shape[0], stride=ROW_TILES), :] = val[:, c * LANES:(c + 1) * LANES]


def _load_row_tile(ref, n_rows, c):
    return ref[pl.ds(c, n_rows, stride=ROW_TILES), :]


BIG_PIECE_ROWS = 64


def _pieces(count, max_rows, fn):
    def run(sizes):
        for p in sizes:
            def piece(p=p):
                fn(count & (-2 * p), p)
            pl.when((count & p) != 0)(piece)

    sizes = [max_rows >> s for s in range(max_rows.bit_length())]
    big = [p for p in sizes if p >= BIG_PIECE_ROWS]
    if big:
        pl.when(count >= BIG_PIECE_ROWS)(lambda: run(big))
    run([p for p in sizes if p < BIG_PIECE_ROWS])


def _start_segment_copies(seg_ref, max_rows, make_copy):
    def per_expert(e, c):
        cnt, off, dst = seg_ref[0, 0, 0, e], seg_ref[1, 0, 0, e], seg_ref[2, 0, 0, e]
        _pieces(cnt, max_rows, lambda first, rows: make_copy(off + first, dst + first, rows).start())
        return c

    lax.fori_loop(0, N_EXPERTS, per_expert, 0)


def _dispatch_kernel(seg_ref, misc_ref, lp_ref, h_ref, xs_ref, sorted_ref, zero_ref, sem, pad_sem):
    i = pl.program_id(0)
    n = pl.num_programs(0)
    tm, D = h_ref.shape
    rows = TOP_K * tm
    slot = lax.rem(i, 2)

    def row_copy(slot_):
        def make(src_row, dst_row, n_rows):
            return pltpu.make_async_copy(sorted_ref.at[slot_, _row_span(src_row, n_rows)],
                                         xs_ref.at[_row_span(dst_row, n_rows)], sem.at[slot_])
        return make

    @pl.when(i == 0)
    def _():
        zero_ref[...] = jnp.zeros_like(zero_ref)
        for wait in (False, True):
            def per_expert(e, c, wait=wait):
                def one(first, n_rows):
                    cp = pltpu.make_async_copy(zero_ref.at[_row_span(0, n_rows)],
                                               xs_ref.at[_row_span(misc_ref[1, e] + first, n_rows)], pad_sem)
                    cp.wait() if wait else cp.start()
                _pieces(misc_ref[2, e], MOE_ROWS // 2, one)
                return c
            lax.fori_loop(0, N_EXPERTS, per_expert, 0)

            def per_spare_half_block(hb, c, wait=wait):
                cp = pltpu.make_async_copy(zero_ref, xs_ref.at[_row_span(hb * (MOE_ROWS // 2), MOE_ROWS // 2)],
                                           pad_sem)
                cp.wait() if wait else cp.start()
                return c
            lax.fori_loop(2 * misc_ref[0, 0], 2 * (xs_ref.shape[0] // (MOE_ROWS * ROW_TILES)),
                          per_spare_half_block, 0)

    lp_t = lp_ref[...].T
    s_ix = lax.broadcasted_iota(I32, (rows, 1), 0).astype(F32)
    hit = s_ix == lp_t[0:1, :]
    for k in range(1, TOP_K):
        hit = hit | (s_ix == lp_t[k:k + 1, :])
    perm = jnp.where(hit, 1.0, 0.0).astype(BF16)
    _store_rows(sorted_ref.at[slot], _dot(perm, h_ref[...].astype(BF16)))

    _start_segment_copies(seg_ref, tm, row_copy(slot))

    @pl.when(i > 0)
    def _():
        row_copy(1 - slot)(0, 0, rows).wait()

    @pl.when(i == n - 1)
    def _():
        row_copy(slot)(0, 0, rows).wait()


def _seg_spec(index_map):
    return pl.BlockSpec((3, 1, 1, LANES), index_map, memory_space=pltpu.SMEM)


def _dispatch(seg4, misc, lp, h1, cap):
    T, D = h1.shape
    assert D == ROW_TILES * LANES
    tm = TOKEN_TILE
    tok = lambda w: pl.BlockSpec((tm, w), lambda i: (i, 0))
    return pl.pallas_call(
        _dispatch_kernel, grid=(T // tm,),
        in_specs=[_seg_spec(lambda i: (0, i, 0, 0)), pl.BlockSpec(memory_space=pltpu.SMEM), tok(LANES), tok(D)],
        out_specs=pl.BlockSpec(memory_space=pl.ANY),
        out_shape=jax.ShapeDtypeStruct((cap * ROW_TILES, LANES), F32),
        scratch_shapes=[pltpu.VMEM((2, TOP_K * tm * ROW_TILES, LANES), F32),
                        pltpu.VMEM((MOE_ROWS // 2 * ROW_TILES, LANES), F32),
                        pltpu.SemaphoreType.DMA((2,)), pltpu.SemaphoreType.DMA(())],
        compiler_params=_cparams("arbitrary"), name="dispatch",
    )(seg4, misc, lp, h1)


def _expert_kernel(blk_ref, used_ref, next_ref, rows_ref, xs_ref, wgu_hbm, bgu_ref, wd_hbm, bd_ref, ys_ref,
                   wgu_f32, wd_f32, wgu_bf, wd_bf, sem, run_ref):
    i = pl.program_id(0)
    live = i < used_ref[0]
    expert = blk_ref[i]

    def weight_copies(e, slot):
        return (pltpu.make_async_copy(wgu_hbm.at[e], wgu_f32.at[slot], sem.at[0, slot]),
                pltpu.make_async_copy(wd_hbm.at[e], wd_f32.at[slot], sem.at[1, slot]))

    @pl.when(i == 0)
    def _():
        run_ref[0] = 0
        for cp in weight_copies(expert, 0):
            cp.start()

    @pl.when(live & ((i == 0) | (expert != blk_ref[jnp.maximum(i - 1, 0)])))
    def _():
        slot = lax.rem(run_ref[0], 2)
        for cp in weight_copies(expert, slot):
            cp.wait()
        wgu_bf[...] = wgu_f32[slot].astype(BF16)
        wd_bf[...] = wd_f32[slot].astype(BF16)
        nxt = next_ref[expert]

        @pl.when(nxt >= 0)
        def _():
            for cp in weight_copies(nxt, 1 - slot):
                cp.start()

        run_ref[0] = run_ref[0] + 1

    def expert_rows(n_rows):
        xb = jnp.concatenate([_load_row_tile(xs_ref, n_rows, c).astype(BF16) for c in range(ROW_TILES)], axis=1)
        gu = _dot(xb, wgu_bf[...]) + bgu_ref[0]
        g = jnp.minimum(gu[:, :D_FF], SWIGLU_LIMIT)
        lin = jnp.clip(gu[:, D_FF:], -SWIGLU_LIMIT, SWIGLU_LIMIT)
        act = g * jax.nn.sigmoid(SWIGLU_ALPHA * g) * (lin + 1.0)
        _store_rows(ys_ref, _dot(act.astype(BF16), wd_bf[...]) + bd_ref[0])

    half = MOE_ROWS // 2
    half_full = rows_ref[i] <= half

    @pl.when(live & jnp.logical_not(half_full))
    def _():
        expert_rows(MOE_ROWS)

    @pl.when(live & half_full)
    def _():
        expert_rows(half)
        ys_ref[half * ROW_TILES:, :] = jnp.zeros((half * ROW_TILES, LANES), F32)

    @pl.when(pl.program_id(0) >= used_ref[0])
    def _():
        ys_ref[...] = jnp.zeros_like(ys_ref)


def _experts(blk_expert, n_used, next_expert, blk_rows, xs, w_gate_up, b_gate_up, w_down, b_down):
    D = w_down.shape[2]
    n_blk = xs.shape[0] // (MOE_ROWS * ROW_TILES)
    E = w_gate_up.shape[0]
    live = lambda i, used: jnp.minimum(i, used[0] - 1)
    row = pl.BlockSpec((MOE_ROWS * ROW_TILES, LANES), lambda i, blk, used, nxt, held: (live(i, used), 0))
    by_e = lambda shape: pl.BlockSpec((1,) + shape, lambda i, blk, used, nxt, held: (blk[live(i, used)], 0, 0))
    in_hbm = pl.BlockSpec(memory_space=pl.ANY)
    grid_spec = pltpu.PrefetchScalarGridSpec(
        num_scalar_prefetch=4, grid=(n_blk,),
        in_specs=[row, in_hbm, by_e((1, 2 * D_FF)), in_hbm, by_e((1, D))],
        out_specs=pl.BlockSpec((MOE_ROWS * ROW_TILES, LANES), lambda i, blk, used, nxt, held: (i, 0)),
        scratch_shapes=[pltpu.VMEM((2, D, 2 * D_FF), F32), pltpu.VMEM((2, D_FF, D), F32),
                        pltpu.VMEM((D, 2 * D_FF), BF16), pltpu.VMEM((D_FF, D), BF16),
                        pltpu.SemaphoreType.DMA((2, 2)), pltpu.SMEM((1,), I32)])
    return pl.pallas_call(
        _expert_kernel, grid_spec=grid_spec, out_shape=jax.ShapeDtypeStruct(xs.shape, F32),
        compiler_params=_cparams("arbitrary"), name="experts",
    )(blk_expert, n_used, next_expert, blk_rows, xs, w_gate_up, b_gate_up.reshape(E, 1, 2 * D_FF), w_down,
      b_down.reshape(E, 1, D))


def _combine_kernel(seg_ref, segn_ref, lp_ref, w4_ref, h1_ref, g_ref, b_ref, ys_ref, o_ref, buf_ref, sem):
    i = pl.program_id(0)
    n = pl.num_programs(0)
    tm = h1_ref.shape[0]
    rows = TOP_K * tm
    slot = lax.rem(i, 2)

    def row_copy(slot_):
        def make(buf_row, ys_row, n_rows):
            return pltpu.make_async_copy(ys_ref.at[_row_span(ys_row, n_rows)],
                                         buf_ref.at[slot_, _row_span(buf_row, n_rows)], sem.at[slot_])
        return make

    @pl.when(i == 0)
    def _():
        _start_segment_copies(seg_ref, tm, row_copy(slot))

    @pl.when(i + 1 < n)
    def _():
        _start_segment_copies(segn_ref, tm, row_copy(1 - slot))

    row_copy(slot)(0, 0, rows).wait()

    lp = lp_ref[...]
    s_ix = lax.broadcasted_iota(I32, (1, rows), 1).astype(F32)
    wmat = jnp.zeros((tm, rows), F32)
    for k in range(TOP_K):
        wmat = jnp.where(s_ix == lp[:, k:k + 1], w4_ref[:, k:k + 1], wmat)
    w_hi = wmat.astype(BF16)
    w_lo = (wmat - w_hi.astype(F32)).astype(BF16)
    y = jnp.concatenate([_load_row_tile(buf_ref.at[slot], rows, c) for c in range(ROW_TILES)], axis=1)
    y_hi = y.astype(BF16)
    y_lo = (y - y_hi.astype(F32)).astype(BF16)
    acc = DEEPNORM_ALPHA * h1_ref[...] + (_dot(w_hi, y_hi) + _dot(w_hi, y_lo) + _dot(w_lo, y_hi))
    o_ref[...] = _layer_norm(acc, g_ref[...], b_ref[...])


def _combine(seg4, lp, w4, h1, ln_g, ln_b, ys):
    T, D = h1.shape
    tm = TOKEN_TILE
    n_tile = T // tm
    return pl.pallas_call(
        _combine_kernel, grid=(n_tile,),
        in_specs=[_seg_spec(lambda i: (0, i, 0, 0)),
                  _seg_spec(lambda i: (0, jnp.minimum(i + 1, n_tile - 1), 0, 0)),
                  pl.BlockSpec((tm, LANES), lambda i: (i, 0)),
                  pl.BlockSpec((tm, LANES), lambda i: (i, 0)),
                  pl.BlockSpec((tm, D), lambda i: (i, 0)),
                  _full_spec((1, D)), _full_spec((1, D)),
                  pl.BlockSpec(memory_space=pl.ANY)],
        out_specs=pl.BlockSpec((tm, D), lambda i: (i, 0)),
        out_shape=jax.ShapeDtypeStruct((T, D), F32),
        scratch_shapes=[pltpu.VMEM((2, TOP_K * tm * ROW_TILES, LANES), F32), pltpu.SemaphoreType.DMA((2,))],
        compiler_params=_cparams("arbitrary"), name="combine",
    )(seg4, seg4, lp, w4, h1, ln_g, ln_b, ys)


def _rope_tables(positions):
    inv = ROPE_THETA ** (-jnp.arange(0, ROT_DIM, 2, dtype=F32) / ROT_DIM)
    ang = positions.astype(F32)[..., None] * inv
    cos_sin = jnp.concatenate([jnp.cos(ang), jnp.sin(ang)], axis=-1)
    half = ROT_DIM // 2
    spread = np.zeros((ROT_DIM, 3 * LANES), np.float32)
    unit = np.ones((1, LANES), np.float32)
    for lane in range(LANES):
        d = lane % HEAD_DIM
        if d < half:
            spread[d, lane] = 1.0
            spread[half + d, 2 * LANES + lane] = -1.0
            unit[0, lane] = 0.0
        elif d < ROT_DIM:
            spread[d - half, lane] = 1.0
            spread[d, LANES + lane] = 1.0
            unit[0, lane] = 0.0
    return cos_sin, jnp.asarray(spread, BF16), jnp.asarray(unit)


def _split_w_in(w_in):
    widths = (NSA_WIDTH,) + (KV_WIDTH,) * 6 + (NSA_HEADS * N_BRANCH, S5_WIDTH, MEM_WIDTH, N_BRANCH * D_MODEL)
    offs = [0]
    for w in widths:
        offs.append(offs[-1] + w)
    col = lambda i: w_in[:, offs[i]:offs[i + 1]]
    wq, kc, vc, ks, vs, kw, vw, wg, wu, wqm, wm = (col(i) for i in range(11))
    wk = jnp.concatenate([kc, ks, kw], axis=1)
    wv = jnp.concatenate([vc, vs, vw], axis=1)
    per_group = NSA_HPG * N_BRANCH
    wg_pad = jnp.zeros((w_in.shape[0], NSA_GROUPS * LANES), w_in.dtype)
    for g in range(NSA_GROUPS):
        wg_pad = wg_pad.at[:, g * LANES:g * LANES + per_group].set(wg[:, g * per_group:(g + 1) * per_group])
    return tuple(w.astype(BF16) for w in (wq, wk, wv, wg_pad, wu, wqm, wm))


def _compress_weights(w1):
    half = CMP_BLOCK // 2
    eye = np.eye(NSA_GROUPS, dtype=np.float32)

    def arrange(w_half):
        full = jnp.einsum('sdf,gh->sgdhf', w_half, eye)
        return full.reshape(half * NSA_GROUPS * HEAD_DIM, NSA_GROUPS * CMP_HIDDEN).astype(BF16)

    return (w1.reshape(CMP_BLOCK * HEAD_DIM, CMP_HIDDEN).astype(BF16), arrange(w1[:half]), arrange(w1[half:]))


def _s5_weights(a_re, a_im, log_dt, b_re, b_im, c_re, c_im):
    step = jnp.exp(log_dt)[:, None]
    mag = jnp.exp(a_re * step)
    ab_re, ab_im = mag * jnp.cos(a_im * step), mag * jnp.sin(a_im * step)
    den = a_re * a_re + a_im * a_im
    nr = ab_re - 1.0
    coef_re = (nr * a_re + ab_im * a_im) / den
    coef_im = (ab_im * a_re - nr * a_im) / den
    bb_re = coef_re[..., None] * b_re - coef_im[..., None] * b_im
    bb_im = coef_re[..., None] * b_im + coef_im[..., None] * b_re
    n_tile = S5_GROUPS * S5_STATE // LANES
    tile_groups = LANES // S5_STATE
    lane_groups = LANES // S5_GROUP_DIM
    tiles_per_lane_tile = lane_groups // tile_groups
    place = np.zeros((n_tile, lane_groups, tile_groups), np.float32)
    for c in range(n_tile):
        for j in range(tile_groups):
            place[c, (c % tiles_per_lane_tile) * tile_groups + j, j] = 1.0

    def in_blocks(bb):
        pairs = bb.reshape(n_tile, tile_groups, S5_STATE, S5_GROUP_DIM)
        return jnp.einsum('cjnp,caj->capjn', pairs, place).reshape(n_tile, LANES, LANES)

    def out_blocks(c):
        pairs = c.reshape(n_tile, tile_groups, S5_GROUP_DIM, S5_STATE)
        return jnp.einsum('cjpn,caj->cjnap', pairs, place).reshape(n_tile, LANES, LANES)

    wb = jnp.concatenate([in_blocks(bb_re), in_blocks(bb_im)], axis=2).astype(BF16)
    wc = jnp.concatenate([out_blocks(c_re), out_blocks(-c_im)], axis=1).astype(BF16)
    return wb, wc, ab_re.reshape(n_tile, LANES), ab_im.reshape(n_tile, LANES)


def _layer(x, mem, positions, ln_emb_g, ln_emb_b, w_in, pe_k, pe_v, w_kcmp1, w_kcmp2, w_vcmp1, w_vcmp2,
           s5_a_re, s5_a_im, s5_log_dt, s5_b_re, s5_b_im, s5_c_re, s5_c_im, s5_d,
           w_s5_glu, w_mem_kv, w_nsa_out, w_mem_out, w_o, ln1_g, ln1_b, w_router, b_router,
           w_gate_up, b_gate_up, w_down, b_down, ln2_g, ln2_b):
    B, L, D = x.shape
    T = B * L
    row = lambda v: v.reshape(1, -1)

    cos_sin, spread, unit = _rope_tables(positions)
    (q_hm, kc, vc, ks, vs, kw, vw, gates, u, qm, gm) = _inproj(
        x, row(ln_emb_g), row(ln_emb_b), cos_sin, spread, unit, *_split_w_in(w_in))

    n_chunk = L // CMP_STRIDE
    chunked = lambda t: t.reshape(B, n_chunk, CMP_STRIDE * KV_WIDTH)
    pe_rows = lambda pe: jnp.broadcast_to(pe.reshape(1, -1), (SUBLANES, CMP_BLOCK * HEAD_DIM)).astype(BF16)
    wk1f, wk1a, wk1b = _compress_weights(w_kcmp1)
    wv1f, wv1a, wv1b = _compress_weights(w_vcmp1)
    ck, cv = _compress(chunked(kc), chunked(vc), pe_rows(pe_k), pe_rows(pe_v), wk1f, wv1f,
                       wk1a, wk1b, wv1a, wv1b, w_kcmp2.astype(BF16), w_vcmp2.astype(BF16))

    per_sb = SEL_BLOCK // CMP_STRIDE
    c_ix = np.arange(n_chunk)[:, None]
    n_ix = np.arange(L // SEL_BLOCK)[None, :]
    w_score = jnp.asarray((c_ix // per_sb == n_ix).astype(np.float32)
                          + ((c_ix + 1) // per_sb == n_ix).astype(np.float32), BF16)
    o_nsa = _nsa(q_hm, ck, cv, ks, vs, kw, vw, gates, w_score)

    wb, wc, a_re, a_im = _s5_weights(s5_a_re, s5_a_im, s5_log_dt, s5_b_re, s5_b_im, s5_c_re, s5_c_im)
    gy = _s5(u, wb, wc, a_re, a_im, row(s5_d))

    k_mem, v_mem = _memkv(mem, w_mem_kv.astype(BF16))

    pad_e = LANES - N_EXPERTS
    wr = jnp.pad(w_router, ((0, 0), (0, pad_e)))
    wr_hi = wr.astype(BF16)
    wr_pair = jnp.concatenate([wr_hi, (wr - wr_hi.astype(F32)).astype(BF16)], axis=1)
    br = jnp.concatenate([b_router, jnp.full((pad_e,), -jnp.inf, F32)]).reshape(1, LANES)
    tm = TOKEN_TILE
    n_tile = T // tm
    strict_lower = lambda n: jnp.asarray(np.tril(np.ones((n, n), np.float32), -1), BF16)
    triu = jnp.asarray(np.triu(np.ones((LANES, LANES), np.float32)), BF16)
    striu = jnp.asarray(np.triu(np.ones((LANES, LANES), np.float32), 1), BF16)
    flat = lambda t: t.reshape(T, t.shape[-1])
    h1, lp, w4, cnt = _merge(
        flat(x), row(ln_emb_g), row(ln_emb_b), flat(o_nsa), flat(gy), flat(qm), k_mem, v_mem, flat(gm),
        w_nsa_out.astype(BF16), w_s5_glu.astype(BF16), w_mem_out.astype(BF16), w_o.astype(BF16),
        row(ln1_g), row(ln1_b), wr_hi, wr_pair, br, strict_lower(tm), striu)

    cap = (T * TOP_K + MOE_ROWS - 1) // MOE_ROWS * MOE_ROWS + N_EXPERTS * MOE_ROWS
    n_blk = cap // MOE_ROWS
    seg, blk_owner, misc = _slots(cnt[:, 0, :], triu, striu, strict_lower(n_tile), n_blk)
    seg4 = seg.reshape(3, n_tile, 1, LANES)
    blk_expert = blk_owner[:, 0]
    n_used = misc[0, :1]

    xs = _dispatch(seg4, misc, lp, h1, cap)
    ys = _experts(blk_expert, n_used, misc[3, :N_EXPERTS], blk_owner[:, 1], xs, w_gate_up, b_gate_up, w_down,
                  b_down)
    out = _combine(seg4, lp, w4, h1, row(ln2_g), row(ln2_b), ys)
    return out.reshape(B, L, D)


def kernel(x, mem, positions, ln_emb_g, ln_emb_b, w_in, pe_k_cmp, pe_v_cmp, w_kcmp1, w_kcmp2, w_vcmp1, w_vcmp2, s5_a_re, s5_a_im, s5_log_dt, s5_b_re, s5_b_im, s5_c_re, s5_c_im, s5_d, w_s5_glu, w_mem_kv, w_nsa_out, w_mem_out, w_o, ln1_g, ln1_b, w_router, b_router, w_gate_up, b_gate_up, w_down, b_down, ln2_g, ln2_b):
    assert w_in.shape[0] == DEPTH
    l = 0
    return _layer(x, mem, positions, ln_emb_g, ln_emb_b, w_in[l], pe_k_cmp[l], pe_v_cmp[l], w_kcmp1[l],
                  w_kcmp2[l], w_vcmp1[l], w_vcmp2[l], s5_a_re[l], s5_a_im[l], s5_log_dt[l], s5_b_re[l],
                  s5_b_im[l], s5_c_re[l], s5_c_im[l], s5_d[l], w_s5_glu[l], w_mem_kv[l], w_nsa_out[l],
                  w_mem_out[l], w_o[l], ln1_g[l], ln1_b[l], w_router[l], b_router[l], w_gate_up[l],
                  b_gate_up[l], w_down[l], b_down[l], ln2_g[l], ln2_b[l])
```

```python
import functools
import math

import jax
import jax.numpy as jnp
import numpy as np
from jax import lax
from jax.experimental import pallas as pl
from jax.experimental.pallas import tpu as pltpu

F32 = jnp.float32
BF16 = jnp.bfloat16
I32 = jnp.int32

D_MODEL = 1024
NSA_HEADS = 8
NSA_GROUPS = 2
NSA_HPG = NSA_HEADS // NSA_GROUPS
HEAD_DIM = 64
NSA_WIDTH = NSA_HEADS * HEAD_DIM
KV_WIDTH = NSA_GROUPS * HEAD_DIM
CMP_BLOCK = 32
CMP_STRIDE = 16
CMP_HIDDEN = 128
SEL_BLOCK = 64
N_SEL = 16
WINDOW = 512
Q_BLOCK = 256
WINDOW_Q = 128
ROPE_THETA = 500000.0
ROT_DIM = HEAD_DIM // 4
S5_WIDTH = 512
S5_GROUP_DIM = 16
S5_GROUPS = S5_WIDTH // S5_GROUP_DIM
S5_STATE = 64
MEM_HEADS = 4
MEM_HEAD_DIM = 128
MEM_WIDTH = MEM_HEADS * MEM_HEAD_DIM
N_BRANCH = 3
N_EXPERTS = 32
TOP_K = 4
D_FF = 1024
SWIGLU_LIMIT = 7.0
SWIGLU_ALPHA = 1.702
LN_EPS = 1e-5
DEPTH = 1
DEEPNORM_ALPHA = (2 * DEPTH) ** 0.25

LANES = 128
SUBLANES = 8
VMEM_LIMIT_BYTES = 56 * 1024 * 1024

TOKEN_TILE = 256
INPROJ_TOKEN_TILE = 512
MERGE_SORT_TILES = 2
SEL_KV_TILE = 512
S5_CHUNK = 512
S5_PITCH = S5_CHUNK + 8
MOE_ROWS = 512
NEG_BIG = -(2.0 ** 100)
Q_SCALE_LOG2 = HEAD_DIM ** -0.5 * math.log2(math.e)


def _cparams(*sem):
    return pltpu.CompilerParams(dimension_semantics=sem, vmem_limit_bytes=VMEM_LIMIT_BYTES)


def _dot(a, b):
    return jnp.dot(a, b, preferred_element_type=F32)


def _dot_nt(a, b):
    return lax.dot_general(a, b, (((1,), (1,)), ((), ())), preferred_element_type=F32)


def _layer_norm(x, g, b):
    mu = jnp.mean(x, axis=-1, keepdims=True)
    xc = x - mu
    var = jnp.mean(xc * xc, axis=-1, keepdims=True)
    return xc * lax.rsqrt(var + LN_EPS) * g + b


def _gelu_tanh(x):
    cdf = 0.5 * (1.0 + jnp.tanh(math.sqrt(2.0 / math.pi) * (x + 0.044715 * (x * x * x))))
    return x * cdf


def _masked_exp2(s, mask):
    s = jnp.where(mask, s, -jnp.inf)
    m = jnp.max(s, axis=-1, keepdims=True)
    m = jnp.where(m > -jnp.inf, m, 0.0)
    return jnp.exp2(s - m)


def _safe_recip(denom):
    return 1.0 / jnp.maximum(denom, jnp.finfo(F32).tiny)


def _split3(x):
    hi = x.astype(BF16)
    r1 = x - hi.astype(F32)
    mid = r1.astype(BF16)
    lo = (r1 - mid.astype(F32)).astype(BF16)
    return hi, mid, lo


def _full_spec(shape):
    nd = len(shape)
    return pl.BlockSpec(shape, lambda *_: (0,) * nd)


def _inproj_kernel(x_ref, g_ref, b_ref, cs_ref, spread_ref, unit_ref,
                   wq_ref, wk_ref, wv_ref, wg_ref, wu_ref, wqm_ref, wm_ref,
                   q_ref, kc_ref, vc_ref, ks_ref, vs_ref, kw_ref, vw_ref,
                   gate_ref, u_ref, qm_ref, gm_ref):
    h = _layer_norm(x_ref[0], g_ref[...], b_ref[...])
    hb = h.astype(BF16)
    tab = sum(_dot(part, spread_ref[...]) for part in _split3(cs_ref[0]))
    cos_t = tab[:, 0:LANES] + unit_ref[...]
    sin_a = tab[:, LANES:2 * LANES]
    sin_b = tab[:, 2 * LANES:3 * LANES]

    def rope(t):
        return (t * cos_t + pltpu.roll(t, ROT_DIM // 2, 1) * sin_a
                + pltpu.roll(t, LANES - ROT_DIM // 2, 1) * sin_b)

    q = _dot(hb, wq_ref[...])
    for c in range(NSA_WIDTH // LANES):
        qc = rope(q[:, c * LANES:(c + 1) * LANES]) * Q_SCALE_LOG2
        for hh in range(2):
            q_ref[0, 2 * c + hh] = qc[:, hh * HEAD_DIM:(hh + 1) * HEAD_DIM].astype(BF16)
    k3 = _dot(hb, wk_ref[...])
    kc = rope(k3[:, 0:LANES])
    ks = rope(k3[:, LANES:2 * LANES])
    kw = rope(k3[:, 2 * LANES:3 * LANES])
    v3 = _dot(hb, wv_ref[...])
    kc_ref[0] = kc.astype(BF16)
    vc_ref[0] = v3[:, 0:LANES].astype(BF16)
    tm = x_ref.shape[1]
    pos = pl.program_id(1) * tm + lax.broadcasted_iota(I32, (tm, LANES), 0)
    blk_hot = jnp.where(lax.broadcasted_iota(I32, (tm, LANES), 1) == pos // SEL_BLOCK, 1.0, 0.0)
    lane_pad = jnp.zeros((tm, LANES - HEAD_DIM), F32)
    ones_pad = jnp.where(lax.broadcasted_iota(I32, (tm, LANES - HEAD_DIM), 1) == 0, 1.0, 0.0)
    for g in range(NSA_GROUPS):
        sl = slice(g * HEAD_DIM, (g + 1) * HEAD_DIM)
        ks_ref[0, g] = jnp.concatenate([blk_hot, ks[:, sl], lane_pad], axis=1).astype(BF16)
        kw_ref[0, g] = kw[:, sl].astype(BF16)
        vs_ref[0, g] = jnp.concatenate([v3[:, LANES:2 * LANES][:, sl], ones_pad], axis=1).astype(BF16)
        vw_ref[0, g] = jnp.concatenate([v3[:, 2 * LANES:3 * LANES][:, sl], ones_pad], axis=1).astype(BF16)
    gate_ref[0] = jax.nn.sigmoid(_dot(hb, wg_ref[...]))
    u_ref[0] = _dot(hb, wu_ref[...])
    qm_ref[0] = _dot(hb, wqm_ref[...]).astype(BF16)
    gm_ref[0] = jax.nn.sigmoid(_dot(hb, wm_ref[...]))


def _inproj(x, ln_g, ln_b, cos_sin, spread, unit, wq, wk, wv, wg, wu, wqm, wm):
    B, L, D = x.shape
    tm = INPROJ_TOKEN_TILE
    grid = (B, L // tm)
    tok = lambda w: pl.BlockSpec((1, tm, w), lambda b, i: (b, i, 0))
    head = lambda n, w=HEAD_DIM: pl.BlockSpec((1, n, tm, w), lambda b, i: (b, 0, i, 0))
    in_specs = [tok(D), _full_spec((1, D)), _full_spec((1, D)), tok(ROT_DIM), _full_spec(spread.shape),
                _full_spec(unit.shape)]
    in_specs += [pl.BlockSpec(w.shape, lambda b, i: (0, 0), pipeline_mode=pl.Buffered(1))
                 for w in (wq, wk, wv, wg, wu, wqm, wm)]
    sd = jax.ShapeDtypeStruct
    out_shape = [
        sd((B, NSA_HEADS, L, HEAD_DIM), BF16),
        sd((B, L, KV_WIDTH), BF16), sd((B, L, KV_WIDTH), BF16),
        sd((B, NSA_GROUPS, L, 2 * LANES), BF16), sd((B, NSA_GROUPS, L, LANES), BF16),
        sd((B, NSA_GROUPS, L, HEAD_DIM), BF16), sd((B, NSA_GROUPS, L, LANES), BF16),
        sd((B, L, NSA_GROUPS * LANES), F32),
        sd((B, L, S5_WIDTH), F32),
        sd((B, L, MEM_WIDTH), BF16),
        sd((B, L, N_BRANCH * D), F32),
    ]
    out_specs = [head(NSA_HEADS), tok(KV_WIDTH), tok(KV_WIDTH), head(NSA_GROUPS, 2 * LANES),
                 head(NSA_GROUPS, LANES), head(NSA_GROUPS), head(NSA_GROUPS, LANES),
                 tok(NSA_GROUPS * LANES), tok(S5_WIDTH),
                 tok(MEM_WIDTH), tok(N_BRANCH * D)]
    return pl.pallas_call(
        _inproj_kernel, grid=grid, in_specs=in_specs, out_specs=out_specs, out_shape=out_shape,
        compiler_params=_cparams("parallel", "parallel"), name="inproj",
    )(x, ln_g, ln_b, cos_sin, spread, unit, wq, wk, wv, wg, wu, wqm, wm)


def _compress_kernel(kc_ref, vc_ref, pek_ref, pev_ref, wk1f_ref, wv1f_ref,
                     wk1a_ref, wk1b_ref, wv1a_ref, wv1b_ref, wk2_ref, wv2_ref, ck_ref, cv_ref):
    n_chunk = kc_ref.shape[1]
    row = lax.broadcasted_iota(I32, (n_chunk, 1), 0)

    def one(x_ref, pe_ref, w1f_ref, w1a_ref, w1b_ref, w2_ref, o_ref):
        x = x_ref[0]
        first = _dot(x, w1a_ref[...])
        second = _dot(x, w1b_ref[...])
        second = pltpu.roll(second, n_chunk - 1, 0)
        pe_term = _dot(pe_ref[...], w1f_ref[...])[0:1]
        pe_term = jnp.concatenate([pe_term] * NSA_GROUPS, axis=1)
        hid = _gelu_tanh(first + second + pe_term).astype(BF16)
        for g in range(NSA_GROUPS):
            o = _dot(hid[:, g * CMP_HIDDEN:(g + 1) * CMP_HIDDEN], w2_ref[...])
            o_ref[0, g] = jnp.where(row < n_chunk - 1, o, 0.0).astype(BF16)

    one(kc_ref, pek_ref, wk1f_ref, wk1a_ref, wk1b_ref, wk2_ref, ck_ref)
    one(vc_ref, pev_ref, wv1f_ref, wv1a_ref, wv1b_ref, wv2_ref, cv_ref)


def _compress(kc_r, vc_r, pek, pev, wk1f, wv1f, wk1a, wk1b, wv1a, wv1b, wk2, wv2):
    B, n_chunk, width = kc_r.shape
    blk = pl.BlockSpec((1, n_chunk, width), lambda b: (b, 0, 0))
    out = pl.BlockSpec((1, NSA_GROUPS, n_chunk, HEAD_DIM), lambda b: (b, 0, 0, 0))
    ws = [pek, pev, wk1f, wv1f, wk1a, wk1b, wv1a, wv1b, wk2, wv2]
    sd = jax.ShapeDtypeStruct((B, NSA_GROUPS, n_chunk, HEAD_DIM), BF16)
    return pl.pallas_call(
        _compress_kernel, grid=(B,), in_specs=[blk, blk] + [_full_spec(w.shape) for w in ws],
        out_specs=[out, out], out_shape=[sd, sd], compiler_params=_cparams("parallel"), name="compress",
    )(kc_r, vc_r, *ws)


def _nsa_kernel(q_ref, ck_ref, cv_ref, ks_ref, vs_ref, kw_ref, vw_ref, gate_ref, wsc_ref, o_ref):
    seq_len = ks_ref.shape[2]
    n_cmp = ck_ref.shape[2]
    n_sb = seq_len // SEL_BLOCK
    n_sel = min(N_SEL, n_sb)
    rows = NSA_HPG * Q_BLOCK
    groups = range(NSA_GROUPS)
    q0 = pl.program_id(1) * Q_BLOCK
    t1 = q0 + lax.broadcasted_iota(I32, (Q_BLOCK, 1), 0)
    t4 = jnp.concatenate([t1] * NSA_HPG, axis=0)
    tk = SEL_KV_TILE

    def front(g):
        q = q_ref[0, g * NSA_HPG:(g + 1) * NSA_HPG].reshape(rows, HEAD_DIM)

        s = _dot_nt(q, ck_ref[0, g])
        c_end = lax.broadcasted_iota(I32, (1, n_cmp), 1) * CMP_STRIDE + (CMP_BLOCK - 1)
        e = _masked_exp2(s, c_end <= t4)
        p_cmp = e * _safe_recip(jnp.sum(e, axis=-1, keepdims=True))
        o_cmp = _dot(p_cmp.astype(BF16), cv_ref[0, g])

        imp = p_cmp[0:Q_BLOCK]
        for hh in range(1, NSA_HPG):
            imp = imp + p_cmp[hh * Q_BLOCK:(hh + 1) * Q_BLOCK]
        w_sc = wsc_ref[...]
        score = sum(_dot(part, w_sc) for part in _split3(imp))
        score_t = score.T
        jb = lax.broadcasted_iota(I32, (n_sb, Q_BLOCK), 0)
        tb = (q0 + lax.broadcasted_iota(I32, (1, Q_BLOCK), 1)) // SEL_BLOCK
        forced = (jb == 0) | (jb == tb) | (jb == tb - 1)
        work = jnp.where(forced | (jb > tb), -jnp.inf, score_t)
        bias_t = jnp.where(forced, 0.0, NEG_BIG)
        jbf = jb.astype(F32)
        for _ in range(n_sel - 3):
            m = jnp.max(work, axis=0, keepdims=True)
            idx = jnp.min(jnp.where(work == m, jbf, float(n_sb)), axis=0, keepdims=True)
            pick = jbf == idx
            bias_t = jnp.where(pick, 0.0, bias_t)
            work = jnp.where(pick, -jnp.inf, work)
        sel_bias = bias_t.T
        if n_sb < LANES:
            sel_bias = jnp.concatenate([sel_bias, jnp.zeros((Q_BLOCK, LANES - n_sb), F32)], axis=1)

        span = WINDOW + WINDOW_Q
        parts = []
        for sub in range(Q_BLOCK // WINDOW_Q):
            pick = lambda a: jnp.concatenate(
                [a[hh * Q_BLOCK + sub * WINDOW_Q:hh * Q_BLOCK + (sub + 1) * WINDOW_Q] for hh in range(NSA_HPG)],
                axis=0)
            w0 = pl.multiple_of(jnp.maximum(q0 + sub * WINDOW_Q - WINDOW, 0), WINDOW_Q)
            s = _dot_nt(pick(q), kw_ref[0, g, pl.ds(w0, span), :])
            diff = pick(t4) - (w0 + lax.broadcasted_iota(I32, (1, span), 1))
            e = _masked_exp2(s, (diff >= 0) & (diff < WINDOW))
            o = _dot(e.astype(BF16), vw_ref[0, g, pl.ds(w0, span), :])
            parts.append(o[:, :HEAD_DIM] * _safe_recip(o[:, HEAD_DIM:HEAD_DIM + 1]))
        o_win = jnp.concatenate([parts[sub][hh * WINDOW_Q:(hh + 1) * WINDOW_Q]
                                 for hh in range(NSA_HPG) for sub in range(len(parts))], axis=0)

        q_aug = jnp.concatenate([jnp.concatenate([sel_bias.astype(BF16)] * NSA_HPG, axis=0), q,
                                 jnp.zeros((rows, LANES - HEAD_DIM), BF16)], axis=1)
        return q_aug, o_cmp, o_win

    fronts = [front(g) for g in groups]

    def sel_tile(g, j, carry, causal):
        m_run, acc = carry
        k0 = pl.multiple_of(j * tk, tk)
        sc = _dot_nt(fronts[g][0], ks_ref[0, g, pl.ds(k0, tk), :])
        if causal:
            kpos = k0 + lax.broadcasted_iota(I32, (1, tk), 1)
            sc = jnp.where(kpos <= t4, sc, NEG_BIG)
        m_new = jnp.maximum(m_run, jnp.max(sc, axis=-1, keepdims=True))
        p = jnp.exp2(sc - m_new)
        acc_new = jnp.exp2(m_run - m_new) * acc + _dot(p.astype(BF16), vs_ref[0, g, pl.ds(k0, tk), :])
        return m_new, acc_new

    def sel_pair(jj, carries, causal):
        return tuple(sel_tile(g, 2 * jj + 1, sel_tile(g, 2 * jj, carries[g], causal), causal) for g in groups)

    init = tuple((jnp.full((rows, 1), NEG_BIG, F32), jnp.zeros((rows, LANES), F32)) for _ in groups)
    last_pair = (q0 // tk) // 2
    carries = sel_pair(last_pair, init, True)
    carries = lax.fori_loop(
        0, last_pair // 2, lambda jq, cs: sel_pair(2 * jq + 1, sel_pair(2 * jq, cs, False), False), carries)
    carries = lax.fori_loop(0, last_pair % 2, lambda _, cs: sel_pair(last_pair - 1, cs, False), carries)

    outs = []
    for g in groups:
        _, o_cmp, o_win = fronts[g]
        acc = carries[g][1]
        o_sel = acc[:, :HEAD_DIM] * (1.0 / acc[:, HEAD_DIM:HEAD_DIM + 1])
        gt = gate_ref[0, :, g * LANES:(g + 1) * LANES]
        for hh in range(NSA_HPG):
            sl = slice(hh * Q_BLOCK, (hh + 1) * Q_BLOCK)
            c = hh * N_BRANCH
            outs.append(o_cmp[sl] * gt[:, c:c + 1] + o_sel[sl] * gt[:, c + 1:c + 2]
                        + o_win[sl] * gt[:, c + 2:c + 3])
    o_ref[0] = jnp.concatenate(outs, axis=1).astype(BF16)


def _nsa(q_hm, ck, cv, ks, vs, kw, vw, gates, w_score):
    B, _, L, _ = q_hm.shape
    assert L // SEL_BLOCK <= LANES and (L // SEL_KV_TILE) % 2 == 0 and L >= WINDOW + Q_BLOCK
    n_cmp = ck.shape[2]
    grid = (B, L // Q_BLOCK)
    qspec = pl.BlockSpec((1, NSA_HEADS, Q_BLOCK, HEAD_DIM), lambda b, i: (b, 0, i, 0))
    cspec = pl.BlockSpec((1, NSA_GROUPS, n_cmp, HEAD_DIM), lambda b, i: (b, 0, 0, 0))
    kvspec = lambda w: pl.BlockSpec((1, NSA_GROUPS, L, w), lambda b, i: (b, 0, 0, 0),
                                    pipeline_mode=pl.Buffered(1))
    gspec = pl.BlockSpec((1, Q_BLOCK, NSA_GROUPS * LANES), lambda b, i: (b, i, 0))
    ospec = pl.BlockSpec((1, Q_BLOCK, NSA_WIDTH), lambda b, i: (b, i, 0))
    return pl.pallas_call(
        _nsa_kernel, grid=grid,
        in_specs=[qspec, cspec, cspec, kvspec(2 * LANES), kvspec(LANES), kvspec(HEAD_DIM), kvspec(LANES), gspec,
                  _full_spec(w_score.shape)],
        out_specs=ospec, out_shape=jax.ShapeDtypeStruct((B, L, NSA_WIDTH), BF16),
        compiler_params=_cparams("parallel", "arbitrary"), name="nsa",
    )(q_hm, ck, cv, ks, vs, kw, vw, gates, w_score)


def _s5_kernel(u_ref, wb_ref, wc_ref, are_ref, aim_ref, d_ref, y_ref, sre_ref, sim_ref, carry_ref):
    n_b, chunk, _ = u_ref.shape
    n_tile = wb_ref.shape[0]
    in_per = n_tile // (S5_WIDTH // LANES)
    pitch = S5_PITCH

    @pl.when(pl.program_id(0) == 0)
    def _():
        carry_ref[...] = jnp.zeros_like(carry_ref)

    for b in range(n_b):
        for c in range(n_tile):
            i = c // in_per
            ub = u_ref[b, :, i * LANES:(i + 1) * LANES].astype(BF16)
            r = _dot(ub, wb_ref[c])
            sre_ref[b, c * pitch:c * pitch + chunk, :] = r[:, :LANES]
            sim_ref[b, c * pitch:c * pitch + chunk, :] = r[:, LANES:]

    a_re, a_im = are_ref[...], aim_ref[...]

    def step(t, carry):
        out = []
        for b in range(n_b):
            s_re, s_im = carry[2 * b], carry[2 * b + 1]
            rows = pl.ds(t, n_tile, stride=pitch)
            n_re = a_re * s_re - a_im * s_im + sre_ref[b, rows, :]
            n_im = a_re * s_im + a_im * s_re + sim_ref[b, rows, :]
            sre_ref[b, rows, :] = n_re
            sim_ref[b, rows, :] = n_im
            out += [n_re, n_im]
        return tuple(out)

    init = tuple(carry_ref[i] for i in range(2 * n_b))
    fin = lax.fori_loop(0, chunk, step, init, unroll=8)
    for i in range(2 * n_b):
        carry_ref[i] = fin[i]

    for b in range(n_b):
        for o in range(S5_WIDTH // LANES):
            acc = jnp.zeros((chunk, LANES), F32)
            for c in range(o * in_per, (o + 1) * in_per):
                rows = slice(c * pitch, c * pitch + chunk)
                state = jnp.concatenate([sre_ref[b, rows, :], sim_ref[b, rows, :]], axis=1).astype(BF16)
                acc = acc + _dot(state, wc_ref[c])
            lanes = slice(o * LANES, (o + 1) * LANES)
            y = acc + d_ref[:, lanes] * u_ref[b, :, lanes]
            y_ref[b, :, lanes] = _gelu_tanh(y).astype(BF16)


def _s5(u, wb, wc, a_re, a_im, d_skip):
    B, L, W = u.shape
    chunk = S5_CHUNK
    n_tile = wb.shape[0]
    blk = pl.BlockSpec((B, chunk, W), lambda i: (0, i, 0))
    slab = pltpu.VMEM((B, n_tile * S5_PITCH, LANES), F32)
    return pl.pallas_call(
        _s5_kernel, grid=(L // chunk,),
        in_specs=[blk] + [_full_spec(w.shape) for w in (wb, wc, a_re, a_im, d_skip)],
        out_specs=blk, out_shape=jax.ShapeDtypeStruct((B, L, W), BF16),
        scratch_shapes=[slab, slab, pltpu.VMEM((2 * B, n_tile, LANES), F32)],
        compiler_params=_cparams("arbitrary"), name="s5",
    )(u, wb, wc, a_re, a_im, d_skip)


def _memkv_kernel(mem_ref, w_ref, k_ref, v_ref):
    kv = _dot(mem_ref[0].astype(BF16), w_ref[...])
    k_ref[0] = kv[:, :MEM_WIDTH].astype(BF16)
    v_ref[0] = kv[:, MEM_WIDTH:].astype(BF16)


def _memkv(mem, w_kv):
    B, M, D = mem.shape
    out = pl.BlockSpec((1, M, MEM_WIDTH), lambda b: (b, 0, 0))
    sd = jax.ShapeDtypeStruct((B, M, MEM_WIDTH), BF16)
    return pl.pallas_call(
        _memkv_kernel, grid=(B,),
        in_specs=[pl.BlockSpec((1, M, D), lambda b: (b, 0, 0)), _full_spec(w_kv.shape)],
        out_specs=[out, out], out_shape=[sd, sd], compiler_params=_cparams("parallel"), name="memkv",
    )(mem, w_kv)


def _memory_attention(q_ref, k_ref, v_ref):
    outs = []
    for h in range(MEM_HEADS):
        sl = slice(h * MEM_HEAD_DIM, (h + 1) * MEM_HEAD_DIM)
        s = _dot_nt(q_ref[:, sl], k_ref[0, :, sl]) * (MEM_HEAD_DIM ** -0.5)
        m = jnp.max(s, axis=-1, keepdims=True)
        e = jnp.exp(s - m)
        p = e / jnp.sum(e, axis=-1, keepdims=True)
        outs.append(_dot(p.astype(BF16), v_ref[0, :, sl]))
    return jnp.concatenate(outs, axis=1).astype(BF16)


def _merge_kernel(x_ref, lng_ref, lnb_ref, on_ref, gy_ref, qm_ref, km_ref, vm_ref, gm_ref,
                  wn_ref, wglu_ref, wmo_ref, wo_ref, l1g_ref, l1b_ref,
                  wrh_ref, wrp_ref, br_ref, tri_ref, striu_ref,
                  h1_ref, lp_ref, w4_ref, cnt_ref):
    D = x_ref.shape[1]
    tm = x_ref.shape[0]
    h =_layer_norm(x_ref[...], lng_ref[...], lnb_ref[...])
    y_nsa = _dot(on_ref[...], wn_ref[...])
    glu = _dot(gy_ref[...], wglu_ref[...])
    y_s5 = glu[:, :D] * jax.nn.sigmoid(glu[:, D:])
    y_mem = _dot(_memory_attention(qm_ref, km_ref, vm_ref), wmo_ref[...])
    merged = gm_ref[:, 0:D] * y_nsa + gm_ref[:, D:2 * D] * y_s5 + gm_ref[:, 2 * D:3 * D] * y_mem
    mix = _dot(merged.astype(BF16), wo_ref[...])
    h1 = _layer_norm(DEEPNORM_ALPHA * h + mix, l1g_ref[...], l1b_ref[...])
    h1_ref[...] = h1

    hh = h1.astype(BF16)
    hl = (h1 - hh.astype(F32)).astype(BF16)
    both = _dot(hh, wrp_ref[...])
    logits = both[:, :LANES] + both[:, LANES:] + _dot(hl, wrh_ref[...]) + br_ref[...]
    lane = lax.broadcasted_iota(I32, (tm, LANES), 1)
    lane_f = lane.astype(F32)
    work = logits
    multi = jnp.zeros((tm, LANES), F32)
    vals, picks = [], []
    for _ in range(TOP_K):
        m = jnp.max(work, axis=-1, keepdims=True)
        idx = jnp.min(jnp.where(work == m, lane_f, float(LANES)), axis=-1, keepdims=True)
        pick = lane_f == idx
        vals.append(m)
        picks.append((pick, idx))
        multi = jnp.where(pick, 1.0, multi)
        work = jnp.where(pick, -jnp.inf, work)
    es = [jnp.exp(v - vals[0]) for v in vals]
    den = es[0] + es[1] + es[2] + es[3]
    st = TOKEN_TILE
    pos = []
    for t in range(tm // st):
        multi_t = multi[t * st:(t + 1) * st]
        cnt = jnp.broadcast_to(jnp.sum(multi_t, axis=0, keepdims=True), (SUBLANES, LANES))
        cnt_ref[t] = cnt
        lower = _dot(cnt.astype(BF16), striu_ref[...])[0:1]
        pos.append(lower + _dot(tri_ref[...], multi_t.astype(BF16)))
    pos = jnp.concatenate(pos, axis=0)
    lp = jnp.full((tm, LANES), -1.0, F32)
    w4 = jnp.zeros((tm, LANES), F32)
    for k in range(TOP_K):
        pick, _ = picks[k]
        lp = jnp.where(lane == k, jnp.sum(jnp.where(pick, pos, 0.0), axis=-1, keepdims=True), lp)
        w4 = jnp.where(lane == k, es[k] / den, w4)
    lp_ref[...] = lp
    w4_ref[...] = w4


def _merge(x2, lng, lnb, o_nsa, gy, qm, k_mem, v_mem, gm, wn, wglu, wmo, wo, l1g, l1b, wrh, wrp, br, tri, striu):
    T, D = x2.shape
    tm = MERGE_SORT_TILES * TOKEN_TILE
    tok = lambda w: pl.BlockSpec((tm, w), lambda i: (i, 0))
    steps_per_batch = T // k_mem.shape[0] // tm
    mem_kv = pl.BlockSpec((1,) + k_mem.shape[1:], lambda i: (i // steps_per_batch, 0, 0))
    ws = [wn, wglu, wmo, wo, l1g, l1b, wrh, wrp, br, tri, striu]
    sd = jax.ShapeDtypeStruct
    lane_out = sd((T, LANES), F32)
    return pl.pallas_call(
        _merge_kernel, grid=(T // tm,),
        in_specs=[tok(D), _full_spec((1, D)), _full_spec((1, D)), tok(NSA_WIDTH), tok(S5_WIDTH),
                  tok(MEM_WIDTH), mem_kv, mem_kv, tok(N_BRANCH * D)] + [_full_spec(w.shape) for w in ws],
        out_specs=[tok(D), tok(LANES), tok(LANES),
                   pl.BlockSpec((MERGE_SORT_TILES, SUBLANES, LANES), lambda i: (i, 0, 0))],
        out_shape=[sd((T, D), F32), lane_out, lane_out, sd((T // TOKEN_TILE, SUBLANES, LANES), F32)],
        compiler_params=_cparams("parallel"), name="merge",
    )(x2, lng, lnb, o_nsa, gy, qm, k_mem, v_mem, gm, *ws)


def _slots_kernel(cnt_ref, triu_ref, striu_ref, tril_ref, seg_ref, blk_ref, misc_ref):
    n_blk = blk_ref.shape[0]
    cnt = cnt_ref[...]
    cnt_b = cnt.astype(BF16)
    total = jnp.sum(cnt, axis=0, keepdims=True)
    nblk_e = jnp.floor((total + (MOE_ROWS - 1)) * (1.0 / MOE_ROWS))
    nblk_8 = jnp.broadcast_to(nblk_e, (SUBLANES, LANES))
    end_b = _dot(nblk_8.astype(BF16), triu_ref[...])
    start_rows = (end_b - nblk_8)[0:1] * MOE_ROWS
    dst = start_rows + _dot(tril_ref[...], cnt_b)
    off = _dot(cnt_b, striu_ref[...])
    seg_ref[0] = cnt.astype(I32)
    seg_ref[1] = off.astype(I32)
    seg_ref[2] = dst.astype(I32)
    blk_i = lax.broadcasted_iota(I32, (n_blk, LANES), 0).astype(F32)
    lane_b = lax.broadcasted_iota(I32, (n_blk, LANES), 1)
    ended = jnp.where((end_b[0:1] <= blk_i) & (lane_b < N_EXPERTS), 1.0, 0.0)
    owner = jnp.minimum(jnp.sum(ended, axis=-1, keepdims=True), float(N_EXPERTS - 1))
    mine = lane_b.astype(F32) == owner
    pick = lambda per_expert: jnp.sum(jnp.where(mine, per_expert, 0.0), axis=-1, keepdims=True)
    earlier = blk_i[:, 0:1] - (pick(end_b[0:1]) - pick(nblk_e))
    held = jnp.clip(pick(total) - earlier * MOE_ROWS, 0.0, float(MOE_ROWS))
    blk_ref[...] = jnp.where(lane_b == 1, held, owner).astype(I32)
    cand = lax.broadcasted_iota(I32, (LANES, LANES), 0)
    has_blocks = jnp.broadcast_to(nblk_e, (LANES, LANES)).T > 0.0
    later = (cand > lax.broadcasted_iota(I32, (LANES, LANES), 1)) & has_blocks
    nxt = jnp.min(jnp.where(later, cand.astype(F32), float(LANES)), axis=0, keepdims=True)
    nxt = jnp.where(nxt < float(LANES), nxt, -1.0)
    lane8 = lax.broadcasted_iota(I32, (SUBLANES, LANES), 1)
    row8 = lax.broadcasted_iota(I32, (SUBLANES, LANES), 0)
    used = jnp.sum(jnp.where(lane8 == N_EXPERTS - 1, end_b, 0.0), axis=-1, keepdims=True)
    misc = jnp.where(row8 == 0, used,
                     jnp.where(row8 == 1, start_rows + total,
                               jnp.where(row8 == 2, nblk_e * MOE_ROWS - total, nxt)))
    misc_ref[...] = misc.astype(I32)


def _slots(cnt, triu, striu, tril, n_blk):
    n_tile = cnt.shape[0]
    sd = jax.ShapeDtypeStruct
    return pl.pallas_call(
        _slots_kernel, grid=(1,),
        in_specs=[_full_spec(cnt.shape), _full_spec(triu.shape), _full_spec(striu.shape), _full_spec(tril.shape)],
        out_specs=[_full_spec((3, n_tile, LANES)), _full_spec((n_blk, LANES)), _full_spec((SUBLANES, LANES))],
        out_shape=[sd((3, n_tile, LANES), I32), sd((n_blk, LANES), I32), sd((SUBLANES, LANES), I32)],
        compiler_params=_cparams("arbitrary"), name="slots",
    )(cnt, triu, striu, tril)


ROW_TILES = D_MODEL // LANES


def _row_span(row, n_rows):
    start = row * ROW_TILES
    if not isinstance(start, int):
        start = pl.multiple_of(start, ROW_TILES)
    return pl.ds(start, n_rows * ROW_TILES)


def _store_rows(ref, val):
    for c in range(ROW_TILES):
        ref[pl.ds(c, val.shape[0], stride=ROW_TILES), :] = val[:, c * LANES:(c + 1) * LANES]


def _load_row_tile(ref, n_rows, c):
    return ref[pl.ds(c, n_rows, stride=ROW_TILES), :]


BIG_PIECE_ROWS = 64


def _pieces(count, max_rows, fn):
    def run(sizes):
        for p in sizes:
            def piece(p=p):
                fn(count & (-2 * p), p)
            pl.when((count & p) != 0)(piece)

    sizes = [max_rows >> s for s in range(max_rows.bit_length())]
    big = [p for p in sizes if p >= BIG_PIECE_ROWS]
    if big:
        pl.when(count >= BIG_PIECE_ROWS)(lambda: run(big))
    run([p for p in sizes if p < BIG_PIECE_ROWS])


def _start_segment_copies(seg_ref, max_rows, make_copy):
    def per_expert(e, c):
        cnt, off, dst = seg_ref[0, 0, 0, e], seg_ref[1, 0, 0, e], seg_ref[2, 0, 0, e]
        _pieces(cnt, max_rows, lambda first, rows: make_copy(off + first, dst + first, rows).start())
        return c

    lax.fori_loop(0, N_EXPERTS, per_expert, 0)


def _dispatch_kernel(seg_ref, misc_ref, lp_ref, h_ref, xs_ref, sorted_ref, zero_ref, sem, pad_sem):
    i = pl.program_id(0)
    n = pl.num_programs(0)
    tm, D = h_ref.shape
    rows = TOP_K * tm
    slot = lax.rem(i, 2)

    def row_copy(slot_):
        def make(src_row, dst_row, n_rows):
            return pltpu.make_async_copy(sorted_ref.at[slot_, _row_span(src_row, n_rows)],
                                         xs_ref.at[_row_span(dst_row, n_rows)], sem.at[slot_])
        return make

    @pl.when(i == 0)
    def _():
        zero_ref[...] = jnp.zeros_like(zero_ref)
        for wait in (False, True):
            def per_expert(e, c, wait=wait):
                def one(first, n_rows):
                    cp = pltpu.make_async_copy(zero_ref.at[_row_span(0, n_rows)],
                                               xs_ref.at[_row_span(misc_ref[1, e] + first, n_rows)], pad_sem)
                    cp.wait() if wait else cp.start()
                _pieces(misc_ref[2, e], MOE_ROWS // 2, one)
                return c
            lax.fori_loop(0, N_EXPERTS, per_expert, 0)

            def per_spare_half_block(hb, c, wait=wait):
                cp = pltpu.make_async_copy(zero_ref, xs_ref.at[_row_span(hb * (MOE_ROWS // 2), MOE_ROWS // 2)],
                                           pad_sem)
                cp.wait() if wait else cp.start()
                return c
            lax.fori_loop(2 * misc_ref[0, 0], 2 * (xs_ref.shape[0] // (MOE_ROWS * ROW_TILES)),
                          per_spare_half_block, 0)

    lp_t = lp_ref[...].T
    s_ix = lax.broadcasted_iota(I32, (rows, 1), 0).astype(F32)
    hit = s_ix == lp_t[0:1, :]
    for k in range(1, TOP_K):
        hit = hit | (s_ix == lp_t[k:k + 1, :])
    perm = jnp.where(hit, 1.0, 0.0).astype(BF16)
    _store_rows(sorted_ref.at[slot], _dot(perm, h_ref[...].astype(BF16)))

    _start_segment_copies(seg_ref, tm, row_copy(slot))

    @pl.when(i > 0)
    def _():
        row_copy(1 - slot)(0, 0, rows).wait()

    @pl.when(i == n - 1)
    def _():
        row_copy(slot)(0, 0, rows).wait()


def _seg_spec(index_map):
    return pl.BlockSpec((3, 1, 1, LANES), index_map, memory_space=pltpu.SMEM)


def _dispatch(seg4, misc, lp, h1, cap):
    T, D = h1.shape
    assert D == ROW_TILES * LANES
    tm = TOKEN_TILE
    tok = lambda w: pl.BlockSpec((tm, w), lambda i: (i, 0))
    return pl.pallas_call(
        _dispatch_kernel, grid=(T // tm,),
        in_specs=[_seg_spec(lambda i: (0, i, 0, 0)), pl.BlockSpec(memory_space=pltpu.SMEM), tok(LANES), tok(D)],
        out_specs=pl.BlockSpec(memory_space=pl.ANY),
        out_shape=jax.ShapeDtypeStruct((cap * ROW_TILES, LANES), F32),
        scratch_shapes=[pltpu.VMEM((2, TOP_K * tm * ROW_TILES, LANES), F32),
                        pltpu.VMEM((MOE_ROWS // 2 * ROW_TILES, LANES), F32),
                        pltpu.SemaphoreType.DMA((2,)), pltpu.SemaphoreType.DMA(())],
        compiler_params=_cparams("arbitrary"), name="dispatch",
    )(seg4, misc, lp, h1)


def _expert_kernel(blk_ref, used_ref, next_ref, rows_ref, xs_ref, wgu_hbm, bgu_ref, wd_hbm, bd_ref, ys_ref,
                   wgu_f32, wd_f32, wgu_bf, wd_bf, sem, run_ref):
    i = pl.program_id(0)
    live = i < used_ref[0]
    expert = blk_ref[i]

    def weight_copies(e, slot):
        return (pltpu.make_async_copy(wgu_hbm.at[e], wgu_f32.at[slot], sem.at[0, slot]),
                pltpu.make_async_copy(wd_hbm.at[e], wd_f32.at[slot], sem.at[1, slot]))

    @pl.when(i == 0)
    def _():
        run_ref[0] = 0
        for cp in weight_copies(expert, 0):
            cp.start()

    @pl.when(live & ((i == 0) | (expert != blk_ref[jnp.maximum(i - 1, 0)])))
    def _():
        slot = lax.rem(run_ref[0], 2)
        for cp in weight_copies(expert, slot):
            cp.wait()
        wgu_bf[...] = wgu_f32[slot].astype(BF16)
        wd_bf[...] = wd_f32[slot].astype(BF16)
        nxt = next_ref[expert]

        @pl.when(nxt >= 0)
        def _():
            for cp in weight_copies(nxt, 1 - slot):
                cp.start()

        run_ref[0] = run_ref[0] + 1

    def expert_rows(n_rows):
        xb = jnp.concatenate([_load_row_tile(xs_ref, n_rows, c).astype(BF16) for c in range(ROW_TILES)], axis=1)
        gu = _dot(xb, wgu_bf[...]) + bgu_ref[0]
        g = jnp.minimum(gu[:, :D_FF], SWIGLU_LIMIT)
        lin = jnp.clip(gu[:, D_FF:], -SWIGLU_LIMIT, SWIGLU_LIMIT)
        act = g * jax.nn.sigmoid(SWIGLU_ALPHA * g) * (lin + 1.0)
        _store_rows(ys_ref, _dot(act.astype(BF16), wd_bf[...]) + bd_ref[0])

    half = MOE_ROWS // 2
    half_full = rows_ref[i] <= half

    @pl.when(live & jnp.logical_not(half_full))
    def _():
        expert_rows(MOE_ROWS)

    @pl.when(live & half_full)
    def _():
        expert_rows(half)
        ys_ref[half * ROW_TILES:, :] = jnp.zeros((half * ROW_TILES, LANES), F32)

    @pl.when(pl.program_id(0) >= used_ref[0])
    def _():
        ys_ref[...] = jnp.zeros_like(ys_ref)


def _experts(blk_expert, n_used, next_expert, blk_rows, xs, w_gate_up, b_gate_up, w_down, b_down):
    D = w_down.shape[2]
    n_blk = xs.shape[0] // (MOE_ROWS * ROW_TILES)
    E = w_gate_up.shape[0]
    live = lambda i, used: jnp.minimum(i, used[0] - 1)
    row = pl.BlockSpec((MOE_ROWS * ROW_TILES, LANES), lambda i, blk, used, nxt, held: (live(i, used), 0))
    by_e = lambda shape: pl.BlockSpec((1,) + shape, lambda i, blk, used, nxt, held: (blk[live(i, used)], 0, 0))
    in_hbm = pl.BlockSpec(memory_space=pl.ANY)
    grid_spec = pltpu.PrefetchScalarGridSpec(
        num_scalar_prefetch=4, grid=(n_blk,),
        in_specs=[row, in_hbm, by_e((1, 2 * D_FF)), in_hbm, by_e((1, D))],
        out_specs=pl.BlockSpec((MOE_ROWS * ROW_TILES, LANES), lambda i, blk, used, nxt, held: (i, 0)),
        scratch_shapes=[pltpu.VMEM((2, D, 2 * D_FF), F32), pltpu.VMEM((2, D_FF, D), F32),
                        pltpu.VMEM((D, 2 * D_FF), BF16), pltpu.VMEM((D_FF, D), BF16),
                        pltpu.SemaphoreType.DMA((2, 2)), pltpu.SMEM((1,), I32)])
    return pl.pallas_call(
        _expert_kernel, grid_spec=grid_spec, out_shape=jax.ShapeDtypeStruct(xs.shape, F32),
        compiler_params=_cparams("arbitrary"), name="experts",
    )(blk_expert, n_used, next_expert, blk_rows, xs, w_gate_up, b_gate_up.reshape(E, 1, 2 * D_FF), w_down,
      b_down.reshape(E, 1, D))


def _combine_kernel(seg_ref, segn_ref, lp_ref, w4_ref, h1_ref, g_ref, b_ref, ys_ref, o_ref, buf_ref, sem):
    i = pl.program_id(0)
    n = pl.num_programs(0)
    tm = h1_ref.shape[0]
    rows = TOP_K * tm
    slot = lax.rem(i, 2)

    def row_copy(slot_):
        def make(buf_row, ys_row, n_rows):
            return pltpu.make_async_copy(ys_ref.at[_row_span(ys_row, n_rows)],
                                         buf_ref.at[slot_, _row_span(buf_row, n_rows)], sem.at[slot_])
        return make

    @pl.when(i == 0)
    def _():
        _start_segment_copies(seg_ref, tm, row_copy(slot))

    @pl.when(i + 1 < n)
    def _():
        _start_segment_copies(segn_ref, tm, row_copy(1 - slot))

    row_copy(slot)(0, 0, rows).wait()

    lp = lp_ref[...]
    s_ix = lax.broadcasted_iota(I32, (1, rows), 1).astype(F32)
    wmat = jnp.zeros((tm, rows), F32)
    for k in range(TOP_K):
        wmat = jnp.where(s_ix == lp[:, k:k + 1], w4_ref[:, k:k + 1], wmat)
    w_hi = wmat.astype(BF16)
    w_lo = (wmat - w_hi.astype(F32)).astype(BF16)
    y = jnp.concatenate([_load_row_tile(buf_ref.at[slot], rows, c) for c in range(ROW_TILES)], axis=1)
    y_hi = y.astype(BF16)
    y_lo = (y - y_hi.astype(F32)).astype(BF16)
    acc = DEEPNORM_ALPHA * h1_ref[...] + (_dot(w_hi, y_hi) + _dot(w_hi, y_lo) + _dot(w_lo, y_hi))
    o_ref[...] = _layer_norm(acc, g_ref[...], b_ref[...])


def _combine(seg4, lp, w4, h1, ln_g, ln_b, ys):
    T, D = h1.shape
    tm = TOKEN_TILE
    n_tile = T // tm
    return pl.pallas_call(
        _combine_kernel, grid=(n_tile,),
        in_specs=[_seg_spec(lambda i: (0, i, 0, 0)),
                  _seg_spec(lambda i: (0, jnp.minimum(i + 1, n_tile - 1), 0, 0)),
                  pl.BlockSpec((tm, LANES), lambda i: (i, 0)),
                  pl.BlockSpec((tm, LANES), lambda i: (i, 0)),
                  pl.BlockSpec((tm, D), lambda i: (i, 0)),
                  _full_spec((1, D)), _full_spec((1, D)),
                  pl.BlockSpec(memory_space=pl.ANY)],
        out_specs=pl.BlockSpec((tm, D), lambda i: (i, 0)),
        out_shape=jax.ShapeDtypeStruct((T, D), F32),
        scratch_shapes=[pltpu.VMEM((2, TOP_K * tm * ROW_TILES, LANES), F32), pltpu.SemaphoreType.DMA((2,))],
        compiler_params=_cparams("arbitrary"), name="combine",
    )(seg4, seg4, lp, w4, h1, ln_g, ln_b, ys)


def _rope_tables(positions):
    inv = ROPE_THETA ** (-jnp.arange(0, ROT_DIM, 2, dtype=F32) / ROT_DIM)
    ang = positions.astype(F32)[..., None] * inv
    cos_sin = jnp.concatenate([jnp.cos(ang), jnp.sin(ang)], axis=-1)
    half = ROT_DIM // 2
    spread = np.zeros((ROT_DIM, 3 * LANES), np.float32)
    unit = np.ones((1, LANES), np.float32)
    for lane in range(LANES):
        d = lane % HEAD_DIM
        if d < half:
            spread[d, lane] = 1.0
            spread[half + d, 2 * LANES + lane] = -1.0
            unit[0, lane] = 0.0
        elif d < ROT_DIM:
            spread[d - half, lane] = 1.0
            spread[d, LANES + lane] = 1.0
            unit[0, lane] = 0.0
    return cos_sin, jnp.asarray(spread, BF16), jnp.asarray(unit)


def _split_w_in(w_in):
    widths = (NSA_WIDTH,) + (KV_WIDTH,) * 6 + (NSA_HEADS * N_BRANCH, S5_WIDTH, MEM_WIDTH, N_BRANCH * D_MODEL)
    offs = [0]
    for w in widths:
        offs.append(offs[-1] + w)
    col = lambda i: w_in[:, offs[i]:offs[i + 1]]
    wq, kc, vc, ks, vs, kw, vw, wg, wu, wqm, wm = (col(i) for i in range(11))
    wk = jnp.concatenate([kc, ks, kw], axis=1)
    wv = jnp.concatenate([vc, vs, vw], axis=1)
    per_group = NSA_HPG * N_BRANCH
    wg_pad = jnp.zeros((w_in.shape[0], NSA_GROUPS * LANES), w_in.dtype)
    for g in range(NSA_GROUPS):
        wg_pad = wg_pad.at[:, g * LANES:g * LANES + per_group].set(wg[:, g * per_group:(g + 1) * per_group])
    return tuple(w.astype(BF16) for w in (wq, wk, wv, wg_pad, wu, wqm, wm))


def _compress_weights(w1):
    half = CMP_BLOCK // 2
    eye = np.eye(NSA_GROUPS, dtype=np.float32)

    def arrange(w_half):
        full = jnp.einsum('sdf,gh->sgdhf', w_half, eye)
        return full.reshape(half * NSA_GROUPS * HEAD_DIM, NSA_GROUPS * CMP_HIDDEN).astype(BF16)

    return (w1.reshape(CMP_BLOCK * HEAD_DIM, CMP_HIDDEN).astype(BF16), arrange(w1[:half]), arrange(w1[half:]))


def _s5_weights(a_re, a_im, log_dt, b_re, b_im, c_re, c_im):
    step = jnp.exp(log_dt)[:, None]
    mag = jnp.exp(a_re * step)
    ab_re, ab_im = mag * jnp.cos(a_im * step), mag * jnp.sin(a_im * step)
    den = a_re * a_re + a_im * a_im
    nr = ab_re - 1.0
    coef_re = (nr * a_re + ab_im * a_im) / den
    coef_im = (ab_im * a_re - nr * a_im) / den
    bb_re = coef_re[..., None] * b_re - coef_im[..., None] * b_im
    bb_im = coef_re[..., None] * b_im + coef_im[..., None] * b_re
    n_tile = S5_GROUPS * S5_STATE // LANES
    tile_groups = LANES // S5_STATE
    lane_groups = LANES // S5_GROUP_DIM
    tiles_per_lane_tile = lane_groups // tile_groups
    place = np.zeros((n_tile, lane_groups, tile_groups), np.float32)
    for c in range(n_tile):
        for j in range(tile_groups):
            place[c, (c % tiles_per_lane_tile) * tile_groups + j, j] = 1.0

    def in_blocks(bb):
        pairs = bb.reshape(n_tile, tile_groups, S5_STATE, S5_GROUP_DIM)
        return jnp.einsum('cjnp,caj->capjn', pairs, place).reshape(n_tile, LANES, LANES)

    def out_blocks(c):
        pairs = c.reshape(n_tile, tile_groups, S5_GROUP_DIM, S5_STATE)
        return jnp.einsum('cjpn,caj->cjnap', pairs, place).reshape(n_tile, LANES, LANES)

    wb = jnp.concatenate([in_blocks(bb_re), in_blocks(bb_im)], axis=2).astype(BF16)
    wc = jnp.concatenate([out_blocks(c_re), out_blocks(-c_im)], axis=1).astype(BF16)
    return wb, wc, ab_re.reshape(n_tile, LANES), ab_im.reshape(n_tile, LANES)


def _layer(x, mem, positions, ln_emb_g, ln_emb_b, w_in, pe_k, pe_v, w_kcmp1, w_kcmp2, w_vcmp1, w_vcmp2,
           s5_a_re, s5_a_im, s5_log_dt, s5_b_re, s5_b_im, s5_c_re, s5_c_im, s5_d,
           w_s5_glu, w_mem_kv, w_nsa_out, w_mem_out, w_o, ln1_g, ln1_b, w_router, b_router,
           w_gate_up, b_gate_up, w_down, b_down, ln2_g, ln2_b):
    B, L, D = x.shape
    T = B * L
    row = lambda v: v.reshape(1, -1)

    cos_sin, spread, unit = _rope_tables(positions)
    (q_hm, kc, vc, ks, vs, kw, vw, gates, u, qm, gm) = _inproj(
        x, row(ln_emb_g), row(ln_emb_b), cos_sin, spread, unit, *_split_w_in(w_in))

    n_chunk = L // CMP_STRIDE
    chunked = lambda t: t.reshape(B, n_chunk, CMP_STRIDE * KV_WIDTH)
    pe_rows = lambda pe: jnp.broadcast_to(pe.reshape(1, -1), (SUBLANES, CMP_BLOCK * HEAD_DIM)).astype(BF16)
    wk1f, wk1a, wk1b = _compress_weights(w_kcmp1)
    wv1f, wv1a, wv1b = _compress_weights(w_vcmp1)
    ck, cv = _compress(chunked(kc), chunked(vc), pe_rows(pe_k), pe_rows(pe_v), wk1f, wv1f,
                       wk1a, wk1b, wv1a, wv1b, w_kcmp2.astype(BF16), w_vcmp2.astype(BF16))

    per_sb = SEL_BLOCK // CMP_STRIDE
    c_ix = np.arange(n_chunk)[:, None]
    n_ix = np.arange(L // SEL_BLOCK)[None, :]
    w_score = jnp.asarray((c_ix // per_sb == n_ix).astype(np.float32)
                          + ((c_ix + 1) // per_sb == n_ix).astype(np.float32), BF16)
    o_nsa = _nsa(q_hm, ck, cv, ks, vs, kw, vw, gates, w_score)

    wb, wc, a_re, a_im = _s5_weights(s5_a_re, s5_a_im, s5_log_dt, s5_b_re, s5_b_im, s5_c_re, s5_c_im)
    gy = _s5(u, wb, wc, a_re, a_im, row(s5_d))

    k_mem, v_mem = _memkv(mem, w_mem_kv.astype(BF16))

    pad_e = LANES - N_EXPERTS
    wr = jnp.pad(w_router, ((0, 0), (0, pad_e)))
    wr_hi = wr.astype(BF16)
    wr_pair = jnp.concatenate([wr_hi, (wr - wr_hi.astype(F32)).astype(BF16)], axis=1)
    br = jnp.concatenate([b_router, jnp.full((pad_e,), -jnp.inf, F32)]).reshape(1, LANES)
    tm = TOKEN_TILE
    n_tile = T // tm
    strict_lower = lambda n: jnp.asarray(np.tril(np.ones((n, n), np.float32), -1), BF16)
    triu = jnp.asarray(np.triu(np.ones((LANES, LANES), np.float32)), BF16)
    striu = jnp.asarray(np.triu(np.ones((LANES, LANES), np.float32), 1), BF16)
    flat = lambda t: t.reshape(T, t.shape[-1])
    h1, lp, w4, cnt = _merge(
        flat(x), row(ln_emb_g), row(ln_emb_b), flat(o_nsa), flat(gy), flat(qm), k_mem, v_mem, flat(gm),
        w_nsa_out.astype(BF16), w_s5_glu.astype(BF16), w_mem_out.astype(BF16), w_o.astype(BF16),
        row(ln1_g), row(ln1_b), wr_hi, wr_pair, br, strict_lower(tm), striu)

    cap = (T * TOP_K + MOE_ROWS - 1) // MOE_ROWS * MOE_ROWS + N_EXPERTS * MOE_ROWS
    n_blk = cap // MOE_ROWS
    seg, blk_owner, misc = _slots(cnt[:, 0, :], triu, striu, strict_lower(n_tile), n_blk)
    seg4 = seg.reshape(3, n_tile, 1, LANES)
    blk_expert = blk_owner[:, 0]
    n_used = misc[0, :1]

    xs = _dispatch(seg4, misc, lp, h1, cap)
    ys = _experts(blk_expert, n_used, misc[3, :N_EXPERTS], blk_owner[:, 1], xs, w_gate_up, b_gate_up, w_down,
                  b_down)
    out = _combine(seg4, lp, w4, h1, row(ln2_g), row(ln2_b), ys)
    return out.reshape(B, L, D)


def kernel(x, mem, positions, ln_emb_g, ln_emb_b, w_in, pe_k_cmp, pe_v_cmp, w_kcmp1, w_kcmp2, w_vcmp1, w_vcmp2, s5_a_re, s5_a_im, s5_log_dt, s5_b_re, s5_b_im, s5_c_re, s5_c_im, s5_d, w_s5_glu, w_mem_kv, w_nsa_out, w_mem_out, w_o, ln1_g, ln1_b, w_router, b_router, w_gate_up, b_gate_up, w_down, b_down, ln2_g, ln2_b):
    assert w_in.shape[0] == DEPTH
    l = 0
    return _layer(x, mem, positions, ln_emb_g, ln_emb_b, w_in[l], pe_k_cmp[l], pe_v_cmp[l], w_kcmp1[l],
                  w_kcmp2[l], w_vcmp1[l], w_vcmp2[l], s5_a_re[l], s5_a_im[l], s5_log_dt[l], s5_b_re[l],
                  s5_b_im[l], s5_c_re[l], s5_c_im[l], s5_d[l], w_s5_glu[l], w_mem_kv[l], w_nsa_out[l],
                  w_mem_out[l], w_o[l], ln1_g[l], ln1_b[l], w_router[l], b_router[l], w_gate_up[l],
                  b_gate_up[l], w_down[l], b_down[l], ln2_g[l], ln2_b[l])
```

```python
import functools
import math

import jax
import jax.numpy as jnp
import numpy as np
from jax import lax
from jax.experimental import pallas as pl
from jax.experimental.pallas import tpu as pltpu

F32 = jnp.float32
BF16 = jnp.bfloat16
I32 = jnp.int32

D_MODEL = 1024
NSA_HEADS = 8
NSA_GROUPS = 2
NSA_HPG = NSA_HEADS // NSA_GROUPS
HEAD_DIM = 64
NSA_WIDTH = NSA_HEADS * HEAD_DIM
KV_WIDTH = NSA_GROUPS * HEAD_DIM
CMP_BLOCK = 32
CMP_STRIDE = 16
CMP_HIDDEN = 128
SEL_BLOCK = 64
N_SEL = 16
WINDOW = 512
Q_BLOCK = 256
WINDOW_Q = 128
ROPE_THETA = 500000.0
ROT_DIM = HEAD_DIM // 4
S5_WIDTH = 512
S5_GROUP_DIM = 16
S5_GROUPS = S5_WIDTH // S5_GROUP_DIM
S5_STATE = 64
MEM_HEADS = 4
MEM_HEAD_DIM = 128
MEM_WIDTH = MEM_HEADS * MEM_HEAD_DIM
N_BRANCH = 3
N_EXPERTS = 32
TOP_K = 4
D_FF = 1024
SWIGLU_LIMIT = 7.0
SWIGLU_ALPHA = 1.702
LN_EPS = 1e-5
DEPTH = 1
DEEPNORM_ALPHA = (2 * DEPTH) ** 0.25

LANES = 128
SUBLANES = 8
VMEM_LIMIT_BYTES = 56 * 1024 * 1024

TOKEN_TILE = 256
INPROJ_TOKEN_TILE = 512
MERGE_SORT_TILES = 2
SEL_KV_TILE = 512
S5_CHUNK = 512
S5_PITCH = S5_CHUNK + 8
MOE_ROWS = 512
NEG_BIG = -(2.0 ** 100)
Q_SCALE_LOG2 = HEAD_DIM ** -0.5 * math.log2(math.e)


def _cparams(*sem):
    return pltpu.CompilerParams(dimension_semantics=sem, vmem_limit_bytes=VMEM_LIMIT_BYTES)


def _dot(a, b):
    return jnp.dot(a, b, preferred_element_type=F32)


def _dot_nt(a, b):
    return lax.dot_general(a, b, (((1,), (1,)), ((), ())), preferred_element_type=F32)


def _layer_norm(x, g, b):
    mu = jnp.mean(x, axis=-1, keepdims=True)
    xc = x - mu
    var = jnp.mean(xc * xc, axis=-1, keepdims=True)
    return xc * lax.rsqrt(var + LN_EPS) * g + b


def _gelu_tanh(x):
    cdf = 0.5 * (1.0 + jnp.tanh(math.sqrt(2.0 / math.pi) * (x + 0.044715 * (x * x * x))))
    return x * cdf


def _masked_exp2(s, mask):
    s = jnp.where(mask, s, -jnp.inf)
    m = jnp.max(s, axis=-1, keepdims=True)
    m = jnp.where(m > -jnp.inf, m, 0.0)
    return jnp.exp2(s - m)


def _safe_recip(denom):
    return 1.0 / jnp.maximum(denom, jnp.finfo(F32).tiny)


def _split3(x):
    hi = x.astype(BF16)
    r1 = x - hi.astype(F32)
    mid = r1.astype(BF16)
    lo = (r1 - mid.astype(F32)).astype(BF16)
    return hi, mid, lo


def _full_spec(shape):
    nd = len(shape)
    return pl.BlockSpec(shape, lambda *_: (0,) * nd)


def _inproj_kernel(x_ref, g_ref, b_ref, cs_ref, spread_ref, unit_ref,
                   wq_ref, wk_ref, wv_ref, wg_ref, wu_ref, wqm_ref, wm_ref,
                   q_ref, kc_ref, vc_ref, ks_ref, vs_ref, kw_ref, vw_ref,
                   gate_ref, u_ref, qm_ref, gm_ref):
    h = _layer_norm(x_ref[0], g_ref[...], b_ref[...])
    hb = h.astype(BF16)
    tab = sum(_dot(part, spread_ref[...]) for part in _split3(cs_ref[0]))
    cos_t = tab[:, 0:LANES] + unit_ref[...]
    sin_a = tab[:, LANES:2 * LANES]
    sin_b = tab[:, 2 * LANES:3 * LANES]

    def rope(t):
        return (t * cos_t + pltpu.roll(t, ROT_DIM // 2, 1) * sin_a
                + pltpu.roll(t, LANES - ROT_DIM // 2, 1) * sin_b)

    q = _dot(hb, wq_ref[...])
    for c in range(NSA_WIDTH // LANES):
        qc = rope(q[:, c * LANES:(c + 1) * LANES]) * Q_SCALE_LOG2
        for hh in range(2):
            q_ref[0, 2 * c + hh] = qc[:, hh * HEAD_DIM:(hh + 1) * HEAD_DIM].astype(BF16)
    k3 = _dot(hb, wk_ref[...])
    kc = rope(k3[:, 0:LANES])
    ks = rope(k3[:, LANES:2 * LANES])
    kw = rope(k3[:, 2 * LANES:3 * LANES])
    v3 = _dot(hb, wv_ref[...])
    kc_ref[0] = kc.astype(BF16)
    vc_ref[0] = v3[:, 0:LANES].astype(BF16)
    tm = x_ref.shape[1]
    pos = pl.program_id(1) * tm + lax.broadcasted_iota(I32, (tm, LANES), 0)
    blk_hot = jnp.where(lax.broadcasted_iota(I32, (tm, LANES), 1) == pos // SEL_BLOCK, 1.0, 0.0)
    lane_pad = jnp.zeros((tm, LANES - HEAD_DIM), F32)
    ones_pad = jnp.where(lax.broadcasted_iota(I32, (tm, LANES - HEAD_DIM), 1) == 0, 1.0, 0.0)
    for g in range(NSA_GROUPS):
        sl = slice(g * HEAD_DIM, (g + 1) * HEAD_DIM)
        ks_ref[0, g] = jnp.concatenate([blk_hot, ks[:, sl], lane_pad], axis=1).astype(BF16)
        kw_ref[0, g] = kw[:, sl].astype(BF16)
        vs_ref[0, g] = jnp.concatenate([v3[:, LANES:2 * LANES][:, sl], ones_pad], axis=1).astype(BF16)
        vw_ref[0, g] = jnp.concatenate([v3[:, 2 * LANES:3 * LANES][:, sl], ones_pad], axis=1).astype(BF16)
    gate_ref[0] = jax.nn.sigmoid(_dot(hb, wg_ref[...]))
    u_ref[0] = _dot(hb, wu_ref[...])
    qm_ref[0] = _dot(hb, wqm_ref[...]).astype(BF16)
    gm_ref[0] = jax.nn.sigmoid(_dot(hb, wm_ref[...]))


def _inproj(x, ln_g, ln_b, cos_sin, spread, unit, wq, wk, wv, wg, wu, wqm, wm):
    B, L, D = x.shape
    tm = INPROJ_TOKEN_TILE
    grid = (B, L // tm)
    tok = lambda w: pl.BlockSpec((1, tm, w), lambda b, i: (b, i, 0))
    head = lambda n, w=HEAD_DIM: pl.BlockSpec((1, n, tm, w), lambda b, i: (b, 0, i, 0))
    in_specs = [tok(D), _full_spec((1, D)), _full_spec((1, D)), tok(ROT_DIM), _full_spec(spread.shape),
                _full_spec(unit.shape)]
    in_specs += [pl.BlockSpec(w.shape, lambda b, i: (0, 0), pipeline_mode=pl.Buffered(1))
                 for w in (wq, wk, wv, wg, wu, wqm, wm)]
    sd = jax.ShapeDtypeStruct
    out_shape = [
        sd((B, NSA_HEADS, L, HEAD_DIM), BF16),
        sd((B, L, KV_WIDTH), BF16), sd((B, L, KV_WIDTH), BF16),
        sd((B, NSA_GROUPS, L, 2 * LANES), BF16), sd((B, NSA_GROUPS, L, LANES), BF16),
        sd((B, NSA_GROUPS, L, HEAD_DIM), BF16), sd((B, NSA_GROUPS, L, LANES), BF16),
        sd((B, L, NSA_GROUPS * LANES), F32),
        sd((B, L, S5_WIDTH), F32),
        sd((B, L, MEM_WIDTH), BF16),
        sd((B, L, N_BRANCH * D), F32),
    ]
    out_specs = [head(NSA_HEADS), tok(KV_WIDTH), tok(KV_WIDTH), head(NSA_GROUPS, 2 * LANES),
                 head(NSA_GROUPS, LANES), head(NSA_GROUPS), head(NSA_GROUPS, LANES),
                 tok(NSA_GROUPS * LANES), tok(S5_WIDTH),
                 tok(MEM_WIDTH), tok(N_BRANCH * D)]
    return pl.pallas_call(
        _inproj_kernel, grid=grid, in_specs=in_specs, out_specs=out_specs, out_shape=out_shape,
        compiler_params=_cparams("parallel", "parallel"), name="inproj",
    )(x, ln_g, ln_b, cos_sin, spread, unit, wq, wk, wv, wg, wu, wqm, wm)


def _compress_kernel(kc_ref, vc_ref, pek_ref, pev_ref, wk1f_ref, wv1f_ref,
                     wk1a_ref, wk1b_ref, wv1a_ref, wv1b_ref, wk2_ref, wv2_ref, ck_ref, cv_ref):
    n_chunk = kc_ref.shape[1]
    row = lax.broadcasted_iota(I32, (n_chunk, 1), 0)

    def one(x_ref, pe_ref, w1f_ref, w1a_ref, w1b_ref, w2_ref, o_ref):
        x = x_ref[0]
        first = _dot(x, w1a_ref[...])
        second = _dot(x, w1b_ref[...])
        second = pltpu.roll(second, n_chunk - 1, 0)
        pe_term = _dot(pe_ref[...], w1f_ref[...])[0:1]
        pe_term = jnp.concatenate([pe_term] * NSA_GROUPS, axis=1)
        hid = _gelu_tanh(first + second + pe_term).astype(BF16)
        for g in range(NSA_GROUPS):
            o = _dot(hid[:, g * CMP_HIDDEN:(g + 1) * CMP_HIDDEN], w2_ref[...])
            o_ref[0, g] = jnp.where(row < n_chunk - 1, o, 0.0).astype(BF16)

    one(kc_ref, pek_ref, wk1f_ref, wk1a_ref, wk1b_ref, wk2_ref, ck_ref)
    one(vc_ref, pev_ref, wv1f_ref, wv1a_ref, wv1b_ref, wv2_ref, cv_ref)


def _compress(kc_r, vc_r, pek, pev, wk1f, wv1f, wk1a, wk1b, wv1a, wv1b, wk2, wv2):
    B, n_chunk, width = kc_r.shape
    blk = pl.BlockSpec((1, n_chunk, width), lambda b: (b, 0, 0))
    out = pl.BlockSpec((1, NSA_GROUPS, n_chunk, HEAD_DIM), lambda b: (b, 0, 0, 0))
    ws = [pek, pev, wk1f, wv1f, wk1a, wk1b, wv1a, wv1b, wk2, wv2]
    sd = jax.ShapeDtypeStruct((B, NSA_GROUPS, n_chunk, HEAD_DIM), BF16)
    return pl.pallas_call(
        _compress_kernel, grid=(B,), in_specs=[blk, blk] + [_full_spec(w.shape) for w in ws],
        out_specs=[out, out], out_shape=[sd, sd], compiler_params=_cparams("parallel"), name="compress",
    )(kc_r, vc_r, *ws)


def _nsa_kernel(q_ref, ck_ref, cv_ref, ks_ref, vs_ref, kw_ref, vw_ref, gate_ref, wsc_ref, o_ref):
    seq_len = ks_ref.shape[2]
    n_cmp = ck_ref.shape[2]
    n_sb = seq_len // SEL_BLOCK
    n_sel = min(N_SEL, n_sb)
    rows = NSA_HPG * Q_BLOCK
    groups = range(NSA_GROUPS)
    q0 = pl.program_id(1) * Q_BLOCK
    t1 = q0 + lax.broadcasted_iota(I32, (Q_BLOCK, 1), 0)
    t4 = jnp.concatenate([t1] * NSA_HPG, axis=0)
    tk = SEL_KV_TILE

    def front(g):
        q = q_ref[0, g * NSA_HPG:(g + 1) * NSA_HPG].reshape(rows, HEAD_DIM)

        s = _dot_nt(q, ck_ref[0, g])
        c_end = lax.broadcasted_iota(I32, (1, n_cmp), 1) * CMP_STRIDE + (CMP_BLOCK - 1)
        e = _masked_exp2(s, c_end <= t4)
        p_cmp = e * _safe_recip(jnp.sum(e, axis=-1, keepdims=True))
        o_cmp = _dot(p_cmp.astype(BF16), cv_ref[0, g])

        imp = p_cmp[0:Q_BLOCK]
        for hh in range(1, NSA_HPG):
            imp = imp + p_cmp[hh * Q_BLOCK:(hh + 1) * Q_BLOCK]
        w_sc = wsc_ref[...]
        score = sum(_dot(part, w_sc) for part in _split3(imp))
        score_t = score.T
        jb = lax.broadcasted_iota(I32, (n_sb, Q_BLOCK), 0)
        tb = (q0 + lax.broadcasted_iota(I32, (1, Q_BLOCK), 1)) // SEL_BLOCK
        forced = (jb == 0) | (jb == tb) | (jb == tb - 1)
        work = jnp.where(forced | (jb > tb), -jnp.inf, score_t)
        bias_t = jnp.where(forced, 0.0, NEG_BIG)
        jbf = jb.astype(F32)
        for _ in range(n_sel - 3):
            m = jnp.max(work, axis=0, keepdims=True)
            idx = jnp.min(jnp.where(work == m, jbf, float(n_sb)), axis=0, keepdims=True)
            pick = jbf == idx
            bias_t = jnp.where(pick, 0.0, bias_t)
            work = jnp.where(pick, -jnp.inf, work)
        sel_bias = bias_t.T
        if n_sb < LANES:
            sel_bias = jnp.concatenate([sel_bias, jnp.zeros((Q_BLOCK, LANES - n_sb), F32)], axis=1)

        span = WINDOW + WINDOW_Q
        parts = []
        for sub in range(Q_BLOCK // WINDOW_Q):
            pick = lambda a: jnp.concatenate(
                [a[hh * Q_BLOCK + sub * WINDOW_Q:hh * Q_BLOCK + (sub + 1) * WINDOW_Q] for hh in range(NSA_HPG)],
                axis=0)
            w0 = pl.multiple_of(jnp.maximum(q0 + sub * WINDOW_Q - WINDOW, 0), WINDOW_Q)
            s = _dot_nt(pick(q), kw_ref[0, g, pl.ds(w0, span), :])
            diff = pick(t4) - (w0 + lax.broadcasted_iota(I32, (1, span), 1))
            e = _masked_exp2(s, (diff >= 0) & (diff < WINDOW))
            o = _dot(e.astype(BF16), vw_ref[0, g, pl.ds(w0, span), :])
            parts.append(o[:, :HEAD_DIM] * _safe_recip(o[:, HEAD_DIM:HEAD_DIM + 1]))
        o_win = jnp.concatenate([parts[sub][hh * WINDOW_Q:(hh + 1) * WINDOW_Q]
                                 for hh in range(NSA_HPG) for sub in range(len(parts))], axis=0)

        q_aug = jnp.concatenate([jnp.concatenate([sel_bias.astype(BF16)] * NSA_HPG, axis=0), q,
                                 jnp.zeros((rows, LANES - HEAD_DIM), BF16)], axis=1)
        return q_aug, o_cmp, o_win

    fronts = [front(g) for g in groups]

    def sel_tile(g, j, carry, causal):
        m_run, acc = carry
        k0 = pl.multiple_of(j * tk, tk)
        sc = _dot_nt(fronts[g][0], ks_ref[0, g, pl.ds(k0, tk), :])
        if causal:
            kpos = k0 + lax.broadcasted_iota(I32, (1, tk), 1)
            sc = jnp.where(kpos <= t4, sc, NEG_BIG)
        m_new = jnp.maximum(m_run, jnp.max(sc, axis=-1, keepdims=True))
        p = jnp.exp2(sc - m_new)
        acc_new = jnp.exp2(m_run - m_new) * acc + _dot(p.astype(BF16), vs_ref[0, g, pl.ds(k0, tk), :])
        return m_new, acc_new

    def sel_pair(jj, carries, causal):
        return tuple(sel_tile(g, 2 * jj + 1, sel_tile(g, 2 * jj, carries[g], causal), causal) for g in groups)

    init = tuple((jnp.full((rows, 1), NEG_BIG, F32), jnp.zeros((rows, LANES), F32)) for _ in groups)
    last_pair = (q0 // tk) // 2
    carries = sel_pair(last_pair, init, True)
    carries = lax.fori_loop(
        0, last_pair // 2, lambda jq, cs: sel_pair(2 * jq + 1, sel_pair(2 * jq, cs, False), False), carries)
    carries = lax.fori_loop(0, last_pair % 2, lambda _, cs: sel_pair(last_pair - 1, cs, False), carries)

    outs = []
    for g in groups:
        _, o_cmp, o_win = fronts[g]
        acc = carries[g][1]
        o_sel = acc[:, :HEAD_DIM] * (1.0 / acc[:, HEAD_DIM:HEAD_DIM + 1])
        gt = gate_ref[0, :, g * LANES:(g + 1) * LANES]
        for hh in range(NSA_HPG):
            sl = slice(hh * Q_BLOCK, (hh + 1) * Q_BLOCK)
            c = hh * N_BRANCH
            outs.append(o_cmp[sl] * gt[:, c:c + 1] + o_sel[sl] * gt[:, c + 1:c + 2]
                        + o_win[sl] * gt[:, c + 2:c + 3])
    o_ref[0] = jnp.concatenate(outs, axis=1).astype(BF16)


def _nsa(q_hm, ck, cv, ks, vs, kw, vw, gates, w_score):
    B, _, L, _ = q_hm.shape
    assert L // SEL_BLOCK <= LANES and (L // SEL_KV_TILE) % 2 == 0 and L >= WINDOW + Q_BLOCK
    n_cmp = ck.shape[2]
    grid = (B, L // Q_BLOCK)
    qspec = pl.BlockSpec((1, NSA_HEADS, Q_BLOCK, HEAD_DIM), lambda b, i: (b, 0, i, 0))
    cspec = pl.BlockSpec((1, NSA_GROUPS, n_cmp, HEAD_DIM), lambda b, i: (b, 0, 0, 0))
    kvspec = lambda w: pl.BlockSpec((1, NSA_GROUPS, L, w), lambda b, i: (b, 0, 0, 0),
                                    pipeline_mode=pl.Buffered(1))
    gspec = pl.BlockSpec((1, Q_BLOCK, NSA_GROUPS * LANES), lambda b, i: (b, i, 0))
    ospec = pl.BlockSpec((1, Q_BLOCK, NSA_WIDTH), lambda b, i: (b, i, 0))
    return pl.pallas_call(
        _nsa_kernel, grid=grid,
        in_specs=[qspec, cspec, cspec, kvspec(2 * LANES), kvspec(LANES), kvspec(HEAD_DIM), kvspec(LANES), gspec,
                  _full_spec(w_score.shape)],
        out_specs=ospec, out_shape=jax.ShapeDtypeStruct((B, L, NSA_WIDTH), BF16),
        compiler_params=_cparams("parallel", "arbitrary"), name="nsa",
    )(q_hm, ck, cv, ks, vs, kw, vw, gates, w_score)


def _s5_kernel(u_ref, wb_ref, wc_ref, are_ref, aim_ref, d_ref, y_ref, sre_ref, sim_ref, carry_ref):
    n_b, chunk, _ = u_ref.shape
    n_tile = wb_ref.shape[0]
    in_per = n_tile // (S5_WIDTH // LANES)
    pitch = S5_PITCH

    @pl.when(pl.program_id(0) == 0)
    def _():
        carry_ref[...] = jnp.zeros_like(carry_ref)

    for b in range(n_b):
        for c in range(n_tile):
            i = c // in_per
            ub = u_ref[b, :, i * LANES:(i + 1) * LANES].astype(BF16)
            r = _dot(ub, wb_ref[c])
            sre_ref[b, c * pitch:c * pitch + chunk, :] = r[:, :LANES]
            sim_ref[b, c * pitch:c * pitch + chunk, :] = r[:, LANES:]

    a_re, a_im = are_ref[...], aim_ref[...]

    def step(t, carry):
        out = []
        for b in range(n_b):
            s_re, s_im = carry[2 * b], carry[2 * b + 1]
            rows = pl.ds(t, n_tile, stride=pitch)
            n_re = a_re * s_re - a_im * s_im + sre_ref[b, rows, :]
            n_im = a_re * s_im + a_im * s_re + sim_ref[b, rows, :]
            sre_ref[b, rows, :] = n_re
            sim_ref[b, rows, :] = n_im
            out += [n_re, n_im]
        return tuple(out)

    init = tuple(carry_ref[i] for i in range(2 * n_b))
    fin = lax.fori_loop(0, chunk, step, init, unroll=True)
    for i in range(2 * n_b):
        carry_ref[i] = fin[i]

    for b in range(n_b):
        for o in range(S5_WIDTH // LANES):
            acc = jnp.zeros((chunk, LANES), F32)
            for c in range(o * in_per, (o + 1) * in_per):
                rows = slice(c * pitch, c * pitch + chunk)
                state = jnp.concatenate([sre_ref[b, rows, :], sim_ref[b, rows, :]], axis=1).astype(BF16)
                acc = acc + _dot(state, wc_ref[c])
            lanes = slice(o * LANES, (o + 1) * LANES)
            y = acc + d_ref[:, lanes] * u_ref[b, :, lanes]
            y_ref[b, :, lanes] = _gelu_tanh(y).astype(BF16)


def _s5(u, wb, wc, a_re, a_im, d_skip):
    B, L, W = u.shape
    chunk = S5_CHUNK
    n_tile = wb.shape[0]
    blk = pl.BlockSpec((B, chunk, W), lambda i: (0, i, 0))
    slab = pltpu.VMEM((B, n_tile * S5_PITCH, LANES), F32)
    return pl.pallas_call(
        _s5_kernel, grid=(L // chunk,),
        in_specs=[blk] + [_full_spec(w.shape) for w in (wb, wc, a_re, a_im, d_skip)],
        out_specs=blk, out_shape=jax.ShapeDtypeStruct((B, L, W), BF16),
        scratch_shapes=[slab, slab, pltpu.VMEM((2 * B, n_tile, LANES), F32)],
        compiler_params=_cparams("arbitrary"), name="s5",
    )(u, wb, wc, a_re, a_im, d_skip)


def _memkv_kernel(mem_ref, w_ref, k_ref, v_ref):
    kv = _dot(mem_ref[0].astype(BF16), w_ref[...])
    k_ref[0] = kv[:, :MEM_WIDTH].astype(BF16)
    v_ref[0] = kv[:, MEM_WIDTH:].astype(BF16)


def _memkv(mem, w_kv):
    B, M, D = mem.shape
    out = pl.BlockSpec((1, M, MEM_WIDTH), lambda b: (b, 0, 0))
    sd = jax.ShapeDtypeStruct((B, M, MEM_WIDTH), BF16)
    return pl.pallas_call(
        _memkv_kernel, grid=(B,),
        in_specs=[pl.BlockSpec((1, M, D), lambda b: (b, 0, 0)), _full_spec(w_kv.shape)],
        out_specs=[out, out], out_shape=[sd, sd], compiler_params=_cparams("parallel"), name="memkv",
    )(mem, w_kv)


def _memory_attention(q_ref, k_ref, v_ref):
    outs = []
    for h in range(MEM_HEADS):
        sl = slice(h * MEM_HEAD_DIM, (h + 1) * MEM_HEAD_DIM)
        s = _dot_nt(q_ref[:, sl], k_ref[0, :, sl]) * (MEM_HEAD_DIM ** -0.5)
        m = jnp.max(s, axis=-1, keepdims=True)
        e = jnp.exp(s - m)
        p = e / jnp.sum(e, axis=-1, keepdims=True)
        outs.append(_dot(p.astype(BF16), v_ref[0, :, sl]))
    return jnp.concatenate(outs, axis=1).astype(BF16)


def _merge_kernel(x_ref, lng_ref, lnb_ref, on_ref, gy_ref, qm_ref, km_ref, vm_ref, gm_ref,
                  wn_ref, wglu_ref, wmo_ref, wo_ref, l1g_ref, l1b_ref,
                  wrh_ref, wrp_ref, br_ref, tri_ref, striu_ref,
                  h1_ref, lp_ref, w4_ref, cnt_ref):
    D = x_ref.shape[1]
    tm = x_ref.shape[0]
    h =_layer_norm(x_ref[...], lng_ref[...], lnb_ref[...])
    y_nsa = _dot(on_ref[...], wn_ref[...])
    glu = _dot(gy_ref[...], wglu_ref[...])
    y_s5 = glu[:, :D] * jax.nn.sigmoid(glu[:, D:])
    y_mem = _dot(_memory_attention(qm_ref, km_ref, vm_ref), wmo_ref[...])
    merged = gm_ref[:, 0:D] * y_nsa + gm_ref[:, D:2 * D] * y_s5 + gm_ref[:, 2 * D:3 * D] * y_mem
    mix = _dot(merged.astype(BF16), wo_ref[...])
    h1 = _layer_norm(DEEPNORM_ALPHA * h + mix, l1g_ref[...], l1b_ref[...])
    h1_ref[...] = h1

    hh = h1.astype(BF16)
    hl = (h1 - hh.astype(F32)).astype(BF16)
    both = _dot(hh, wrp_ref[...])
    logits = both[:, :LANES] + both[:, LANES:] + _dot(hl, wrh_ref[...]) + br_ref[...]
    lane = lax.broadcasted_iota(I32, (tm, LANES), 1)
    lane_f = lane.astype(F32)
    work = logits
    multi = jnp.zeros((tm, LANES), F32)
    vals, picks = [], []
    for _ in range(TOP_K):
        m = jnp.max(work, axis=-1, keepdims=True)
        idx = jnp.min(jnp.where(work == m, lane_f, float(LANES)), axis=-1, keepdims=True)
        pick = lane_f == idx
        vals.append(m)
        picks.append((pick, idx))
        multi = jnp.where(pick, 1.0, multi)
        work = jnp.where(pick, -jnp.inf, work)
    es = [jnp.exp(v - vals[0]) for v in vals]
    den = es[0] + es[1] + es[2] + es[3]
    st = TOKEN_TILE
    pos = []
    for t in range(tm // st):
        multi_t = multi[t * st:(t + 1) * st]
        cnt = jnp.broadcast_to(jnp.sum(multi_t, axis=0, keepdims=True), (SUBLANES, LANES))
        cnt_ref[t] = cnt
        lower = _dot(cnt.astype(BF16), striu_ref[...])[0:1]
        pos.append(lower + _dot(tri_ref[...], multi_t.astype(BF16)))
    pos = jnp.concatenate(pos, axis=0)
    lp = jnp.full((tm, LANES), -1.0, F32)
    w4 = jnp.zeros((tm, LANES), F32)
    for k in range(TOP_K):
        pick, _ = picks[k]
        lp = jnp.where(lane == k, jnp.sum(jnp.where(pick, pos, 0.0), axis=-1, keepdims=True), lp)
        w4 = jnp.where(lane == k, es[k] / den, w4)
    lp_ref[...] = lp
    w4_ref[...] = w4


def _merge(x2, lng, lnb, o_nsa, gy, qm, k_mem, v_mem, gm, wn, wglu, wmo, wo, l1g, l1b, wrh, wrp, br, tri, striu):
    T, D = x2.shape
    tm = MERGE_SORT_TILES * TOKEN_TILE
    tok = lambda w: pl.BlockSpec((tm, w), lambda i: (i, 0))
    steps_per_batch = T // k_mem.shape[0] // tm
    mem_kv = pl.BlockSpec((1,) + k_mem.shape[1:], lambda i: (i // steps_per_batch, 0, 0))
    ws = [wn, wglu, wmo, wo, l1g, l1b, wrh, wrp, br, tri, striu]
    sd = jax.ShapeDtypeStruct
    lane_out = sd((T, LANES), F32)
    return pl.pallas_call(
        _merge_kernel, grid=(T // tm,),
        in_specs=[tok(D), _full_spec((1, D)), _full_spec((1, D)), tok(NSA_WIDTH), tok(S5_WIDTH),
                  tok(MEM_WIDTH), mem_kv, mem_kv, tok(N_BRANCH * D)] + [_full_spec(w.shape) for w in ws],
        out_specs=[tok(D), tok(LANES), tok(LANES),
                   pl.BlockSpec((MERGE_SORT_TILES, SUBLANES, LANES), lambda i: (i, 0, 0))],
        out_shape=[sd((T, D), F32), lane_out, lane_out, sd((T // TOKEN_TILE, SUBLANES, LANES), F32)],
        compiler_params=_cparams("parallel"), name="merge",
    )(x2, lng, lnb, o_nsa, gy, qm, k_mem, v_mem, gm, *ws)


def _slots_kernel(cnt_ref, triu_ref, striu_ref, tril_ref, seg_ref, blk_ref, misc_ref):
    n_blk = blk_ref.shape[0]
    cnt = cnt_ref[...]
    cnt_b = cnt.astype(BF16)
    total = jnp.sum(cnt, axis=0, keepdims=True)
    nblk_e = jnp.floor((total + (MOE_ROWS - 1)) * (1.0 / MOE_ROWS))
    nblk_8 = jnp.broadcast_to(nblk_e, (SUBLANES, LANES))
    end_b = _dot(nblk_8.astype(BF16), triu_ref[...])
    start_rows = (end_b - nblk_8)[0:1] * MOE_ROWS
    dst = start_rows + _dot(tril_ref[...], cnt_b)
    off = _dot(cnt_b, striu_ref[...])
    seg_ref[0] = cnt.astype(I32)
    seg_ref[1] = off.astype(I32)
    seg_ref[2] = dst.astype(I32)
    blk_i = lax.broadcasted_iota(I32, (n_blk, LANES), 0).astype(F32)
    lane_b = lax.broadcasted_iota(I32, (n_blk, LANES), 1)
    ended = jnp.where((end_b[0:1] <= blk_i) & (lane_b < N_EXPERTS), 1.0, 0.0)
    owner = jnp.minimum(jnp.sum(ended, axis=-1, keepdims=True), float(N_EXPERTS - 1))
    mine = lane_b.astype(F32) == owner
    pick = lambda per_expert: jnp.sum(jnp.where(mine, per_expert, 0.0), axis=-1, keepdims=True)
    earlier = blk_i[:, 0:1] - (pick(end_b[0:1]) - pick(nblk_e))
    held = jnp.clip(pick(total) - earlier * MOE_ROWS, 0.0, float(MOE_ROWS))
    blk_ref[...] = jnp.where(lane_b == 1, held, owner).astype(I32)
    cand = lax.broadcasted_iota(I32, (LANES, LANES), 0)
    has_blocks = jnp.broadcast_to(nblk_e, (LANES, LANES)).T > 0.0
    later = (cand > lax.broadcasted_iota(I32, (LANES, LANES), 1)) & has_blocks
    nxt = jnp.min(jnp.where(later, cand.astype(F32), float(LANES)), axis=0, keepdims=True)
    nxt = jnp.where(nxt < float(LANES), nxt, -1.0)
    lane8 = lax.broadcasted_iota(I32, (SUBLANES, LANES), 1)
    row8 = lax.broadcasted_iota(I32, (SUBLANES, LANES), 0)
    used = jnp.sum(jnp.where(lane8 == N_EXPERTS - 1, end_b, 0.0), axis=-1, keepdims=True)
    misc = jnp.where(row8 == 0, used,
                     jnp.where(row8 == 1, start_rows + total,
                               jnp.where(row8 == 2, nblk_e * MOE_ROWS - total, nxt)))
    misc_ref[...] = misc.astype(I32)


def _slots(cnt, triu, striu, tril, n_blk):
    n_tile = cnt.shape[0]
    sd = jax.ShapeDtypeStruct
    return pl.pallas_call(
        _slots_kernel, grid=(1,),
        in_specs=[_full_spec(cnt.shape), _full_spec(triu.shape), _full_spec(striu.shape), _full_spec(tril.shape)],
        out_specs=[_full_spec((3, n_tile, LANES)), _full_spec((n_blk, LANES)), _full_spec((SUBLANES, LANES))],
        out_shape=[sd((3, n_tile, LANES), I32), sd((n_blk, LANES), I32), sd((SUBLANES, LANES), I32)],
        compiler_params=_cparams("arbitrary"), name="slots",
    )(cnt, triu, striu, tril)


ROW_TILES = D_MODEL // LANES


def _row_span(row, n_rows):
    start = row * ROW_TILES
    if not isinstance(start, int):
        start = pl.multiple_of(start, ROW_TILES)
    return pl.ds(start, n_rows * ROW_TILES)


def _store_rows(ref, val):
    for c in range(ROW_TILES):
        ref[pl.ds(c, val.shape[0], stride=ROW_TILES), :] = val[:, c * LANES:(c + 1) * LANES]


def _load_row_tile(ref, n_rows, c):
    return ref[pl.ds(c, n_rows, stride=ROW_TILES), :]


BIG_PIECE_ROWS = 64


def _pieces(count, max_rows, fn):
    def run(sizes):
        for p in sizes:
            def piece(p=p):
                fn(count & (-2 * p), p)
            pl.when((count & p) != 0)(piece)

    sizes = [max_rows >> s for s in range(max_rows.bit_length())]
    big = [p for p in sizes if p >= BIG_PIECE_ROWS]
    if big:
        pl.when(count >= BIG_PIECE_ROWS)(lambda: run(big))
    run([p for p in sizes if p < BIG_PIECE_ROWS])


def _start_segment_copies(seg_ref, max_rows, make_copy):
    def per_expert(e, c):
        cnt, off, dst = seg_ref[0, 0, 0, e], seg_ref[1, 0, 0, e], seg_ref[2, 0, 0, e]
        _pieces(cnt, max_rows, lambda first, rows: make_copy(off + first, dst + first, rows).start())
        return c

    lax.fori_loop(0, N_EXPERTS, per_expert, 0)


def _dispatch_kernel(seg_ref, misc_ref, lp_ref, h_ref, xs_ref, sorted_ref, zero_ref, sem, pad_sem):
    i = pl.program_id(0)
    n = pl.num_programs(0)
    tm, D = h_ref.shape
    rows = TOP_K * tm
    slot = lax.rem(i, 2)

    def row_copy(slot_):
        def make(src_row, dst_row, n_rows):
            return pltpu.make_async_copy(sorted_ref.at[slot_, _row_span(src_row, n_rows)],
                                         xs_ref.at[_row_span(dst_row, n_rows)], sem.at[slot_])
        return make

    @pl.when(i == 0)
    def _():
        zero_ref[...] = jnp.zeros_like(zero_ref)
        for wait in (False, True):
            def per_expert(e, c, wait=wait):
                def one(first, n_rows):
                    cp = pltpu.make_async_copy(zero_ref.at[_row_span(0, n_rows)],
                                               xs_ref.at[_row_span(misc_ref[1, e] + first, n_rows)], pad_sem)
                    cp.wait() if wait else cp.start()
                _pieces(misc_ref[2, e], MOE_ROWS // 2, one)
                return c
            lax.fori_loop(0, N_EXPERTS, per_expert, 0)

            def per_spare_half_block(hb, c, wait=wait):
                cp = pltpu.make_async_copy(zero_ref, xs_ref.at[_row_span(hb * (MOE_ROWS // 2), MOE_ROWS // 2)],
                                           pad_sem)
                cp.wait() if wait else cp.start()
                return c
            lax.fori_loop(2 * misc_ref[0, 0], 2 * (xs_ref.shape[0] // (MOE_ROWS * ROW_TILES)),
                          per_spare_half_block, 0)

    lp_t = lp_ref[...].T
    s_ix = lax.broadcasted_iota(I32, (rows, 1), 0).astype(F32)
    hit = s_ix == lp_t[0:1, :]
    for k in range(1, TOP_K):
        hit = hit | (s_ix == lp_t[k:k + 1, :])
    perm = jnp.where(hit, 1.0, 0.0).astype(BF16)
    _store_rows(sorted_ref.at[slot], _dot(perm, h_ref[...].astype(BF16)))

    _start_segment_copies(seg_ref, tm, row_copy(slot))

    @pl.when(i > 0)
    def _():
        row_copy(1 - slot)(0, 0, rows).wait()

    @pl.when(i == n - 1)
    def _():
        row_copy(slot)(0, 0, rows).wait()


def _seg_spec(index_map):
    return pl.BlockSpec((3, 1, 1, LANES), index_map, memory_space=pltpu.SMEM)


def _dispatch(seg4, misc, lp, h1, cap):
    T, D = h1.shape
    assert D == ROW_TILES * LANES
    tm = TOKEN_TILE
    tok = lambda w: pl.BlockSpec((tm, w), lambda i: (i, 0))
    return pl.pallas_call(
        _dispatch_kernel, grid=(T // tm,),
        in_specs=[_seg_spec(lambda i: (0, i, 0, 0)), pl.BlockSpec(memory_space=pltpu.SMEM), tok(LANES), tok(D)],
        out_specs=pl.BlockSpec(memory_space=pl.ANY),
        out_shape=jax.ShapeDtypeStruct((cap * ROW_TILES, LANES), F32),
        scratch_shapes=[pltpu.VMEM((2, TOP_K * tm * ROW_TILES, LANES), F32),
                        pltpu.VMEM((MOE_ROWS // 2 * ROW_TILES, LANES), F32),
                        pltpu.SemaphoreType.DMA((2,)), pltpu.SemaphoreType.DMA(())],
        compiler_params=_cparams("arbitrary"), name="dispatch",
    )(seg4, misc, lp, h1)


def _expert_kernel(blk_ref, used_ref, next_ref, rows_ref, xs_ref, wgu_hbm, bgu_ref, wd_hbm, bd_ref, ys_ref,
                   wgu_f32, wd_f32, wgu_bf, wd_bf, sem, run_ref):
    i = pl.program_id(0)
    live = i < used_ref[0]
    expert = blk_ref[i]

    def weight_copies(e, slot):
        return (pltpu.make_async_copy(wgu_hbm.at[e], wgu_f32.at[slot], sem.at[0, slot]),
                pltpu.make_async_copy(wd_hbm.at[e], wd_f32.at[slot], sem.at[1, slot]))

    @pl.when(i == 0)
    def _():
        run_ref[0] = 0
        for cp in weight_copies(expert, 0):
            cp.start()

    @pl.when(live & ((i == 0) | (expert != blk_ref[jnp.maximum(i - 1, 0)])))
    def _():
        slot = lax.rem(run_ref[0], 2)
        for cp in weight_copies(expert, slot):
            cp.wait()
        wgu_bf[...] = wgu_f32[slot].astype(BF16)
        wd_bf[...] = wd_f32[slot].astype(BF16)
        nxt = next_ref[expert]

        @pl.when(nxt >= 0)
        def _():
            for cp in weight_copies(nxt, 1 - slot):
                cp.start()

        run_ref[0] = run_ref[0] + 1

    def expert_rows(n_rows):
        xb = jnp.concatenate([_load_row_tile(xs_ref, n_rows, c).astype(BF16) for c in range(ROW_TILES)], axis=1)
        gu = _dot(xb, wgu_bf[...]) + bgu_ref[0]
        g = jnp.minimum(gu[:, :D_FF], SWIGLU_LIMIT)
        lin = jnp.clip(gu[:, D_FF:], -SWIGLU_LIMIT, SWIGLU_LIMIT)
        act = g * jax.nn.sigmoid(SWIGLU_ALPHA * g) * (lin + 1.0)
        _store_rows(ys_ref, _dot(act.astype(BF16), wd_bf[...]) + bd_ref[0])

    half = MOE_ROWS // 2
    half_full = rows_ref[i] <= half

    @pl.when(live & jnp.logical_not(half_full))
    def _():
        expert_rows(MOE_ROWS)

    @pl.when(live & half_full)
    def _():
        expert_rows(half)
        ys_ref[half * ROW_TILES:, :] = jnp.zeros((half * ROW_TILES, LANES), F32)

    @pl.when(pl.program_id(0) >= used_ref[0])
    def _():
        ys_ref[...] = jnp.zeros_like(ys_ref)


def _experts(blk_expert, n_used, next_expert, blk_rows, xs, w_gate_up, b_gate_up, w_down, b_down):
    D = w_down.shape[2]
    n_blk = xs.shape[0] // (MOE_ROWS * ROW_TILES)
    E = w_gate_up.shape[0]
    live = lambda i, used: jnp.minimum(i, used[0] - 1)
    row = pl.BlockSpec((MOE_ROWS * ROW_TILES, LANES), lambda i, blk, used, nxt, held: (live(i, used), 0))
    by_e = lambda shape: pl.BlockSpec((1,) + shape, lambda i, blk, used, nxt, held: (blk[live(i, used)], 0, 0))
    in_hbm = pl.BlockSpec(memory_space=pl.ANY)
    grid_spec = pltpu.PrefetchScalarGridSpec(
        num_scalar_prefetch=4, grid=(n_blk,),
        in_specs=[row, in_hbm, by_e((1, 2 * D_FF)), in_hbm, by_e((1, D))],
        out_specs=pl.BlockSpec((MOE_ROWS * ROW_TILES, LANES), lambda i, blk, used, nxt, held: (i, 0)),
        scratch_shapes=[pltpu.VMEM((2, D, 2 * D_FF), F32), pltpu.VMEM((2, D_FF, D), F32),
                        pltpu.VMEM((D, 2 * D_FF), BF16), pltpu.VMEM((D_FF, D), BF16),
                        pltpu.SemaphoreType.DMA((2, 2)), pltpu.SMEM((1,), I32)])
    return pl.pallas_call(
        _expert_kernel, grid_spec=grid_spec, out_shape=jax.ShapeDtypeStruct(xs.shape, F32),
        compiler_params=_cparams("arbitrary"), name="experts",
    )(blk_expert, n_used, next_expert, blk_rows, xs, w_gate_up, b_gate_up.reshape(E, 1, 2 * D_FF), w_down,
      b_down.reshape(E, 1, D))


def _combine_kernel(seg_ref, segn_ref, lp_ref, w4_ref, h1_ref, g_ref, b_ref, ys_ref, o_ref, buf_ref, sem):
    i = pl.program_id(0)
    n = pl.num_programs(0)
    tm = h1_ref.shape[0]
    rows = TOP_K * tm
    slot = lax.rem(i, 2)

    def row_copy(slot_):
        def make(buf_row, ys_row, n_rows):
            return pltpu.make_async_copy(ys_ref.at[_row_span(ys_row, n_rows)],
                                         buf_ref.at[slot_, _row_span(buf_row, n_rows)], sem.at[slot_])
        return make

    @pl.when(i == 0)
    def _():
        _start_segment_copies(seg_ref, tm, row_copy(slot))

    @pl.when(i + 1 < n)
    def _():
        _start_segment_copies(segn_ref, tm, row_copy(1 - slot))

    row_copy(slot)(0, 0, rows).wait()

    lp = lp_ref[...]
    s_ix = lax.broadcasted_iota(I32, (1, rows), 1).astype(F32)
    wmat = jnp.zeros((tm, rows), F32)
    for k in range(TOP_K):
        wmat = jnp.where(s_ix == lp[:, k:k + 1], w4_ref[:, k:k + 1], wmat)
    w_hi = wmat.astype(BF16)
    w_lo = (wmat - w_hi.astype(F32)).astype(BF16)
    y = jnp.concatenate([_load_row_tile(buf_ref.at[slot], rows, c) for c in range(ROW_TILES)], axis=1)
    y_hi = y.astype(BF16)
    y_lo = (y - y_hi.astype(F32)).astype(BF16)
    acc = DEEPNORM_ALPHA * h1_ref[...] + (_dot(w_hi, y_hi) + _dot(w_hi, y_lo) + _dot(w_lo, y_hi))
    o_ref[...] = _layer_norm(acc, g_ref[...], b_ref[...])


def _combine(seg4, lp, w4, h1, ln_g, ln_b, ys):
    T, D = h1.shape
    tm = TOKEN_TILE
    n_tile = T // tm
    return pl.pallas_call(
        _combine_kernel, grid=(n_tile,),
        in_specs=[_seg_spec(lambda i: (0, i, 0, 0)),
                  _seg_spec(lambda i: (0, jnp.minimum(i + 1, n_tile - 1), 0, 0)),
                  pl.BlockSpec((tm, LANES), lambda i: (i, 0)),
                  pl.BlockSpec((tm, LANES), lambda i: (i, 0)),
                  pl.BlockSpec((tm, D), lambda i: (i, 0)),
                  _full_spec((1, D)), _full_spec((1, D)),
                  pl.BlockSpec(memory_space=pl.ANY)],
        out_specs=pl.BlockSpec((tm, D), lambda i: (i, 0)),
        out_shape=jax.ShapeDtypeStruct((T, D), F32),
        scratch_shapes=[pltpu.VMEM((2, TOP_K * tm * ROW_TILES, LANES), F32), pltpu.SemaphoreType.DMA((2,))],
        compiler_params=_cparams("arbitrary"), name="combine",
    )(seg4, seg4, lp, w4, h1, ln_g, ln_b, ys)


def _rope_tables(positions):
    inv = ROPE_THETA ** (-jnp.arange(0, ROT_DIM, 2, dtype=F32) / ROT_DIM)
    ang = positions.astype(F32)[..., None] * inv
    cos_sin = jnp.concatenate([jnp.cos(ang), jnp.sin(ang)], axis=-1)
    half = ROT_DIM // 2
    spread = np.zeros((ROT_DIM, 3 * LANES), np.float32)
    unit = np.ones((1, LANES), np.float32)
    for lane in range(LANES):
        d = lane % HEAD_DIM
        if d < half:
            spread[d, lane] = 1.0
            spread[half + d, 2 * LANES + lane] = -1.0
            unit[0, lane] = 0.0
        elif d < ROT_DIM:
            spread[d - half, lane] = 1.0
            spread[d, LANES + lane] = 1.0
            unit[0, lane] = 0.0
    return cos_sin, jnp.asarray(spread, BF16), jnp.asarray(unit)


def _split_w_in(w_in):
    widths = (NSA_WIDTH,) + (KV_WIDTH,) * 6 + (NSA_HEADS * N_BRANCH, S5_WIDTH, MEM_WIDTH, N_BRANCH * D_MODEL)
    offs = [0]
    for w in widths:
        offs.append(offs[-1] + w)
    col = lambda i: w_in[:, offs[i]:offs[i + 1]]
    wq, kc, vc, ks, vs, kw, vw, wg, wu, wqm, wm = (col(i) for i in range(11))
    wk = jnp.concatenate([kc, ks, kw], axis=1)
    wv = jnp.concatenate([vc, vs, vw], axis=1)
    per_group = NSA_HPG * N_BRANCH
    wg_pad = jnp.zeros((w_in.shape[0], NSA_GROUPS * LANES), w_in.dtype)
    for g in range(NSA_GROUPS):
        wg_pad = wg_pad.at[:, g * LANES:g * LANES + per_group].set(wg[:, g * per_group:(g + 1) * per_group])
    return tuple(w.astype(BF16) for w in (wq, wk, wv, wg_pad, wu, wqm, wm))


def _compress_weights(w1):
    half = CMP_BLOCK // 2
    eye = np.eye(NSA_GROUPS, dtype=np.float32)

    def arrange(w_half):
        full = jnp.einsum('sdf,gh->sgdhf', w_half, eye)
        return full.reshape(half * NSA_GROUPS * HEAD_DIM, NSA_GROUPS * CMP_HIDDEN).astype(BF16)

    return (w1.reshape(CMP_BLOCK * HEAD_DIM, CMP_HIDDEN).astype(BF16), arrange(w1[:half]), arrange(w1[half:]))


def _s5_weights(a_re, a_im, log_dt, b_re, b_im, c_re, c_im):
    step = jnp.exp(log_dt)[:, None]
    mag = jnp.exp(a_re * step)
    ab_re, ab_im = mag * jnp.cos(a_im * step), mag * jnp.sin(a_im * step)
    den = a_re * a_re + a_im * a_im
    nr = ab_re - 1.0
    coef_re = (nr * a_re + ab_im * a_im) / den
    coef_im = (ab_im * a_re - nr * a_im) / den
    bb_re = coef_re[..., None] * b_re - coef_im[..., None] * b_im
    bb_im = coef_re[..., None] * b_im + coef_im[..., None] * b_re
    n_tile = S5_GROUPS * S5_STATE // LANES
    tile_groups = LANES // S5_STATE
    lane_groups = LANES // S5_GROUP_DIM
    tiles_per_lane_tile = lane_groups // tile_groups
    place = np.zeros((n_tile, lane_groups, tile_groups), np.float32)
    for c in range(n_tile):
        for j in range(tile_groups):
            place[c, (c % tiles_per_lane_tile) * tile_groups + j, j] = 1.0

    def in_blocks(bb):
        pairs = bb.reshape(n_tile, tile_groups, S5_STATE, S5_GROUP_DIM)
        return jnp.einsum('cjnp,caj->capjn', pairs, place).reshape(n_tile, LANES, LANES)

    def out_blocks(c):
        pairs = c.reshape(n_tile, tile_groups, S5_GROUP_DIM, S5_STATE)
        return jnp.einsum('cjpn,caj->cjnap', pairs, place).reshape(n_tile, LANES, LANES)

    wb = jnp.concatenate([in_blocks(bb_re), in_blocks(bb_im)], axis=2).astype(BF16)
    wc = jnp.concatenate([out_blocks(c_re), out_blocks(-c_im)], axis=1).astype(BF16)
    return wb, wc, ab_re.reshape(n_tile, LANES), ab_im.reshape(n_tile, LANES)


def _layer(x, mem, positions, ln_emb_g, ln_emb_b, w_in, pe_k, pe_v, w_kcmp1, w_kcmp2, w_vcmp1, w_vcmp2,
           s5_a_re, s5_a_im, s5_log_dt, s5_b_re, s5_b_im, s5_c_re, s5_c_im, s5_d,
           w_s5_glu, w_mem_kv, w_nsa_out, w_mem_out, w_o, ln1_g, ln1_b, w_router, b_router,
           w_gate_up, b_gate_up, w_down, b_down, ln2_g, ln2_b):
    B, L, D = x.shape
    T = B * L
    row = lambda v: v.reshape(1, -1)

    cos_sin, spread, unit = _rope_tables(positions)
    (q_hm, kc, vc, ks, vs, kw, vw, gates, u, qm, gm) = _inproj(
        x, row(ln_emb_g), row(ln_emb_b), cos_sin, spread, unit, *_split_w_in(w_in))

    n_chunk = L // CMP_STRIDE
    chunked = lambda t: t.reshape(B, n_chunk, CMP_STRIDE * KV_WIDTH)
    pe_rows = lambda pe: jnp.broadcast_to(pe.reshape(1, -1), (SUBLANES, CMP_BLOCK * HEAD_DIM)).astype(BF16)
    wk1f, wk1a, wk1b = _compress_weights(w_kcmp1)
    wv1f, wv1a, wv1b = _compress_weights(w_vcmp1)
    ck, cv = _compress(chunked(kc), chunked(vc), pe_rows(pe_k), pe_rows(pe_v), wk1f, wv1f,
                       wk1a, wk1b, wv1a, wv1b, w_kcmp2.astype(BF16), w_vcmp2.astype(BF16))

    per_sb = SEL_BLOCK // CMP_STRIDE
    c_ix = np.arange(n_chunk)[:, None]
    n_ix = np.arange(L // SEL_BLOCK)[None, :]
    w_score = jnp.asarray((c_ix // per_sb == n_ix).astype(np.float32)
                          + ((c_ix + 1) // per_sb == n_ix).astype(np.float32), BF16)
    o_nsa = _nsa(q_hm, ck, cv, ks, vs, kw, vw, gates, w_score)

    wb, wc, a_re, a_im = _s5_weights(s5_a_re, s5_a_im, s5_log_dt, s5_b_re, s5_b_im, s5_c_re, s5_c_im)
    gy = _s5(u, wb, wc, a_re, a_im, row(s5_d))

    k_mem, v_mem = _memkv(mem, w_mem_kv.astype(BF16))

    pad_e = LANES - N_EXPERTS
    wr = jnp.pad(w_router, ((0, 0), (0, pad_e)))
    wr_hi = wr.astype(BF16)
    wr_pair = jnp.concatenate([wr_hi, (wr - wr_hi.astype(F32)).astype(BF16)], axis=1)
    br = jnp.concatenate([b_router, jnp.full((pad_e,), -jnp.inf, F32)]).reshape(1, LANES)
    tm = TOKEN_TILE
    n_tile = T // tm
    strict_lower = lambda n: jnp.asarray(np.tril(np.ones((n, n), np.float32), -1), BF16)
    triu = jnp.asarray(np.triu(np.ones((LANES, LANES), np.float32)), BF16)
    striu = jnp.asarray(np.triu(np.ones((LANES, LANES), np.float32), 1), BF16)
    flat = lambda t: t.reshape(T, t.shape[-1])
    h1, lp, w4, cnt = _merge(
        flat(x), row(ln_emb_g), row(ln_emb_b), flat(o_nsa), flat(gy), flat(qm), k_mem, v_mem, flat(gm),
        w_nsa_out.astype(BF16), w_s5_glu.astype(BF16), w_mem_out.astype(BF16), w_o.astype(BF16),
        row(ln1_g), row(ln1_b), wr_hi, wr_pair, br, strict_lower(tm), striu)

    cap = (T * TOP_K + MOE_ROWS - 1) // MOE_ROWS * MOE_ROWS + N_EXPERTS * MOE_ROWS
    n_blk = cap // MOE_ROWS
    seg, blk_owner, misc = _slots(cnt[:, 0, :], triu, striu, strict_lower(n_tile), n_blk)
    seg4 = seg.reshape(3, n_tile, 1, LANES)
    blk_expert = blk_owner[:, 0]
    n_used = misc[0, :1]

    xs = _dispatch(seg4, misc, lp, h1, cap)
    ys = _experts(blk_expert, n_used, misc[3, :N_EXPERTS], blk_owner[:, 1], xs, w_gate_up, b_gate_up, w_down,
                  b_down)
    out = _combine(seg4, lp, w4, h1, row(ln2_g), row(ln2_b), ys)
    return out.reshape(B, L, D)


def kernel(x, mem, positions, ln_emb_g, ln_emb_b, w_in, pe_k_cmp, pe_v_cmp, w_kcmp1, w_kcmp2, w_vcmp1, w_vcmp2, s5_a_re, s5_a_im, s5_log_dt, s5_b_re, s5_b_im, s5_c_re, s5_c_im, s5_d, w_s5_glu, w_mem_kv, w_nsa_out, w_mem_out, w_o, ln1_g, ln1_b, w_router, b_router, w_gate_up, b_gate_up, w_down, b_down, ln2_g, ln2_b):
    assert w_in.shape[0] == DEPTH
    l = 0
    return _layer(x, mem, positions, ln_emb_g, ln_emb_b, w_in[l], pe_k_cmp[l], pe_v_cmp[l], w_kcmp1[l],
                  w_kcmp2[l], w_vcmp1[l], w_vcmp2[l], s5_a_re[l], s5_a_im[l], s5_log_dt[l], s5_b_re[l],
                  s5_b_im[l], s5_c_re[l], s5_c_im[l], s5_d[l], w_s5_glu[l], w_mem_kv[l], w_nsa_out[l],
                  w_mem_out[l], w_o[l], ln1_g[l], ln1_b[l], w_router[l], b_router[l], w_gate_up[l],
                  b_gate_up[l], w_down[l], b_down[l], ln2_g[l], ln2_b[l])
```

```python
import functools
import math

import jax
import jax.numpy as jnp
import numpy as np
from jax import lax
from jax.experimental import pallas as pl
from jax.experimental.pallas import tpu as pltpu

F32 = jnp.float32
BF16 = jnp.bfloat16
I32 = jnp.int32

D_MODEL = 1024
NSA_HEADS = 8
NSA_GROUPS = 2
NSA_HPG = NSA_HEADS // NSA_GROUPS
HEAD_DIM = 64
NSA_WIDTH = NSA_HEADS * HEAD_DIM
KV_WIDTH = NSA_GROUPS * HEAD_DIM
CMP_BLOCK = 32
CMP_STRIDE = 16
CMP_HIDDEN = 128
SEL_BLOCK = 64
N_SEL = 16
WINDOW = 512
Q_BLOCK = 256
WINDOW_Q = 128
ROPE_THETA = 500000.0
ROT_DIM = HEAD_DIM // 4
S5_WIDTH = 512
S5_GROUP_DIM = 16
S5_GROUPS = S5_WIDTH // S5_GROUP_DIM
S5_STATE = 64
MEM_HEADS = 4
MEM_HEAD_DIM = 128
MEM_WIDTH = MEM_HEADS * MEM_HEAD_DIM
N_BRANCH = 3
N_EXPERTS = 32
TOP_K = 4
D_FF = 1024
SWIGLU_LIMIT = 7.0
SWIGLU_ALPHA = 1.702
LN_EPS = 1e-5
DEPTH = 1
DEEPNORM_ALPHA = (2 * DEPTH) ** 0.25

LANES = 128
SUBLANES = 8
VMEM_LIMIT_BYTES = 56 * 1024 * 1024

TOKEN_TILE = 256
INPROJ_TOKEN_TILE = 512
MERGE_SORT_TILES = 2
SEL_KV_TILE = 512
S5_CHUNK = 512
S5_PITCH = S5_CHUNK + 4
MOE_ROWS = 512
NEG_BIG = -(2.0 ** 100)
Q_SCALE_LOG2 = HEAD_DIM ** -0.5 * math.log2(math.e)


def _cparams(*sem):
    return pltpu.CompilerParams(dimension_semantics=sem, vmem_limit_bytes=VMEM_LIMIT_BYTES)


def _dot(a, b):
    return jnp.dot(a, b, preferred_element_type=F32)


def _dot_nt(a, b):
    return lax.dot_general(a, b, (((1,), (1,)), ((), ())), preferred_element_type=F32)


def _layer_norm(x, g, b):
    mu = jnp.mean(x, axis=-1, keepdims=True)
    xc = x - mu
    var = jnp.mean(xc * xc, axis=-1, keepdims=True)
    return xc * lax.rsqrt(var + LN_EPS) * g + b


def _gelu_tanh(x):
    cdf = 0.5 * (1.0 + jnp.tanh(math.sqrt(2.0 / math.pi) * (x + 0.044715 * (x * x * x))))
    return x * cdf


def _masked_exp2(s, mask):
    s = jnp.where(mask, s, -jnp.inf)
    m = jnp.max(s, axis=-1, keepdims=True)
    m = jnp.where(m > -jnp.inf, m, 0.0)
    return jnp.exp2(s - m)


def _safe_recip(denom):
    return 1.0 / jnp.maximum(denom, jnp.finfo(F32).tiny)


def _split3(x):
    hi = x.astype(BF16)
    r1 = x - hi.astype(F32)
    mid = r1.astype(BF16)
    lo = (r1 - mid.astype(F32)).astype(BF16)
    return hi, mid, lo


def _full_spec(shape):
    nd = len(shape)
    return pl.BlockSpec(shape, lambda *_: (0,) * nd)


def _inproj_kernel(x_ref, g_ref, b_ref, cs_ref, spread_ref, unit_ref,
                   wq_ref, wk_ref, wv_ref, wg_ref, wu_ref, wqm_ref, wm_ref,
                   q_ref, kc_ref, vc_ref, ks_ref, vs_ref, kw_ref, vw_ref,
                   gate_ref, u_ref, qm_ref, gm_ref):
    h = _layer_norm(x_ref[0], g_ref[...], b_ref[...])
    hb = h.astype(BF16)
    tab = sum(_dot(part, spread_ref[...]) for part in _split3(cs_ref[0]))
    cos_t = tab[:, 0:LANES] + unit_ref[...]
    sin_a = tab[:, LANES:2 * LANES]
    sin_b = tab[:, 2 * LANES:3 * LANES]

    def rope(t):
        return (t * cos_t + pltpu.roll(t, ROT_DIM // 2, 1) * sin_a
                + pltpu.roll(t, LANES - ROT_DIM // 2, 1) * sin_b)

    q = _dot(hb, wq_ref[...])
    for c in range(NSA_WIDTH // LANES):
        qc = rope(q[:, c * LANES:(c + 1) * LANES]) * Q_SCALE_LOG2
        for hh in range(2):
            q_ref[0, 2 * c + hh] = qc[:, hh * HEAD_DIM:(hh + 1) * HEAD_DIM].astype(BF16)
    k3 = _dot(hb, wk_ref[...])
    kc = rope(k3[:, 0:LANES])
    ks = rope(k3[:, LANES:2 * LANES])
    kw = rope(k3[:, 2 * LANES:3 * LANES])
    v3 = _dot(hb, wv_ref[...])
    kc_ref[0] = kc.astype(BF16)
    vc_ref[0] = v3[:, 0:LANES].astype(BF16)
    tm = x_ref.shape[1]
    pos = pl.program_id(1) * tm + lax.broadcasted_iota(I32, (tm, LANES), 0)
    blk_hot = jnp.where(lax.broadcasted_iota(I32, (tm, LANES), 1) == pos // SEL_BLOCK, 1.0, 0.0)
    lane_pad = jnp.zeros((tm, LANES - HEAD_DIM), F32)
    ones_pad = jnp.where(lax.broadcasted_iota(I32, (tm, LANES - HEAD_DIM), 1) == 0, 1.0, 0.0)
    for g in range(NSA_GROUPS):
        sl = slice(g * HEAD_DIM, (g + 1) * HEAD_DIM)
        ks_ref[0, g] = jnp.concatenate([blk_hot, ks[:, sl], lane_pad], axis=1).astype(BF16)
        kw_ref[0, g] = kw[:, sl].astype(BF16)
        vs_ref[0, g] = jnp.concatenate([v3[:, LANES:2 * LANES][:, sl], ones_pad], axis=1).astype(BF16)
        vw_ref[0, g] = jnp.concatenate([v3[:, 2 * LANES:3 * LANES][:, sl], ones_pad], axis=1).astype(BF16)
    gate_ref[0] = jax.nn.sigmoid(_dot(hb, wg_ref[...]))
    u_ref[0] = _dot(hb, wu_ref[...])
    qm_ref[0] = _dot(hb, wqm_ref[...]).astype(BF16)
    gm_ref[0] = jax.nn.sigmoid(_dot(hb, wm_ref[...]))


def _inproj(x, ln_g, ln_b, cos_sin, spread, unit, wq, wk, wv, wg, wu, wqm, wm):
    B, L, D = x.shape
    tm = INPROJ_TOKEN_TILE
    grid = (B, L // tm)
    tok = lambda w: pl.BlockSpec((1, tm, w), lambda b, i: (b, i, 0))
    head = lambda n, w=HEAD_DIM: pl.BlockSpec((1, n, tm, w), lambda b, i: (b, 0, i, 0))
    in_specs = [tok(D), _full_spec((1, D)), _full_spec((1, D)), tok(ROT_DIM), _full_spec(spread.shape),
                _full_spec(unit.shape)]
    in_specs += [pl.BlockSpec(w.shape, lambda b, i: (0, 0), pipeline_mode=pl.Buffered(1))
                 for w in (wq, wk, wv, wg, wu, wqm, wm)]
    sd = jax.ShapeDtypeStruct
    out_shape = [
        sd((B, NSA_HEADS, L, HEAD_DIM), BF16),
        sd((B, L, KV_WIDTH), BF16), sd((B, L, KV_WIDTH), BF16),
        sd((B, NSA_GROUPS, L, 2 * LANES), BF16), sd((B, NSA_GROUPS, L, LANES), BF16),
        sd((B, NSA_GROUPS, L, HEAD_DIM), BF16), sd((B, NSA_GROUPS, L, LANES), BF16),
        sd((B, L, NSA_GROUPS * LANES), F32),
        sd((B, L, S5_WIDTH), F32),
        sd((B, L, MEM_WIDTH), BF16),
        sd((B, L, N_BRANCH * D), F32),
    ]
    out_specs = [head(NSA_HEADS), tok(KV_WIDTH), tok(KV_WIDTH), head(NSA_GROUPS, 2 * LANES),
                 head(NSA_GROUPS, LANES), head(NSA_GROUPS), head(NSA_GROUPS, LANES),
                 tok(NSA_GROUPS * LANES), tok(S5_WIDTH),
                 tok(MEM_WIDTH), tok(N_BRANCH * D)]
    return pl.pallas_call(
        _inproj_kernel, grid=grid, in_specs=in_specs, out_specs=out_specs, out_shape=out_shape,
        compiler_params=_cparams("parallel", "parallel"), name="inproj",
    )(x, ln_g, ln_b, cos_sin, spread, unit, wq, wk, wv, wg, wu, wqm, wm)


def _compress_kernel(kc_ref, vc_ref, pek_ref, pev_ref, wk1f_ref, wv1f_ref,
                     wk1a_ref, wk1b_ref, wv1a_ref, wv1b_ref, wk2_ref, wv2_ref, ck_ref, cv_ref):
    n_chunk = kc_ref.shape[1]
    row = lax.broadcasted_iota(I32, (n_chunk, 1), 0)

    def one(x_ref, pe_ref, w1f_ref, w1a_ref, w1b_ref, w2_ref, o_ref):
        x = x_ref[0]
        first = _dot(x, w1a_ref[...])
        second = _dot(x, w1b_ref[...])
        second = pltpu.roll(second, n_chunk - 1, 0)
        pe_term = _dot(pe_ref[...], w1f_ref[...])[0:1]
        pe_term = jnp.concatenate([pe_term] * NSA_GROUPS, axis=1)
        hid = _gelu_tanh(first + second + pe_term).astype(BF16)
        for g in range(NSA_GROUPS):
            o = _dot(hid[:, g * CMP_HIDDEN:(g + 1) * CMP_HIDDEN], w2_ref[...])
            o_ref[0, g] = jnp.where(row < n_chunk - 1, o, 0.0).astype(BF16)

    one(kc_ref, pek_ref, wk1f_ref, wk1a_ref, wk1b_ref, wk2_ref, ck_ref)
    one(vc_ref, pev_ref, wv1f_ref, wv1a_ref, wv1b_ref, wv2_ref, cv_ref)


def _compress(kc_r, vc_r, pek, pev, wk1f, wv1f, wk1a, wk1b, wv1a, wv1b, wk2, wv2):
    B, n_chunk, width = kc_r.shape
    blk = pl.BlockSpec((1, n_chunk, width), lambda b: (b, 0, 0))
    out = pl.BlockSpec((1, NSA_GROUPS, n_chunk, HEAD_DIM), lambda b: (b, 0, 0, 0))
    ws = [pek, pev, wk1f, wv1f, wk1a, wk1b, wv1a, wv1b, wk2, wv2]
    sd = jax.ShapeDtypeStruct((B, NSA_GROUPS, n_chunk, HEAD_DIM), BF16)
    return pl.pallas_call(
        _compress_kernel, grid=(B,), in_specs=[blk, blk] + [_full_spec(w.shape) for w in ws],
        out_specs=[out, out], out_shape=[sd, sd], compiler_params=_cparams("parallel"), name="compress",
    )(kc_r, vc_r, *ws)


def _nsa_kernel(q_ref, ck_ref, cv_ref, ks_ref, vs_ref, kw_ref, vw_ref, gate_ref, wsc_ref, o_ref):
    seq_len = ks_ref.shape[2]
    n_cmp = ck_ref.shape[2]
    n_sb = seq_len // SEL_BLOCK
    n_sel = min(N_SEL, n_sb)
    rows = NSA_HPG * Q_BLOCK
    groups = range(NSA_GROUPS)
    q0 = pl.program_id(1) * Q_BLOCK
    t1 = q0 + lax.broadcasted_iota(I32, (Q_BLOCK, 1), 0)
    t4 = jnp.concatenate([t1] * NSA_HPG, axis=0)
    tk = SEL_KV_TILE

    def front(g):
        q = q_ref[0, g * NSA_HPG:(g + 1) * NSA_HPG].reshape(rows, HEAD_DIM)

        s = _dot_nt(q, ck_ref[0, g])
        c_end = lax.broadcasted_iota(I32, (1, n_cmp), 1) * CMP_STRIDE + (CMP_BLOCK - 1)
        e = _masked_exp2(s, c_end <= t4)
        p_cmp = e * _safe_recip(jnp.sum(e, axis=-1, keepdims=True))
        o_cmp = _dot(p_cmp.astype(BF16), cv_ref[0, g])

        imp = p_cmp[0:Q_BLOCK]
        for hh in range(1, NSA_HPG):
            imp = imp + p_cmp[hh * Q_BLOCK:(hh + 1) * Q_BLOCK]
        w_sc = wsc_ref[...]
        score = sum(_dot(part, w_sc) for part in _split3(imp))
        score_t = score.T
        jb = lax.broadcasted_iota(I32, (n_sb, Q_BLOCK), 0)
        tb = (q0 + lax.broadcasted_iota(I32, (1, Q_BLOCK), 1)) // SEL_BLOCK
        forced = (jb == 0) | (jb == tb) | (jb == tb - 1)
        work = jnp.where(forced | (jb > tb), -jnp.inf, score_t)
        bias_t = jnp.where(forced, 0.0, NEG_BIG)
        jbf = jb.astype(F32)
        for _ in range(n_sel - 3):
            m = jnp.max(work, axis=0, keepdims=True)
            idx = jnp.min(jnp.where(work == m, jbf, float(n_sb)), axis=0, keepdims=True)
            pick = jbf == idx
            bias_t = jnp.where(pick, 0.0, bias_t)
            work = jnp.where(pick, -jnp.inf, work)
        sel_bias = bias_t.T
        if n_sb < LANES:
            sel_bias = jnp.concatenate([sel_bias, jnp.zeros((Q_BLOCK, LANES - n_sb), F32)], axis=1)

        span = WINDOW + WINDOW_Q
        parts = []
        for sub in range(Q_BLOCK // WINDOW_Q):
            pick = lambda a: jnp.concatenate(
                [a[hh * Q_BLOCK + sub * WINDOW_Q:hh * Q_BLOCK + (sub + 1) * WINDOW_Q] for hh in range(NSA_HPG)],
                axis=0)
            w0 = pl.multiple_of(jnp.maximum(q0 + sub * WINDOW_Q - WINDOW, 0), WINDOW_Q)
            s = _dot_nt(pick(q), kw_ref[0, g, pl.ds(w0, span), :])
            diff = pick(t4) - (w0 + lax.broadcasted_iota(I32, (1, span), 1))
            e = _masked_exp2(s, (diff >= 0) & (diff < WINDOW))
            o = _dot(e.astype(BF16), vw_ref[0, g, pl.ds(w0, span), :])
            parts.append(o[:, :HEAD_DIM] * _safe_recip(o[:, HEAD_DIM:HEAD_DIM + 1]))
        o_win = jnp.concatenate([parts[sub][hh * WINDOW_Q:(hh + 1) * WINDOW_Q]
                                 for hh in range(NSA_HPG) for sub in range(len(parts))], axis=0)

        q_aug = jnp.concatenate([jnp.concatenate([sel_bias.astype(BF16)] * NSA_HPG, axis=0), q,
                                 jnp.zeros((rows, LANES - HEAD_DIM), BF16)], axis=1)
        return q_aug, o_cmp, o_win

    fronts = [front(g) for g in groups]

    def sel_tile(g, j, carry, causal):
        m_run, acc = carry
        k0 = pl.multiple_of(j * tk, tk)
        sc = _dot_nt(fronts[g][0], ks_ref[0, g, pl.ds(k0, tk), :])
        if causal:
            kpos = k0 + lax.broadcasted_iota(I32, (1, tk), 1)
            sc = jnp.where(kpos <= t4, sc, NEG_BIG)
        m_new = jnp.maximum(m_run, jnp.max(sc, axis=-1, keepdims=True))
        p = jnp.exp2(sc - m_new)
        acc_new = jnp.exp2(m_run - m_new) * acc + _dot(p.astype(BF16), vs_ref[0, g, pl.ds(k0, tk), :])
        return m_new, acc_new

    def sel_pair(jj, carries, causal):
        return tuple(sel_tile(g, 2 * jj + 1, sel_tile(g, 2 * jj, carries[g], causal), causal) for g in groups)

    init = tuple((jnp.full((rows, 1), NEG_BIG, F32), jnp.zeros((rows, LANES), F32)) for _ in groups)
    last_pair = (q0 // tk) // 2
    carries = sel_pair(last_pair, init, True)
    carries = lax.fori_loop(
        0, last_pair // 2, lambda jq, cs: sel_pair(2 * jq + 1, sel_pair(2 * jq, cs, False), False), carries)
    carries = lax.fori_loop(0, last_pair % 2, lambda _, cs: sel_pair(last_pair - 1, cs, False), carries)

    outs = []
    for g in groups:
        _, o_cmp, o_win = fronts[g]
        acc = carries[g][1]
        o_sel = acc[:, :HEAD_DIM] * (1.0 / acc[:, HEAD_DIM:HEAD_DIM + 1])
        gt = gate_ref[0, :, g * LANES:(g + 1) * LANES]
        for hh in range(NSA_HPG):
            sl = slice(hh * Q_BLOCK, (hh + 1) * Q_BLOCK)
            c = hh * N_BRANCH
            outs.append(o_cmp[sl] * gt[:, c:c + 1] + o_sel[sl] * gt[:, c + 1:c + 2]
                        + o_win[sl] * gt[:, c + 2:c + 3])
    o_ref[0] = jnp.concatenate(outs, axis=1).astype(BF16)


def _nsa(q_hm, ck, cv, ks, vs, kw, vw, gates, w_score):
    B, _, L, _ = q_hm.shape
    assert L // SEL_BLOCK <= LANES and (L // SEL_KV_TILE) % 2 == 0 and L >= WINDOW + Q_BLOCK
    n_cmp = ck.shape[2]
    grid = (B, L // Q_BLOCK)
    qspec = pl.BlockSpec((1, NSA_HEADS, Q_BLOCK, HEAD_DIM), lambda b, i: (b, 0, i, 0))
    cspec = pl.BlockSpec((1, NSA_GROUPS, n_cmp, HEAD_DIM), lambda b, i: (b, 0, 0, 0))
    kvspec = lambda w: pl.BlockSpec((1, NSA_GROUPS, L, w), lambda b, i: (b, 0, 0, 0),
                                    pipeline_mode=pl.Buffered(1))
    gspec = pl.BlockSpec((1, Q_BLOCK, NSA_GROUPS * LANES), lambda b, i: (b, i, 0))
    ospec = pl.BlockSpec((1, Q_BLOCK, NSA_WIDTH), lambda b, i: (b, i, 0))
    return pl.pallas_call(
        _nsa_kernel, grid=grid,
        in_specs=[qspec, cspec, cspec, kvspec(2 * LANES), kvspec(LANES), kvspec(HEAD_DIM), kvspec(LANES), gspec,
                  _full_spec(w_score.shape)],
        out_specs=ospec, out_shape=jax.ShapeDtypeStruct((B, L, NSA_WIDTH), BF16),
        compiler_params=_cparams("parallel", "arbitrary"), name="nsa",
    )(q_hm, ck, cv, ks, vs, kw, vw, gates, w_score)


def _s5_kernel(u_ref, wb_ref, wc_ref, are_ref, aim_ref, d_ref, y_ref, sre_ref, sim_ref, carry_ref):
    n_b, chunk, _ = u_ref.shape
    n_tile = wb_ref.shape[0]
    in_per = n_tile // (S5_WIDTH // LANES)
    pitch = S5_PITCH

    @pl.when(pl.program_id(0) == 0)
    def _():
        carry_ref[...] = jnp.zeros_like(carry_ref)

    for b in range(n_b):
        for c in range(n_tile):
            i = c // in_per
            ub = u_ref[b, :, i * LANES:(i + 1) * LANES].astype(BF16)
            r = _dot(ub, wb_ref[c])
            sre_ref[b, c * pitch:c * pitch + chunk, :] = r[:, :LANES]
            sim_ref[b, c * pitch:c * pitch + chunk, :] = r[:, LANES:]

    a_re, a_im = are_ref[...], aim_ref[...]

    def step(t, carry):
        out = []
        for b in range(n_b):
            s_re, s_im = carry[2 * b], carry[2 * b + 1]
            rows = pl.ds(t, n_tile, stride=pitch)
            n_re = a_re * s_re - a_im * s_im + sre_ref[b, rows, :]
            n_im = a_re * s_im + a_im * s_re + sim_ref[b, rows, :]
            sre_ref[b, rows, :] = n_re
            sim_ref[b, rows, :] = n_im
            out += [n_re, n_im]
        return tuple(out)

    init = tuple(carry_ref[i] for i in range(2 * n_b))
    fin = lax.fori_loop(0, chunk, step, init, unroll=True)
    for i in range(2 * n_b):
        carry_ref[i] = fin[i]

    for b in range(n_b):
        for o in range(S5_WIDTH // LANES):
            acc = jnp.zeros((chunk, LANES), F32)
            for c in range(o * in_per, (o + 1) * in_per):
                rows = slice(c * pitch, c * pitch + chunk)
                state = jnp.concatenate([sre_ref[b, rows, :], sim_ref[b, rows, :]], axis=1).astype(BF16)
                acc = acc + _dot(state, wc_ref[c])
            lanes = slice(o * LANES, (o + 1) * LANES)
            y = acc + d_ref[:, lanes] * u_ref[b, :, lanes]
            y_ref[b, :, lanes] = _gelu_tanh(y).astype(BF16)


def _s5(u, wb, wc, a_re, a_im, d_skip):
    B, L, W = u.shape
    chunk = S5_CHUNK
    n_tile = wb.shape[0]
    blk = pl.BlockSpec((B, chunk, W), lambda i: (0, i, 0))
    slab = pltpu.VMEM((B, n_tile * S5_PITCH, LANES), F32)
    return pl.pallas_call(
        _s5_kernel, grid=(L // chunk,),
        in_specs=[blk] + [_full_spec(w.shape) for w in (wb, wc, a_re, a_im, d_skip)],
        out_specs=blk, out_shape=jax.ShapeDtypeStruct((B, L, W), BF16),
        scratch_shapes=[slab, slab, pltpu.VMEM((2 * B, n_tile, LANES), F32)],
        compiler_params=_cparams("arbitrary"), name="s5",
    )(u, wb, wc, a_re, a_im, d_skip)


def _memkv_kernel(mem_ref, w_ref, k_ref, v_ref):
    kv = _dot(mem_ref[0].astype(BF16), w_ref[...])
    k_ref[0] = kv[:, :MEM_WIDTH].astype(BF16)
    v_ref[0] = kv[:, MEM_WIDTH:].astype(BF16)


def _memkv(mem, w_kv):
    B, M, D = mem.shape
    out = pl.BlockSpec((1, M, MEM_WIDTH), lambda b: (b, 0, 0))
    sd = jax.ShapeDtypeStruct((B, M, MEM_WIDTH), BF16)
    return pl.pallas_call(
        _memkv_kernel, grid=(B,),
        in_specs=[pl.BlockSpec((1, M, D), lambda b: (b, 0, 0)), _full_spec(w_kv.shape)],
        out_specs=[out, out], out_shape=[sd, sd], compiler_params=_cparams("parallel"), name="memkv",
    )(mem, w_kv)


def _memory_attention(q_ref, k_ref, v_ref):
    outs = []
    for h in range(MEM_HEADS):
        sl = slice(h * MEM_HEAD_DIM, (h + 1) * MEM_HEAD_DIM)
        s = _dot_nt(q_ref[:, sl], k_ref[0, :, sl]) * (MEM_HEAD_DIM ** -0.5)
        m = jnp.max(s, axis=-1, keepdims=True)
        e = jnp.exp(s - m)
        p = e / jnp.sum(e, axis=-1, keepdims=True)
        outs.append(_dot(p.astype(BF16), v_ref[0, :, sl]))
    return jnp.concatenate(outs, axis=1).astype(BF16)


def _merge_kernel(x_ref, lng_ref, lnb_ref, on_ref, gy_ref, qm_ref, km_ref, vm_ref, gm_ref,
                  wn_ref, wglu_ref, wmo_ref, wo_ref, l1g_ref, l1b_ref,
                  wrh_ref, wrp_ref, br_ref, tri_ref, striu_ref,
                  h1_ref, lp_ref, w4_ref, cnt_ref):
    D = x_ref.shape[1]
    tm = x_ref.shape[0]
    h =_layer_norm(x_ref[...], lng_ref[...], lnb_ref[...])
    y_nsa = _dot(on_ref[...], wn_ref[...])
    glu = _dot(gy_ref[...], wglu_ref[...])
    y_s5 = glu[:, :D] * jax.nn.sigmoid(glu[:, D:])
    y_mem = _dot(_memory_attention(qm_ref, km_ref, vm_ref), wmo_ref[...])
    merged = gm_ref[:, 0:D] * y_nsa + gm_ref[:, D:2 * D] * y_s5 + gm_ref[:, 2 * D:3 * D] * y_mem
    mix = _dot(merged.astype(BF16), wo_ref[...])
    h1 = _layer_norm(DEEPNORM_ALPHA * h + mix, l1g_ref[...], l1b_ref[...])
    h1_ref[...] = h1

    hh = h1.astype(BF16)
    hl = (h1 - hh.astype(F32)).astype(BF16)
    both = _dot(hh, wrp_ref[...])
    logits = both[:, :LANES] + both[:, LANES:] + _dot(hl, wrh_ref[...]) + br_ref[...]
    lane = lax.broadcasted_iota(I32, (tm, LANES), 1)
    lane_f = lane.astype(F32)
    work = logits
    multi = jnp.zeros((tm, LANES), F32)
    vals, picks = [], []
    for _ in range(TOP_K):
        m = jnp.max(work, axis=-1, keepdims=True)
        idx = jnp.min(jnp.where(work == m, lane_f, float(LANES)), axis=-1, keepdims=True)
        pick = lane_f == idx
        vals.append(m)
        picks.append((pick, idx))
        multi = jnp.where(pick, 1.0, multi)
        work = jnp.where(pick, -jnp.inf, work)
    es = [jnp.exp(v - vals[0]) for v in vals]
    den = es[0] + es[1] + es[2] + es[3]
    st = TOKEN_TILE
    pos = []
    for t in range(tm // st):
        multi_t = multi[t * st:(t + 1) * st]
        cnt = jnp.broadcast_to(jnp.sum(multi_t, axis=0, keepdims=True), (SUBLANES, LANES))
        cnt_ref[t] = cnt
        lower = _dot(cnt.astype(BF16), striu_ref[...])[0:1]
        pos.append(lower + _dot(tri_ref[...], multi_t.astype(BF16)))
    pos = jnp.concatenate(pos, axis=0)
    lp = jnp.full((tm, LANES), -1.0, F32)
    w4 = jnp.zeros((tm, LANES), F32)
    for k in range(TOP_K):
        pick, _ = picks[k]
        lp = jnp.where(lane == k, jnp.sum(jnp.where(pick, pos, 0.0), axis=-1, keepdims=True), lp)
        w4 = jnp.where(lane == k, es[k] / den, w4)
    lp_ref[...] = lp
    w4_ref[...] = w4


def _merge(x2, lng, lnb, o_nsa, gy, qm, k_mem, v_mem, gm, wn, wglu, wmo, wo, l1g, l1b, wrh, wrp, br, tri, striu):
    T, D = x2.shape
    tm = MERGE_SORT_TILES * TOKEN_TILE
    tok = lambda w: pl.BlockSpec((tm, w), lambda i: (i, 0))
    steps_per_batch = T // k_mem.shape[0] // tm
    mem_kv = pl.BlockSpec((1,) + k_mem.shape[1:], lambda i: (i // steps_per_batch, 0, 0))
    ws = [wn, wglu, wmo, wo, l1g, l1b, wrh, wrp, br, tri, striu]
    sd = jax.ShapeDtypeStruct
    lane_out = sd((T, LANES), F32)
    return pl.pallas_call(
        _merge_kernel, grid=(T // tm,),
        in_specs=[tok(D), _full_spec((1, D)), _full_spec((1, D)), tok(NSA_WIDTH), tok(S5_WIDTH),
                  tok(MEM_WIDTH), mem_kv, mem_kv, tok(N_BRANCH * D)] + [_full_spec(w.shape) for w in ws],
        out_specs=[tok(D), tok(LANES), tok(LANES),
                   pl.BlockSpec((MERGE_SORT_TILES, SUBLANES, LANES), lambda i: (i, 0, 0))],
        out_shape=[sd((T, D), F32), lane_out, lane_out, sd((T // TOKEN_TILE, SUBLANES, LANES), F32)],
        compiler_params=_cparams("parallel"), name="merge",
    )(x2, lng, lnb, o_nsa, gy, qm, k_mem, v_mem, gm, *ws)


def _slots_kernel(cnt_ref, triu_ref, striu_ref, tril_ref, seg_ref, blk_ref, misc_ref):
    n_blk = blk_ref.shape[0]
    cnt = cnt_ref[...]
    cnt_b = cnt.astype(BF16)
    total = jnp.sum(cnt, axis=0, keepdims=True)
    nblk_e = jnp.floor((total + (MOE_ROWS - 1)) * (1.0 / MOE_ROWS))
    nblk_8 = jnp.broadcast_to(nblk_e, (SUBLANES, LANES))
    end_b = _dot(nblk_8.astype(BF16), triu_ref[...])
    start_rows = (end_b - nblk_8)[0:1] * MOE_ROWS
    dst = start_rows + _dot(tril_ref[...], cnt_b)
    off = _dot(cnt_b, striu_ref[...])
    seg_ref[0] = cnt.astype(I32)
    seg_ref[1] = off.astype(I32)
    seg_ref[2] = dst.astype(I32)
    blk_i = lax.broadcasted_iota(I32, (n_blk, LANES), 0).astype(F32)
    lane_b = lax.broadcasted_iota(I32, (n_blk, LANES), 1)
    ended = jnp.where((end_b[0:1] <= blk_i) & (lane_b < N_EXPERTS), 1.0, 0.0)
    owner = jnp.minimum(jnp.sum(ended, axis=-1, keepdims=True), float(N_EXPERTS - 1))
    mine = lane_b.astype(F32) == owner
    pick = lambda per_expert: jnp.sum(jnp.where(mine, per_expert, 0.0), axis=-1, keepdims=True)
    earlier = blk_i[:, 0:1] - (pick(end_b[0:1]) - pick(nblk_e))
    held = jnp.clip(pick(total) - earlier * MOE_ROWS, 0.0, float(MOE_ROWS))
    blk_ref[...] = jnp.where(lane_b == 1, held, owner).astype(I32)
    cand = lax.broadcasted_iota(I32, (LANES, LANES), 0)
    has_blocks = jnp.broadcast_to(nblk_e, (LANES, LANES)).T > 0.0
    later = (cand > lax.broadcasted_iota(I32, (LANES, LANES), 1)) & has_blocks
    nxt = jnp.min(jnp.where(later, cand.astype(F32), float(LANES)), axis=0, keepdims=True)
    nxt = jnp.where(nxt < float(LANES), nxt, -1.0)
    lane8 = lax.broadcasted_iota(I32, (SUBLANES, LANES), 1)
    row8 = lax.broadcasted_iota(I32, (SUBLANES, LANES), 0)
    used = jnp.sum(jnp.where(lane8 == N_EXPERTS - 1, end_b, 0.0), axis=-1, keepdims=True)
    misc = jnp.where(row8 == 0, used,
                     jnp.where(row8 == 1, start_rows + total,
                               jnp.where(row8 == 2, nblk_e * MOE_ROWS - total, nxt)))
    misc_ref[...] = misc.astype(I32)


def _slots(cnt, triu, striu, tril, n_blk):
    n_tile = cnt.shape[0]
    sd = jax.ShapeDtypeStruct
    return pl.pallas_call(
        _slots_kernel, grid=(1,),
        in_specs=[_full_spec(cnt.shape), _full_spec(triu.shape), _full_spec(striu.shape), _full_spec(tril.shape)],
        out_specs=[_full_spec((3, n_tile, LANES)), _full_spec((n_blk, LANES)), _full_spec((SUBLANES, LANES))],
        out_shape=[sd((3, n_tile, LANES), I32), sd((n_blk, LANES), I32), sd((SUBLANES, LANES), I32)],
        compiler_params=_cparams("arbitrary"), name="slots",
    )(cnt, triu, striu, tril)


ROW_TILES = D_MODEL // LANES


def _row_span(row, n_rows):
    start = row * ROW_TILES
    if not isinstance(start, int):
        start = pl.multiple_of(start, ROW_TILES)
    return pl.ds(start, n_rows * ROW_TILES)


def _store_rows(ref, val):
    for c in range(ROW_TILES):
        ref[pl.ds(c, val.shape[0], stride=ROW_TILES), :] = val[:, c * LANES:(c + 1) * LANES]


def _load_row_tile(ref, n_rows, c):
    return ref[pl.ds(c, n_rows, stride=ROW_TILES), :]


BIG_PIECE_ROWS = 64


def _pieces(count, max_rows, fn):
    def run(sizes):
        for p in sizes:
            def piece(p=p):
                fn(count & (-2 * p), p)
            pl.when((count & p) != 0)(piece)

    sizes = [max_rows >> s for s in range(max_rows.bit_length())]
    big = [p for p in sizes if p >= BIG_PIECE_ROWS]
    if big:
        pl.when(count >= BIG_PIECE_ROWS)(lambda: run(big))
    run([p for p in sizes if p < BIG_PIECE_ROWS])


def _start_segment_copies(seg_ref, max_rows, make_copy):
    def per_expert(e, c):
        cnt, off, dst = seg_ref[0, 0, 0, e], seg_ref[1, 0, 0, e], seg_ref[2, 0, 0, e]
        _pieces(cnt, max_rows, lambda first, rows: make_copy(off + first, dst + first, rows).start())
        return c

    lax.fori_loop(0, N_EXPERTS, per_expert, 0)


def _dispatch_kernel(seg_ref, misc_ref, lp_ref, h_ref, xs_ref, sorted_ref, zero_ref, sem, pad_sem):
    i = pl.program_id(0)
    n = pl.num_programs(0)
    tm, D = h_ref.shape
    rows = TOP_K * tm
    slot = lax.rem(i, 2)

    def row_copy(slot_):
        def make(src_row, dst_row, n_rows):
            return pltpu.make_async_copy(sorted_ref.at[slot_, _row_span(src_row, n_rows)],
                                         xs_ref.at[_row_span(dst_row, n_rows)], sem.at[slot_])
        return make

    @pl.when(i == 0)
    def _():
        zero_ref[...] = jnp.zeros_like(zero_ref)
        for wait in (False, True):
            def per_expert(e, c, wait=wait):
                def one(first, n_rows):
                    cp = pltpu.make_async_copy(zero_ref.at[_row_span(0, n_rows)],
                                               xs_ref.at[_row_span(misc_ref[1, e] + first, n_rows)], pad_sem)
                    cp.wait() if wait else cp.start()
                _pieces(misc_ref[2, e], MOE_ROWS // 2, one)
                return c
            lax.fori_loop(0, N_EXPERTS, per_expert, 0)

            def per_spare_half_block(hb, c, wait=wait):
                cp = pltpu.make_async_copy(zero_ref, xs_ref.at[_row_span(hb * (MOE_ROWS // 2), MOE_ROWS // 2)],
                                           pad_sem)
                cp.wait() if wait else cp.start()
                return c
            lax.fori_loop(2 * misc_ref[0, 0], 2 * (xs_ref.shape[0] // (MOE_ROWS * ROW_TILES)),
                          per_spare_half_block, 0)

    lp_t = lp_ref[...].T
    s_ix = lax.broadcasted_iota(I32, (rows, 1), 0).astype(F32)
    hit = s_ix == lp_t[0:1, :]
    for k in range(1, TOP_K):
        hit = hit | (s_ix == lp_t[k:k + 1, :])
    perm = jnp.where(hit, 1.0, 0.0).astype(BF16)
    _store_rows(sorted_ref.at[slot], _dot(perm, h_ref[...].astype(BF16)))

    _start_segment_copies(seg_ref, tm, row_copy(slot))

    @pl.when(i > 0)
    def _():
        row_copy(1 - slot)(0, 0, rows).wait()

    @pl.when(i == n - 1)
    def _():
        row_copy(slot)(0, 0, rows).wait()


def _seg_spec(index_map):
    return pl.BlockSpec((3, 1, 1, LANES), index_map, memory_space=pltpu.SMEM)


def _dispatch(seg4, misc, lp, h1, cap):
    T, D = h1.shape
    assert D == ROW_TILES * LANES
    tm = TOKEN_TILE
    tok = lambda w: pl.BlockSpec((tm, w), lambda i: (i, 0))
    return pl.pallas_call(
        _dispatch_kernel, grid=(T // tm,),
        in_specs=[_seg_spec(lambda i: (0, i, 0, 0)), pl.BlockSpec(memory_space=pltpu.SMEM), tok(LANES), tok(D)],
        out_specs=pl.BlockSpec(memory_space=pl.ANY),
        out_shape=jax.ShapeDtypeStruct((cap * ROW_TILES, LANES), F32),
        scratch_shapes=[pltpu.VMEM((2, TOP_K * tm * ROW_TILES, LANES), F32),
                        pltpu.VMEM((MOE_ROWS // 2 * ROW_TILES, LANES), F32),
                        pltpu.SemaphoreType.DMA((2,)), pltpu.SemaphoreType.DMA(())],
        compiler_params=_cparams("arbitrary"), name="dispatch",
    )(seg4, misc, lp, h1)


def _expert_kernel(blk_ref, used_ref, next_ref, rows_ref, xs_ref, wgu_hbm, bgu_ref, wd_hbm, bd_ref, ys_ref,
                   wgu_f32, wd_f32, wgu_bf, wd_bf, sem, run_ref):
    i = pl.program_id(0)
    live = i < used_ref[0]
    expert = blk_ref[i]

    def weight_copies(e, slot):
        return (pltpu.make_async_copy(wgu_hbm.at[e], wgu_f32.at[slot], sem.at[0, slot]),
                pltpu.make_async_copy(wd_hbm.at[e], wd_f32.at[slot], sem.at[1, slot]))

    @pl.when(i == 0)
    def _():
        run_ref[0] = 0
        for cp in weight_copies(expert, 0):
            cp.start()

    @pl.when(live & ((i == 0) | (expert != blk_ref[jnp.maximum(i - 1, 0)])))
    def _():
        slot = lax.rem(run_ref[0], 2)
        for cp in weight_copies(expert, slot):
            cp.wait()
        wgu_bf[...] = wgu_f32[slot].astype(BF16)
        wd_bf[...] = wd_f32[slot].astype(BF16)
        nxt = next_ref[expert]

        @pl.when(nxt >= 0)
        def _():
            for cp in weight_copies(nxt, 1 - slot):
                cp.start()

        run_ref[0] = run_ref[0] + 1

    def expert_rows(n_rows):
        xb = jnp.concatenate([_load_row_tile(xs_ref, n_rows, c).astype(BF16) for c in range(ROW_TILES)], axis=1)
        gu = _dot(xb, wgu_bf[...]) + bgu_ref[0]
        g = jnp.minimum(gu[:, :D_FF], SWIGLU_LIMIT)
        lin = jnp.clip(gu[:, D_FF:], -SWIGLU_LIMIT, SWIGLU_LIMIT)
        act = g * jax.nn.sigmoid(SWIGLU_ALPHA * g) * (lin + 1.0)
        _store_rows(ys_ref, _dot(act.astype(BF16), wd_bf[...]) + bd_ref[0])

    half = MOE_ROWS // 2
    half_full = rows_ref[i] <= half

    @pl.when(live & jnp.logical_not(half_full))
    def _():
        expert_rows(MOE_ROWS)

    @pl.when(live & half_full)
    def _():
        expert_rows(half)
        ys_ref[half * ROW_TILES:, :] = jnp.zeros((half * ROW_TILES, LANES), F32)

    @pl.when(pl.program_id(0) >= used_ref[0])
    def _():
        ys_ref[...] = jnp.zeros_like(ys_ref)


def _experts(blk_expert, n_used, next_expert, blk_rows, xs, w_gate_up, b_gate_up, w_down, b_down):
    D = w_down.shape[2]
    n_blk = xs.shape[0] // (MOE_ROWS * ROW_TILES)
    E = w_gate_up.shape[0]
    live = lambda i, used: jnp.minimum(i, used[0] - 1)
    row = pl.BlockSpec((MOE_ROWS * ROW_TILES, LANES), lambda i, blk, used, nxt, held: (live(i, used), 0))
    by_e = lambda shape: pl.BlockSpec((1,) + shape, lambda i, blk, used, nxt, held: (blk[live(i, used)], 0, 0))
    in_hbm = pl.BlockSpec(memory_space=pl.ANY)
    grid_spec = pltpu.PrefetchScalarGridSpec(
        num_scalar_prefetch=4, grid=(n_blk,),
        in_specs=[row, in_hbm, by_e((1, 2 * D_FF)), in_hbm, by_e((1, D))],
        out_specs=pl.BlockSpec((MOE_ROWS * ROW_TILES, LANES), lambda i, blk, used, nxt, held: (i, 0)),
        scratch_shapes=[pltpu.VMEM((2, D, 2 * D_FF), F32), pltpu.VMEM((2, D_FF, D), F32),
                        pltpu.VMEM((D, 2 * D_FF), BF16), pltpu.VMEM((D_FF, D), BF16),
                        pltpu.SemaphoreType.DMA((2, 2)), pltpu.SMEM((1,), I32)])
    return pl.pallas_call(
        _expert_kernel, grid_spec=grid_spec, out_shape=jax.ShapeDtypeStruct(xs.shape, F32),
        compiler_params=_cparams("arbitrary"), name="experts",
    )(blk_expert, n_used, next_expert, blk_rows, xs, w_gate_up, b_gate_up.reshape(E, 1, 2 * D_FF), w_down,
      b_down.reshape(E, 1, D))


def _combine_kernel(seg_ref, segn_ref, lp_ref, w4_ref, h1_ref, g_ref, b_ref, ys_ref, o_ref, buf_ref, sem):
    i = pl.program_id(0)
    n = pl.num_programs(0)
    tm = h1_ref.shape[0]
    rows = TOP_K * tm
    slot = lax.rem(i, 2)

    def row_copy(slot_):
        def make(buf_row, ys_row, n_rows):
            return pltpu.make_async_copy(ys_ref.at[_row_span(ys_row, n_rows)],
                                         buf_ref.at[slot_, _row_span(buf_row, n_rows)], sem.at[slot_])
        return make

    @pl.when(i == 0)
    def _():
        _start_segment_copies(seg_ref, tm, row_copy(slot))

    @pl.when(i + 1 < n)
    def _():
        _start_segment_copies(segn_ref, tm, row_copy(1 - slot))

    row_copy(slot)(0, 0, rows).wait()

    lp = lp_ref[...]
    s_ix = lax.broadcasted_iota(I32, (1, rows), 1).astype(F32)
    wmat = jnp.zeros((tm, rows), F32)
    for k in range(TOP_K):
        wmat = jnp.where(s_ix == lp[:, k:k + 1], w4_ref[:, k:k + 1], wmat)
    w_hi = wmat.astype(BF16)
    w_lo = (wmat - w_hi.astype(F32)).astype(BF16)
    y = jnp.concatenate([_load_row_tile(buf_ref.at[slot], rows, c) for c in range(ROW_TILES)], axis=1)
    y_hi = y.astype(BF16)
    y_lo = (y - y_hi.astype(F32)).astype(BF16)
    acc = DEEPNORM_ALPHA * h1_ref[...] + (_dot(w_hi, y_hi) + _dot(w_hi, y_lo) + _dot(w_lo, y_hi))
    o_ref[...] = _layer_norm(acc, g_ref[...], b_ref[...])


def _combine(seg4, lp, w4, h1, ln_g, ln_b, ys):
    T, D = h1.shape
    tm = TOKEN_TILE
    n_tile = T // tm
    return pl.pallas_call(
        _combine_kernel, grid=(n_tile,),
        in_specs=[_seg_spec(lambda i: (0, i, 0, 0)),
                  _seg_spec(lambda i: (0, jnp.minimum(i + 1, n_tile - 1), 0, 0)),
                  pl.BlockSpec((tm, LANES), lambda i: (i, 0)),
                  pl.BlockSpec((tm, LANES), lambda i: (i, 0)),
                  pl.BlockSpec((tm, D), lambda i: (i, 0)),
                  _full_spec((1, D)), _full_spec((1, D)),
                  pl.BlockSpec(memory_space=pl.ANY)],
        out_specs=pl.BlockSpec((tm, D), lambda i: (i, 0)),
        out_shape=jax.ShapeDtypeStruct((T, D), F32),
        scratch_shapes=[pltpu.VMEM((2, TOP_K * tm * ROW_TILES, LANES), F32), pltpu.SemaphoreType.DMA((2,))],
        compiler_params=_cparams("arbitrary"), name="combine",
    )(seg4, seg4, lp, w4, h1, ln_g, ln_b, ys)


def _rope_tables(positions):
    inv = ROPE_THETA ** (-jnp.arange(0, ROT_DIM, 2, dtype=F32) / ROT_DIM)
    ang = positions.astype(F32)[..., None] * inv
    cos_sin = jnp.concatenate([jnp.cos(ang), jnp.sin(ang)], axis=-1)
    half = ROT_DIM // 2
    spread = np.zeros((ROT_DIM, 3 * LANES), np.float32)
    unit = np.ones((1, LANES), np.float32)
    for lane in range(LANES):
        d = lane % HEAD_DIM
        if d < half:
            spread[d, lane] = 1.0
            spread[half + d, 2 * LANES + lane] = -1.0
            unit[0, lane] = 0.0
        elif d < ROT_DIM:
            spread[d - half, lane] = 1.0
            spread[d, LANES + lane] = 1.0
            unit[0, lane] = 0.0
    return cos_sin, jnp.asarray(spread, BF16), jnp.asarray(unit)


def _split_w_in(w_in):
    widths = (NSA_WIDTH,) + (KV_WIDTH,) * 6 + (NSA_HEADS * N_BRANCH, S5_WIDTH, MEM_WIDTH, N_BRANCH * D_MODEL)
    offs = [0]
    for w in widths:
        offs.append(offs[-1] + w)
    col = lambda i: w_in[:, offs[i]:offs[i + 1]]
    wq, kc, vc, ks, vs, kw, vw, wg, wu, wqm, wm = (col(i) for i in range(11))
    wk = jnp.concatenate([kc, ks, kw], axis=1)
    wv = jnp.concatenate([vc, vs, vw], axis=1)
    per_group = NSA_HPG * N_BRANCH
    wg_pad = jnp.zeros((w_in.shape[0], NSA_GROUPS * LANES), w_in.dtype)
    for g in range(NSA_GROUPS):
        wg_pad = wg_pad.at[:, g * LANES:g * LANES + per_group].set(wg[:, g * per_group:(g + 1) * per_group])
    return tuple(w.astype(BF16) for w in (wq, wk, wv, wg_pad, wu, wqm, wm))


def _compress_weights(w1):
    half = CMP_BLOCK // 2
    eye = np.eye(NSA_GROUPS, dtype=np.float32)

    def arrange(w_half):
        full = jnp.einsum('sdf,gh->sgdhf', w_half, eye)
        return full.reshape(half * NSA_GROUPS * HEAD_DIM, NSA_GROUPS * CMP_HIDDEN).astype(BF16)

    return (w1.reshape(CMP_BLOCK * HEAD_DIM, CMP_HIDDEN).astype(BF16), arrange(w1[:half]), arrange(w1[half:]))


def _s5_weights(a_re, a_im, log_dt, b_re, b_im, c_re, c_im):
    step = jnp.exp(log_dt)[:, None]
    mag = jnp.exp(a_re * step)
    ab_re, ab_im = mag * jnp.cos(a_im * step), mag * jnp.sin(a_im * step)
    den = a_re * a_re + a_im * a_im
    nr = ab_re - 1.0
    coef_re = (nr * a_re + ab_im * a_im) / den
    coef_im = (ab_im * a_re - nr * a_im) / den
    bb_re = coef_re[..., None] * b_re - coef_im[..., None] * b_im
    bb_im = coef_re[..., None] * b_im + coef_im[..., None] * b_re
    n_tile = S5_GROUPS * S5_STATE // LANES
    tile_groups = LANES // S5_STATE
    lane_groups = LANES // S5_GROUP_DIM
    tiles_per_lane_tile = lane_groups // tile_groups
    place = np.zeros((n_tile, lane_groups, tile_groups), np.float32)
    for c in range(n_tile):
        for j in range(tile_groups):
            place[c, (c % tiles_per_lane_tile) * tile_groups + j, j] = 1.0

    def in_blocks(bb):
        pairs = bb.reshape(n_tile, tile_groups, S5_STATE, S5_GROUP_DIM)
        return jnp.einsum('cjnp,caj->capjn', pairs, place).reshape(n_tile, LANES, LANES)

    def out_blocks(c):
        pairs = c.reshape(n_tile, tile_groups, S5_GROUP_DIM, S5_STATE)
        return jnp.einsum('cjpn,caj->cjnap', pairs, place).reshape(n_tile, LANES, LANES)

    wb = jnp.concatenate([in_blocks(bb_re), in_blocks(bb_im)], axis=2).astype(BF16)
    wc = jnp.concatenate([out_blocks(c_re), out_blocks(-c_im)], axis=1).astype(BF16)
    return wb, wc, ab_re.reshape(n_tile, LANES), ab_im.reshape(n_tile, LANES)


def _layer(x, mem, positions, ln_emb_g, ln_emb_b, w_in, pe_k, pe_v, w_kcmp1, w_kcmp2, w_vcmp1, w_vcmp2,
           s5_a_re, s5_a_im, s5_log_dt, s5_b_re, s5_b_im, s5_c_re, s5_c_im, s5_d,
           w_s5_glu, w_mem_kv, w_nsa_out, w_mem_out, w_o, ln1_g, ln1_b, w_router, b_router,
           w_gate_up, b_gate_up, w_down, b_down, ln2_g, ln2_b):
    B, L, D = x.shape
    T = B * L
    row = lambda v: v.reshape(1, -1)

    cos_sin, spread, unit = _rope_tables(positions)
    (q_hm, kc, vc, ks, vs, kw, vw, gates, u, qm, gm) = _inproj(
        x, row(ln_emb_g), row(ln_emb_b), cos_sin, spread, unit, *_split_w_in(w_in))

    n_chunk = L // CMP_STRIDE
    chunked = lambda t: t.reshape(B, n_chunk, CMP_STRIDE * KV_WIDTH)
    pe_rows = lambda pe: jnp.broadcast_to(pe.reshape(1, -1), (SUBLANES, CMP_BLOCK * HEAD_DIM)).astype(BF16)
    wk1f, wk1a, wk1b = _compress_weights(w_kcmp1)
    wv1f, wv1a, wv1b = _compress_weights(w_vcmp1)
    ck, cv = _compress(chunked(kc), chunked(vc), pe_rows(pe_k), pe_rows(pe_v), wk1f, wv1f,
                       wk1a, wk1b, wv1a, wv1b, w_kcmp2.astype(BF16), w_vcmp2.astype(BF16))

    per_sb = SEL_BLOCK // CMP_STRIDE
    c_ix = np.arange(n_chunk)[:, None]
    n_ix = np.arange(L // SEL_BLOCK)[None, :]
    w_score = jnp.asarray((c_ix // per_sb == n_ix).astype(np.float32)
                          + ((c_ix + 1) // per_sb == n_ix).astype(np.float32), BF16)
    o_nsa = _nsa(q_hm, ck, cv, ks, vs, kw, vw, gates, w_score)

    wb, wc, a_re, a_im = _s5_weights(s5_a_re, s5_a_im, s5_log_dt, s5_b_re, s5_b_im, s5_c_re, s5_c_im)
    gy = _s5(u, wb, wc, a_re, a_im, row(s5_d))

    k_mem, v_mem = _memkv(mem, w_mem_kv.astype(BF16))

    pad_e = LANES - N_EXPERTS
    wr = jnp.pad(w_router, ((0, 0), (0, pad_e)))
    wr_hi = wr.astype(BF16)
    wr_pair = jnp.concatenate([wr_hi, (wr - wr_hi.astype(F32)).astype(BF16)], axis=1)
    br = jnp.concatenate([b_router, jnp.full((pad_e,), -jnp.inf, F32)]).reshape(1, LANES)
    tm = TOKEN_TILE
    n_tile = T // tm
    strict_lower = lambda n: jnp.asarray(np.tril(np.ones((n, n), np.float32), -1), BF16)
    triu = jnp.asarray(np.triu(np.ones((LANES, LANES), np.float32)), BF16)
    striu = jnp.asarray(np.triu(np.ones((LANES, LANES), np.float32), 1), BF16)
    flat = lambda t: t.reshape(T, t.shape[-1])
    h1, lp, w4, cnt = _merge(
        flat(x), row(ln_emb_g), row(ln_emb_b), flat(o_nsa), flat(gy), flat(qm), k_mem, v_mem, flat(gm),
        w_nsa_out.astype(BF16), w_s5_glu.astype(BF16), w_mem_out.astype(BF16), w_o.astype(BF16),
        row(ln1_g), row(ln1_b), wr_hi, wr_pair, br, strict_lower(tm), striu)

    cap = (T * TOP_K + MOE_ROWS - 1) // MOE_ROWS * MOE_ROWS + N_EXPERTS * MOE_ROWS
    n_blk = cap // MOE_ROWS
    seg, blk_owner, misc = _slots(cnt[:, 0, :], triu, striu, strict_lower(n_tile), n_blk)
    seg4 = seg.reshape(3, n_tile, 1, LANES)
    blk_expert = blk_owner[:, 0]
    n_used = misc[0, :1]

    xs = _dispatch(seg4, misc, lp, h1, cap)
    ys = _experts(blk_expert, n_used, misc[3, :N_EXPERTS], blk_owner[:, 1], xs, w_gate_up, b_gate_up, w_down,
                  b_down)
    out = _combine(seg4, lp, w4, h1, row(ln2_g), row(ln2_b), ys)
    return out.reshape(B, L, D)


def kernel(x, mem, positions, ln_emb_g, ln_emb_b, w_in, pe_k_cmp, pe_v_cmp, w_kcmp1, w_kcmp2, w_vcmp1, w_vcmp2, s5_a_re, s5_a_im, s5_log_dt, s5_b_re, s5_b_im, s5_c_re, s5_c_im, s5_d, w_s5_glu, w_mem_kv, w_nsa_out, w_mem_out, w_o, ln1_g, ln1_b, w_router, b_router, w_gate_up, b_gate_up, w_down, b_down, ln2_g, ln2_b):
    assert w_in.shape[0] == DEPTH
    l = 0
    return _layer(x, mem, positions, ln_emb_g, ln_emb_b, w_in[l], pe_k_cmp[l], pe_v_cmp[l], w_kcmp1[l],
                  w_kcmp2[l], w_vcmp1[l], w_vcmp2[l], s5_a_re[l], s5_a_im[l], s5_log_dt[l], s5_b_re[l],
                  s5_b_im[l], s5_c_re[l], s5_c_im[l], s5_d[l], w_s5_glu[l], w_mem_kv[l], w_nsa_out[l],
                  w_mem_out[l], w_o[l], ln1_g[l], ln1_b[l], w_router[l], b_router[l], w_gate_up[l],
                  b_gate_up[l], w_down[l], b_down[l], ln2_g[l], ln2_b[l])
```

```python
import functools
import math

import jax
import jax.numpy as jnp
import numpy as np
from jax import lax
from jax.experimental import pallas as pl
from jax.experimental.pallas import tpu as pltpu

F32 = jnp.float32
BF16 = jnp.bfloat16
I32 = jnp.int32

D_MODEL = 1024
NSA_HEADS = 8
NSA_GROUPS = 2
NSA_HPG = NSA_HEADS // NSA_GROUPS
HEAD_DIM = 64
NSA_WIDTH = NSA_HEADS * HEAD_DIM
KV_WIDTH = NSA_GROUPS * HEAD_DIM
CMP_BLOCK = 32
CMP_STRIDE = 16
CMP_HIDDEN = 128
SEL_BLOCK = 64
N_SEL = 16
WINDOW = 512
Q_BLOCK = 256
WINDOW_Q = 128
ROPE_THETA = 500000.0
ROT_DIM = HEAD_DIM // 4
S5_WIDTH = 512
S5_GROUP_DIM = 16
S5_GROUPS = S5_WIDTH // S5_GROUP_DIM
S5_STATE = 64
MEM_HEADS = 4
MEM_HEAD_DIM = 128
MEM_WIDTH = MEM_HEADS * MEM_HEAD_DIM
N_BRANCH = 3
N_EXPERTS = 32
TOP_K = 4
D_FF = 1024
SWIGLU_LIMIT = 7.0
SWIGLU_ALPHA = 1.702
LN_EPS = 1e-5
DEPTH = 1
DEEPNORM_ALPHA = (2 * DEPTH) ** 0.25

LANES = 128
SUBLANES = 8
VMEM_LIMIT_BYTES = 56 * 1024 * 1024

TOKEN_TILE = 256
INPROJ_TOKEN_TILE = 512
MERGE_SORT_TILES = 2
SEL_KV_TILE = 512
S5_CHUNK = 512
S5_PITCH = S5_CHUNK + 4
MOE_ROWS = 512
NEG_BIG = -(2.0 ** 100)
Q_SCALE_LOG2 = HEAD_DIM ** -0.5 * math.log2(math.e)


def _cparams(*sem):
    return pltpu.CompilerParams(dimension_semantics=sem, vmem_limit_bytes=VMEM_LIMIT_BYTES)


def _dot(a, b):
    return jnp.dot(a, b, preferred_element_type=F32)


def _dot_nt(a, b):
    return lax.dot_general(a, b, (((1,), (1,)), ((), ())), preferred_element_type=F32)


def _layer_norm(x, g, b):
    mu = jnp.mean(x, axis=-1, keepdims=True)
    xc = x - mu
    var = jnp.mean(xc * xc, axis=-1, keepdims=True)
    return xc * lax.rsqrt(var + LN_EPS) * g + b


def _gelu_tanh(x):
    cdf = 0.5 * (1.0 + jnp.tanh(math.sqrt(2.0 / math.pi) * (x + 0.044715 * (x * x * x))))
    return x * cdf


def _masked_exp2(s, mask):
    s = jnp.where(mask, s, -jnp.inf)
    m = jnp.max(s, axis=-1, keepdims=True)
    m = jnp.where(m > -jnp.inf, m, 0.0)
    return jnp.exp2(s - m)


def _safe_recip(denom):
    return 1.0 / jnp.maximum(denom, jnp.finfo(F32).tiny)


def _split3(x):
    hi = x.astype(BF16)
    r1 = x - hi.astype(F32)
    mid = r1.astype(BF16)
    lo = (r1 - mid.astype(F32)).astype(BF16)
    return hi, mid, lo


def _full_spec(shape):
    nd = len(shape)
    return pl.BlockSpec(shape, lambda *_: (0,) * nd)


def _inproj_kernel(x_ref, g_ref, b_ref, cs_ref, spread_ref, unit_ref,
                   wq_ref, wk_ref, wv_ref, wg_ref, wu_ref, wqm_ref, wm_ref,
                   q_ref, kc_ref, vc_ref, ks_ref, vs_ref, kw_ref, vw_ref,
                   gate_ref, u_ref, qm_ref, gm_ref):
    h = _layer_norm(x_ref[0], g_ref[...], b_ref[...])
    hb = h.astype(BF16)
    tab = sum(_dot(part, spread_ref[...]) for part in _split3(cs_ref[0]))
    cos_t = tab[:, 0:LANES] + unit_ref[...]
    sin_a = tab[:, LANES:2 * LANES]
    sin_b = tab[:, 2 * LANES:3 * LANES]

    def rope(t):
        return (t * cos_t + pltpu.roll(t, ROT_DIM // 2, 1) * sin_a
                + pltpu.roll(t, LANES - ROT_DIM // 2, 1) * sin_b)

    q = _dot(hb, wq_ref[...])
    for c in range(NSA_WIDTH // LANES):
        qc = rope(q[:, c * LANES:(c + 1) * LANES]) * Q_SCALE_LOG2
        for hh in range(2):
            q_ref[0, 2 * c + hh] = qc[:, hh * HEAD_DIM:(hh + 1) * HEAD_DIM].astype(BF16)
    k3 = _dot(hb, wk_ref[...])
    kc = rope(k3[:, 0:LANES])
    ks = rope(k3[:, LANES:2 * LANES])
    kw = rope(k3[:, 2 * LANES:3 * LANES])
    v3 = _dot(hb, wv_ref[...])
    kc_ref[0] = kc.astype(BF16)
    vc_ref[0] = v3[:, 0:LANES].astype(BF16)
    tm = x_ref.shape[1]
    pos = pl.program_id(1) * tm + lax.broadcasted_iota(I32, (tm, LANES), 0)
    blk_hot = jnp.where(lax.broadcasted_iota(I32, (tm, LANES), 1) == pos // SEL_BLOCK, 1.0, 0.0)
    lane_pad = jnp.zeros((tm, LANES - HEAD_DIM), F32)
    ones_pad = jnp.where(lax.broadcasted_iota(I32, (tm, LANES - HEAD_DIM), 1) == 0, 1.0, 0.0)
    for g in range(NSA_GROUPS):
        sl = slice(g * HEAD_DIM, (g + 1) * HEAD_DIM)
        ks_ref[0, g] = jnp.concatenate([blk_hot, ks[:, sl], lane_pad], axis=1).astype(BF16)
        kw_ref[0, g] = kw[:, sl].astype(BF16)
        vs_ref[0, g] = jnp.concatenate([v3[:, LANES:2 * LANES][:, sl], ones_pad], axis=1).astype(BF16)
        vw_ref[0, g] = jnp.concatenate([v3[:, 2 * LANES:3 * LANES][:, sl], ones_pad], axis=1).astype(BF16)
    gate_ref[0] = jax.nn.sigmoid(_dot(hb, wg_ref[...]))
    u_ref[0] = _dot(hb, wu_ref[...])
    qm_ref[0] = _dot(hb, wqm_ref[...]).astype(BF16)
    gm_ref[0] = jax.nn.sigmoid(_dot(hb, wm_ref[...]))


def _inproj(x, ln_g, ln_b, cos_sin, spread, unit, wq, wk, wv, wg, wu, wqm, wm):
    B, L, D = x.shape
    tm = INPROJ_TOKEN_TILE
    grid = (B, L // tm)
    tok = lambda w: pl.BlockSpec((1, tm, w), lambda b, i: (b, i, 0))
    head = lambda n, w=HEAD_DIM: pl.BlockSpec((1, n, tm, w), lambda b, i: (b, 0, i, 0))
    in_specs = [tok(D), _full_spec((1, D)), _full_spec((1, D)), tok(ROT_DIM), _full_spec(spread.shape),
                _full_spec(unit.shape)]
    in_specs += [pl.BlockSpec(w.shape, lambda b, i: (0, 0), pipeline_mode=pl.Buffered(1))
                 for w in (wq, wk, wv, wg, wu, wqm, wm)]
    sd = jax.ShapeDtypeStruct
    out_shape = [
        sd((B, NSA_HEADS, L, HEAD_DIM), BF16),
        sd((B, L, KV_WIDTH), BF16), sd((B, L, KV_WIDTH), BF16),
        sd((B, NSA_GROUPS, L, 2 * LANES), BF16), sd((B, NSA_GROUPS, L, LANES), BF16),
        sd((B, NSA_GROUPS, L, HEAD_DIM), BF16), sd((B, NSA_GROUPS, L, LANES), BF16),
        sd((B, L, NSA_GROUPS * LANES), F32),
        sd((B, L, S5_WIDTH), F32),
        sd((B, L, MEM_WIDTH), BF16),
        sd((B, L, N_BRANCH * D), F32),
    ]
    out_specs = [head(NSA_HEADS), tok(KV_WIDTH), tok(KV_WIDTH), head(NSA_GROUPS, 2 * LANES),
                 head(NSA_GROUPS, LANES), head(NSA_GROUPS), head(NSA_GROUPS, LANES),
                 tok(NSA_GROUPS * LANES), tok(S5_WIDTH),
                 tok(MEM_WIDTH), tok(N_BRANCH * D)]
    return pl.pallas_call(
        _inproj_kernel, grid=grid, in_specs=in_specs, out_specs=out_specs, out_shape=out_shape,
        compiler_params=_cparams("parallel", "parallel"), name="inproj",
    )(x, ln_g, ln_b, cos_sin, spread, unit, wq, wk, wv, wg, wu, wqm, wm)


def _compress_kernel(kc_ref, vc_ref, pek_ref, pev_ref, wk1f_ref, wv1f_ref,
                     wk1a_ref, wk1b_ref, wv1a_ref, wv1b_ref, wk2_ref, wv2_ref, ck_ref, cv_ref):
    n_chunk = kc_ref.shape[1]
    row = lax.broadcasted_iota(I32, (n_chunk, 1), 0)

    def one(x_ref, pe_ref, w1f_ref, w1a_ref, w1b_ref, w2_ref, o_ref):
        x = x_ref[0]
        first = _dot(x, w1a_ref[...])
        second = _dot(x, w1b_ref[...])
        second = pltpu.roll(second, n_chunk - 1, 0)
        pe_term = _dot(pe_ref[...], w1f_ref[...])[0:1]
        pe_term = jnp.concatenate([pe_term] * NSA_GROUPS, axis=1)
        hid = _gelu_tanh(first + second + pe_term).astype(BF16)
        for g in range(NSA_GROUPS):
            o = _dot(hid[:, g * CMP_HIDDEN:(g + 1) * CMP_HIDDEN], w2_ref[...])
            o_ref[0, g] = jnp.where(row < n_chunk - 1, o, 0.0).astype(BF16)

    one(kc_ref, pek_ref, wk1f_ref, wk1a_ref, wk1b_ref, wk2_ref, ck_ref)
    one(vc_ref, pev_ref, wv1f_ref, wv1a_ref, wv1b_ref, wv2_ref, cv_ref)


def _compress(kc_r, vc_r, pek, pev, wk1f, wv1f, wk1a, wk1b, wv1a, wv1b, wk2, wv2):
    B, n_chunk, width = kc_r.shape
    blk = pl.BlockSpec((1, n_chunk, width), lambda b: (b, 0, 0))
    out = pl.BlockSpec((1, NSA_GROUPS, n_chunk, HEAD_DIM), lambda b: (b, 0, 0, 0))
    ws = [pek, pev, wk1f, wv1f, wk1a, wk1b, wv1a, wv1b, wk2, wv2]
    sd = jax.ShapeDtypeStruct((B, NSA_GROUPS, n_chunk, HEAD_DIM), BF16)
    return pl.pallas_call(
        _compress_kernel, grid=(B,), in_specs=[blk, blk] + [_full_spec(w.shape) for w in ws],
        out_specs=[out, out], out_shape=[sd, sd], compiler_params=_cparams("parallel"), name="compress",
    )(kc_r, vc_r, *ws)


def _nsa_kernel(q_ref, ck_ref, cv_ref, ks_ref, vs_ref, kw_ref, vw_ref, gate_ref, wsc_ref, o_ref):
    seq_len = ks_ref.shape[2]
    n_cmp = ck_ref.shape[2]
    n_sb = seq_len // SEL_BLOCK
    n_sel = min(N_SEL, n_sb)
    rows = NSA_HPG * Q_BLOCK
    groups = range(NSA_GROUPS)
    q0 = pl.program_id(1) * Q_BLOCK
    t1 = q0 + lax.broadcasted_iota(I32, (Q_BLOCK, 1), 0)
    t4 = jnp.concatenate([t1] * NSA_HPG, axis=0)
    tk = SEL_KV_TILE

    def front(g):
        q = q_ref[0, g * NSA_HPG:(g + 1) * NSA_HPG].reshape(rows, HEAD_DIM)

        s = _dot_nt(q, ck_ref[0, g])
        c_end = lax.broadcasted_iota(I32, (1, n_cmp), 1) * CMP_STRIDE + (CMP_BLOCK - 1)
        e = _masked_exp2(s, c_end <= t4)
        p_cmp = e * _safe_recip(jnp.sum(e, axis=-1, keepdims=True))
        o_cmp = _dot(p_cmp.astype(BF16), cv_ref[0, g])

        imp = p_cmp[0:Q_BLOCK]
        for hh in range(1, NSA_HPG):
            imp = imp + p_cmp[hh * Q_BLOCK:(hh + 1) * Q_BLOCK]
        w_sc = wsc_ref[...]
        score = sum(_dot(part, w_sc) for part in _split3(imp))
        score_t = score.T
        jb = lax.broadcasted_iota(I32, (n_sb, Q_BLOCK), 0)
        tb = (q0 + lax.broadcasted_iota(I32, (1, Q_BLOCK), 1)) // SEL_BLOCK
        forced = (jb == 0) | (jb == tb) | (jb == tb - 1)
        work = jnp.where(forced | (jb > tb), -jnp.inf, score_t)
        bias_t = jnp.where(forced, 0.0, NEG_BIG)
        jbf = jb.astype(F32)
        for _ in range(n_sel - 3):
            m = jnp.max(work, axis=0, keepdims=True)
            idx = jnp.min(jnp.where(work == m, jbf, float(n_sb)), axis=0, keepdims=True)
            pick = jbf == idx
            bias_t = jnp.where(pick, 0.0, bias_t)
            work = jnp.where(pick, -jnp.inf, work)
        sel_bias = bias_t.T
        if n_sb < LANES:
            sel_bias = jnp.concatenate([sel_bias, jnp.zeros((Q_BLOCK, LANES - n_sb), F32)], axis=1)

        span = WINDOW + WINDOW_Q
        parts = []
        for sub in range(Q_BLOCK // WINDOW_Q):
            pick = lambda a: jnp.concatenate(
                [a[hh * Q_BLOCK + sub * WINDOW_Q:hh * Q_BLOCK + (sub + 1) * WINDOW_Q] for hh in range(NSA_HPG)],
                axis=0)
            w0 = pl.multiple_of(jnp.maximum(q0 + sub * WINDOW_Q - WINDOW, 0), WINDOW_Q)
            s = _dot_nt(pick(q), kw_ref[0, g, pl.ds(w0, span), :])
            diff = pick(t4) - (w0 + lax.broadcasted_iota(I32, (1, span), 1))
            e = _masked_exp2(s, (diff >= 0) & (diff < WINDOW))
            o = _dot(e.astype(BF16), vw_ref[0, g, pl.ds(w0, span), :])
            parts.append(o[:, :HEAD_DIM] * _safe_recip(o[:, HEAD_DIM:HEAD_DIM + 1]))
        o_win = jnp.concatenate([parts[sub][hh * WINDOW_Q:(hh + 1) * WINDOW_Q]
                                 for hh in range(NSA_HPG) for sub in range(len(parts))], axis=0)

        q_aug = jnp.concatenate([jnp.concatenate([sel_bias.astype(BF16)] * NSA_HPG, axis=0), q,
                                 jnp.zeros((rows, LANES - HEAD_DIM), BF16)], axis=1)
        return q_aug, o_cmp, o_win

    fronts = [front(g) for g in groups]

    def sel_tile(g, j, carry, causal):
        m_run, acc = carry
        k0 = pl.multiple_of(j * tk, tk)
        sc = _dot_nt(fronts[g][0], ks_ref[0, g, pl.ds(k0, tk), :])
        if causal:
            kpos = k0 + lax.broadcasted_iota(I32, (1, tk), 1)
            sc = jnp.where(kpos <= t4, sc, NEG_BIG)
        m_new = jnp.maximum(m_run, jnp.max(sc, axis=-1, keepdims=True))
        p = jnp.exp2(sc - m_new)
        acc_new = jnp.exp2(m_run - m_new) * acc + _dot(p.astype(BF16), vs_ref[0, g, pl.ds(k0, tk), :])
        return m_new, acc_new

    def sel_pair(jj, carries, causal):
        return tuple(sel_tile(g, 2 * jj + 1, sel_tile(g, 2 * jj, carries[g], causal), causal) for g in groups)

    init = tuple((jnp.full((rows, 1), NEG_BIG, F32), jnp.zeros((rows, LANES), F32)) for _ in groups)
    last_pair = (q0 // tk) // 2
    carries = sel_pair(last_pair, init, True)
    carries = lax.fori_loop(
        0, last_pair // 2, lambda jq, cs: sel_pair(2 * jq + 1, sel_pair(2 * jq, cs, False), False), carries)
    carries = lax.fori_loop(0, last_pair % 2, lambda _, cs: sel_pair(last_pair - 1, cs, False), carries)

    outs = []
    for g in groups:
        _, o_cmp, o_win = fronts[g]
        acc = carries[g][1]
        o_sel = acc[:, :HEAD_DIM] * (1.0 / acc[:, HEAD_DIM:HEAD_DIM + 1])
        gt = gate_ref[0, :, g * LANES:(g + 1) * LANES]
        for hh in range(NSA_HPG):
            sl = slice(hh * Q_BLOCK, (hh + 1) * Q_BLOCK)
            c = hh * N_BRANCH
            outs.append(o_cmp[sl] * gt[:, c:c + 1] + o_sel[sl] * gt[:, c + 1:c + 2]
                        + o_win[sl] * gt[:, c + 2:c + 3])
    o_ref[0] = jnp.concatenate(outs, axis=1).astype(BF16)


def _nsa(q_hm, ck, cv, ks, vs, kw, vw, gates, w_score):
    B, _, L, _ = q_hm.shape
    assert L // SEL_BLOCK <= LANES and (L // SEL_KV_TILE) % 2 == 0 and L >= WINDOW + Q_BLOCK
    n_cmp = ck.shape[2]
    grid = (B, L // Q_BLOCK)
    qspec = pl.BlockSpec((1, NSA_HEADS, Q_BLOCK, HEAD_DIM), lambda b, i: (b, 0, i, 0))
    cspec = pl.BlockSpec((1, NSA_GROUPS, n_cmp, HEAD_DIM), lambda b, i: (b, 0, 0, 0))
    kvspec = lambda w: pl.BlockSpec((1, NSA_GROUPS, L, w), lambda b, i: (b, 0, 0, 0),
                                    pipeline_mode=pl.Buffered(1))
    gspec = pl.BlockSpec((1, Q_BLOCK, NSA_GROUPS * LANES), lambda b, i: (b, i, 0))
    ospec = pl.BlockSpec((1, Q_BLOCK, NSA_WIDTH), lambda b, i: (b, i, 0))
    return pl.pallas_call(
        _nsa_kernel, grid=grid,
        in_specs=[qspec, cspec, cspec, kvspec(2 * LANES), kvspec(LANES), kvspec(HEAD_DIM), kvspec(LANES), gspec,
                  _full_spec(w_score.shape)],
        out_specs=ospec, out_shape=jax.ShapeDtypeStruct((B, L, NSA_WIDTH), BF16),
        compiler_params=_cparams("parallel", "arbitrary"), name="nsa",
    )(q_hm, ck, cv, ks, vs, kw, vw, gates, w_score)


def _s5_kernel(u_ref, wb_ref, wc_ref, are_ref, aim_ref, d_ref, y_ref, sre_ref, sim_ref, carry_ref):
    n_b, chunk, _ = u_ref.shape
    n_tile = wb_ref.shape[0]
    in_per = n_tile // (S5_WIDTH // LANES)
    pitch = S5_PITCH

    @pl.when(pl.program_id(0) == 0)
    def _():
        carry_ref[...] = jnp.zeros_like(carry_ref)

    for b in range(n_b):
        for c in range(n_tile):
            i = c // in_per
            ub = u_ref[b, :, i * LANES:(i + 1) * LANES].astype(BF16)
            r = _dot(ub, wb_ref[c])
            sre_ref[b, c * pitch:c * pitch + chunk, :] = r[:, :LANES]
            sim_ref[b, c * pitch:c * pitch + chunk, :] = r[:, LANES:]

    a_re, a_im = are_ref[...], aim_ref[...]

    def step(t, carry):
        out = []
        for b in range(n_b):
            s_re, s_im = carry[2 * b], carry[2 * b + 1]
            rows = pl.ds(t, n_tile, stride=pitch)
            n_re = a_re * s_re - a_im * s_im + sre_ref[b, rows, :]
            n_im = a_re * s_im + a_im * s_re + sim_ref[b, rows, :]
            sre_ref[b, rows, :] = n_re
            sim_ref[b, rows, :] = n_im
            out += [n_re, n_im]
        return tuple(out)

    init = tuple(carry_ref[i] for i in range(2 * n_b))
    fin = lax.fori_loop(0, chunk, step, init, unroll=True)
    for i in range(2 * n_b):
        carry_ref[i] = fin[i]

    for b in range(n_b):
        for o in range(S5_WIDTH // LANES):
            acc = jnp.zeros((chunk, LANES), F32)
            for c in range(o * in_per, (o + 1) * in_per):
                rows = slice(c * pitch, c * pitch + chunk)
                state = jnp.concatenate([sre_ref[b, rows, :], sim_ref[b, rows, :]], axis=1).astype(BF16)
                acc = acc + _dot(state, wc_ref[c])
            lanes = slice(o * LANES, (o + 1) * LANES)
            y = acc + d_ref[:, lanes] * u_ref[b, :, lanes]
            y_ref[b, :, lanes] = _gelu_tanh(y).astype(BF16)


def _s5(u, wb, wc, a_re, a_im, d_skip):
    B, L, W = u.shape
    chunk = S5_CHUNK
    n_tile = wb.shape[0]
    blk = pl.BlockSpec((B, chunk, W), lambda i: (0, i, 0))
    slab = pltpu.VMEM((B, n_tile * S5_PITCH, LANES), F32)
    return pl.pallas_call(
        _s5_kernel, grid=(L // chunk,),
        in_specs=[blk] + [_full_spec(w.shape) for w in (wb, wc, a_re, a_im, d_skip)],
        out_specs=blk, out_shape=jax.ShapeDtypeStruct((B, L, W), BF16),
        scratch_shapes=[slab, slab, pltpu.VMEM((2 * B, n_tile, LANES), F32)],
        compiler_params=_cparams("arbitrary"), name="s5",
    )(u, wb, wc, a_re, a_im, d_skip)


def _memkv_kernel(mem_ref, w_ref, k_ref, v_ref):
    kv = _dot(mem_ref[0].astype(BF16), w_ref[...])
    k_ref[0] = kv[:, :MEM_WIDTH].astype(BF16)
    v_ref[0] = kv[:, MEM_WIDTH:].astype(BF16)


def _memkv(mem, w_kv):
    B, M, D = mem.shape
    out = pl.BlockSpec((1, M, MEM_WIDTH), lambda b: (b, 0, 0))
    sd = jax.ShapeDtypeStruct((B, M, MEM_WIDTH), BF16)
    return pl.pallas_call(
        _memkv_kernel, grid=(B,),
        in_specs=[pl.BlockSpec((1, M, D), lambda b: (b, 0, 0)), _full_spec(w_kv.shape)],
        out_specs=[out, out], out_shape=[sd, sd], compiler_params=_cparams("parallel"), name="memkv",
    )(mem, w_kv)


def _memory_attention(q_ref, k_ref, v_ref):
    outs = []
    for h in range(MEM_HEADS):
        sl = slice(h * MEM_HEAD_DIM, (h + 1) * MEM_HEAD_DIM)
        s = _dot_nt(q_ref[:, sl], k_ref[0, :, sl]) * (MEM_HEAD_DIM ** -0.5)
        m = jnp.max(s, axis=-1, keepdims=True)
        e = jnp.exp(s - m)
        p = e / jnp.sum(e, axis=-1, keepdims=True)
        outs.append(_dot(p.astype(BF16), v_ref[0, :, sl]))
    return jnp.concatenate(outs, axis=1).astype(BF16)


def _merge_kernel(x_ref, lng_ref, lnb_ref, on_ref, gy_ref, qm_ref, km_ref, vm_ref, gm_ref,
                  wn_ref, wglu_ref, wmo_ref, wo_ref, l1g_ref, l1b_ref,
                  wrh_ref, wrp_ref, br_ref, tri_ref, striu_ref,
                  h1_ref, lp_ref, w4_ref, cnt_ref):
    D = x_ref.shape[1]
    tm = x_ref.shape[0]
    h =_layer_norm(x_ref[...], lng_ref[...], lnb_ref[...])
    y_nsa = _dot(on_ref[...], wn_ref[...])
    glu = _dot(gy_ref[...], wglu_ref[...])
    y_s5 = glu[:, :D] * jax.nn.sigmoid(glu[:, D:])
    y_mem = _dot(_memory_attention(qm_ref, km_ref, vm_ref), wmo_ref[...])
    merged = gm_ref[:, 0:D] * y_nsa + gm_ref[:, D:2 * D] * y_s5 + gm_ref[:, 2 * D:3 * D] * y_mem
    mix = _dot(merged.astype(BF16), wo_ref[...])
    h1 = _layer_norm(DEEPNORM_ALPHA * h + mix, l1g_ref[...], l1b_ref[...])
    h1_ref[...] = h1

    hh = h1.astype(BF16)
    hl = (h1 - hh.astype(F32)).astype(BF16)
    both = _dot(hh, wrp_ref[...])
    logits = both[:, :LANES] + both[:, LANES:] + _dot(hl, wrh_ref[...]) + br_ref[...]
    lane = lax.broadcasted_iota(I32, (tm, LANES), 1)
    lane_f = lane.astype(F32)
    work = logits
    multi = jnp.zeros((tm, LANES), F32)
    vals, picks = [], []
    for _ in range(TOP_K):
        m = jnp.max(work, axis=-1, keepdims=True)
        idx = jnp.min(jnp.where(work == m, lane_f, float(LANES)), axis=-1, keepdims=True)
        pick = lane_f == idx
        vals.append(m)
        picks.append((pick, idx))
        multi = jnp.where(pick, 1.0, multi)
        work = jnp.where(pick, -jnp.inf, work)
    es = [jnp.exp(v - vals[0]) for v in vals]
    den = es[0] + es[1] + es[2] + es[3]
    st = TOKEN_TILE
    pos = []
    for t in range(tm // st):
        multi_t = multi[t * st:(t + 1) * st]
        cnt = jnp.broadcast_to(jnp.sum(multi_t, axis=0, keepdims=True), (SUBLANES, LANES))
        cnt_ref[t] = cnt
        lower = _dot(cnt.astype(BF16), striu_ref[...])[0:1]
        pos.append(lower + _dot(tri_ref[...], multi_t.astype(BF16)))
    pos = jnp.concatenate(pos, axis=0)
    lp = jnp.full((tm, LANES), -1.0, F32)
    w4 = jnp.zeros((tm, LANES), F32)
    for k in range(TOP_K):
        pick, _ = picks[k]
        lp = jnp.where(lane == k, jnp.sum(jnp.where(pick, pos, 0.0), axis=-1, keepdims=True), lp)
        w4 = jnp.where(lane == k, es[k] / den, w4)
    lp_ref[...] = lp
    w4_ref[...] = w4


def _merge(x2, lng, lnb, o_nsa, gy, qm, k_mem, v_mem, gm, wn, wglu, wmo, wo, l1g, l1b, wrh, wrp, br, tri, striu):
    T, D = x2.shape
    tm = MERGE_SORT_TILES * TOKEN_TILE
    tok = lambda w: pl.BlockSpec((tm, w), lambda i: (i, 0))
    steps_per_batch = T // k_mem.shape[0] // tm
    mem_kv = pl.BlockSpec((1,) + k_mem.shape[1:], lambda i: (i // steps_per_batch, 0, 0))
    ws = [wn, wglu, wmo, wo, l1g, l1b, wrh, wrp, br, tri, striu]
    sd = jax.ShapeDtypeStruct
    lane_out = sd((T, LANES), F32)
    return pl.pallas_call(
        _merge_kernel, grid=(T // tm,),
        in_specs=[tok(D), _full_spec((1, D)), _full_spec((1, D)), tok(NSA_WIDTH), tok(S5_WIDTH),
                  tok(MEM_WIDTH), mem_kv, mem_kv, tok(N_BRANCH * D)] + [_full_spec(w.shape) for w in ws],
        out_specs=[tok(D), tok(LANES), tok(LANES),
                   pl.BlockSpec((MERGE_SORT_TILES, SUBLANES, LANES), lambda i: (i, 0, 0))],
        out_shape=[sd((T, D), F32), lane_out, lane_out, sd((T // TOKEN_TILE, SUBLANES, LANES), F32)],
        compiler_params=_cparams("parallel"), name="merge",
    )(x2, lng, lnb, o_nsa, gy, qm, k_mem, v_mem, gm, *ws)


def _slots_kernel(cnt_ref, triu_ref, striu_ref, tril_ref, seg_ref, blk_ref, misc_ref):
    n_blk = blk_ref.shape[0]
    cnt = cnt_ref[...]
    cnt_b = cnt.astype(BF16)
    total = jnp.sum(cnt, axis=0, keepdims=True)
    nblk_e = jnp.floor((total + (MOE_ROWS - 1)) * (1.0 / MOE_ROWS))
    nblk_8 = jnp.broadcast_to(nblk_e, (SUBLANES, LANES))
    end_b = _dot(nblk_8.astype(BF16), triu_ref[...])
    start_rows = (end_b - nblk_8)[0:1] * MOE_ROWS
    dst = start_rows + _dot(tril_ref[...], cnt_b)
    off = _dot(cnt_b, striu_ref[...])
    seg_ref[0] = cnt.astype(I32)
    seg_ref[1] = off.astype(I32)
    seg_ref[2] = dst.astype(I32)
    blk_i = lax.broadcasted_iota(I32, (n_blk, LANES), 0).astype(F32)
    lane_b = lax.broadcasted_iota(I32, (n_blk, LANES), 1)
    ended = jnp.where((end_b[0:1] <= blk_i) & (lane_b < N_EXPERTS), 1.0, 0.0)
    owner = jnp.minimum(jnp.sum(ended, axis=-1, keepdims=True), float(N_EXPERTS - 1))
    mine = lane_b.astype(F32) == owner
    pick = lambda per_expert: jnp.sum(jnp.where(mine, per_expert, 0.0), axis=-1, keepdims=True)
    earlier = blk_i[:, 0:1] - (pick(end_b[0:1]) - pick(nblk_e))
    held = jnp.clip(pick(total) - earlier * MOE_ROWS, 0.0, float(MOE_ROWS))
    blk_ref[...] = jnp.where(lane_b == 1, held, owner).astype(I32)
    cand = lax.broadcasted_iota(I32, (LANES, LANES), 0)
    has_blocks = jnp.broadcast_to(nblk_e, (LANES, LANES)).T > 0.0
    later = (cand > lax.broadcasted_iota(I32, (LANES, LANES), 1)) & has_blocks
    nxt = jnp.min(jnp.where(later, cand.astype(F32), float(LANES)), axis=0, keepdims=True)
    nxt = jnp.where(nxt < float(LANES), nxt, -1.0)
    lane8 = lax.broadcasted_iota(I32, (SUBLANES, LANES), 1)
    row8 = lax.broadcasted_iota(I32, (SUBLANES, LANES), 0)
    used = jnp.sum(jnp.where(lane8 == N_EXPERTS - 1, end_b, 0.0), axis=-1, keepdims=True)
    misc = jnp.where(row8 == 0, used,
                     jnp.where(row8 == 1, start_rows + total,
                               jnp.where(row8 == 2, nblk_e * MOE_ROWS - total, nxt)))
    misc_ref[...] = misc.astype(I32)


def _slots(cnt, triu, striu, tril, n_blk):
    n_tile = cnt.shape[0]
    sd = jax.ShapeDtypeStruct
    return pl.pallas_call(
        _slots_kernel, grid=(1,),
        in_specs=[_full_spec(cnt.shape), _full_spec(triu.shape), _full_spec(striu.shape), _full_spec(tril.shape)],
        out_specs=[_full_spec((3, n_tile, LANES)), _full_spec((n_blk, LANES)), _full_spec((SUBLANES, LANES))],
        out_shape=[sd((3, n_tile, LANES), I32), sd((n_blk, LANES), I32), sd((SUBLANES, LANES), I32)],
        compiler_params=_cparams("arbitrary"), name="slots",
    )(cnt, triu, striu, tril)


ROW_TILES = D_MODEL // LANES


def _row_span(row, n_rows):
    start = row * ROW_TILES
    if not isinstance(start, int):
        start = pl.multiple_of(start, ROW_TILES)
    return pl.ds(start, n_rows * ROW_TILES)


def _store_rows(ref, val):
    for c in range(ROW_TILES):
        ref[pl.ds(c, val.shape[0], stride=ROW_TILES), :] = val[:, c * LANES:(c + 1) * LANES]


def _load_row_tile(ref, n_rows, c):
    return ref[pl.ds(c, n_rows, stride=ROW_TILES), :]


BIG_PIECE_ROWS = 64


def _pieces(count, max_rows, fn):
    def run(sizes):
        for p in sizes:
            def piece(p=p):
                fn(count & (-2 * p), p)
            pl.when((count & p) != 0)(piece)

    sizes = [max_rows >> s for s in range(max_rows.bit_length())]
    big = [p for p in sizes if p >= BIG_PIECE_ROWS]
    if big:
        pl.when(count >= BIG_PIECE_ROWS)(lambda: run(big))
    run([p for p in sizes if p < BIG_PIECE_ROWS])


def _start_segment_copies(seg_ref, max_rows, make_copy):
    def per_expert(e, c):
        cnt, off, dst = seg_ref[0, 0, 0, e], seg_ref[1, 0, 0, e], seg_ref[2, 0, 0, e]
        _pieces(cnt, max_rows, lambda first, rows: make_copy(off + first, dst + first, rows).start(
            priority=rows.bit_length() % 2))
        return c

    lax.fori_loop(0, N_EXPERTS, per_expert, 0)


def _dispatch_kernel(seg_ref, misc_ref, lp_ref, h_ref, xs_ref, sorted_ref, zero_ref, sem, pad_sem):
    i = pl.program_id(0)
    n = pl.num_programs(0)
    tm, D = h_ref.shape
    rows = TOP_K * tm
    slot = lax.rem(i, 2)

    def row_copy(slot_):
        def make(src_row, dst_row, n_rows):
            return pltpu.make_async_copy(sorted_ref.at[slot_, _row_span(src_row, n_rows)],
                                         xs_ref.at[_row_span(dst_row, n_rows)], sem.at[slot_])
        return make

    @pl.when(i == 0)
    def _():
        zero_ref[...] = jnp.zeros_like(zero_ref)
        for wait in (False, True):
            def per_expert(e, c, wait=wait):
                def one(first, n_rows):
                    cp = pltpu.make_async_copy(zero_ref.at[_row_span(0, n_rows)],
                                               xs_ref.at[_row_span(misc_ref[1, e] + first, n_rows)], pad_sem)
                    cp.wait() if wait else cp.start()
                _pieces(misc_ref[2, e], MOE_ROWS // 2, one)
                return c
            lax.fori_loop(0, N_EXPERTS, per_expert, 0)

            def per_spare_half_block(hb, c, wait=wait):
                cp = pltpu.make_async_copy(zero_ref, xs_ref.at[_row_span(hb * (MOE_ROWS // 2), MOE_ROWS // 2)],
                                           pad_sem)
                cp.wait() if wait else cp.start()
                return c
            lax.fori_loop(2 * misc_ref[0, 0], 2 * (xs_ref.shape[0] // (MOE_ROWS * ROW_TILES)),
                          per_spare_half_block, 0)

    lp_t = lp_ref[...].T
    s_ix = lax.broadcasted_iota(I32, (rows, 1), 0).astype(F32)
    hit = s_ix == lp_t[0:1, :]
    for k in range(1, TOP_K):
        hit = hit | (s_ix == lp_t[k:k + 1, :])
    perm = jnp.where(hit, 1.0, 0.0).astype(BF16)
    _store_rows(sorted_ref.at[slot], _dot(perm, h_ref[...].astype(BF16)))

    _start_segment_copies(seg_ref, tm, row_copy(slot))

    @pl.when(i > 0)
    def _():
        row_copy(1 - slot)(0, 0, rows).wait()

    @pl.when(i == n - 1)
    def _():
        row_copy(slot)(0, 0, rows).wait()


def _seg_spec(index_map):
    return pl.BlockSpec((3, 1, 1, LANES), index_map, memory_space=pltpu.SMEM)


def _dispatch(seg4, misc, lp, h1, cap):
    T, D = h1.shape
    assert D == ROW_TILES * LANES
    tm = TOKEN_TILE
    tok = lambda w: pl.BlockSpec((tm, w), lambda i: (i, 0))
    return pl.pallas_call(
        _dispatch_kernel, grid=(T // tm,),
        in_specs=[_seg_spec(lambda i: (0, i, 0, 0)), pl.BlockSpec(memory_space=pltpu.SMEM), tok(LANES), tok(D)],
        out_specs=pl.BlockSpec(memory_space=pl.ANY),
        out_shape=jax.ShapeDtypeStruct((cap * ROW_TILES, LANES), F32),
        scratch_shapes=[pltpu.VMEM((2, TOP_K * tm * ROW_TILES, LANES), F32),
                        pltpu.VMEM((MOE_ROWS // 2 * ROW_TILES, LANES), F32),
                        pltpu.SemaphoreType.DMA((2,)), pltpu.SemaphoreType.DMA(())],
        compiler_params=_cparams("arbitrary"), name="dispatch",
    )(seg4, misc, lp, h1)


def _expert_kernel(blk_ref, used_ref, next_ref, rows_ref, xs_ref, wgu_hbm, bgu_ref, wd_hbm, bd_ref, ys_ref,
                   wgu_f32, wd_f32, wgu_bf, wd_bf, sem, run_ref):
    i = pl.program_id(0)
    live = i < used_ref[0]
    expert = blk_ref[i]

    def weight_copies(e, slot):
        return (pltpu.make_async_copy(wgu_hbm.at[e], wgu_f32.at[slot], sem.at[0, slot]),
                pltpu.make_async_copy(wd_hbm.at[e], wd_f32.at[slot], sem.at[1, slot]))

    @pl.when(i == 0)
    def _():
        run_ref[0] = 0
        for cp in weight_copies(expert, 0):
            cp.start()

    @pl.when(live & ((i == 0) | (expert != blk_ref[jnp.maximum(i - 1, 0)])))
    def _():
        slot = lax.rem(run_ref[0], 2)
        for cp in weight_copies(expert, slot):
            cp.wait()
        wgu_bf[...] = wgu_f32[slot].astype(BF16)
        wd_bf[...] = wd_f32[slot].astype(BF16)
        nxt = next_ref[expert]

        @pl.when(nxt >= 0)
        def _():
            for cp in weight_copies(nxt, 1 - slot):
                cp.start()

        run_ref[0] = run_ref[0] + 1

    def expert_rows(n_rows):
        xb = jnp.concatenate([_load_row_tile(xs_ref, n_rows, c).astype(BF16) for c in range(ROW_TILES)], axis=1)
        gu = _dot(xb, wgu_bf[...]) + bgu_ref[0]
        g = jnp.minimum(gu[:, :D_FF], SWIGLU_LIMIT)
        lin = jnp.clip(gu[:, D_FF:], -SWIGLU_LIMIT, SWIGLU_LIMIT)
        act = g * jax.nn.sigmoid(SWIGLU_ALPHA * g) * (lin + 1.0)
        _store_rows(ys_ref, _dot(act.astype(BF16), wd_bf[...]) + bd_ref[0])

    half = MOE_ROWS // 2
    half_full = rows_ref[i] <= half

    @pl.when(live & jnp.logical_not(half_full))
    def _():
        expert_rows(MOE_ROWS)

    @pl.when(live & half_full)
    def _():
        expert_rows(half)
        ys_ref[half * ROW_TILES:, :] = jnp.zeros((half * ROW_TILES, LANES), F32)

    @pl.when(pl.program_id(0) >= used_ref[0])
    def _():
        ys_ref[...] = jnp.zeros_like(ys_ref)


def _experts(blk_expert, n_used, next_expert, blk_rows, xs, w_gate_up, b_gate_up, w_down, b_down):
    D = w_down.shape[2]
    n_blk = xs.shape[0] // (MOE_ROWS * ROW_TILES)
    E = w_gate_up.shape[0]
    live = lambda i, used: jnp.minimum(i, used[0] - 1)
    row = pl.BlockSpec((MOE_ROWS * ROW_TILES, LANES), lambda i, blk, used, nxt, held: (live(i, used), 0))
    by_e = lambda shape: pl.BlockSpec((1,) + shape, lambda i, blk, used, nxt, held: (blk[live(i, used)], 0, 0))
    in_hbm = pl.BlockSpec(memory_space=pl.ANY)
    grid_spec = pltpu.PrefetchScalarGridSpec(
        num_scalar_prefetch=4, grid=(n_blk,),
        in_specs=[row, in_hbm, by_e((1, 2 * D_FF)), in_hbm, by_e((1, D))],
        out_specs=pl.BlockSpec((MOE_ROWS * ROW_TILES, LANES), lambda i, blk, used, nxt, held: (i, 0)),
        scratch_shapes=[pltpu.VMEM((2, D, 2 * D_FF), F32), pltpu.VMEM((2, D_FF, D), F32),
                        pltpu.VMEM((D, 2 * D_FF), BF16), pltpu.VMEM((D_FF, D), BF16),
                        pltpu.SemaphoreType.DMA((2, 2)), pltpu.SMEM((1,), I32)])
    return pl.pallas_call(
        _expert_kernel, grid_spec=grid_spec, out_shape=jax.ShapeDtypeStruct(xs.shape, F32),
        compiler_params=_cparams("arbitrary"), name="experts",
    )(blk_expert, n_used, next_expert, blk_rows, xs, w_gate_up, b_gate_up.reshape(E, 1, 2 * D_FF), w_down,
      b_down.reshape(E, 1, D))


def _combine_kernel(seg_ref, segn_ref, lp_ref, w4_ref, h1_ref, g_ref, b_ref, ys_ref, o_ref, buf_ref, sem):
    i = pl.program_id(0)
    n = pl.num_programs(0)
    tm = h1_ref.shape[0]
    rows = TOP_K * tm
    slot = lax.rem(i, 2)

    def row_copy(slot_):
        def make(buf_row, ys_row, n_rows):
            return pltpu.make_async_copy(ys_ref.at[_row_span(ys_row, n_rows)],
                                         buf_ref.at[slot_, _row_span(buf_row, n_rows)], sem.at[slot_])
        return make

    @pl.when(i == 0)
    def _():
        _start_segment_copies(seg_ref, tm, row_copy(slot))

    @pl.when(i + 1 < n)
    def _():
        _start_segment_copies(segn_ref, tm, row_copy(1 - slot))

    row_copy(slot)(0, 0, rows).wait()

    lp = lp_ref[...]
    s_ix = lax.broadcasted_iota(I32, (1, rows), 1).astype(F32)
    wmat = jnp.zeros((tm, rows), F32)
    for k in range(TOP_K):
        wmat = jnp.where(s_ix == lp[:, k:k + 1], w4_ref[:, k:k + 1], wmat)
    w_hi = wmat.astype(BF16)
    w_lo = (wmat - w_hi.astype(F32)).astype(BF16)
    y = jnp.concatenate([_load_row_tile(buf_ref.at[slot], rows, c) for c in range(ROW_TILES)], axis=1)
    y_hi = y.astype(BF16)
    y_lo = (y - y_hi.astype(F32)).astype(BF16)
    acc = DEEPNORM_ALPHA * h1_ref[...] + (_dot(w_hi, y_hi) + _dot(w_hi, y_lo) + _dot(w_lo, y_hi))
    o_ref[...] = _layer_norm(acc, g_ref[...], b_ref[...])


def _combine(seg4, lp, w4, h1, ln_g, ln_b, ys):
    T, D = h1.shape
    tm = TOKEN_TILE
    n_tile = T // tm
    return pl.pallas_call(
        _combine_kernel, grid=(n_tile,),
        in_specs=[_seg_spec(lambda i: (0, i, 0, 0)),
                  _seg_spec(lambda i: (0, jnp.minimum(i + 1, n_tile - 1), 0, 0)),
                  pl.BlockSpec((tm, LANES), lambda i: (i, 0)),
                  pl.BlockSpec((tm, LANES), lambda i: (i, 0)),
                  pl.BlockSpec((tm, D), lambda i: (i, 0)),
                  _full_spec((1, D)), _full_spec((1, D)),
                  pl.BlockSpec(memory_space=pl.ANY)],
        out_specs=pl.BlockSpec((tm, D), lambda i: (i, 0)),
        out_shape=jax.ShapeDtypeStruct((T, D), F32),
        scratch_shapes=[pltpu.VMEM((2, TOP_K * tm * ROW_TILES, LANES), F32), pltpu.SemaphoreType.DMA((2,))],
        compiler_params=_cparams("arbitrary"), name="combine",
    )(seg4, seg4, lp, w4, h1, ln_g, ln_b, ys)


def _rope_tables(positions):
    inv = ROPE_THETA ** (-jnp.arange(0, ROT_DIM, 2, dtype=F32) / ROT_DIM)
    ang = positions.astype(F32)[..., None] * inv
    cos_sin = jnp.concatenate([jnp.cos(ang), jnp.sin(ang)], axis=-1)
    half = ROT_DIM // 2
    spread = np.zeros((ROT_DIM, 3 * LANES), np.float32)
    unit = np.ones((1, LANES), np.float32)
    for lane in range(LANES):
        d = lane % HEAD_DIM
        if d < half:
            spread[d, lane] = 1.0
            spread[half + d, 2 * LANES + lane] = -1.0
            unit[0, lane] = 0.0
        elif d < ROT_DIM:
            spread[d - half, lane] = 1.0
            spread[d, LANES + lane] = 1.0
            unit[0, lane] = 0.0
    return cos_sin, jnp.asarray(spread, BF16), jnp.asarray(unit)


def _split_w_in(w_in):
    widths = (NSA_WIDTH,) + (KV_WIDTH,) * 6 + (NSA_HEADS * N_BRANCH, S5_WIDTH, MEM_WIDTH, N_BRANCH * D_MODEL)
    offs = [0]
    for w in widths:
        offs.append(offs[-1] + w)
    col = lambda i: w_in[:, offs[i]:offs[i + 1]]
    wq, kc, vc, ks, vs, kw, vw, wg, wu, wqm, wm = (col(i) for i in range(11))
    wk = jnp.concatenate([kc, ks, kw], axis=1)
    wv = jnp.concatenate([vc, vs, vw], axis=1)
    per_group = NSA_HPG * N_BRANCH
    wg_pad = jnp.zeros((w_in.shape[0], NSA_GROUPS * LANES), w_in.dtype)
    for g in range(NSA_GROUPS):
        wg_pad = wg_pad.at[:, g * LANES:g * LANES + per_group].set(wg[:, g * per_group:(g + 1) * per_group])
    return tuple(w.astype(BF16) for w in (wq, wk, wv, wg_pad, wu, wqm, wm))


def _compress_weights(w1):
    half = CMP_BLOCK // 2
    eye = np.eye(NSA_GROUPS, dtype=np.float32)

    def arrange(w_half):
        full = jnp.einsum('sdf,gh->sgdhf', w_half, eye)
        return full.reshape(half * NSA_GROUPS * HEAD_DIM, NSA_GROUPS * CMP_HIDDEN).astype(BF16)

    return (w1.reshape(CMP_BLOCK * HEAD_DIM, CMP_HIDDEN).astype(BF16), arrange(w1[:half]), arrange(w1[half:]))


def _s5_weights(a_re, a_im, log_dt, b_re, b_im, c_re, c_im):
    step = jnp.exp(log_dt)[:, None]
    mag = jnp.exp(a_re * step)
    ab_re, ab_im = mag * jnp.cos(a_im * step), mag * jnp.sin(a_im * step)
    den = a_re * a_re + a_im * a_im
    nr = ab_re - 1.0
    coef_re = (nr * a_re + ab_im * a_im) / den
    coef_im = (ab_im * a_re - nr * a_im) / den
    bb_re = coef_re[..., None] * b_re - coef_im[..., None] * b_im
    bb_im = coef_re[..., None] * b_im + coef_im[..., None] * b_re
    n_tile = S5_GROUPS * S5_STATE // LANES
    tile_groups = LANES // S5_STATE
    lane_groups = LANES // S5_GROUP_DIM
    tiles_per_lane_tile = lane_groups // tile_groups
    place = np.zeros((n_tile, lane_groups, tile_groups), np.float32)
    for c in range(n_tile):
        for j in range(tile_groups):
            place[c, (c % tiles_per_lane_tile) * tile_groups + j, j] = 1.0

    def in_blocks(bb):
        pairs = bb.reshape(n_tile, tile_groups, S5_STATE, S5_GROUP_DIM)
        return jnp.einsum('cjnp,caj->capjn', pairs, place).reshape(n_tile, LANES, LANES)

    def out_blocks(c):
        pairs = c.reshape(n_tile, tile_groups, S5_GROUP_DIM, S5_STATE)
        return jnp.einsum('cjpn,caj->cjnap', pairs, place).reshape(n_tile, LANES, LANES)

    wb = jnp.concatenate([in_blocks(bb_re), in_blocks(bb_im)], axis=2).astype(BF16)
    wc = jnp.concatenate([out_blocks(c_re), out_blocks(-c_im)], axis=1).astype(BF16)
    return wb, wc, ab_re.reshape(n_tile, LANES), ab_im.reshape(n_tile, LANES)


def _layer(x, mem, positions, ln_emb_g, ln_emb_b, w_in, pe_k, pe_v, w_kcmp1, w_kcmp2, w_vcmp1, w_vcmp2,
           s5_a_re, s5_a_im, s5_log_dt, s5_b_re, s5_b_im, s5_c_re, s5_c_im, s5_d,
           w_s5_glu, w_mem_kv, w_nsa_out, w_mem_out, w_o, ln1_g, ln1_b, w_router, b_router,
           w_gate_up, b_gate_up, w_down, b_down, ln2_g, ln2_b):
    B, L, D = x.shape
    T = B * L
    row = lambda v: v.reshape(1, -1)

    cos_sin, spread, unit = _rope_tables(positions)
    (q_hm, kc, vc, ks, vs, kw, vw, gates, u, qm, gm) = _inproj(
        x, row(ln_emb_g), row(ln_emb_b), cos_sin, spread, unit, *_split_w_in(w_in))

    n_chunk = L // CMP_STRIDE
    chunked = lambda t: t.reshape(B, n_chunk, CMP_STRIDE * KV_WIDTH)
    pe_rows = lambda pe: jnp.broadcast_to(pe.reshape(1, -1), (SUBLANES, CMP_BLOCK * HEAD_DIM)).astype(BF16)
    wk1f, wk1a, wk1b = _compress_weights(w_kcmp1)
    wv1f, wv1a, wv1b = _compress_weights(w_vcmp1)
    ck, cv = _compress(chunked(kc), chunked(vc), pe_rows(pe_k), pe_rows(pe_v), wk1f, wv1f,
                       wk1a, wk1b, wv1a, wv1b, w_kcmp2.astype(BF16), w_vcmp2.astype(BF16))

    per_sb = SEL_BLOCK // CMP_STRIDE
    c_ix = np.arange(n_chunk)[:, None]
    n_ix = np.arange(L // SEL_BLOCK)[None, :]
    w_score = jnp.asarray((c_ix // per_sb == n_ix).astype(np.float32)
                          + ((c_ix + 1) // per_sb == n_ix).astype(np.float32), BF16)
    o_nsa = _nsa(q_hm, ck, cv, ks, vs, kw, vw, gates, w_score)

    wb, wc, a_re, a_im = _s5_weights(s5_a_re, s5_a_im, s5_log_dt, s5_b_re, s5_b_im, s5_c_re, s5_c_im)
    gy = _s5(u, wb, wc, a_re, a_im, row(s5_d))

    k_mem, v_mem = _memkv(mem, w_mem_kv.astype(BF16))

    pad_e = LANES - N_EXPERTS
    wr = jnp.pad(w_router, ((0, 0), (0, pad_e)))
    wr_hi = wr.astype(BF16)
    wr_pair = jnp.concatenate([wr_hi, (wr - wr_hi.astype(F32)).astype(BF16)], axis=1)
    br = jnp.concatenate([b_router, jnp.full((pad_e,), -jnp.inf, F32)]).reshape(1, LANES)
    tm = TOKEN_TILE
    n_tile = T // tm
    strict_lower = lambda n: jnp.asarray(np.tril(np.ones((n, n), np.float32), -1), BF16)
    triu = jnp.asarray(np.triu(np.ones((LANES, LANES), np.float32)), BF16)
    striu = jnp.asarray(np.triu(np.ones((LANES, LANES), np.float32), 1), BF16)
    flat = lambda t: t.reshape(T, t.shape[-1])
    h1, lp, w4, cnt = _merge(
        flat(x), row(ln_emb_g), row(ln_emb_b), flat(o_nsa), flat(gy), flat(qm), k_mem, v_mem, flat(gm),
        w_nsa_out.astype(BF16), w_s5_glu.astype(BF16), w_mem_out.astype(BF16), w_o.astype(BF16),
        row(ln1_g), row(ln1_b), wr_hi, wr_pair, br, strict_lower(tm), striu)

    cap = (T * TOP_K + MOE_ROWS - 1) // MOE_ROWS * MOE_ROWS + N_EXPERTS * MOE_ROWS
    n_blk = cap // MOE_ROWS
    seg, blk_owner, misc = _slots(cnt[:, 0, :], triu, striu, strict_lower(n_tile), n_blk)
    seg4 = seg.reshape(3, n_tile, 1, LANES)
    blk_expert = blk_owner[:, 0]
    n_used = misc[0, :1]

    xs = _dispatch(seg4, misc, lp, h1, cap)
    ys = _experts(blk_expert, n_used, misc[3, :N_EXPERTS], blk_owner[:, 1], xs, w_gate_up, b_gate_up, w_down,
                  b_down)
    out = _combine(seg4, lp, w4, h1, row(ln2_g), row(ln2_b), ys)
    return out.reshape(B, L, D)


def kernel(x, mem, positions, ln_emb_g, ln_emb_b, w_in, pe_k_cmp, pe_v_cmp, w_kcmp1, w_kcmp2, w_vcmp1, w_vcmp2, s5_a_re, s5_a_im, s5_log_dt, s5_b_re, s5_b_im, s5_c_re, s5_c_im, s5_d, w_s5_glu, w_mem_kv, w_nsa_out, w_mem_out, w_o, ln1_g, ln1_b, w_router, b_router, w_gate_up, b_gate_up, w_down, b_down, ln2_g, ln2_b):
    assert w_in.shape[0] == DEPTH
    l = 0
    return _layer(x, mem, positions, ln_emb_g, ln_emb_b, w_in[l], pe_k_cmp[l], pe_v_cmp[l], w_kcmp1[l],
                  w_kcmp2[l], w_vcmp1[l], w_vcmp2[l], s5_a_re[l], s5_a_im[l], s5_log_dt[l], s5_b_re[l],
                  s5_b_im[l], s5_c_re[l], s5_c_im[l], s5_d[l], w_s5_glu[l], w_mem_kv[l], w_nsa_out[l],
                  w_mem_out[l], w_o[l], ln1_g[l], ln1_b[l], w_router[l], b_router[l], w_gate_up[l],
                  b_gate_up[l], w_down[l], b_down[l], ln2_g[l], ln2_b[l])
```
